```python
import jax, jax.numpy as jnp
from jax import lax
import numpy as np

D_MODEL = 2048
BATCH = 8
SEQ = 8192
DEPTH = 1

CHUNK = 64
Q_BLOCK = 128
EPS = 1e-6

CONV_WIDTH = 1024
CONV_K = 3

MLA_HEADS = 16
Q_LORA = 512
KV_LORA = 512
QK_NOPE = 128
QK_ROPE = 64
V_HEAD = 128
MLA_WIDTH = MLA_HEADS * V_HEAD
ROPE_THETA = 10000.0

MEM_TOKENS = 256
MEM_HEADS = 4
MEM_HEAD_DIM = 256
MEM_WIDTH = MEM_HEADS * MEM_HEAD_DIM

IN_SPLITS = (
    CONV_WIDTH, CONV_WIDTH, CONV_WIDTH, CONV_WIDTH,
    Q_LORA, KV_LORA, QK_ROPE, MLA_WIDTH,
    MEM_WIDTH, MEM_WIDTH,
    D_MODEL, D_MODEL, D_MODEL,
)
IN_WIDTH = sum(IN_SPLITS)

kernel_name = "hybrid_conv_mla_memory_block"


def rms_norm(x, g):
    xf = x.astype(jnp.float32)
    y = xf * lax.rsqrt(jnp.mean(xf * xf, axis=-1, keepdims=True) + EPS)
    return (y * g.astype(jnp.float32)).astype(x.dtype)


def apply_rope(x, cos, sin):
    x1, x2 = jnp.split(x.astype(jnp.float32), 2, axis=-1)
    return jnp.concatenate([x1 * cos - x2 * sin, x2 * cos + x1 * sin], axis=-1).astype(x.dtype)


def causal_depthwise_conv(u, w):
    k, c = w.shape
    return lax.conv_general_dilated(
        u, w[:, None, :].astype(u.dtype), window_strides=(1,), padding=[(k - 1, 0)],
        dimension_numbers=("NWC", "WIO", "NWC"), feature_group_count=c)


def mla_attention(c_q, c_kv, k_rope_raw, cos, sin, q_norm_g, w_uq, kv_norm_g, w_ukv,
                  qn_nope_g, qn_rope_g, kn_nope_g, kn_rope_g):
    b, s, _ = c_q.shape
    q = (rms_norm(c_q, q_norm_g) @ w_uq).reshape(b, s, MLA_HEADS, QK_NOPE + QK_ROPE)
    q_nope = rms_norm(q[..., :QK_NOPE], qn_nope_g)
    q_rope = apply_rope(rms_norm(q[..., QK_NOPE:], qn_rope_g), cos[:, :, None], sin[:, :, None])
    kv = (rms_norm(c_kv, kv_norm_g) @ w_ukv).reshape(b, s, MLA_HEADS, QK_NOPE + V_HEAD)
    k_nope = rms_norm(kv[..., :QK_NOPE], kn_nope_g)
    v = kv[..., QK_NOPE:]
    k_rope = apply_rope(rms_norm(k_rope_raw, kn_rope_g), cos, sin)
    scale = (QK_NOPE + QK_ROPE) ** -0.5
    n_blk = s // Q_BLOCK
    qn_blocks = q_nope.reshape(b, n_blk, Q_BLOCK, MLA_HEADS, QK_NOPE).transpose(1, 0, 2, 3, 4)
    qr_blocks = q_rope.reshape(b, n_blk, Q_BLOCK, MLA_HEADS, QK_ROPE).transpose(1, 0, 2, 3, 4)
    k_chunk = jnp.arange(s) // CHUNK

    def attend(args):
        qn, qr, blk = args
        sc = (jnp.einsum("bqhd,bkhd->bhqk", qn, k_nope, preferred_element_type=jnp.float32)
              + jnp.einsum("bqhr,bkr->bhqk", qr, k_rope, preferred_element_type=jnp.float32))
        q_chunk = (blk * Q_BLOCK + jnp.arange(Q_BLOCK)) // CHUNK
        allowed = k_chunk[None, :] <= q_chunk[:, None]
        p = jax.nn.softmax(jnp.where(allowed, sc * scale, -jnp.inf), axis=-1)
        return jnp.einsum("bhqk,bkhd->bqhd", p.astype(v.dtype), v)

    o = lax.map(attend, (qn_blocks, qr_blocks, jnp.arange(n_blk)))
    return o.transpose(1, 0, 2, 3, 4).reshape(b, s, MLA_WIDTH)


def memory_attention(q_raw, mem, mem_norm_g, w_mem_kv, qn_g, kn_g):
    b, s, _ = q_raw.shape
    m = mem.shape[1]
    q = rms_norm(q_raw.reshape(b, s, MEM_HEADS, MEM_HEAD_DIM), qn_g)
    k, v = jnp.split(rms_norm(mem, mem_norm_g) @ w_mem_kv, 2, axis=-1)
    k = rms_norm(k.reshape(b, m, MEM_HEADS, MEM_HEAD_DIM), kn_g)
    v = v.reshape(b, m, MEM_HEADS, MEM_HEAD_DIM)
    sc = jnp.einsum("bqhd,bmhd->bhqm", q, k, preferred_element_type=jnp.float32) * (MEM_HEAD_DIM ** -0.5)
    p = jax.nn.softmax(sc, axis=-1)
    return jnp.einsum("bhqm,bmhd->bqhd", p.astype(v.dtype), v).reshape(b, s, MEM_WIDTH)


def _fwd_setup_inputs(seed: int = 0) -> dict:
    key = jax.random.key(seed)
    ks = jax.random.split(key, 24)
    f32 = jnp.float32

    def w(k, shape, fan_in):
        return jax.random.normal(k, shape, f32) * (fan_in ** -0.5)

    def gain(k, shape):
        return 1.0 + 0.02 * jax.random.normal(k, shape, f32)

    x = jax.random.normal(ks[0], (BATCH, SEQ, D_MODEL), f32)
    offsets = jax.random.randint(ks[1], (BATCH, 1), 0, 64, dtype=jnp.int32) * CHUNK
    positions = (offsets + jnp.arange(SEQ, dtype=jnp.int32)[None, :]).astype(jnp.int32)
    mem = jax.random.normal(ks[2], (BATCH, MEM_TOKENS, D_MODEL), f32)
    L = DEPTH
    return {
        "x": x,
        "positions": positions,
        "mem": mem,
        "norm_g": gain(ks[3], (L, D_MODEL)),
        "w_in": w(ks[4], (L, D_MODEL, IN_WIDTH), D_MODEL),
        "conv_w": w(ks[5], (L, CONV_K, CONV_WIDTH), CONV_K),
        "w_conv_out": w(ks[6], (L, CONV_WIDTH, D_MODEL), CONV_WIDTH),
        "mla_q_norm_g": gain(ks[7], (L, Q_LORA)),
        "w_uq": w(ks[8], (L, Q_LORA, MLA_HEADS * (QK_NOPE + QK_ROPE)), Q_LORA),
        "mla_kv_norm_g": gain(ks[9], (L, KV_LORA)),
        "w_ukv": w(ks[10], (L, KV_LORA, MLA_HEADS * (QK_NOPE + V_HEAD)), KV_LORA),
        "mla_qn_nope_g": gain(ks[11], (L, QK_NOPE)),
        "mla_qn_rope_g": gain(ks[12], (L, QK_ROPE)),
        "mla_kn_nope_g": gain(ks[13], (L, QK_NOPE)),
        "mla_kn_rope_g": gain(ks[14], (L, QK_ROPE)),
        "w_mla_out": w(ks[15], (L, MLA_WIDTH, D_MODEL), MLA_WIDTH),
        "mem_norm_g": gain(ks[16], (L, D_MODEL)),
        "w_mem_kv": w(ks[17], (L, D_MODEL, 2 * MEM_WIDTH), D_MODEL),
        "mem_qn_g": gain(ks[18], (L, MEM_HEAD_DIM)),
        "mem_kn_g": gain(ks[19], (L, MEM_HEAD_DIM)),
        "w_mem_out": w(ks[20], (L, MEM_WIDTH, D_MODEL), MEM_WIDTH),
        "w_o": w(ks[21], (L, D_MODEL, D_MODEL), D_MODEL),
    }


def _fwd_reference(x, positions, mem, norm_g, w_in, conv_w, w_conv_out, mla_q_norm_g, w_uq,
              mla_kv_norm_g, w_ukv, mla_qn_nope_g, mla_qn_rope_g, mla_kn_nope_g, mla_kn_rope_g,
              w_mla_out, mem_norm_g, w_mem_kv, mem_qn_g, mem_kn_g, w_mem_out, w_o):
    half = QK_ROPE // 2
    inv_freq = jnp.power(ROPE_THETA, -jnp.arange(half, dtype=jnp.float32) / half)
    ang = positions.astype(jnp.float32)[..., None] * inv_freq
    cos, sin = jnp.cos(ang), jnp.sin(ang)
    split_at = np.cumsum(IN_SPLITS)[:-1].tolist()

    for l in range(DEPTH):
        h = rms_norm(x, norm_g[l])
        proj = h @ w_in[l]
        (c_gate, b_gate, u, conv_z, c_q, c_kv, k_rope_raw, mla_z,
         mem_q, mem_z, g_conv, g_mla, g_mem) = jnp.split(proj, split_at, axis=-1)

        conv_y = b_gate * causal_depthwise_conv(c_gate * u, conv_w[l])
        o_conv = (conv_y * jax.nn.silu(conv_z)) @ w_conv_out[l]

        mla_y = mla_attention(c_q, c_kv, k_rope_raw, cos, sin, mla_q_norm_g[l], w_uq[l],
                              mla_kv_norm_g[l], w_ukv[l], mla_qn_nope_g[l], mla_qn_rope_g[l],
                              mla_kn_nope_g[l], mla_kn_rope_g[l])
        o_mla = (mla_y * jax.nn.silu(mla_z)) @ w_mla_out[l]

        mem_y = memory_attention(mem_q, mem, mem_norm_g[l], w_mem_kv[l], mem_qn_g[l], mem_kn_g[l])
        o_mem = (mem_y * jax.nn.silu(mem_z)) @ w_mem_out[l]

        merged = (jax.nn.sigmoid(g_conv) * o_conv + jax.nn.sigmoid(g_mla) * o_mla
                  + jax.nn.sigmoid(g_mem) * o_mem)
        x = x + merged @ w_o[l]
    return x


import jax as _jax
import jax.numpy as _jnp

TWIN_FORMAT = 'train_step'
FWD_PARAMS = ['x', 'positions', 'mem', 'norm_g', 'w_in', 'conv_w', 'w_conv_out', 'mla_q_norm_g', 'w_uq', 'mla_kv_norm_g', 'w_ukv', 'mla_qn_nope_g', 'mla_qn_rope_g', 'mla_kn_nope_g', 'mla_kn_rope_g', 'w_mla_out', 'mem_norm_g', 'w_mem_kv', 'mem_qn_g', 'mem_kn_g', 'w_mem_out', 'w_o']
TWIN_WEIGHTS = ['norm_g', 'w_in', 'conv_w', 'w_conv_out', 'mla_q_norm_g', 'w_uq', 'mla_kv_norm_g', 'w_ukv', 'mla_qn_nope_g', 'mla_qn_rope_g', 'mla_kn_nope_g', 'mla_kn_rope_g', 'w_mla_out', 'mem_norm_g', 'w_mem_kv', 'mem_qn_g', 'mem_kn_g', 'w_mem_out', 'w_o']
TWIN_DIFF_INPUT = 'x'
TWIN_INPUTS = ['x', 'positions', 'mem', 'norm_g', 'w_in', 'conv_w', 'w_conv_out', 'mla_q_norm_g', 'w_uq', 'mla_kv_norm_g', 'w_ukv', 'mla_qn_nope_g', 'mla_qn_rope_g', 'mla_kn_nope_g', 'mla_kn_rope_g', 'w_mla_out', 'mem_norm_g', 'w_mem_kv', 'mem_qn_g', 'mem_kn_g', 'w_mem_out', 'w_o', 'loss_target', 'm_norm_g', 'm_w_in', 'm_conv_w', 'm_w_conv_out', 'm_mla_q_norm_g', 'm_w_uq', 'm_mla_kv_norm_g', 'm_w_ukv', 'm_mla_qn_nope_g', 'm_mla_qn_rope_g', 'm_mla_kn_nope_g', 'm_mla_kn_rope_g', 'm_w_mla_out', 'm_mem_norm_g', 'm_w_mem_kv', 'm_mem_qn_g', 'm_mem_kn_g', 'm_w_mem_out', 'm_w_o', 'v_norm_g', 'v_w_in', 'v_conv_w', 'v_w_conv_out', 'v_mla_q_norm_g', 'v_w_uq', 'v_mla_kv_norm_g', 'v_w_ukv', 'v_mla_qn_nope_g', 'v_mla_qn_rope_g', 'v_mla_kn_nope_g', 'v_mla_kn_rope_g', 'v_w_mla_out', 'v_mem_norm_g', 'v_w_mem_kv', 'v_mem_qn_g', 'v_mem_kn_g', 'v_w_mem_out', 'v_w_o']
TWIN_OUTPUTS = ['loss', 'grad_x', 'grad_norm_g', 'grad_w_in', 'grad_conv_w', 'grad_w_conv_out', 'grad_mla_q_norm_g', 'grad_w_uq', 'grad_mla_kv_norm_g', 'grad_w_ukv', 'grad_mla_qn_nope_g', 'grad_mla_qn_rope_g', 'grad_mla_kn_nope_g', 'grad_mla_kn_rope_g', 'grad_w_mla_out', 'grad_mem_norm_g', 'grad_w_mem_kv', 'grad_mem_qn_g', 'grad_mem_kn_g', 'grad_w_mem_out', 'grad_w_o', 'delta_norm_g', 'delta_w_in', 'delta_conv_w', 'delta_w_conv_out', 'delta_mla_q_norm_g', 'delta_w_uq', 'delta_mla_kv_norm_g', 'delta_w_ukv', 'delta_mla_qn_nope_g', 'delta_mla_qn_rope_g', 'delta_mla_kn_nope_g', 'delta_mla_kn_rope_g', 'delta_w_mla_out', 'delta_mem_norm_g', 'delta_w_mem_kv', 'delta_mem_qn_g', 'delta_mem_kn_g', 'delta_w_mem_out', 'delta_w_o', 'new_m_norm_g', 'new_m_w_in', 'new_m_conv_w', 'new_m_w_conv_out', 'new_m_mla_q_norm_g', 'new_m_w_uq', 'new_m_mla_kv_norm_g', 'new_m_w_ukv', 'new_m_mla_qn_nope_g', 'new_m_mla_qn_rope_g', 'new_m_mla_kn_nope_g', 'new_m_mla_kn_rope_g', 'new_m_w_mla_out', 'new_m_mem_norm_g', 'new_m_w_mem_kv', 'new_m_mem_qn_g', 'new_m_mem_kn_g', 'new_m_w_mem_out', 'new_m_w_o', 'new_v_norm_g', 'new_v_w_in', 'new_v_conv_w', 'new_v_w_conv_out', 'new_v_mla_q_norm_g', 'new_v_w_uq', 'new_v_mla_kv_norm_g', 'new_v_w_ukv', 'new_v_mla_qn_nope_g', 'new_v_mla_qn_rope_g', 'new_v_mla_kn_nope_g', 'new_v_mla_kn_rope_g', 'new_v_w_mla_out', 'new_v_mem_norm_g', 'new_v_w_mem_kv', 'new_v_mem_qn_g', 'new_v_mem_kn_g', 'new_v_w_mem_out', 'new_v_w_o']
TWIN_LEAF_KINDS = {'loss': 'loss', 'grad_x': 'grad_x', 'grad_norm_g': 'grad_w', 'grad_w_in': 'grad_w', 'grad_conv_w': 'grad_w', 'grad_w_conv_out': 'grad_w', 'grad_mla_q_norm_g': 'grad_w', 'grad_w_uq': 'grad_w', 'grad_mla_kv_norm_g': 'grad_w', 'grad_w_ukv': 'grad_w', 'grad_mla_qn_nope_g': 'grad_w', 'grad_mla_qn_rope_g': 'grad_w', 'grad_mla_kn_nope_g': 'grad_w', 'grad_mla_kn_rope_g': 'grad_w', 'grad_w_mla_out': 'grad_w', 'grad_mem_norm_g': 'grad_w', 'grad_w_mem_kv': 'grad_w', 'grad_mem_qn_g': 'grad_w', 'grad_mem_kn_g': 'grad_w', 'grad_w_mem_out': 'grad_w', 'grad_w_o': 'grad_w', 'delta_norm_g': 'delta_w', 'delta_w_in': 'delta_w', 'delta_conv_w': 'delta_w', 'delta_w_conv_out': 'delta_w', 'delta_mla_q_norm_g': 'delta_w', 'delta_w_uq': 'delta_w', 'delta_mla_kv_norm_g': 'delta_w', 'delta_w_ukv': 'delta_w', 'delta_mla_qn_nope_g': 'delta_w', 'delta_mla_qn_rope_g': 'delta_w', 'delta_mla_kn_nope_g': 'delta_w', 'delta_mla_kn_rope_g': 'delta_w', 'delta_w_mla_out': 'delta_w', 'delta_mem_norm_g': 'delta_w', 'delta_w_mem_kv': 'delta_w', 'delta_mem_qn_g': 'delta_w', 'delta_mem_kn_g': 'delta_w', 'delta_w_mem_out': 'delta_w', 'delta_w_o': 'delta_w', 'new_m_norm_g': 'new_m', 'new_m_w_in': 'new_m', 'new_m_conv_w': 'new_m', 'new_m_w_conv_out': 'new_m', 'new_m_mla_q_norm_g': 'new_m', 'new_m_w_uq': 'new_m', 'new_m_mla_kv_norm_g': 'new_m', 'new_m_w_ukv': 'new_m', 'new_m_mla_qn_nope_g': 'new_m', 'new_m_mla_qn_rope_g': 'new_m', 'new_m_mla_kn_nope_g': 'new_m', 'new_m_mla_kn_rope_g': 'new_m', 'new_m_w_mla_out': 'new_m', 'new_m_mem_norm_g': 'new_m', 'new_m_w_mem_kv': 'new_m', 'new_m_mem_qn_g': 'new_m', 'new_m_mem_kn_g': 'new_m', 'new_m_w_mem_out': 'new_m', 'new_m_w_o': 'new_m', 'new_v_norm_g': 'new_v', 'new_v_w_in': 'new_v', 'new_v_conv_w': 'new_v', 'new_v_w_conv_out': 'new_v', 'new_v_mla_q_norm_g': 'new_v', 'new_v_w_uq': 'new_v', 'new_v_mla_kv_norm_g': 'new_v', 'new_v_w_ukv': 'new_v', 'new_v_mla_qn_nope_g': 'new_v', 'new_v_mla_qn_rope_g': 'new_v', 'new_v_mla_kn_nope_g': 'new_v', 'new_v_mla_kn_rope_g': 'new_v', 'new_v_w_mla_out': 'new_v', 'new_v_mem_norm_g': 'new_v', 'new_v_w_mem_kv': 'new_v', 'new_v_mem_qn_g': 'new_v', 'new_v_mem_kn_g': 'new_v', 'new_v_w_mem_out': 'new_v', 'new_v_w_o': 'new_v'}


def _forward(args):
    return _fwd_reference(*[args[k] for k in FWD_PARAMS])


def _output_shape():
    def fwd():
        inp = _fwd_setup_inputs(0)
        return _fwd_reference(*[inp[k] for k in FWD_PARAMS])
    out = _jax.eval_shape(fwd)
    return out.shape, out.dtype

N_MICROBATCH = 1
ADAM_LR = 0.001
ADAM_B1 = 0.9
ADAM_B2 = 0.999
ADAM_EPS = 1e-08
ADAM_WD = 0.01
ADAM_STEP = 10
PER_EXAMPLE_BATCH_AXIS = {'x': 0, 'positions': 0, 'mem': 0, 'loss_target': 0}
SHARED_INPUTS = []
_WEIGHT_DTYPES = {'norm_g': _jnp.float32, 'w_in': _jnp.float32, 'conv_w': _jnp.float32, 'w_conv_out': _jnp.float32, 'mla_q_norm_g': _jnp.float32, 'w_uq': _jnp.float32, 'mla_kv_norm_g': _jnp.float32, 'w_ukv': _jnp.float32, 'mla_qn_nope_g': _jnp.float32, 'mla_qn_rope_g': _jnp.float32, 'mla_kn_nope_g': _jnp.float32, 'mla_kn_rope_g': _jnp.float32, 'w_mla_out': _jnp.float32, 'mem_norm_g': _jnp.float32, 'w_mem_kv': _jnp.float32, 'mem_qn_g': _jnp.float32, 'mem_kn_g': _jnp.float32, 'w_mem_out': _jnp.float32, 'w_o': _jnp.float32}
MOMENT_SCALE = {'norm_g': 1.368065e+01, 'w_in': 1.343455e-01, 'conv_w': 3.563711e+00, 'w_conv_out': 1.203680e-01, 'mla_q_norm_g': 1.760648e-02, 'w_uq': 7.128941e-03, 'mla_kv_norm_g': 5.842140e-02, 'w_ukv': 8.267772e-03, 'mla_qn_nope_g': 1.287437e-01, 'mla_qn_rope_g': 1.111912e-01, 'mla_kn_nope_g': 1.285032e-01, 'mla_kn_rope_g': 1.107816e-01, 'w_mla_out': 9.020600e-03, 'mem_norm_g': 2.771295e-02, 'w_mem_kv': 9.774532e-03, 'mem_qn_g': 2.544691e-01, 'mem_kn_g': 2.543063e-01, 'w_mem_out': 6.708720e-03, 'w_o': 9.599010e-02}


def _to_microbatches(a, axis):
    t = _jnp.moveaxis(a, axis, 0)
    t = t.reshape((N_MICROBATCH, t.shape[0] // N_MICROBATCH) + t.shape[1:])
    return _jnp.moveaxis(t, 1, axis + 1)


def setup_inputs(seed: int = 0) -> dict:
    inp = _fwd_setup_inputs(seed)
    key = _jax.random.fold_in(_jax.random.key(seed), 7919)
    shape, _ = _output_shape()
    out = dict(inp)
    out["loss_target"] = _jax.random.normal(_jax.random.fold_in(key, 0), shape, _jnp.float32)
    for i, name in enumerate(TWIN_WEIGHTS):
        w = inp[name].astype(_jnp.float32)
        if MOMENT_SCALE is None:
            s = _jnp.sqrt(_jnp.mean(_jnp.square(w)) + 1e-30)
        else:
            s = MOMENT_SCALE[name]
        km, kv = _jax.random.split(_jax.random.fold_in(key, i + 1))
        out[name] = w
        out["m_" + name] = s * _jax.random.normal(km, w.shape, _jnp.float32)
        out["v_" + name] = (s * s) * _jax.random.uniform(kv, w.shape, _jnp.float32, 0.5, 1.5)
    if N_MICROBATCH > 1:
        for name, axis in PER_EXAMPLE_BATCH_AXIS.items():
            out[name] = _to_microbatches(out[name], axis)
    return {'x': out['x'], 'positions': out['positions'], 'mem': out['mem'], 'norm_g': out['norm_g'], 'w_in': out['w_in'], 'conv_w': out['conv_w'], 'w_conv_out': out['w_conv_out'], 'mla_q_norm_g': out['mla_q_norm_g'], 'w_uq': out['w_uq'], 'mla_kv_norm_g': out['mla_kv_norm_g'], 'w_ukv': out['w_ukv'], 'mla_qn_nope_g': out['mla_qn_nope_g'], 'mla_qn_rope_g': out['mla_qn_rope_g'], 'mla_kn_nope_g': out['mla_kn_nope_g'], 'mla_kn_rope_g': out['mla_kn_rope_g'], 'w_mla_out': out['w_mla_out'], 'mem_norm_g': out['mem_norm_g'], 'w_mem_kv': out['w_mem_kv'], 'mem_qn_g': out['mem_qn_g'], 'mem_kn_g': out['mem_kn_g'], 'w_mem_out': out['w_mem_out'], 'w_o': out['w_o'], 'loss_target': out['loss_target'], 'm_norm_g': out['m_norm_g'], 'm_w_in': out['m_w_in'], 'm_conv_w': out['m_conv_w'], 'm_w_conv_out': out['m_w_conv_out'], 'm_mla_q_norm_g': out['m_mla_q_norm_g'], 'm_w_uq': out['m_w_uq'], 'm_mla_kv_norm_g': out['m_mla_kv_norm_g'], 'm_w_ukv': out['m_w_ukv'], 'm_mla_qn_nope_g': out['m_mla_qn_nope_g'], 'm_mla_qn_rope_g': out['m_mla_qn_rope_g'], 'm_mla_kn_nope_g': out['m_mla_kn_nope_g'], 'm_mla_kn_rope_g': out['m_mla_kn_rope_g'], 'm_w_mla_out': out['m_w_mla_out'], 'm_mem_norm_g': out['m_mem_norm_g'], 'm_w_mem_kv': out['m_w_mem_kv'], 'm_mem_qn_g': out['m_mem_qn_g'], 'm_mem_kn_g': out['m_mem_kn_g'], 'm_w_mem_out': out['m_w_mem_out'], 'm_w_o': out['m_w_o'], 'v_norm_g': out['v_norm_g'], 'v_w_in': out['v_w_in'], 'v_conv_w': out['v_conv_w'], 'v_w_conv_out': out['v_w_conv_out'], 'v_mla_q_norm_g': out['v_mla_q_norm_g'], 'v_w_uq': out['v_w_uq'], 'v_mla_kv_norm_g': out['v_mla_kv_norm_g'], 'v_w_ukv': out['v_w_ukv'], 'v_mla_qn_nope_g': out['v_mla_qn_nope_g'], 'v_mla_qn_rope_g': out['v_mla_qn_rope_g'], 'v_mla_kn_nope_g': out['v_mla_kn_nope_g'], 'v_mla_kn_rope_g': out['v_mla_kn_rope_g'], 'v_w_mla_out': out['v_w_mla_out'], 'v_mem_norm_g': out['v_mem_norm_g'], 'v_w_mem_kv': out['v_w_mem_kv'], 'v_mem_qn_g': out['v_mem_qn_g'], 'v_mem_kn_g': out['v_mem_kn_g'], 'v_w_mem_out': out['v_w_mem_out'], 'v_w_o': out['v_w_o']}


def _loss(weights, diff, rest, loss_target):
    with _jax.named_scope("forward"):
        args = {**rest, TWIN_DIFF_INPUT: diff, **{k: w.astype(_WEIGHT_DTYPES[k]) for k, w in weights.items()}}
        y = _forward(args)
    with _jax.named_scope("loss_head"):
        err = _jnp.square(y.astype(_jnp.float32) - loss_target)
        return 0.5 * _jnp.sum(_jnp.mean(err, axis=-1)) if err.ndim else 0.5 * err


def _adamw(w, g, m, v):
    m = ADAM_B1 * m + (1.0 - ADAM_B1) * g
    v = ADAM_B2 * v + (1.0 - ADAM_B2) * _jnp.square(g)
    m_hat = m / (1.0 - ADAM_B1 ** ADAM_STEP)
    v_hat = v / (1.0 - ADAM_B2 ** ADAM_STEP)
    delta = -ADAM_LR * (m_hat / (_jnp.sqrt(v_hat) + ADAM_EPS) + ADAM_WD * w)
    return delta, m, v


def reference(x, positions, mem, norm_g, w_in, conv_w, w_conv_out, mla_q_norm_g, w_uq, mla_kv_norm_g, w_ukv, mla_qn_nope_g, mla_qn_rope_g, mla_kn_nope_g, mla_kn_rope_g, w_mla_out, mem_norm_g, w_mem_kv, mem_qn_g, mem_kn_g, w_mem_out, w_o, loss_target, m_norm_g, m_w_in, m_conv_w, m_w_conv_out, m_mla_q_norm_g, m_w_uq, m_mla_kv_norm_g, m_w_ukv, m_mla_qn_nope_g, m_mla_qn_rope_g, m_mla_kn_nope_g, m_mla_kn_rope_g, m_w_mla_out, m_mem_norm_g, m_w_mem_kv, m_mem_qn_g, m_mem_kn_g, m_w_mem_out, m_w_o, v_norm_g, v_w_in, v_conv_w, v_w_conv_out, v_mla_q_norm_g, v_w_uq, v_mla_kv_norm_g, v_w_ukv, v_mla_qn_nope_g, v_mla_qn_rope_g, v_mla_kn_nope_g, v_mla_kn_rope_g, v_w_mla_out, v_mem_norm_g, v_w_mem_kv, v_mem_qn_g, v_mem_kn_g, v_w_mem_out, v_w_o):
    given = dict(x=x, positions=positions, mem=mem, norm_g=norm_g, w_in=w_in, conv_w=conv_w, w_conv_out=w_conv_out, mla_q_norm_g=mla_q_norm_g, w_uq=w_uq, mla_kv_norm_g=mla_kv_norm_g, w_ukv=w_ukv, mla_qn_nope_g=mla_qn_nope_g, mla_qn_rope_g=mla_qn_rope_g, mla_kn_nope_g=mla_kn_nope_g, mla_kn_rope_g=mla_kn_rope_g, w_mla_out=w_mla_out, mem_norm_g=mem_norm_g, w_mem_kv=w_mem_kv, mem_qn_g=mem_qn_g, mem_kn_g=mem_kn_g, w_mem_out=w_mem_out, w_o=w_o, loss_target=loss_target, m_norm_g=m_norm_g, m_w_in=m_w_in, m_conv_w=m_conv_w, m_w_conv_out=m_w_conv_out, m_mla_q_norm_g=m_mla_q_norm_g, m_w_uq=m_w_uq, m_mla_kv_norm_g=m_mla_kv_norm_g, m_w_ukv=m_w_ukv, m_mla_qn_nope_g=m_mla_qn_nope_g, m_mla_qn_rope_g=m_mla_qn_rope_g, m_mla_kn_nope_g=m_mla_kn_nope_g, m_mla_kn_rope_g=m_mla_kn_rope_g, m_w_mla_out=m_w_mla_out, m_mem_norm_g=m_mem_norm_g, m_w_mem_kv=m_w_mem_kv, m_mem_qn_g=m_mem_qn_g, m_mem_kn_g=m_mem_kn_g, m_w_mem_out=m_w_mem_out, m_w_o=m_w_o, v_norm_g=v_norm_g, v_w_in=v_w_in, v_conv_w=v_conv_w, v_w_conv_out=v_w_conv_out, v_mla_q_norm_g=v_mla_q_norm_g, v_w_uq=v_w_uq, v_mla_kv_norm_g=v_mla_kv_norm_g, v_w_ukv=v_w_ukv, v_mla_qn_nope_g=v_mla_qn_nope_g, v_mla_qn_rope_g=v_mla_qn_rope_g, v_mla_kn_nope_g=v_mla_kn_nope_g, v_mla_kn_rope_g=v_mla_kn_rope_g, v_w_mla_out=v_w_mla_out, v_mem_norm_g=v_mem_norm_g, v_w_mem_kv=v_w_mem_kv, v_mem_qn_g=v_mem_qn_g, v_mem_kn_g=v_mem_kn_g, v_w_mem_out=v_w_mem_out, v_w_o=v_w_o)
    weights = {n: given[n] for n in TWIN_WEIGHTS}
    shared = {n: given[n] for n in SHARED_INPUTS}
    per_example = {n: given[n] for n in ['x', 'positions', 'mem']}
    grad_fn = _jax.value_and_grad(_loss, argnums=(0, 1))

    def one_microbatch(ex, loss_target):
        ex = dict(ex)
        diff = ex.pop(TWIN_DIFF_INPUT)
        return grad_fn(weights, diff, {**shared, **ex}, loss_target)

    if N_MICROBATCH == 1:
        loss, (grad_w, grad_x) = one_microbatch(per_example, given["loss_target"])
    else:
        def body(carry, xs):
            loss_sum, grad_sum = carry
            l_k, (gw_k, gx_k) = one_microbatch(xs[0], xs[1])
            with _jax.named_scope("update"):
                return (loss_sum + l_k, _jax.tree.map(_jnp.add, grad_sum, gw_k)), gx_k

        init = (_jnp.zeros((), _jnp.float32), _jax.tree.map(_jnp.zeros_like, weights))
        (loss, grad_w), grad_x = _jax.lax.scan(body, init, (per_example, given["loss_target"]))
    with _jax.named_scope("update"):
        delta_w, new_m, new_v = {}, {}, {}
        for n in TWIN_WEIGHTS:
            delta_w[n], new_m[n], new_v[n] = _adamw(weights[n], grad_w[n], given["m_" + n], given["v_" + n])
    return (loss, grad_x, *[grad_w[n] for n in TWIN_WEIGHTS], *[delta_w[n] for n in TWIN_WEIGHTS],
            *[new_m[n] for n in TWIN_WEIGHTS], *[new_v[n] for n in TWIN_WEIGHTS])
```

```python
import functools

import numpy as np
import jax
import jax.numpy as jnp
from jax import lax
from jax.experimental import pallas as pl
from jax.experimental.pallas import tpu as pltpu

F32 = jnp.float32
BF16 = jnp.bfloat16
MESH = pl.DeviceIdType.MESH

D_MODEL = 2048
EPS = 1e-6
CHUNK = 64
MLA_HEADS = 16
QK_NOPE = 128
QK_ROPE = 64
HEAD_PAD = 256
MEM_TOKENS = 256
MEM_HEADS = 4
MEM_HEAD_DIM = 256
ROPE_THETA = 10000.0
MLA_SCALE = (QK_NOPE + QK_ROPE) ** -0.5
MEM_SCALE = MEM_HEAD_DIM ** -0.5
NEG = -1e30

ADAM_LR = 0.001
ADAM_B1 = 0.9
ADAM_B2 = 0.999
ADAM_EPS = 1e-08
ADAM_WD = 0.01
ADAM_STEP = 10

NP = 16384
COL_G, COL_Z, COL_C, COL_M, COL_S = 0, 6144, 8192, 12288, 14336

TM_PROJ = 1024
TN_PROJ = 1024
TS_ROW = 512
TQ_ATT = 1024
TK_ATT = 1024
VMEM_LIMIT = 56 * 1024 * 1024

NT_DIMS = (((1,), (1,)), ((), ()))
TN_DIMS = (((0,), (0,)), ((), ()))


def _dot(a, b, mode="nn"):
    if mode == "nn":
        return jnp.dot(a, b, preferred_element_type=F32)
    return lax.dot_general(a, b, NT_DIMS if mode == "nt" else TN_DIMS, preferred_element_type=F32)


def _sigmoid(z):
    return 1.0 / (1.0 + jnp.exp(-z))


def _params(sem=None, vmem=VMEM_LIMIT):
    return pltpu.CompilerParams(dimension_semantics=sem, vmem_limit_bytes=vmem)


def _fold8(v):
    r, c = v.shape
    return v.reshape(r // 8, 8, c).sum(axis=0)


def _swap32(t):
    lane = lax.broadcasted_iota(jnp.int32, t.shape, 1)
    return jnp.where(lane < 32, pltpu.roll(t, 96, 1), pltpu.roll(t, 32, 1))


def _rope(t, cs, sg):
    return t * cs + _swap32(t) * sg


def _rope_bwd(d, cs, sg):
    return d * cs + _swap32(d * sg)


def _rms_fwd(xf, g, n):
    r = lax.rsqrt(jnp.sum(xf * xf, axis=-1, keepdims=True) * (1.0 / n) + EPS)
    return xf * r * g


def _rms_bwd(xf, g, dy, n):
    r = lax.rsqrt(jnp.sum(xf * xf, axis=-1, keepdims=True) * (1.0 / n) + EPS)
    xhat = xf * r
    dyg = dy * g
    dx = r * (dyg - xhat * (jnp.sum(dyg * xhat, axis=-1, keepdims=True) * (1.0 / n)))
    return dx, dy * xhat


def _matmul(a, b, mode, out_dtype, name, tm=1024, tn=1024, tk=1024):
    if mode == "nn":
        (m, k), (_, n) = a.shape, b.shape
    elif mode == "nt":
        (m, k), (n, _) = a.shape, b.shape
    else:
        (k, m), (_, n) = a.shape, b.shape
    tm, tn, tk = min(tm, m), min(tn, n), min(tk, k)
    nk = k // tk
    if mode == "tn":
        a_spec = pl.BlockSpec((tk, tm), lambda i, j, kk: (kk, i))
    else:
        a_spec = pl.BlockSpec((tm, tk), lambda i, j, kk: (i, kk))
    if mode == "nt":
        b_spec = pl.BlockSpec((tn, tk), lambda i, j, kk: (j, kk))
    else:
        b_spec = pl.BlockSpec((tk, tn), lambda i, j, kk: (kk, j))

    def body(a_ref, b_ref, o_ref, acc_ref):
        kk = pl.program_id(2)

        @pl.when(kk == 0)
        def _():
            acc_ref[...] = jnp.zeros_like(acc_ref)

        acc_ref[...] += _dot(a_ref[...].astype(BF16), b_ref[...].astype(BF16), mode)

        @pl.when(kk == nk - 1)
        def _():
            o_ref[...] = acc_ref[...].astype(out_dtype)

    return pl.pallas_call(
        body, name=name, grid=(m // tm, n // tn, nk),
        in_specs=[a_spec, b_spec],
        out_specs=pl.BlockSpec((tm, tn), lambda i, j, kk: (i, j)),
        out_shape=jax.ShapeDtypeStruct((m, n), out_dtype),
        scratch_shapes=[pltpu.VMEM((tm, tn), F32)],
        compiler_params=_params(("parallel", "parallel", "arbitrary")),
    )(a, b)


def _proj_fwd(x, g, w_all):
    s = x.shape[0]
    tm, tn = min(TM_PROJ, s), TN_PROJ

    def body(x_ref, g_ref, w_ref, p_ref, h_ref):
        @pl.when(pl.program_id(1) == 0)
        def _():
            h_ref[...] = _rms_fwd(x_ref[...], g_ref[...], D_MODEL).astype(BF16)

        p_ref[...] = _dot(h_ref[...], w_ref[...]).astype(BF16)

    return pl.pallas_call(
        body, name="proj_fwd", grid=(s // tm, NP // tn),
        in_specs=[pl.BlockSpec((tm, D_MODEL), lambda i, j: (i, 0)),
                  pl.BlockSpec((1, D_MODEL), lambda i, j: (0, 0)),
                  pl.BlockSpec((D_MODEL, tn), lambda i, j: (0, j))],
        out_specs=[pl.BlockSpec((tm, tn), lambda i, j: (i, j)),
                   pl.BlockSpec((tm, D_MODEL), lambda i, j: (i, 0))],
        out_shape=[jax.ShapeDtypeStruct((s, NP), BF16), jax.ShapeDtypeStruct((s, D_MODEL), BF16)],
        compiler_params=_params(("parallel", "arbitrary")),
    )(x, g, w_all)


def _shift_rows(v, carry, j, ts, back):
    if back:
        main = pltpu.roll(v, j, 0)
        edge = pltpu.roll(jnp.concatenate([carry, v[:8]], axis=0), j, 0)[8:]
        return jnp.concatenate([edge, main[8:]], axis=0)
    main = pltpu.roll(v, ts - j, 0)
    edge = pltpu.roll(jnp.concatenate([v[ts - 8:], carry], axis=0), 16 - j, 0)[:8]
    return jnp.concatenate([main[:ts - 8], edge], axis=0)


def _conv_fwd(p_all, conv_w):
    s = p_all.shape[0]
    ts = min(TS_ROW, s)
    c0 = COL_C // 1024

    def body(cg_ref, bg_ref, u_ref, z_ref, w_ref, a_ref, co_ref, carry_ref):
        @pl.when(pl.program_id(0) == 0)
        def _():
            carry_ref[...] = jnp.zeros_like(carry_ref)

        p = cg_ref[...].astype(F32) * u_ref[...].astype(F32)
        carry = carry_ref[...]
        co = (w_ref[2:3, :] * p + w_ref[1:2, :] * _shift_rows(p, carry, 1, ts, True)
              + w_ref[0:1, :] * _shift_rows(p, carry, 2, ts, True))
        carry_ref[...] = p[ts - 8:]
        z = z_ref[...].astype(F32)
        a_ref[...] = (bg_ref[...].astype(F32) * co * (z * _sigmoid(z))).astype(BF16)
        co_ref[...] = co.astype(BF16)

    seg = lambda c: pl.BlockSpec((ts, 1024), lambda i: (i, c0 + c))
    return pl.pallas_call(
        body, name="conv_fwd", grid=(s // ts,),
        in_specs=[seg(0), seg(1), seg(2), seg(3), pl.BlockSpec((3, 1024), lambda i: (0, 0))],
        out_specs=[pl.BlockSpec((ts, 1024), lambda i: (i, 0))] * 2,
        out_shape=[jax.ShapeDtypeStruct((s, 1024), BF16)] * 2,
        scratch_shapes=[pltpu.VMEM((8, 1024), F32)],
        compiler_params=_params(("arbitrary",)),
    )(p_all, p_all, p_all, p_all, conv_w)


def _lora_fwd(p_all, gq, gkv, gkr, cs, sg):
    s = p_all.shape[0]
    ts = min(TS_ROW, s)

    def body(cq_ref, ckv_ref, kr_ref, gq_ref, gkv_ref, gkr_ref, cs_ref, sg_ref, cqn_ref, ckvn_ref, krope_ref):
        cqn_ref[...] = _rms_fwd(cq_ref[...].astype(F32), gq_ref[...], 512).astype(BF16)
        ckvn_ref[...] = _rms_fwd(ckv_ref[...].astype(F32), gkv_ref[...], 512).astype(BF16)
        kn = _rms_fwd(kr_ref[...].astype(F32), gkr_ref[...], QK_ROPE)
        krope_ref[...] = _rope(kn, cs_ref[...], sg_ref[...]).astype(BF16)

    row = lambda w: pl.BlockSpec((ts, w), lambda i: (i, 0))
    vec = lambda w: pl.BlockSpec((1, w), lambda i: (0, 0))
    return pl.pallas_call(
        body, name="lora_fwd", grid=(s // ts,),
        in_specs=[pl.BlockSpec((ts, 512), lambda i: (i, COL_S // 512)),
                  pl.BlockSpec((ts, 512), lambda i: (i, COL_S // 512 + 1)),
                  pl.BlockSpec((ts, 128), lambda i: (i, (COL_S + 1024) // 128)),
                  vec(512), vec(512), vec(128), row(128), row(128)],
        out_specs=[row(512), row(512), row(128)],
        out_shape=[jax.ShapeDtypeStruct((s, 512), BF16), jax.ShapeDtypeStruct((s, 512), BF16),
                   jax.ShapeDtypeStruct((s, 128), BF16)],
        compiler_params=_params(("parallel",)),
    )(p_all, p_all, p_all, gq, gkv, gkr, cs, sg)


def _q_prep(q_raw, gn, gr, cs, sg):
    s = q_raw.shape[0]
    ts = min(TS_ROW, s)

    def body(q_ref, gn_ref, gr_ref, cs_ref, sg_ref, o_ref):
        q = q_ref[...].astype(F32)
        a = _rms_fwd(q[:, :128], gn_ref[...], QK_NOPE)
        b = _rope(_rms_fwd(q[:, 128:], gr_ref[...], QK_ROPE), cs_ref[...], sg_ref[...])
        o_ref[0] = jnp.concatenate([a, b], axis=1).astype(BF16)

    return pl.pallas_call(
        body, name="q_prep", grid=(MLA_HEADS, s // ts),
        in_specs=[pl.BlockSpec((ts, HEAD_PAD), lambda h, i: (i, h)),
                  pl.BlockSpec((1, 128), lambda h, i: (0, 0)), pl.BlockSpec((1, 128), lambda h, i: (0, 0)),
                  pl.BlockSpec((ts, 128), lambda h, i: (i, 0)), pl.BlockSpec((ts, 128), lambda h, i: (i, 0))],
        out_specs=pl.BlockSpec((1, ts, HEAD_PAD), lambda h, i: (h, i, 0)),
        out_shape=jax.ShapeDtypeStruct((MLA_HEADS, s, HEAD_PAD), BF16),
        compiler_params=_params(("parallel", "parallel")),
    )(q_raw, gn, gr, cs, sg)


def _k_prep(kv, krope, gk):
    s = kv.shape[0]
    ts = min(TS_ROW, s)

    def body(k_ref, kr_ref, gk_ref, o_ref):
        a = _rms_fwd(k_ref[...].astype(F32), gk_ref[...], QK_NOPE)
        o_ref[0] = jnp.concatenate([a.astype(BF16), kr_ref[...]], axis=1)

    return pl.pallas_call(
        body, name="k_prep", grid=(MLA_HEADS, s // ts),
        in_specs=[pl.BlockSpec((ts, 128), lambda h, i: (i, 2 * h)),
                  pl.BlockSpec((ts, 128), lambda h, i: (i, 0)),
                  pl.BlockSpec((1, 128), lambda h, i: (0, 0))],
        out_specs=pl.BlockSpec((1, ts, HEAD_PAD), lambda h, i: (h, i, 0)),
        out_shape=jax.ShapeDtypeStruct((MLA_HEADS, s, HEAD_PAD), BF16),
        compiler_params=_params(("parallel", "parallel")),
    )(kv, krope, gk)


def _chunk_mask(tq, tk):
    r = lax.broadcasted_iota(jnp.int32, (tq, tk), 0) // CHUNK
    c = lax.broadcasted_iota(jnp.int32, (tq, tk), 1) // CHUNK
    return c <= r


def _flash_fwd(qcat, kcat, kv, p_all):
    s = qcat.shape[1]
    tq = tk = min(TQ_ATT, s)
    nq = s // tq
    zc = COL_Z // 128

    def body(q_ref, k_ref, v_ref, z_ref, y_ref, a_ref, lse_ref, m_sc, l_sc, acc_sc):
        i, j = pl.program_id(1), pl.program_id(2)

        @pl.when(j == 0)
        def _():
            m_sc[...] = jnp.full_like(m_sc, NEG)
            l_sc[...] = jnp.zeros_like(l_sc)
            acc_sc[...] = jnp.zeros_like(acc_sc)

        def step(masked):
            sc = _dot(q_ref[0], k_ref[0], "nt") * MLA_SCALE
            if masked:
                sc = jnp.where(_chunk_mask(tq, tk), sc, NEG)
            m_prev = m_sc[...]
            m_new = jnp.maximum(m_prev, jnp.max(sc, axis=-1, keepdims=True))
            alpha = jnp.exp(m_prev - m_new)
            p = jnp.exp(sc - m_new)
            l_sc[...] = alpha * l_sc[...] + jnp.sum(p, axis=-1, keepdims=True)
            acc_sc[...] = alpha * acc_sc[...] + _dot(p.astype(BF16), v_ref[...])
            m_sc[...] = m_new

        @pl.when(j < i)
        def _():
            step(False)

        @pl.when(j == i)
        def _():
            step(True)
            y = acc_sc[...] * (1.0 / l_sc[...])
            z = z_ref[...].astype(F32)
            y_ref[...] = y.astype(BF16)
            a_ref[...] = (y * (z * _sigmoid(z))).astype(BF16)
            lse_ref[0] = jnp.broadcast_to(m_sc[...] + jnp.log(l_sc[...]), (tq, 128))

    return pl.pallas_call(
        body, name="flash_fwd", grid=(MLA_HEADS, nq, nq),
        in_specs=[pl.BlockSpec((1, tq, HEAD_PAD), lambda h, i, j: (h, i, 0)),
                  pl.BlockSpec((1, tk, HEAD_PAD), lambda h, i, j: (h, jnp.minimum(i, j), 0)),
                  pl.BlockSpec((tk, 128), lambda h, i, j: (jnp.minimum(i, j), 2 * h + 1)),
                  pl.BlockSpec((tq, 128), lambda h, i, j: (i, zc + h))],
        out_specs=[pl.BlockSpec((tq, 128), lambda h, i, j: (i, h)),
                   pl.BlockSpec((tq, 128), lambda h, i, j: (i, h)),
                   pl.BlockSpec((1, tq, 128), lambda h, i, j: (h, i, 0))],
        out_shape=[jax.ShapeDtypeStruct((s, MLA_HEADS * 128), BF16),
                   jax.ShapeDtypeStruct((s, MLA_HEADS * 128), BF16),
                   jax.ShapeDtypeStruct((MLA_HEADS, s, 128), F32)],
        scratch_shapes=[pltpu.VMEM((tq, 1), F32), pltpu.VMEM((tq, 1), F32), pltpu.VMEM((tq, 128), F32)],
        compiler_params=_params(("parallel", "parallel", "arbitrary")),
    )(qcat, kcat, kv, p_all)


def _mem_kv_fwd(mem, g_norm, w_kv, g_kn):
    m = mem.shape[0]

    def body(mem_ref, g_ref, w_ref, gk_ref, memn_ref, kraw_ref, kn_ref, v_ref):
        memn = _rms_fwd(mem_ref[...], g_ref[...], D_MODEL).astype(BF16)
        memn_ref[...] = memn
        kvm = _dot(memn, w_ref[...])
        kraw_ref[...] = kvm[:, :1024]
        v_ref[...] = kvm[:, 1024:].astype(BF16)
        for hh in range(MEM_HEADS):
            sl = slice(hh * MEM_HEAD_DIM, (hh + 1) * MEM_HEAD_DIM)
            kn_ref[:, sl] = _rms_fwd(kvm[:, sl], gk_ref[...], MEM_HEAD_DIM).astype(BF16)

    return pl.pallas_call(
        body, name="mem_kv_fwd",
        out_shape=[jax.ShapeDtypeStruct((m, D_MODEL), BF16), jax.ShapeDtypeStruct((m, 1024), F32),
                   jax.ShapeDtypeStruct((m, 1024), BF16), jax.ShapeDtypeStruct((m, 1024), BF16)],
        compiler_params=_params(),
    )(mem, g_norm, w_kv, g_kn)


def _mem_attn_fwd(p_all, kn, v, gq):
    s = p_all.shape[0]
    tq = min(TS_ROW, s)

    def body(q_ref, z_ref, kn_ref, v_ref, gq_ref, a_ref):
        z = z_ref[...].astype(F32)
        gate = z * _sigmoid(z)
        for hh in range(MEM_HEADS):
            sl = slice(hh * MEM_HEAD_DIM, (hh + 1) * MEM_HEAD_DIM)
            qn = _rms_fwd(q_ref[:, sl].astype(F32), gq_ref[...], MEM_HEAD_DIM).astype(BF16)
            sc = _dot(qn, kn_ref[:, sl], "nt") * MEM_SCALE
            e = jnp.exp(sc - jnp.max(sc, axis=-1, keepdims=True))
            p = e * (1.0 / jnp.sum(e, axis=-1, keepdims=True))
            y = _dot(p.astype(BF16), v_ref[:, sl])
            a_ref[:, sl] = (y * gate[:, sl]).astype(BF16)

    full = lambda a: pl.BlockSpec(a.shape, lambda i: (0, 0))
    return pl.pallas_call(
        body, name="mem_attn_fwd", grid=(s // tq,),
        in_specs=[pl.BlockSpec((tq, 1024), lambda i: (i, COL_M // 1024)),
                  pl.BlockSpec((tq, 1024), lambda i: (i, COL_M // 1024 + 1)),
                  full(kn), full(v), full(gq)],
        out_specs=pl.BlockSpec((tq, 1024), lambda i: (i, 0)),
        out_shape=jax.ShapeDtypeStruct((s, 1024), BF16),
        compiler_params=_params(("parallel",)),
    )(p_all, p_all, kn, v, gq)


def _merge_fwd(a_conv, w_c, a_mla, w_m, a_mem, w_e, p_all):
    s = a_conv.shape[0]
    tm, tn = min(TS_ROW, s), 1024
    nj = D_MODEL // tn

    def body(ac_ref, wc_ref, am_ref, wm_ref, ae_ref, we_ref, gc_ref, gm_ref, ge_ref,
             oc_ref, om_ref, oe_ref, mg_ref):
        oc = _dot(ac_ref[...], wc_ref[...])
        om = _dot(am_ref[...], wm_ref[...])
        oe = _dot(ae_ref[...], we_ref[...])
        oc_ref[...] = oc.astype(BF16)
        om_ref[...] = om.astype(BF16)
        oe_ref[...] = oe.astype(BF16)
        mg_ref[...] = (_sigmoid(gc_ref[...].astype(F32)) * oc + _sigmoid(gm_ref[...].astype(F32)) * om
                       + _sigmoid(ge_ref[...].astype(F32)) * oe).astype(BF16)

    act = lambda k: pl.BlockSpec((tm, k), lambda i, j: (i, 0))
    wgt = lambda k: pl.BlockSpec((k, tn), lambda i, j: (0, j))
    gate = lambda b: pl.BlockSpec((tm, tn), lambda i, j: (i, b * nj + j))
    out = pl.BlockSpec((tm, tn), lambda i, j: (i, j))
    return pl.pallas_call(
        body, name="merge_fwd", grid=(s // tm, nj),
        in_specs=[act(1024), wgt(1024), act(2048), wgt(2048), act(1024), wgt(1024), gate(0), gate(1), gate(2)],
        out_specs=[out] * 4,
        out_shape=[jax.ShapeDtypeStruct((s, D_MODEL), BF16)] * 4,
        compiler_params=_params(("parallel", "parallel")),
    )(a_conv, w_c, a_mla, w_m, a_mem, w_e, p_all, p_all, p_all)


def _out_fwd(merged, w_o, x, target):
    s = merged.shape[0]
    tm, tn = min(TS_ROW, s), 1024
    nj = D_MODEL // tn

    def body(mg_ref, w_ref, x_ref, t_ref, dy_ref, dyb_ref, ls_ref):
        e = x_ref[...] + _dot(mg_ref[...], w_ref[...]) - t_ref[...]
        dy = e * (1.0 / D_MODEL)
        dy_ref[...] = dy
        dyb_ref[...] = dy.astype(BF16)
        r = _fold8(e * e)
        acc = r[:, 0:128]
        for cc in range(1, tn // 128):
            acc = acc + r[:, cc * 128:(cc + 1) * 128]
        ls_ref[...] = acc

    tile = pl.BlockSpec((tm, tn), lambda i, j: (i, j))
    return pl.pallas_call(
        body, name="out_fwd", grid=(s // tm, nj),
        in_specs=[pl.BlockSpec((tm, D_MODEL), lambda i, j: (i, 0)),
                  pl.BlockSpec((D_MODEL, tn), lambda i, j: (0, j)), tile, tile],
        out_specs=[tile, tile, pl.BlockSpec((8, 128), lambda i, j: (i, j))],
        out_shape=[jax.ShapeDtypeStruct((s, D_MODEL), F32), jax.ShapeDtypeStruct((s, D_MODEL), BF16),
                   jax.ShapeDtypeStruct((s // tm * 8, nj * 128), F32)],
        compiler_params=_params(("parallel", "parallel")),
    )(merged, w_o, x, target)


def _merge_bwd(dyb, w_o, p_all, o_c, o_m, o_e):
    s = dyb.shape[0]
    tm = min(256, s)

    def body(dy_ref, w_ref, g_ref, oc_ref, om_ref, oe_ref, dc_ref, dm_ref, de_ref, dg_ref):
        dmg = _dot(dy_ref[...], w_ref[...], "nt")
        for b, (o_ref, d_ref) in enumerate(((oc_ref, dc_ref), (om_ref, dm_ref), (oe_ref, de_ref))):
            sl = slice(b * D_MODEL, (b + 1) * D_MODEL)
            sg = _sigmoid(g_ref[:, sl].astype(F32))
            d_ref[...] = (dmg * sg).astype(BF16)
            dg_ref[:, sl] = (dmg * o_ref[...].astype(F32) * sg * (1.0 - sg)).astype(BF16)

    row = pl.BlockSpec((tm, D_MODEL), lambda i: (i, 0))
    wide = pl.BlockSpec((tm, 3 * D_MODEL), lambda i: (i, 0))
    return pl.pallas_call(
        body, name="merge_bwd", grid=(s // tm,),
        in_specs=[row, pl.BlockSpec((D_MODEL, D_MODEL), lambda i: (0, 0)), wide, row, row, row],
        out_specs=[row, row, row, wide],
        out_shape=[jax.ShapeDtypeStruct((s, D_MODEL), BF16)] * 3 + [jax.ShapeDtypeStruct((s, 3 * D_MODEL), BF16)],
        compiler_params=_params(("parallel",)),
    )(dyb, w_o, p_all, o_c, o_m, o_e)


def _conv_bwd(da, p_all, co, conv_w):
    s = da.shape[0]
    ts = min(TS_ROW, s)
    n = s // ts
    c0 = COL_C // 1024

    def body(da_ref, cg_ref, bg_ref, u_ref, z_ref, co_ref, w_ref, dp_ref, dw_ref, carry_ref):
        @pl.when(pl.program_id(0) == 0)
        def _():
            carry_ref[...] = jnp.zeros_like(carry_ref)
            dw_ref[...] = jnp.zeros_like(dw_ref)

        da_ = da_ref[...].astype(F32)
        cg, bg = cg_ref[...].astype(F32), bg_ref[...].astype(F32)
        u, z, cov = u_ref[...].astype(F32), z_ref[...].astype(F32), co_ref[...].astype(F32)
        sz = _sigmoid(z)
        dyc = da_ * (z * sz)
        dz = da_ * (bg * cov) * (sz * (1.0 + z * (1.0 - sz)))
        db = dyc * cov
        dco = dyc * bg
        carry = carry_ref[...]
        d1 = _shift_rows(dco, carry, 1, ts, False)
        d2 = _shift_rows(dco, carry, 2, ts, False)
        carry_ref[...] = dco[:8]
        dpp = w_ref[2:3, :] * dco + w_ref[1:2, :] * d1 + w_ref[0:1, :] * d2
        p = cg * u
        dw_ref[0] += _fold8(p * d2)
        dw_ref[1] += _fold8(p * d1)
        dw_ref[2] += _fold8(p * dco)
        dp_ref[...] = jnp.concatenate([dpp * u, db, dpp * cg, dz], axis=1).astype(BF16)

    rev = lambda c: pl.BlockSpec((ts, 1024), lambda i: (n - 1 - i, c))
    return pl.pallas_call(
        body, name="conv_bwd", grid=(n,),
        in_specs=[rev(0), rev(c0), rev(c0 + 1), rev(c0 + 2), rev(c0 + 3), rev(0),
                  pl.BlockSpec((3, 1024), lambda i: (0, 0))],
        out_specs=[pl.BlockSpec((ts, 4096), lambda i: (n - 1 - i, 0)),
                   pl.BlockSpec((3, 8, 1024), lambda i: (0, 0, 0))],
        out_shape=[jax.ShapeDtypeStruct((s, 4096), BF16), jax.ShapeDtypeStruct((3, 8, 1024), F32)],
        scratch_shapes=[pltpu.VMEM((8, 1024), F32)],
        compiler_params=_params(("arbitrary",)),
    )(da, p_all, p_all, p_all, p_all, co, conv_w)


def _mla_gate_bwd(da, y, p_all):
    s = da.shape[0]
    ts = min(TS_ROW, s)

    def body(da_ref, y_ref, z_ref, dy_ref, dz_ref, dl_ref):
        da_, yv, z = da_ref[...].astype(F32), y_ref[...].astype(F32), z_ref[...].astype(F32)
        sz = _sigmoid(z)
        dyv = da_ * (z * sz)
        dy_ref[...] = dyv.astype(BF16)
        dz_ref[...] = (da_ * yv * (sz * (1.0 + z * (1.0 - sz)))).astype(BF16)
        pr = dyv * yv
        for h in range(MLA_HEADS):
            dl_ref[h] = jnp.broadcast_to(jnp.sum(pr[:, h * 128:(h + 1) * 128], axis=-1, keepdims=True), (ts, 128))

    row = pl.BlockSpec((ts, D_MODEL), lambda i: (i, 0))
    return pl.pallas_call(
        body, name="mla_gate_bwd", grid=(s // ts,),
        in_specs=[row, row, pl.BlockSpec((ts, D_MODEL), lambda i: (i, COL_Z // D_MODEL))],
        out_specs=[row, row, pl.BlockSpec((MLA_HEADS, ts, 128), lambda i: (0, i, 0))],
        out_shape=[jax.ShapeDtypeStruct((s, D_MODEL), BF16)] * 2 + [jax.ShapeDtypeStruct((MLA_HEADS, s, 128), F32)],
        compiler_params=_params(("parallel",)),
    )(da, y, p_all)


def _flash_bwd(qcat, kcat, kv, dy, lse, delta):
    s = qcat.shape[1]
    tq = tk = min(TQ_ATT, s)
    nq = s // tq

    def body(q_ref, k_ref, v_ref, do_ref, lse_ref, dl_ref, dq_ref, dk_ref, dv_ref):
        j, i = pl.program_id(1), pl.program_id(2)

        def step(masked):
            q, k = q_ref[0], k_ref[0]
            sc = _dot(q, k, "nt") * MLA_SCALE
            if masked:
                sc = jnp.where(_chunk_mask(tq, tk), sc, NEG)
            p = jnp.exp(sc - lse_ref[0][:, 0:1])
            do = do_ref[...]
            dpv = _dot(do, v_ref[...], "nt")
            ds = (p * (dpv - dl_ref[0][:, 0:1]) * MLA_SCALE).astype(BF16)
            pb = p.astype(BF16)
            dv_new = _dot(pb, do, "tn")
            dk_new = _dot(ds, q, "tn")
            dq_new = _dot(ds, k)
            rows = pl.ds(pl.multiple_of(i * tq, tq), tq)
            if masked:
                dv_ref[0] = dv_new
                dk_ref[0] = dk_new
            else:
                dv_ref[0] += dv_new
                dk_ref[0] += dk_new

            @pl.when(j == 0)
            def _():
                dq_ref[0, rows, :] = dq_new

            @pl.when(j > 0)
            def _():
                dq_ref[0, rows, :] += dq_new

        @pl.when(i == j)
        def _():
            step(True)

        @pl.when(i > j)
        def _():
            step(False)

    qrow = lambda w: pl.BlockSpec((1, tq, w), lambda h, j, i: (h, jnp.maximum(i, j), 0))
    krow = lambda w: pl.BlockSpec((1, tk, w), lambda h, j, i: (h, j, 0))
    return pl.pallas_call(
        body, name="flash_bwd", grid=(MLA_HEADS, nq, nq),
        in_specs=[qrow(HEAD_PAD), krow(HEAD_PAD),
                  pl.BlockSpec((tk, 128), lambda h, j, i: (j, 2 * h + 1)),
                  pl.BlockSpec((tq, 128), lambda h, j, i: (jnp.maximum(i, j), h)),
                  qrow(128), qrow(128)],
        out_specs=[pl.BlockSpec((1, s, HEAD_PAD), lambda h, j, i: (h, 0, 0)), krow(HEAD_PAD), krow(128)],
        out_shape=[jax.ShapeDtypeStruct((MLA_HEADS, s, HEAD_PAD), F32),
                   jax.ShapeDtypeStruct((MLA_HEADS, s, HEAD_PAD), F32),
                   jax.ShapeDtypeStruct((MLA_HEADS, s, 128), F32)],
        compiler_params=_params(("parallel", "arbitrary", "arbitrary")),
    )(qcat, kcat, kv, dy, lse, delta)


def _q_prep_bwd(dqcat, q_raw, gn, gr, cs, sg):
    s = q_raw.shape[0]
    ts = min(TS_ROW, s)

    def body(dq_ref, q_ref, gn_ref, gr_ref, cs_ref, sg_ref, o_ref, dg_ref):
        @pl.when((pl.program_id(0) == 0) & (pl.program_id(1) == 0))
        def _():
            dg_ref[...] = jnp.zeros_like(dg_ref)

        q = q_ref[...].astype(F32)
        d = dq_ref[0]
        da, ga = _rms_bwd(q[:, :128], gn_ref[...], d[:, :128], QK_NOPE)
        db, gb = _rms_bwd(q[:, 128:], gr_ref[...], _rope_bwd(d[:, 128:], cs_ref[...], sg_ref[...]), QK_ROPE)
        o_ref[...] = jnp.concatenate([da, db], axis=1).astype(BF16)
        dg_ref[0] += _fold8(ga)
        dg_ref[1] += _fold8(gb)

    return pl.pallas_call(
        body, name="q_prep_bwd", grid=(MLA_HEADS, s // ts),
        in_specs=[pl.BlockSpec((1, ts, HEAD_PAD), lambda h, i: (h, i, 0)),
                  pl.BlockSpec((ts, HEAD_PAD), lambda h, i: (i, h)),
                  pl.BlockSpec((1, 128), lambda h, i: (0, 0)), pl.BlockSpec((1, 128), lambda h, i: (0, 0)),
                  pl.BlockSpec((ts, 128), lambda h, i: (i, 0)), pl.BlockSpec((ts, 128), lambda h, i: (i, 0))],
        out_specs=[pl.BlockSpec((ts, HEAD_PAD), lambda h, i: (i, h)),
                   pl.BlockSpec((2, 8, 128), lambda h, i: (0, 0, 0))],
        out_shape=[jax.ShapeDtypeStruct((s, MLA_HEADS * HEAD_PAD), BF16), jax.ShapeDtypeStruct((2, 8, 128), F32)],
        compiler_params=_params(("arbitrary", "arbitrary")),
    )(dqcat, q_raw, gn, gr, cs, sg)


def _k_prep_bwd(dkcat, dv, kv, gk):
    s = kv.shape[0]
    ts = min(TS_ROW, s)

    def body(dk_ref, dv_ref, k_ref, gk_ref, o_ref, dkr_ref, dg_ref):
        i, h = pl.program_id(0), pl.program_id(1)

        @pl.when((i == 0) & (h == 0))
        def _():
            dg_ref[...] = jnp.zeros_like(dg_ref)

        d = dk_ref[0]
        dk_raw, gg = _rms_bwd(k_ref[...].astype(F32), gk_ref[...], d[:, :128], QK_NOPE)
        o_ref[...] = jnp.concatenate([dk_raw, dv_ref[0]], axis=1).astype(BF16)
        dg_ref[...] += _fold8(gg)

        @pl.when(h == 0)
        def _():
            dkr_ref[...] = d[:, 128:]

        @pl.when(h > 0)
        def _():
            dkr_ref[...] += d[:, 128:]

    return pl.pallas_call(
        body, name="k_prep_bwd", grid=(s // ts, MLA_HEADS),
        in_specs=[pl.BlockSpec((1, ts, HEAD_PAD), lambda i, h: (h, i, 0)),
                  pl.BlockSpec((1, ts, 128), lambda i, h: (h, i, 0)),
                  pl.BlockSpec((ts, 128), lambda i, h: (i, 2 * h)),
                  pl.BlockSpec((1, 128), lambda i, h: (0, 0))],
        out_specs=[pl.BlockSpec((ts, HEAD_PAD), lambda i, h: (i, h)),
                   pl.BlockSpec((ts, 128), lambda i, h: (i, 0)),
                   pl.BlockSpec((8, 128), lambda i, h: (0, 0))],
        out_shape=[jax.ShapeDtypeStruct((s, MLA_HEADS * HEAD_PAD), BF16), jax.ShapeDtypeStruct((s, 128), F32),
                   jax.ShapeDtypeStruct((8, 128), F32)],
        compiler_params=_params(("arbitrary", "arbitrary")),
    )(dkcat, dv, kv, gk)


def _lora_bwd(dcqn, dckvn, dkr, p_all, gq, gkv, gkr, cs, sg):
    s = p_all.shape[0]
    ts = min(TS_ROW, s)

    def body(dq_ref, dkv_ref, dkr_ref, cq_ref, ckv_ref, kr_ref, gq_ref, gkv_ref, gkr_ref, cs_ref, sg_ref,
             o_ref, dgq_ref, dgkv_ref, dgkr_ref):
        @pl.when(pl.program_id(0) == 0)
        def _():
            dgq_ref[...] = jnp.zeros_like(dgq_ref)
            dgkv_ref[...] = jnp.zeros_like(dgkv_ref)
            dgkr_ref[...] = jnp.zeros_like(dgkr_ref)

        dq, g1 = _rms_bwd(cq_ref[...].astype(F32), gq_ref[...], dq_ref[...], 512)
        dkv, g2 = _rms_bwd(ckv_ref[...].astype(F32), gkv_ref[...], dkv_ref[...], 512)
        dkn = _rope_bwd(dkr_ref[...], cs_ref[...], sg_ref[...])
        dk, g3 = _rms_bwd(kr_ref[...].astype(F32), gkr_ref[...], dkn, QK_ROPE)
        o_ref[...] = jnp.concatenate([dq, dkv, dk, jnp.zeros((ts, 2048 - 1152), F32)], axis=1).astype(BF16)
        dgq_ref[...] += _fold8(g1)
        dgkv_ref[...] += _fold8(g2)
        dgkr_ref[...] += _fold8(g3)

    row = lambda w: pl.BlockSpec((ts, w), lambda i: (i, 0))
    vec = lambda w: pl.BlockSpec((1, w), lambda i: (0, 0))
    acc = lambda w: pl.BlockSpec((8, w), lambda i: (0, 0))
    return pl.pallas_call(
        body, name="lora_bwd", grid=(s // ts,),
        in_specs=[row(512), row(512), row(128),
                  pl.BlockSpec((ts, 512), lambda i: (i, COL_S // 512)),
                  pl.BlockSpec((ts, 512), lambda i: (i, COL_S // 512 + 1)),
                  pl.BlockSpec((ts, 128), lambda i: (i, (COL_S + 1024) // 128)),
                  vec(512), vec(512), vec(128), row(128), row(128)],
        out_specs=[row(2048), acc(512), acc(512), acc(128)],
        out_shape=[jax.ShapeDtypeStruct((s, 2048), BF16), jax.ShapeDtypeStruct((8, 512), F32),
                   jax.ShapeDtypeStruct((8, 512), F32), jax.ShapeDtypeStruct((8, 128), F32)],
        compiler_params=_params(("arbitrary",)),
    )(dcqn, dckvn, dkr, p_all, p_all, p_all, gq, gkv, gkr, cs, sg)


def _mem_attn_bwd(da, p_all, kn, v, gq):
    s = p_all.shape[0]
    tq = min(TS_ROW, s)

    def body(da_ref, q_ref, z_ref, kn_ref, v_ref, gq_ref, o_ref, dkn_ref, dv_ref, dg_ref):
        @pl.when(pl.program_id(0) == 0)
        def _():
            dkn_ref[...] = jnp.zeros_like(dkn_ref)
            dv_ref[...] = jnp.zeros_like(dv_ref)
            dg_ref[...] = jnp.zeros_like(dg_ref)

        z = z_ref[...].astype(F32)
        da_ = da_ref[...].astype(F32)
        sz = _sigmoid(z)
        gate = z * sz
        dgate = sz * (1.0 + z * (1.0 - sz))
        for hh in range(MEM_HEADS):
            sl = slice(hh * MEM_HEAD_DIM, (hh + 1) * MEM_HEAD_DIM)
            qf = q_ref[:, sl].astype(F32)
            qn = _rms_fwd(qf, gq_ref[...], MEM_HEAD_DIM).astype(BF16)
            knh, vh = kn_ref[:, sl], v_ref[:, sl]
            sc = _dot(qn, knh, "nt") * MEM_SCALE
            e = jnp.exp(sc - jnp.max(sc, axis=-1, keepdims=True))
            p = e * (1.0 / jnp.sum(e, axis=-1, keepdims=True))
            pb = p.astype(BF16)
            y = _dot(pb, vh)
            dyh = da_[:, sl] * gate[:, sl]
            o_ref[:, 1024 + hh * MEM_HEAD_DIM:1024 + (hh + 1) * MEM_HEAD_DIM] = (
                da_[:, sl] * y * dgate[:, sl]).astype(BF16)
            dyb = dyh.astype(BF16)
            dpm = _dot(dyb, vh, "nt")
            ds = (p * (dpm - jnp.sum(dpm * p, axis=-1, keepdims=True)) * MEM_SCALE).astype(BF16)
            dqn = _dot(ds, knh)
            dkn_ref[:, sl] += _dot(ds, qn, "tn")
            dv_ref[:, sl] += _dot(pb, dyb, "tn")
            dq, gg = _rms_bwd(qf, gq_ref[...], dqn, MEM_HEAD_DIM)
            o_ref[:, sl] = dq.astype(BF16)
            dg_ref[...] += _fold8(gg)

    full = lambda a: pl.BlockSpec(a.shape, lambda i: (0, 0))
    return pl.pallas_call(
        body, name="mem_attn_bwd", grid=(s // tq,),
        in_specs=[pl.BlockSpec((tq, 1024), lambda i: (i, 0)),
                  pl.BlockSpec((tq, 1024), lambda i: (i, COL_M // 1024)),
                  pl.BlockSpec((tq, 1024), lambda i: (i, COL_M // 1024 + 1)),
                  full(kn), full(v), full(gq)],
        out_specs=[pl.BlockSpec((tq, 2048), lambda i: (i, 0)),
                   pl.BlockSpec((MEM_TOKENS, 1024), lambda i: (0, 0)),
                   pl.BlockSpec((MEM_TOKENS, 1024), lambda i: (0, 0)),
                   pl.BlockSpec((8, MEM_HEAD_DIM), lambda i: (0, 0))],
        out_shape=[jax.ShapeDtypeStruct((s, 2048), BF16), jax.ShapeDtypeStruct((MEM_TOKENS, 1024), F32),
                   jax.ShapeDtypeStruct((MEM_TOKENS, 1024), F32), jax.ShapeDtypeStruct((8, MEM_HEAD_DIM), F32)],
        compiler_params=_params(("arbitrary",)),
    )(da, p_all, p_all, kn, v, gq)


def _mem_kv_bwd(dkn, dv, kraw, memn, mem, g_norm, w_kv, g_kn):
    m = mem.shape[0]

    def body(dkn_ref, dv_ref, kraw_ref, memn_ref, mem_ref, g_ref, w_ref, gk_ref, dw_ref, dgn_ref, dgk_ref, dkv_sc):
        gk_acc = jnp.zeros((8, MEM_HEAD_DIM), F32)
        for hh in range(MEM_HEADS):
            sl = slice(hh * MEM_HEAD_DIM, (hh + 1) * MEM_HEAD_DIM)
            dk, gg = _rms_bwd(kraw_ref[:, sl], gk_ref[...], dkn_ref[:, sl], MEM_HEAD_DIM)
            dkv_sc[:, sl] = dk.astype(BF16)
            gk_acc = gk_acc + _fold8(gg)
        dgk_ref[...] = gk_acc
        dkv_sc[:, 1024:] = dv_ref[...].astype(BF16)
        dkv = dkv_sc[...]
        dw_ref[...] = _dot(memn_ref[...], dkv, "tn")
        dmemn = _dot(dkv, w_ref[...], "nt")
        xf = mem_ref[...]
        r = lax.rsqrt(jnp.mean(xf * xf, axis=-1, keepdims=True) + EPS)
        dgn_ref[...] = _fold8(dmemn * (xf * r))

    return pl.pallas_call(
        body, name="mem_kv_bwd",
        out_shape=[jax.ShapeDtypeStruct((D_MODEL, D_MODEL), F32), jax.ShapeDtypeStruct((8, D_MODEL), F32),
                   jax.ShapeDtypeStruct((8, MEM_HEAD_DIM), F32)],
        scratch_shapes=[pltpu.VMEM((m, D_MODEL), BF16)],
        compiler_params=_params(),
    )(dkn, dv, kraw, memn, mem, g_norm, w_kv, g_kn)


def _dh_bwd(dps, w_all):
    s = dps[0].shape[0]
    tm, tk = min(TS_ROW, s), 1024
    widths = [a.shape[1] // tk for a in dps]
    starts = [int(v) for v in np.cumsum([0] + widths[:-1])]
    nk = sum(widths)

    def body(*refs):
        d_refs, w_ref, o_ref, acc_ref = refs[:5], refs[5], refs[6], refs[7]
        kk = pl.program_id(1)

        @pl.when(kk == 0)
        def _():
            acc_ref[...] = jnp.zeros_like(acc_ref)

        for d_ref, st, wd in zip(d_refs, starts, widths):
            @pl.when((kk >= st) & (kk < st + wd))
            def _(d_ref=d_ref):
                acc_ref[...] += _dot(d_ref[...], w_ref[...], "nt")

        @pl.when(kk == nk - 1)
        def _():
            o_ref[...] = acc_ref[...]

    def dspec(st, wd):
        return pl.BlockSpec((tm, tk), lambda i, kk: (i, jnp.clip(kk - st, 0, wd - 1)))

    return pl.pallas_call(
        body, name="dh_bwd", grid=(s // tm, nk),
        in_specs=[dspec(st, wd) for st, wd in zip(starts, widths)]
        + [pl.BlockSpec((D_MODEL, tk), lambda i, kk: (0, kk))],
        out_specs=pl.BlockSpec((tm, D_MODEL), lambda i, kk: (i, 0)),
        out_shape=jax.ShapeDtypeStruct((s, D_MODEL), F32),
        scratch_shapes=[pltpu.VMEM((tm, D_MODEL), F32)],
        compiler_params=_params(("parallel", "arbitrary")),
    )(*dps, w_all)


def _dx_bwd(dh, x, dy, g):
    s = x.shape[0]
    ts = min(256, s)

    def body(dh_ref, x_ref, dy_ref, g_ref, o_ref, dg_ref):
        @pl.when(pl.program_id(0) == 0)
        def _():
            dg_ref[...] = jnp.zeros_like(dg_ref)

        dx, gg = _rms_bwd(x_ref[...], g_ref[...], dh_ref[...], D_MODEL)
        o_ref[...] = dy_ref[...] + dx
        dg_ref[...] += _fold8(gg)

    row = pl.BlockSpec((ts, D_MODEL), lambda i: (i, 0))
    return pl.pallas_call(
        body, name="dx_bwd", grid=(s // ts,),
        in_specs=[row, row, row, pl.BlockSpec((1, D_MODEL), lambda i: (0, 0))],
        out_specs=[row, pl.BlockSpec((8, D_MODEL), lambda i: (0, 0))],
        out_shape=[jax.ShapeDtypeStruct((s, D_MODEL), F32), jax.ShapeDtypeStruct((8, D_MODEL), F32)],
        compiler_params=_params(("arbitrary",)),
    )(dh, x, dy, g)


def _pad128(v, n):
    return jnp.pad(v.reshape(1, n), ((0, 0), (0, 128 - n)))


def _local_step(x, positions, mem, target, wts, gains):
    half = QK_ROPE // 2
    inv_freq = jnp.power(ROPE_THETA, -jnp.arange(half, dtype=F32) / half)
    ang = positions.astype(F32)[:, None] * inv_freq
    cos, sin = jnp.cos(ang), jnp.sin(ang)
    zpad = jnp.zeros((x.shape[0], 64), F32)
    cs = jnp.concatenate([cos, cos, zpad], axis=1)
    sg = jnp.concatenate([-sin, sin, zpad], axis=1)
    g_qr, g_kr = _pad128(gains["mla_qn_rope_g"], 64), _pad128(gains["mla_kn_rope_g"], 64)
    g_qn, g_kn = gains["mla_qn_nope_g"], gains["mla_kn_nope_g"]

    p_all, h = _proj_fwd(x, gains["norm_g"], wts["w_all"])
    a_conv, co = _conv_fwd(p_all, wts["conv_w"])
    cqn, ckvn, krope = _lora_fwd(p_all, gains["mla_q_norm_g"], gains["mla_kv_norm_g"], g_kr, cs, sg)
    q_raw = _matmul(cqn, wts["w_uq"], "nn", BF16, "q_up")
    kv = _matmul(ckvn, wts["w_ukv"], "nn", BF16, "kv_up")
    qcat = _q_prep(q_raw, g_qn, g_qr, cs, sg)
    kcat = _k_prep(kv, krope, g_kn)
    mla_y, a_mla, lse = _flash_fwd(qcat, kcat, kv, p_all)
    memn, kraw, kn, vmem = _mem_kv_fwd(mem, gains["mem_norm_g"], wts["w_mem_kv"], gains["mem_kn_g"])
    a_mem = _mem_attn_fwd(p_all, kn, vmem, gains["mem_qn_g"])
    o_c, o_m, o_e, merged = _merge_fwd(a_conv, wts["w_conv_out"], a_mla, wts["w_mla_out"], a_mem,
                                       wts["w_mem_out"], p_all)
    dy, dyb, loss_parts = _out_fwd(merged, wts["w_o"], x, target)

    gw = {}
    gw["w_o"] = _matmul(merged, dyb, "tn", F32, "dw_o")
    do_c, do_m, do_e, dp_g = _merge_bwd(dyb, wts["w_o"], p_all, o_c, o_m, o_e)
    gw["w_conv_out"] = _matmul(a_conv, do_c, "tn", F32, "dw_conv_out")
    gw["w_mla_out"] = _matmul(a_mla, do_m, "tn", F32, "dw_mla_out")
    gw["w_mem_out"] = _matmul(a_mem, do_e, "tn", F32, "dw_mem_out")
    da_conv = _matmul(do_c, wts["w_conv_out"], "nt", BF16, "da_conv")
    da_mla = _matmul(do_m, wts["w_mla_out"], "nt", BF16, "da_mla")
    da_mem = _matmul(do_e, wts["w_mem_out"], "nt", BF16, "da_mem")
    dp_c, dconv_w = _conv_bwd(da_conv, p_all, co, wts["conv_w"])
    dmla_y, dp_z, delta = _mla_gate_bwd(da_mla, mla_y, p_all)
    dqcat, dkcat, dv = _flash_bwd(qcat, kcat, kv, dmla_y, lse, delta)
    dq_raw, dg_q = _q_prep_bwd(dqcat, q_raw, g_qn, g_qr, cs, sg)
    dkv, dkr, dg_kn = _k_prep_bwd(dkcat, dv, kv, g_kn)
    gw["w_uq"] = _matmul(cqn, dq_raw, "tn", F32, "dw_uq")
    gw["w_ukv"] = _matmul(ckvn, dkv, "tn", F32, "dw_ukv")
    dcqn = _matmul(dq_raw, wts["w_uq"], "nt", F32, "dcqn")
    dckvn = _matmul(dkv, wts["w_ukv"], "nt", F32, "dckvn")
    dp_s, dg_qn, dg_kvn, dg_kr = _lora_bwd(dcqn, dckvn, dkr, p_all, gains["mla_q_norm_g"],
                                            gains["mla_kv_norm_g"], g_kr, cs, sg)
    dp_m, dkn, dvm, dg_mq = _mem_attn_bwd(da_mem, p_all, kn, vmem, gains["mem_qn_g"])
    gw["w_mem_kv"], dg_mn, dg_mk = _mem_kv_bwd(dkn, dvm, kraw, memn, mem, gains["mem_norm_g"],
                                               wts["w_mem_kv"], gains["mem_kn_g"])
    dps = [dp_g, dp_z, dp_c, dp_m, dp_s]
    dh = _dh_bwd(dps, wts["w_all"])
    grad_x, dg_n = _dx_bwd(dh, x, dy, gains["norm_g"])
    gw["w_all"] = [_matmul(h, d, "tn", F32, "dw_in_%d" % k) for k, d in enumerate(dps)]

    gsmall = {
        "norm_g": dg_n.sum(0), "conv_w": dconv_w.sum(1), "mla_q_norm_g": dg_qn.sum(0),
        "mla_kv_norm_g": dg_kvn.sum(0), "mla_qn_nope_g": dg_q[0].sum(0), "mla_qn_rope_g": dg_q[1].sum(0)[:64],
        "mla_kn_nope_g": dg_kn.sum(0), "mla_kn_rope_g": dg_kr.sum(0)[:64], "mem_norm_g": dg_mn.sum(0),
        "mem_qn_g": dg_mq.sum(0), "mem_kn_g": dg_mk.sum(0),
    }
    return loss_parts, grad_x, gw, gsmall


def _me():
    return lax.axis_index("x"), lax.axis_index("y"), lax.axis_index("c")


def _chips(x, y):
    return [(1 - x, y), (x, 1 - y), (1 - x, 1 - y)]


ANY = pl.BlockSpec(memory_space=pl.ANY)


def _allgather_shards(xs, name):
    r, c = xs.shape
    hr = r // 2

    def body(x_ref, out_ref, send_sems, recv_sems, local_sem):
        x, y, cc = _me()
        sibling = (x, y, 1 - cc)
        chips = _chips(x, y)

        def rows(px, py, pc):
            return out_ref.at[pl.ds((4 * px + 2 * py + pc) * hr, hr), :]

        def copy(k, block, to, src=None):
            return pltpu.make_async_remote_copy(
                src_ref=rows(*block) if src is None else src, dst_ref=rows(*block),
                send_sem=send_sems.at[k], recv_sem=recv_sems.at[k], device_id=to, device_id_type=MESH)

        mine = pltpu.make_async_copy(x_ref, out_ref.at[pl.ds((4 * x + 2 * y) * hr, r), :], local_sem)
        mine.start()
        my_half = x_ref.at[pl.ds(cc * hr, hr), :]
        first = [copy(j, (x, y, cc), (*chip, cc), src=my_half) for j, chip in enumerate(chips)]
        for cp in first:
            cp.start()
        passed = [copy(3 + j, (*chip, cc), sibling) for j, chip in enumerate(chips)]
        for j, chip in enumerate(chips):
            copy(j, (*chip, cc), (x, y, cc)).wait_recv()
            passed[j].start()
        for j, chip in enumerate(chips):
            copy(3 + j, (*chip, 1 - cc), (x, y, cc)).wait_recv()
        for cp in first + passed:
            cp.wait_send()
        mine.wait()

    out = pl.pallas_call(
        body, name=name, in_specs=[ANY], out_specs=ANY,
        out_shape=jax.ShapeDtypeStruct((8 * hr, c), xs.dtype),
        scratch_shapes=[pltpu.SemaphoreType.DMA((6,)), pltpu.SemaphoreType.DMA((6,)), pltpu.SemaphoreType.DMA],
    )(xs)
    return out.reshape(4, r, c)


def _swap_sibling(src, name):
    def body(src_ref, got_ref, send_sem, recv_sem):
        x, y, cc = _me()
        cp = pltpu.make_async_remote_copy(src_ref=src_ref.at[1 - cc], dst_ref=got_ref, send_sem=send_sem,
                                          recv_sem=recv_sem, device_id=(x, y, 1 - cc), device_id_type=MESH)
        cp.start()
        cp.wait()

    return pl.pallas_call(
        body, name=name, in_specs=[ANY], out_specs=ANY,
        out_shape=jax.ShapeDtypeStruct(src.shape[1:], src.dtype),
        scratch_shapes=[pltpu.SemaphoreType.DMA, pltpu.SemaphoreType.DMA],
    )(src)


def _exchange_chips(t, name):
    def body(t_ref, r_ref, send_sems, recv_sems):
        x, y, cc = _me()
        cps = []
        for j, (px, py) in enumerate(_chips(x, y)):
            cps.append(pltpu.make_async_remote_copy(
                src_ref=t_ref.at[2 * px + py], dst_ref=r_ref.at[j], send_sem=send_sems.at[j],
                recv_sem=recv_sems.at[j], device_id=(px, py, cc), device_id_type=MESH))
        for cp in cps:
            cp.start()
        for cp in cps:
            cp.wait()

    return pl.pallas_call(
        body, name=name, in_specs=[ANY], out_specs=ANY,
        out_shape=jax.ShapeDtypeStruct((3,) + t.shape[1:], t.dtype),
        scratch_shapes=[pltpu.SemaphoreType.DMA((3,)), pltpu.SemaphoreType.DMA((3,))],
    )(t)


def _join_halves(fh, name):
    def body(f_ref, out_ref, send_sem, recv_sem, local_sem):
        x, y, cc = _me()
        loc = pltpu.make_async_copy(f_ref, out_ref.at[cc], local_sem)
        rem = pltpu.make_async_remote_copy(src_ref=f_ref, dst_ref=out_ref.at[cc], send_sem=send_sem,
                                           recv_sem=recv_sem, device_id=(x, y, 1 - cc), device_id_type=MESH)
        loc.start()
        rem.start()
        rem.wait_send()
        pltpu.make_async_remote_copy(src_ref=f_ref, dst_ref=out_ref.at[1 - cc], send_sem=send_sem,
                                     recv_sem=recv_sem, device_id=(x, y, 1 - cc), device_id_type=MESH).wait_recv()
        loc.wait()

    return pl.pallas_call(
        body, name=name, in_specs=[ANY], out_specs=ANY,
        out_shape=jax.ShapeDtypeStruct((2,) + fh.shape, fh.dtype),
        scratch_shapes=[pltpu.SemaphoreType.DMA, pltpu.SemaphoreType.DMA, pltpu.SemaphoreType.DMA],
    )(fh)


def _row_tile(hr, c, itemsize):
    t = hr
    while t > 16 and t * c * itemsize > (2 << 20):
        t //= 2
    return t


def _add_own_half(gb, got, sel, name):
    _, _, hr, c = gb.shape
    tr = _row_tile(hr, c, 4)

    def body(sel_ref, a_ref, b_ref, o_ref):
        o_ref[...] = (a_ref[0].astype(F32) + b_ref[...].astype(F32)).astype(BF16)

    return pl.pallas_call(
        body, name=name,
        grid_spec=pltpu.PrefetchScalarGridSpec(
            num_scalar_prefetch=1, grid=(4, hr // tr),
            in_specs=[pl.BlockSpec((1, 1, tr, c), lambda k, i, sel_ref: (sel_ref[0], k, i, 0)),
                      pl.BlockSpec((1, tr, c), lambda k, i, sel_ref: (k, i, 0))],
            out_specs=pl.BlockSpec((1, tr, c), lambda k, i, sel_ref: (k, i, 0))),
        out_shape=jax.ShapeDtypeStruct(got.shape, BF16),
        compiler_params=_params(("parallel", "parallel")),
    )(sel, gb, got)


def _sum_partials(t, rcv, sel, name):
    _, hr, c = t.shape
    tr = _row_tile(hr, c, 4)

    def body(sel_ref, t_ref, r_ref, o_ref):
        o_ref[...] = ((t_ref[0].astype(F32) + r_ref[0].astype(F32)) + r_ref[1].astype(F32)) + r_ref[2].astype(F32)

    return pl.pallas_call(
        body, name=name,
        grid_spec=pltpu.PrefetchScalarGridSpec(
            num_scalar_prefetch=1, grid=(hr // tr,),
            in_specs=[pl.BlockSpec((1, tr, c), lambda i, sel_ref: (sel_ref[0], i, 0)),
                      pl.BlockSpec((3, tr, c), lambda i, sel_ref: (0, i, 0))],
            out_specs=pl.BlockSpec((tr, c), lambda i, sel_ref: (i, 0))),
        out_shape=jax.ShapeDtypeStruct((hr, c), F32),
        compiler_params=_params(("parallel",)),
    )(sel, t, rcv)


def _reduce_scatter(g_sharded, tag):
    _, r, c = g_sharded.shape
    hr = r // 2
    x, y, cc = _me()
    gb = g_sharded.astype(BF16).reshape(4, 2, hr, c).transpose(1, 0, 2, 3)
    got = _swap_sibling(gb, "rs_swap_" + tag)
    t = _add_own_half(gb, got, jnp.reshape(cc, (1,)).astype(jnp.int32), "rs_add_" + tag)
    rcv = _exchange_chips(t, "rs_ici_" + tag)
    fh = _sum_partials(t, rcv, jnp.reshape(2 * x + y, (1,)).astype(jnp.int32), "rs_sum_" + tag)
    return _join_halves(fh, "rs_join_" + tag).reshape(r, c)


def _allreduce_small(v, tag):
    r = v.shape[0]

    def body(v_ref, out_ref, buf_ref, send_sems, recv_sems):
        x, y, cc = _me()
        me = 4 * x + 2 * y + cc
        buf_ref[pl.ds(pl.multiple_of(me * r, 8), r), :] = v_ref[...]
        peers = [(x, y, 1 - cc)] + [(px, py, pc) for (px, py) in _chips(x, y) for pc in (cc, 1 - cc)]
        cps = []
        for k, (px, py, pc) in enumerate(peers):
            mine = buf_ref.at[pl.ds(pl.multiple_of(me * r, 8), r), :]
            cps.append(pltpu.make_async_remote_copy(
                src_ref=v_ref, dst_ref=mine, send_sem=send_sems.at[k], recv_sem=recv_sems.at[k],
                device_id=(px, py, pc), device_id_type=MESH))
        for cp in cps:
            cp.start()
        for cp in cps:
            cp.wait()
        acc = buf_ref[0:r, :]
        for d in range(1, 8):
            acc = acc + buf_ref[d * r:(d + 1) * r, :]
        out_ref[...] = acc

    vm = pl.BlockSpec(memory_space=pltpu.VMEM)
    return pl.pallas_call(
        body, name="allreduce_small_" + tag, in_specs=[vm], out_specs=vm,
        out_shape=jax.ShapeDtypeStruct((r, 128), F32),
        scratch_shapes=[pltpu.VMEM((8 * r, 128), F32), pltpu.SemaphoreType.DMA((7,)), pltpu.SemaphoreType.DMA((7,))],
    )(v)


def _adamw(w, g, m, v, name):
    r, c = w.shape
    tr = _row_tile(r, c, 4) if r >= 16 else r
    c1 = 1.0 / (1.0 - ADAM_B1 ** ADAM_STEP)
    c2 = 1.0 / (1.0 - ADAM_B2 ** ADAM_STEP)

    def body(w_ref, g_ref, m_ref, v_ref, d_ref, mo_ref, vo_ref):
        gg = g_ref[...]
        mn = ADAM_B1 * m_ref[...] + (1.0 - ADAM_B1) * gg
        vn = ADAM_B2 * v_ref[...] + (1.0 - ADAM_B2) * (gg * gg)
        mo_ref[...] = mn
        vo_ref[...] = vn
        d_ref[...] = -ADAM_LR * ((mn * c1) / (jnp.sqrt(vn * c2) + ADAM_EPS) + ADAM_WD * w_ref[...])

    blk = pl.BlockSpec((tr, c), lambda i: (i, 0))
    return pl.pallas_call(
        body, name=name, grid=(r // tr,), in_specs=[blk] * 4, out_specs=[blk] * 3,
        out_shape=[jax.ShapeDtypeStruct((r, c), F32)] * 3,
        compiler_params=_params(("parallel",)),
    )(w, g, m, v)


WEIGHTS = ['norm_g', 'w_in', 'conv_w', 'w_conv_out', 'mla_q_norm_g', 'w_uq', 'mla_kv_norm_g', 'w_ukv',
           'mla_qn_nope_g', 'mla_qn_rope_g', 'mla_kn_nope_g', 'mla_kn_rope_g', 'w_mla_out', 'mem_norm_g',
           'w_mem_kv', 'mem_qn_g', 'mem_kn_g', 'w_mem_out', 'w_o']
COL_SHARDED = ['w_in', 'w_conv_out', 'w_uq', 'w_ukv', 'w_mem_out']
ROW_SHARDED = ['w_mla_out', 'w_mem_kv', 'w_o']
SMALL = ['norm_g', 'conv_w', 'mla_q_norm_g', 'mla_kv_norm_g', 'mla_qn_nope_g', 'mla_qn_rope_g', 'mla_kn_nope_g',
         'mla_kn_rope_g', 'mem_norm_g', 'mem_qn_g', 'mem_kn_g']
SMALL_SIZES = [2048, 3072, 512, 512, 128, 64, 128, 64, 2048, 256, 256]
PACK_ROWS = 72


def _gather_weights(shards):
    full = {}
    for n in COL_SHARDED:
        g = _allgather_shards(shards[n].astype(BF16), "ag_" + n)
        full[n] = g.transpose(1, 0, 2).reshape(g.shape[1], 4 * g.shape[2])
    for n in ROW_SHARDED:
        g = _allgather_shards(shards[n].astype(BF16), "ag_" + n)
        full[n] = g.reshape(4 * g.shape[1], g.shape[2])
    return full


def _to_local_layouts(full):
    w_in = full["w_in"]
    conv, small, mz, memq, gates = (w_in[:, 0:4096], w_in[:, 4096:5184], w_in[:, 5184:7232],
                                    w_in[:, 7232:9280], w_in[:, 9280:15424])
    wts = dict(full)
    wts["w_all"] = jnp.concatenate([gates, mz, conv, memq, small, jnp.zeros((D_MODEL, 2048 - 1088), BF16)], axis=1)
    wts["w_uq"] = jnp.pad(full["w_uq"].reshape(512, MLA_HEADS, 192), ((0, 0), (0, 0), (0, 64))).reshape(512, 4096)
    del wts["w_in"]
    return wts


def _grads_to_reference_layout(gw):
    dw_g, dw_z, dw_c, dw_m, dw_s = gw["w_all"]
    out = {k: v for k, v in gw.items() if k != "w_all"}
    out["w_in"] = jnp.concatenate([dw_c, dw_s[:, :1088], dw_z, dw_m, dw_g], axis=1)
    out["w_uq"] = gw["w_uq"].reshape(512, MLA_HEADS, HEAD_PAD)[:, :, :192].reshape(512, 3072)
    return out


def kernel(x, positions, mem, norm_g, w_in, conv_w, w_conv_out, mla_q_norm_g, w_uq, mla_kv_norm_g, w_ukv, mla_qn_nope_g, mla_qn_rope_g, mla_kn_nope_g, mla_kn_rope_g, w_mla_out, mem_norm_g, w_mem_kv, mem_qn_g, mem_kn_g, w_mem_out, w_o, loss_target, m_norm_g, m_w_in, m_conv_w, m_w_conv_out, m_mla_q_norm_g, m_w_uq, m_mla_kv_norm_g, m_w_ukv, m_mla_qn_nope_g, m_mla_qn_rope_g, m_mla_kn_nope_g, m_mla_kn_rope_g, m_w_mla_out, m_mem_norm_g, m_w_mem_kv, m_mem_qn_g, m_mem_kn_g, m_w_mem_out, m_w_o, v_norm_g, v_w_in, v_conv_w, v_w_conv_out, v_mla_q_norm_g, v_w_uq, v_mla_kv_norm_g, v_w_ukv, v_mla_qn_nope_g, v_mla_qn_rope_g, v_mla_kn_nope_g, v_mla_kn_rope_g, v_w_mla_out, v_mem_norm_g, v_w_mem_kv, v_mem_qn_g, v_mem_kn_g, v_w_mem_out, v_w_o):
    args = locals()
    w = {n: args[n][0] for n in WEIGHTS}
    m1 = {n: args["m_" + n][0] for n in WEIGHTS}
    v2 = {n: args["v_" + n][0] for n in WEIGHTS}
    xi, yi, ci = _me()
    chip = 2 * xi + yi

    conv_slot = jnp.zeros((3, 4, 256), F32).at[:, chip, :].set(w["conv_w"] * jnp.where(ci == 0, 1.0, 0.0))
    pre = jnp.zeros((PACK_ROWS * 128,), F32).at[0:3072].set(conv_slot.reshape(3072))
    conv_full = _allreduce_small(pre.reshape(PACK_ROWS, 128), "conv_w").reshape(-1)[0:3072].reshape(3, 1024)

    full = _gather_weights({n: w[n] for n in COL_SHARDED + ROW_SHARDED})
    wts = _to_local_layouts(full)
    wts["conv_w"] = conv_full
    gains = {n: w[n].reshape(1, -1) for n in SMALL if n != "conv_w"}

    loss_parts, grad_x, gw, gsmall = _local_step(x[0], positions[0], mem[0], loss_target[0], wts, gains)
    gw = _grads_to_reference_layout(gw)

    loss_local = 0.5 * jnp.sum(loss_parts) * (1.0 / D_MODEL)
    flat = jnp.concatenate([gsmall[n].reshape(-1) for n in SMALL] + [loss_local.reshape(1)])
    flat = jnp.pad(flat, (0, PACK_ROWS * 128 - flat.shape[0]))
    tot = _allreduce_small(flat.reshape(PACK_ROWS, 128), "grads").reshape(-1)
    grads = {}
    off = 0
    for n, sz in zip(SMALL, SMALL_SIZES):
        grads[n] = tot[off:off + sz]
        off += sz
    loss = tot[off]
    grads["conv_w"] = lax.dynamic_slice(grads["conv_w"].reshape(3, 1024), (0, chip * 256), (3, 256))
    for n in SMALL:
        grads[n] = grads[n].reshape(w[n].shape)

    for n in COL_SHARDED:
        g = gw[n]
        r, c = g.shape
        grads[n] = _reduce_scatter(g.reshape(r, 4, c // 4).transpose(1, 0, 2), n)
    for n in ROW_SHARDED:
        g = gw[n]
        r, c = g.shape
        grads[n] = _reduce_scatter(g.reshape(4, r // 4, c), n)

    deltas, new_m, new_v = {}, {}, {}
    for n in WEIGHTS:
        shp = w[n].shape
        two_d = (lambda a: a.reshape(1, -1)) if len(shp) == 1 else (lambda a: a)
        d, mn, vn = _adamw(two_d(w[n]), two_d(grads[n]), two_d(m1[n]), two_d(v2[n]), "adamw_" + n)
        deltas[n], new_m[n], new_v[n] = d.reshape(shp), mn.reshape(shp), vn.reshape(shp)

    lead = lambda a: a[None]
    return (loss, grad_x[None], *[lead(grads[n]) for n in WEIGHTS], *[lead(deltas[n]) for n in WEIGHTS],
            *[lead(new_m[n]) for n in WEIGHTS], *[lead(new_v[n]) for n in WEIGHTS])
```

```python
import functools

import numpy as np
import jax
import jax.numpy as jnp
from jax import lax
from jax.experimental import pallas as pl
from jax.experimental.pallas import tpu as pltpu

F32 = jnp.float32
BF16 = jnp.bfloat16
MESH = pl.DeviceIdType.MESH

D_MODEL = 2048
EPS = 1e-6
CHUNK = 64
MLA_HEADS = 16
QK_NOPE = 128
QK_ROPE = 64
HEAD_PAD = 256
MEM_TOKENS = 256
MEM_HEADS = 4
MEM_HEAD_DIM = 256
ROPE_THETA = 10000.0
MLA_SCALE = (QK_NOPE + QK_ROPE) ** -0.5
MEM_SCALE = MEM_HEAD_DIM ** -0.5
NEG = -1e30

ADAM_LR = 0.001
ADAM_B1 = 0.9
ADAM_B2 = 0.999
ADAM_EPS = 1e-08
ADAM_WD = 0.01
ADAM_STEP = 10

NP = 16384
COL_G, COL_Z, COL_C, COL_M, COL_S = 0, 6144, 8192, 12288, 14336

TM_PROJ = 1024
TN_PROJ = 1024
TS_ROW = 512
TS_BWD = 256
TQ_ATT = 1024
TK_ATT = 1024
TR_ATT = 256
VMEM_LIMIT = 56 * 1024 * 1024

NT_DIMS = (((1,), (1,)), ((), ()))
TN_DIMS = (((0,), (0,)), ((), ()))


def _dot(a, b, mode="nn"):
    if mode == "nn":
        return jnp.dot(a, b, preferred_element_type=F32)
    return lax.dot_general(a, b, NT_DIMS if mode == "nt" else TN_DIMS, preferred_element_type=F32)


def _sigmoid(z):
    return 1.0 / (1.0 + jnp.exp(-z))


def _params(sem=None, vmem=VMEM_LIMIT):
    return pltpu.CompilerParams(dimension_semantics=sem, vmem_limit_bytes=vmem)


def _fold8(v):
    r, c = v.shape
    return v.reshape(r // 8, 8, c).sum(axis=0)


def _swap32(t):
    lane = lax.broadcasted_iota(jnp.int32, t.shape, 1)
    return jnp.where(lane < 32, pltpu.roll(t, 96, 1), pltpu.roll(t, 32, 1))


def _rope(t, cs, sg):
    return t * cs + _swap32(t) * sg


def _rope_bwd(d, cs, sg):
    return d * cs + _swap32(d * sg)


def _rms_fwd(xf, g, n):
    r = lax.rsqrt(jnp.sum(xf * xf, axis=-1, keepdims=True) * (1.0 / n) + EPS)
    return xf * r * g


def _rms_bwd(xf, g, dy, n):
    r = lax.rsqrt(jnp.sum(xf * xf, axis=-1, keepdims=True) * (1.0 / n) + EPS)
    xhat = xf * r
    dyg = dy * g
    dx = r * (dyg - xhat * (jnp.sum(dyg * xhat, axis=-1, keepdims=True) * (1.0 / n)))
    return dx, dy * xhat


def _matmul(a, b, mode, out_dtype, name, tm=1024, tn=1024, tk=1024):
    if mode == "nn":
        (m, k), (_, n) = a.shape, b.shape
    elif mode == "nt":
        (m, k), (n, _) = a.shape, b.shape
    else:
        (k, m), (_, n) = a.shape, b.shape
    tm, tn, tk = min(tm, m), min(tn, n), min(tk, k)
    nk = k // tk
    if mode == "tn":
        a_spec = pl.BlockSpec((tk, tm), lambda i, j, kk: (kk, i))
    else:
        a_spec = pl.BlockSpec((tm, tk), lambda i, j, kk: (i, kk))
    if mode == "nt":
        b_spec = pl.BlockSpec((tn, tk), lambda i, j, kk: (j, kk))
    else:
        b_spec = pl.BlockSpec((tk, tn), lambda i, j, kk: (kk, j))

    def body(a_ref, b_ref, o_ref, acc_ref):
        kk = pl.program_id(2)

        @pl.when(kk == 0)
        def _():
            acc_ref[...] = jnp.zeros_like(acc_ref)

        acc_ref[...] += _dot(a_ref[...].astype(BF16), b_ref[...].astype(BF16), mode)

        @pl.when(kk == nk - 1)
        def _():
            o_ref[...] = acc_ref[...].astype(out_dtype)

    return pl.pallas_call(
        body, name=name, grid=(m // tm, n // tn, nk),
        in_specs=[a_spec, b_spec],
        out_specs=pl.BlockSpec((tm, tn), lambda i, j, kk: (i, j)),
        out_shape=jax.ShapeDtypeStruct((m, n), out_dtype),
        scratch_shapes=[pltpu.VMEM((tm, tn), F32)],
        compiler_params=_params(("parallel", "parallel", "arbitrary")),
    )(a, b)


def _proj_fwd(x, g, w_all):
    s = x.shape[0]
    tm, tn = min(TM_PROJ, s), TN_PROJ

    def body(x_ref, g_ref, w_ref, p_ref, h_ref):
        @pl.when(pl.program_id(1) == 0)
        def _():
            h_ref[...] = _rms_fwd(x_ref[...], g_ref[...], D_MODEL).astype(BF16)

        p_ref[...] = _dot(h_ref[...], w_ref[...]).astype(BF16)

    return pl.pallas_call(
        body, name="proj_fwd", grid=(s // tm, NP // tn),
        in_specs=[pl.BlockSpec((tm, D_MODEL), lambda i, j: (i, 0)),
                  pl.BlockSpec((1, D_MODEL), lambda i, j: (0, 0)),
                  pl.BlockSpec((D_MODEL, tn), lambda i, j: (0, j))],
        out_specs=[pl.BlockSpec((tm, tn), lambda i, j: (i, j)),
                   pl.BlockSpec((tm, D_MODEL), lambda i, j: (i, 0))],
        out_shape=[jax.ShapeDtypeStruct((s, NP), BF16), jax.ShapeDtypeStruct((s, D_MODEL), BF16)],
        compiler_params=_params(("parallel", "arbitrary")),
    )(x, g, w_all)


def _shift_rows(v, carry, j, ts, back):
    if back:
        main = pltpu.roll(v, j, 0)
        edge = pltpu.roll(jnp.concatenate([carry, v[:8]], axis=0), j, 0)[8:]
        return jnp.concatenate([edge, main[8:]], axis=0)
    main = pltpu.roll(v, ts - j, 0)
    edge = pltpu.roll(jnp.concatenate([v[ts - 8:], carry], axis=0), 16 - j, 0)[:8]
    return jnp.concatenate([main[:ts - 8], edge], axis=0)


def _conv_fwd(p_all, conv_w):
    s = p_all.shape[0]
    ts = min(TS_ROW, s)
    c0 = COL_C // 1024

    def body(cg_ref, bg_ref, u_ref, z_ref, w_ref, a_ref, co_ref, carry_ref):
        @pl.when(pl.program_id(0) == 0)
        def _():
            carry_ref[...] = jnp.zeros_like(carry_ref)

        p = cg_ref[...].astype(F32) * u_ref[...].astype(F32)
        carry = carry_ref[...]
        co = (w_ref[2:3, :] * p + w_ref[1:2, :] * _shift_rows(p, carry, 1, ts, True)
              + w_ref[0:1, :] * _shift_rows(p, carry, 2, ts, True))
        carry_ref[...] = p[ts - 8:]
        z = z_ref[...].astype(F32)
        a_ref[...] = (bg_ref[...].astype(F32) * co * (z * _sigmoid(z))).astype(BF16)
        co_ref[...] = co.astype(BF16)

    seg = lambda c: pl.BlockSpec((ts, 1024), lambda i: (i, c0 + c))
    return pl.pallas_call(
        body, name="conv_fwd", grid=(s // ts,),
        in_specs=[seg(0), seg(1), seg(2), seg(3), pl.BlockSpec((3, 1024), lambda i: (0, 0))],
        out_specs=[pl.BlockSpec((ts, 1024), lambda i: (i, 0))] * 2,
        out_shape=[jax.ShapeDtypeStruct((s, 1024), BF16)] * 2,
        scratch_shapes=[pltpu.VMEM((8, 1024), F32)],
        compiler_params=_params(("arbitrary",)),
    )(p_all, p_all, p_all, p_all, conv_w)


def _lora_fwd(p_all, gq, gkv, gkr, cs, sg):
    s = p_all.shape[0]
    ts = min(TS_ROW, s)

    def body(cq_ref, ckv_ref, kr_ref, gq_ref, gkv_ref, gkr_ref, cs_ref, sg_ref, cqn_ref, ckvn_ref, krope_ref):
        cqn_ref[...] = _rms_fwd(cq_ref[...].astype(F32), gq_ref[...], 512).astype(BF16)
        ckvn_ref[...] = _rms_fwd(ckv_ref[...].astype(F32), gkv_ref[...], 512).astype(BF16)
        kn = _rms_fwd(kr_ref[...].astype(F32), gkr_ref[...], QK_ROPE)
        krope_ref[...] = _rope(kn, cs_ref[...], sg_ref[...]).astype(BF16)

    row = lambda w: pl.BlockSpec((ts, w), lambda i: (i, 0))
    vec = lambda w: pl.BlockSpec((1, w), lambda i: (0, 0))
    return pl.pallas_call(
        body, name="lora_fwd", grid=(s // ts,),
        in_specs=[pl.BlockSpec((ts, 512), lambda i: (i, COL_S // 512)),
                  pl.BlockSpec((ts, 512), lambda i: (i, COL_S // 512 + 1)),
                  pl.BlockSpec((ts, 128), lambda i: (i, (COL_S + 1024) // 128)),
                  vec(512), vec(512), vec(128), row(128), row(128)],
        out_specs=[row(512), row(512), row(128)],
        out_shape=[jax.ShapeDtypeStruct((s, 512), BF16), jax.ShapeDtypeStruct((s, 512), BF16),
                   jax.ShapeDtypeStruct((s, 128), BF16)],
        compiler_params=_params(("parallel",)),
    )(p_all, p_all, p_all, gq, gkv, gkr, cs, sg)


def _q_prep(q_raw, gn, gr, cs, sg):
    s = q_raw.shape[0]
    ts = min(TS_ROW, s)

    def body(q_ref, gn_ref, gr_ref, cs_ref, sg_ref, o_ref):
        for h in range(MLA_HEADS):
            q = q_ref[:, h * HEAD_PAD:(h + 1) * HEAD_PAD].astype(F32)
            a = _rms_fwd(q[:, :128], gn_ref[...], QK_NOPE)
            b = _rope(_rms_fwd(q[:, 128:], gr_ref[...], QK_ROPE), cs_ref[...], sg_ref[...])
            o_ref[h] = jnp.concatenate([a, b], axis=1).astype(BF16)

    return pl.pallas_call(
        body, name="q_prep", grid=(s // ts,),
        in_specs=[pl.BlockSpec((ts, MLA_HEADS * HEAD_PAD), lambda i: (i, 0)),
                  pl.BlockSpec((1, 128), lambda i: (0, 0)), pl.BlockSpec((1, 128), lambda i: (0, 0)),
                  pl.BlockSpec((ts, 128), lambda i: (i, 0)), pl.BlockSpec((ts, 128), lambda i: (i, 0))],
        out_specs=pl.BlockSpec((MLA_HEADS, ts, HEAD_PAD), lambda i: (0, i, 0)),
        out_shape=jax.ShapeDtypeStruct((MLA_HEADS, s, HEAD_PAD), BF16),
        compiler_params=_params(("parallel",)),
    )(q_raw, gn, gr, cs, sg)


def _k_prep(kv, krope, gk):
    s = kv.shape[0]
    ts = min(TS_ROW, s)

    def body(k_ref, kr_ref, gk_ref, o_ref):
        for h in range(MLA_HEADS):
            a = _rms_fwd(k_ref[:, h * HEAD_PAD:h * HEAD_PAD + 128].astype(F32), gk_ref[...], QK_NOPE)
            o_ref[h] = jnp.concatenate([a.astype(BF16), kr_ref[...]], axis=1)

    return pl.pallas_call(
        body, name="k_prep", grid=(s // ts,),
        in_specs=[pl.BlockSpec((ts, MLA_HEADS * HEAD_PAD), lambda i: (i, 0)),
                  pl.BlockSpec((ts, 128), lambda i: (i, 0)),
                  pl.BlockSpec((1, 128), lambda i: (0, 0))],
        out_specs=pl.BlockSpec((MLA_HEADS, ts, HEAD_PAD), lambda i: (0, i, 0)),
        out_shape=jax.ShapeDtypeStruct((MLA_HEADS, s, HEAD_PAD), BF16),
        compiler_params=_params(("parallel",)),
    )(kv, krope, gk)


def _chunk_mask(tq, tk, row0=0):
    r = (lax.broadcasted_iota(jnp.int32, (tq, tk), 0) + row0) // CHUNK
    c = lax.broadcasted_iota(jnp.int32, (tq, tk), 1) // CHUNK
    return c <= r


def _flash_fwd(qcat, kcat, kv, p_all):
    s = qcat.shape[1]
    tq = tk = min(TQ_ATT, s)
    tr = min(TR_ATT, tq)
    nq = s // tq
    zc = COL_Z // 128

    def body(q_ref, k_ref, v_ref, z_ref, y_ref, a_ref, lse_ref, m_sc, l_sc, acc_sc):
        i, j = pl.program_id(1), pl.program_id(2)

        @pl.when(j == 0)
        def _():
            m_sc[...] = jnp.full_like(m_sc, NEG)
            l_sc[...] = jnp.zeros_like(l_sc)
            acc_sc[...] = jnp.zeros_like(acc_sc)

        def step(masked):
            for r in range(tq // tr):
                rows = slice(r * tr, (r + 1) * tr)
                nkeys = (r + 1) * tr if masked else tk
                sc = _dot(q_ref[0, rows, :], k_ref[0, :nkeys, :], "nt") * MLA_SCALE
                if masked:
                    sc = jnp.where(_chunk_mask(tr, nkeys, r * tr), sc, NEG)
                m_prev = m_sc[rows]
                m_new = jnp.maximum(m_prev, jnp.max(sc, axis=-1, keepdims=True))
                alpha = jnp.exp(m_prev - m_new)
                p = jnp.exp(sc - m_new)
                l_sc[rows] = alpha * l_sc[rows] + jnp.sum(p, axis=-1, keepdims=True)
                acc_sc[rows] = alpha * acc_sc[rows] + _dot(p.astype(BF16), v_ref[:nkeys, :])
                m_sc[rows] = m_new

        @pl.when(j < i)
        def _():
            step(False)

        @pl.when(j == i)
        def _():
            step(True)
            y = acc_sc[...] * (1.0 / l_sc[...])
            z = z_ref[...].astype(F32)
            y_ref[...] = y.astype(BF16)
            a_ref[...] = (y * (z * _sigmoid(z))).astype(BF16)
            lse_ref[0] = jnp.broadcast_to(m_sc[...] + jnp.log(l_sc[...]), (tq, 128))

    return pl.pallas_call(
        body, name="flash_fwd", grid=(MLA_HEADS, nq, nq),
        in_specs=[pl.BlockSpec((1, tq, HEAD_PAD), lambda h, i, j: (h, i, 0)),
                  pl.BlockSpec((1, tk, HEAD_PAD), lambda h, i, j: (h, jnp.minimum(i, j), 0)),
                  pl.BlockSpec((tk, 128), lambda h, i, j: (jnp.minimum(i, j), 2 * h + 1)),
                  pl.BlockSpec((tq, 128), lambda h, i, j: (i, zc + h))],
        out_specs=[pl.BlockSpec((tq, 128), lambda h, i, j: (i, h)),
                   pl.BlockSpec((tq, 128), lambda h, i, j: (i, h)),
                   pl.BlockSpec((1, tq, 128), lambda h, i, j: (h, i, 0))],
        out_shape=[jax.ShapeDtypeStruct((s, MLA_HEADS * 128), BF16),
                   jax.ShapeDtypeStruct((s, MLA_HEADS * 128), BF16),
                   jax.ShapeDtypeStruct((MLA_HEADS, s, 128), F32)],
        scratch_shapes=[pltpu.VMEM((tq, 1), F32), pltpu.VMEM((tq, 1), F32), pltpu.VMEM((tq, 128), F32)],
        compiler_params=_params(("parallel", "parallel", "arbitrary")),
    )(qcat, kcat, kv, p_all)


def _mem_kv_fwd(mem, g_norm, w_kv, g_kn):
    m = mem.shape[0]

    def body(mem_ref, g_ref, w_ref, gk_ref, memn_ref, kraw_ref, kn_ref, v_ref):
        memn = _rms_fwd(mem_ref[...], g_ref[...], D_MODEL).astype(BF16)
        memn_ref[...] = memn
        kvm = _dot(memn, w_ref[...])
        kraw_ref[...] = kvm[:, :1024]
        v_ref[...] = kvm[:, 1024:].astype(BF16)
        for hh in range(MEM_HEADS):
            sl = slice(hh * MEM_HEAD_DIM, (hh + 1) * MEM_HEAD_DIM)
            kn_ref[:, sl] = _rms_fwd(kvm[:, sl], gk_ref[...], MEM_HEAD_DIM).astype(BF16)

    return pl.pallas_call(
        body, name="mem_kv_fwd",
        out_shape=[jax.ShapeDtypeStruct((m, D_MODEL), BF16), jax.ShapeDtypeStruct((m, 1024), F32),
                   jax.ShapeDtypeStruct((m, 1024), BF16), jax.ShapeDtypeStruct((m, 1024), BF16)],
        compiler_params=_params(),
    )(mem, g_norm, w_kv, g_kn)


def _mem_attn_fwd(p_all, kn, v, gq):
    s = p_all.shape[0]
    tq = min(TS_ROW, s)

    def body(q_ref, z_ref, kn_ref, v_ref, gq_ref, a_ref):
        z = z_ref[...].astype(F32)
        gate = z * _sigmoid(z)
        for hh in range(MEM_HEADS):
            sl = slice(hh * MEM_HEAD_DIM, (hh + 1) * MEM_HEAD_DIM)
            qn = _rms_fwd(q_ref[:, sl].astype(F32), gq_ref[...], MEM_HEAD_DIM).astype(BF16)
            sc = _dot(qn, kn_ref[:, sl], "nt") * MEM_SCALE
            e = jnp.exp(sc - jnp.max(sc, axis=-1, keepdims=True))
            p = e * (1.0 / jnp.sum(e, axis=-1, keepdims=True))
            y = _dot(p.astype(BF16), v_ref[:, sl])
            a_ref[:, sl] = (y * gate[:, sl]).astype(BF16)

    full = lambda a: pl.BlockSpec(a.shape, lambda i: (0, 0))
    return pl.pallas_call(
        body, name="mem_attn_fwd", grid=(s // tq,),
        in_specs=[pl.BlockSpec((tq, 1024), lambda i: (i, COL_M // 1024)),
                  pl.BlockSpec((tq, 1024), lambda i: (i, COL_M // 1024 + 1)),
                  full(kn), full(v), full(gq)],
        out_specs=pl.BlockSpec((tq, 1024), lambda i: (i, 0)),
        out_shape=jax.ShapeDtypeStruct((s, 1024), BF16),
        compiler_params=_params(("parallel",)),
    )(p_all, p_all, kn, v, gq)


def _merge_fwd(a_conv, w_c, a_mla, w_m, a_mem, w_e, p_all):
    s = a_conv.shape[0]
    tm, tn = min(TS_ROW, s), 1024
    nj = D_MODEL // tn

    def body(ac_ref, wc_ref, am_ref, wm_ref, ae_ref, we_ref, gc_ref, gm_ref, ge_ref,
             oc_ref, om_ref, oe_ref, mg_ref):
        oc = _dot(ac_ref[...], wc_ref[...])
        om = _dot(am_ref[...], wm_ref[...])
        oe = _dot(ae_ref[...], we_ref[...])
        oc_ref[...] = oc.astype(BF16)
        om_ref[...] = om.astype(BF16)
        oe_ref[...] = oe.astype(BF16)
        mg_ref[...] = (_sigmoid(gc_ref[...].astype(F32)) * oc + _sigmoid(gm_ref[...].astype(F32)) * om
                       + _sigmoid(ge_ref[...].astype(F32)) * oe).astype(BF16)

    act = lambda k: pl.BlockSpec((tm, k), lambda i, j: (i, 0))
    wgt = lambda k: pl.BlockSpec((k, tn), lambda i, j: (0, j))
    gate = lambda b: pl.BlockSpec((tm, tn), lambda i, j: (i, b * nj + j))
    out = pl.BlockSpec((tm, tn), lambda i, j: (i, j))
    return pl.pallas_call(
        body, name="merge_fwd", grid=(s // tm, nj),
        in_specs=[act(1024), wgt(1024), act(2048), wgt(2048), act(1024), wgt(1024), gate(0), gate(1), gate(2)],
        out_specs=[out] * 4,
        out_shape=[jax.ShapeDtypeStruct((s, D_MODEL), BF16)] * 4,
        compiler_params=_params(("parallel", "parallel")),
    )(a_conv, w_c, a_mla, w_m, a_mem, w_e, p_all, p_all, p_all)


def _out_fwd(merged, w_o, x, target):
    s = merged.shape[0]
    tm, tn = min(TS_ROW, s), 1024
    nj = D_MODEL // tn

    def body(mg_ref, w_ref, x_ref, t_ref, dy_ref, dyb_ref, ls_ref):
        e = x_ref[...] + _dot(mg_ref[...], w_ref[...]) - t_ref[...]
        dy = e * (1.0 / D_MODEL)
        dy_ref[...] = dy
        dyb_ref[...] = dy.astype(BF16)
        r = _fold8(e * e)
        acc = r[:, 0:128]
        for cc in range(1, tn // 128):
            acc = acc + r[:, cc * 128:(cc + 1) * 128]
        ls_ref[...] = acc

    tile = pl.BlockSpec((tm, tn), lambda i, j: (i, j))
    return pl.pallas_call(
        body, name="out_fwd", grid=(s // tm, nj),
        in_specs=[pl.BlockSpec((tm, D_MODEL), lambda i, j: (i, 0)),
                  pl.BlockSpec((D_MODEL, tn), lambda i, j: (0, j)), tile, tile],
        out_specs=[tile, tile, pl.BlockSpec((8, 128), lambda i, j: (i, j))],
        out_shape=[jax.ShapeDtypeStruct((s, D_MODEL), F32), jax.ShapeDtypeStruct((s, D_MODEL), BF16),
                   jax.ShapeDtypeStruct((s // tm * 8, nj * 128), F32)],
        compiler_params=_params(("parallel", "parallel")),
    )(merged, w_o, x, target)


def _merge_bwd(dyb, w_o, p_all, o_c, o_m, o_e):
    s = dyb.shape[0]
    tm = min(256, s)

    def body(dy_ref, w_ref, g_ref, oc_ref, om_ref, oe_ref, dc_ref, dm_ref, de_ref, dg_ref):
        dmg = _dot(dy_ref[...], w_ref[...], "nt")
        for b, (o_ref, d_ref) in enumerate(((oc_ref, dc_ref), (om_ref, dm_ref), (oe_ref, de_ref))):
            sl = slice(b * D_MODEL, (b + 1) * D_MODEL)
            sg = _sigmoid(g_ref[:, sl].astype(F32))
            d_ref[...] = (dmg * sg).astype(BF16)
            dg_ref[:, sl] = (dmg * o_ref[...].astype(F32) * sg * (1.0 - sg)).astype(BF16)

    row = pl.BlockSpec((tm, D_MODEL), lambda i: (i, 0))
    wide = pl.BlockSpec((tm, 3 * D_MODEL), lambda i: (i, 0))
    return pl.pallas_call(
        body, name="merge_bwd", grid=(s // tm,),
        in_specs=[row, pl.BlockSpec((D_MODEL, D_MODEL), lambda i: (0, 0)), wide, row, row, row],
        out_specs=[row, row, row, wide],
        out_shape=[jax.ShapeDtypeStruct((s, D_MODEL), BF16)] * 3 + [jax.ShapeDtypeStruct((s, 3 * D_MODEL), BF16)],
        compiler_params=_params(("parallel",)),
    )(dyb, w_o, p_all, o_c, o_m, o_e)


def _conv_bwd(da, p_all, co, conv_w):
    s = da.shape[0]
    ts = min(TS_ROW, s)
    n = s // ts
    c0 = COL_C // 1024

    def body(da_ref, cg_ref, bg_ref, u_ref, z_ref, co_ref, w_ref, dp_ref, dw_ref, carry_ref):
        @pl.when(pl.program_id(0) == 0)
        def _():
            carry_ref[...] = jnp.zeros_like(carry_ref)
            dw_ref[...] = jnp.zeros_like(dw_ref)

        da_ = da_ref[...].astype(F32)
        cg, bg = cg_ref[...].astype(F32), bg_ref[...].astype(F32)
        u, z, cov = u_ref[...].astype(F32), z_ref[...].astype(F32), co_ref[...].astype(F32)
        sz = _sigmoid(z)
        dyc = da_ * (z * sz)
        dz = da_ * (bg * cov) * (sz * (1.0 + z * (1.0 - sz)))
        db = dyc * cov
        dco = dyc * bg
        carry = carry_ref[...]
        d1 = _shift_rows(dco, carry, 1, ts, False)
        d2 = _shift_rows(dco, carry, 2, ts, False)
        carry_ref[...] = dco[:8]
        dpp = w_ref[2:3, :] * dco + w_ref[1:2, :] * d1 + w_ref[0:1, :] * d2
        p = cg * u
        dw_ref[0] += _fold8(p * d2)
        dw_ref[1] += _fold8(p * d1)
        dw_ref[2] += _fold8(p * dco)
        dp_ref[...] = jnp.concatenate([dpp * u, db, dpp * cg, dz], axis=1).astype(BF16)

    rev = lambda c: pl.BlockSpec((ts, 1024), lambda i: (n - 1 - i, c))
    return pl.pallas_call(
        body, name="conv_bwd", grid=(n,),
        in_specs=[rev(0), rev(c0), rev(c0 + 1), rev(c0 + 2), rev(c0 + 3), rev(0),
                  pl.BlockSpec((3, 1024), lambda i: (0, 0))],
        out_specs=[pl.BlockSpec((ts, 4096), lambda i: (n - 1 - i, 0)),
                   pl.BlockSpec((3, 8, 1024), lambda i: (0, 0, 0))],
        out_shape=[jax.ShapeDtypeStruct((s, 4096), BF16), jax.ShapeDtypeStruct((3, 8, 1024), F32)],
        scratch_shapes=[pltpu.VMEM((8, 1024), F32)],
        compiler_params=_params(("arbitrary",)),
    )(da, p_all, p_all, p_all, p_all, co, conv_w)


def _mla_gate_bwd(da, y, p_all):
    s = da.shape[0]
    ts = min(TS_ROW, s)

    def body(da_ref, y_ref, z_ref, dy_ref, dz_ref, dl_ref):
        da_, yv, z = da_ref[...].astype(F32), y_ref[...].astype(F32), z_ref[...].astype(F32)
        sz = _sigmoid(z)
        dyv = da_ * (z * sz)
        dy_ref[...] = dyv.astype(BF16)
        dz_ref[...] = (da_ * yv * (sz * (1.0 + z * (1.0 - sz)))).astype(BF16)
        pr = dyv * yv
        for h in range(MLA_HEADS):
            dl_ref[h] = jnp.broadcast_to(jnp.sum(pr[:, h * 128:(h + 1) * 128], axis=-1, keepdims=True), (ts, 128))

    row = pl.BlockSpec((ts, D_MODEL), lambda i: (i, 0))
    return pl.pallas_call(
        body, name="mla_gate_bwd", grid=(s // ts,),
        in_specs=[row, row, pl.BlockSpec((ts, D_MODEL), lambda i: (i, COL_Z // D_MODEL))],
        out_specs=[row, row, pl.BlockSpec((MLA_HEADS, ts, 128), lambda i: (0, i, 0))],
        out_shape=[jax.ShapeDtypeStruct((s, D_MODEL), BF16)] * 2 + [jax.ShapeDtypeStruct((MLA_HEADS, s, 128), F32)],
        compiler_params=_params(("parallel",)),
    )(da, y, p_all)


def _flash_bwd(qcat, kcat, kv, dy, lse, delta):
    s = qcat.shape[1]
    tq = tk = min(TQ_ATT, s)
    nq = s // tq

    def body(q_ref, k_ref, v_ref, do_ref, lse_ref, dl_ref, dq_ref, dk_ref, dv_ref):
        j, i = pl.program_id(1), pl.program_id(2)

        def step(masked):
            q, k = q_ref[0], k_ref[0]
            sc = _dot(q, k, "nt") * MLA_SCALE
            if masked:
                sc = jnp.where(_chunk_mask(tq, tk), sc, NEG)
            p = jnp.exp(sc - lse_ref[0][:, 0:1])
            do = do_ref[...]
            dpv = _dot(do, v_ref[...], "nt")
            ds = (p * (dpv - dl_ref[0][:, 0:1]) * MLA_SCALE).astype(BF16)
            pb = p.astype(BF16)
            dv_new = _dot(pb, do, "tn")
            dk_new = _dot(ds, q, "tn")
            dq_new = _dot(ds, k)
            rows = pl.ds(pl.multiple_of(i * tq, tq), tq)
            if masked:
                dv_ref[0] = dv_new
                dk_ref[0] = dk_new
            else:
                dv_ref[0] += dv_new
                dk_ref[0] += dk_new

            @pl.when(j == 0)
            def _():
                dq_ref[0, rows, :] = dq_new

            @pl.when(j > 0)
            def _():
                dq_ref[0, rows, :] += dq_new

        @pl.when(i == j)
        def _():
            step(True)

        @pl.when(i > j)
        def _():
            step(False)

    qrow = lambda w: pl.BlockSpec((1, tq, w), lambda h, j, i: (h, jnp.maximum(i, j), 0))
    krow = lambda w: pl.BlockSpec((1, tk, w), lambda h, j, i: (h, j, 0))
    return pl.pallas_call(
        body, name="flash_bwd", grid=(MLA_HEADS, nq, nq),
        in_specs=[qrow(HEAD_PAD), krow(HEAD_PAD),
                  pl.BlockSpec((tk, 128), lambda h, j, i: (j, 2 * h + 1)),
                  pl.BlockSpec((tq, 128), lambda h, j, i: (jnp.maximum(i, j), h)),
                  qrow(128), qrow(128)],
        out_specs=[pl.BlockSpec((1, s, HEAD_PAD), lambda h, j, i: (h, 0, 0)), krow(HEAD_PAD), krow(128)],
        out_shape=[jax.ShapeDtypeStruct((MLA_HEADS, s, HEAD_PAD), F32),
                   jax.ShapeDtypeStruct((MLA_HEADS, s, HEAD_PAD), F32),
                   jax.ShapeDtypeStruct((MLA_HEADS, s, 128), F32)],
        compiler_params=_params(("parallel", "arbitrary", "arbitrary")),
    )(qcat, kcat, kv, dy, lse, delta)


def _q_prep_bwd(dqcat, q_raw, gn, gr, cs, sg):
    s = q_raw.shape[0]
    ts = min(TS_BWD, s)

    def body(dq_ref, q_ref, gn_ref, gr_ref, cs_ref, sg_ref, o_ref, dg_ref):
        @pl.when(pl.program_id(0) == 0)
        def _():
            dg_ref[...] = jnp.zeros_like(dg_ref)

        ga_acc = jnp.zeros((8, 128), F32)
        gb_acc = jnp.zeros((8, 128), F32)
        for h in range(MLA_HEADS):
            cols = slice(h * HEAD_PAD, (h + 1) * HEAD_PAD)
            q = q_ref[:, cols].astype(F32)
            d = dq_ref[h]
            da, ga = _rms_bwd(q[:, :128], gn_ref[...], d[:, :128], QK_NOPE)
            db, gb = _rms_bwd(q[:, 128:], gr_ref[...], _rope_bwd(d[:, 128:], cs_ref[...], sg_ref[...]), QK_ROPE)
            o_ref[:, cols] = jnp.concatenate([da, db], axis=1).astype(BF16)
            ga_acc = ga_acc + _fold8(ga)
            gb_acc = gb_acc + _fold8(gb)
        dg_ref[0] += ga_acc
        dg_ref[1] += gb_acc

    wide = pl.BlockSpec((ts, MLA_HEADS * HEAD_PAD), lambda i: (i, 0))
    return pl.pallas_call(
        body, name="q_prep_bwd", grid=(s // ts,),
        in_specs=[pl.BlockSpec((MLA_HEADS, ts, HEAD_PAD), lambda i: (0, i, 0)), wide,
                  pl.BlockSpec((1, 128), lambda i: (0, 0)), pl.BlockSpec((1, 128), lambda i: (0, 0)),
                  pl.BlockSpec((ts, 128), lambda i: (i, 0)), pl.BlockSpec((ts, 128), lambda i: (i, 0))],
        out_specs=[wide, pl.BlockSpec((2, 8, 128), lambda i: (0, 0, 0))],
        out_shape=[jax.ShapeDtypeStruct((s, MLA_HEADS * HEAD_PAD), BF16), jax.ShapeDtypeStruct((2, 8, 128), F32)],
        compiler_params=_params(("arbitrary",)),
    )(dqcat, q_raw, gn, gr, cs, sg)


def _k_prep_bwd(dkcat, dv, kv, gk):
    s = kv.shape[0]
    ts = min(TS_BWD, s)

    def body(dk_ref, dv_ref, k_ref, gk_ref, o_ref, dkr_ref, dg_ref):
        @pl.when(pl.program_id(0) == 0)
        def _():
            dg_ref[...] = jnp.zeros_like(dg_ref)

        g_acc = jnp.zeros((8, 128), F32)
        dkr = jnp.zeros((ts, 128), F32)
        for h in range(MLA_HEADS):
            d = dk_ref[h]
            k = k_ref[:, h * HEAD_PAD:h * HEAD_PAD + 128].astype(F32)
            dk_raw, gg = _rms_bwd(k, gk_ref[...], d[:, :128], QK_NOPE)
            o_ref[:, h * HEAD_PAD:(h + 1) * HEAD_PAD] = jnp.concatenate([dk_raw, dv_ref[h]], axis=1).astype(BF16)
            g_acc = g_acc + _fold8(gg)
            dkr = dkr + d[:, 128:]
        dkr_ref[...] = dkr
        dg_ref[...] += g_acc

    wide = pl.BlockSpec((ts, MLA_HEADS * HEAD_PAD), lambda i: (i, 0))
    return pl.pallas_call(
        body, name="k_prep_bwd", grid=(s // ts,),
        in_specs=[pl.BlockSpec((MLA_HEADS, ts, HEAD_PAD), lambda i: (0, i, 0)),
                  pl.BlockSpec((MLA_HEADS, ts, 128), lambda i: (0, i, 0)), wide,
                  pl.BlockSpec((1, 128), lambda i: (0, 0))],
        out_specs=[wide, pl.BlockSpec((ts, 128), lambda i: (i, 0)), pl.BlockSpec((8, 128), lambda i: (0, 0))],
        out_shape=[jax.ShapeDtypeStruct((s, MLA_HEADS * HEAD_PAD), BF16), jax.ShapeDtypeStruct((s, 128), F32),
                   jax.ShapeDtypeStruct((8, 128), F32)],
        compiler_params=_params(("arbitrary",)),
    )(dkcat, dv, kv, gk)


def _lora_bwd(dcqn, dckvn, dkr, p_all, gq, gkv, gkr, cs, sg):
    s = p_all.shape[0]
    ts = min(TS_ROW, s)

    def body(dq_ref, dkv_ref, dkr_ref, cq_ref, ckv_ref, kr_ref, gq_ref, gkv_ref, gkr_ref, cs_ref, sg_ref,
             o_ref, dgq_ref, dgkv_ref, dgkr_ref):
        @pl.when(pl.program_id(0) == 0)
        def _():
            dgq_ref[...] = jnp.zeros_like(dgq_ref)
            dgkv_ref[...] = jnp.zeros_like(dgkv_ref)
            dgkr_ref[...] = jnp.zeros_like(dgkr_ref)

        dq, g1 = _rms_bwd(cq_ref[...].astype(F32), gq_ref[...], dq_ref[...], 512)
        dkv, g2 = _rms_bwd(ckv_ref[...].astype(F32), gkv_ref[...], dkv_ref[...], 512)
        dkn = _rope_bwd(dkr_ref[...], cs_ref[...], sg_ref[...])
        dk, g3 = _rms_bwd(kr_ref[...].astype(F32), gkr_ref[...], dkn, QK_ROPE)
        o_ref[...] = jnp.concatenate([dq, dkv, dk, jnp.zeros((ts, 2048 - 1152), F32)], axis=1).astype(BF16)
        dgq_ref[...] += _fold8(g1)
        dgkv_ref[...] += _fold8(g2)
        dgkr_ref[...] += _fold8(g3)

    row = lambda w: pl.BlockSpec((ts, w), lambda i: (i, 0))
    vec = lambda w: pl.BlockSpec((1, w), lambda i: (0, 0))
    acc = lambda w: pl.BlockSpec((8, w), lambda i: (0, 0))
    return pl.pallas_call(
        body, name="lora_bwd", grid=(s // ts,),
        in_specs=[row(512), row(512), row(128),
                  pl.BlockSpec((ts, 512), lambda i: (i, COL_S // 512)),
                  pl.BlockSpec((ts, 512), lambda i: (i, COL_S // 512 + 1)),
                  pl.BlockSpec((ts, 128), lambda i: (i, (COL_S + 1024) // 128)),
                  vec(512), vec(512), vec(128), row(128), row(128)],
        out_specs=[row(2048), acc(512), acc(512), acc(128)],
        out_shape=[jax.ShapeDtypeStruct((s, 2048), BF16), jax.ShapeDtypeStruct((8, 512), F32),
                   jax.ShapeDtypeStruct((8, 512), F32), jax.ShapeDtypeStruct((8, 128), F32)],
        compiler_params=_params(("arbitrary",)),
    )(dcqn, dckvn, dkr, p_all, p_all, p_all, gq, gkv, gkr, cs, sg)


def _mem_attn_bwd(da, p_all, kn, v, gq):
    s = p_all.shape[0]
    tq = min(TS_ROW, s)

    def body(da_ref, q_ref, z_ref, kn_ref, v_ref, gq_ref, o_ref, dkn_ref, dv_ref, dg_ref):
        @pl.when(pl.program_id(0) == 0)
        def _():
            dkn_ref[...] = jnp.zeros_like(dkn_ref)
            dv_ref[...] = jnp.zeros_like(dv_ref)
            dg_ref[...] = jnp.zeros_like(dg_ref)

        z = z_ref[...].astype(F32)
        da_ = da_ref[...].astype(F32)
        sz = _sigmoid(z)
        gate = z * sz
        dgate = sz * (1.0 + z * (1.0 - sz))
        for hh in range(MEM_HEADS):
            sl = slice(hh * MEM_HEAD_DIM, (hh + 1) * MEM_HEAD_DIM)
            qf = q_ref[:, sl].astype(F32)
            qn = _rms_fwd(qf, gq_ref[...], MEM_HEAD_DIM).astype(BF16)
            knh, vh = kn_ref[:, sl], v_ref[:, sl]
            sc = _dot(qn, knh, "nt") * MEM_SCALE
            e = jnp.exp(sc - jnp.max(sc, axis=-1, keepdims=True))
            p = e * (1.0 / jnp.sum(e, axis=-1, keepdims=True))
            pb = p.astype(BF16)
            y = _dot(pb, vh)
            dyh = da_[:, sl] * gate[:, sl]
            o_ref[:, 1024 + hh * MEM_HEAD_DIM:1024 + (hh + 1) * MEM_HEAD_DIM] = (
                da_[:, sl] * y * dgate[:, sl]).astype(BF16)
            dyb = dyh.astype(BF16)
            dpm = _dot(dyb, vh, "nt")
            ds = (p * (dpm - jnp.sum(dpm * p, axis=-1, keepdims=True)) * MEM_SCALE).astype(BF16)
            dqn = _dot(ds, knh)
            dkn_ref[:, sl] += _dot(ds, qn, "tn")
            dv_ref[:, sl] += _dot(pb, dyb, "tn")
            dq, gg = _rms_bwd(qf, gq_ref[...], dqn, MEM_HEAD_DIM)
            o_ref[:, sl] = dq.astype(BF16)
            dg_ref[...] += _fold8(gg)

    full = lambda a: pl.BlockSpec(a.shape, lambda i: (0, 0))
    return pl.pallas_call(
        body, name="mem_attn_bwd", grid=(s // tq,),
        in_specs=[pl.BlockSpec((tq, 1024), lambda i: (i, 0)),
                  pl.BlockSpec((tq, 1024), lambda i: (i, COL_M // 1024)),
                  pl.BlockSpec((tq, 1024), lambda i: (i, COL_M // 1024 + 1)),
                  full(kn), full(v), full(gq)],
        out_specs=[pl.BlockSpec((tq, 2048), lambda i: (i, 0)),
                   pl.BlockSpec((MEM_TOKENS, 1024), lambda i: (0, 0)),
                   pl.BlockSpec((MEM_TOKENS, 1024), lambda i: (0, 0)),
                   pl.BlockSpec((8, MEM_HEAD_DIM), lambda i: (0, 0))],
        out_shape=[jax.ShapeDtypeStruct((s, 2048), BF16), jax.ShapeDtypeStruct((MEM_TOKENS, 1024), F32),
                   jax.ShapeDtypeStruct((MEM_TOKENS, 1024), F32), jax.ShapeDtypeStruct((8, MEM_HEAD_DIM), F32)],
        compiler_params=_params(("arbitrary",)),
    )(da, p_all, p_all, kn, v, gq)


def _mem_kv_bwd(dkn, dv, kraw, memn, mem, g_norm, w_kv, g_kn):
    m = mem.shape[0]

    def body(dkn_ref, dv_ref, kraw_ref, memn_ref, mem_ref, g_ref, w_ref, gk_ref, dw_ref, dgn_ref, dgk_ref, dkv_sc):
        gk_acc = jnp.zeros((8, MEM_HEAD_DIM), F32)
        for hh in range(MEM_HEADS):
            sl = slice(hh * MEM_HEAD_DIM, (hh + 1) * MEM_HEAD_DIM)
            dk, gg = _rms_bwd(kraw_ref[:, sl], gk_ref[...], dkn_ref[:, sl], MEM_HEAD_DIM)
            dkv_sc[:, sl] = dk.astype(BF16)
            gk_acc = gk_acc + _fold8(gg)
        dgk_ref[...] = gk_acc
        dkv_sc[:, 1024:] = dv_ref[...].astype(BF16)
        dkv = dkv_sc[...]
        dw_ref[...] = _dot(memn_ref[...], dkv, "tn")
        dmemn = _dot(dkv, w_ref[...], "nt")
        xf = mem_ref[...]
        r = lax.rsqrt(jnp.mean(xf * xf, axis=-1, keepdims=True) + EPS)
        dgn_ref[...] = _fold8(dmemn * (xf * r))

    return pl.pallas_call(
        body, name="mem_kv_bwd",
        out_shape=[jax.ShapeDtypeStruct((D_MODEL, D_MODEL), F32), jax.ShapeDtypeStruct((8, D_MODEL), F32),
                   jax.ShapeDtypeStruct((8, MEM_HEAD_DIM), F32)],
        scratch_shapes=[pltpu.VMEM((m, D_MODEL), BF16)],
        compiler_params=_params(),
    )(dkn, dv, kraw, memn, mem, g_norm, w_kv, g_kn)


def _dh_bwd(dps, w_all):
    s = dps[0].shape[0]
    tm, tk = min(TS_ROW, s), 1024
    widths = [a.shape[1] // tk for a in dps]
    starts = [int(v) for v in np.cumsum([0] + widths[:-1])]
    nk = sum(widths)

    def body(*refs):
        d_refs, w_ref, o_ref, acc_ref = refs[:5], refs[5], refs[6], refs[7]
        kk = pl.program_id(1)

        @pl.when(kk == 0)
        def _():
            acc_ref[...] = jnp.zeros_like(acc_ref)

        for d_ref, st, wd in zip(d_refs, starts, widths):
            @pl.when((kk >= st) & (kk < st + wd))
            def _(d_ref=d_ref):
                acc_ref[...] += _dot(d_ref[...], w_ref[...], "nt")

        @pl.when(kk == nk - 1)
        def _():
            o_ref[...] = acc_ref[...]

    def dspec(st, wd):
        return pl.BlockSpec((tm, tk), lambda i, kk: (i, jnp.clip(kk - st, 0, wd - 1)))

    return pl.pallas_call(
        body, name="dh_bwd", grid=(s // tm, nk),
        in_specs=[dspec(st, wd) for st, wd in zip(starts, widths)]
        + [pl.BlockSpec((D_MODEL, tk), lambda i, kk: (0, kk))],
        out_specs=pl.BlockSpec((tm, D_MODEL), lambda i, kk: (i, 0)),
        out_shape=jax.ShapeDtypeStruct((s, D_MODEL), F32),
        scratch_shapes=[pltpu.VMEM((tm, D_MODEL), F32)],
        compiler_params=_params(("parallel", "arbitrary")),
    )(*dps, w_all)


def _dx_bwd(dh, x, dy, g):
    s = x.shape[0]
    ts = min(256, s)

    def body(dh_ref, x_ref, dy_ref, g_ref, o_ref, dg_ref):
        @pl.when(pl.program_id(0) == 0)
        def _():
            dg_ref[...] = jnp.zeros_like(dg_ref)

        dx, gg = _rms_bwd(x_ref[...], g_ref[...], dh_ref[...], D_MODEL)
        o_ref[...] = dy_ref[...] + dx
        dg_ref[...] += _fold8(gg)

    row = pl.BlockSpec((ts, D_MODEL), lambda i: (i, 0))
    return pl.pallas_call(
        body, name="dx_bwd", grid=(s // ts,),
        in_specs=[row, row, row, pl.BlockSpec((1, D_MODEL), lambda i: (0, 0))],
        out_specs=[row, pl.BlockSpec((8, D_MODEL), lambda i: (0, 0))],
        out_shape=[jax.ShapeDtypeStruct((s, D_MODEL), F32), jax.ShapeDtypeStruct((8, D_MODEL), F32)],
        compiler_params=_params(("arbitrary",)),
    )(dh, x, dy, g)


def _pad128(v, n):
    return jnp.pad(v.reshape(1, n), ((0, 0), (0, 128 - n)))


def _local_step(x, positions, mem, target, wts, gains):
    half = QK_ROPE // 2
    inv_freq = jnp.power(ROPE_THETA, -jnp.arange(half, dtype=F32) / half)
    ang = positions.astype(F32)[:, None] * inv_freq
    cos, sin = jnp.cos(ang), jnp.sin(ang)
    zpad = jnp.zeros((x.shape[0], 64), F32)
    cs = jnp.concatenate([cos, cos, zpad], axis=1)
    sg = jnp.concatenate([-sin, sin, zpad], axis=1)
    g_qr, g_kr = _pad128(gains["mla_qn_rope_g"], 64), _pad128(gains["mla_kn_rope_g"], 64)
    g_qn, g_kn = gains["mla_qn_nope_g"], gains["mla_kn_nope_g"]

    p_all, h = _proj_fwd(x, gains["norm_g"], wts["w_all"])
    a_conv, co = _conv_fwd(p_all, wts["conv_w"])
    cqn, ckvn, krope = _lora_fwd(p_all, gains["mla_q_norm_g"], gains["mla_kv_norm_g"], g_kr, cs, sg)
    q_raw = _matmul(cqn, wts["w_uq"], "nn", BF16, "q_up")
    kv = _matmul(ckvn, wts["w_ukv"], "nn", BF16, "kv_up")
    qcat = _q_prep(q_raw, g_qn, g_qr, cs, sg)
    kcat = _k_prep(kv, krope, g_kn)
    mla_y, a_mla, lse = _flash_fwd(qcat, kcat, kv, p_all)
    memn, kraw, kn, vmem = _mem_kv_fwd(mem, gains["mem_norm_g"], wts["w_mem_kv"], gains["mem_kn_g"])
    a_mem = _mem_attn_fwd(p_all, kn, vmem, gains["mem_qn_g"])
    o_c, o_m, o_e, merged = _merge_fwd(a_conv, wts["w_conv_out"], a_mla, wts["w_mla_out"], a_mem,
                                       wts["w_mem_out"], p_all)
    dy, dyb, loss_parts = _out_fwd(merged, wts["w_o"], x, target)

    gw = {}
    gw["w_o"] = _matmul(merged, dyb, "tn", F32, "dw_o")
    do_c, do_m, do_e, dp_g = _merge_bwd(dyb, wts["w_o"], p_all, o_c, o_m, o_e)
    gw["w_conv_out"] = _matmul(a_conv, do_c, "tn", F32, "dw_conv_out")
    gw["w_mla_out"] = _matmul(a_mla, do_m, "tn", F32, "dw_mla_out")
    gw["w_mem_out"] = _matmul(a_mem, do_e, "tn", F32, "dw_mem_out")
    da_conv = _matmul(do_c, wts["w_conv_out"], "nt", BF16, "da_conv")
    da_mla = _matmul(do_m, wts["w_mla_out"], "nt", BF16, "da_mla")
    da_mem = _matmul(do_e, wts["w_mem_out"], "nt", BF16, "da_mem")
    dp_c, dconv_w = _conv_bwd(da_conv, p_all, co, wts["conv_w"])
    dmla_y, dp_z, delta = _mla_gate_bwd(da_mla, mla_y, p_all)
    dqcat, dkcat, dv = _flash_bwd(qcat, kcat, kv, dmla_y, lse, delta)
    dq_raw, dg_q = _q_prep_bwd(dqcat, q_raw, g_qn, g_qr, cs, sg)
    dkv, dkr, dg_kn = _k_prep_bwd(dkcat, dv, kv, g_kn)
    gw["w_uq"] = _matmul(cqn, dq_raw, "tn", F32, "dw_uq")
    gw["w_ukv"] = _matmul(ckvn, dkv, "tn", F32, "dw_ukv")
    dcqn = _matmul(dq_raw, wts["w_uq"], "nt", F32, "dcqn")
    dckvn = _matmul(dkv, wts["w_ukv"], "nt", F32, "dckvn")
    dp_s, dg_qn, dg_kvn, dg_kr = _lora_bwd(dcqn, dckvn, dkr, p_all, gains["mla_q_norm_g"],
                                            gains["mla_kv_norm_g"], g_kr, cs, sg)
    dp_m, dkn, dvm, dg_mq = _mem_attn_bwd(da_mem, p_all, kn, vmem, gains["mem_qn_g"])
    gw["w_mem_kv"], dg_mn, dg_mk = _mem_kv_bwd(dkn, dvm, kraw, memn, mem, gains["mem_norm_g"],
                                               wts["w_mem_kv"], gains["mem_kn_g"])
    dps = [dp_g, dp_z, dp_c, dp_m, dp_s]
    dh = _dh_bwd(dps, wts["w_all"])
    grad_x, dg_n = _dx_bwd(dh, x, dy, gains["norm_g"])
    gw["w_all"] = [_matmul(h, d, "tn", F32, "dw_in_%d" % k) for k, d in enumerate(dps)]

    gsmall = {
        "norm_g": dg_n.sum(0), "conv_w": dconv_w.sum(1), "mla_q_norm_g": dg_qn.sum(0),
        "mla_kv_norm_g": dg_kvn.sum(0), "mla_qn_nope_g": dg_q[0].sum(0), "mla_qn_rope_g": dg_q[1].sum(0)[:64],
        "mla_kn_nope_g": dg_kn.sum(0), "mla_kn_rope_g": dg_kr.sum(0)[:64], "mem_norm_g": dg_mn.sum(0),
        "mem_qn_g": dg_mq.sum(0), "mem_kn_g": dg_mk.sum(0),
    }
    return loss_parts, grad_x, gw, gsmall


def _me():
    return lax.axis_index("x"), lax.axis_index("y"), lax.axis_index("c")


def _chips(x, y):
    return [(1 - x, y), (x, 1 - y), (1 - x, 1 - y)]


ANY = pl.BlockSpec(memory_space=pl.ANY)


def _allgather_shards(xs, name):
    r, c = xs.shape
    hr = r // 2

    def body(x_ref, out_ref, send_sems, recv_sems):
        x, y, cc = _me()
        sibling = (x, y, 1 - cc)
        chips = _chips(x, y)

        def rows(px, py, pc):
            return out_ref.at[pl.ds((4 * px + 2 * py + pc) * hr, hr), :]

        def copy(k, block, to, src=None):
            return pltpu.make_async_remote_copy(
                src_ref=rows(*block) if src is None else src, dst_ref=rows(*block),
                send_sem=send_sems.at[k], recv_sem=recv_sems.at[k], device_id=to, device_id_type=MESH)

        my_half = x_ref.at[pl.ds(cc * hr, hr), :]
        first = [copy(j, (x, y, cc), (*chip, cc), src=my_half) for j, chip in enumerate(chips)]
        for cp in first:
            cp.start()
        passed = [copy(3 + j, (*chip, cc), sibling) for j, chip in enumerate(chips)]
        for j, chip in enumerate(chips):
            copy(j, (*chip, cc), (x, y, cc)).wait_recv()
            passed[j].start()
        for j, chip in enumerate(chips):
            copy(3 + j, (*chip, 1 - cc), (x, y, cc)).wait_recv()
        for cp in first + passed:
            cp.wait_send()

    out = pl.pallas_call(
        body, name=name, in_specs=[ANY], out_specs=ANY,
        out_shape=jax.ShapeDtypeStruct((8 * hr, c), xs.dtype),
        scratch_shapes=[pltpu.SemaphoreType.DMA((6,)), pltpu.SemaphoreType.DMA((6,))],
    )(xs)
    xi, yi, _ = _me()
    return lax.dynamic_update_slice(out.reshape(4, r, c), xs[None], (2 * xi + yi, 0, 0))


def _swap_sibling(src, name, by_half):
    def body(src_ref, got_ref, send_sem, recv_sem):
        x, y, cc = _me()
        cp = pltpu.make_async_remote_copy(src_ref=src_ref.at[1 - cc] if by_half else src_ref, dst_ref=got_ref,
                                          send_sem=send_sem, recv_sem=recv_sem, device_id=(x, y, 1 - cc),
                                          device_id_type=MESH)
        cp.start()
        cp.wait()

    return pl.pallas_call(
        body, name=name, in_specs=[ANY], out_specs=ANY,
        out_shape=jax.ShapeDtypeStruct(src.shape[1:] if by_half else src.shape, src.dtype),
        scratch_shapes=[pltpu.SemaphoreType.DMA, pltpu.SemaphoreType.DMA],
    )(src)


def _exchange_chips(t, name):
    def body(t_ref, r_ref, send_sems, recv_sems):
        x, y, cc = _me()
        cps = []
        for j, (px, py) in enumerate(_chips(x, y)):
            cps.append(pltpu.make_async_remote_copy(
                src_ref=t_ref.at[2 * px + py], dst_ref=r_ref.at[j], send_sem=send_sems.at[j],
                recv_sem=recv_sems.at[j], device_id=(px, py, cc), device_id_type=MESH))
        for cp in cps:
            cp.start()
        for cp in cps:
            cp.wait()

    return pl.pallas_call(
        body, name=name, in_specs=[ANY], out_specs=ANY,
        out_shape=jax.ShapeDtypeStruct((3,) + t.shape[1:], t.dtype),
        scratch_shapes=[pltpu.SemaphoreType.DMA((3,)), pltpu.SemaphoreType.DMA((3,))],
    )(t)


def _row_tile(hr, c, itemsize):
    t = hr
    while t > 16 and t * c * itemsize > (2 << 20):
        t //= 2
    return t


def _add_own_half(gb, got, sel, name):
    _, _, hr, c = gb.shape
    tr = _row_tile(hr, c, 4)

    def body(sel_ref, a_ref, b_ref, o_ref):
        o_ref[...] = (a_ref[0].astype(F32) + b_ref[...].astype(F32)).astype(BF16)

    return pl.pallas_call(
        body, name=name,
        grid_spec=pltpu.PrefetchScalarGridSpec(
            num_scalar_prefetch=1, grid=(4, hr // tr),
            in_specs=[pl.BlockSpec((1, 1, tr, c), lambda k, i, sel_ref: (sel_ref[0], k, i, 0)),
                      pl.BlockSpec((1, tr, c), lambda k, i, sel_ref: (k, i, 0))],
            out_specs=pl.BlockSpec((1, tr, c), lambda k, i, sel_ref: (k, i, 0))),
        out_shape=jax.ShapeDtypeStruct(got.shape, BF16),
        compiler_params=_params(("parallel", "parallel")),
    )(sel, gb, got)


def _sum_partials(t, rcv, sel, name):
    _, hr, c = t.shape
    tr = _row_tile(hr, c, 4)

    def body(sel_ref, t_ref, r_ref, o_ref):
        o_ref[...] = ((t_ref[0].astype(F32) + r_ref[0].astype(F32)) + r_ref[1].astype(F32)) + r_ref[2].astype(F32)

    return pl.pallas_call(
        body, name=name,
        grid_spec=pltpu.PrefetchScalarGridSpec(
            num_scalar_prefetch=1, grid=(hr // tr,),
            in_specs=[pl.BlockSpec((1, tr, c), lambda i, sel_ref: (sel_ref[0], i, 0)),
                      pl.BlockSpec((3, tr, c), lambda i, sel_ref: (0, i, 0))],
            out_specs=pl.BlockSpec((tr, c), lambda i, sel_ref: (i, 0))),
        out_shape=jax.ShapeDtypeStruct((hr, c), F32),
        compiler_params=_params(("parallel",)),
    )(sel, t, rcv)


def _reduce_scatter(g_sharded, tag):
    _, r, c = g_sharded.shape
    hr = r // 2
    x, y, cc = _me()
    gb = g_sharded.astype(BF16).reshape(4, 2, hr, c).transpose(1, 0, 2, 3)
    got = _swap_sibling(gb, "rs_swap_" + tag, True)
    t = _add_own_half(gb, got, jnp.reshape(cc, (1,)).astype(jnp.int32), "rs_add_" + tag)
    rcv = _exchange_chips(t, "rs_ici_" + tag)
    fh = _sum_partials(t, rcv, jnp.reshape(2 * x + y, (1,)).astype(jnp.int32), "rs_sum_" + tag)
    return fh, _swap_sibling(fh, "rs_join_" + tag, False)


def _allreduce_small(v, tag):
    r = v.shape[0]

    def body(v_ref, out_ref, buf_ref, send_sems, recv_sems):
        x, y, cc = _me()
        me = 4 * x + 2 * y + cc
        buf_ref[pl.ds(pl.multiple_of(me * r, 8), r), :] = v_ref[...]
        peers = [(x, y, 1 - cc)] + [(px, py, pc) for (px, py) in _chips(x, y) for pc in (cc, 1 - cc)]
        cps = []
        for k, (px, py, pc) in enumerate(peers):
            mine = buf_ref.at[pl.ds(pl.multiple_of(me * r, 8), r), :]
            cps.append(pltpu.make_async_remote_copy(
                src_ref=v_ref, dst_ref=mine, send_sem=send_sems.at[k], recv_sem=recv_sems.at[k],
                device_id=(px, py, pc), device_id_type=MESH))
        for cp in cps:
            cp.start()
        for cp in cps:
            cp.wait()
        acc = buf_ref[0:r, :]
        for d in range(1, 8):
            acc = acc + buf_ref[d * r:(d + 1) * r, :]
        out_ref[...] = acc

    vm = pl.BlockSpec(memory_space=pltpu.VMEM)
    return pl.pallas_call(
        body, name="allreduce_small_" + tag, in_specs=[vm], out_specs=vm,
        out_shape=jax.ShapeDtypeStruct((r, 128), F32),
        scratch_shapes=[pltpu.VMEM((8 * r, 128), F32), pltpu.SemaphoreType.DMA((7,)), pltpu.SemaphoreType.DMA((7,))],
    )(v)


def _adamw(w, g, m, v, name):
    r, c = w.shape
    tr = _row_tile(r, c, 4) if r >= 16 else r
    c1 = 1.0 / (1.0 - ADAM_B1 ** ADAM_STEP)
    c2 = 1.0 / (1.0 - ADAM_B2 ** ADAM_STEP)

    def body(w_ref, g_ref, m_ref, v_ref, d_ref, mo_ref, vo_ref):
        gg = g_ref[...]
        mn = ADAM_B1 * m_ref[...] + (1.0 - ADAM_B1) * gg
        vn = ADAM_B2 * v_ref[...] + (1.0 - ADAM_B2) * (gg * gg)
        mo_ref[...] = mn
        vo_ref[...] = vn
        d_ref[...] = -ADAM_LR * ((mn * c1) / (jnp.sqrt(vn * c2) + ADAM_EPS) + ADAM_WD * w_ref[...])

    blk = pl.BlockSpec((tr, c), lambda i: (i, 0))
    return pl.pallas_call(
        body, name=name, grid=(r // tr,), in_specs=[blk] * 4, out_specs=[blk] * 3,
        out_shape=[jax.ShapeDtypeStruct((r, c), F32)] * 3,
        compiler_params=_params(("parallel",)),
    )(w, g, m, v)


def _adamw_halves(w, g_own, g_other, m, v, sel, name):
    r, c = w.shape
    hr = r // 2
    tr = _row_tile(hr, c, 4)
    nh = hr // tr
    c1 = 1.0 / (1.0 - ADAM_B1 ** ADAM_STEP)
    c2 = 1.0 / (1.0 - ADAM_B2 ** ADAM_STEP)

    def body(sel_ref, w_ref, go_ref, gx_ref, m_ref, v_ref, d_ref, mo_ref, vo_ref, g_ref):
        mine = (pl.program_id(0) // nh) == sel_ref[0]
        gg = jnp.where(mine, go_ref[...], gx_ref[...])
        g_ref[...] = gg
        mn = ADAM_B1 * m_ref[...] + (1.0 - ADAM_B1) * gg
        vn = ADAM_B2 * v_ref[...] + (1.0 - ADAM_B2) * (gg * gg)
        mo_ref[...] = mn
        vo_ref[...] = vn
        d_ref[...] = -ADAM_LR * ((mn * c1) / (jnp.sqrt(vn * c2) + ADAM_EPS) + ADAM_WD * w_ref[...])

    blk = pl.BlockSpec((tr, c), lambda i, sel_ref: (i, 0))
    half = pl.BlockSpec((tr, c), lambda i, sel_ref: (i % nh, 0))
    return pl.pallas_call(
        body, name=name,
        grid_spec=pltpu.PrefetchScalarGridSpec(
            num_scalar_prefetch=1, grid=(r // tr,),
            in_specs=[blk, half, half, blk, blk], out_specs=[blk] * 4),
        out_shape=[jax.ShapeDtypeStruct((r, c), F32)] * 4,
        compiler_params=_params(("parallel",)),
    )(sel, w, g_own, g_other, m, v)


WEIGHTS = ['norm_g', 'w_in', 'conv_w', 'w_conv_out', 'mla_q_norm_g', 'w_uq', 'mla_kv_norm_g', 'w_ukv',
           'mla_qn_nope_g', 'mla_qn_rope_g', 'mla_kn_nope_g', 'mla_kn_rope_g', 'w_mla_out', 'mem_norm_g',
           'w_mem_kv', 'mem_qn_g', 'mem_kn_g', 'w_mem_out', 'w_o']
COL_SHARDED = ['w_in', 'w_conv_out', 'w_uq', 'w_ukv', 'w_mem_out']
ROW_SHARDED = ['w_mla_out', 'w_mem_kv', 'w_o']
SMALL = ['norm_g', 'conv_w', 'mla_q_norm_g', 'mla_kv_norm_g', 'mla_qn_nope_g', 'mla_qn_rope_g', 'mla_kn_nope_g',
         'mla_kn_rope_g', 'mem_norm_g', 'mem_qn_g', 'mem_kn_g']
SMALL_SIZES = [2048, 3072, 512, 512, 128, 64, 128, 64, 2048, 256, 256]
PACK_ROWS = 72


def _gather_weights(shards):
    full = {}
    for n in COL_SHARDED:
        g = _allgather_shards(shards[n].astype(BF16), "ag_" + n)
        full[n] = g.transpose(1, 0, 2).reshape(g.shape[1], 4 * g.shape[2])
    for n in ROW_SHARDED:
        g = _allgather_shards(shards[n].astype(BF16), "ag_" + n)
        full[n] = g.reshape(4 * g.shape[1], g.shape[2])
    return full


def _to_local_layouts(full):
    w_in = full["w_in"]
    conv, small, mz, memq, gates = (w_in[:, 0:4096], w_in[:, 4096:5184], w_in[:, 5184:7232],
                                    w_in[:, 7232:9280], w_in[:, 9280:15424])
    wts = dict(full)
    wts["w_all"] = jnp.concatenate([gates, mz, conv, memq, small, jnp.zeros((D_MODEL, 2048 - 1088), BF16)], axis=1)
    wts["w_uq"] = jnp.pad(full["w_uq"].reshape(512, MLA_HEADS, 192), ((0, 0), (0, 0), (0, 64))).reshape(512, 4096)
    del wts["w_in"]
    return wts


def _grads_to_reference_layout(gw):
    dw_g, dw_z, dw_c, dw_m, dw_s = gw["w_all"]
    out = {k: v for k, v in gw.items() if k != "w_all"}
    out["w_in"] = jnp.concatenate([dw_c, dw_s[:, :1088], dw_z, dw_m, dw_g], axis=1)
    out["w_uq"] = gw["w_uq"].reshape(512, MLA_HEADS, HEAD_PAD)[:, :, :192].reshape(512, 3072)
    return out


def kernel(x, positions, mem, norm_g, w_in, conv_w, w_conv_out, mla_q_norm_g, w_uq, mla_kv_norm_g, w_ukv, mla_qn_nope_g, mla_qn_rope_g, mla_kn_nope_g, mla_kn_rope_g, w_mla_out, mem_norm_g, w_mem_kv, mem_qn_g, mem_kn_g, w_mem_out, w_o, loss_target, m_norm_g, m_w_in, m_conv_w, m_w_conv_out, m_mla_q_norm_g, m_w_uq, m_mla_kv_norm_g, m_w_ukv, m_mla_qn_nope_g, m_mla_qn_rope_g, m_mla_kn_nope_g, m_mla_kn_rope_g, m_w_mla_out, m_mem_norm_g, m_w_mem_kv, m_mem_qn_g, m_mem_kn_g, m_w_mem_out, m_w_o, v_norm_g, v_w_in, v_conv_w, v_w_conv_out, v_mla_q_norm_g, v_w_uq, v_mla_kv_norm_g, v_w_ukv, v_mla_qn_nope_g, v_mla_qn_rope_g, v_mla_kn_nope_g, v_mla_kn_rope_g, v_w_mla_out, v_mem_norm_g, v_w_mem_kv, v_mem_qn_g, v_mem_kn_g, v_w_mem_out, v_w_o):
    args = locals()
    w = {n: args[n][0] for n in WEIGHTS}
    m1 = {n: args["m_" + n][0] for n in WEIGHTS}
    v2 = {n: args["v_" + n][0] for n in WEIGHTS}
    xi, yi, ci = _me()
    chip = 2 * xi + yi

    conv_slot = jnp.zeros((3, 4, 256), F32).at[:, chip, :].set(w["conv_w"] * jnp.where(ci == 0, 1.0, 0.0))
    pre = jnp.zeros((PACK_ROWS * 128,), F32).at[0:3072].set(conv_slot.reshape(3072))
    conv_full = _allreduce_small(pre.reshape(PACK_ROWS, 128), "conv_w").reshape(-1)[0:3072].reshape(3, 1024)

    full = _gather_weights({n: w[n] for n in COL_SHARDED + ROW_SHARDED})
    wts = _to_local_layouts(full)
    wts["conv_w"] = conv_full
    gains = {n: w[n].reshape(1, -1) for n in SMALL if n != "conv_w"}

    loss_parts, grad_x, gw, gsmall = _local_step(x[0], positions[0], mem[0], loss_target[0], wts, gains)
    gw = _grads_to_reference_layout(gw)

    loss_local = 0.5 * jnp.sum(loss_parts) * (1.0 / D_MODEL)
    flat = jnp.concatenate([gsmall[n].reshape(-1) for n in SMALL] + [loss_local.reshape(1)])
    flat = jnp.pad(flat, (0, PACK_ROWS * 128 - flat.shape[0]))
    tot = _allreduce_small(flat.reshape(PACK_ROWS, 128), "grads").reshape(-1)
    grads = {}
    off = 0
    for n, sz in zip(SMALL, SMALL_SIZES):
        grads[n] = tot[off:off + sz]
        off += sz
    loss = tot[off]
    grads["conv_w"] = lax.dynamic_slice(grads["conv_w"].reshape(3, 1024), (0, chip * 256), (3, 256))
    for n in SMALL:
        grads[n] = grads[n].reshape(w[n].shape)

    deltas, new_m, new_v = {}, {}, {}
    for n in SMALL:
        shp = w[n].shape
        two_d = (lambda a: a.reshape(1, -1)) if len(shp) == 1 else (lambda a: a)
        d, mn, vn = _adamw(two_d(w[n]), two_d(grads[n]), two_d(m1[n]), two_d(v2[n]), "adamw_" + n)
        deltas[n], new_m[n], new_v[n] = d.reshape(shp), mn.reshape(shp), vn.reshape(shp)

    sel_c = jnp.reshape(ci, (1,)).astype(jnp.int32)
    for n in COL_SHARDED + ROW_SHARDED:
        g = gw[n]
        r, c = g.shape
        cut = g.reshape(r, 4, c // 4).transpose(1, 0, 2) if n in COL_SHARDED else g.reshape(4, r // 4, c)
        g_own, g_other = _reduce_scatter(cut, n)
        deltas[n], new_m[n], new_v[n], grads[n] = _adamw_halves(w[n], g_own, g_other, m1[n], v2[n], sel_c,
                                                                 "adamw_" + n)

    lead = lambda a: a[None]
    return (loss, grad_x[None], *[lead(grads[n]) for n in WEIGHTS], *[lead(deltas[n]) for n in WEIGHTS],
            *[lead(new_m[n]) for n in WEIGHTS], *[lead(new_v[n]) for n in WEIGHTS])
```

```python
import functools

import numpy as np
import jax
import jax.numpy as jnp
from jax import lax
from jax.experimental import pallas as pl
from jax.experimental.pallas import tpu as pltpu

F32 = jnp.float32
BF16 = jnp.bfloat16
MESH = pl.DeviceIdType.MESH

D_MODEL = 2048
EPS = 1e-6
CHUNK = 64
MLA_HEADS = 16
QK_NOPE = 128
QK_ROPE = 64
HEAD_PAD = 256
MEM_TOKENS = 256
MEM_HEADS = 4
MEM_HEAD_DIM = 256
ROPE_THETA = 10000.0
MLA_SCALE = (QK_NOPE + QK_ROPE) ** -0.5
MEM_SCALE = MEM_HEAD_DIM ** -0.5
LN2 = 0.6931471805599453
Q_SCALE = MLA_SCALE / LN2
NEG = -1e30

ADAM_LR = 0.001
ADAM_B1 = 0.9
ADAM_B2 = 0.999
ADAM_EPS = 1e-08
ADAM_WD = 0.01
ADAM_STEP = 10

NP = 16384
COL_G, COL_Z, COL_C, COL_M, COL_S = 0, 6144, 8192, 12288, 14336

TM_PROJ = 1024
TN_PROJ = 1024
TS_ROW = 512
TS_BWD = 256
TQ_ATT = 1024
TK_ATT = 1024
TR_ATT = 256
VMEM_LIMIT = 56 * 1024 * 1024

NT_DIMS = (((1,), (1,)), ((), ()))
TN_DIMS = (((0,), (0,)), ((), ()))


def _dot(a, b, mode="nn"):
    if mode == "nn":
        return jnp.dot(a, b, preferred_element_type=F32)
    return lax.dot_general(a, b, NT_DIMS if mode == "nt" else TN_DIMS, preferred_element_type=F32)


def _sigmoid(z):
    return 1.0 / (1.0 + jnp.exp(-z))


def _params(sem=None, vmem=VMEM_LIMIT):
    return pltpu.CompilerParams(dimension_semantics=sem, vmem_limit_bytes=vmem)


def _fold8(v):
    r, c = v.shape
    return v.reshape(r // 8, 8, c).sum(axis=0)


def _swap32(t):
    lane = lax.broadcasted_iota(jnp.int32, t.shape, 1)
    return jnp.where(lane < 32, pltpu.roll(t, 96, 1), pltpu.roll(t, 32, 1))


def _rope(t, cs, sg):
    return t * cs + _swap32(t) * sg


def _rope_bwd(d, cs, sg):
    return d * cs + _swap32(d * sg)


def _rms_fwd(xf, g, n):
    r = lax.rsqrt(jnp.sum(xf * xf, axis=-1, keepdims=True) * (1.0 / n) + EPS)
    return xf * r * g


def _rms_bwd(xf, g, dy, n):
    r = lax.rsqrt(jnp.sum(xf * xf, axis=-1, keepdims=True) * (1.0 / n) + EPS)
    xhat = xf * r
    dyg = dy * g
    dx = r * (dyg - xhat * (jnp.sum(dyg * xhat, axis=-1, keepdims=True) * (1.0 / n)))
    return dx, dy * xhat


def _matmul(a, b, mode, out_dtype, name, tm=1024, tn=1024, tk=1024):
    if mode == "nn":
        (m, k), (_, n) = a.shape, b.shape
    elif mode == "nt":
        (m, k), (n, _) = a.shape, b.shape
    else:
        (k, m), (_, n) = a.shape, b.shape
    tm, tn, tk = min(tm, m), min(tn, n), min(tk, k)
    nk = k // tk
    if mode == "tn":
        a_spec = pl.BlockSpec((tk, tm), lambda i, j, kk: (kk, i))
    else:
        a_spec = pl.BlockSpec((tm, tk), lambda i, j, kk: (i, kk))
    if mode == "nt":
        b_spec = pl.BlockSpec((tn, tk), lambda i, j, kk: (j, kk))
    else:
        b_spec = pl.BlockSpec((tk, tn), lambda i, j, kk: (kk, j))

    def body(a_ref, b_ref, o_ref, acc_ref):
        kk = pl.program_id(2)

        @pl.when(kk == 0)
        def _():
            acc_ref[...] = jnp.zeros_like(acc_ref)

        acc_ref[...] += _dot(a_ref[...].astype(BF16), b_ref[...].astype(BF16), mode)

        @pl.when(kk == nk - 1)
        def _():
            o_ref[...] = acc_ref[...].astype(out_dtype)

    return pl.pallas_call(
        body, name=name, grid=(m // tm, n // tn, nk),
        in_specs=[a_spec, b_spec],
        out_specs=pl.BlockSpec((tm, tn), lambda i, j, kk: (i, j)),
        out_shape=jax.ShapeDtypeStruct((m, n), out_dtype),
        scratch_shapes=[pltpu.VMEM((tm, tn), F32)],
        compiler_params=_params(("parallel", "parallel", "arbitrary")),
    )(a, b)


def _proj_fwd(x, g, w_all):
    s = x.shape[0]
    tm, tn = min(TM_PROJ, s), TN_PROJ

    def body(x_ref, g_ref, w_ref, p_ref, h_ref):
        @pl.when(pl.program_id(1) == 0)
        def _():
            h_ref[...] = _rms_fwd(x_ref[...], g_ref[...], D_MODEL).astype(BF16)

        p_ref[...] = _dot(h_ref[...], w_ref[...]).astype(BF16)

    return pl.pallas_call(
        body, name="proj_fwd", grid=(s // tm, NP // tn),
        in_specs=[pl.BlockSpec((tm, D_MODEL), lambda i, j: (i, 0)),
                  pl.BlockSpec((1, D_MODEL), lambda i, j: (0, 0)),
                  pl.BlockSpec((D_MODEL, tn), lambda i, j: (0, j))],
        out_specs=[pl.BlockSpec((tm, tn), lambda i, j: (i, j)),
                   pl.BlockSpec((tm, D_MODEL), lambda i, j: (i, 0))],
        out_shape=[jax.ShapeDtypeStruct((s, NP), BF16), jax.ShapeDtypeStruct((s, D_MODEL), BF16)],
        compiler_params=_params(("parallel", "arbitrary")),
    )(x, g, w_all)


def _shift_rows(v, carry, j, ts, back):
    if back:
        main = pltpu.roll(v, j, 0)
        edge = pltpu.roll(jnp.concatenate([carry, v[:8]], axis=0), j, 0)[8:]
        return jnp.concatenate([edge, main[8:]], axis=0)
    main = pltpu.roll(v, ts - j, 0)
    edge = pltpu.roll(jnp.concatenate([v[ts - 8:], carry], axis=0), 16 - j, 0)[:8]
    return jnp.concatenate([main[:ts - 8], edge], axis=0)


def _conv_fwd(p_all, conv_w):
    s = p_all.shape[0]
    ts = min(TS_ROW, s)
    c0 = COL_C // 1024

    def body(cg_ref, bg_ref, u_ref, z_ref, w_ref, a_ref, co_ref, carry_ref):
        @pl.when(pl.program_id(0) == 0)
        def _():
            carry_ref[...] = jnp.zeros_like(carry_ref)

        p = cg_ref[...].astype(F32) * u_ref[...].astype(F32)
        carry = carry_ref[...]
        co = (w_ref[2:3, :] * p + w_ref[1:2, :] * _shift_rows(p, carry, 1, ts, True)
              + w_ref[0:1, :] * _shift_rows(p, carry, 2, ts, True))
        carry_ref[...] = p[ts - 8:]
        z = z_ref[...].astype(F32)
        a_ref[...] = (bg_ref[...].astype(F32) * co * (z * _sigmoid(z))).astype(BF16)
        co_ref[...] = co.astype(BF16)

    seg = lambda c: pl.BlockSpec((ts, 1024), lambda i: (i, c0 + c))
    return pl.pallas_call(
        body, name="conv_fwd", grid=(s // ts,),
        in_specs=[seg(0), seg(1), seg(2), seg(3), pl.BlockSpec((3, 1024), lambda i: (0, 0))],
        out_specs=[pl.BlockSpec((ts, 1024), lambda i: (i, 0))] * 2,
        out_shape=[jax.ShapeDtypeStruct((s, 1024), BF16)] * 2,
        scratch_shapes=[pltpu.VMEM((8, 1024), F32)],
        compiler_params=_params(("arbitrary",)),
    )(p_all, p_all, p_all, p_all, conv_w)


def _lora_fwd(p_all, gq, gkv, gkr, cs, sg):
    s = p_all.shape[0]
    ts = min(TS_ROW, s)

    def body(cq_ref, ckv_ref, kr_ref, gq_ref, gkv_ref, gkr_ref, cs_ref, sg_ref, cqn_ref, ckvn_ref, krope_ref):
        cqn_ref[...] = _rms_fwd(cq_ref[...].astype(F32), gq_ref[...], 512).astype(BF16)
        ckvn_ref[...] = _rms_fwd(ckv_ref[...].astype(F32), gkv_ref[...], 512).astype(BF16)
        kn = _rms_fwd(kr_ref[...].astype(F32), gkr_ref[...], QK_ROPE)
        krope_ref[...] = _rope(kn, cs_ref[...], sg_ref[...]).astype(BF16)

    row = lambda w: pl.BlockSpec((ts, w), lambda i: (i, 0))
    vec = lambda w: pl.BlockSpec((1, w), lambda i: (0, 0))
    return pl.pallas_call(
        body, name="lora_fwd", grid=(s // ts,),
        in_specs=[pl.BlockSpec((ts, 512), lambda i: (i, COL_S // 512)),
                  pl.BlockSpec((ts, 512), lambda i: (i, COL_S // 512 + 1)),
                  pl.BlockSpec((ts, 128), lambda i: (i, (COL_S + 1024) // 128)),
                  vec(512), vec(512), vec(128), row(128), row(128)],
        out_specs=[row(512), row(512), row(128)],
        out_shape=[jax.ShapeDtypeStruct((s, 512), BF16), jax.ShapeDtypeStruct((s, 512), BF16),
                   jax.ShapeDtypeStruct((s, 128), BF16)],
        compiler_params=_params(("parallel",)),
    )(p_all, p_all, p_all, gq, gkv, gkr, cs, sg)


def _q_prep(q_raw, gn, gr, cs, sg):
    s = q_raw.shape[0]
    ts = min(TS_ROW, s)

    def body(q_ref, gn_ref, gr_ref, cs_ref, sg_ref, o_ref):
        for h in range(MLA_HEADS):
            q = q_ref[:, h * HEAD_PAD:(h + 1) * HEAD_PAD].astype(F32)
            a = _rms_fwd(q[:, :128], gn_ref[...], QK_NOPE)
            b = _rope(_rms_fwd(q[:, 128:], gr_ref[...], QK_ROPE), cs_ref[...], sg_ref[...])
            o_ref[h] = (jnp.concatenate([a, b], axis=1) * Q_SCALE).astype(BF16)

    return pl.pallas_call(
        body, name="q_prep", grid=(s // ts,),
        in_specs=[pl.BlockSpec((ts, MLA_HEADS * HEAD_PAD), lambda i: (i, 0)),
                  pl.BlockSpec((1, 128), lambda i: (0, 0)), pl.BlockSpec((1, 128), lambda i: (0, 0)),
                  pl.BlockSpec((ts, 128), lambda i: (i, 0)), pl.BlockSpec((ts, 128), lambda i: (i, 0))],
        out_specs=pl.BlockSpec((MLA_HEADS, ts, HEAD_PAD), lambda i: (0, i, 0)),
        out_shape=jax.ShapeDtypeStruct((MLA_HEADS, s, HEAD_PAD), BF16),
        compiler_params=_params(("parallel",)),
    )(q_raw, gn, gr, cs, sg)


def _k_prep(kv, krope, gk):
    s = kv.shape[0]
    ts = min(TS_ROW, s)

    def body(k_ref, kr_ref, gk_ref, o_ref, vt_ref):
        for h in range(MLA_HEADS):
            a = _rms_fwd(k_ref[:, h * HEAD_PAD:h * HEAD_PAD + 128].astype(F32), gk_ref[...], QK_NOPE)
            o_ref[h] = jnp.concatenate([a.astype(BF16), kr_ref[...]], axis=1)
            vt_ref[h] = k_ref[:, h * HEAD_PAD + 128:(h + 1) * HEAD_PAD].astype(F32).T.astype(BF16)

    return pl.pallas_call(
        body, name="k_prep", grid=(s // ts,),
        in_specs=[pl.BlockSpec((ts, MLA_HEADS * HEAD_PAD), lambda i: (i, 0)),
                  pl.BlockSpec((ts, 128), lambda i: (i, 0)),
                  pl.BlockSpec((1, 128), lambda i: (0, 0))],
        out_specs=[pl.BlockSpec((MLA_HEADS, ts, HEAD_PAD), lambda i: (0, i, 0)),
                   pl.BlockSpec((MLA_HEADS, 128, ts), lambda i: (0, 0, i))],
        out_shape=[jax.ShapeDtypeStruct((MLA_HEADS, s, HEAD_PAD), BF16),
                   jax.ShapeDtypeStruct((MLA_HEADS, 128, s), BF16)],
        compiler_params=_params(("parallel",)),
    )(kv, krope, gk)


def _chunk_mask(tq, tk, row0=0):
    r = (lax.broadcasted_iota(jnp.int32, (tq, tk), 0) + row0) // CHUNK
    c = lax.broadcasted_iota(jnp.int32, (tq, tk), 1) // CHUNK
    return c <= r


def _causal_pairs(n, by_key):
    if by_key:
        pairs = [(i, j) for j in range(n) for i in range(j, n)]
    else:
        pairs = [(i, j) for i in range(n) for j in range(i + 1)]
    return (jnp.asarray([p[0] for p in pairs], jnp.int32), jnp.asarray([p[1] for p in pairs], jnp.int32))


def _flash_fwd(qcat, kcat, vt, p_all):
    s = qcat.shape[1]
    tq = tk = min(TQ_ATT, s)
    tc = min(TR_ATT, tq)
    nq = s // tq
    zc = COL_Z // 128
    qi, kj = _causal_pairs(nq, False)

    def body(qi_ref, kj_ref, q_ref, k_ref, vt_ref, z_ref, y_ref, a_ref, lse_ref, m_sc, l_sc, acc_sc):
        t = pl.program_id(1)
        i, j = qi_ref[t], kj_ref[t]

        @pl.when(j == 0)
        def _():
            m_sc[...] = jnp.full_like(m_sc, NEG)
            l_sc[...] = jnp.zeros_like(l_sc)
            acc_sc[...] = jnp.zeros_like(acc_sc)

        def step(masked):
            nc = tq // tc
            state = [[m_sc[:, c * tc:(c + 1) * tc], l_sc[:, c * tc:(c + 1) * tc], acc_sc[:, c * tc:(c + 1) * tc]]
                     for c in range(nc)]
            units = [(u, c) for u in range(tk // tc) for c in range(nc) if (u <= c or not masked)]

            def scores(u, c):
                return _dot(k_ref[0, u * tc:(u + 1) * tc, :], q_ref[0, c * tc:(c + 1) * tc, :], "nt")

            def weighted_values(u, c, alpha, pb):
                state[c][2] = alpha * state[c][2] + _dot(vt_ref[0, :, u * tc:(u + 1) * tc], pb)

            ahead = 3
            pending = [scores(*un) for un in units[:ahead]]
            late = None
            for n, (u, c) in enumerate(units):
                st = pending.pop(0)
                if n + ahead < len(units):
                    pending.append(scores(*units[n + ahead]))
                if masked and u == c:
                    kc = lax.broadcasted_iota(jnp.int32, (tc, tc), 0) // CHUNK
                    qc = lax.broadcasted_iota(jnp.int32, (tc, tc), 1) // CHUNK
                    st = jnp.where(kc <= qc, st, NEG)
                m_run, l_run, _ = state[c]
                m_new = jnp.maximum(m_run, jnp.max(st, axis=0, keepdims=True))
                alpha = jnp.exp2(m_run - m_new)
                p = jnp.exp2(st - m_new)
                state[c][0] = m_new
                state[c][1] = alpha * l_run + jnp.sum(p, axis=0, keepdims=True)
                if late is not None:
                    weighted_values(*late)
                late = (u, c, alpha, p.astype(BF16))
            weighted_values(*late)
            m_sc[...] = jnp.concatenate([s_[0] for s_ in state], axis=1)
            l_sc[...] = jnp.concatenate([s_[1] for s_ in state], axis=1)
            acc_sc[...] = jnp.concatenate([s_[2] for s_ in state], axis=1)

        @pl.when(j < i)
        def _():
            step(False)

        @pl.when(j == i)
        def _():
            step(True)
            y = (acc_sc[...] * (1.0 / l_sc[...])).T
            z = z_ref[...].astype(F32)
            y_ref[...] = y.astype(BF16)
            a_ref[...] = (y * (z * _sigmoid(z))).astype(BF16)
            lse2 = m_sc[...] + jnp.log(l_sc[...]) * (1.0 / LN2)
            lse_ref[0] = jnp.broadcast_to(lse2, (128, tq)).T

    return pl.pallas_call(
        body, name="flash_fwd",
        grid_spec=pltpu.PrefetchScalarGridSpec(
            num_scalar_prefetch=2, grid=(MLA_HEADS, qi.shape[0]),
            in_specs=[pl.BlockSpec((1, tq, HEAD_PAD), lambda h, t, qi_ref, kj_ref: (h, qi_ref[t], 0)),
                      pl.BlockSpec((1, tk, HEAD_PAD), lambda h, t, qi_ref, kj_ref: (h, kj_ref[t], 0)),
                      pl.BlockSpec((1, 128, tk), lambda h, t, qi_ref, kj_ref: (h, 0, kj_ref[t])),
                      pl.BlockSpec((tq, 128), lambda h, t, qi_ref, kj_ref: (qi_ref[t], zc + h))],
            out_specs=[pl.BlockSpec((tq, 128), lambda h, t, qi_ref, kj_ref: (qi_ref[t], h)),
                       pl.BlockSpec((tq, 128), lambda h, t, qi_ref, kj_ref: (qi_ref[t], h)),
                       pl.BlockSpec((1, tq, 128), lambda h, t, qi_ref, kj_ref: (h, qi_ref[t], 0))],
            scratch_shapes=[pltpu.VMEM((1, tq), F32), pltpu.VMEM((1, tq), F32), pltpu.VMEM((128, tq), F32)]),
        out_shape=[jax.ShapeDtypeStruct((s, MLA_HEADS * 128), BF16),
                   jax.ShapeDtypeStruct((s, MLA_HEADS * 128), BF16),
                   jax.ShapeDtypeStruct((MLA_HEADS, s, 128), F32)],
        compiler_params=_params(("parallel", "arbitrary")),
    )(qi, kj, qcat, kcat, vt, p_all)


def _mem_kv_fwd(mem, g_norm, w_kv, g_kn):
    m = mem.shape[0]

    def body(mem_ref, g_ref, w_ref, gk_ref, memn_ref, kraw_ref, kn_ref, v_ref):
        memn = _rms_fwd(mem_ref[...], g_ref[...], D_MODEL).astype(BF16)
        memn_ref[...] = memn
        kvm = _dot(memn, w_ref[...])
        kraw_ref[...] = kvm[:, :1024]
        v_ref[...] = kvm[:, 1024:].astype(BF16)
        for hh in range(MEM_HEADS):
            sl = slice(hh * MEM_HEAD_DIM, (hh + 1) * MEM_HEAD_DIM)
            kn_ref[:, sl] = _rms_fwd(kvm[:, sl], gk_ref[...], MEM_HEAD_DIM).astype(BF16)

    return pl.pallas_call(
        body, name="mem_kv_fwd",
        out_shape=[jax.ShapeDtypeStruct((m, D_MODEL), BF16), jax.ShapeDtypeStruct((m, 1024), F32),
                   jax.ShapeDtypeStruct((m, 1024), BF16), jax.ShapeDtypeStruct((m, 1024), BF16)],
        compiler_params=_params(),
    )(mem, g_norm, w_kv, g_kn)


def _mem_attn_fwd(p_all, kn, v, gq):
    s = p_all.shape[0]
    tq = min(TS_ROW, s)

    def body(q_ref, z_ref, kn_ref, v_ref, gq_ref, a_ref):
        z = z_ref[...].astype(F32)
        gate = z * _sigmoid(z)
        for hh in range(MEM_HEADS):
            sl = slice(hh * MEM_HEAD_DIM, (hh + 1) * MEM_HEAD_DIM)
            qn = _rms_fwd(q_ref[:, sl].astype(F32), gq_ref[...], MEM_HEAD_DIM).astype(BF16)
            sc = _dot(qn, kn_ref[:, sl], "nt") * MEM_SCALE
            e = jnp.exp(sc - jnp.max(sc, axis=-1, keepdims=True))
            p = e * (1.0 / jnp.sum(e, axis=-1, keepdims=True))
            y = _dot(p.astype(BF16), v_ref[:, sl])
            a_ref[:, sl] = (y * gate[:, sl]).astype(BF16)

    full = lambda a: pl.BlockSpec(a.shape, lambda i: (0, 0))
    return pl.pallas_call(
        body, name="mem_attn_fwd", grid=(s // tq,),
        in_specs=[pl.BlockSpec((tq, 1024), lambda i: (i, COL_M // 1024)),
                  pl.BlockSpec((tq, 1024), lambda i: (i, COL_M // 1024 + 1)),
                  full(kn), full(v), full(gq)],
        out_specs=pl.BlockSpec((tq, 1024), lambda i: (i, 0)),
        out_shape=jax.ShapeDtypeStruct((s, 1024), BF16),
        compiler_params=_params(("parallel",)),
    )(p_all, p_all, kn, v, gq)


def _merge_fwd(a_conv, w_c, a_mla, w_m, a_mem, w_e, p_all):
    s = a_conv.shape[0]
    tm, tn = min(TS_ROW, s), 1024
    nj = D_MODEL // tn

    def body(ac_ref, wc_ref, am_ref, wm_ref, ae_ref, we_ref, gc_ref, gm_ref, ge_ref,
             oc_ref, om_ref, oe_ref, mg_ref):
        oc = _dot(ac_ref[...], wc_ref[...])
        om = _dot(am_ref[...], wm_ref[...])
        oe = _dot(ae_ref[...], we_ref[...])
        oc_ref[...] = oc.astype(BF16)
        om_ref[...] = om.astype(BF16)
        oe_ref[...] = oe.astype(BF16)
        mg_ref[...] = (_sigmoid(gc_ref[...].astype(F32)) * oc + _sigmoid(gm_ref[...].astype(F32)) * om
                       + _sigmoid(ge_ref[...].astype(F32)) * oe).astype(BF16)

    act = lambda k: pl.BlockSpec((tm, k), lambda i, j: (i, 0))
    wgt = lambda k: pl.BlockSpec((k, tn), lambda i, j: (0, j))
    gate = lambda b: pl.BlockSpec((tm, tn), lambda i, j: (i, b * nj + j))
    out = pl.BlockSpec((tm, tn), lambda i, j: (i, j))
    return pl.pallas_call(
        body, name="merge_fwd", grid=(s // tm, nj),
        in_specs=[act(1024), wgt(1024), act(2048), wgt(2048), act(1024), wgt(1024), gate(0), gate(1), gate(2)],
        out_specs=[out] * 4,
        out_shape=[jax.ShapeDtypeStruct((s, D_MODEL), BF16)] * 4,
        compiler_params=_params(("parallel", "parallel")),
    )(a_conv, w_c, a_mla, w_m, a_mem, w_e, p_all, p_all, p_all)


def _out_fwd(merged, w_o, x, target):
    s = merged.shape[0]
    tm, tn = min(TS_ROW, s), 1024
    nj = D_MODEL // tn

    def body(mg_ref, w_ref, x_ref, t_ref, dy_ref, dyb_ref, ls_ref):
        e = x_ref[...] + _dot(mg_ref[...], w_ref[...]) - t_ref[...]
        dy = e * (1.0 / D_MODEL)
        dy_ref[...] = dy
        dyb_ref[...] = dy.astype(BF16)
        r = _fold8(e * e)
        acc = r[:, 0:128]
        for cc in range(1, tn // 128):
            acc = acc + r[:, cc * 128:(cc + 1) * 128]
        ls_ref[...] = acc

    tile = pl.BlockSpec((tm, tn), lambda i, j: (i, j))
    return pl.pallas_call(
        body, name="out_fwd", grid=(s // tm, nj),
        in_specs=[pl.BlockSpec((tm, D_MODEL), lambda i, j: (i, 0)),
                  pl.BlockSpec((D_MODEL, tn), lambda i, j: (0, j)), tile, tile],
        out_specs=[tile, tile, pl.BlockSpec((8, 128), lambda i, j: (i, j))],
        out_shape=[jax.ShapeDtypeStruct((s, D_MODEL), F32), jax.ShapeDtypeStruct((s, D_MODEL), BF16),
                   jax.ShapeDtypeStruct((s // tm * 8, nj * 128), F32)],
        compiler_params=_params(("parallel", "parallel")),
    )(merged, w_o, x, target)


def _merge_bwd(dyb, w_o, p_all, o_c, o_m, o_e):
    s = dyb.shape[0]
    tm = min(256, s)

    def body(dy_ref, w_ref, g_ref, oc_ref, om_ref, oe_ref, dc_ref, dm_ref, de_ref, dg_ref):
        dmg = _dot(dy_ref[...], w_ref[...], "nt")
        for b, (o_ref, d_ref) in enumerate(((oc_ref, dc_ref), (om_ref, dm_ref), (oe_ref, de_ref))):
            sl = slice(b * D_MODEL, (b + 1) * D_MODEL)
            sg = _sigmoid(g_ref[:, sl].astype(F32))
            d_ref[...] = (dmg * sg).astype(BF16)
            dg_ref[:, sl] = (dmg * o_ref[...].astype(F32) * sg * (1.0 - sg)).astype(BF16)

    row = pl.BlockSpec((tm, D_MODEL), lambda i: (i, 0))
    wide = pl.BlockSpec((tm, 3 * D_MODEL), lambda i: (i, 0))
    return pl.pallas_call(
        body, name="merge_bwd", grid=(s // tm,),
        in_specs=[row, pl.BlockSpec((D_MODEL, D_MODEL), lambda i: (0, 0)), wide, row, row, row],
        out_specs=[row, row, row, wide],
        out_shape=[jax.ShapeDtypeStruct((s, D_MODEL), BF16)] * 3 + [jax.ShapeDtypeStruct((s, 3 * D_MODEL), BF16)],
        compiler_params=_params(("parallel",)),
    )(dyb, w_o, p_all, o_c, o_m, o_e)


def _conv_bwd(da, p_all, co, conv_w):
    s = da.shape[0]
    ts = min(TS_ROW, s)
    n = s // ts
    c0 = COL_C // 1024

    def body(da_ref, cg_ref, bg_ref, u_ref, z_ref, co_ref, w_ref, dp_ref, dw_ref, carry_ref):
        @pl.when(pl.program_id(0) == 0)
        def _():
            carry_ref[...] = jnp.zeros_like(carry_ref)
            dw_ref[...] = jnp.zeros_like(dw_ref)

        da_ = da_ref[...].astype(F32)
        cg, bg = cg_ref[...].astype(F32), bg_ref[...].astype(F32)
        u, z, cov = u_ref[...].astype(F32), z_ref[...].astype(F32), co_ref[...].astype(F32)
        sz = _sigmoid(z)
        dyc = da_ * (z * sz)
        dz = da_ * (bg * cov) * (sz * (1.0 + z * (1.0 - sz)))
        db = dyc * cov
        dco = dyc * bg
        carry = carry_ref[...]
        d1 = _shift_rows(dco, carry, 1, ts, False)
        d2 = _shift_rows(dco, carry, 2, ts, False)
        carry_ref[...] = dco[:8]
        dpp = w_ref[2:3, :] * dco + w_ref[1:2, :] * d1 + w_ref[0:1, :] * d2
        p = cg * u
        dw_ref[0] += _fold8(p * d2)
        dw_ref[1] += _fold8(p * d1)
        dw_ref[2] += _fold8(p * dco)
        dp_ref[...] = jnp.concatenate([dpp * u, db, dpp * cg, dz], axis=1).astype(BF16)

    rev = lambda c: pl.BlockSpec((ts, 1024), lambda i: (n - 1 - i, c))
    return pl.pallas_call(
        body, name="conv_bwd", grid=(n,),
        in_specs=[rev(0), rev(c0), rev(c0 + 1), rev(c0 + 2), rev(c0 + 3), rev(0),
                  pl.BlockSpec((3, 1024), lambda i: (0, 0))],
        out_specs=[pl.BlockSpec((ts, 4096), lambda i: (n - 1 - i, 0)),
                   pl.BlockSpec((3, 8, 1024), lambda i: (0, 0, 0))],
        out_shape=[jax.ShapeDtypeStruct((s, 4096), BF16), jax.ShapeDtypeStruct((3, 8, 1024), F32)],
        scratch_shapes=[pltpu.VMEM((8, 1024), F32)],
        compiler_params=_params(("arbitrary",)),
    )(da, p_all, p_all, p_all, p_all, co, conv_w)


def _mla_gate_bwd(da, y, p_all):
    s = da.shape[0]
    ts = min(TS_ROW, s)

    def body(da_ref, y_ref, z_ref, dy_ref, dz_ref, dl_ref):
        da_, yv, z = da_ref[...].astype(F32), y_ref[...].astype(F32), z_ref[...].astype(F32)
        sz = _sigmoid(z)
        dyv = da_ * (z * sz)
        dy_ref[...] = dyv.astype(BF16)
        dz_ref[...] = (da_ * yv * (sz * (1.0 + z * (1.0 - sz)))).astype(BF16)
        pr = dyv * yv
        for h in range(MLA_HEADS):
            dl_ref[h] = jnp.broadcast_to(jnp.sum(pr[:, h * 128:(h + 1) * 128], axis=-1, keepdims=True), (ts, 128))

    row = pl.BlockSpec((ts, D_MODEL), lambda i: (i, 0))
    return pl.pallas_call(
        body, name="mla_gate_bwd", grid=(s // ts,),
        in_specs=[row, row, pl.BlockSpec((ts, D_MODEL), lambda i: (i, COL_Z // D_MODEL))],
        out_specs=[row, row, pl.BlockSpec((MLA_HEADS, ts, 128), lambda i: (0, i, 0))],
        out_shape=[jax.ShapeDtypeStruct((s, D_MODEL), BF16)] * 2 + [jax.ShapeDtypeStruct((MLA_HEADS, s, 128), F32)],
        compiler_params=_params(("parallel",)),
    )(da, y, p_all)


def _flash_bwd(qcat, kcat, kv, dy, lse, delta):
    s = qcat.shape[1]
    tq = tk = min(TQ_ATT, s)
    nq = s // tq
    qi, kj = _causal_pairs(nq, True)

    def body(qi_ref, kj_ref, q_ref, k_ref, v_ref, do_ref, lse_ref, dl_ref, dq_ref, dk_ref, dv_ref):
        t = pl.program_id(1)
        i, j = qi_ref[t], kj_ref[t]

        def step(masked):
            q, k = q_ref[0], k_ref[0]
            sc = _dot(q, k, "nt")
            if masked:
                sc = jnp.where(_chunk_mask(tq, tk), sc, NEG)
            p = jnp.exp2(sc - lse_ref[0][:, 0:1])
            do = do_ref[...]
            dpv = _dot(do, v_ref[...], "nt")
            ds = (p * (dpv - dl_ref[0][:, 0:1])).astype(BF16)
            pb = p.astype(BF16)
            dv_new = _dot(pb, do, "tn")
            dk_new = _dot(ds, q, "tn")
            dq_new = _dot(ds, k)
            rows = pl.ds(pl.multiple_of(i * tq, tq), tq)
            if masked:
                dv_ref[0] = dv_new
                dk_ref[0] = dk_new
            else:
                dv_ref[0] += dv_new
                dk_ref[0] += dk_new

            @pl.when(j == 0)
            def _():
                dq_ref[0, rows, :] = dq_new

            @pl.when(j > 0)
            def _():
                dq_ref[0, rows, :] += dq_new

        @pl.when(i == j)
        def _():
            step(True)

        @pl.when(i > j)
        def _():
            step(False)

    qrow = lambda w: pl.BlockSpec((1, tq, w), lambda h, t, qi_ref, kj_ref: (h, qi_ref[t], 0))
    krow = lambda w: pl.BlockSpec((1, tk, w), lambda h, t, qi_ref, kj_ref: (h, kj_ref[t], 0))
    return pl.pallas_call(
        body, name="flash_bwd",
        grid_spec=pltpu.PrefetchScalarGridSpec(
            num_scalar_prefetch=2, grid=(MLA_HEADS, qi.shape[0]),
            in_specs=[qrow(HEAD_PAD), krow(HEAD_PAD),
                      pl.BlockSpec((tk, 128), lambda h, t, qi_ref, kj_ref: (kj_ref[t], 2 * h + 1)),
                      pl.BlockSpec((tq, 128), lambda h, t, qi_ref, kj_ref: (qi_ref[t], h)),
                      qrow(128), qrow(128)],
            out_specs=[pl.BlockSpec((1, s, HEAD_PAD), lambda h, t, qi_ref, kj_ref: (h, 0, 0)),
                       krow(HEAD_PAD), krow(128)]),
        out_shape=[jax.ShapeDtypeStruct((MLA_HEADS, s, HEAD_PAD), F32),
                   jax.ShapeDtypeStruct((MLA_HEADS, s, HEAD_PAD), F32),
                   jax.ShapeDtypeStruct((MLA_HEADS, s, 128), F32)],
        compiler_params=_params(("parallel", "arbitrary")),
    )(qi, kj, qcat, kcat, kv, dy, lse, delta)


def _q_prep_bwd(dqcat, q_raw, gn, gr, cs, sg):
    s = q_raw.shape[0]
    ts = min(TS_BWD, s)

    def body(dq_ref, q_ref, gn_ref, gr_ref, cs_ref, sg_ref, o_ref, dg_ref):
        @pl.when(pl.program_id(0) == 0)
        def _():
            dg_ref[...] = jnp.zeros_like(dg_ref)

        ga_acc = jnp.zeros((8, 128), F32)
        gb_acc = jnp.zeros((8, 128), F32)
        for h in range(MLA_HEADS):
            cols = slice(h * HEAD_PAD, (h + 1) * HEAD_PAD)
            q = q_ref[:, cols].astype(F32)
            d = dq_ref[h] * MLA_SCALE
            da, ga = _rms_bwd(q[:, :128], gn_ref[...], d[:, :128], QK_NOPE)
            db, gb = _rms_bwd(q[:, 128:], gr_ref[...], _rope_bwd(d[:, 128:], cs_ref[...], sg_ref[...]), QK_ROPE)
            o_ref[:, cols] = jnp.concatenate([da, db], axis=1).astype(BF16)
            ga_acc = ga_acc + _fold8(ga)
            gb_acc = gb_acc + _fold8(gb)
        dg_ref[0] += ga_acc
        dg_ref[1] += gb_acc

    wide = pl.BlockSpec((ts, MLA_HEADS * HEAD_PAD), lambda i: (i, 0))
    return pl.pallas_call(
        body, name="q_prep_bwd", grid=(s // ts,),
        in_specs=[pl.BlockSpec((MLA_HEADS, ts, HEAD_PAD), lambda i: (0, i, 0)), wide,
                  pl.BlockSpec((1, 128), lambda i: (0, 0)), pl.BlockSpec((1, 128), lambda i: (0, 0)),
                  pl.BlockSpec((ts, 128), lambda i: (i, 0)), pl.BlockSpec((ts, 128), lambda i: (i, 0))],
        out_specs=[wide, pl.BlockSpec((2, 8, 128), lambda i: (0, 0, 0))],
        out_shape=[jax.ShapeDtypeStruct((s, MLA_HEADS * HEAD_PAD), BF16), jax.ShapeDtypeStruct((2, 8, 128), F32)],
        compiler_params=_params(("arbitrary",)),
    )(dqcat, q_raw, gn, gr, cs, sg)


def _k_prep_bwd(dkcat, dv, kv, gk):
    s = kv.shape[0]
    ts = min(TS_BWD, s)

    def body(dk_ref, dv_ref, k_ref, gk_ref, o_ref, dkr_ref, dg_ref):
        @pl.when(pl.program_id(0) == 0)
        def _():
            dg_ref[...] = jnp.zeros_like(dg_ref)

        g_acc = jnp.zeros((8, 128), F32)
        dkr = jnp.zeros((ts, 128), F32)
        for h in range(MLA_HEADS):
            d = dk_ref[h] * LN2
            k = k_ref[:, h * HEAD_PAD:h * HEAD_PAD + 128].astype(F32)
            dk_raw, gg = _rms_bwd(k, gk_ref[...], d[:, :128], QK_NOPE)
            o_ref[:, h * HEAD_PAD:(h + 1) * HEAD_PAD] = jnp.concatenate([dk_raw, dv_ref[h]], axis=1).astype(BF16)
            g_acc = g_acc + _fold8(gg)
            dkr = dkr + d[:, 128:]
        dkr_ref[...] = dkr
        dg_ref[...] += g_acc

    wide = pl.BlockSpec((ts, MLA_HEADS * HEAD_PAD), lambda i: (i, 0))
    return pl.pallas_call(
        body, name="k_prep_bwd", grid=(s // ts,),
        in_specs=[pl.BlockSpec((MLA_HEADS, ts, HEAD_PAD), lambda i: (0, i, 0)),
                  pl.BlockSpec((MLA_HEADS, ts, 128), lambda i: (0, i, 0)), wide,
                  pl.BlockSpec((1, 128), lambda i: (0, 0))],
        out_specs=[wide, pl.BlockSpec((ts, 128), lambda i: (i, 0)), pl.BlockSpec((8, 128), lambda i: (0, 0))],
        out_shape=[jax.ShapeDtypeStruct((s, MLA_HEADS * HEAD_PAD), BF16), jax.ShapeDtypeStruct((s, 128), F32),
                   jax.ShapeDtypeStruct((8, 128), F32)],
        compiler_params=_params(("arbitrary",)),
    )(dkcat, dv, kv, gk)


def _lora_bwd(dcqn, dckvn, dkr, p_all, gq, gkv, gkr, cs, sg):
    s = p_all.shape[0]
    ts = min(TS_ROW, s)

    def body(dq_ref, dkv_ref, dkr_ref, cq_ref, ckv_ref, kr_ref, gq_ref, gkv_ref, gkr_ref, cs_ref, sg_ref,
             o_ref, dgq_ref, dgkv_ref, dgkr_ref):
        @pl.when(pl.program_id(0) == 0)
        def _():
            dgq_ref[...] = jnp.zeros_like(dgq_ref)
            dgkv_ref[...] = jnp.zeros_like(dgkv_ref)
            dgkr_ref[...] = jnp.zeros_like(dgkr_ref)

        dq, g1 = _rms_bwd(cq_ref[...].astype(F32), gq_ref[...], dq_ref[...], 512)
        dkv, g2 = _rms_bwd(ckv_ref[...].astype(F32), gkv_ref[...], dkv_ref[...], 512)
        dkn = _rope_bwd(dkr_ref[...], cs_ref[...], sg_ref[...])
        dk, g3 = _rms_bwd(kr_ref[...].astype(F32), gkr_ref[...], dkn, QK_ROPE)
        o_ref[...] = jnp.concatenate([dq, dkv, dk, jnp.zeros((ts, 2048 - 1152), F32)], axis=1).astype(BF16)
        dgq_ref[...] += _fold8(g1)
        dgkv_ref[...] += _fold8(g2)
        dgkr_ref[...] += _fold8(g3)

    row = lambda w: pl.BlockSpec((ts, w), lambda i: (i, 0))
    vec = lambda w: pl.BlockSpec((1, w), lambda i: (0, 0))
    acc = lambda w: pl.BlockSpec((8, w), lambda i: (0, 0))
    return pl.pallas_call(
        body, name="lora_bwd", grid=(s // ts,),
        in_specs=[row(512), row(512), row(128),
                  pl.BlockSpec((ts, 512), lambda i: (i, COL_S // 512)),
                  pl.BlockSpec((ts, 512), lambda i: (i, COL_S // 512 + 1)),
                  pl.BlockSpec((ts, 128), lambda i: (i, (COL_S + 1024) // 128)),
                  vec(512), vec(512), vec(128), row(128), row(128)],
        out_specs=[row(2048), acc(512), acc(512), acc(128)],
        out_shape=[jax.ShapeDtypeStruct((s, 2048), BF16), jax.ShapeDtypeStruct((8, 512), F32),
                   jax.ShapeDtypeStruct((8, 512), F32), jax.ShapeDtypeStruct((8, 128), F32)],
        compiler_params=_params(("arbitrary",)),
    )(dcqn, dckvn, dkr, p_all, p_all, p_all, gq, gkv, gkr, cs, sg)


def _mem_attn_bwd(da, p_all, kn, v, gq):
    s = p_all.shape[0]
    tq = min(TS_ROW, s)

    def body(da_ref, q_ref, z_ref, kn_ref, v_ref, gq_ref, o_ref, dkn_ref, dv_ref, dg_ref):
        @pl.when(pl.program_id(0) == 0)
        def _():
            dkn_ref[...] = jnp.zeros_like(dkn_ref)
            dv_ref[...] = jnp.zeros_like(dv_ref)
            dg_ref[...] = jnp.zeros_like(dg_ref)

        z = z_ref[...].astype(F32)
        da_ = da_ref[...].astype(F32)
        sz = _sigmoid(z)
        gate = z * sz
        dgate = sz * (1.0 + z * (1.0 - sz))
        for hh in range(MEM_HEADS):
            sl = slice(hh * MEM_HEAD_DIM, (hh + 1) * MEM_HEAD_DIM)
            qf = q_ref[:, sl].astype(F32)
            qn = _rms_fwd(qf, gq_ref[...], MEM_HEAD_DIM).astype(BF16)
            knh, vh = kn_ref[:, sl], v_ref[:, sl]
            sc = _dot(qn, knh, "nt") * MEM_SCALE
            e = jnp.exp(sc - jnp.max(sc, axis=-1, keepdims=True))
            p = e * (1.0 / jnp.sum(e, axis=-1, keepdims=True))
            pb = p.astype(BF16)
            y = _dot(pb, vh)
            dyh = da_[:, sl] * gate[:, sl]
            o_ref[:, 1024 + hh * MEM_HEAD_DIM:1024 + (hh + 1) * MEM_HEAD_DIM] = (
                da_[:, sl] * y * dgate[:, sl]).astype(BF16)
            dyb = dyh.astype(BF16)
            dpm = _dot(dyb, vh, "nt")
            ds = (p * (dpm - jnp.sum(dpm * p, axis=-1, keepdims=True)) * MEM_SCALE).astype(BF16)
            dqn = _dot(ds, knh)
            dkn_ref[:, sl] += _dot(ds, qn, "tn")
            dv_ref[:, sl] += _dot(pb, dyb, "tn")
            dq, gg = _rms_bwd(qf, gq_ref[...], dqn, MEM_HEAD_DIM)
            o_ref[:, sl] = dq.astype(BF16)
            dg_ref[...] += _fold8(gg)

    full = lambda a: pl.BlockSpec(a.shape, lambda i: (0, 0))
    return pl.pallas_call(
        body, name="mem_attn_bwd", grid=(s // tq,),
        in_specs=[pl.BlockSpec((tq, 1024), lambda i: (i, 0)),
                  pl.BlockSpec((tq, 1024), lambda i: (i, COL_M // 1024)),
                  pl.BlockSpec((tq, 1024), lambda i: (i, COL_M // 1024 + 1)),
                  full(kn), full(v), full(gq)],
        out_specs=[pl.BlockSpec((tq, 2048), lambda i: (i, 0)),
                   pl.BlockSpec((MEM_TOKENS, 1024), lambda i: (0, 0)),
                   pl.BlockSpec((MEM_TOKENS, 1024), lambda i: (0, 0)),
                   pl.BlockSpec((8, MEM_HEAD_DIM), lambda i: (0, 0))],
        out_shape=[jax.ShapeDtypeStruct((s, 2048), BF16), jax.ShapeDtypeStruct((MEM_TOKENS, 1024), F32),
                   jax.ShapeDtypeStruct((MEM_TOKENS, 1024), F32), jax.ShapeDtypeStruct((8, MEM_HEAD_DIM), F32)],
        compiler_params=_params(("arbitrary",)),
    )(da, p_all, p_all, kn, v, gq)


def _mem_kv_bwd(dkn, dv, kraw, memn, mem, g_norm, w_kv, g_kn):
    m = mem.shape[0]

    def body(dkn_ref, dv_ref, kraw_ref, memn_ref, mem_ref, g_ref, w_ref, gk_ref, dw_ref, dgn_ref, dgk_ref, dkv_sc):
        gk_acc = jnp.zeros((8, MEM_HEAD_DIM), F32)
        for hh in range(MEM_HEADS):
            sl = slice(hh * MEM_HEAD_DIM, (hh + 1) * MEM_HEAD_DIM)
            dk, gg = _rms_bwd(kraw_ref[:, sl], gk_ref[...], dkn_ref[:, sl], MEM_HEAD_DIM)
            dkv_sc[:, sl] = dk.astype(BF16)
            gk_acc = gk_acc + _fold8(gg)
        dgk_ref[...] = gk_acc
        dkv_sc[:, 1024:] = dv_ref[...].astype(BF16)
        dkv = dkv_sc[...]
        dw_ref[...] = _dot(memn_ref[...], dkv, "tn")
        dmemn = _dot(dkv, w_ref[...], "nt")
        xf = mem_ref[...]
        r = lax.rsqrt(jnp.mean(xf * xf, axis=-1, keepdims=True) + EPS)
        dgn_ref[...] = _fold8(dmemn * (xf * r))

    return pl.pallas_call(
        body, name="mem_kv_bwd",
        out_shape=[jax.ShapeDtypeStruct((D_MODEL, D_MODEL), F32), jax.ShapeDtypeStruct((8, D_MODEL), F32),
                   jax.ShapeDtypeStruct((8, MEM_HEAD_DIM), F32)],
        scratch_shapes=[pltpu.VMEM((m, D_MODEL), BF16)],
        compiler_params=_params(),
    )(dkn, dv, kraw, memn, mem, g_norm, w_kv, g_kn)


def _dh_bwd(dps, w_all):
    s = dps[0].shape[0]
    tm, tk = min(TS_ROW, s), 1024
    widths = [a.shape[1] // tk for a in dps]
    starts = [int(v) for v in np.cumsum([0] + widths[:-1])]
    nk = sum(widths)

    def body(*refs):
        d_refs, w_ref, o_ref, acc_ref = refs[:5], refs[5], refs[6], refs[7]
        kk = pl.program_id(1)

        @pl.when(kk == 0)
        def _():
            acc_ref[...] = jnp.zeros_like(acc_ref)

        for d_ref, st, wd in zip(d_refs, starts, widths):
            @pl.when((kk >= st) & (kk < st + wd))
            def _(d_ref=d_ref):
                acc_ref[...] += _dot(d_ref[...], w_ref[...], "nt")

        @pl.when(kk == nk - 1)
        def _():
            o_ref[...] = acc_ref[...]

    def dspec(st, wd):
        return pl.BlockSpec((tm, tk), lambda i, kk: (i, jnp.clip(kk - st, 0, wd - 1)))

    return pl.pallas_call(
        body, name="dh_bwd", grid=(s // tm, nk),
        in_specs=[dspec(st, wd) for st, wd in zip(starts, widths)]
        + [pl.BlockSpec((D_MODEL, tk), lambda i, kk: (0, kk))],
        out_specs=pl.BlockSpec((tm, D_MODEL), lambda i, kk: (i, 0)),
        out_shape=jax.ShapeDtypeStruct((s, D_MODEL), F32),
        scratch_shapes=[pltpu.VMEM((tm, D_MODEL), F32)],
        compiler_params=_params(("parallel", "arbitrary")),
    )(*dps, w_all)


def _dx_bwd(dh, x, dy, g):
    s = x.shape[0]
    ts = min(256, s)

    def body(dh_ref, x_ref, dy_ref, g_ref, o_ref, dg_ref):
        @pl.when(pl.program_id(0) == 0)
        def _():
            dg_ref[...] = jnp.zeros_like(dg_ref)

        dx, gg = _rms_bwd(x_ref[...], g_ref[...], dh_ref[...], D_MODEL)
        o_ref[...] = dy_ref[...] + dx
        dg_ref[...] += _fold8(gg)

    row = pl.BlockSpec((ts, D_MODEL), lambda i: (i, 0))
    return pl.pallas_call(
        body, name="dx_bwd", grid=(s // ts,),
        in_specs=[row, row, row, pl.BlockSpec((1, D_MODEL), lambda i: (0, 0))],
        out_specs=[row, pl.BlockSpec((8, D_MODEL), lambda i: (0, 0))],
        out_shape=[jax.ShapeDtypeStruct((s, D_MODEL), F32), jax.ShapeDtypeStruct((8, D_MODEL), F32)],
        compiler_params=_params(("arbitrary",)),
    )(dh, x, dy, g)


def _pad128(v, n):
    return jnp.pad(v.reshape(1, n), ((0, 0), (0, 128 - n)))


def _local_step(x, positions, mem, target, wts, gains):
    half = QK_ROPE // 2
    inv_freq = jnp.power(ROPE_THETA, -jnp.arange(half, dtype=F32) / half)
    ang = positions.astype(F32)[:, None] * inv_freq
    cos, sin = jnp.cos(ang), jnp.sin(ang)
    zpad = jnp.zeros((x.shape[0], 64), F32)
    cs = jnp.concatenate([cos, cos, zpad], axis=1)
    sg = jnp.concatenate([-sin, sin, zpad], axis=1)
    g_qr, g_kr = _pad128(gains["mla_qn_rope_g"], 64), _pad128(gains["mla_kn_rope_g"], 64)
    g_qn, g_kn = gains["mla_qn_nope_g"], gains["mla_kn_nope_g"]

    p_all, h = _proj_fwd(x, gains["norm_g"], wts["w_all"])
    a_conv, co = _conv_fwd(p_all, wts["conv_w"])
    cqn, ckvn, krope = _lora_fwd(p_all, gains["mla_q_norm_g"], gains["mla_kv_norm_g"], g_kr, cs, sg)
    q_raw = _matmul(cqn, wts["w_uq"], "nn", BF16, "q_up")
    kv = _matmul(ckvn, wts["w_ukv"], "nn", BF16, "kv_up")
    qcat = _q_prep(q_raw, g_qn, g_qr, cs, sg)
    kcat, vt = _k_prep(kv, krope, g_kn)
    mla_y, a_mla, lse = _flash_fwd(qcat, kcat, vt, p_all)
    memn, kraw, kn, vmem = _mem_kv_fwd(mem, gains["mem_norm_g"], wts["w_mem_kv"], gains["mem_kn_g"])
    a_mem = _mem_attn_fwd(p_all, kn, vmem, gains["mem_qn_g"])
    o_c, o_m, o_e, merged = _merge_fwd(a_conv, wts["w_conv_out"], a_mla, wts["w_mla_out"], a_mem,
                                       wts["w_mem_out"], p_all)
    dy, dyb, loss_parts = _out_fwd(merged, wts["w_o"], x, target)

    gw = {}
    gw["w_o"] = _matmul(merged, dyb, "tn", F32, "dw_o")
    do_c, do_m, do_e, dp_g = _merge_bwd(dyb, wts["w_o"], p_all, o_c, o_m, o_e)
    gw["w_conv_out"] = _matmul(a_conv, do_c, "tn", F32, "dw_conv_out")
    gw["w_mla_out"] = _matmul(a_mla, do_m, "tn", F32, "dw_mla_out")
    gw["w_mem_out"] = _matmul(a_mem, do_e, "tn", F32, "dw_mem_out")
    da_conv = _matmul(do_c, wts["w_conv_out"], "nt", BF16, "da_conv")
    da_mla = _matmul(do_m, wts["w_mla_out"], "nt", BF16, "da_mla")
    da_mem = _matmul(do_e, wts["w_mem_out"], "nt", BF16, "da_mem")
    dp_c, dconv_w = _conv_bwd(da_conv, p_all, co, wts["conv_w"])
    dmla_y, dp_z, delta = _mla_gate_bwd(da_mla, mla_y, p_all)
    dqcat, dkcat, dv = _flash_bwd(qcat, kcat, kv, dmla_y, lse, delta)
    dq_raw, dg_q = _q_prep_bwd(dqcat, q_raw, g_qn, g_qr, cs, sg)
    dkv, dkr, dg_kn = _k_prep_bwd(dkcat, dv, kv, g_kn)
    gw["w_uq"] = _matmul(cqn, dq_raw, "tn", F32, "dw_uq")
    gw["w_ukv"] = _matmul(ckvn, dkv, "tn", F32, "dw_ukv")
    dcqn = _matmul(dq_raw, wts["w_uq"], "nt", F32, "dcqn")
    dckvn = _matmul(dkv, wts["w_ukv"], "nt", F32, "dckvn")
    dp_s, dg_qn, dg_kvn, dg_kr = _lora_bwd(dcqn, dckvn, dkr, p_all, gains["mla_q_norm_g"],
                                            gains["mla_kv_norm_g"], g_kr, cs, sg)
    dp_m, dkn, dvm, dg_mq = _mem_attn_bwd(da_mem, p_all, kn, vmem, gains["mem_qn_g"])
    gw["w_mem_kv"], dg_mn, dg_mk = _mem_kv_bwd(dkn, dvm, kraw, memn, mem, gains["mem_norm_g"],
                                               wts["w_mem_kv"], gains["mem_kn_g"])
    dps = [dp_g, dp_z, dp_c, dp_m, dp_s]
    dh = _dh_bwd(dps, wts["w_all"])
    grad_x, dg_n = _dx_bwd(dh, x, dy, gains["norm_g"])
    gw["w_all"] = [_matmul(h, d, "tn", F32, "dw_in_%d" % k) for k, d in enumerate(dps)]

    gsmall = {
        "norm_g": dg_n.sum(0), "conv_w": dconv_w.sum(1), "mla_q_norm_g": dg_qn.sum(0),
        "mla_kv_norm_g": dg_kvn.sum(0), "mla_qn_nope_g": dg_q[0].sum(0), "mla_qn_rope_g": dg_q[1].sum(0)[:64],
        "mla_kn_nope_g": dg_kn.sum(0), "mla_kn_rope_g": dg_kr.sum(0)[:64], "mem_norm_g": dg_mn.sum(0),
        "mem_qn_g": dg_mq.sum(0), "mem_kn_g": dg_mk.sum(0),
    }
    return loss_parts, grad_x, gw, gsmall


def _me():
    return lax.axis_index("x"), lax.axis_index("y"), lax.axis_index("c")


def _chips(x, y):
    return [(1 - x, y), (x, 1 - y), (1 - x, 1 - y)]


ANY = pl.BlockSpec(memory_space=pl.ANY)


def _allgather_shards(xs, name):
    r, c = xs.shape
    hr = r // 2

    def body(x_ref, out_ref, send_sems, recv_sems):
        x, y, cc = _me()
        sibling = (x, y, 1 - cc)
        chips = _chips(x, y)

        def rows(px, py, pc):
            return out_ref.at[pl.ds((4 * px + 2 * py + pc) * hr, hr), :]

        def copy(k, block, to, src=None):
            return pltpu.make_async_remote_copy(
                src_ref=rows(*block) if src is None else src, dst_ref=rows(*block),
                send_sem=send_sems.at[k], recv_sem=recv_sems.at[k], device_id=to, device_id_type=MESH)

        my_half = x_ref.at[pl.ds(cc * hr, hr), :]
        first = [copy(j, (x, y, cc), (*chip, cc), src=my_half) for j, chip in enumerate(chips)]
        for cp in first:
            cp.start()
        passed = [copy(3 + j, (*chip, cc), sibling) for j, chip in enumerate(chips)]
        for j, chip in enumerate(chips):
            copy(j, (*chip, cc), (x, y, cc)).wait_recv()
            passed[j].start()
        for j, chip in enumerate(chips):
            copy(3 + j, (*chip, 1 - cc), (x, y, cc)).wait_recv()
        for cp in first + passed:
            cp.wait_send()

    out = pl.pallas_call(
        body, name=name, in_specs=[ANY], out_specs=ANY,
        out_shape=jax.ShapeDtypeStruct((8 * hr, c), xs.dtype),
        scratch_shapes=[pltpu.SemaphoreType.DMA((6,)), pltpu.SemaphoreType.DMA((6,))],
    )(xs)
    xi, yi, _ = _me()
    return lax.dynamic_update_slice(out.reshape(4, r, c), xs[None], (2 * xi + yi, 0, 0))


def _swap_sibling(src, name, by_half):
    def body(src_ref, got_ref, send_sem, recv_sem):
        x, y, cc = _me()
        cp = pltpu.make_async_remote_copy(src_ref=src_ref.at[1 - cc] if by_half else src_ref, dst_ref=got_ref,
                                          send_sem=send_sem, recv_sem=recv_sem, device_id=(x, y, 1 - cc),
                                          device_id_type=MESH)
        cp.start()
        cp.wait()

    return pl.pallas_call(
        body, name=name, in_specs=[ANY], out_specs=ANY,
        out_shape=jax.ShapeDtypeStruct(src.shape[1:] if by_half else src.shape, src.dtype),
        scratch_shapes=[pltpu.SemaphoreType.DMA, pltpu.SemaphoreType.DMA],
    )(src)


def _exchange_chips(t, name):
    def body(t_ref, r_ref, send_sems, recv_sems):
        x, y, cc = _me()
        cps = []
        for j, (px, py) in enumerate(_chips(x, y)):
            cps.append(pltpu.make_async_remote_copy(
                src_ref=t_ref.at[2 * px + py], dst_ref=r_ref.at[j], send_sem=send_sems.at[j],
                recv_sem=recv_sems.at[j], device_id=(px, py, cc), device_id_type=MESH))
        for cp in cps:
            cp.start()
        for cp in cps:
            cp.wait()

    return pl.pallas_call(
        body, name=name, in_specs=[ANY], out_specs=ANY,
        out_shape=jax.ShapeDtypeStruct((3,) + t.shape[1:], t.dtype),
        scratch_shapes=[pltpu.SemaphoreType.DMA((3,)), pltpu.SemaphoreType.DMA((3,))],
    )(t)


def _row_tile(hr, c, itemsize):
    t = hr
    while t > 16 and t * c * itemsize > (2 << 20):
        t //= 2
    return t


def _add_own_half(gb, got, sel, name):
    _, _, hr, c = gb.shape
    tr = _row_tile(hr, c, 4)

    def body(sel_ref, a_ref, b_ref, o_ref):
        o_ref[...] = (a_ref[0].astype(F32) + b_ref[...].astype(F32)).astype(BF16)

    return pl.pallas_call(
        body, name=name,
        grid_spec=pltpu.PrefetchScalarGridSpec(
            num_scalar_prefetch=1, grid=(4, hr // tr),
            in_specs=[pl.BlockSpec((1, 1, tr, c), lambda k, i, sel_ref: (sel_ref[0], k, i, 0)),
                      pl.BlockSpec((1, tr, c), lambda k, i, sel_ref: (k, i, 0))],
            out_specs=pl.BlockSpec((1, tr, c), lambda k, i, sel_ref: (k, i, 0))),
        out_shape=jax.ShapeDtypeStruct(got.shape, BF16),
        compiler_params=_params(("parallel", "parallel")),
    )(sel, gb, got)


def _sum_partials(t, rcv, sel, name):
    _, hr, c = t.shape
    tr = _row_tile(hr, c, 4)

    def body(sel_ref, t_ref, r_ref, o_ref):
        o_ref[...] = ((t_ref[0].astype(F32) + r_ref[0].astype(F32)) + r_ref[1].astype(F32)) + r_ref[2].astype(F32)

    return pl.pallas_call(
        body, name=name,
        grid_spec=pltpu.PrefetchScalarGridSpec(
            num_scalar_prefetch=1, grid=(hr // tr,),
            in_specs=[pl.BlockSpec((1, tr, c), lambda i, sel_ref: (sel_ref[0], i, 0)),
                      pl.BlockSpec((3, tr, c), lambda i, sel_ref: (0, i, 0))],
            out_specs=pl.BlockSpec((tr, c), lambda i, sel_ref: (i, 0))),
        out_shape=jax.ShapeDtypeStruct((hr, c), F32),
        compiler_params=_params(("parallel",)),
    )(sel, t, rcv)


def _reduce_scatter(g_sharded, tag):
    _, r, c = g_sharded.shape
    hr = r // 2
    x, y, cc = _me()
    gb = g_sharded.astype(BF16).reshape(4, 2, hr, c).transpose(1, 0, 2, 3)
    got = _swap_sibling(gb, "rs_swap_" + tag, True)
    t = _add_own_half(gb, got, jnp.reshape(cc, (1,)).astype(jnp.int32), "rs_add_" + tag)
    rcv = _exchange_chips(t, "rs_ici_" + tag)
    fh = _sum_partials(t, rcv, jnp.reshape(2 * x + y, (1,)).astype(jnp.int32), "rs_sum_" + tag)
    return fh, _swap_sibling(fh, "rs_join_" + tag, False)


def _allreduce_small(v, tag):
    r = v.shape[0]

    def body(v_ref, out_ref, buf_ref, send_sems, recv_sems):
        x, y, cc = _me()
        me = 4 * x + 2 * y + cc
        buf_ref[pl.ds(pl.multiple_of(me * r, 8), r), :] = v_ref[...]
        peers = [(x, y, 1 - cc)] + [(px, py, pc) for (px, py) in _chips(x, y) for pc in (cc, 1 - cc)]
        cps = []
        for k, (px, py, pc) in enumerate(peers):
            mine = buf_ref.at[pl.ds(pl.multiple_of(me * r, 8), r), :]
            cps.append(pltpu.make_async_remote_copy(
                src_ref=v_ref, dst_ref=mine, send_sem=send_sems.at[k], recv_sem=recv_sems.at[k],
                device_id=(px, py, pc), device_id_type=MESH))
        for cp in cps:
            cp.start()
        for cp in cps:
            cp.wait()
        acc = buf_ref[0:r, :]
        for d in range(1, 8):
            acc = acc + buf_ref[d * r:(d + 1) * r, :]
        out_ref[...] = acc

    vm = pl.BlockSpec(memory_space=pltpu.VMEM)
    return pl.pallas_call(
        body, name="allreduce_small_" + tag, in_specs=[vm], out_specs=vm,
        out_shape=jax.ShapeDtypeStruct((r, 128), F32),
        scratch_shapes=[pltpu.VMEM((8 * r, 128), F32), pltpu.SemaphoreType.DMA((7,)), pltpu.SemaphoreType.DMA((7,))],
    )(v)


def _adamw(w, g, m, v, name):
    r, c = w.shape
    tr = _row_tile(r, c, 4) if r >= 16 else r
    c1 = 1.0 / (1.0 - ADAM_B1 ** ADAM_STEP)
    c2 = 1.0 / (1.0 - ADAM_B2 ** ADAM_STEP)

    def body(w_ref, g_ref, m_ref, v_ref, d_ref, mo_ref, vo_ref):
        gg = g_ref[...]
        mn = ADAM_B1 * m_ref[...] + (1.0 - ADAM_B1) * gg
        vn = ADAM_B2 * v_ref[...] + (1.0 - ADAM_B2) * (gg * gg)
        mo_ref[...] = mn
        vo_ref[...] = vn
        d_ref[...] = -ADAM_LR * ((mn * c1) / (jnp.sqrt(vn * c2) + ADAM_EPS) + ADAM_WD * w_ref[...])

    blk = pl.BlockSpec((tr, c), lambda i: (i, 0))
    return pl.pallas_call(
        body, name=name, grid=(r // tr,), in_specs=[blk] * 4, out_specs=[blk] * 3,
        out_shape=[jax.ShapeDtypeStruct((r, c), F32)] * 3,
        compiler_params=_params(("parallel",)),
    )(w, g, m, v)


def _adamw_halves(w, g_own, g_other, m, v, sel, name):
    r, c = w.shape
    hr = r // 2
    tr = _row_tile(hr, c, 4)
    nh = hr // tr
    c1 = 1.0 / (1.0 - ADAM_B1 ** ADAM_STEP)
    c2 = 1.0 / (1.0 - ADAM_B2 ** ADAM_STEP)

    def body(sel_ref, w_ref, go_ref, gx_ref, m_ref, v_ref, d_ref, mo_ref, vo_ref, g_ref):
        mine = (pl.program_id(0) // nh) == sel_ref[0]
        gg = jnp.where(mine, go_ref[...], gx_ref[...])
        g_ref[...] = gg
        mn = ADAM_B1 * m_ref[...] + (1.0 - ADAM_B1) * gg
        vn = ADAM_B2 * v_ref[...] + (1.0 - ADAM_B2) * (gg * gg)
        mo_ref[...] = mn
        vo_ref[...] = vn
        d_ref[...] = -ADAM_LR * ((mn * c1) / (jnp.sqrt(vn * c2) + ADAM_EPS) + ADAM_WD * w_ref[...])

    blk = pl.BlockSpec((tr, c), lambda i, sel_ref: (i, 0))
    half = pl.BlockSpec((tr, c), lambda i, sel_ref: (i % nh, 0))
    return pl.pallas_call(
        body, name=name,
        grid_spec=pltpu.PrefetchScalarGridSpec(
            num_scalar_prefetch=1, grid=(r // tr,),
            in_specs=[blk, half, half, blk, blk], out_specs=[blk] * 4),
        out_shape=[jax.ShapeDtypeStruct((r, c), F32)] * 4,
        compiler_params=_params(("parallel",)),
    )(sel, w, g_own, g_other, m, v)


WEIGHTS = ['norm_g', 'w_in', 'conv_w', 'w_conv_out', 'mla_q_norm_g', 'w_uq', 'mla_kv_norm_g', 'w_ukv',
           'mla_qn_nope_g', 'mla_qn_rope_g', 'mla_kn_nope_g', 'mla_kn_rope_g', 'w_mla_out', 'mem_norm_g',
           'w_mem_kv', 'mem_qn_g', 'mem_kn_g', 'w_mem_out', 'w_o']
COL_SHARDED = ['w_in', 'w_conv_out', 'w_uq', 'w_ukv', 'w_mem_out']
ROW_SHARDED = ['w_mla_out', 'w_mem_kv', 'w_o']
SMALL = ['norm_g', 'conv_w', 'mla_q_norm_g', 'mla_kv_norm_g', 'mla_qn_nope_g', 'mla_qn_rope_g', 'mla_kn_nope_g',
         'mla_kn_rope_g', 'mem_norm_g', 'mem_qn_g', 'mem_kn_g']
SMALL_SIZES = [2048, 3072, 512, 512, 128, 64, 128, 64, 2048, 256, 256]
PACK_ROWS = 72


def _gather_weights(shards):
    full = {}
    for n in COL_SHARDED:
        g = _allgather_shards(shards[n].astype(BF16), "ag_" + n)
        full[n] = g.transpose(1, 0, 2).reshape(g.shape[1], 4 * g.shape[2])
    for n in ROW_SHARDED:
        g = _allgather_shards(shards[n].astype(BF16), "ag_" + n)
        full[n] = g.reshape(4 * g.shape[1], g.shape[2])
    return full


def _to_local_layouts(full):
    w_in = full["w_in"]
    conv, small, mz, memq, gates = (w_in[:, 0:4096], w_in[:, 4096:5184], w_in[:, 5184:7232],
                                    w_in[:, 7232:9280], w_in[:, 9280:15424])
    wts = dict(full)
    wts["w_all"] = jnp.concatenate([gates, mz, conv, memq, small, jnp.zeros((D_MODEL, 2048 - 1088), BF16)], axis=1)
    wts["w_uq"] = jnp.pad(full["w_uq"].reshape(512, MLA_HEADS, 192), ((0, 0), (0, 0), (0, 64))).reshape(512, 4096)
    del wts["w_in"]
    return wts


def _grads_to_reference_layout(gw):
    dw_g, dw_z, dw_c, dw_m, dw_s = gw["w_all"]
    out = {k: v for k, v in gw.items() if k != "w_all"}
    out["w_in"] = jnp.concatenate([dw_c, dw_s[:, :1088], dw_z, dw_m, dw_g], axis=1)
    out["w_uq"] = gw["w_uq"].reshape(512, MLA_HEADS, HEAD_PAD)[:, :, :192].reshape(512, 3072)
    return out


def kernel(x, positions, mem, norm_g, w_in, conv_w, w_conv_out, mla_q_norm_g, w_uq, mla_kv_norm_g, w_ukv, mla_qn_nope_g, mla_qn_rope_g, mla_kn_nope_g, mla_kn_rope_g, w_mla_out, mem_norm_g, w_mem_kv, mem_qn_g, mem_kn_g, w_mem_out, w_o, loss_target, m_norm_g, m_w_in, m_conv_w, m_w_conv_out, m_mla_q_norm_g, m_w_uq, m_mla_kv_norm_g, m_w_ukv, m_mla_qn_nope_g, m_mla_qn_rope_g, m_mla_kn_nope_g, m_mla_kn_rope_g, m_w_mla_out, m_mem_norm_g, m_w_mem_kv, m_mem_qn_g, m_mem_kn_g, m_w_mem_out, m_w_o, v_norm_g, v_w_in, v_conv_w, v_w_conv_out, v_mla_q_norm_g, v_w_uq, v_mla_kv_norm_g, v_w_ukv, v_mla_qn_nope_g, v_mla_qn_rope_g, v_mla_kn_nope_g, v_mla_kn_rope_g, v_w_mla_out, v_mem_norm_g, v_w_mem_kv, v_mem_qn_g, v_mem_kn_g, v_w_mem_out, v_w_o):
    args = locals()
    w = {n: args[n][0] for n in WEIGHTS}
    m1 = {n: args["m_" + n][0] for n in WEIGHTS}
    v2 = {n: args["v_" + n][0] for n in WEIGHTS}
    xi, yi, ci = _me()
    chip = 2 * xi + yi

    conv_slot = jnp.zeros((3, 4, 256), F32).at[:, chip, :].set(w["conv_w"] * jnp.where(ci == 0, 1.0, 0.0))
    pre = jnp.zeros((PACK_ROWS * 128,), F32).at[0:3072].set(conv_slot.reshape(3072))
    conv_full = _allreduce_small(pre.reshape(PACK_ROWS, 128), "conv_w").reshape(-1)[0:3072].reshape(3, 1024)

    full = _gather_weights({n: w[n] for n in COL_SHARDED + ROW_SHARDED})
    wts = _to_local_layouts(full)
    wts["conv_w"] = conv_full
    gains = {n: w[n].reshape(1, -1) for n in SMALL if n != "conv_w"}

    loss_parts, grad_x, gw, gsmall = _local_step(x[0], positions[0], mem[0], loss_target[0], wts, gains)
    gw = _grads_to_reference_layout(gw)

    loss_local = 0.5 * jnp.sum(loss_parts) * (1.0 / D_MODEL)
    flat = jnp.concatenate([gsmall[n].reshape(-1) for n in SMALL] + [loss_local.reshape(1)])
    flat = jnp.pad(flat, (0, PACK_ROWS * 128 - flat.shape[0]))
    tot = _allreduce_small(flat.reshape(PACK_ROWS, 128), "grads").reshape(-1)
    grads = {}
    off = 0
    for n, sz in zip(SMALL, SMALL_SIZES):
        grads[n] = tot[off:off + sz]
        off += sz
    loss = tot[off]
    grads["conv_w"] = lax.dynamic_slice(grads["conv_w"].reshape(3, 1024), (0, chip * 256), (3, 256))
    for n in SMALL:
        grads[n] = grads[n].reshape(w[n].shape)

    deltas, new_m, new_v = {}, {}, {}
    for n in SMALL:
        shp = w[n].shape
        two_d = (lambda a: a.reshape(1, -1)) if len(shp) == 1 else (lambda a: a)
        d, mn, vn = _adamw(two_d(w[n]), two_d(grads[n]), two_d(m1[n]), two_d(v2[n]), "adamw_" + n)
        deltas[n], new_m[n], new_v[n] = d.reshape(shp), mn.reshape(shp), vn.reshape(shp)

    sel_c = jnp.reshape(ci, (1,)).astype(jnp.int32)
    for n in COL_SHARDED + ROW_SHARDED:
        g = gw[n]
        r, c = g.shape
        cut = g.reshape(r, 4, c // 4).transpose(1, 0, 2) if n in COL_SHARDED else g.reshape(4, r // 4, c)
        g_own, g_other = _reduce_scatter(cut, n)
        deltas[n], new_m[n], new_v[n], grads[n] = _adamw_halves(w[n], g_own, g_other, m1[n], v2[n], sel_c,
                                                                 "adamw_" + n)

    lead = lambda a: a[None]
    return (loss, grad_x[None], *[lead(grads[n]) for n in WEIGHTS], *[lead(deltas[n]) for n in WEIGHTS],
            *[lead(new_m[n]) for n in WEIGHTS], *[lead(new_v[n]) for n in WEIGHTS])
```

```python
import functools

import numpy as np
import jax
import jax.numpy as jnp
from jax import lax
from jax.experimental import pallas as pl
from jax.experimental.pallas import tpu as pltpu

F32 = jnp.float32
BF16 = jnp.bfloat16
MESH = pl.DeviceIdType.MESH

D_MODEL = 2048
EPS = 1e-6
CHUNK = 64
MLA_HEADS = 16
QK_NOPE = 128
QK_ROPE = 64
HEAD_PAD = 256
MEM_TOKENS = 256
MEM_HEADS = 4
MEM_HEAD_DIM = 256
ROPE_THETA = 10000.0
MLA_SCALE = (QK_NOPE + QK_ROPE) ** -0.5
MEM_SCALE = MEM_HEAD_DIM ** -0.5
LN2 = 0.6931471805599453
Q_SCALE = MLA_SCALE / LN2
NEG = -1e30

ADAM_LR = 0.001
ADAM_B1 = 0.9
ADAM_B2 = 0.999
ADAM_EPS = 1e-08
ADAM_WD = 0.01
ADAM_STEP = 10

NP = 16384
COL_G, COL_Z, COL_C, COL_M, COL_S = 0, 6144, 8192, 12288, 14336

TM_PROJ = 1024
TN_PROJ = 1024
TS_ROW = 512
TS_BWD = 256
TQ_ATT = 1024
TK_ATT = 1024
TR_ATT = 256
VMEM_LIMIT = 56 * 1024 * 1024

NT_DIMS = (((1,), (1,)), ((), ()))
TN_DIMS = (((0,), (0,)), ((), ()))


def _dot(a, b, mode="nn"):
    if mode == "nn":
        return jnp.dot(a, b, preferred_element_type=F32)
    return lax.dot_general(a, b, NT_DIMS if mode == "nt" else TN_DIMS, preferred_element_type=F32)


def _sigmoid(z):
    return 1.0 / (1.0 + jnp.exp(-z))


def _params(sem=None, vmem=VMEM_LIMIT):
    return pltpu.CompilerParams(dimension_semantics=sem, vmem_limit_bytes=vmem)


def _fold8(v):
    r, c = v.shape
    return v.reshape(r // 8, 8, c).sum(axis=0)


def _swap32(t):
    lane = lax.broadcasted_iota(jnp.int32, t.shape, 1)
    return jnp.where(lane < 32, pltpu.roll(t, 96, 1), pltpu.roll(t, 32, 1))


def _rope(t, cs, sg):
    return t * cs + _swap32(t) * sg


def _rope_bwd(d, cs, sg):
    return d * cs + _swap32(d * sg)


def _rms_fwd(xf, g, n):
    r = lax.rsqrt(jnp.sum(xf * xf, axis=-1, keepdims=True) * (1.0 / n) + EPS)
    return xf * r * g


def _rms_bwd(xf, g, dy, n):
    r = lax.rsqrt(jnp.sum(xf * xf, axis=-1, keepdims=True) * (1.0 / n) + EPS)
    xhat = xf * r
    dyg = dy * g
    dx = r * (dyg - xhat * (jnp.sum(dyg * xhat, axis=-1, keepdims=True) * (1.0 / n)))
    return dx, dy * xhat


ANY = pl.BlockSpec(memory_space=pl.ANY)


class _Side:
    def __init__(self, inputs, out_shapes, n_copies, build, aliases=()):
        self.inputs, self.out_shapes, self.n_copies = list(inputs), list(out_shapes), n_copies
        self.build, self.aliases = build, dict(aliases)


def _pcall(body, side, name, grid, in_specs, out_specs, out_shape, args, sem, scratch_shapes=(), prefetch=()):
    n_pre, n_in, n_out = len(prefetch), len(in_specs), len(out_specs)
    if side is None:
        outs = pl.pallas_call(
            body, name=name,
            grid_spec=pltpu.PrefetchScalarGridSpec(num_scalar_prefetch=n_pre, grid=grid, in_specs=list(in_specs),
                                                   out_specs=list(out_specs), scratch_shapes=list(scratch_shapes)),
            out_shape=list(out_shape), compiler_params=_params(sem),
        )(*prefetch, *args)
        return list(outs), []
    ns_in, ns_out = len(side.inputs), len(side.out_shapes)

    def wrapped(*refs):
        pre, r = refs[:n_pre], refs[n_pre:]
        ins, s_ins = r[:n_in], r[n_in:n_in + ns_in]
        outs = r[n_in + ns_in:n_in + ns_in + n_out]
        s_outs = r[n_in + ns_in + n_out:n_in + ns_in + n_out + ns_out]
        scr, send_sems, recv_sems = r[n_in + ns_in + n_out + ns_out:-2], r[-2], r[-1]
        pairs = side.build(s_ins, s_outs, send_sems, recv_sems)
        first = pl.program_id(0) == 0
        last = pl.program_id(0) == grid[0] - 1
        for d in range(1, len(grid)):
            first = first & (pl.program_id(d) == 0)
            last = last & (pl.program_id(d) == grid[d] - 1)

        @pl.when(first)
        def _():
            for send, _ in pairs:
                send.start()

        body(*pre, *ins, *outs, *scr)

        @pl.when(last)
        def _():
            for send, recv in pairs:
                send.wait_send()
                recv.wait_recv()

    outs = pl.pallas_call(
        wrapped, name=name,
        grid_spec=pltpu.PrefetchScalarGridSpec(
            num_scalar_prefetch=n_pre, grid=grid, in_specs=list(in_specs) + [ANY] * ns_in,
            out_specs=list(out_specs) + [ANY] * ns_out,
            scratch_shapes=list(scratch_shapes) + [pltpu.SemaphoreType.DMA((side.n_copies,)),
                                                   pltpu.SemaphoreType.DMA((side.n_copies,))]),
        out_shape=list(out_shape) + side.out_shapes,
        input_output_aliases={n_pre + n_in + i: n_out + o for i, o in side.aliases.items()},
        compiler_params=_params(("arbitrary",) * len(grid)),
    )(*prefetch, *args, *side.inputs)
    return list(outs[:n_out]), list(outs[n_out:])


def _matmul(a, b, mode, out_dtype, name, tm=1024, tn=1024, tk=1024, side=None):
    if mode == "nn":
        (m, k), (_, n) = a.shape, b.shape
    elif mode == "nt":
        (m, k), (n, _) = a.shape, b.shape
    else:
        (k, m), (_, n) = a.shape, b.shape
    tm, tn, tk = min(tm, m), min(tn, n), min(tk, k)
    nk = k // tk
    if mode == "tn":
        a_spec = pl.BlockSpec((tk, tm), lambda i, j, kk: (kk, i))
    else:
        a_spec = pl.BlockSpec((tm, tk), lambda i, j, kk: (i, kk))
    if mode == "nt":
        b_spec = pl.BlockSpec((tn, tk), lambda i, j, kk: (j, kk))
    else:
        b_spec = pl.BlockSpec((tk, tn), lambda i, j, kk: (kk, j))

    def body(a_ref, b_ref, o_ref, acc_ref):
        kk = pl.program_id(2)

        @pl.when(kk == 0)
        def _():
            acc_ref[...] = jnp.zeros_like(acc_ref)

        acc_ref[...] += _dot(a_ref[...].astype(BF16), b_ref[...].astype(BF16), mode)

        @pl.when(kk == nk - 1)
        def _():
            o_ref[...] = acc_ref[...].astype(out_dtype)

    outs, extra = _pcall(
        body, side, name, (m // tm, n // tn, nk), [a_spec, b_spec],
        [pl.BlockSpec((tm, tn), lambda i, j, kk: (i, j))], [jax.ShapeDtypeStruct((m, n), out_dtype)], (a, b),
        ("parallel", "parallel", "arbitrary"), scratch_shapes=[pltpu.VMEM((tm, tn), F32)])
    return outs[0] if side is None else (outs[0], extra)


def _proj_fwd(x, g, w_all, side=None):
    s = x.shape[0]
    tm, tn = min(TM_PROJ, s), TN_PROJ

    def body(x_ref, g_ref, w_ref, p_ref, h_ref):
        @pl.when(pl.program_id(1) == 0)
        def _():
            h_ref[...] = _rms_fwd(x_ref[...], g_ref[...], D_MODEL).astype(BF16)

        p_ref[...] = _dot(h_ref[...], w_ref[...]).astype(BF16)

    return _pcall(
        body, side, "proj_fwd", (s // tm, NP // tn),
        [pl.BlockSpec((tm, D_MODEL), lambda i, j: (i, 0)),
         pl.BlockSpec((1, D_MODEL), lambda i, j: (0, 0)),
         pl.BlockSpec((D_MODEL, tn), lambda i, j: (0, j))],
        [pl.BlockSpec((tm, tn), lambda i, j: (i, j)), pl.BlockSpec((tm, D_MODEL), lambda i, j: (i, 0))],
        [jax.ShapeDtypeStruct((s, NP), BF16), jax.ShapeDtypeStruct((s, D_MODEL), BF16)], (x, g, w_all),
        ("parallel", "arbitrary"))


def _shift_rows(v, carry, j, ts, back):
    if back:
        main = pltpu.roll(v, j, 0)
        edge = pltpu.roll(jnp.concatenate([carry, v[:8]], axis=0), j, 0)[8:]
        return jnp.concatenate([edge, main[8:]], axis=0)
    main = pltpu.roll(v, ts - j, 0)
    edge = pltpu.roll(jnp.concatenate([v[ts - 8:], carry], axis=0), 16 - j, 0)[:8]
    return jnp.concatenate([main[:ts - 8], edge], axis=0)


def _conv_fwd(p_all, conv_w):
    s = p_all.shape[0]
    ts = min(TS_ROW, s)
    c0 = COL_C // 1024

    def body(cg_ref, bg_ref, u_ref, z_ref, w_ref, a_ref, co_ref, carry_ref):
        @pl.when(pl.program_id(0) == 0)
        def _():
            carry_ref[...] = jnp.zeros_like(carry_ref)

        p = cg_ref[...].astype(F32) * u_ref[...].astype(F32)
        carry = carry_ref[...]
        co = (w_ref[2:3, :] * p + w_ref[1:2, :] * _shift_rows(p, carry, 1, ts, True)
              + w_ref[0:1, :] * _shift_rows(p, carry, 2, ts, True))
        carry_ref[...] = p[ts - 8:]
        z = z_ref[...].astype(F32)
        a_ref[...] = (bg_ref[...].astype(F32) * co * (z * _sigmoid(z))).astype(BF16)
        co_ref[...] = co.astype(BF16)

    seg = lambda c: pl.BlockSpec((ts, 1024), lambda i: (i, c0 + c))
    return pl.pallas_call(
        body, name="conv_fwd", grid=(s // ts,),
        in_specs=[seg(0), seg(1), seg(2), seg(3), pl.BlockSpec((3, 1024), lambda i: (0, 0))],
        out_specs=[pl.BlockSpec((ts, 1024), lambda i: (i, 0))] * 2,
        out_shape=[jax.ShapeDtypeStruct((s, 1024), BF16)] * 2,
        scratch_shapes=[pltpu.VMEM((8, 1024), F32)],
        compiler_params=_params(("arbitrary",)),
    )(p_all, p_all, p_all, p_all, conv_w)


def _lora_fwd(p_all, gq, gkv, gkr, cs, sg, side=None):
    s = p_all.shape[0]
    ts = min(TS_ROW, s)

    def body(cq_ref, ckv_ref, kr_ref, gq_ref, gkv_ref, gkr_ref, cs_ref, sg_ref, cqn_ref, ckvn_ref, krope_ref):
        cqn_ref[...] = _rms_fwd(cq_ref[...].astype(F32), gq_ref[...], 512).astype(BF16)
        ckvn_ref[...] = _rms_fwd(ckv_ref[...].astype(F32), gkv_ref[...], 512).astype(BF16)
        kn = _rms_fwd(kr_ref[...].astype(F32), gkr_ref[...], QK_ROPE)
        krope_ref[...] = _rope(kn, cs_ref[...], sg_ref[...]).astype(BF16)

    row = lambda w: pl.BlockSpec((ts, w), lambda i: (i, 0))
    vec = lambda w: pl.BlockSpec((1, w), lambda i: (0, 0))
    return _pcall(
        body, side, "lora_fwd", (s // ts,),
        [pl.BlockSpec((ts, 512), lambda i: (i, COL_S // 512)),
         pl.BlockSpec((ts, 512), lambda i: (i, COL_S // 512 + 1)),
         pl.BlockSpec((ts, 128), lambda i: (i, (COL_S + 1024) // 128)),
         vec(512), vec(512), vec(128), row(128), row(128)],
        [row(512), row(512), row(128)],
        [jax.ShapeDtypeStruct((s, 512), BF16), jax.ShapeDtypeStruct((s, 512), BF16),
         jax.ShapeDtypeStruct((s, 128), BF16)],
        (p_all, p_all, p_all, gq, gkv, gkr, cs, sg), ("parallel",))


def _q_prep(q_raw, gn, gr, cs, sg):
    s = q_raw.shape[0]
    ts = min(TS_ROW, s)

    def body(q_ref, gn_ref, gr_ref, cs_ref, sg_ref, o_ref):
        for h in range(MLA_HEADS):
            q = q_ref[:, h * HEAD_PAD:(h + 1) * HEAD_PAD].astype(F32)
            a = _rms_fwd(q[:, :128], gn_ref[...], QK_NOPE)
            b = _rope(_rms_fwd(q[:, 128:], gr_ref[...], QK_ROPE), cs_ref[...], sg_ref[...])
            o_ref[h] = (jnp.concatenate([a, b], axis=1) * Q_SCALE).astype(BF16)

    return pl.pallas_call(
        body, name="q_prep", grid=(s // ts,),
        in_specs=[pl.BlockSpec((ts, MLA_HEADS * HEAD_PAD), lambda i: (i, 0)),
                  pl.BlockSpec((1, 128), lambda i: (0, 0)), pl.BlockSpec((1, 128), lambda i: (0, 0)),
                  pl.BlockSpec((ts, 128), lambda i: (i, 0)), pl.BlockSpec((ts, 128), lambda i: (i, 0))],
        out_specs=pl.BlockSpec((MLA_HEADS, ts, HEAD_PAD), lambda i: (0, i, 0)),
        out_shape=jax.ShapeDtypeStruct((MLA_HEADS, s, HEAD_PAD), BF16),
        compiler_params=_params(("parallel",)),
    )(q_raw, gn, gr, cs, sg)


def _k_prep(kv, krope, gk):
    s = kv.shape[0]
    ts = min(TS_ROW, s)

    def body(k_ref, kr_ref, gk_ref, o_ref, vt_ref):
        for h in range(MLA_HEADS):
            a = _rms_fwd(k_ref[:, h * HEAD_PAD:h * HEAD_PAD + 128].astype(F32), gk_ref[...], QK_NOPE)
            o_ref[h] = jnp.concatenate([a.astype(BF16), kr_ref[...]], axis=1)
            vt_ref[h] = k_ref[:, h * HEAD_PAD + 128:(h + 1) * HEAD_PAD].astype(F32).T.astype(BF16)

    return pl.pallas_call(
        body, name="k_prep", grid=(s // ts,),
        in_specs=[pl.BlockSpec((ts, MLA_HEADS * HEAD_PAD), lambda i: (i, 0)),
                  pl.BlockSpec((ts, 128), lambda i: (i, 0)),
                  pl.BlockSpec((1, 128), lambda i: (0, 0))],
        out_specs=[pl.BlockSpec((MLA_HEADS, ts, HEAD_PAD), lambda i: (0, i, 0)),
                   pl.BlockSpec((MLA_HEADS, 128, ts), lambda i: (0, 0, i))],
        out_shape=[jax.ShapeDtypeStruct((MLA_HEADS, s, HEAD_PAD), BF16),
                   jax.ShapeDtypeStruct((MLA_HEADS, 128, s), BF16)],
        compiler_params=_params(("parallel",)),
    )(kv, krope, gk)


def _chunk_mask(tq, tk, row0=0):
    r = (lax.broadcasted_iota(jnp.int32, (tq, tk), 0) + row0) // CHUNK
    c = lax.broadcasted_iota(jnp.int32, (tq, tk), 1) // CHUNK
    return c <= r


def _causal_pairs(n, by_key):
    if by_key:
        pairs = [(i, j) for j in range(n) for i in range(j, n)]
    else:
        pairs = [(i, j) for i in range(n) for j in range(i + 1)]
    return (jnp.asarray([p[0] for p in pairs], jnp.int32), jnp.asarray([p[1] for p in pairs], jnp.int32))


def _flash_fwd(qcat, kcat, vt, p_all):
    s = qcat.shape[1]
    tq = tk = min(TQ_ATT, s)
    tc = min(TR_ATT, tq)
    nq = s // tq
    zc = COL_Z // 128
    qi, kj = _causal_pairs(nq, False)

    def body(qi_ref, kj_ref, q_ref, k_ref, vt_ref, z_ref, y_ref, a_ref, lse_ref, m_sc, l_sc, acc_sc):
        t = pl.program_id(1)
        i, j = qi_ref[t], kj_ref[t]

        @pl.when(j == 0)
        def _():
            m_sc[...] = jnp.full_like(m_sc, NEG)
            l_sc[...] = jnp.zeros_like(l_sc)
            acc_sc[...] = jnp.zeros_like(acc_sc)

        def step(masked):
            nc = tq // tc
            state = [[m_sc[:, c * tc:(c + 1) * tc], l_sc[:, c * tc:(c + 1) * tc], acc_sc[:, c * tc:(c + 1) * tc]]
                     for c in range(nc)]
            units = [(u, c) for u in range(tk // tc) for c in range(nc) if (u <= c or not masked)]

            def scores(u, c):
                return _dot(k_ref[0, u * tc:(u + 1) * tc, :], q_ref[0, c * tc:(c + 1) * tc, :], "nt")

            def weighted_values(u, c, alpha, pb):
                state[c][2] = alpha * state[c][2] + _dot(vt_ref[0, :, u * tc:(u + 1) * tc], pb)

            ahead = 3
            pending = [scores(*un) for un in units[:ahead]]
            late = None
            for n, (u, c) in enumerate(units):
                st = pending.pop(0)
                if n + ahead < len(units):
                    pending.append(scores(*units[n + ahead]))
                if masked and u == c:
                    kc = lax.broadcasted_iota(jnp.int32, (tc, tc), 0) // CHUNK
                    qc = lax.broadcasted_iota(jnp.int32, (tc, tc), 1) // CHUNK
                    st = jnp.where(kc <= qc, st, NEG)
                m_run, l_run, _ = state[c]
                m_new = jnp.maximum(m_run, jnp.max(st, axis=0, keepdims=True))
                alpha = jnp.exp2(m_run - m_new)
                p = jnp.exp2(st - m_new)
                state[c][0] = m_new
                state[c][1] = alpha * l_run + jnp.sum(p, axis=0, keepdims=True)
                if late is not None:
                    weighted_values(*late)
                late = (u, c, alpha, p.astype(BF16))
            weighted_values(*late)
            m_sc[...] = jnp.concatenate([s_[0] for s_ in state], axis=1)
            l_sc[...] = jnp.concatenate([s_[1] for s_ in state], axis=1)
            acc_sc[...] = jnp.concatenate([s_[2] for s_ in state], axis=1)

        @pl.when(j < i)
        def _():
            step(False)

        @pl.when(j == i)
        def _():
            step(True)
            y = (acc_sc[...] * (1.0 / l_sc[...])).T
            z = z_ref[...].astype(F32)
            y_ref[...] = y.astype(BF16)
            a_ref[...] = (y * (z * _sigmoid(z))).astype(BF16)
            lse2 = m_sc[...] + jnp.log(l_sc[...]) * (1.0 / LN2)
            lse_ref[0] = jnp.broadcast_to(lse2, (128, tq)).T

    return pl.pallas_call(
        body, name="flash_fwd",
        grid_spec=pltpu.PrefetchScalarGridSpec(
            num_scalar_prefetch=2, grid=(MLA_HEADS, qi.shape[0]),
            in_specs=[pl.BlockSpec((1, tq, HEAD_PAD), lambda h, t, qi_ref, kj_ref: (h, qi_ref[t], 0)),
                      pl.BlockSpec((1, tk, HEAD_PAD), lambda h, t, qi_ref, kj_ref: (h, kj_ref[t], 0)),
                      pl.BlockSpec((1, 128, tk), lambda h, t, qi_ref, kj_ref: (h, 0, kj_ref[t])),
                      pl.BlockSpec((tq, 128), lambda h, t, qi_ref, kj_ref: (qi_ref[t], zc + h))],
            out_specs=[pl.BlockSpec((tq, 128), lambda h, t, qi_ref, kj_ref: (qi_ref[t], h)),
                       pl.BlockSpec((tq, 128), lambda h, t, qi_ref, kj_ref: (qi_ref[t], h)),
                       pl.BlockSpec((1, tq, 128), lambda h, t, qi_ref, kj_ref: (h, qi_ref[t], 0))],
            scratch_shapes=[pltpu.VMEM((1, tq), F32), pltpu.VMEM((1, tq), F32), pltpu.VMEM((128, tq), F32)]),
        out_shape=[jax.ShapeDtypeStruct((s, MLA_HEADS * 128), BF16),
                   jax.ShapeDtypeStruct((s, MLA_HEADS * 128), BF16),
                   jax.ShapeDtypeStruct((MLA_HEADS, s, 128), F32)],
        compiler_params=_params(("parallel", "arbitrary")),
    )(qi, kj, qcat, kcat, vt, p_all)


def _mem_kv_fwd(mem, g_norm, w_kv, g_kn):
    m = mem.shape[0]

    def body(mem_ref, g_ref, w_ref, gk_ref, memn_ref, kraw_ref, kn_ref, v_ref):
        memn = _rms_fwd(mem_ref[...], g_ref[...], D_MODEL).astype(BF16)
        memn_ref[...] = memn
        kvm = _dot(memn, w_ref[...])
        kraw_ref[...] = kvm[:, :1024]
        v_ref[...] = kvm[:, 1024:].astype(BF16)
        for hh in range(MEM_HEADS):
            sl = slice(hh * MEM_HEAD_DIM, (hh + 1) * MEM_HEAD_DIM)
            kn_ref[:, sl] = _rms_fwd(kvm[:, sl], gk_ref[...], MEM_HEAD_DIM).astype(BF16)

    return pl.pallas_call(
        body, name="mem_kv_fwd",
        out_shape=[jax.ShapeDtypeStruct((m, D_MODEL), BF16), jax.ShapeDtypeStruct((m, 1024), F32),
                   jax.ShapeDtypeStruct((m, 1024), BF16), jax.ShapeDtypeStruct((m, 1024), BF16)],
        compiler_params=_params(),
    )(mem, g_norm, w_kv, g_kn)


def _mem_attn_fwd(p_all, kn, v, gq):
    s = p_all.shape[0]
    tq = min(TS_ROW, s)

    def body(q_ref, z_ref, kn_ref, v_ref, gq_ref, a_ref):
        z = z_ref[...].astype(F32)
        gate = z * _sigmoid(z)
        for hh in range(MEM_HEADS):
            sl = slice(hh * MEM_HEAD_DIM, (hh + 1) * MEM_HEAD_DIM)
            qn = _rms_fwd(q_ref[:, sl].astype(F32), gq_ref[...], MEM_HEAD_DIM).astype(BF16)
            sc = _dot(qn, kn_ref[:, sl], "nt") * MEM_SCALE
            e = jnp.exp(sc - jnp.max(sc, axis=-1, keepdims=True))
            p = e * (1.0 / jnp.sum(e, axis=-1, keepdims=True))
            y = _dot(p.astype(BF16), v_ref[:, sl])
            a_ref[:, sl] = (y * gate[:, sl]).astype(BF16)

    full = lambda a: pl.BlockSpec(a.shape, lambda i: (0, 0))
    return pl.pallas_call(
        body, name="mem_attn_fwd", grid=(s // tq,),
        in_specs=[pl.BlockSpec((tq, 1024), lambda i: (i, COL_M // 1024)),
                  pl.BlockSpec((tq, 1024), lambda i: (i, COL_M // 1024 + 1)),
                  full(kn), full(v), full(gq)],
        out_specs=pl.BlockSpec((tq, 1024), lambda i: (i, 0)),
        out_shape=jax.ShapeDtypeStruct((s, 1024), BF16),
        compiler_params=_params(("parallel",)),
    )(p_all, p_all, kn, v, gq)


def _merge_fwd(a_conv, w_c, a_mla, w_m, a_mem, w_e, p_all):
    s = a_conv.shape[0]
    tm, tn = min(TS_ROW, s), 1024
    nj = D_MODEL // tn

    def body(ac_ref, wc_ref, am_ref, wm_ref, ae_ref, we_ref, gc_ref, gm_ref, ge_ref,
             oc_ref, om_ref, oe_ref, mg_ref):
        oc = _dot(ac_ref[...], wc_ref[...])
        om = _dot(am_ref[...], wm_ref[...])
        oe = _dot(ae_ref[...], we_ref[...])
        oc_ref[...] = oc.astype(BF16)
        om_ref[...] = om.astype(BF16)
        oe_ref[...] = oe.astype(BF16)
        mg_ref[...] = (_sigmoid(gc_ref[...].astype(F32)) * oc + _sigmoid(gm_ref[...].astype(F32)) * om
                       + _sigmoid(ge_ref[...].astype(F32)) * oe).astype(BF16)

    act = lambda k: pl.BlockSpec((tm, k), lambda i, j: (i, 0))
    wgt = lambda k: pl.BlockSpec((k, tn), lambda i, j: (0, j))
    gate = lambda b: pl.BlockSpec((tm, tn), lambda i, j: (i, b * nj + j))
    out = pl.BlockSpec((tm, tn), lambda i, j: (i, j))
    return pl.pallas_call(
        body, name="merge_fwd", grid=(s // tm, nj),
        in_specs=[act(1024), wgt(1024), act(2048), wgt(2048), act(1024), wgt(1024), gate(0), gate(1), gate(2)],
        out_specs=[out] * 4,
        out_shape=[jax.ShapeDtypeStruct((s, D_MODEL), BF16)] * 4,
        compiler_params=_params(("parallel", "parallel")),
    )(a_conv, w_c, a_mla, w_m, a_mem, w_e, p_all, p_all, p_all)


def _out_fwd(merged, w_o, x, target):
    s = merged.shape[0]
    tm, tn = min(TS_ROW, s), 1024
    nj = D_MODEL // tn

    def body(mg_ref, w_ref, x_ref, t_ref, dy_ref, dyb_ref, ls_ref):
        e = x_ref[...] + _dot(mg_ref[...], w_ref[...]) - t_ref[...]
        dy = e * (1.0 / D_MODEL)
        dy_ref[...] = dy
        dyb_ref[...] = dy.astype(BF16)
        r = _fold8(e * e)
        acc = r[:, 0:128]
        for cc in range(1, tn // 128):
            acc = acc + r[:, cc * 128:(cc + 1) * 128]
        ls_ref[...] = acc

    tile = pl.BlockSpec((tm, tn), lambda i, j: (i, j))
    return pl.pallas_call(
        body, name="out_fwd", grid=(s // tm, nj),
        in_specs=[pl.BlockSpec((tm, D_MODEL), lambda i, j: (i, 0)),
                  pl.BlockSpec((D_MODEL, tn), lambda i, j: (0, j)), tile, tile],
        out_specs=[tile, tile, pl.BlockSpec((8, 128), lambda i, j: (i, j))],
        out_shape=[jax.ShapeDtypeStruct((s, D_MODEL), F32), jax.ShapeDtypeStruct((s, D_MODEL), BF16),
                   jax.ShapeDtypeStruct((s // tm * 8, nj * 128), F32)],
        compiler_params=_params(("parallel", "parallel")),
    )(merged, w_o, x, target)


def _merge_bwd(dyb, w_o, p_all, o_c, o_m, o_e):
    s = dyb.shape[0]
    tm = min(256, s)

    def body(dy_ref, w_ref, g_ref, oc_ref, om_ref, oe_ref, dc_ref, dm_ref, de_ref, dg_ref):
        dmg = _dot(dy_ref[...], w_ref[...], "nt")
        for b, (o_ref, d_ref) in enumerate(((oc_ref, dc_ref), (om_ref, dm_ref), (oe_ref, de_ref))):
            sl = slice(b * D_MODEL, (b + 1) * D_MODEL)
            sg = _sigmoid(g_ref[:, sl].astype(F32))
            d_ref[...] = (dmg * sg).astype(BF16)
            dg_ref[:, sl] = (dmg * o_ref[...].astype(F32) * sg * (1.0 - sg)).astype(BF16)

    row = pl.BlockSpec((tm, D_MODEL), lambda i: (i, 0))
    wide = pl.BlockSpec((tm, 3 * D_MODEL), lambda i: (i, 0))
    return pl.pallas_call(
        body, name="merge_bwd", grid=(s // tm,),
        in_specs=[row, pl.BlockSpec((D_MODEL, D_MODEL), lambda i: (0, 0)), wide, row, row, row],
        out_specs=[row, row, row, wide],
        out_shape=[jax.ShapeDtypeStruct((s, D_MODEL), BF16)] * 3 + [jax.ShapeDtypeStruct((s, 3 * D_MODEL), BF16)],
        compiler_params=_params(("parallel",)),
    )(dyb, w_o, p_all, o_c, o_m, o_e)


def _conv_bwd(da, p_all, co, conv_w):
    s = da.shape[0]
    ts = min(TS_ROW, s)
    n = s // ts
    c0 = COL_C // 1024

    def body(da_ref, cg_ref, bg_ref, u_ref, z_ref, co_ref, w_ref, dp_ref, dw_ref, carry_ref):
        @pl.when(pl.program_id(0) == 0)
        def _():
            carry_ref[...] = jnp.zeros_like(carry_ref)
            dw_ref[...] = jnp.zeros_like(dw_ref)

        da_ = da_ref[...].astype(F32)
        cg, bg = cg_ref[...].astype(F32), bg_ref[...].astype(F32)
        u, z, cov = u_ref[...].astype(F32), z_ref[...].astype(F32), co_ref[...].astype(F32)
        sz = _sigmoid(z)
        dyc = da_ * (z * sz)
        dz = da_ * (bg * cov) * (sz * (1.0 + z * (1.0 - sz)))
        db = dyc * cov
        dco = dyc * bg
        carry = carry_ref[...]
        d1 = _shift_rows(dco, carry, 1, ts, False)
        d2 = _shift_rows(dco, carry, 2, ts, False)
        carry_ref[...] = dco[:8]
        dpp = w_ref[2:3, :] * dco + w_ref[1:2, :] * d1 + w_ref[0:1, :] * d2
        p = cg * u
        dw_ref[0] += _fold8(p * d2)
        dw_ref[1] += _fold8(p * d1)
        dw_ref[2] += _fold8(p * dco)
        dp_ref[...] = jnp.concatenate([dpp * u, db, dpp * cg, dz], axis=1).astype(BF16)

    rev = lambda c: pl.BlockSpec((ts, 1024), lambda i: (n - 1 - i, c))
    return pl.pallas_call(
        body, name="conv_bwd", grid=(n,),
        in_specs=[rev(0), rev(c0), rev(c0 + 1), rev(c0 + 2), rev(c0 + 3), rev(0),
                  pl.BlockSpec((3, 1024), lambda i: (0, 0))],
        out_specs=[pl.BlockSpec((ts, 4096), lambda i: (n - 1 - i, 0)),
                   pl.BlockSpec((3, 8, 1024), lambda i: (0, 0, 0))],
        out_shape=[jax.ShapeDtypeStruct((s, 4096), BF16), jax.ShapeDtypeStruct((3, 8, 1024), F32)],
        scratch_shapes=[pltpu.VMEM((8, 1024), F32)],
        compiler_params=_params(("arbitrary",)),
    )(da, p_all, p_all, p_all, p_all, co, conv_w)


def _mla_gate_bwd(da, y, p_all):
    s = da.shape[0]
    ts = min(TS_ROW, s)

    def body(da_ref, y_ref, z_ref, dy_ref, dz_ref, dl_ref):
        da_, yv, z = da_ref[...].astype(F32), y_ref[...].astype(F32), z_ref[...].astype(F32)
        sz = _sigmoid(z)
        dyv = da_ * (z * sz)
        dy_ref[...] = dyv.astype(BF16)
        dz_ref[...] = (da_ * yv * (sz * (1.0 + z * (1.0 - sz)))).astype(BF16)
        pr = dyv * yv
        for h in range(MLA_HEADS):
            dl_ref[h] = jnp.broadcast_to(jnp.sum(pr[:, h * 128:(h + 1) * 128], axis=-1, keepdims=True), (ts, 128))

    row = pl.BlockSpec((ts, D_MODEL), lambda i: (i, 0))
    return pl.pallas_call(
        body, name="mla_gate_bwd", grid=(s // ts,),
        in_specs=[row, row, pl.BlockSpec((ts, D_MODEL), lambda i: (i, COL_Z // D_MODEL))],
        out_specs=[row, row, pl.BlockSpec((MLA_HEADS, ts, 128), lambda i: (0, i, 0))],
        out_shape=[jax.ShapeDtypeStruct((s, D_MODEL), BF16)] * 2 + [jax.ShapeDtypeStruct((MLA_HEADS, s, 128), F32)],
        compiler_params=_params(("parallel",)),
    )(da, y, p_all)


def _flash_bwd(qcat, kcat, kv, dy, lse, delta):
    s = qcat.shape[1]
    tq = tk = min(TQ_ATT, s)
    nq = s // tq
    qi, kj = _causal_pairs(nq, True)

    def body(qi_ref, kj_ref, q_ref, k_ref, v_ref, do_ref, lse_ref, dl_ref, dq_ref, dk_ref, dv_ref):
        t = pl.program_id(1)
        i, j = qi_ref[t], kj_ref[t]

        def step(masked):
            q, k = q_ref[0], k_ref[0]
            sc = _dot(q, k, "nt")
            if masked:
                sc = jnp.where(_chunk_mask(tq, tk), sc, NEG)
            p = jnp.exp2(sc - lse_ref[0][:, 0:1])
            do = do_ref[...]
            dpv = _dot(do, v_ref[...], "nt")
            ds = (p * (dpv - dl_ref[0][:, 0:1])).astype(BF16)
            pb = p.astype(BF16)
            dv_new = _dot(pb, do, "tn")
            dk_new = _dot(ds, q, "tn")
            dq_new = _dot(ds, k)
            rows = pl.ds(pl.multiple_of(i * tq, tq), tq)
            if masked:
                dv_ref[0] = dv_new
                dk_ref[0] = dk_new
            else:
                dv_ref[0] += dv_new
                dk_ref[0] += dk_new

            @pl.when(j == 0)
            def _():
                dq_ref[0, rows, :] = dq_new

            @pl.when(j > 0)
            def _():
                dq_ref[0, rows, :] += dq_new

        @pl.when(i == j)
        def _():
            step(True)

        @pl.when(i > j)
        def _():
            step(False)

    qrow = lambda w: pl.BlockSpec((1, tq, w), lambda h, t, qi_ref, kj_ref: (h, qi_ref[t], 0))
    krow = lambda w: pl.BlockSpec((1, tk, w), lambda h, t, qi_ref, kj_ref: (h, kj_ref[t], 0))
    return pl.pallas_call(
        body, name="flash_bwd",
        grid_spec=pltpu.PrefetchScalarGridSpec(
            num_scalar_prefetch=2, grid=(MLA_HEADS, qi.shape[0]),
            in_specs=[qrow(HEAD_PAD), krow(HEAD_PAD),
                      pl.BlockSpec((tk, 128), lambda h, t, qi_ref, kj_ref: (kj_ref[t], 2 * h + 1)),
                      pl.BlockSpec((tq, 128), lambda h, t, qi_ref, kj_ref: (qi_ref[t], h)),
                      qrow(128), qrow(128)],
            out_specs=[pl.BlockSpec((1, s, HEAD_PAD), lambda h, t, qi_ref, kj_ref: (h, 0, 0)),
                       krow(HEAD_PAD), krow(128)]),
        out_shape=[jax.ShapeDtypeStruct((MLA_HEADS, s, HEAD_PAD), F32),
                   jax.ShapeDtypeStruct((MLA_HEADS, s, HEAD_PAD), F32),
                   jax.ShapeDtypeStruct((MLA_HEADS, s, 128), F32)],
        compiler_params=_params(("parallel", "arbitrary")),
    )(qi, kj, qcat, kcat, kv, dy, lse, delta)


def _q_prep_bwd(dqcat, q_raw, gn, gr, cs, sg, side=None):
    s = q_raw.shape[0]
    ts = min(TS_BWD, s)

    def body(dq_ref, q_ref, gn_ref, gr_ref, cs_ref, sg_ref, o_ref, dg_ref):
        @pl.when(pl.program_id(0) == 0)
        def _():
            dg_ref[...] = jnp.zeros_like(dg_ref)

        ga_acc = jnp.zeros((8, 128), F32)
        gb_acc = jnp.zeros((8, 128), F32)
        for h in range(MLA_HEADS):
            cols = slice(h * HEAD_PAD, (h + 1) * HEAD_PAD)
            q = q_ref[:, cols].astype(F32)
            d = dq_ref[h] * MLA_SCALE
            da, ga = _rms_bwd(q[:, :128], gn_ref[...], d[:, :128], QK_NOPE)
            db, gb = _rms_bwd(q[:, 128:], gr_ref[...], _rope_bwd(d[:, 128:], cs_ref[...], sg_ref[...]), QK_ROPE)
            o_ref[:, cols] = jnp.concatenate([da, db], axis=1).astype(BF16)
            ga_acc = ga_acc + _fold8(ga)
            gb_acc = gb_acc + _fold8(gb)
        dg_ref[0] += ga_acc
        dg_ref[1] += gb_acc

    wide = pl.BlockSpec((ts, MLA_HEADS * HEAD_PAD), lambda i: (i, 0))
    return _pcall(
        body, side, "q_prep_bwd", (s // ts,),
        [pl.BlockSpec((MLA_HEADS, ts, HEAD_PAD), lambda i: (0, i, 0)), wide,
         pl.BlockSpec((1, 128), lambda i: (0, 0)), pl.BlockSpec((1, 128), lambda i: (0, 0)),
         pl.BlockSpec((ts, 128), lambda i: (i, 0)), pl.BlockSpec((ts, 128), lambda i: (i, 0))],
        [wide, pl.BlockSpec((2, 8, 128), lambda i: (0, 0, 0))],
        [jax.ShapeDtypeStruct((s, MLA_HEADS * HEAD_PAD), BF16), jax.ShapeDtypeStruct((2, 8, 128), F32)],
        (dqcat, q_raw, gn, gr, cs, sg), ("arbitrary",))


def _k_prep_bwd(dkcat, dv, kv, gk, side=None):
    s = kv.shape[0]
    ts = min(TS_BWD, s)

    def body(dk_ref, dv_ref, k_ref, gk_ref, o_ref, dkr_ref, dg_ref):
        @pl.when(pl.program_id(0) == 0)
        def _():
            dg_ref[...] = jnp.zeros_like(dg_ref)

        g_acc = jnp.zeros((8, 128), F32)
        dkr = jnp.zeros((ts, 128), F32)
        for h in range(MLA_HEADS):
            d = dk_ref[h] * LN2
            k = k_ref[:, h * HEAD_PAD:h * HEAD_PAD + 128].astype(F32)
            dk_raw, gg = _rms_bwd(k, gk_ref[...], d[:, :128], QK_NOPE)
            o_ref[:, h * HEAD_PAD:(h + 1) * HEAD_PAD] = jnp.concatenate([dk_raw, dv_ref[h]], axis=1).astype(BF16)
            g_acc = g_acc + _fold8(gg)
            dkr = dkr + d[:, 128:]
        dkr_ref[...] = dkr
        dg_ref[...] += g_acc

    wide = pl.BlockSpec((ts, MLA_HEADS * HEAD_PAD), lambda i: (i, 0))
    return _pcall(
        body, side, "k_prep_bwd", (s // ts,),
        [pl.BlockSpec((MLA_HEADS, ts, HEAD_PAD), lambda i: (0, i, 0)),
         pl.BlockSpec((MLA_HEADS, ts, 128), lambda i: (0, i, 0)), wide,
         pl.BlockSpec((1, 128), lambda i: (0, 0))],
        [wide, pl.BlockSpec((ts, 128), lambda i: (i, 0)), pl.BlockSpec((8, 128), lambda i: (0, 0))],
        [jax.ShapeDtypeStruct((s, MLA_HEADS * HEAD_PAD), BF16), jax.ShapeDtypeStruct((s, 128), F32),
         jax.ShapeDtypeStruct((8, 128), F32)],
        (dkcat, dv, kv, gk), ("arbitrary",))


def _lora_bwd(dcqn, dckvn, dkr, p_all, gq, gkv, gkr, cs, sg):
    s = p_all.shape[0]
    ts = min(TS_ROW, s)

    def body(dq_ref, dkv_ref, dkr_ref, cq_ref, ckv_ref, kr_ref, gq_ref, gkv_ref, gkr_ref, cs_ref, sg_ref,
             o_ref, dgq_ref, dgkv_ref, dgkr_ref):
        @pl.when(pl.program_id(0) == 0)
        def _():
            dgq_ref[...] = jnp.zeros_like(dgq_ref)
            dgkv_ref[...] = jnp.zeros_like(dgkv_ref)
            dgkr_ref[...] = jnp.zeros_like(dgkr_ref)

        dq, g1 = _rms_bwd(cq_ref[...].astype(F32), gq_ref[...], dq_ref[...], 512)
        dkv, g2 = _rms_bwd(ckv_ref[...].astype(F32), gkv_ref[...], dkv_ref[...], 512)
        dkn = _rope_bwd(dkr_ref[...], cs_ref[...], sg_ref[...])
        dk, g3 = _rms_bwd(kr_ref[...].astype(F32), gkr_ref[...], dkn, QK_ROPE)
        o_ref[...] = jnp.concatenate([dq, dkv, dk, jnp.zeros((ts, 2048 - 1152), F32)], axis=1).astype(BF16)
        dgq_ref[...] += _fold8(g1)
        dgkv_ref[...] += _fold8(g2)
        dgkr_ref[...] += _fold8(g3)

    row = lambda w: pl.BlockSpec((ts, w), lambda i: (i, 0))
    vec = lambda w: pl.BlockSpec((1, w), lambda i: (0, 0))
    acc = lambda w: pl.BlockSpec((8, w), lambda i: (0, 0))
    return pl.pallas_call(
        body, name="lora_bwd", grid=(s // ts,),
        in_specs=[row(512), row(512), row(128),
                  pl.BlockSpec((ts, 512), lambda i: (i, COL_S // 512)),
                  pl.BlockSpec((ts, 512), lambda i: (i, COL_S // 512 + 1)),
                  pl.BlockSpec((ts, 128), lambda i: (i, (COL_S + 1024) // 128)),
                  vec(512), vec(512), vec(128), row(128), row(128)],
        out_specs=[row(2048), acc(512), acc(512), acc(128)],
        out_shape=[jax.ShapeDtypeStruct((s, 2048), BF16), jax.ShapeDtypeStruct((8, 512), F32),
                   jax.ShapeDtypeStruct((8, 512), F32), jax.ShapeDtypeStruct((8, 128), F32)],
        compiler_params=_params(("arbitrary",)),
    )(dcqn, dckvn, dkr, p_all, p_all, p_all, gq, gkv, gkr, cs, sg)


def _mem_attn_bwd(da, p_all, kn, v, gq):
    s = p_all.shape[0]
    tq = min(TS_ROW, s)

    def body(da_ref, q_ref, z_ref, kn_ref, v_ref, gq_ref, o_ref, dkn_ref, dv_ref, dg_ref):
        @pl.when(pl.program_id(0) == 0)
        def _():
            dkn_ref[...] = jnp.zeros_like(dkn_ref)
            dv_ref[...] = jnp.zeros_like(dv_ref)
            dg_ref[...] = jnp.zeros_like(dg_ref)

        z = z_ref[...].astype(F32)
        da_ = da_ref[...].astype(F32)
        sz = _sigmoid(z)
        gate = z * sz
        dgate = sz * (1.0 + z * (1.0 - sz))
        for hh in range(MEM_HEADS):
            sl = slice(hh * MEM_HEAD_DIM, (hh + 1) * MEM_HEAD_DIM)
            qf = q_ref[:, sl].astype(F32)
            qn = _rms_fwd(qf, gq_ref[...], MEM_HEAD_DIM).astype(BF16)
            knh, vh = kn_ref[:, sl], v_ref[:, sl]
            sc = _dot(qn, knh, "nt") * MEM_SCALE
            e = jnp.exp(sc - jnp.max(sc, axis=-1, keepdims=True))
            p = e * (1.0 / jnp.sum(e, axis=-1, keepdims=True))
            pb = p.astype(BF16)
            y = _dot(pb, vh)
            dyh = da_[:, sl] * gate[:, sl]
            o_ref[:, 1024 + hh * MEM_HEAD_DIM:1024 + (hh + 1) * MEM_HEAD_DIM] = (
                da_[:, sl] * y * dgate[:, sl]).astype(BF16)
            dyb = dyh.astype(BF16)
            dpm = _dot(dyb, vh, "nt")
            ds = (p * (dpm - jnp.sum(dpm * p, axis=-1, keepdims=True)) * MEM_SCALE).astype(BF16)
            dqn = _dot(ds, knh)
            dkn_ref[:, sl] += _dot(ds, qn, "tn")
            dv_ref[:, sl] += _dot(pb, dyb, "tn")
            dq, gg = _rms_bwd(qf, gq_ref[...], dqn, MEM_HEAD_DIM)
            o_ref[:, sl] = dq.astype(BF16)
            dg_ref[...] += _fold8(gg)

    full = lambda a: pl.BlockSpec(a.shape, lambda i: (0, 0))
    return pl.pallas_call(
        body, name="mem_attn_bwd", grid=(s // tq,),
        in_specs=[pl.BlockSpec((tq, 1024), lambda i: (i, 0)),
                  pl.BlockSpec((tq, 1024), lambda i: (i, COL_M // 1024)),
                  pl.BlockSpec((tq, 1024), lambda i: (i, COL_M // 1024 + 1)),
                  full(kn), full(v), full(gq)],
        out_specs=[pl.BlockSpec((tq, 2048), lambda i: (i, 0)),
                   pl.BlockSpec((MEM_TOKENS, 1024), lambda i: (0, 0)),
                   pl.BlockSpec((MEM_TOKENS, 1024), lambda i: (0, 0)),
                   pl.BlockSpec((8, MEM_HEAD_DIM), lambda i: (0, 0))],
        out_shape=[jax.ShapeDtypeStruct((s, 2048), BF16), jax.ShapeDtypeStruct((MEM_TOKENS, 1024), F32),
                   jax.ShapeDtypeStruct((MEM_TOKENS, 1024), F32), jax.ShapeDtypeStruct((8, MEM_HEAD_DIM), F32)],
        compiler_params=_params(("arbitrary",)),
    )(da, p_all, p_all, kn, v, gq)


def _mem_kv_bwd(dkn, dv, kraw, memn, mem, g_norm, w_kv, g_kn):
    m = mem.shape[0]

    def body(dkn_ref, dv_ref, kraw_ref, memn_ref, mem_ref, g_ref, w_ref, gk_ref, dw_ref, dgn_ref, dgk_ref, dkv_sc):
        gk_acc = jnp.zeros((8, MEM_HEAD_DIM), F32)
        for hh in range(MEM_HEADS):
            sl = slice(hh * MEM_HEAD_DIM, (hh + 1) * MEM_HEAD_DIM)
            dk, gg = _rms_bwd(kraw_ref[:, sl], gk_ref[...], dkn_ref[:, sl], MEM_HEAD_DIM)
            dkv_sc[:, sl] = dk.astype(BF16)
            gk_acc = gk_acc + _fold8(gg)
        dgk_ref[...] = gk_acc
        dkv_sc[:, 1024:] = dv_ref[...].astype(BF16)
        dkv = dkv_sc[...]
        dw_ref[...] = _dot(memn_ref[...], dkv, "tn")
        dmemn = _dot(dkv, w_ref[...], "nt")
        xf = mem_ref[...]
        r = lax.rsqrt(jnp.mean(xf * xf, axis=-1, keepdims=True) + EPS)
        dgn_ref[...] = _fold8(dmemn * (xf * r))

    return pl.pallas_call(
        body, name="mem_kv_bwd",
        out_shape=[jax.ShapeDtypeStruct((D_MODEL, D_MODEL), F32), jax.ShapeDtypeStruct((8, D_MODEL), F32),
                   jax.ShapeDtypeStruct((8, MEM_HEAD_DIM), F32)],
        scratch_shapes=[pltpu.VMEM((m, D_MODEL), BF16)],
        compiler_params=_params(),
    )(dkn, dv, kraw, memn, mem, g_norm, w_kv, g_kn)


def _dh_bwd(dps, w_all, row0, nrows, name, side=None):
    tm, tk = min(TS_ROW, nrows), 1024
    blk0 = row0 // tm
    widths = [a.shape[1] // tk for a in dps]
    starts = [int(v) for v in np.cumsum([0] + widths[:-1])]
    nk = sum(widths)

    def body(*refs):
        d_refs, w_ref, o_ref, acc_ref = refs[:5], refs[5], refs[6], refs[7]
        kk = pl.program_id(1)

        @pl.when(kk == 0)
        def _():
            acc_ref[...] = jnp.zeros_like(acc_ref)

        for d_ref, st, wd in zip(d_refs, starts, widths):
            @pl.when((kk >= st) & (kk < st + wd))
            def _(d_ref=d_ref):
                acc_ref[...] += _dot(d_ref[...], w_ref[...], "nt")

        @pl.when(kk == nk - 1)
        def _():
            o_ref[...] = acc_ref[...]

    def dspec(st, wd):
        return pl.BlockSpec((tm, tk), lambda i, kk: (blk0 + i, jnp.clip(kk - st, 0, wd - 1)))

    outs, extra = _pcall(
        body, side, name, (nrows // tm, nk),
        [dspec(st, wd) for st, wd in zip(starts, widths)] + [pl.BlockSpec((D_MODEL, tk), lambda i, kk: (0, kk))],
        [pl.BlockSpec((tm, D_MODEL), lambda i, kk: (i, 0))], [jax.ShapeDtypeStruct((nrows, D_MODEL), F32)],
        (*dps, w_all), ("parallel", "arbitrary"), scratch_shapes=[pltpu.VMEM((tm, D_MODEL), F32)])
    return outs[0], extra


def _dx_bwd(dh_a, dh_b, x, dy, g, side=None):
    s = x.shape[0]
    ts = min(TS_BWD, dh_a.shape[0])
    na, nb = dh_a.shape[0] // ts, dh_b.shape[0] // ts

    def body(dha_ref, dhb_ref, x_ref, dy_ref, g_ref, o_ref, dg_ref):
        @pl.when(pl.program_id(0) == 0)
        def _():
            dg_ref[...] = jnp.zeros_like(dg_ref)

        dh = jnp.where(pl.program_id(0) < na, dha_ref[...], dhb_ref[...])
        dx, gg = _rms_bwd(x_ref[...], g_ref[...], dh, D_MODEL)
        o_ref[...] = dy_ref[...] + dx
        dg_ref[...] += _fold8(gg)

    row = pl.BlockSpec((ts, D_MODEL), lambda i: (i, 0))
    return _pcall(
        body, side, "dx_bwd", (s // ts,),
        [pl.BlockSpec((ts, D_MODEL), lambda i: (jnp.minimum(i, na - 1), 0)),
         pl.BlockSpec((ts, D_MODEL), lambda i: (jnp.clip(i - na, 0, nb - 1), 0)),
         row, row, pl.BlockSpec((1, D_MODEL), lambda i: (0, 0))],
        [row, pl.BlockSpec((8, D_MODEL), lambda i: (0, 0))],
        [jax.ShapeDtypeStruct((s, D_MODEL), F32), jax.ShapeDtypeStruct((8, D_MODEL), F32)],
        (dh_a, dh_b, x, dy, g), ("arbitrary",))


def _pad128(v, n):
    return jnp.pad(v.reshape(1, n), ((0, 0), (0, 128 - n)))


def _local_step(x, positions, mem, target, comm, gains):
    half = QK_ROPE // 2
    inv_freq = jnp.power(ROPE_THETA, -jnp.arange(half, dtype=F32) / half)
    ang = positions.astype(F32)[:, None] * inv_freq
    cos, sin = jnp.cos(ang), jnp.sin(ang)
    zpad = jnp.zeros((x.shape[0], 64), F32)
    cs = jnp.concatenate([cos, cos, zpad], axis=1)
    sg = jnp.concatenate([-sin, sin, zpad], axis=1)
    g_qr, g_kr = _pad128(gains["mla_qn_rope_g"], 64), _pad128(gains["mla_kn_rope_g"], 64)
    g_qn, g_kn = gains["mla_qn_nope_g"], gains["mla_kn_nope_g"]

    def hosted_matmul(tag, a, b, mode, out_dtype):
        side = comm.side(tag)
        if side is None:
            return _matmul(a, b, mode, out_dtype, tag)
        out, extra = _matmul(a, b, mode, out_dtype, tag, side=side)
        comm.done(tag, extra)
        return out

    (p_all, h), extra = _proj_fwd(x, gains["norm_g"], comm.weight("w_all"), comm.side("proj_fwd"))
    comm.done("proj_fwd", extra)
    a_conv, co = _conv_fwd(p_all, comm.weight("conv_w"))
    (cqn, ckvn, krope), extra = _lora_fwd(p_all, gains["mla_q_norm_g"], gains["mla_kv_norm_g"], g_kr, cs, sg,
                                          comm.side("lora_fwd"))
    comm.done("lora_fwd", extra)
    q_raw = _matmul(cqn, comm.weight("w_uq"), "nn", BF16, "q_up")
    kv = _matmul(ckvn, comm.weight("w_ukv"), "nn", BF16, "kv_up")
    qcat = _q_prep(q_raw, g_qn, g_qr, cs, sg)
    kcat, vt = _k_prep(kv, krope, g_kn)
    mla_y, a_mla, lse = _flash_fwd(qcat, kcat, vt, p_all)
    memn, kraw, kn, vmem = _mem_kv_fwd(mem, gains["mem_norm_g"], comm.weight("w_mem_kv"), gains["mem_kn_g"])
    a_mem = _mem_attn_fwd(p_all, kn, vmem, gains["mem_qn_g"])
    o_c, o_m, o_e, merged = _merge_fwd(a_conv, comm.weight("w_conv_out"), a_mla, comm.weight("w_mla_out"), a_mem,
                                       comm.weight("w_mem_out"), p_all)
    dy, dyb, loss_parts = _out_fwd(merged, comm.weight("w_o"), x, target)

    comm.put("w_o", _matmul(merged, dyb, "tn", F32, "dw_o"))
    do_c, do_m, do_e, dp_g = _merge_bwd(dyb, comm.weight("w_o"), p_all, o_c, o_m, o_e)
    comm.put("w_conv_out", _matmul(a_conv, do_c, "tn", F32, "dw_conv_out"))
    comm.put("w_mla_out", _matmul(a_mla, do_m, "tn", F32, "dw_mla_out"))
    comm.put("w_mem_out", _matmul(a_mem, do_e, "tn", F32, "dw_mem_out"))
    da_conv = hosted_matmul("da_conv", do_c, comm.weight("w_conv_out"), "nt", BF16)
    da_mla = _matmul(do_m, comm.weight("w_mla_out"), "nt", BF16, "da_mla")
    da_mem = _matmul(do_e, comm.weight("w_mem_out"), "nt", BF16, "da_mem")
    dp_c, dconv_w = _conv_bwd(da_conv, p_all, co, comm.weight("conv_w"))
    dmla_y, dp_z, delta = _mla_gate_bwd(da_mla, mla_y, p_all)
    dqcat, dkcat, dv = _flash_bwd(qcat, kcat, kv, dmla_y, lse, delta)
    (dq_raw, dg_q), extra = _q_prep_bwd(dqcat, q_raw, g_qn, g_qr, cs, sg, comm.side("q_prep_bwd"))
    comm.done("q_prep_bwd", extra)
    (dkv, dkr, dg_kn), extra = _k_prep_bwd(dkcat, dv, kv, g_kn, comm.side("k_prep_bwd"))
    comm.done("k_prep_bwd", extra)
    comm.put("w_uq", _matmul(cqn, dq_raw, "tn", F32, "dw_uq"))
    comm.put("w_ukv", _matmul(ckvn, dkv, "tn", F32, "dw_ukv"))
    dcqn = _matmul(dq_raw, comm.weight("w_uq"), "nt", F32, "dcqn")
    dckvn = _matmul(dkv, comm.weight("w_ukv"), "nt", F32, "dckvn")
    dp_s, dg_qn, dg_kvn, dg_kr = _lora_bwd(dcqn, dckvn, dkr, p_all, gains["mla_q_norm_g"],
                                            gains["mla_kv_norm_g"], g_kr, cs, sg)
    dp_m, dkn, dvm, dg_mq = _mem_attn_bwd(da_mem, p_all, kn, vmem, gains["mem_qn_g"])
    dw_mem_kv, dg_mn, dg_mk = _mem_kv_bwd(dkn, dvm, kraw, memn, mem, gains["mem_norm_g"],
                                          comm.weight("w_mem_kv"), gains["mem_kn_g"])
    comm.put("w_mem_kv", dw_mem_kv)
    dps = [dp_g, dp_z, dp_c, dp_m, dp_s]
    comm.put("w_all", [hosted_matmul("dw_in_%d" % k, h, d, "tn", F32) for k, d in enumerate(dps)])
    s = x.shape[0]
    dh_a, extra = _dh_bwd(dps, comm.weight("w_all"), 0, s // 4, "dh_a", comm.side("dh_a"))
    comm.done("dh_a", extra)
    dh_b, extra = _dh_bwd(dps, comm.weight("w_all"), s // 4, s - s // 4, "dh_b", comm.side("dh_b"))
    comm.done("dh_b", extra)
    (grad_x, dg_n), extra = _dx_bwd(dh_a, dh_b, x, dy, gains["norm_g"], comm.side("dx_bwd"))
    comm.done("dx_bwd", extra)

    gsmall = {
        "norm_g": dg_n.sum(0), "conv_w": dconv_w.sum(1), "mla_q_norm_g": dg_qn.sum(0),
        "mla_kv_norm_g": dg_kvn.sum(0), "mla_qn_nope_g": dg_q[0].sum(0), "mla_qn_rope_g": dg_q[1].sum(0)[:64],
        "mla_kn_nope_g": dg_kn.sum(0), "mla_kn_rope_g": dg_kr.sum(0)[:64], "mem_norm_g": dg_mn.sum(0),
        "mem_qn_g": dg_mq.sum(0), "mem_kn_g": dg_mk.sum(0),
    }
    return loss_parts, grad_x, gsmall


def _me():
    return lax.axis_index("x"), lax.axis_index("y"), lax.axis_index("c")


def _chips(x, y):
    return [(1 - x, y), (x, 1 - y), (1 - x, 1 - y)]


def _allgather_shards(xs, name):
    r, c = xs.shape
    hr = r // 2

    def body(x_ref, out_ref, send_sems, recv_sems):
        x, y, cc = _me()
        sibling = (x, y, 1 - cc)
        chips = _chips(x, y)

        def rows(px, py, pc):
            return out_ref.at[pl.ds((4 * px + 2 * py + pc) * hr, hr), :]

        def copy(k, block, to, src=None):
            return pltpu.make_async_remote_copy(
                src_ref=rows(*block) if src is None else src, dst_ref=rows(*block),
                send_sem=send_sems.at[k], recv_sem=recv_sems.at[k], device_id=to, device_id_type=MESH)

        my_half = x_ref.at[pl.ds(cc * hr, hr), :]
        first = [copy(j, (x, y, cc), (*chip, cc), src=my_half) for j, chip in enumerate(chips)]
        for cp in first:
            cp.start()
        passed = [copy(3 + j, (*chip, cc), sibling) for j, chip in enumerate(chips)]
        for j, chip in enumerate(chips):
            copy(j, (*chip, cc), (x, y, cc)).wait_recv()
            passed[j].start()
        for j, chip in enumerate(chips):
            copy(3 + j, (*chip, 1 - cc), (x, y, cc)).wait_recv()
        for cp in first + passed:
            cp.wait_send()

    out = pl.pallas_call(
        body, name=name, in_specs=[ANY], out_specs=ANY,
        out_shape=jax.ShapeDtypeStruct((8 * hr, c), xs.dtype),
        scratch_shapes=[pltpu.SemaphoreType.DMA((6,)), pltpu.SemaphoreType.DMA((6,))],
    )(xs)
    xi, yi, _ = _me()
    return lax.dynamic_update_slice(out.reshape(4, r, c), xs[None], (2 * xi + yi, 0, 0))


def _remote(src, dst, send_sems, recv_sems, k, to):
    return pltpu.make_async_remote_copy(src_ref=src, dst_ref=dst, send_sem=send_sems.at[k], recv_sem=recv_sems.at[k],
                                        device_id=to, device_id_type=MESH)


def _side_gather_ici(shards):
    def build(ins, outs, send_sems, recv_sems):
        x, y, cc = _me()
        pairs = []
        for a, (x_ref, o_ref) in enumerate(zip(ins, outs)):
            hr = x_ref.shape[0] // 2
            my_half = x_ref.at[pl.ds(cc * hr, hr), :]
            for j, (px, py) in enumerate(_chips(x, y)):
                mine = o_ref.at[pl.ds((4 * x + 2 * y + cc) * hr, hr), :]
                theirs = o_ref.at[pl.ds((4 * px + 2 * py + cc) * hr, hr), :]
                pairs.append((_remote(my_half, mine, send_sems, recv_sems, 3 * a + j, (px, py, cc)),
                              _remote(my_half, theirs, send_sems, recv_sems, 3 * a + j, (px, py, cc))))
        return pairs

    shapes = [jax.ShapeDtypeStruct((4 * s.shape[0], s.shape[1]), s.dtype) for s in shards]
    return _Side(shards, shapes, 3 * len(shards), build)


def _side_gather_pass(bufs):
    def build(ins, outs, send_sems, recv_sems):
        x, y, cc = _me()
        pairs = []
        for a, o_ref in enumerate(outs):
            hr = o_ref.shape[0] // 8
            for j, (px, py) in enumerate(_chips(x, y)):
                got = o_ref.at[pl.ds((4 * px + 2 * py + cc) * hr, hr), :]
                coming = o_ref.at[pl.ds((4 * px + 2 * py + 1 - cc) * hr, hr), :]
                pairs.append((_remote(got, got, send_sems, recv_sems, 3 * a + j, (x, y, 1 - cc)),
                              _remote(got, coming, send_sems, recv_sems, 3 * a + j, (x, y, 1 - cc))))
        return pairs

    shapes = [jax.ShapeDtypeStruct(b.shape, b.dtype) for b in bufs]
    return _Side(bufs, shapes, 3 * len(bufs), build, aliases={a: a for a in range(len(bufs))})


def _side_swap(srcs, by_half):
    def build(ins, outs, send_sems, recv_sems):
        x, y, cc = _me()
        pairs = []
        for a, (s_ref, o_ref) in enumerate(zip(ins, outs)):
            cp = _remote(s_ref.at[1 - cc] if by_half else s_ref, o_ref, send_sems, recv_sems, a, (x, y, 1 - cc))
            pairs.append((cp, cp))
        return pairs

    shapes = [jax.ShapeDtypeStruct(s.shape[1:] if by_half else s.shape, s.dtype) for s in srcs]
    return _Side(srcs, shapes, len(srcs), build)


def _side_chips(ts):
    def build(ins, outs, send_sems, recv_sems):
        x, y, cc = _me()
        pairs = []
        for a, (t_ref, r_ref) in enumerate(zip(ins, outs)):
            for j, (px, py) in enumerate(_chips(x, y)):
                cp = _remote(t_ref.at[2 * px + py], r_ref.at[j], send_sems, recv_sems, 3 * a + j, (px, py, cc))
                pairs.append((cp, cp))
        return pairs

    shapes = [jax.ShapeDtypeStruct((3,) + t.shape[1:], t.dtype) for t in ts]
    return _Side(ts, shapes, 3 * len(ts), build)


def _row_tile(hr, c, itemsize):
    t = hr
    while t > 16 and t * c * itemsize > (2 << 20):
        t //= 2
    return t


def _add_own_half(gb, got, sel, name):
    _, _, hr, c = gb.shape
    tr = _row_tile(hr, c, 4)

    def body(sel_ref, a_ref, b_ref, o_ref):
        o_ref[...] = (a_ref[0].astype(F32) + b_ref[...].astype(F32)).astype(BF16)

    return pl.pallas_call(
        body, name=name,
        grid_spec=pltpu.PrefetchScalarGridSpec(
            num_scalar_prefetch=1, grid=(4, hr // tr),
            in_specs=[pl.BlockSpec((1, 1, tr, c), lambda k, i, sel_ref: (sel_ref[0], k, i, 0)),
                      pl.BlockSpec((1, tr, c), lambda k, i, sel_ref: (k, i, 0))],
            out_specs=pl.BlockSpec((1, tr, c), lambda k, i, sel_ref: (k, i, 0))),
        out_shape=jax.ShapeDtypeStruct(got.shape, BF16),
        compiler_params=_params(("parallel", "parallel")),
    )(sel, gb, got)


def _sum_partials(t, rcv, sel, name):
    _, hr, c = t.shape
    tr = _row_tile(hr, c, 4)

    def body(sel_ref, t_ref, r_ref, o_ref):
        o_ref[...] = ((t_ref[0].astype(F32) + r_ref[0].astype(F32)) + r_ref[1].astype(F32)) + r_ref[2].astype(F32)

    return pl.pallas_call(
        body, name=name,
        grid_spec=pltpu.PrefetchScalarGridSpec(
            num_scalar_prefetch=1, grid=(hr // tr,),
            in_specs=[pl.BlockSpec((1, tr, c), lambda i, sel_ref: (sel_ref[0], i, 0)),
                      pl.BlockSpec((3, tr, c), lambda i, sel_ref: (0, i, 0))],
            out_specs=pl.BlockSpec((tr, c), lambda i, sel_ref: (i, 0))),
        out_shape=jax.ShapeDtypeStruct((hr, c), F32),
        compiler_params=_params(("parallel",)),
    )(sel, t, rcv)


class _GroupReduce:
    def __init__(self):
        self.names, self.gb, self.t, self.fh, self.other = [], [], [], [], []

    def put(self, name, cut):
        _, r, c = cut.shape
        self.names.append(name)
        self.gb.append(cut.astype(BF16).reshape(4, 2, r // 2, c).transpose(1, 0, 2, 3))

    def side(self, stage):
        return (_side_swap(self.gb, True), _side_chips(self.t), _side_swap(self.fh, False))[stage - 1]

    def done(self, stage, outs):
        x, y, cc = _me()
        if stage == 1:
            sel = jnp.reshape(cc, (1,)).astype(jnp.int32)
            self.t = [_add_own_half(gb, got, sel, "rs_add_" + n) for n, gb, got in zip(self.names, self.gb, outs)]
        elif stage == 2:
            sel = jnp.reshape(2 * x + y, (1,)).astype(jnp.int32)
            self.fh = [_sum_partials(t, rcv, sel, "rs_sum_" + n) for n, t, rcv in zip(self.names, self.t, outs)]
        else:
            self.other = list(outs)

    def result(self):
        return {n: (fh, other) for n, fh, other in zip(self.names, self.fh, self.other)}


def _allreduce_small(v, tag):
    r = v.shape[0]

    def body(v_ref, out_ref, buf_ref, send_sems, recv_sems):
        x, y, cc = _me()
        me = 4 * x + 2 * y + cc
        buf_ref[pl.ds(pl.multiple_of(me * r, 8), r), :] = v_ref[...]
        peers = [(x, y, 1 - cc)] + [(px, py, pc) for (px, py) in _chips(x, y) for pc in (cc, 1 - cc)]
        cps = []
        for k, (px, py, pc) in enumerate(peers):
            mine = buf_ref.at[pl.ds(pl.multiple_of(me * r, 8), r), :]
            cps.append(pltpu.make_async_remote_copy(
                src_ref=v_ref, dst_ref=mine, send_sem=send_sems.at[k], recv_sem=recv_sems.at[k],
                device_id=(px, py, pc), device_id_type=MESH))
        for cp in cps:
            cp.start()
        for cp in cps:
            cp.wait()
        acc = buf_ref[0:r, :]
        for d in range(1, 8):
            acc = acc + buf_ref[d * r:(d + 1) * r, :]
        out_ref[...] = acc

    vm = pl.BlockSpec(memory_space=pltpu.VMEM)
    return pl.pallas_call(
        body, name="allreduce_small_" + tag, in_specs=[vm], out_specs=vm,
        out_shape=jax.ShapeDtypeStruct((r, 128), F32),
        scratch_shapes=[pltpu.VMEM((8 * r, 128), F32), pltpu.SemaphoreType.DMA((7,)), pltpu.SemaphoreType.DMA((7,))],
    )(v)


def _adamw(w, g, m, v, name):
    r, c = w.shape
    tr = _row_tile(r, c, 4) if r >= 16 else r
    c1 = 1.0 / (1.0 - ADAM_B1 ** ADAM_STEP)
    c2 = 1.0 / (1.0 - ADAM_B2 ** ADAM_STEP)

    def body(w_ref, g_ref, m_ref, v_ref, d_ref, mo_ref, vo_ref):
        gg = g_ref[...]
        mn = ADAM_B1 * m_ref[...] + (1.0 - ADAM_B1) * gg
        vn = ADAM_B2 * v_ref[...] + (1.0 - ADAM_B2) * (gg * gg)
        mo_ref[...] = mn
        vo_ref[...] = vn
        d_ref[...] = -ADAM_LR * ((mn * c1) / (jnp.sqrt(vn * c2) + ADAM_EPS) + ADAM_WD * w_ref[...])

    blk = pl.BlockSpec((tr, c), lambda i: (i, 0))
    return pl.pallas_call(
        body, name=name, grid=(r // tr,), in_specs=[blk] * 4, out_specs=[blk] * 3,
        out_shape=[jax.ShapeDtypeStruct((r, c), F32)] * 3,
        compiler_params=_params(("parallel",)),
    )(w, g, m, v)


def _adamw_halves(w, g_own, g_other, m, v, sel, name):
    r, c = w.shape
    hr = r // 2
    tr = _row_tile(hr, c, 4)
    nh = hr // tr
    c1 = 1.0 / (1.0 - ADAM_B1 ** ADAM_STEP)
    c2 = 1.0 / (1.0 - ADAM_B2 ** ADAM_STEP)

    def body(sel_ref, w_ref, go_ref, gx_ref, m_ref, v_ref, d_ref, mo_ref, vo_ref, g_ref):
        mine = (pl.program_id(0) // nh) == sel_ref[0]
        gg = jnp.where(mine, go_ref[...], gx_ref[...])
        g_ref[...] = gg
        mn = ADAM_B1 * m_ref[...] + (1.0 - ADAM_B1) * gg
        vn = ADAM_B2 * v_ref[...] + (1.0 - ADAM_B2) * (gg * gg)
        mo_ref[...] = mn
        vo_ref[...] = vn
        d_ref[...] = -ADAM_LR * ((mn * c1) / (jnp.sqrt(vn * c2) + ADAM_EPS) + ADAM_WD * w_ref[...])

    blk = pl.BlockSpec((tr, c), lambda i, sel_ref: (i, 0))
    half = pl.BlockSpec((tr, c), lambda i, sel_ref: (i % nh, 0))
    return pl.pallas_call(
        body, name=name,
        grid_spec=pltpu.PrefetchScalarGridSpec(
            num_scalar_prefetch=1, grid=(r // tr,),
            in_specs=[blk, half, half, blk, blk], out_specs=[blk] * 4),
        out_shape=[jax.ShapeDtypeStruct((r, c), F32)] * 4,
        compiler_params=_params(("parallel",)),
    )(sel, w, g_own, g_other, m, v)


WEIGHTS = ['norm_g', 'w_in', 'conv_w', 'w_conv_out', 'mla_q_norm_g', 'w_uq', 'mla_kv_norm_g', 'w_ukv',
           'mla_qn_nope_g', 'mla_qn_rope_g', 'mla_kn_nope_g', 'mla_kn_rope_g', 'w_mla_out', 'mem_norm_g',
           'w_mem_kv', 'mem_qn_g', 'mem_kn_g', 'w_mem_out', 'w_o']
COL_SHARDED = ['w_in', 'w_conv_out', 'w_uq', 'w_ukv', 'w_mem_out']
ROW_SHARDED = ['w_mla_out', 'w_mem_kv', 'w_o']
SMALL = ['norm_g', 'conv_w', 'mla_q_norm_g', 'mla_kv_norm_g', 'mla_qn_nope_g', 'mla_qn_rope_g', 'mla_kn_nope_g',
         'mla_kn_rope_g', 'mem_norm_g', 'mem_qn_g', 'mem_kn_g']
SMALL_SIZES = [2048, 3072, 512, 512, 128, 64, 128, 64, 2048, 256, 256]
PACK_ROWS = 72


def _full_from_shards(name, g4):
    if name in COL_SHARDED:
        return g4.transpose(1, 0, 2).reshape(g4.shape[1], 4 * g4.shape[2])
    return g4.reshape(4 * g4.shape[1], g4.shape[2])


def _w_all_from_w_in(w_in):
    conv, small, mz, memq, gates = (w_in[:, 0:4096], w_in[:, 4096:5184], w_in[:, 5184:7232],
                                    w_in[:, 7232:9280], w_in[:, 9280:15424])
    return jnp.concatenate([gates, mz, conv, memq, small, jnp.zeros((D_MODEL, 2048 - 1088), BF16)], axis=1)


def _pad_w_uq(w_uq):
    return jnp.pad(w_uq.reshape(512, MLA_HEADS, 192), ((0, 0), (0, 0), (0, 64))).reshape(512, 4096)


def _grad_to_reference_layout(name, g):
    if name == "w_all":
        dw_g, dw_z, dw_c, dw_m, dw_s = g
        return "w_in", jnp.concatenate([dw_c, dw_s[:, :1088], dw_z, dw_m, dw_g], axis=1)
    if name == "w_uq":
        return name, g.reshape(512, MLA_HEADS, HEAD_PAD)[:, :, :192].reshape(512, 3072)
    return name, g


LATE_WEIGHTS = ['w_conv_out', 'w_uq', 'w_ukv', 'w_mem_out', 'w_mla_out', 'w_mem_kv', 'w_o']
HOSTS = {"proj_fwd": ("gather", 1), "lora_fwd": ("gather", 2),
         "da_conv": ("g1", 1), "q_prep_bwd": ("g1", 2), "k_prep_bwd": ("g1", 3),
         "dw_in_0": ("g2", 1), "dw_in_2": ("g2", 2), "dw_in_4": ("g2", 3),
         "dh_a": ("g3", 1), "dh_b": ("g3", 2), "dx_bwd": ("g3", 3)}
GROUP_OF = {"w_o": "g1", "w_conv_out": "g1", "w_mla_out": "g1", "w_mem_out": "g1",
            "w_uq": "g2", "w_ukv": "g2", "w_mem_kv": "g2", "w_in": "g3"}


class _Comm:
    def __init__(self, shards, w_in_full, conv_full):
        self.shards = shards
        self.w = {"w_all": _w_all_from_w_in(w_in_full), "conv_w": conv_full}
        self.bufs = None
        self.groups = {"g1": _GroupReduce(), "g2": _GroupReduce(), "g3": _GroupReduce()}

    def weight(self, name):
        return self.w[name]

    def side(self, tag):
        if tag not in HOSTS:
            return None
        kind, stage = HOSTS[tag]
        if kind == "gather":
            return _side_gather_ici([self.shards[n] for n in LATE_WEIGHTS]) if stage == 1 else \
                _side_gather_pass(self.bufs)
        return self.groups[kind].side(stage)

    def done(self, tag, outs):
        kind, stage = HOSTS[tag]
        if kind != "gather":
            self.groups[kind].done(stage, outs)
        elif stage == 1:
            self.bufs = list(outs)
        else:
            xi, yi, _ = _me()
            for n, buf in zip(LATE_WEIGHTS, outs):
                own = self.shards[n]
                g4 = lax.dynamic_update_slice(buf.reshape((4,) + own.shape), own[None], (2 * xi + yi, 0, 0))
                self.w[n] = _full_from_shards(n, g4)
            self.w["w_uq"] = _pad_w_uq(self.w["w_uq"])

    def put(self, name, g):
        name, g = _grad_to_reference_layout(name, g)
        r, c = g.shape
        cut = g.reshape(r, 4, c // 4).transpose(1, 0, 2) if name in COL_SHARDED else g.reshape(4, r // 4, c)
        self.groups[GROUP_OF[name]].put(name, cut)

    def reduced(self):
        out = {}
        for grp in self.groups.values():
            out.update(grp.result())
        return out


def kernel(x, positions, mem, norm_g, w_in, conv_w, w_conv_out, mla_q_norm_g, w_uq, mla_kv_norm_g, w_ukv, mla_qn_nope_g, mla_qn_rope_g, mla_kn_nope_g, mla_kn_rope_g, w_mla_out, mem_norm_g, w_mem_kv, mem_qn_g, mem_kn_g, w_mem_out, w_o, loss_target, m_norm_g, m_w_in, m_conv_w, m_w_conv_out, m_mla_q_norm_g, m_w_uq, m_mla_kv_norm_g, m_w_ukv, m_mla_qn_nope_g, m_mla_qn_rope_g, m_mla_kn_nope_g, m_mla_kn_rope_g, m_w_mla_out, m_mem_norm_g, m_w_mem_kv, m_mem_qn_g, m_mem_kn_g, m_w_mem_out, m_w_o, v_norm_g, v_w_in, v_conv_w, v_w_conv_out, v_mla_q_norm_g, v_w_uq, v_mla_kv_norm_g, v_w_ukv, v_mla_qn_nope_g, v_mla_qn_rope_g, v_mla_kn_nope_g, v_mla_kn_rope_g, v_w_mla_out, v_mem_norm_g, v_w_mem_kv, v_mem_qn_g, v_mem_kn_g, v_w_mem_out, v_w_o):
    args = locals()
    w = {n: args[n][0] for n in WEIGHTS}
    m1 = {n: args["m_" + n][0] for n in WEIGHTS}
    v2 = {n: args["v_" + n][0] for n in WEIGHTS}
    xi, yi, ci = _me()
    chip = 2 * xi + yi

    conv_slot = jnp.zeros((3, 4, 256), F32).at[:, chip, :].set(w["conv_w"] * jnp.where(ci == 0, 1.0, 0.0))
    pre = jnp.zeros((PACK_ROWS * 128,), F32).at[0:3072].set(conv_slot.reshape(3072))
    conv_full = _allreduce_small(pre.reshape(PACK_ROWS, 128), "conv_w").reshape(-1)[0:3072].reshape(3, 1024)

    w_in_full = _full_from_shards("w_in", _allgather_shards(w["w_in"].astype(BF16), "ag_w_in"))
    comm = _Comm({n: w[n].astype(BF16) for n in LATE_WEIGHTS}, w_in_full, conv_full)
    gains = {n: w[n].reshape(1, -1) for n in SMALL if n != "conv_w"}

    loss_parts, grad_x, gsmall = _local_step(x[0], positions[0], mem[0], loss_target[0], comm, gains)

    loss_local = 0.5 * jnp.sum(loss_parts) * (1.0 / D_MODEL)
    flat = jnp.concatenate([gsmall[n].reshape(-1) for n in SMALL] + [loss_local.reshape(1)])
    flat = jnp.pad(flat, (0, PACK_ROWS * 128 - flat.shape[0]))
    tot = _allreduce_small(flat.reshape(PACK_ROWS, 128), "grads").reshape(-1)
    grads = {}
    off = 0
    for n, sz in zip(SMALL, SMALL_SIZES):
        grads[n] = tot[off:off + sz]
        off += sz
    loss = tot[off]
    grads["conv_w"] = lax.dynamic_slice(grads["conv_w"].reshape(3, 1024), (0, chip * 256), (3, 256))
    for n in SMALL:
        grads[n] = grads[n].reshape(w[n].shape)

    deltas, new_m, new_v = {}, {}, {}
    for n in SMALL:
        shp = w[n].shape
        two_d = (lambda a: a.reshape(1, -1)) if len(shp) == 1 else (lambda a: a)
        d, mn, vn = _adamw(two_d(w[n]), two_d(grads[n]), two_d(m1[n]), two_d(v2[n]), "adamw_" + n)
        deltas[n], new_m[n], new_v[n] = d.reshape(shp), mn.reshape(shp), vn.reshape(shp)

    sel_c = jnp.reshape(ci, (1,)).astype(jnp.int32)
    reduced = comm.reduced()
    for n in COL_SHARDED + ROW_SHARDED:
        g_own, g_other = reduced[n]
        deltas[n], new_m[n], new_v[n], grads[n] = _adamw_halves(w[n], g_own, g_other, m1[n], v2[n], sel_c,
                                                                 "adamw_" + n)

    lead = lambda a: a[None]
    return (loss, grad_x[None], *[lead(grads[n]) for n in WEIGHTS], *[lead(deltas[n]) for n in WEIGHTS],
            *[lead(new_m[n]) for n in WEIGHTS], *[lead(new_v[n]) for n in WEIGHTS])
```

```python
import functools

import numpy as np
import jax
import jax.numpy as jnp
from jax import lax
from jax.experimental import pallas as pl
from jax.experimental.pallas import tpu as pltpu

F32 = jnp.float32
BF16 = jnp.bfloat16
MESH = pl.DeviceIdType.MESH

D_MODEL = 2048
EPS = 1e-6
CHUNK = 64
MLA_HEADS = 16
QK_NOPE = 128
QK_ROPE = 64
HEAD_PAD = 256
MEM_TOKENS = 256
MEM_HEADS = 4
MEM_HEAD_DIM = 256
ROPE_THETA = 10000.0
MLA_SCALE = (QK_NOPE + QK_ROPE) ** -0.5
MEM_SCALE = MEM_HEAD_DIM ** -0.5
LN2 = 0.6931471805599453
Q_SCALE = MLA_SCALE / LN2
NEG = -1e30

ADAM_LR = 0.001
ADAM_B1 = 0.9
ADAM_B2 = 0.999
ADAM_EPS = 1e-08
ADAM_WD = 0.01
ADAM_STEP = 10

NP = 16384
COL_G, COL_Z, COL_C, COL_M, COL_S = 0, 6144, 8192, 12288, 14336

TM_PROJ = 1024
TN_PROJ = 1024
TS_ROW = 512
TS_BWD = 256
TQ_ATT = 1024
TK_ATT = 1024
TR_ATT = 256
VMEM_LIMIT = 56 * 1024 * 1024

NT_DIMS = (((1,), (1,)), ((), ()))
TN_DIMS = (((0,), (0,)), ((), ()))


def _dot(a, b, mode="nn"):
    if mode == "nn":
        return jnp.dot(a, b, preferred_element_type=F32)
    return lax.dot_general(a, b, NT_DIMS if mode == "nt" else TN_DIMS, preferred_element_type=F32)


def _sigmoid(z):
    return 1.0 / (1.0 + jnp.exp(-z))


def _params(sem=None, vmem=VMEM_LIMIT):
    return pltpu.CompilerParams(dimension_semantics=sem, vmem_limit_bytes=vmem)


def _fold8(v):
    r, c = v.shape
    return v.reshape(r // 8, 8, c).sum(axis=0)


def _swap32(t):
    lane = lax.broadcasted_iota(jnp.int32, t.shape, 1)
    return jnp.where(lane < 32, pltpu.roll(t, 96, 1), pltpu.roll(t, 32, 1))


def _rope(t, cs, sg):
    return t * cs + _swap32(t) * sg


def _rope_bwd(d, cs, sg):
    return d * cs + _swap32(d * sg)


def _rms_fwd(xf, g, n):
    r = lax.rsqrt(jnp.sum(xf * xf, axis=-1, keepdims=True) * (1.0 / n) + EPS)
    return xf * r * g


def _rms_bwd(xf, g, dy, n):
    r = lax.rsqrt(jnp.sum(xf * xf, axis=-1, keepdims=True) * (1.0 / n) + EPS)
    xhat = xf * r
    dyg = dy * g
    dx = r * (dyg - xhat * (jnp.sum(dyg * xhat, axis=-1, keepdims=True) * (1.0 / n)))
    return dx, dy * xhat


ANY = pl.BlockSpec(memory_space=pl.ANY)


class _Side:
    def __init__(self, inputs, out_shapes, n_copies, build, aliases=()):
        self.inputs, self.out_shapes, self.n_copies = list(inputs), list(out_shapes), n_copies
        self.build, self.aliases = build, dict(aliases)


def _pcall(body, side, name, grid, in_specs, out_specs, out_shape, args, sem, scratch_shapes=(), prefetch=()):
    n_pre, n_in, n_out = len(prefetch), len(in_specs), len(out_specs)
    if side is None:
        outs = pl.pallas_call(
            body, name=name,
            grid_spec=pltpu.PrefetchScalarGridSpec(num_scalar_prefetch=n_pre, grid=grid, in_specs=list(in_specs),
                                                   out_specs=list(out_specs), scratch_shapes=list(scratch_shapes)),
            out_shape=list(out_shape), compiler_params=_params(sem),
        )(*prefetch, *args)
        return list(outs), []
    ns_in, ns_out = len(side.inputs), len(side.out_shapes)

    def wrapped(*refs):
        pre, r = refs[:n_pre], refs[n_pre:]
        ins, s_ins = r[:n_in], r[n_in:n_in + ns_in]
        outs = r[n_in + ns_in:n_in + ns_in + n_out]
        s_outs = r[n_in + ns_in + n_out:n_in + ns_in + n_out + ns_out]
        scr, send_sems, recv_sems = r[n_in + ns_in + n_out + ns_out:-2], r[-2], r[-1]
        pairs = side.build(s_ins, s_outs, send_sems, recv_sems)
        first = pl.program_id(0) == 0
        last = pl.program_id(0) == grid[0] - 1
        for d in range(1, len(grid)):
            first = first & (pl.program_id(d) == 0)
            last = last & (pl.program_id(d) == grid[d] - 1)

        @pl.when(first)
        def _():
            for send, _ in pairs:
                send.start()

        body(*pre, *ins, *outs, *scr)

        @pl.when(last)
        def _():
            for send, recv in pairs:
                send.wait_send()
                recv.wait_recv()

    outs = pl.pallas_call(
        wrapped, name=name,
        grid_spec=pltpu.PrefetchScalarGridSpec(
            num_scalar_prefetch=n_pre, grid=grid, in_specs=list(in_specs) + [ANY] * ns_in,
            out_specs=list(out_specs) + [ANY] * ns_out,
            scratch_shapes=list(scratch_shapes) + [pltpu.SemaphoreType.DMA((side.n_copies,)),
                                                   pltpu.SemaphoreType.DMA((side.n_copies,))]),
        out_shape=list(out_shape) + side.out_shapes,
        input_output_aliases={n_pre + n_in + i: n_out + o for i, o in side.aliases.items()},
        compiler_params=_params(("arbitrary",) * len(grid)),
    )(*prefetch, *args, *side.inputs)
    return list(outs[:n_out]), list(outs[n_out:])


def _matmul(a, b, mode, out_dtype, name, tm=1024, tn=1024, tk=1024, side=None):
    if mode == "nn":
        (m, k), (_, n) = a.shape, b.shape
    elif mode == "nt":
        (m, k), (n, _) = a.shape, b.shape
    else:
        (k, m), (_, n) = a.shape, b.shape
    tm, tn, tk = min(tm, m), min(tn, n), min(tk, k)
    nk = k // tk
    if mode == "tn":
        a_spec = pl.BlockSpec((tk, tm), lambda i, j, kk: (kk, i))
    else:
        a_spec = pl.BlockSpec((tm, tk), lambda i, j, kk: (i, kk))
    if mode == "nt":
        b_spec = pl.BlockSpec((tn, tk), lambda i, j, kk: (j, kk))
    else:
        b_spec = pl.BlockSpec((tk, tn), lambda i, j, kk: (kk, j))

    def body(a_ref, b_ref, o_ref, acc_ref):
        kk = pl.program_id(2)

        @pl.when(kk == 0)
        def _():
            acc_ref[...] = jnp.zeros_like(acc_ref)

        acc_ref[...] += _dot(a_ref[...].astype(BF16), b_ref[...].astype(BF16), mode)

        @pl.when(kk == nk - 1)
        def _():
            o_ref[...] = acc_ref[...].astype(out_dtype)

    outs, extra = _pcall(
        body, side, name, (m // tm, n // tn, nk), [a_spec, b_spec],
        [pl.BlockSpec((tm, tn), lambda i, j, kk: (i, j))], [jax.ShapeDtypeStruct((m, n), out_dtype)], (a, b),
        ("parallel", "parallel", "arbitrary"), scratch_shapes=[pltpu.VMEM((tm, tn), F32)])
    return outs[0] if side is None else (outs[0], extra)


def _proj_fwd(x, g, w_all_t, side=None):
    s = x.shape[0]
    tm, tn = min(TM_PROJ, s), TN_PROJ

    def body(x_ref, g_ref, w_ref, p_ref, h_ref):
        @pl.when(pl.program_id(1) == 0)
        def _():
            h_ref[...] = _rms_fwd(x_ref[...], g_ref[...], D_MODEL).astype(BF16)

        p_ref[...] = _dot(h_ref[...], w_ref[...], "nt").astype(BF16)

    return _pcall(
        body, side, "proj_fwd", (s // tm, NP // tn),
        [pl.BlockSpec((tm, D_MODEL), lambda i, j: (i, 0)),
         pl.BlockSpec((1, D_MODEL), lambda i, j: (0, 0)),
         pl.BlockSpec((tn, D_MODEL), lambda i, j: (j, 0))],
        [pl.BlockSpec((tm, tn), lambda i, j: (i, j)), pl.BlockSpec((tm, D_MODEL), lambda i, j: (i, 0))],
        [jax.ShapeDtypeStruct((s, NP), BF16), jax.ShapeDtypeStruct((s, D_MODEL), BF16)], (x, g, w_all_t),
        ("parallel", "arbitrary"))


def _shift_rows(v, carry, j, ts, back):
    if back:
        main = pltpu.roll(v, j, 0)
        edge = pltpu.roll(jnp.concatenate([carry, v[:8]], axis=0), j, 0)[8:]
        return jnp.concatenate([edge, main[8:]], axis=0)
    main = pltpu.roll(v, ts - j, 0)
    edge = pltpu.roll(jnp.concatenate([v[ts - 8:], carry], axis=0), 16 - j, 0)[:8]
    return jnp.concatenate([main[:ts - 8], edge], axis=0)


def _conv_fwd(p_all, conv_w):
    s = p_all.shape[0]
    ts = min(TS_ROW, s)
    c0 = COL_C // 1024

    def body(cg_ref, bg_ref, u_ref, z_ref, w_ref, a_ref, co_ref, carry_ref):
        @pl.when(pl.program_id(0) == 0)
        def _():
            carry_ref[...] = jnp.zeros_like(carry_ref)

        p = cg_ref[...].astype(F32) * u_ref[...].astype(F32)
        carry = carry_ref[...]
        co = (w_ref[2:3, :] * p + w_ref[1:2, :] * _shift_rows(p, carry, 1, ts, True)
              + w_ref[0:1, :] * _shift_rows(p, carry, 2, ts, True))
        carry_ref[...] = p[ts - 8:]
        z = z_ref[...].astype(F32)
        a_ref[...] = (bg_ref[...].astype(F32) * co * (z * _sigmoid(z))).astype(BF16)
        co_ref[...] = co.astype(BF16)

    seg = lambda c: pl.BlockSpec((ts, 1024), lambda i: (i, c0 + c))
    return pl.pallas_call(
        body, name="conv_fwd", grid=(s // ts,),
        in_specs=[seg(0), seg(1), seg(2), seg(3), pl.BlockSpec((3, 1024), lambda i: (0, 0))],
        out_specs=[pl.BlockSpec((ts, 1024), lambda i: (i, 0))] * 2,
        out_shape=[jax.ShapeDtypeStruct((s, 1024), BF16)] * 2,
        scratch_shapes=[pltpu.VMEM((8, 1024), F32)],
        compiler_params=_params(("arbitrary",)),
    )(p_all, p_all, p_all, p_all, conv_w)


def _lora_fwd(p_all, gq, gkv, gkr, cs, sg, side=None):
    s = p_all.shape[0]
    ts = min(TS_ROW, s)

    def body(cq_ref, ckv_ref, kr_ref, gq_ref, gkv_ref, gkr_ref, cs_ref, sg_ref, cqn_ref, ckvn_ref, krope_ref):
        cqn_ref[...] = _rms_fwd(cq_ref[...].astype(F32), gq_ref[...], 512).astype(BF16)
        ckvn_ref[...] = _rms_fwd(ckv_ref[...].astype(F32), gkv_ref[...], 512).astype(BF16)
        kn = _rms_fwd(kr_ref[...].astype(F32), gkr_ref[...], QK_ROPE)
        krope_ref[...] = _rope(kn, cs_ref[...], sg_ref[...]).astype(BF16)

    row = lambda w: pl.BlockSpec((ts, w), lambda i: (i, 0))
    vec = lambda w: pl.BlockSpec((1, w), lambda i: (0, 0))
    return _pcall(
        body, side, "lora_fwd", (s // ts,),
        [pl.BlockSpec((ts, 512), lambda i: (i, COL_S // 512)),
         pl.BlockSpec((ts, 512), lambda i: (i, COL_S // 512 + 1)),
         pl.BlockSpec((ts, 128), lambda i: (i, (COL_S + 1024) // 128)),
         vec(512), vec(512), vec(128), row(128), row(128)],
        [row(512), row(512), row(128)],
        [jax.ShapeDtypeStruct((s, 512), BF16), jax.ShapeDtypeStruct((s, 512), BF16),
         jax.ShapeDtypeStruct((s, 128), BF16)],
        (p_all, p_all, p_all, gq, gkv, gkr, cs, sg), ("parallel",))


def _q_prep(q_raw, gn, gr, cs, sg):
    s = q_raw.shape[0]
    ts = min(TS_ROW, s)

    def body(q_ref, gn_ref, gr_ref, cs_ref, sg_ref, o_ref):
        for h in range(MLA_HEADS):
            q = q_ref[:, h * HEAD_PAD:(h + 1) * HEAD_PAD].astype(F32)
            a = _rms_fwd(q[:, :128], gn_ref[...], QK_NOPE)
            b = _rope(_rms_fwd(q[:, 128:], gr_ref[...], QK_ROPE), cs_ref[...], sg_ref[...])
            o_ref[h] = (jnp.concatenate([a, b], axis=1) * Q_SCALE).astype(BF16)

    return pl.pallas_call(
        body, name="q_prep", grid=(s // ts,),
        in_specs=[pl.BlockSpec((ts, MLA_HEADS * HEAD_PAD), lambda i: (i, 0)),
                  pl.BlockSpec((1, 128), lambda i: (0, 0)), pl.BlockSpec((1, 128), lambda i: (0, 0)),
                  pl.BlockSpec((ts, 128), lambda i: (i, 0)), pl.BlockSpec((ts, 128), lambda i: (i, 0))],
        out_specs=pl.BlockSpec((MLA_HEADS, ts, HEAD_PAD), lambda i: (0, i, 0)),
        out_shape=jax.ShapeDtypeStruct((MLA_HEADS, s, HEAD_PAD), BF16),
        compiler_params=_params(("parallel",)),
    )(q_raw, gn, gr, cs, sg)


def _k_prep(kv, krope, gk):
    s = kv.shape[0]
    ts = min(TS_ROW, s)

    def body(k_ref, kr_ref, gk_ref, o_ref, vt_ref):
        for h in range(MLA_HEADS):
            a = _rms_fwd(k_ref[:, h * HEAD_PAD:h * HEAD_PAD + 128].astype(F32), gk_ref[...], QK_NOPE)
            o_ref[h] = jnp.concatenate([a.astype(BF16), kr_ref[...]], axis=1)
            vt_ref[h] = k_ref[:, h * HEAD_PAD + 128:(h + 1) * HEAD_PAD].astype(F32).T.astype(BF16)

    return pl.pallas_call(
        body, name="k_prep", grid=(s // ts,),
        in_specs=[pl.BlockSpec((ts, MLA_HEADS * HEAD_PAD), lambda i: (i, 0)),
                  pl.BlockSpec((ts, 128), lambda i: (i, 0)),
                  pl.BlockSpec((1, 128), lambda i: (0, 0))],
        out_specs=[pl.BlockSpec((MLA_HEADS, ts, HEAD_PAD), lambda i: (0, i, 0)),
                   pl.BlockSpec((MLA_HEADS, 128, ts), lambda i: (0, 0, i))],
        out_shape=[jax.ShapeDtypeStruct((MLA_HEADS, s, HEAD_PAD), BF16),
                   jax.ShapeDtypeStruct((MLA_HEADS, 128, s), BF16)],
        compiler_params=_params(("parallel",)),
    )(kv, krope, gk)


def _chunk_mask(tq, tk, row0=0):
    r = (lax.broadcasted_iota(jnp.int32, (tq, tk), 0) + row0) // CHUNK
    c = lax.broadcasted_iota(jnp.int32, (tq, tk), 1) // CHUNK
    return c <= r


def _causal_pairs(n, by_key):
    if by_key:
        pairs = [(i, j) for j in range(n) for i in range(j, n)]
    else:
        pairs = [(i, j) for i in range(n) for j in range(i + 1)]
    return (jnp.asarray([p[0] for p in pairs], jnp.int32), jnp.asarray([p[1] for p in pairs], jnp.int32))


def _flash_fwd(qcat, kcat, vt, p_all):
    s = qcat.shape[1]
    tq = tk = min(TQ_ATT, s)
    tc = min(TR_ATT, tq)
    nq = s // tq
    zc = COL_Z // 128
    qi, kj = _causal_pairs(nq, False)

    def body(qi_ref, kj_ref, q_ref, k_ref, vt_ref, z_ref, y_ref, a_ref, lse_ref, m_sc, l_sc, acc_sc):
        t = pl.program_id(1)
        i, j = qi_ref[t], kj_ref[t]

        @pl.when(j == 0)
        def _():
            m_sc[...] = jnp.full_like(m_sc, NEG)
            l_sc[...] = jnp.zeros_like(l_sc)
            acc_sc[...] = jnp.zeros_like(acc_sc)

        def step(masked):
            nc = tq // tc
            state = [[m_sc[:, c * tc:(c + 1) * tc], l_sc[:, c * tc:(c + 1) * tc], acc_sc[:, c * tc:(c + 1) * tc]]
                     for c in range(nc)]
            units = [(u, c) for u in range(tk // tc) for c in range(nc) if (u <= c or not masked)]

            def scores(u, c):
                return _dot(k_ref[0, u * tc:(u + 1) * tc, :], q_ref[0, c * tc:(c + 1) * tc, :], "nt")

            def weighted_values(u, c, alpha, pb):
                state[c][2] = alpha * state[c][2] + _dot(vt_ref[0, :, u * tc:(u + 1) * tc], pb)

            ahead = 3
            pending = [scores(*un) for un in units[:ahead]]
            late = None
            for n, (u, c) in enumerate(units):
                st = pending.pop(0)
                if n + ahead < len(units):
                    pending.append(scores(*units[n + ahead]))
                if masked and u == c:
                    kc = lax.broadcasted_iota(jnp.int32, (tc, tc), 0) // CHUNK
                    qc = lax.broadcasted_iota(jnp.int32, (tc, tc), 1) // CHUNK
                    st = jnp.where(kc <= qc, st, NEG)
                m_run, l_run, _ = state[c]
                m_new = jnp.maximum(m_run, jnp.max(st, axis=0, keepdims=True))
                alpha = jnp.exp2(m_run - m_new)
                p = jnp.exp2(st - m_new)
                state[c][0] = m_new
                state[c][1] = alpha * l_run + jnp.sum(p, axis=0, keepdims=True)
                if late is not None:
                    weighted_values(*late)
                late = (u, c, alpha, p.astype(BF16))
            weighted_values(*late)
            m_sc[...] = jnp.concatenate([s_[0] for s_ in state], axis=1)
            l_sc[...] = jnp.concatenate([s_[1] for s_ in state], axis=1)
            acc_sc[...] = jnp.concatenate([s_[2] for s_ in state], axis=1)

        @pl.when(j < i)
        def _():
            step(False)

        @pl.when(j == i)
        def _():
            step(True)
            y = (acc_sc[...] * (1.0 / l_sc[...])).T
            z = z_ref[...].astype(F32)
            y_ref[...] = y.astype(BF16)
            a_ref[...] = (y * (z * _sigmoid(z))).astype(BF16)
            lse2 = m_sc[...] + jnp.log(l_sc[...]) * (1.0 / LN2)
            lse_ref[0] = jnp.broadcast_to(lse2, (128, tq)).T

    return pl.pallas_call(
        body, name="flash_fwd",
        grid_spec=pltpu.PrefetchScalarGridSpec(
            num_scalar_prefetch=2, grid=(MLA_HEADS, qi.shape[0]),
            in_specs=[pl.BlockSpec((1, tq, HEAD_PAD), lambda h, t, qi_ref, kj_ref: (h, qi_ref[t], 0)),
                      pl.BlockSpec((1, tk, HEAD_PAD), lambda h, t, qi_ref, kj_ref: (h, kj_ref[t], 0)),
                      pl.BlockSpec((1, 128, tk), lambda h, t, qi_ref, kj_ref: (h, 0, kj_ref[t])),
                      pl.BlockSpec((tq, 128), lambda h, t, qi_ref, kj_ref: (qi_ref[t], zc + h))],
            out_specs=[pl.BlockSpec((tq, 128), lambda h, t, qi_ref, kj_ref: (qi_ref[t], h)),
                       pl.BlockSpec((tq, 128), lambda h, t, qi_ref, kj_ref: (qi_ref[t], h)),
                       pl.BlockSpec((1, tq, 128), lambda h, t, qi_ref, kj_ref: (h, qi_ref[t], 0))],
            scratch_shapes=[pltpu.VMEM((1, tq), F32), pltpu.VMEM((1, tq), F32), pltpu.VMEM((128, tq), F32)]),
        out_shape=[jax.ShapeDtypeStruct((s, MLA_HEADS * 128), BF16),
                   jax.ShapeDtypeStruct((s, MLA_HEADS * 128), BF16),
                   jax.ShapeDtypeStruct((MLA_HEADS, s, 128), F32)],
        compiler_params=_params(("parallel", "arbitrary")),
    )(qi, kj, qcat, kcat, vt, p_all)


def _mem_kv_fwd(mem, g_norm, w_kv, g_kn):
    m = mem.shape[0]

    def body(mem_ref, g_ref, w_ref, gk_ref, memn_ref, kraw_ref, kn_ref, v_ref):
        memn = _rms_fwd(mem_ref[...], g_ref[...], D_MODEL).astype(BF16)
        memn_ref[...] = memn
        kvm = _dot(memn, w_ref[...])
        kraw_ref[...] = kvm[:, :1024]
        v_ref[...] = kvm[:, 1024:].astype(BF16)
        for hh in range(MEM_HEADS):
            sl = slice(hh * MEM_HEAD_DIM, (hh + 1) * MEM_HEAD_DIM)
            kn_ref[:, sl] = _rms_fwd(kvm[:, sl], gk_ref[...], MEM_HEAD_DIM).astype(BF16)

    return pl.pallas_call(
        body, name="mem_kv_fwd",
        out_shape=[jax.ShapeDtypeStruct((m, D_MODEL), BF16), jax.ShapeDtypeStruct((m, 1024), F32),
                   jax.ShapeDtypeStruct((m, 1024), BF16), jax.ShapeDtypeStruct((m, 1024), BF16)],
        compiler_params=_params(),
    )(mem, g_norm, w_kv, g_kn)


def _mem_attn_fwd(p_all, kn, v, gq):
    s = p_all.shape[0]
    tq = min(TS_ROW, s)

    def body(q_ref, z_ref, kn_ref, v_ref, gq_ref, a_ref):
        z = z_ref[...].astype(F32)
        gate = z * _sigmoid(z)
        for hh in range(MEM_HEADS):
            sl = slice(hh * MEM_HEAD_DIM, (hh + 1) * MEM_HEAD_DIM)
            qn = _rms_fwd(q_ref[:, sl].astype(F32), gq_ref[...], MEM_HEAD_DIM).astype(BF16)
            sc = _dot(qn, kn_ref[:, sl], "nt") * MEM_SCALE
            e = jnp.exp(sc - jnp.max(sc, axis=-1, keepdims=True))
            p = e * (1.0 / jnp.sum(e, axis=-1, keepdims=True))
            y = _dot(p.astype(BF16), v_ref[:, sl])
            a_ref[:, sl] = (y * gate[:, sl]).astype(BF16)

    full = lambda a: pl.BlockSpec(a.shape, lambda i: (0, 0))
    return pl.pallas_call(
        body, name="mem_attn_fwd", grid=(s // tq,),
        in_specs=[pl.BlockSpec((tq, 1024), lambda i: (i, COL_M // 1024)),
                  pl.BlockSpec((tq, 1024), lambda i: (i, COL_M // 1024 + 1)),
                  full(kn), full(v), full(gq)],
        out_specs=pl.BlockSpec((tq, 1024), lambda i: (i, 0)),
        out_shape=jax.ShapeDtypeStruct((s, 1024), BF16),
        compiler_params=_params(("parallel",)),
    )(p_all, p_all, kn, v, gq)


def _merge_fwd(a_conv, w_c, a_mla, w_m, a_mem, w_e, p_all):
    s = a_conv.shape[0]
    tm, tn = min(TS_ROW, s), 1024
    nj = D_MODEL // tn

    def body(ac_ref, wc_ref, am_ref, wm_ref, ae_ref, we_ref, gc_ref, gm_ref, ge_ref,
             oc_ref, om_ref, oe_ref, mg_ref):
        oc = _dot(ac_ref[...], wc_ref[...])
        om = _dot(am_ref[...], wm_ref[...])
        oe = _dot(ae_ref[...], we_ref[...])
        oc_ref[...] = oc.astype(BF16)
        om_ref[...] = om.astype(BF16)
        oe_ref[...] = oe.astype(BF16)
        mg_ref[...] = (_sigmoid(gc_ref[...].astype(F32)) * oc + _sigmoid(gm_ref[...].astype(F32)) * om
                       + _sigmoid(ge_ref[...].astype(F32)) * oe).astype(BF16)

    act = lambda k: pl.BlockSpec((tm, k), lambda i, j: (i, 0))
    wgt = lambda k: pl.BlockSpec((k, tn), lambda i, j: (0, j))
    gate = lambda b: pl.BlockSpec((tm, tn), lambda i, j: (i, b * nj + j))
    out = pl.BlockSpec((tm, tn), lambda i, j: (i, j))
    return pl.pallas_call(
        body, name="merge_fwd", grid=(s // tm, nj),
        in_specs=[act(1024), wgt(1024), act(2048), wgt(2048), act(1024), wgt(1024), gate(0), gate(1), gate(2)],
        out_specs=[out] * 4,
        out_shape=[jax.ShapeDtypeStruct((s, D_MODEL), BF16)] * 4,
        compiler_params=_params(("parallel", "parallel")),
    )(a_conv, w_c, a_mla, w_m, a_mem, w_e, p_all, p_all, p_all)


def _out_fwd(merged, w_o, x, target):
    s = merged.shape[0]
    tm, tn = min(TS_ROW, s), 1024
    nj = D_MODEL // tn

    def body(mg_ref, w_ref, x_ref, t_ref, dy_ref, dyb_ref, ls_ref):
        e = x_ref[...] + _dot(mg_ref[...], w_ref[...]) - t_ref[...]
        dy = e * (1.0 / D_MODEL)
        dy_ref[...] = dy
        dyb_ref[...] = dy.astype(BF16)
        r = _fold8(e * e)
        acc = r[:, 0:128]
        for cc in range(1, tn // 128):
            acc = acc + r[:, cc * 128:(cc + 1) * 128]
        ls_ref[...] = acc

    tile = pl.BlockSpec((tm, tn), lambda i, j: (i, j))
    return pl.pallas_call(
        body, name="out_fwd", grid=(s // tm, nj),
        in_specs=[pl.BlockSpec((tm, D_MODEL), lambda i, j: (i, 0)),
                  pl.BlockSpec((D_MODEL, tn), lambda i, j: (0, j)), tile, tile],
        out_specs=[tile, tile, pl.BlockSpec((8, 128), lambda i, j: (i, j))],
        out_shape=[jax.ShapeDtypeStruct((s, D_MODEL), F32), jax.ShapeDtypeStruct((s, D_MODEL), BF16),
                   jax.ShapeDtypeStruct((s // tm * 8, nj * 128), F32)],
        compiler_params=_params(("parallel", "parallel")),
    )(merged, w_o, x, target)


def _merge_bwd(dyb, w_o, p_all, o_c, o_m, o_e):
    s = dyb.shape[0]
    tm = min(256, s)

    def body(dy_ref, w_ref, g_ref, oc_ref, om_ref, oe_ref, dc_ref, dm_ref, de_ref, dg_ref):
        dmg = _dot(dy_ref[...], w_ref[...], "nt")
        for b, (o_ref, d_ref) in enumerate(((oc_ref, dc_ref), (om_ref, dm_ref), (oe_ref, de_ref))):
            sl = slice(b * D_MODEL, (b + 1) * D_MODEL)
            sg = _sigmoid(g_ref[:, sl].astype(F32))
            d_ref[...] = (dmg * sg).astype(BF16)
            dg_ref[:, sl] = (dmg * o_ref[...].astype(F32) * sg * (1.0 - sg)).astype(BF16)

    row = pl.BlockSpec((tm, D_MODEL), lambda i: (i, 0))
    wide = pl.BlockSpec((tm, 3 * D_MODEL), lambda i: (i, 0))
    return pl.pallas_call(
        body, name="merge_bwd", grid=(s // tm,),
        in_specs=[row, pl.BlockSpec((D_MODEL, D_MODEL), lambda i: (0, 0)), wide, row, row, row],
        out_specs=[row, row, row, wide],
        out_shape=[jax.ShapeDtypeStruct((s, D_MODEL), BF16)] * 3 + [jax.ShapeDtypeStruct((s, 3 * D_MODEL), BF16)],
        compiler_params=_params(("parallel",)),
    )(dyb, w_o, p_all, o_c, o_m, o_e)


def _conv_bwd(da, p_all, co, conv_w):
    s = da.shape[0]
    ts = min(TS_ROW, s)
    n = s // ts
    c0 = COL_C // 1024

    def body(da_ref, cg_ref, bg_ref, u_ref, z_ref, co_ref, w_ref, dp_ref, dw_ref, carry_ref):
        @pl.when(pl.program_id(0) == 0)
        def _():
            carry_ref[...] = jnp.zeros_like(carry_ref)
            dw_ref[...] = jnp.zeros_like(dw_ref)

        da_ = da_ref[...].astype(F32)
        cg, bg = cg_ref[...].astype(F32), bg_ref[...].astype(F32)
        u, z, cov = u_ref[...].astype(F32), z_ref[...].astype(F32), co_ref[...].astype(F32)
        sz = _sigmoid(z)
        dyc = da_ * (z * sz)
        dz = da_ * (bg * cov) * (sz * (1.0 + z * (1.0 - sz)))
        db = dyc * cov
        dco = dyc * bg
        carry = carry_ref[...]
        d1 = _shift_rows(dco, carry, 1, ts, False)
        d2 = _shift_rows(dco, carry, 2, ts, False)
        carry_ref[...] = dco[:8]
        dpp = w_ref[2:3, :] * dco + w_ref[1:2, :] * d1 + w_ref[0:1, :] * d2
        p = cg * u
        dw_ref[0] += _fold8(p * d2)
        dw_ref[1] += _fold8(p * d1)
        dw_ref[2] += _fold8(p * dco)
        dp_ref[...] = jnp.concatenate([dpp * u, db, dpp * cg, dz], axis=1).astype(BF16)

    rev = lambda c: pl.BlockSpec((ts, 1024), lambda i: (n - 1 - i, c))
    return pl.pallas_call(
        body, name="conv_bwd", grid=(n,),
        in_specs=[rev(0), rev(c0), rev(c0 + 1), rev(c0 + 2), rev(c0 + 3), rev(0),
                  pl.BlockSpec((3, 1024), lambda i: (0, 0))],
        out_specs=[pl.BlockSpec((ts, 4096), lambda i: (n - 1 - i, 0)),
                   pl.BlockSpec((3, 8, 1024), lambda i: (0, 0, 0))],
        out_shape=[jax.ShapeDtypeStruct((s, 4096), BF16), jax.ShapeDtypeStruct((3, 8, 1024), F32)],
        scratch_shapes=[pltpu.VMEM((8, 1024), F32)],
        compiler_params=_params(("arbitrary",)),
    )(da, p_all, p_all, p_all, p_all, co, conv_w)


def _mla_gate_bwd(da, y, p_all):
    s = da.shape[0]
    ts = min(TS_ROW, s)

    def body(da_ref, y_ref, z_ref, dy_ref, dz_ref, dl_ref):
        da_, yv, z = da_ref[...].astype(F32), y_ref[...].astype(F32), z_ref[...].astype(F32)
        sz = _sigmoid(z)
        dyv = da_ * (z * sz)
        dy_ref[...] = dyv.astype(BF16)
        dz_ref[...] = (da_ * yv * (sz * (1.0 + z * (1.0 - sz)))).astype(BF16)
        pr = dyv * yv
        for h in range(MLA_HEADS):
            dl_ref[h] = jnp.broadcast_to(jnp.sum(pr[:, h * 128:(h + 1) * 128], axis=-1, keepdims=True), (ts, 128))

    row = pl.BlockSpec((ts, D_MODEL), lambda i: (i, 0))
    return pl.pallas_call(
        body, name="mla_gate_bwd", grid=(s // ts,),
        in_specs=[row, row, pl.BlockSpec((ts, D_MODEL), lambda i: (i, COL_Z // D_MODEL))],
        out_specs=[row, row, pl.BlockSpec((MLA_HEADS, ts, 128), lambda i: (0, i, 0))],
        out_shape=[jax.ShapeDtypeStruct((s, D_MODEL), BF16)] * 2 + [jax.ShapeDtypeStruct((MLA_HEADS, s, 128), F32)],
        compiler_params=_params(("parallel",)),
    )(da, y, p_all)


def _flash_bwd(qcat, kcat, kv, dy, lse, delta):
    s = qcat.shape[1]
    tq = tk = min(TQ_ATT, s)
    nq = s // tq
    qi, kj = _causal_pairs(nq, True)

    def body(qi_ref, kj_ref, q_ref, k_ref, v_ref, do_ref, lse_ref, dl_ref, dq_ref, dk_ref, dv_ref):
        t = pl.program_id(1)
        i, j = qi_ref[t], kj_ref[t]

        def step(masked):
            q, k = q_ref[0], k_ref[0]
            sc = _dot(q, k, "nt")
            if masked:
                sc = jnp.where(_chunk_mask(tq, tk), sc, NEG)
            p = jnp.exp2(sc - lse_ref[0][:, 0:1])
            do = do_ref[...]
            dpv = _dot(do, v_ref[...], "nt")
            ds = (p * (dpv - dl_ref[0][:, 0:1])).astype(BF16)
            pb = p.astype(BF16)
            dv_new = _dot(pb, do, "tn")
            dk_new = _dot(ds, q, "tn")
            dq_new = _dot(ds, k)
            rows = pl.ds(pl.multiple_of(i * tq, tq), tq)
            if masked:
                dv_ref[0] = dv_new
                dk_ref[0] = dk_new
            else:
                dv_ref[0] += dv_new
                dk_ref[0] += dk_new

            @pl.when(j == 0)
            def _():
                dq_ref[0, rows, :] = dq_new

            @pl.when(j > 0)
            def _():
                dq_ref[0, rows, :] += dq_new

        @pl.when(i == j)
        def _():
            step(True)

        @pl.when(i > j)
        def _():
            step(False)

    qrow = lambda w: pl.BlockSpec((1, tq, w), lambda h, t, qi_ref, kj_ref: (h, qi_ref[t], 0))
    krow = lambda w: pl.BlockSpec((1, tk, w), lambda h, t, qi_ref, kj_ref: (h, kj_ref[t], 0))
    return pl.pallas_call(
        body, name="flash_bwd",
        grid_spec=pltpu.PrefetchScalarGridSpec(
            num_scalar_prefetch=2, grid=(MLA_HEADS, qi.shape[0]),
            in_specs=[qrow(HEAD_PAD), krow(HEAD_PAD),
                      pl.BlockSpec((tk, 128), lambda h, t, qi_ref, kj_ref: (kj_ref[t], 2 * h + 1)),
                      pl.BlockSpec((tq, 128), lambda h, t, qi_ref, kj_ref: (qi_ref[t], h)),
                      qrow(128), qrow(128)],
            out_specs=[pl.BlockSpec((1, s, HEAD_PAD), lambda h, t, qi_ref, kj_ref: (h, 0, 0)),
                       krow(HEAD_PAD), krow(128)]),
        out_shape=[jax.ShapeDtypeStruct((MLA_HEADS, s, HEAD_PAD), F32),
                   jax.ShapeDtypeStruct((MLA_HEADS, s, HEAD_PAD), F32),
                   jax.ShapeDtypeStruct((MLA_HEADS, s, 128), F32)],
        compiler_params=_params(("parallel", "arbitrary")),
    )(qi, kj, qcat, kcat, kv, dy, lse, delta)


def _q_prep_bwd(dqcat, q_raw, gn, gr, cs, sg, side=None):
    s = q_raw.shape[0]
    ts = min(TS_BWD, s)

    def body(dq_ref, q_ref, gn_ref, gr_ref, cs_ref, sg_ref, o_ref, dg_ref):
        @pl.when(pl.program_id(0) == 0)
        def _():
            dg_ref[...] = jnp.zeros_like(dg_ref)

        ga_acc = jnp.zeros((8, 128), F32)
        gb_acc = jnp.zeros((8, 128), F32)
        for h in range(MLA_HEADS):
            cols = slice(h * HEAD_PAD, (h + 1) * HEAD_PAD)
            q = q_ref[:, cols].astype(F32)
            d = dq_ref[h] * MLA_SCALE
            da, ga = _rms_bwd(q[:, :128], gn_ref[...], d[:, :128], QK_NOPE)
            db, gb = _rms_bwd(q[:, 128:], gr_ref[...], _rope_bwd(d[:, 128:], cs_ref[...], sg_ref[...]), QK_ROPE)
            o_ref[:, cols] = jnp.concatenate([da, db], axis=1).astype(BF16)
            ga_acc = ga_acc + _fold8(ga)
            gb_acc = gb_acc + _fold8(gb)
        dg_ref[0] += ga_acc
        dg_ref[1] += gb_acc

    wide = pl.BlockSpec((ts, MLA_HEADS * HEAD_PAD), lambda i: (i, 0))
    return _pcall(
        body, side, "q_prep_bwd", (s // ts,),
        [pl.BlockSpec((MLA_HEADS, ts, HEAD_PAD), lambda i: (0, i, 0)), wide,
         pl.BlockSpec((1, 128), lambda i: (0, 0)), pl.BlockSpec((1, 128), lambda i: (0, 0)),
         pl.BlockSpec((ts, 128), lambda i: (i, 0)), pl.BlockSpec((ts, 128), lambda i: (i, 0))],
        [wide, pl.BlockSpec((2, 8, 128), lambda i: (0, 0, 0))],
        [jax.ShapeDtypeStruct((s, MLA_HEADS * HEAD_PAD), BF16), jax.ShapeDtypeStruct((2, 8, 128), F32)],
        (dqcat, q_raw, gn, gr, cs, sg), ("arbitrary",))


def _k_prep_bwd(dkcat, dv, kv, gk, side=None):
    s = kv.shape[0]
    ts = min(TS_BWD, s)

    def body(dk_ref, dv_ref, k_ref, gk_ref, o_ref, dkr_ref, dg_ref):
        @pl.when(pl.program_id(0) == 0)
        def _():
            dg_ref[...] = jnp.zeros_like(dg_ref)

        g_acc = jnp.zeros((8, 128), F32)
        dkr = jnp.zeros((ts, 128), F32)
        for h in range(MLA_HEADS):
            d = dk_ref[h] * LN2
            k = k_ref[:, h * HEAD_PAD:h * HEAD_PAD + 128].astype(F32)
            dk_raw, gg = _rms_bwd(k, gk_ref[...], d[:, :128], QK_NOPE)
            o_ref[:, h * HEAD_PAD:(h + 1) * HEAD_PAD] = jnp.concatenate([dk_raw, dv_ref[h]], axis=1).astype(BF16)
            g_acc = g_acc + _fold8(gg)
            dkr = dkr + d[:, 128:]
        dkr_ref[...] = dkr
        dg_ref[...] += g_acc

    wide = pl.BlockSpec((ts, MLA_HEADS * HEAD_PAD), lambda i: (i, 0))
    return _pcall(
        body, side, "k_prep_bwd", (s // ts,),
        [pl.BlockSpec((MLA_HEADS, ts, HEAD_PAD), lambda i: (0, i, 0)),
         pl.BlockSpec((MLA_HEADS, ts, 128), lambda i: (0, i, 0)), wide,
         pl.BlockSpec((1, 128), lambda i: (0, 0))],
        [wide, pl.BlockSpec((ts, 128), lambda i: (i, 0)), pl.BlockSpec((8, 128), lambda i: (0, 0))],
        [jax.ShapeDtypeStruct((s, MLA_HEADS * HEAD_PAD), BF16), jax.ShapeDtypeStruct((s, 128), F32),
         jax.ShapeDtypeStruct((8, 128), F32)],
        (dkcat, dv, kv, gk), ("arbitrary",))


def _lora_bwd(dcqn, dckvn, dkr, p_all, gq, gkv, gkr, cs, sg):
    s = p_all.shape[0]
    ts = min(TS_ROW, s)

    def body(dq_ref, dkv_ref, dkr_ref, cq_ref, ckv_ref, kr_ref, gq_ref, gkv_ref, gkr_ref, cs_ref, sg_ref,
             o_ref, dgq_ref, dgkv_ref, dgkr_ref):
        @pl.when(pl.program_id(0) == 0)
        def _():
            dgq_ref[...] = jnp.zeros_like(dgq_ref)
            dgkv_ref[...] = jnp.zeros_like(dgkv_ref)
            dgkr_ref[...] = jnp.zeros_like(dgkr_ref)

        dq, g1 = _rms_bwd(cq_ref[...].astype(F32), gq_ref[...], dq_ref[...], 512)
        dkv, g2 = _rms_bwd(ckv_ref[...].astype(F32), gkv_ref[...], dkv_ref[...], 512)
        dkn = _rope_bwd(dkr_ref[...], cs_ref[...], sg_ref[...])
        dk, g3 = _rms_bwd(kr_ref[...].astype(F32), gkr_ref[...], dkn, QK_ROPE)
        o_ref[...] = jnp.concatenate([dq, dkv, dk, jnp.zeros((ts, 2048 - 1152), F32)], axis=1).astype(BF16)
        dgq_ref[...] += _fold8(g1)
        dgkv_ref[...] += _fold8(g2)
        dgkr_ref[...] += _fold8(g3)

    row = lambda w: pl.BlockSpec((ts, w), lambda i: (i, 0))
    vec = lambda w: pl.BlockSpec((1, w), lambda i: (0, 0))
    acc = lambda w: pl.BlockSpec((8, w), lambda i: (0, 0))
    return pl.pallas_call(
        body, name="lora_bwd", grid=(s // ts,),
        in_specs=[row(512), row(512), row(128),
                  pl.BlockSpec((ts, 512), lambda i: (i, COL_S // 512)),
                  pl.BlockSpec((ts, 512), lambda i: (i, COL_S // 512 + 1)),
                  pl.BlockSpec((ts, 128), lambda i: (i, (COL_S + 1024) // 128)),
                  vec(512), vec(512), vec(128), row(128), row(128)],
        out_specs=[row(2048), acc(512), acc(512), acc(128)],
        out_shape=[jax.ShapeDtypeStruct((s, 2048), BF16), jax.ShapeDtypeStruct((8, 512), F32),
                   jax.ShapeDtypeStruct((8, 512), F32), jax.ShapeDtypeStruct((8, 128), F32)],
        compiler_params=_params(("arbitrary",)),
    )(dcqn, dckvn, dkr, p_all, p_all, p_all, gq, gkv, gkr, cs, sg)


def _mem_attn_bwd(da, p_all, kn, v, gq):
    s = p_all.shape[0]
    tq = min(TS_ROW, s)

    def body(da_ref, q_ref, z_ref, kn_ref, v_ref, gq_ref, o_ref, dkn_ref, dv_ref, dg_ref):
        @pl.when(pl.program_id(0) == 0)
        def _():
            dkn_ref[...] = jnp.zeros_like(dkn_ref)
            dv_ref[...] = jnp.zeros_like(dv_ref)
            dg_ref[...] = jnp.zeros_like(dg_ref)

        z = z_ref[...].astype(F32)
        da_ = da_ref[...].astype(F32)
        sz = _sigmoid(z)
        gate = z * sz
        dgate = sz * (1.0 + z * (1.0 - sz))
        for hh in range(MEM_HEADS):
            sl = slice(hh * MEM_HEAD_DIM, (hh + 1) * MEM_HEAD_DIM)
            qf = q_ref[:, sl].astype(F32)
            qn = _rms_fwd(qf, gq_ref[...], MEM_HEAD_DIM).astype(BF16)
            knh, vh = kn_ref[:, sl], v_ref[:, sl]
            sc = _dot(qn, knh, "nt") * MEM_SCALE
            e = jnp.exp(sc - jnp.max(sc, axis=-1, keepdims=True))
            p = e * (1.0 / jnp.sum(e, axis=-1, keepdims=True))
            pb = p.astype(BF16)
            y = _dot(pb, vh)
            dyh = da_[:, sl] * gate[:, sl]
            o_ref[:, 1024 + hh * MEM_HEAD_DIM:1024 + (hh + 1) * MEM_HEAD_DIM] = (
                da_[:, sl] * y * dgate[:, sl]).astype(BF16)
            dyb = dyh.astype(BF16)
            dpm = _dot(dyb, vh, "nt")
            ds = (p * (dpm - jnp.sum(dpm * p, axis=-1, keepdims=True)) * MEM_SCALE).astype(BF16)
            dqn = _dot(ds, knh)
            dkn_ref[:, sl] += _dot(ds, qn, "tn")
            dv_ref[:, sl] += _dot(pb, dyb, "tn")
            dq, gg = _rms_bwd(qf, gq_ref[...], dqn, MEM_HEAD_DIM)
            o_ref[:, sl] = dq.astype(BF16)
            dg_ref[...] += _fold8(gg)

    full = lambda a: pl.BlockSpec(a.shape, lambda i: (0, 0))
    return pl.pallas_call(
        body, name="mem_attn_bwd", grid=(s // tq,),
        in_specs=[pl.BlockSpec((tq, 1024), lambda i: (i, 0)),
                  pl.BlockSpec((tq, 1024), lambda i: (i, COL_M // 1024)),
                  pl.BlockSpec((tq, 1024), lambda i: (i, COL_M // 1024 + 1)),
                  full(kn), full(v), full(gq)],
        out_specs=[pl.BlockSpec((tq, 2048), lambda i: (i, 0)),
                   pl.BlockSpec((MEM_TOKENS, 1024), lambda i: (0, 0)),
                   pl.BlockSpec((MEM_TOKENS, 1024), lambda i: (0, 0)),
                   pl.BlockSpec((8, MEM_HEAD_DIM), lambda i: (0, 0))],
        out_shape=[jax.ShapeDtypeStruct((s, 2048), BF16), jax.ShapeDtypeStruct((MEM_TOKENS, 1024), F32),
                   jax.ShapeDtypeStruct((MEM_TOKENS, 1024), F32), jax.ShapeDtypeStruct((8, MEM_HEAD_DIM), F32)],
        compiler_params=_params(("arbitrary",)),
    )(da, p_all, p_all, kn, v, gq)


def _mem_kv_bwd(dkn, dv, kraw, memn, mem, g_norm, w_kv, g_kn):
    m = mem.shape[0]

    def body(dkn_ref, dv_ref, kraw_ref, memn_ref, mem_ref, g_ref, w_ref, gk_ref, dw_ref, dgn_ref, dgk_ref, dkv_sc):
        gk_acc = jnp.zeros((8, MEM_HEAD_DIM), F32)
        for hh in range(MEM_HEADS):
            sl = slice(hh * MEM_HEAD_DIM, (hh + 1) * MEM_HEAD_DIM)
            dk, gg = _rms_bwd(kraw_ref[:, sl], gk_ref[...], dkn_ref[:, sl], MEM_HEAD_DIM)
            dkv_sc[:, sl] = dk.astype(BF16)
            gk_acc = gk_acc + _fold8(gg)
        dgk_ref[...] = gk_acc
        dkv_sc[:, 1024:] = dv_ref[...].astype(BF16)
        dkv = dkv_sc[...]
        dw_ref[...] = _dot(memn_ref[...], dkv, "tn")
        dmemn = _dot(dkv, w_ref[...], "nt")
        xf = mem_ref[...]
        r = lax.rsqrt(jnp.mean(xf * xf, axis=-1, keepdims=True) + EPS)
        dgn_ref[...] = _fold8(dmemn * (xf * r))

    return pl.pallas_call(
        body, name="mem_kv_bwd",
        out_shape=[jax.ShapeDtypeStruct((D_MODEL, D_MODEL), F32), jax.ShapeDtypeStruct((8, D_MODEL), F32),
                   jax.ShapeDtypeStruct((8, MEM_HEAD_DIM), F32)],
        scratch_shapes=[pltpu.VMEM((m, D_MODEL), BF16)],
        compiler_params=_params(),
    )(dkn, dv, kraw, memn, mem, g_norm, w_kv, g_kn)


def _dh_bwd(dps, w_all_t, row0, nrows, name, side=None):
    tm, tk = min(TS_ROW, nrows), 1024
    blk0 = row0 // tm
    widths = [a.shape[1] // tk for a in dps]
    starts = [int(v) for v in np.cumsum([0] + widths[:-1])]
    nk = sum(widths)

    def body(*refs):
        d_refs, w_ref, o_ref, acc_ref = refs[:5], refs[5], refs[6], refs[7]
        kk = pl.program_id(1)

        @pl.when(kk == 0)
        def _():
            acc_ref[...] = jnp.zeros_like(acc_ref)

        for d_ref, st, wd in zip(d_refs, starts, widths):
            @pl.when((kk >= st) & (kk < st + wd))
            def _(d_ref=d_ref):
                acc_ref[...] += _dot(d_ref[...], w_ref[...])

        @pl.when(kk == nk - 1)
        def _():
            o_ref[...] = acc_ref[...]

    def dspec(st, wd):
        return pl.BlockSpec((tm, tk), lambda i, kk: (blk0 + i, jnp.clip(kk - st, 0, wd - 1)))

    outs, extra = _pcall(
        body, side, name, (nrows // tm, nk),
        [dspec(st, wd) for st, wd in zip(starts, widths)] + [pl.BlockSpec((tk, D_MODEL), lambda i, kk: (kk, 0))],
        [pl.BlockSpec((tm, D_MODEL), lambda i, kk: (i, 0))], [jax.ShapeDtypeStruct((nrows, D_MODEL), F32)],
        (*dps, w_all_t), ("parallel", "arbitrary"), scratch_shapes=[pltpu.VMEM((tm, D_MODEL), F32)])
    return outs[0], extra


def _dx_bwd(dh_a, dh_b, x, dy, g, side=None):
    s = x.shape[0]
    ts = min(TS_BWD, dh_a.shape[0])
    na, nb = dh_a.shape[0] // ts, dh_b.shape[0] // ts

    def body(dha_ref, dhb_ref, x_ref, dy_ref, g_ref, o_ref, dg_ref):
        @pl.when(pl.program_id(0) == 0)
        def _():
            dg_ref[...] = jnp.zeros_like(dg_ref)

        dh = jnp.where(pl.program_id(0) < na, dha_ref[...], dhb_ref[...])
        dx, gg = _rms_bwd(x_ref[...], g_ref[...], dh, D_MODEL)
        o_ref[...] = dy_ref[...] + dx
        dg_ref[...] += _fold8(gg)

    row = pl.BlockSpec((ts, D_MODEL), lambda i: (i, 0))
    return _pcall(
        body, side, "dx_bwd", (s // ts,),
        [pl.BlockSpec((ts, D_MODEL), lambda i: (jnp.minimum(i, na - 1), 0)),
         pl.BlockSpec((ts, D_MODEL), lambda i: (jnp.clip(i - na, 0, nb - 1), 0)),
         row, row, pl.BlockSpec((1, D_MODEL), lambda i: (0, 0))],
        [row, pl.BlockSpec((8, D_MODEL), lambda i: (0, 0))],
        [jax.ShapeDtypeStruct((s, D_MODEL), F32), jax.ShapeDtypeStruct((8, D_MODEL), F32)],
        (dh_a, dh_b, x, dy, g), ("arbitrary",))


def _pad128(v, n):
    return jnp.pad(v.reshape(1, n), ((0, 0), (0, 128 - n)))


def _local_step(x, positions, mem, target, comm, gains):
    half = QK_ROPE // 2
    inv_freq = jnp.power(ROPE_THETA, -jnp.arange(half, dtype=F32) / half)
    ang = positions.astype(F32)[:, None] * inv_freq
    cos, sin = jnp.cos(ang), jnp.sin(ang)
    zpad = jnp.zeros((x.shape[0], 64), F32)
    cs = jnp.concatenate([cos, cos, zpad], axis=1)
    sg = jnp.concatenate([-sin, sin, zpad], axis=1)
    g_qr, g_kr = _pad128(gains["mla_qn_rope_g"], 64), _pad128(gains["mla_kn_rope_g"], 64)
    g_qn, g_kn = gains["mla_qn_nope_g"], gains["mla_kn_nope_g"]

    def hosted_matmul(tag, a, b, mode, out_dtype):
        side = comm.side(tag)
        if side is None:
            return _matmul(a, b, mode, out_dtype, tag)
        out, extra = _matmul(a, b, mode, out_dtype, tag, side=side)
        comm.done(tag, extra)
        return out

    (p_all, h), extra = _proj_fwd(x, gains["norm_g"], comm.weight("w_all_t"), comm.side("proj_fwd"))
    comm.done("proj_fwd", extra)
    a_conv, co = _conv_fwd(p_all, comm.weight("conv_w"))
    (cqn, ckvn, krope), extra = _lora_fwd(p_all, gains["mla_q_norm_g"], gains["mla_kv_norm_g"], g_kr, cs, sg,
                                          comm.side("lora_fwd"))
    comm.done("lora_fwd", extra)
    q_raw = _matmul(cqn, comm.weight("w_uq"), "nn", BF16, "q_up")
    kv = _matmul(ckvn, comm.weight("w_ukv"), "nn", BF16, "kv_up")
    qcat = _q_prep(q_raw, g_qn, g_qr, cs, sg)
    kcat, vt = _k_prep(kv, krope, g_kn)
    mla_y, a_mla, lse = _flash_fwd(qcat, kcat, vt, p_all)
    memn, kraw, kn, vmem = _mem_kv_fwd(mem, gains["mem_norm_g"], comm.weight("w_mem_kv"), gains["mem_kn_g"])
    a_mem = _mem_attn_fwd(p_all, kn, vmem, gains["mem_qn_g"])
    o_c, o_m, o_e, merged = _merge_fwd(a_conv, comm.weight("w_conv_out"), a_mla, comm.weight("w_mla_out"), a_mem,
                                       comm.weight("w_mem_out"), p_all)
    dy, dyb, loss_parts = _out_fwd(merged, comm.weight("w_o"), x, target)

    comm.put("w_o", _matmul(merged, dyb, "tn", BF16, "dw_o"))
    do_c, do_m, do_e, dp_g = _merge_bwd(dyb, comm.weight("w_o"), p_all, o_c, o_m, o_e)
    comm.put("w_conv_out", _matmul(a_conv, do_c, "tn", BF16, "dw_conv_out"))
    comm.put("w_mla_out", _matmul(a_mla, do_m, "tn", BF16, "dw_mla_out"))
    comm.put("w_mem_out", _matmul(a_mem, do_e, "tn", BF16, "dw_mem_out"))
    da_conv = hosted_matmul("da_conv", do_c, comm.weight("w_conv_out"), "nt", BF16)
    da_mla = _matmul(do_m, comm.weight("w_mla_out"), "nt", BF16, "da_mla")
    da_mem = _matmul(do_e, comm.weight("w_mem_out"), "nt", BF16, "da_mem")
    dp_c, dconv_w = _conv_bwd(da_conv, p_all, co, comm.weight("conv_w"))
    dmla_y, dp_z, delta = _mla_gate_bwd(da_mla, mla_y, p_all)
    dqcat, dkcat, dv = _flash_bwd(qcat, kcat, kv, dmla_y, lse, delta)
    (dq_raw, dg_q), extra = _q_prep_bwd(dqcat, q_raw, g_qn, g_qr, cs, sg, comm.side("q_prep_bwd"))
    comm.done("q_prep_bwd", extra)
    (dkv, dkr, dg_kn), extra = _k_prep_bwd(dkcat, dv, kv, g_kn, comm.side("k_prep_bwd"))
    comm.done("k_prep_bwd", extra)
    comm.put("w_uq", _matmul(cqn, dq_raw, "tn", BF16, "dw_uq"))
    comm.put("w_ukv", _matmul(ckvn, dkv, "tn", BF16, "dw_ukv"))
    dcqn = _matmul(dq_raw, comm.weight("w_uq"), "nt", F32, "dcqn")
    dckvn = _matmul(dkv, comm.weight("w_ukv"), "nt", F32, "dckvn")
    dp_s, dg_qn, dg_kvn, dg_kr = _lora_bwd(dcqn, dckvn, dkr, p_all, gains["mla_q_norm_g"],
                                            gains["mla_kv_norm_g"], g_kr, cs, sg)
    dp_m, dkn, dvm, dg_mq = _mem_attn_bwd(da_mem, p_all, kn, vmem, gains["mem_qn_g"])
    dw_mem_kv, dg_mn, dg_mk = _mem_kv_bwd(dkn, dvm, kraw, memn, mem, gains["mem_norm_g"],
                                          comm.weight("w_mem_kv"), gains["mem_kn_g"])
    comm.put("w_mem_kv", dw_mem_kv.astype(BF16))
    dps = [dp_g, dp_z, dp_c, dp_m, dp_s]
    comm.put("w_all_t", [hosted_matmul("dw_in_%d" % k, d, h, "tn", BF16) for k, d in enumerate(dps)])
    s = x.shape[0]
    dh_a, extra = _dh_bwd(dps, comm.weight("w_all_t"), 0, s // 4, "dh_a", comm.side("dh_a"))
    comm.done("dh_a", extra)
    dh_b, extra = _dh_bwd(dps, comm.weight("w_all_t"), s // 4, s - s // 4, "dh_b", comm.side("dh_b"))
    comm.done("dh_b", extra)
    (grad_x, dg_n), extra = _dx_bwd(dh_a, dh_b, x, dy, gains["norm_g"], comm.side("dx_bwd"))
    comm.done("dx_bwd", extra)

    gsmall = {
        "norm_g": dg_n.sum(0), "conv_w": dconv_w.sum(1), "mla_q_norm_g": dg_qn.sum(0),
        "mla_kv_norm_g": dg_kvn.sum(0), "mla_qn_nope_g": dg_q[0].sum(0), "mla_qn_rope_g": dg_q[1].sum(0)[:64],
        "mla_kn_nope_g": dg_kn.sum(0), "mla_kn_rope_g": dg_kr.sum(0)[:64], "mem_norm_g": dg_mn.sum(0),
        "mem_qn_g": dg_mq.sum(0), "mem_kn_g": dg_mk.sum(0),
    }
    return loss_parts, grad_x, gsmall


def _me():
    return lax.axis_index("x"), lax.axis_index("y"), lax.axis_index("c")


def _chips(x, y):
    return [(1 - x, y), (x, 1 - y), (1 - x, 1 - y)]


def _allgather_shards(xs, name):
    r, c = xs.shape
    hc = c // 2

    def body(x_ref, out_ref, send_sems, recv_sems):
        x, y, cc = _me()
        sibling = (x, y, 1 - cc)
        chips = _chips(x, y)

        def rows(px, py, pc):
            return out_ref.at[2 * px + py, :, pl.ds(pc * hc, hc)]

        def copy(k, block, to, src=None):
            return pltpu.make_async_remote_copy(
                src_ref=rows(*block) if src is None else src, dst_ref=rows(*block),
                send_sem=send_sems.at[k], recv_sem=recv_sems.at[k], device_id=to, device_id_type=MESH)

        my_half = x_ref.at[:, pl.ds(cc * hc, hc)]
        first = [copy(j, (x, y, cc), (*chip, cc), src=my_half) for j, chip in enumerate(chips)]
        for cp in first:
            cp.start()
        passed = [copy(3 + j, (*chip, cc), sibling) for j, chip in enumerate(chips)]
        for j, chip in enumerate(chips):
            copy(j, (*chip, cc), (x, y, cc)).wait_recv()
            passed[j].start()
        for j, chip in enumerate(chips):
            copy(3 + j, (*chip, 1 - cc), (x, y, cc)).wait_recv()
        for cp in first + passed:
            cp.wait_send()

    return pl.pallas_call(
        body, name=name, in_specs=[ANY], out_specs=ANY,
        out_shape=jax.ShapeDtypeStruct((4, r, c), xs.dtype),
        scratch_shapes=[pltpu.SemaphoreType.DMA((6,)), pltpu.SemaphoreType.DMA((6,))],
    )(xs)


def _with_own(g4, own):
    xi, yi, _ = _me()
    pick = lax.broadcasted_iota(jnp.int32, (4,) + (1,) * own.ndim, 0) == 2 * xi + yi
    return jnp.where(pick, own[None], g4)


def _remote(src, dst, send_sems, recv_sems, k, to):
    return pltpu.make_async_remote_copy(src_ref=src, dst_ref=dst, send_sem=send_sems.at[k], recv_sem=recv_sems.at[k],
                                        device_id=to, device_id_type=MESH)


def _side_gather_ici(shards):
    def build(ins, outs, send_sems, recv_sems):
        x, y, cc = _me()
        pairs = []
        for a, (x_ref, o_ref) in enumerate(zip(ins, outs)):
            hr = x_ref.shape[0] // 2
            my_half = x_ref.at[pl.ds(cc * hr, hr), :]
            for j, (px, py) in enumerate(_chips(x, y)):
                mine = o_ref.at[pl.ds((4 * x + 2 * y + cc) * hr, hr), :]
                theirs = o_ref.at[pl.ds((4 * px + 2 * py + cc) * hr, hr), :]
                pairs.append((_remote(my_half, mine, send_sems, recv_sems, 3 * a + j, (px, py, cc)),
                              _remote(my_half, theirs, send_sems, recv_sems, 3 * a + j, (px, py, cc))))
        return pairs

    shapes = [jax.ShapeDtypeStruct((4 * s.shape[0], s.shape[1]), s.dtype) for s in shards]
    return _Side(shards, shapes, 3 * len(shards), build)


def _side_gather_pass(bufs):
    def build(ins, outs, send_sems, recv_sems):
        x, y, cc = _me()
        pairs = []
        for a, o_ref in enumerate(outs):
            hr = o_ref.shape[0] // 8
            for j, (px, py) in enumerate(_chips(x, y)):
                got = o_ref.at[pl.ds((4 * px + 2 * py + cc) * hr, hr), :]
                coming = o_ref.at[pl.ds((4 * px + 2 * py + 1 - cc) * hr, hr), :]
                pairs.append((_remote(got, got, send_sems, recv_sems, 3 * a + j, (x, y, 1 - cc)),
                              _remote(got, coming, send_sems, recv_sems, 3 * a + j, (x, y, 1 - cc))))
        return pairs

    shapes = [jax.ShapeDtypeStruct(b.shape, b.dtype) for b in bufs]
    return _Side(bufs, shapes, 3 * len(bufs), build, aliases={a: a for a in range(len(bufs))})


def _side_swap(srcs, by_half):
    def build(ins, outs, send_sems, recv_sems):
        x, y, cc = _me()
        pairs = []
        for a, (s_ref, o_ref) in enumerate(zip(ins, outs)):
            cp = _remote(s_ref.at[1 - cc] if by_half else s_ref, o_ref, send_sems, recv_sems, a, (x, y, 1 - cc))
            pairs.append((cp, cp))
        return pairs

    shapes = [jax.ShapeDtypeStruct(s.shape[1:] if by_half else s.shape, s.dtype) for s in srcs]
    return _Side(srcs, shapes, len(srcs), build)


def _side_chips(ts):
    def build(ins, outs, send_sems, recv_sems):
        x, y, cc = _me()
        pairs = []
        for a, (t_ref, r_ref) in enumerate(zip(ins, outs)):
            for j, (px, py) in enumerate(_chips(x, y)):
                cp = _remote(t_ref.at[2 * px + py], r_ref.at[j], send_sems, recv_sems, 3 * a + j, (px, py, cc))
                pairs.append((cp, cp))
        return pairs

    shapes = [jax.ShapeDtypeStruct((3,) + t.shape[1:], t.dtype) for t in ts]
    return _Side(ts, shapes, 3 * len(ts), build)


ELEMENTWISE_BLOCK_BYTES = 2 << 20


def _tile(r, c, itemsize):
    tr, tc = r, c
    while tr * tc * itemsize > ELEMENTWISE_BLOCK_BYTES and tr % 32 == 0:
        tr //= 2
    while tr * tc * itemsize > ELEMENTWISE_BLOCK_BYTES and tc % 256 == 0:
        tc //= 2
    return tr, tc


def _add_own_half(gb, got, sel, name):
    _, _, hr, c = gb.shape
    tr, tc = _tile(hr, c, 4)

    def body(sel_ref, a_ref, b_ref, o_ref):
        o_ref[...] = (a_ref[0].astype(F32) + b_ref[...].astype(F32)).astype(BF16)

    return pl.pallas_call(
        body, name=name,
        grid_spec=pltpu.PrefetchScalarGridSpec(
            num_scalar_prefetch=1, grid=(4, hr // tr, c // tc),
            in_specs=[pl.BlockSpec((1, 1, tr, tc), lambda k, i, j, sel_ref: (sel_ref[0], k, i, j)),
                      pl.BlockSpec((1, tr, tc), lambda k, i, j, sel_ref: (k, i, j))],
            out_specs=pl.BlockSpec((1, tr, tc), lambda k, i, j, sel_ref: (k, i, j))),
        out_shape=jax.ShapeDtypeStruct(got.shape, BF16),
        compiler_params=_params(("parallel", "parallel", "parallel")),
    )(sel, gb, got)


def _sum_partials(t, rcv, sel, name):
    _, hr, c = t.shape
    tr, tc = _tile(hr, c, 4)

    def body(sel_ref, t_ref, r_ref, o_ref):
        o_ref[...] = ((t_ref[0].astype(F32) + r_ref[0].astype(F32)) + r_ref[1].astype(F32)) + r_ref[2].astype(F32)

    return pl.pallas_call(
        body, name=name,
        grid_spec=pltpu.PrefetchScalarGridSpec(
            num_scalar_prefetch=1, grid=(hr // tr, c // tc),
            in_specs=[pl.BlockSpec((1, tr, tc), lambda i, j, sel_ref: (sel_ref[0], i, j)),
                      pl.BlockSpec((3, tr, tc), lambda i, j, sel_ref: (0, i, j))],
            out_specs=pl.BlockSpec((tr, tc), lambda i, j, sel_ref: (i, j))),
        out_shape=jax.ShapeDtypeStruct((hr, c), F32),
        compiler_params=_params(("parallel", "parallel")),
    )(sel, t, rcv)


class _GroupReduce:
    def __init__(self):
        self.names, self.gb, self.t, self.fh, self.other = [], [], [], [], []

    def put(self, name, gb):
        self.names.append(name)
        self.gb.append(gb)

    def side(self, stage):
        return (_side_swap(self.gb, True), _side_chips(self.t), _side_swap(self.fh, False))[stage - 1]

    def done(self, stage, outs):
        x, y, cc = _me()
        if stage == 1:
            sel = jnp.reshape(cc, (1,)).astype(jnp.int32)
            self.t = [_add_own_half(gb, got, sel, "rs_add_" + n) for n, gb, got in zip(self.names, self.gb, outs)]
        elif stage == 2:
            sel = jnp.reshape(2 * x + y, (1,)).astype(jnp.int32)
            self.fh = [_sum_partials(t, rcv, sel, "rs_sum_" + n) for n, t, rcv in zip(self.names, self.t, outs)]
        else:
            self.other = list(outs)

    def result(self):
        return {n: (fh, other) for n, fh, other in zip(self.names, self.fh, self.other)}


def _allreduce_small(v, tag):
    r = v.shape[0]

    def body(v_ref, out_ref, buf_ref, send_sems, recv_sems):
        x, y, cc = _me()
        me = 4 * x + 2 * y + cc
        buf_ref[pl.ds(pl.multiple_of(me * r, 8), r), :] = v_ref[...]
        peers = [(x, y, 1 - cc)] + [(px, py, pc) for (px, py) in _chips(x, y) for pc in (cc, 1 - cc)]
        cps = []
        for k, (px, py, pc) in enumerate(peers):
            mine = buf_ref.at[pl.ds(pl.multiple_of(me * r, 8), r), :]
            cps.append(pltpu.make_async_remote_copy(
                src_ref=v_ref, dst_ref=mine, send_sem=send_sems.at[k], recv_sem=recv_sems.at[k],
                device_id=(px, py, pc), device_id_type=MESH))
        for cp in cps:
            cp.start()
        for cp in cps:
            cp.wait()
        acc = buf_ref[0:r, :]
        for d in range(1, 8):
            acc = acc + buf_ref[d * r:(d + 1) * r, :]
        out_ref[...] = acc

    vm = pl.BlockSpec(memory_space=pltpu.VMEM)
    return pl.pallas_call(
        body, name="allreduce_small_" + tag, in_specs=[vm], out_specs=vm,
        out_shape=jax.ShapeDtypeStruct((r, 128), F32),
        scratch_shapes=[pltpu.VMEM((8 * r, 128), F32), pltpu.SemaphoreType.DMA((7,)), pltpu.SemaphoreType.DMA((7,))],
    )(v)


def _adamw(w, g, m, v, name):
    r, c = w.shape
    tr = r
    c1 = 1.0 / (1.0 - ADAM_B1 ** ADAM_STEP)
    c2 = 1.0 / (1.0 - ADAM_B2 ** ADAM_STEP)

    def body(w_ref, g_ref, m_ref, v_ref, d_ref, mo_ref, vo_ref):
        gg = g_ref[...]
        mn = ADAM_B1 * m_ref[...] + (1.0 - ADAM_B1) * gg
        vn = ADAM_B2 * v_ref[...] + (1.0 - ADAM_B2) * (gg * gg)
        mo_ref[...] = mn
        vo_ref[...] = vn
        d_ref[...] = -ADAM_LR * ((mn * c1) / (jnp.sqrt(vn * c2) + ADAM_EPS) + ADAM_WD * w_ref[...])

    blk = pl.BlockSpec((tr, c), lambda i: (i, 0))
    return pl.pallas_call(
        body, name=name, grid=(r // tr,), in_specs=[blk] * 4, out_specs=[blk] * 3,
        out_shape=[jax.ShapeDtypeStruct((r, c), F32)] * 3,
        compiler_params=_params(("parallel",)),
    )(w, g, m, v)


def _adamw_halves(w, g_own, g_other, m, v, sel, name, axis):
    r, c = w.shape
    tr, tc = _tile(g_own.shape[0], g_own.shape[1], 4)
    nh = (g_own.shape[axis]) // (tr, tc)[axis]
    c1 = 1.0 / (1.0 - ADAM_B1 ** ADAM_STEP)
    c2 = 1.0 / (1.0 - ADAM_B2 ** ADAM_STEP)

    def body(sel_ref, w_ref, go_ref, gx_ref, m_ref, v_ref, d_ref, mo_ref, vo_ref, g_ref):
        mine = (pl.program_id(axis) // nh) == sel_ref[0]
        gg = jnp.where(mine, go_ref[...], gx_ref[...])
        g_ref[...] = gg
        mn = ADAM_B1 * m_ref[...] + (1.0 - ADAM_B1) * gg
        vn = ADAM_B2 * v_ref[...] + (1.0 - ADAM_B2) * (gg * gg)
        mo_ref[...] = mn
        vo_ref[...] = vn
        d_ref[...] = -ADAM_LR * ((mn * c1) / (jnp.sqrt(vn * c2) + ADAM_EPS) + ADAM_WD * w_ref[...])

    blk = pl.BlockSpec((tr, tc), lambda i, j, sel_ref: (i, j))
    if axis == 0:
        half = pl.BlockSpec((tr, tc), lambda i, j, sel_ref: (i % nh, j))
    else:
        half = pl.BlockSpec((tr, tc), lambda i, j, sel_ref: (i, j % nh))
    return pl.pallas_call(
        body, name=name,
        grid_spec=pltpu.PrefetchScalarGridSpec(
            num_scalar_prefetch=1, grid=(r // tr, c // tc),
            in_specs=[blk, half, half, blk, blk], out_specs=[blk] * 4),
        out_shape=[jax.ShapeDtypeStruct((r, c), F32)] * 4,
        compiler_params=_params(("parallel", "parallel")),
    )(sel, w, g_own, g_other, m, v)


WEIGHTS = ['norm_g', 'w_in', 'conv_w', 'w_conv_out', 'mla_q_norm_g', 'w_uq', 'mla_kv_norm_g', 'w_ukv',
           'mla_qn_nope_g', 'mla_qn_rope_g', 'mla_kn_nope_g', 'mla_kn_rope_g', 'w_mla_out', 'mem_norm_g',
           'w_mem_kv', 'mem_qn_g', 'mem_kn_g', 'w_mem_out', 'w_o']
COL_SHARDED = ['w_in', 'w_conv_out', 'w_uq', 'w_ukv', 'w_mem_out']
ROW_SHARDED = ['w_mla_out', 'w_mem_kv', 'w_o']
SMALL = ['norm_g', 'conv_w', 'mla_q_norm_g', 'mla_kv_norm_g', 'mla_qn_nope_g', 'mla_qn_rope_g', 'mla_kn_nope_g',
         'mla_kn_rope_g', 'mem_norm_g', 'mem_qn_g', 'mem_kn_g']
SMALL_SIZES = [2048, 3072, 512, 512, 128, 64, 128, 64, 2048, 256, 256]
PACK_ROWS = 72


def _full_from_shards(name, g4):
    if name in COL_SHARDED:
        return g4.transpose(1, 0, 2).reshape(g4.shape[1], 4 * g4.shape[2])
    return g4.reshape(4 * g4.shape[1], g4.shape[2])


def _w_all_t_from_w_in_t(w_in_t):
    conv, small, mz, memq, gates = (w_in_t[0:4096], w_in_t[4096:5184], w_in_t[5184:7232],
                                    w_in_t[7232:9280], w_in_t[9280:15424])
    return jnp.concatenate([gates, mz, conv, memq, small, jnp.zeros((2048 - 1088, D_MODEL), BF16)], axis=0)


def _pad_w_uq(w_uq):
    return jnp.pad(w_uq.reshape(512, MLA_HEADS, 192), ((0, 0), (0, 0), (0, 64))).reshape(512, 4096)


def _grad_halves_by_owner(name, g):
    if name == "w_all_t":
        dw_g, dw_z, dw_c, dw_m, dw_s = g
        w_in_t = jnp.concatenate([dw_c, dw_s[:1088], dw_z, dw_m, dw_g], axis=0)
        return "w_in", w_in_t.reshape(4, 15424 // 4, 2, D_MODEL // 2).transpose(2, 0, 1, 3)
    if name == "w_uq":
        g = g.reshape(512, MLA_HEADS, HEAD_PAD)[:, :, :192].reshape(512, 3072)
    r, c = g.shape
    if name in COL_SHARDED:
        return name, g.reshape(2, r // 2, 4, c // 4).transpose(0, 2, 1, 3)
    return name, g.reshape(4, 2, r // 8, c).transpose(1, 0, 2, 3)


LATE_WEIGHTS = ['w_conv_out', 'w_uq', 'w_ukv', 'w_mem_out', 'w_mla_out', 'w_mem_kv', 'w_o']
HOSTS = {"proj_fwd": ("gather", 1), "lora_fwd": ("gather", 2),
         "da_conv": ("g1", 1), "q_prep_bwd": ("g1", 2), "k_prep_bwd": ("g1", 3),
         "dw_in_0": ("g2", 1), "dw_in_2": ("g2", 2), "dw_in_4": ("g2", 3),
         "dh_a": ("g3", 1), "dh_b": ("g3", 2), "dx_bwd": ("g3", 3)}
GROUP_OF = {"w_o": "g1", "w_conv_out": "g1", "w_mla_out": "g1", "w_mem_out": "g1",
            "w_uq": "g2", "w_ukv": "g2", "w_mem_kv": "g2", "w_in": "g3"}


class _Comm:
    def __init__(self, shards, w_in_t_full, conv_full):
        self.shards = shards
        self.w = {"w_all_t": _w_all_t_from_w_in_t(w_in_t_full), "conv_w": conv_full}
        self.bufs = None
        self.groups = {"g1": _GroupReduce(), "g2": _GroupReduce(), "g3": _GroupReduce()}

    def weight(self, name):
        return self.w[name]

    def side(self, tag):
        if tag not in HOSTS:
            return None
        kind, stage = HOSTS[tag]
        if kind == "gather":
            return _side_gather_ici([self.shards[n] for n in LATE_WEIGHTS]) if stage == 1 else \
                _side_gather_pass(self.bufs)
        return self.groups[kind].side(stage)

    def done(self, tag, outs):
        kind, stage = HOSTS[tag]
        if kind != "gather":
            self.groups[kind].done(stage, outs)
        elif stage == 1:
            self.bufs = list(outs)
        else:
            for n, buf in zip(LATE_WEIGHTS, outs):
                own = self.shards[n]
                self.w[n] = _full_from_shards(n, _with_own(buf.reshape((4,) + own.shape), own))
            self.w["w_uq"] = _pad_w_uq(self.w["w_uq"])

    def put(self, name, g):
        name, gb = _grad_halves_by_owner(name, g)
        self.groups[GROUP_OF[name]].put(name, gb)

    def reduced(self):
        out = {}
        for grp in self.groups.values():
            out.update(grp.result())
        return out


def kernel(x, positions, mem, norm_g, w_in, conv_w, w_conv_out, mla_q_norm_g, w_uq, mla_kv_norm_g, w_ukv, mla_qn_nope_g, mla_qn_rope_g, mla_kn_nope_g, mla_kn_rope_g, w_mla_out, mem_norm_g, w_mem_kv, mem_qn_g, mem_kn_g, w_mem_out, w_o, loss_target, m_norm_g, m_w_in, m_conv_w, m_w_conv_out, m_mla_q_norm_g, m_w_uq, m_mla_kv_norm_g, m_w_ukv, m_mla_qn_nope_g, m_mla_qn_rope_g, m_mla_kn_nope_g, m_mla_kn_rope_g, m_w_mla_out, m_mem_norm_g, m_w_mem_kv, m_mem_qn_g, m_mem_kn_g, m_w_mem_out, m_w_o, v_norm_g, v_w_in, v_conv_w, v_w_conv_out, v_mla_q_norm_g, v_w_uq, v_mla_kv_norm_g, v_w_ukv, v_mla_qn_nope_g, v_mla_qn_rope_g, v_mla_kn_nope_g, v_mla_kn_rope_g, v_w_mla_out, v_mem_norm_g, v_w_mem_kv, v_mem_qn_g, v_mem_kn_g, v_w_mem_out, v_w_o):
    args = locals()
    w = {n: args[n][0] for n in WEIGHTS}
    m1 = {n: args["m_" + n][0] for n in WEIGHTS}
    v2 = {n: args["v_" + n][0] for n in WEIGHTS}
    xi, yi, ci = _me()
    chip = 2 * xi + yi

    conv_slot = jnp.zeros((3, 4, 256), F32).at[:, chip, :].set(w["conv_w"] * jnp.where(ci == 0, 1.0, 0.0))
    pre = jnp.zeros((PACK_ROWS * 128,), F32).at[0:3072].set(conv_slot.reshape(3072))
    conv_full = _allreduce_small(pre.reshape(PACK_ROWS, 128), "conv_w").reshape(-1)[0:3072].reshape(3, 1024)

    w_in_t = w["w_in"].T.astype(BF16)
    w_in_t_full = _with_own(_allgather_shards(w_in_t, "ag_w_in"), w_in_t).reshape(4 * w_in_t.shape[0], D_MODEL)
    comm = _Comm({n: w[n].astype(BF16) for n in LATE_WEIGHTS}, w_in_t_full, conv_full)
    gains = {n: w[n].reshape(1, -1) for n in SMALL if n != "conv_w"}

    loss_parts, grad_x, gsmall = _local_step(x[0], positions[0], mem[0], loss_target[0], comm, gains)

    loss_local = 0.5 * jnp.sum(loss_parts) * (1.0 / D_MODEL)
    flat = jnp.concatenate([gsmall[n].reshape(-1) for n in SMALL] + [loss_local.reshape(1)])
    flat = jnp.pad(flat, (0, PACK_ROWS * 128 - flat.shape[0]))
    tot = _allreduce_small(flat.reshape(PACK_ROWS, 128), "grads").reshape(-1)
    grads = {}
    off = 0
    for n, sz in zip(SMALL, SMALL_SIZES):
        grads[n] = tot[off:off + sz]
        off += sz
    loss = tot[off]
    grads["conv_w"] = lax.dynamic_slice(grads["conv_w"].reshape(3, 1024), (0, chip * 256), (3, 256))
    for n in SMALL:
        grads[n] = grads[n].reshape(w[n].shape)

    deltas, new_m, new_v = {}, {}, {}
    for n in SMALL:
        shp = w[n].shape
        two_d = (lambda a: a.reshape(1, -1)) if len(shp) == 1 else (lambda a: a)
        d, mn, vn = _adamw(two_d(w[n]), two_d(grads[n]), two_d(m1[n]), two_d(v2[n]), "adamw_" + n)
        deltas[n], new_m[n], new_v[n] = d.reshape(shp), mn.reshape(shp), vn.reshape(shp)

    sel_c = jnp.reshape(ci, (1,)).astype(jnp.int32)
    reduced = comm.reduced()
    for n in COL_SHARDED + ROW_SHARDED:
        g_own, g_other = reduced[n]
        if n == "w_in":
            outs = _adamw_halves(w[n].T, g_own, g_other, m1[n].T, v2[n].T, sel_c, "adamw_" + n, 1)
            deltas[n], new_m[n], new_v[n], grads[n] = [o.T for o in outs]
        else:
            deltas[n], new_m[n], new_v[n], grads[n] = _adamw_halves(w[n], g_own, g_other, m1[n], v2[n], sel_c,
                                                                     "adamw_" + n, 0)

    lead = lambda a: a[None]
    return (loss, grad_x[None], *[lead(grads[n]) for n in WEIGHTS], *[lead(deltas[n]) for n in WEIGHTS],
            *[lead(new_m[n]) for n in WEIGHTS], *[lead(new_v[n]) for n in WEIGHTS])
```

```python
import functools

import numpy as np
import jax
import jax.numpy as jnp
from jax import lax
from jax.experimental import pallas as pl
from jax.experimental.pallas import tpu as pltpu

F32 = jnp.float32
BF16 = jnp.bfloat16
MESH = pl.DeviceIdType.MESH

D_MODEL = 2048
EPS = 1e-6
CHUNK = 64
MLA_HEADS = 16
QK_NOPE = 128
QK_ROPE = 64
HEAD_PAD = 256
MEM_TOKENS = 256
MEM_HEADS = 4
MEM_HEAD_DIM = 256
ROPE_THETA = 10000.0
MLA_SCALE = (QK_NOPE + QK_ROPE) ** -0.5
MEM_SCALE = MEM_HEAD_DIM ** -0.5
LN2 = 0.6931471805599453
Q_SCALE = MLA_SCALE / LN2
NEG = -1e30

ADAM_LR = 0.001
ADAM_B1 = 0.9
ADAM_B2 = 0.999
ADAM_EPS = 1e-08
ADAM_WD = 0.01
ADAM_STEP = 10

NP = 16384
COL_G, COL_Z, COL_C, COL_M, COL_S = 0, 6144, 8192, 12288, 14336

TM_PROJ = 1024
TN_PROJ = 1024
TS_ROW = 512
TS_BWD = 256
TQ_ATT = 1024
TK_ATT = 1024
TR_ATT = 256
VMEM_LIMIT = 56 * 1024 * 1024

NT_DIMS = (((1,), (1,)), ((), ()))
TN_DIMS = (((0,), (0,)), ((), ()))


def _dot(a, b, mode="nn"):
    if mode == "nn":
        return jnp.dot(a, b, preferred_element_type=F32)
    return lax.dot_general(a, b, NT_DIMS if mode == "nt" else TN_DIMS, preferred_element_type=F32)


def _sigmoid(z):
    return 1.0 / (1.0 + jnp.exp(-z))


def _params(sem=None, vmem=VMEM_LIMIT):
    return pltpu.CompilerParams(dimension_semantics=sem, vmem_limit_bytes=vmem)


def _fold8(v):
    r, c = v.shape
    return v.reshape(r // 8, 8, c).sum(axis=0)


def _swap32(t):
    lane = lax.broadcasted_iota(jnp.int32, t.shape, 1)
    return jnp.where(lane < 32, pltpu.roll(t, 96, 1), pltpu.roll(t, 32, 1))


def _rope(t, cs, sg):
    return t * cs + _swap32(t) * sg


def _rope_bwd(d, cs, sg):
    return d * cs + _swap32(d * sg)


def _rms_fwd(xf, g, n):
    r = lax.rsqrt(jnp.sum(xf * xf, axis=-1, keepdims=True) * (1.0 / n) + EPS)
    return xf * r * g


def _rms_bwd(xf, g, dy, n):
    r = lax.rsqrt(jnp.sum(xf * xf, axis=-1, keepdims=True) * (1.0 / n) + EPS)
    xhat = xf * r
    dyg = dy * g
    dx = r * (dyg - xhat * (jnp.sum(dyg * xhat, axis=-1, keepdims=True) * (1.0 / n)))
    return dx, dy * xhat


ANY = pl.BlockSpec(memory_space=pl.ANY)


class _Side:
    def __init__(self, inputs, out_shapes, n_copies, build, aliases=()):
        self.inputs, self.out_shapes, self.n_copies = list(inputs), list(out_shapes), n_copies
        self.build, self.aliases = build, dict(aliases)


def _pcall(body, side, name, grid, in_specs, out_specs, out_shape, args, sem, scratch_shapes=(), prefetch=()):
    n_pre, n_in, n_out = len(prefetch), len(in_specs), len(out_specs)
    if side is None:
        outs = pl.pallas_call(
            body, name=name,
            grid_spec=pltpu.PrefetchScalarGridSpec(num_scalar_prefetch=n_pre, grid=grid, in_specs=list(in_specs),
                                                   out_specs=list(out_specs), scratch_shapes=list(scratch_shapes)),
            out_shape=list(out_shape), compiler_params=_params(sem),
        )(*prefetch, *args)
        return list(outs), []
    ns_in, ns_out = len(side.inputs), len(side.out_shapes)

    def wrapped(*refs):
        pre, r = refs[:n_pre], refs[n_pre:]
        ins, s_ins = r[:n_in], r[n_in:n_in + ns_in]
        outs = r[n_in + ns_in:n_in + ns_in + n_out]
        s_outs = r[n_in + ns_in + n_out:n_in + ns_in + n_out + ns_out]
        scr, send_sems, recv_sems = r[n_in + ns_in + n_out + ns_out:-2], r[-2], r[-1]
        pairs = side.build(s_ins, s_outs, send_sems, recv_sems)
        first = pl.program_id(0) == 0
        last = pl.program_id(0) == grid[0] - 1
        for d in range(1, len(grid)):
            first = first & (pl.program_id(d) == 0)
            last = last & (pl.program_id(d) == grid[d] - 1)

        @pl.when(first)
        def _():
            for send, _ in pairs:
                send.start()

        body(*pre, *ins, *outs, *scr)

        @pl.when(last)
        def _():
            for send, recv in pairs:
                send.wait_send()
                recv.wait_recv()

    outs = pl.pallas_call(
        wrapped, name=name,
        grid_spec=pltpu.PrefetchScalarGridSpec(
            num_scalar_prefetch=n_pre, grid=grid, in_specs=list(in_specs) + [ANY] * ns_in,
            out_specs=list(out_specs) + [ANY] * ns_out,
            scratch_shapes=list(scratch_shapes) + [pltpu.SemaphoreType.DMA((side.n_copies,)),
                                                   pltpu.SemaphoreType.DMA((side.n_copies,))]),
        out_shape=list(out_shape) + side.out_shapes,
        input_output_aliases={n_pre + n_in + i: n_out + o for i, o in side.aliases.items()},
        compiler_params=_params(("arbitrary",) * len(grid)),
    )(*prefetch, *args, *side.inputs)
    return list(outs[:n_out]), list(outs[n_out:])


def _matmul(a, b, mode, out_dtype, name, tm=1024, tn=1024, tk=1024, side=None):
    if mode == "nn":
        (m, k), (_, n) = a.shape, b.shape
    elif mode == "nt":
        (m, k), (n, _) = a.shape, b.shape
    else:
        (k, m), (_, n) = a.shape, b.shape
    tm, tn, tk = min(tm, m), min(tn, n), min(tk, k)
    nk = k // tk
    if mode == "tn":
        a_spec = pl.BlockSpec((tk, tm), lambda i, j, kk: (kk, i))
    else:
        a_spec = pl.BlockSpec((tm, tk), lambda i, j, kk: (i, kk))
    if mode == "nt":
        b_spec = pl.BlockSpec((tn, tk), lambda i, j, kk: (j, kk))
    else:
        b_spec = pl.BlockSpec((tk, tn), lambda i, j, kk: (kk, j))

    def body(a_ref, b_ref, o_ref, acc_ref):
        kk = pl.program_id(2)

        @pl.when(kk == 0)
        def _():
            acc_ref[...] = jnp.zeros_like(acc_ref)

        acc_ref[...] += _dot(a_ref[...].astype(BF16), b_ref[...].astype(BF16), mode)

        @pl.when(kk == nk - 1)
        def _():
            o_ref[...] = acc_ref[...].astype(out_dtype)

    outs, extra = _pcall(
        body, side, name, (m // tm, n // tn, nk), [a_spec, b_spec],
        [pl.BlockSpec((tm, tn), lambda i, j, kk: (i, j))], [jax.ShapeDtypeStruct((m, n), out_dtype)], (a, b),
        ("parallel", "parallel", "arbitrary"), scratch_shapes=[pltpu.VMEM((tm, tn), F32)])
    return outs[0] if side is None else (outs[0], extra)


def _proj_fwd(x, g, w_all_t, side=None):
    s = x.shape[0]
    tm, tn = min(TM_PROJ, s), TN_PROJ

    def body(x_ref, g_ref, w_ref, p_ref, h_ref):
        @pl.when(pl.program_id(1) == 0)
        def _():
            h_ref[...] = _rms_fwd(x_ref[...], g_ref[...], D_MODEL).astype(BF16)

        p_ref[...] = _dot(h_ref[...], w_ref[...], "nt").astype(BF16)

    return _pcall(
        body, side, "proj_fwd", (s // tm, NP // tn),
        [pl.BlockSpec((tm, D_MODEL), lambda i, j: (i, 0)),
         pl.BlockSpec((1, D_MODEL), lambda i, j: (0, 0)),
         pl.BlockSpec((tn, D_MODEL), lambda i, j: (j, 0))],
        [pl.BlockSpec((tm, tn), lambda i, j: (i, j)), pl.BlockSpec((tm, D_MODEL), lambda i, j: (i, 0))],
        [jax.ShapeDtypeStruct((s, NP), BF16), jax.ShapeDtypeStruct((s, D_MODEL), BF16)], (x, g, w_all_t),
        ("parallel", "arbitrary"))


def _shift_rows(v, carry, j, ts, back):
    if back:
        main = pltpu.roll(v, j, 0)
        edge = pltpu.roll(jnp.concatenate([carry, v[:8]], axis=0), j, 0)[8:]
        return jnp.concatenate([edge, main[8:]], axis=0)
    main = pltpu.roll(v, ts - j, 0)
    edge = pltpu.roll(jnp.concatenate([v[ts - 8:], carry], axis=0), 16 - j, 0)[:8]
    return jnp.concatenate([main[:ts - 8], edge], axis=0)


def _conv_fwd(p_all, conv_w):
    s = p_all.shape[0]
    ts = min(TS_ROW, s)
    c0 = COL_C // 1024

    def body(cg_ref, bg_ref, u_ref, z_ref, w_ref, a_ref, co_ref, carry_ref):
        @pl.when(pl.program_id(0) == 0)
        def _():
            carry_ref[...] = jnp.zeros_like(carry_ref)

        p = cg_ref[...].astype(F32) * u_ref[...].astype(F32)
        carry = carry_ref[...]
        co = (w_ref[2:3, :] * p + w_ref[1:2, :] * _shift_rows(p, carry, 1, ts, True)
              + w_ref[0:1, :] * _shift_rows(p, carry, 2, ts, True))
        carry_ref[...] = p[ts - 8:]
        z = z_ref[...].astype(F32)
        a_ref[...] = (bg_ref[...].astype(F32) * co * (z * _sigmoid(z))).astype(BF16)
        co_ref[...] = co.astype(BF16)

    seg = lambda c: pl.BlockSpec((ts, 1024), lambda i: (i, c0 + c))
    return pl.pallas_call(
        body, name="conv_fwd", grid=(s // ts,),
        in_specs=[seg(0), seg(1), seg(2), seg(3), pl.BlockSpec((3, 1024), lambda i: (0, 0))],
        out_specs=[pl.BlockSpec((ts, 1024), lambda i: (i, 0))] * 2,
        out_shape=[jax.ShapeDtypeStruct((s, 1024), BF16)] * 2,
        scratch_shapes=[pltpu.VMEM((8, 1024), F32)],
        compiler_params=_params(("arbitrary",)),
    )(p_all, p_all, p_all, p_all, conv_w)


def _lora_fwd(p_all, gq, gkv, gkr, cs, sg, side=None):
    s = p_all.shape[0]
    ts = min(TS_ROW, s)

    def body(cq_ref, ckv_ref, kr_ref, gq_ref, gkv_ref, gkr_ref, cs_ref, sg_ref, cqn_ref, ckvn_ref, krope_ref):
        cqn_ref[...] = _rms_fwd(cq_ref[...].astype(F32), gq_ref[...], 512).astype(BF16)
        ckvn_ref[...] = _rms_fwd(ckv_ref[...].astype(F32), gkv_ref[...], 512).astype(BF16)
        kn = _rms_fwd(kr_ref[...].astype(F32), gkr_ref[...], QK_ROPE)
        krope_ref[...] = _rope(kn, cs_ref[...], sg_ref[...]).astype(BF16)

    row = lambda w: pl.BlockSpec((ts, w), lambda i: (i, 0))
    vec = lambda w: pl.BlockSpec((1, w), lambda i: (0, 0))
    return _pcall(
        body, side, "lora_fwd", (s // ts,),
        [pl.BlockSpec((ts, 512), lambda i: (i, COL_S // 512)),
         pl.BlockSpec((ts, 512), lambda i: (i, COL_S // 512 + 1)),
         pl.BlockSpec((ts, 128), lambda i: (i, (COL_S + 1024) // 128)),
         vec(512), vec(512), vec(128), row(128), row(128)],
        [row(512), row(512), row(128)],
        [jax.ShapeDtypeStruct((s, 512), BF16), jax.ShapeDtypeStruct((s, 512), BF16),
         jax.ShapeDtypeStruct((s, 128), BF16)],
        (p_all, p_all, p_all, gq, gkv, gkr, cs, sg), ("parallel",))


def _q_prep(q_raw, gn, gr, cs, sg):
    s = q_raw.shape[0]
    ts = min(TS_ROW, s)

    def body(q_ref, gn_ref, gr_ref, cs_ref, sg_ref, o_ref):
        for h in range(MLA_HEADS):
            q = q_ref[:, h * HEAD_PAD:(h + 1) * HEAD_PAD].astype(F32)
            a = _rms_fwd(q[:, :128], gn_ref[...], QK_NOPE)
            b = _rope(_rms_fwd(q[:, 128:], gr_ref[...], QK_ROPE), cs_ref[...], sg_ref[...])
            o_ref[h] = (jnp.concatenate([a, b], axis=1) * Q_SCALE).astype(BF16)

    return pl.pallas_call(
        body, name="q_prep", grid=(s // ts,),
        in_specs=[pl.BlockSpec((ts, MLA_HEADS * HEAD_PAD), lambda i: (i, 0)),
                  pl.BlockSpec((1, 128), lambda i: (0, 0)), pl.BlockSpec((1, 128), lambda i: (0, 0)),
                  pl.BlockSpec((ts, 128), lambda i: (i, 0)), pl.BlockSpec((ts, 128), lambda i: (i, 0))],
        out_specs=pl.BlockSpec((MLA_HEADS, ts, HEAD_PAD), lambda i: (0, i, 0)),
        out_shape=jax.ShapeDtypeStruct((MLA_HEADS, s, HEAD_PAD), BF16),
        compiler_params=_params(("parallel",)),
    )(q_raw, gn, gr, cs, sg)


def _k_prep(kv, krope, gk):
    s = kv.shape[0]
    ts = min(TS_ROW, s)

    def body(k_ref, kr_ref, gk_ref, o_ref, vt_ref):
        for h in range(MLA_HEADS):
            a = _rms_fwd(k_ref[:, h * HEAD_PAD:h * HEAD_PAD + 128].astype(F32), gk_ref[...], QK_NOPE)
            o_ref[h] = jnp.concatenate([a.astype(BF16), kr_ref[...]], axis=1)
            vt_ref[h] = k_ref[:, h * HEAD_PAD + 128:(h + 1) * HEAD_PAD].astype(F32).T.astype(BF16)

    return pl.pallas_call(
        body, name="k_prep", grid=(s // ts,),
        in_specs=[pl.BlockSpec((ts, MLA_HEADS * HEAD_PAD), lambda i: (i, 0)),
                  pl.BlockSpec((ts, 128), lambda i: (i, 0)),
                  pl.BlockSpec((1, 128), lambda i: (0, 0))],
        out_specs=[pl.BlockSpec((MLA_HEADS, ts, HEAD_PAD), lambda i: (0, i, 0)),
                   pl.BlockSpec((MLA_HEADS, 128, ts), lambda i: (0, 0, i))],
        out_shape=[jax.ShapeDtypeStruct((MLA_HEADS, s, HEAD_PAD), BF16),
                   jax.ShapeDtypeStruct((MLA_HEADS, 128, s), BF16)],
        compiler_params=_params(("parallel",)),
    )(kv, krope, gk)


def _chunk_mask(tq, tk, row0=0):
    r = (lax.broadcasted_iota(jnp.int32, (tq, tk), 0) + row0) // CHUNK
    c = lax.broadcasted_iota(jnp.int32, (tq, tk), 1) // CHUNK
    return c <= r


def _causal_pairs(n, by_key):
    if by_key:
        pairs = [(i, j) for j in range(n) for i in range(j, n)]
    else:
        pairs = [(i, j) for i in range(n) for j in range(i + 1)]
    return (jnp.asarray([p[0] for p in pairs], jnp.int32), jnp.asarray([p[1] for p in pairs], jnp.int32))


def _flash_fwd(qcat, kcat, vt, p_all):
    s = qcat.shape[1]
    tq = tk = min(TQ_ATT, s)
    tc = min(TR_ATT, tq)
    nq = s // tq
    zc = COL_Z // 128
    qi, kj = _causal_pairs(nq, False)

    def body(qi_ref, kj_ref, q_ref, k_ref, vt_ref, z_ref, y_ref, a_ref, lse_ref, m_sc, l_sc, acc_sc):
        t = pl.program_id(1)
        i, j = qi_ref[t], kj_ref[t]

        @pl.when(j == 0)
        def _():
            m_sc[...] = jnp.full_like(m_sc, NEG)
            l_sc[...] = jnp.zeros_like(l_sc)
            acc_sc[...] = jnp.zeros_like(acc_sc)

        def step(masked):
            nc = tq // tc
            state = [[m_sc[:, c * tc:(c + 1) * tc], l_sc[:, c * tc:(c + 1) * tc], acc_sc[:, c * tc:(c + 1) * tc]]
                     for c in range(nc)]
            units = [(u, c) for u in range(tk // tc) for c in range(nc) if (u <= c or not masked)]

            def scores(u, c):
                return _dot(k_ref[0, u * tc:(u + 1) * tc, :], q_ref[0, c * tc:(c + 1) * tc, :], "nt")

            def weighted_values(u, c, alpha, pb):
                state[c][2] = alpha * state[c][2] + _dot(vt_ref[0, :, u * tc:(u + 1) * tc], pb)

            ahead = 4
            pending = [scores(*un) for un in units[:ahead]]
            late = None
            for n, (u, c) in enumerate(units):
                st = pending.pop(0)
                if n + ahead < len(units):
                    pending.append(scores(*units[n + ahead]))
                if masked and u == c:
                    kc = lax.broadcasted_iota(jnp.int32, (tc, tc), 0) // CHUNK
                    qc = lax.broadcasted_iota(jnp.int32, (tc, tc), 1) // CHUNK
                    st = jnp.where(kc <= qc, st, NEG)
                m_run, l_run, _ = state[c]
                m_new = jnp.maximum(m_run, jnp.max(st, axis=0, keepdims=True))
                alpha = jnp.exp2(m_run - m_new)
                p = jnp.exp2(st - m_new)
                state[c][0] = m_new
                state[c][1] = alpha * l_run + jnp.sum(p, axis=0, keepdims=True)
                if late is not None:
                    weighted_values(*late)
                late = (u, c, alpha, p.astype(BF16))
            weighted_values(*late)
            m_sc[...] = jnp.concatenate([s_[0] for s_ in state], axis=1)
            l_sc[...] = jnp.concatenate([s_[1] for s_ in state], axis=1)
            acc_sc[...] = jnp.concatenate([s_[2] for s_ in state], axis=1)

        @pl.when(j < i)
        def _():
            step(False)

        @pl.when(j == i)
        def _():
            step(True)
            y = (acc_sc[...] * (1.0 / l_sc[...])).T
            z = z_ref[...].astype(F32)
            y_ref[...] = y.astype(BF16)
            a_ref[...] = (y * (z * _sigmoid(z))).astype(BF16)
            lse2 = m_sc[...] + jnp.log(l_sc[...]) * (1.0 / LN2)
            lse_ref[0] = jnp.broadcast_to(lse2, (128, tq)).T

    return pl.pallas_call(
        body, name="flash_fwd",
        grid_spec=pltpu.PrefetchScalarGridSpec(
            num_scalar_prefetch=2, grid=(MLA_HEADS, qi.shape[0]),
            in_specs=[pl.BlockSpec((1, tq, HEAD_PAD), lambda h, t, qi_ref, kj_ref: (h, qi_ref[t], 0)),
                      pl.BlockSpec((1, tk, HEAD_PAD), lambda h, t, qi_ref, kj_ref: (h, kj_ref[t], 0)),
                      pl.BlockSpec((1, 128, tk), lambda h, t, qi_ref, kj_ref: (h, 0, kj_ref[t])),
                      pl.BlockSpec((tq, 128), lambda h, t, qi_ref, kj_ref: (qi_ref[t], zc + h))],
            out_specs=[pl.BlockSpec((tq, 128), lambda h, t, qi_ref, kj_ref: (qi_ref[t], h)),
                       pl.BlockSpec((tq, 128), lambda h, t, qi_ref, kj_ref: (qi_ref[t], h)),
                       pl.BlockSpec((1, tq, 128), lambda h, t, qi_ref, kj_ref: (h, qi_ref[t], 0))],
            scratch_shapes=[pltpu.VMEM((1, tq), F32), pltpu.VMEM((1, tq), F32), pltpu.VMEM((128, tq), F32)]),
        out_shape=[jax.ShapeDtypeStruct((s, MLA_HEADS * 128), BF16),
                   jax.ShapeDtypeStruct((s, MLA_HEADS * 128), BF16),
                   jax.ShapeDtypeStruct((MLA_HEADS, s, 128), F32)],
        compiler_params=_params(("parallel", "arbitrary")),
    )(qi, kj, qcat, kcat, vt, p_all)


def _mem_kv_fwd(mem, g_norm, w_kv, g_kn):
    m = mem.shape[0]

    def body(mem_ref, g_ref, w_ref, gk_ref, memn_ref, kraw_ref, kn_ref, v_ref):
        memn = _rms_fwd(mem_ref[...], g_ref[...], D_MODEL).astype(BF16)
        memn_ref[...] = memn
        kvm = _dot(memn, w_ref[...])
        kraw_ref[...] = kvm[:, :1024]
        v_ref[...] = kvm[:, 1024:].astype(BF16)
        for hh in range(MEM_HEADS):
            sl = slice(hh * MEM_HEAD_DIM, (hh + 1) * MEM_HEAD_DIM)
            kn_ref[:, sl] = _rms_fwd(kvm[:, sl], gk_ref[...], MEM_HEAD_DIM).astype(BF16)

    return pl.pallas_call(
        body, name="mem_kv_fwd",
        out_shape=[jax.ShapeDtypeStruct((m, D_MODEL), BF16), jax.ShapeDtypeStruct((m, 1024), F32),
                   jax.ShapeDtypeStruct((m, 1024), BF16), jax.ShapeDtypeStruct((m, 1024), BF16)],
        compiler_params=_params(),
    )(mem, g_norm, w_kv, g_kn)


def _mem_attn_fwd(p_all, kn, v, gq):
    s = p_all.shape[0]
    tq = min(TS_ROW, s)

    def body(q_ref, z_ref, kn_ref, v_ref, gq_ref, a_ref):
        z = z_ref[...].astype(F32)
        gate = z * _sigmoid(z)
        for hh in range(MEM_HEADS):
            sl = slice(hh * MEM_HEAD_DIM, (hh + 1) * MEM_HEAD_DIM)
            qn = _rms_fwd(q_ref[:, sl].astype(F32), gq_ref[...], MEM_HEAD_DIM).astype(BF16)
            sc = _dot(qn, kn_ref[:, sl], "nt") * MEM_SCALE
            e = jnp.exp(sc - jnp.max(sc, axis=-1, keepdims=True))
            p = e * (1.0 / jnp.sum(e, axis=-1, keepdims=True))
            y = _dot(p.astype(BF16), v_ref[:, sl])
            a_ref[:, sl] = (y * gate[:, sl]).astype(BF16)

    full = lambda a: pl.BlockSpec(a.shape, lambda i: (0, 0))
    return pl.pallas_call(
        body, name="mem_attn_fwd", grid=(s // tq,),
        in_specs=[pl.BlockSpec((tq, 1024), lambda i: (i, COL_M // 1024)),
                  pl.BlockSpec((tq, 1024), lambda i: (i, COL_M // 1024 + 1)),
                  full(kn), full(v), full(gq)],
        out_specs=pl.BlockSpec((tq, 1024), lambda i: (i, 0)),
        out_shape=jax.ShapeDtypeStruct((s, 1024), BF16),
        compiler_params=_params(("parallel",)),
    )(p_all, p_all, kn, v, gq)


def _merge_fwd(a_conv, w_c, a_mla, w_m, a_mem, w_e, p_all):
    s = a_conv.shape[0]
    tm, tn = min(TS_ROW, s), 1024
    nj = D_MODEL // tn

    def body(ac_ref, wc_ref, am_ref, wm_ref, ae_ref, we_ref, gc_ref, gm_ref, ge_ref,
             oc_ref, om_ref, oe_ref, mg_ref):
        oc = _dot(ac_ref[...], wc_ref[...])
        om = _dot(am_ref[...], wm_ref[...])
        oe = _dot(ae_ref[...], we_ref[...])
        oc_ref[...] = oc.astype(BF16)
        om_ref[...] = om.astype(BF16)
        oe_ref[...] = oe.astype(BF16)
        mg_ref[...] = (_sigmoid(gc_ref[...].astype(F32)) * oc + _sigmoid(gm_ref[...].astype(F32)) * om
                       + _sigmoid(ge_ref[...].astype(F32)) * oe).astype(BF16)

    act = lambda k: pl.BlockSpec((tm, k), lambda i, j: (i, 0))
    wgt = lambda k: pl.BlockSpec((k, tn), lambda i, j: (0, j))
    gate = lambda b: pl.BlockSpec((tm, tn), lambda i, j: (i, b * nj + j))
    out = pl.BlockSpec((tm, tn), lambda i, j: (i, j))
    return pl.pallas_call(
        body, name="merge_fwd", grid=(s // tm, nj),
        in_specs=[act(1024), wgt(1024), act(2048), wgt(2048), act(1024), wgt(1024), gate(0), gate(1), gate(2)],
        out_specs=[out] * 4,
        out_shape=[jax.ShapeDtypeStruct((s, D_MODEL), BF16)] * 4,
        compiler_params=_params(("parallel", "parallel")),
    )(a_conv, w_c, a_mla, w_m, a_mem, w_e, p_all, p_all, p_all)


def _out_fwd(merged, w_o, x, target):
    s = merged.shape[0]
    tm, tn = min(TS_ROW, s), 1024
    nj = D_MODEL // tn

    def body(mg_ref, w_ref, x_ref, t_ref, dy_ref, dyb_ref, ls_ref):
        e = x_ref[...] + _dot(mg_ref[...], w_ref[...]) - t_ref[...]
        dy = e * (1.0 / D_MODEL)
        dy_ref[...] = dy
        dyb_ref[...] = dy.astype(BF16)
        r = _fold8(e * e)
        acc = r[:, 0:128]
        for cc in range(1, tn // 128):
            acc = acc + r[:, cc * 128:(cc + 1) * 128]
        ls_ref[...] = acc

    tile = pl.BlockSpec((tm, tn), lambda i, j: (i, j))
    return pl.pallas_call(
        body, name="out_fwd", grid=(s // tm, nj),
        in_specs=[pl.BlockSpec((tm, D_MODEL), lambda i, j: (i, 0)),
                  pl.BlockSpec((D_MODEL, tn), lambda i, j: (0, j)), tile, tile],
        out_specs=[tile, tile, pl.BlockSpec((8, 128), lambda i, j: (i, j))],
        out_shape=[jax.ShapeDtypeStruct((s, D_MODEL), F32), jax.ShapeDtypeStruct((s, D_MODEL), BF16),
                   jax.ShapeDtypeStruct((s // tm * 8, nj * 128), F32)],
        compiler_params=_params(("parallel", "parallel")),
    )(merged, w_o, x, target)


def _merge_bwd(dyb, w_o, p_all, o_c, o_m, o_e):
    s = dyb.shape[0]
    tm = min(256, s)

    def body(dy_ref, w_ref, g_ref, oc_ref, om_ref, oe_ref, dc_ref, dm_ref, de_ref, dg_ref):
        dmg = _dot(dy_ref[...], w_ref[...], "nt")
        for b, (o_ref, d_ref) in enumerate(((oc_ref, dc_ref), (om_ref, dm_ref), (oe_ref, de_ref))):
            sl = slice(b * D_MODEL, (b + 1) * D_MODEL)
            sg = _sigmoid(g_ref[:, sl].astype(F32))
            d_ref[...] = (dmg * sg).astype(BF16)
            dg_ref[:, sl] = (dmg * o_ref[...].astype(F32) * sg * (1.0 - sg)).astype(BF16)

    row = pl.BlockSpec((tm, D_MODEL), lambda i: (i, 0))
    wide = pl.BlockSpec((tm, 3 * D_MODEL), lambda i: (i, 0))
    return pl.pallas_call(
        body, name="merge_bwd", grid=(s // tm,),
        in_specs=[row, pl.BlockSpec((D_MODEL, D_MODEL), lambda i: (0, 0)), wide, row, row, row],
        out_specs=[row, row, row, wide],
        out_shape=[jax.ShapeDtypeStruct((s, D_MODEL), BF16)] * 3 + [jax.ShapeDtypeStruct((s, 3 * D_MODEL), BF16)],
        compiler_params=_params(("parallel",)),
    )(dyb, w_o, p_all, o_c, o_m, o_e)


def _conv_bwd(da, p_all, co, conv_w):
    s = da.shape[0]
    ts = min(TS_ROW, s)
    n = s // ts
    c0 = COL_C // 1024

    def body(da_ref, cg_ref, bg_ref, u_ref, z_ref, co_ref, w_ref, dp_ref, dw_ref, carry_ref):
        @pl.when(pl.program_id(0) == 0)
        def _():
            carry_ref[...] = jnp.zeros_like(carry_ref)
            dw_ref[...] = jnp.zeros_like(dw_ref)

        da_ = da_ref[...].astype(F32)
        cg, bg = cg_ref[...].astype(F32), bg_ref[...].astype(F32)
        u, z, cov = u_ref[...].astype(F32), z_ref[...].astype(F32), co_ref[...].astype(F32)
        sz = _sigmoid(z)
        dyc = da_ * (z * sz)
        dz = da_ * (bg * cov) * (sz * (1.0 + z * (1.0 - sz)))
        db = dyc * cov
        dco = dyc * bg
        carry = carry_ref[...]
        d1 = _shift_rows(dco, carry, 1, ts, False)
        d2 = _shift_rows(dco, carry, 2, ts, False)
        carry_ref[...] = dco[:8]
        dpp = w_ref[2:3, :] * dco + w_ref[1:2, :] * d1 + w_ref[0:1, :] * d2
        p = cg * u
        dw_ref[0] += _fold8(p * d2)
        dw_ref[1] += _fold8(p * d1)
        dw_ref[2] += _fold8(p * dco)
        dp_ref[...] = jnp.concatenate([dpp * u, db, dpp * cg, dz], axis=1).astype(BF16)

    rev = lambda c: pl.BlockSpec((ts, 1024), lambda i: (n - 1 - i, c))
    return pl.pallas_call(
        body, name="conv_bwd", grid=(n,),
        in_specs=[rev(0), rev(c0), rev(c0 + 1), rev(c0 + 2), rev(c0 + 3), rev(0),
                  pl.BlockSpec((3, 1024), lambda i: (0, 0))],
        out_specs=[pl.BlockSpec((ts, 4096), lambda i: (n - 1 - i, 0)),
                   pl.BlockSpec((3, 8, 1024), lambda i: (0, 0, 0))],
        out_shape=[jax.ShapeDtypeStruct((s, 4096), BF16), jax.ShapeDtypeStruct((3, 8, 1024), F32)],
        scratch_shapes=[pltpu.VMEM((8, 1024), F32)],
        compiler_params=_params(("arbitrary",)),
    )(da, p_all, p_all, p_all, p_all, co, conv_w)


def _mla_gate_bwd(da, y, p_all):
    s = da.shape[0]
    ts = min(TS_ROW, s)

    def body(da_ref, y_ref, z_ref, dy_ref, dz_ref, dl_ref):
        da_, yv, z = da_ref[...].astype(F32), y_ref[...].astype(F32), z_ref[...].astype(F32)
        sz = _sigmoid(z)
        dyv = da_ * (z * sz)
        dy_ref[...] = dyv.astype(BF16)
        dz_ref[...] = (da_ * yv * (sz * (1.0 + z * (1.0 - sz)))).astype(BF16)
        pr = dyv * yv
        for h in range(MLA_HEADS):
            dl_ref[h] = jnp.broadcast_to(jnp.sum(pr[:, h * 128:(h + 1) * 128], axis=-1, keepdims=True), (ts, 128))

    row = pl.BlockSpec((ts, D_MODEL), lambda i: (i, 0))
    return pl.pallas_call(
        body, name="mla_gate_bwd", grid=(s // ts,),
        in_specs=[row, row, pl.BlockSpec((ts, D_MODEL), lambda i: (i, COL_Z // D_MODEL))],
        out_specs=[row, row, pl.BlockSpec((MLA_HEADS, ts, 128), lambda i: (0, i, 0))],
        out_shape=[jax.ShapeDtypeStruct((s, D_MODEL), BF16)] * 2 + [jax.ShapeDtypeStruct((MLA_HEADS, s, 128), F32)],
        compiler_params=_params(("parallel",)),
    )(da, y, p_all)


def _flash_bwd(qcat, kcat, kv, dy, lse, delta):
    s = qcat.shape[1]
    tq = tk = min(TQ_ATT, s)
    nq = s // tq
    qi, kj = _causal_pairs(nq, True)

    def body(qi_ref, kj_ref, q_ref, k_ref, v_ref, do_ref, lse_ref, dl_ref, dq_ref, dk_ref, dv_ref):
        t = pl.program_id(1)
        i, j = qi_ref[t], kj_ref[t]

        def step(masked):
            q, k = q_ref[0], k_ref[0]
            sc = _dot(q, k, "nt")
            if masked:
                sc = jnp.where(_chunk_mask(tq, tk), sc, NEG)
            p = jnp.exp2(sc - lse_ref[0][:, 0:1])
            do = do_ref[...]
            dpv = _dot(do, v_ref[...], "nt")
            ds = (p * (dpv - dl_ref[0][:, 0:1])).astype(BF16)
            pb = p.astype(BF16)
            dv_new = _dot(pb, do, "tn")
            dk_new = _dot(ds, q, "tn")
            dq_new = _dot(ds, k)
            rows = pl.ds(pl.multiple_of(i * tq, tq), tq)
            if masked:
                dv_ref[0] = dv_new
                dk_ref[0] = dk_new
            else:
                dv_ref[0] += dv_new
                dk_ref[0] += dk_new

            @pl.when(j == 0)
            def _():
                dq_ref[0, rows, :] = dq_new

            @pl.when(j > 0)
            def _():
                dq_ref[0, rows, :] += dq_new

        @pl.when(i == j)
        def _():
            step(True)

        @pl.when(i > j)
        def _():
            step(False)

    qrow = lambda w: pl.BlockSpec((1, tq, w), lambda h, t, qi_ref, kj_ref: (h, qi_ref[t], 0))
    krow = lambda w: pl.BlockSpec((1, tk, w), lambda h, t, qi_ref, kj_ref: (h, kj_ref[t], 0))
    return pl.pallas_call(
        body, name="flash_bwd",
        grid_spec=pltpu.PrefetchScalarGridSpec(
            num_scalar_prefetch=2, grid=(MLA_HEADS, qi.shape[0]),
            in_specs=[qrow(HEAD_PAD), krow(HEAD_PAD),
                      pl.BlockSpec((tk, 128), lambda h, t, qi_ref, kj_ref: (kj_ref[t], 2 * h + 1)),
                      pl.BlockSpec((tq, 128), lambda h, t, qi_ref, kj_ref: (qi_ref[t], h)),
                      qrow(128), qrow(128)],
            out_specs=[pl.BlockSpec((1, s, HEAD_PAD), lambda h, t, qi_ref, kj_ref: (h, 0, 0)),
                       krow(HEAD_PAD), krow(128)]),
        out_shape=[jax.ShapeDtypeStruct((MLA_HEADS, s, HEAD_PAD), F32),
                   jax.ShapeDtypeStruct((MLA_HEADS, s, HEAD_PAD), F32),
                   jax.ShapeDtypeStruct((MLA_HEADS, s, 128), F32)],
        compiler_params=_params(("parallel", "arbitrary")),
    )(qi, kj, qcat, kcat, kv, dy, lse, delta)


def _q_prep_bwd(dqcat, q_raw, gn, gr, cs, sg, side=None):
    s = q_raw.shape[0]
    ts = min(TS_BWD, s)

    def body(dq_ref, q_ref, gn_ref, gr_ref, cs_ref, sg_ref, o_ref, dg_ref):
        @pl.when(pl.program_id(0) == 0)
        def _():
            dg_ref[...] = jnp.zeros_like(dg_ref)

        ga_acc = jnp.zeros((8, 128), F32)
        gb_acc = jnp.zeros((8, 128), F32)
        for h in range(MLA_HEADS):
            cols = slice(h * HEAD_PAD, (h + 1) * HEAD_PAD)
            q = q_ref[:, cols].astype(F32)
            d = dq_ref[h] * MLA_SCALE
            da, ga = _rms_bwd(q[:, :128], gn_ref[...], d[:, :128], QK_NOPE)
            db, gb = _rms_bwd(q[:, 128:], gr_ref[...], _rope_bwd(d[:, 128:], cs_ref[...], sg_ref[...]), QK_ROPE)
            o_ref[:, cols] = jnp.concatenate([da, db], axis=1).astype(BF16)
            ga_acc = ga_acc + _fold8(ga)
            gb_acc = gb_acc + _fold8(gb)
        dg_ref[0] += ga_acc
        dg_ref[1] += gb_acc

    wide = pl.BlockSpec((ts, MLA_HEADS * HEAD_PAD), lambda i: (i, 0))
    return _pcall(
        body, side, "q_prep_bwd", (s // ts,),
        [pl.BlockSpec((MLA_HEADS, ts, HEAD_PAD), lambda i: (0, i, 0)), wide,
         pl.BlockSpec((1, 128), lambda i: (0, 0)), pl.BlockSpec((1, 128), lambda i: (0, 0)),
         pl.BlockSpec((ts, 128), lambda i: (i, 0)), pl.BlockSpec((ts, 128), lambda i: (i, 0))],
        [wide, pl.BlockSpec((2, 8, 128), lambda i: (0, 0, 0))],
        [jax.ShapeDtypeStruct((s, MLA_HEADS * HEAD_PAD), BF16), jax.ShapeDtypeStruct((2, 8, 128), F32)],
        (dqcat, q_raw, gn, gr, cs, sg), ("arbitrary",))


def _k_prep_bwd(dkcat, dv, kv, gk, side=None):
    s = kv.shape[0]
    ts = min(TS_BWD, s)

    def body(dk_ref, dv_ref, k_ref, gk_ref, o_ref, dkr_ref, dg_ref):
        @pl.when(pl.program_id(0) == 0)
        def _():
            dg_ref[...] = jnp.zeros_like(dg_ref)

        g_acc = jnp.zeros((8, 128), F32)
        dkr = jnp.zeros((ts, 128), F32)
        for h in range(MLA_HEADS):
            d = dk_ref[h] * LN2
            k = k_ref[:, h * HEAD_PAD:h * HEAD_PAD + 128].astype(F32)
            dk_raw, gg = _rms_bwd(k, gk_ref[...], d[:, :128], QK_NOPE)
            o_ref[:, h * HEAD_PAD:(h + 1) * HEAD_PAD] = jnp.concatenate([dk_raw, dv_ref[h]], axis=1).astype(BF16)
            g_acc = g_acc + _fold8(gg)
            dkr = dkr + d[:, 128:]
        dkr_ref[...] = dkr
        dg_ref[...] += g_acc

    wide = pl.BlockSpec((ts, MLA_HEADS * HEAD_PAD), lambda i: (i, 0))
    return _pcall(
        body, side, "k_prep_bwd", (s // ts,),
        [pl.BlockSpec((MLA_HEADS, ts, HEAD_PAD), lambda i: (0, i, 0)),
         pl.BlockSpec((MLA_HEADS, ts, 128), lambda i: (0, i, 0)), wide,
         pl.BlockSpec((1, 128), lambda i: (0, 0))],
        [wide, pl.BlockSpec((ts, 128), lambda i: (i, 0)), pl.BlockSpec((8, 128), lambda i: (0, 0))],
        [jax.ShapeDtypeStruct((s, MLA_HEADS * HEAD_PAD), BF16), jax.ShapeDtypeStruct((s, 128), F32),
         jax.ShapeDtypeStruct((8, 128), F32)],
        (dkcat, dv, kv, gk), ("arbitrary",))


def _lora_bwd(dcqn, dckvn, dkr, p_all, gq, gkv, gkr, cs, sg):
    s = p_all.shape[0]
    ts = min(TS_ROW, s)

    def body(dq_ref, dkv_ref, dkr_ref, cq_ref, ckv_ref, kr_ref, gq_ref, gkv_ref, gkr_ref, cs_ref, sg_ref,
             o_ref, dgq_ref, dgkv_ref, dgkr_ref):
        @pl.when(pl.program_id(0) == 0)
        def _():
            dgq_ref[...] = jnp.zeros_like(dgq_ref)
            dgkv_ref[...] = jnp.zeros_like(dgkv_ref)
            dgkr_ref[...] = jnp.zeros_like(dgkr_ref)

        dq, g1 = _rms_bwd(cq_ref[...].astype(F32), gq_ref[...], dq_ref[...], 512)
        dkv, g2 = _rms_bwd(ckv_ref[...].astype(F32), gkv_ref[...], dkv_ref[...], 512)
        dkn = _rope_bwd(dkr_ref[...], cs_ref[...], sg_ref[...])
        dk, g3 = _rms_bwd(kr_ref[...].astype(F32), gkr_ref[...], dkn, QK_ROPE)
        o_ref[...] = jnp.concatenate([dq, dkv, dk, jnp.zeros((ts, 2048 - 1152), F32)], axis=1).astype(BF16)
        dgq_ref[...] += _fold8(g1)
        dgkv_ref[...] += _fold8(g2)
        dgkr_ref[...] += _fold8(g3)

    row = lambda w: pl.BlockSpec((ts, w), lambda i: (i, 0))
    vec = lambda w: pl.BlockSpec((1, w), lambda i: (0, 0))
    acc = lambda w: pl.BlockSpec((8, w), lambda i: (0, 0))
    return pl.pallas_call(
        body, name="lora_bwd", grid=(s // ts,),
        in_specs=[row(512), row(512), row(128),
                  pl.BlockSpec((ts, 512), lambda i: (i, COL_S // 512)),
                  pl.BlockSpec((ts, 512), lambda i: (i, COL_S // 512 + 1)),
                  pl.BlockSpec((ts, 128), lambda i: (i, (COL_S + 1024) // 128)),
                  vec(512), vec(512), vec(128), row(128), row(128)],
        out_specs=[row(2048), acc(512), acc(512), acc(128)],
        out_shape=[jax.ShapeDtypeStruct((s, 2048), BF16), jax.ShapeDtypeStruct((8, 512), F32),
                   jax.ShapeDtypeStruct((8, 512), F32), jax.ShapeDtypeStruct((8, 128), F32)],
        compiler_params=_params(("arbitrary",)),
    )(dcqn, dckvn, dkr, p_all, p_all, p_all, gq, gkv, gkr, cs, sg)


def _mem_attn_bwd(da, p_all, kn, v, gq):
    s = p_all.shape[0]
    tq = min(TS_ROW, s)

    def body(da_ref, q_ref, z_ref, kn_ref, v_ref, gq_ref, o_ref, dkn_ref, dv_ref, dg_ref):
        @pl.when(pl.program_id(0) == 0)
        def _():
            dkn_ref[...] = jnp.zeros_like(dkn_ref)
            dv_ref[...] = jnp.zeros_like(dv_ref)
            dg_ref[...] = jnp.zeros_like(dg_ref)

        z = z_ref[...].astype(F32)
        da_ = da_ref[...].astype(F32)
        sz = _sigmoid(z)
        gate = z * sz
        dgate = sz * (1.0 + z * (1.0 - sz))
        for hh in range(MEM_HEADS):
            sl = slice(hh * MEM_HEAD_DIM, (hh + 1) * MEM_HEAD_DIM)
            qf = q_ref[:, sl].astype(F32)
            qn = _rms_fwd(qf, gq_ref[...], MEM_HEAD_DIM).astype(BF16)
            knh, vh = kn_ref[:, sl], v_ref[:, sl]
            sc = _dot(qn, knh, "nt") * MEM_SCALE
            e = jnp.exp(sc - jnp.max(sc, axis=-1, keepdims=True))
            p = e * (1.0 / jnp.sum(e, axis=-1, keepdims=True))
            pb = p.astype(BF16)
            y = _dot(pb, vh)
            dyh = da_[:, sl] * gate[:, sl]
            o_ref[:, 1024 + hh * MEM_HEAD_DIM:1024 + (hh + 1) * MEM_HEAD_DIM] = (
                da_[:, sl] * y * dgate[:, sl]).astype(BF16)
            dyb = dyh.astype(BF16)
            dpm = _dot(dyb, vh, "nt")
            ds = (p * (dpm - jnp.sum(dpm * p, axis=-1, keepdims=True)) * MEM_SCALE).astype(BF16)
            dqn = _dot(ds, knh)
            dkn_ref[:, sl] += _dot(ds, qn, "tn")
            dv_ref[:, sl] += _dot(pb, dyb, "tn")
            dq, gg = _rms_bwd(qf, gq_ref[...], dqn, MEM_HEAD_DIM)
            o_ref[:, sl] = dq.astype(BF16)
            dg_ref[...] += _fold8(gg)

    full = lambda a: pl.BlockSpec(a.shape, lambda i: (0, 0))
    return pl.pallas_call(
        body, name="mem_attn_bwd", grid=(s // tq,),
        in_specs=[pl.BlockSpec((tq, 1024), lambda i: (i, 0)),
                  pl.BlockSpec((tq, 1024), lambda i: (i, COL_M // 1024)),
                  pl.BlockSpec((tq, 1024), lambda i: (i, COL_M // 1024 + 1)),
                  full(kn), full(v), full(gq)],
        out_specs=[pl.BlockSpec((tq, 2048), lambda i: (i, 0)),
                   pl.BlockSpec((MEM_TOKENS, 1024), lambda i: (0, 0)),
                   pl.BlockSpec((MEM_TOKENS, 1024), lambda i: (0, 0)),
                   pl.BlockSpec((8, MEM_HEAD_DIM), lambda i: (0, 0))],
        out_shape=[jax.ShapeDtypeStruct((s, 2048), BF16), jax.ShapeDtypeStruct((MEM_TOKENS, 1024), F32),
                   jax.ShapeDtypeStruct((MEM_TOKENS, 1024), F32), jax.ShapeDtypeStruct((8, MEM_HEAD_DIM), F32)],
        compiler_params=_params(("arbitrary",)),
    )(da, p_all, p_all, kn, v, gq)


def _mem_kv_bwd(dkn, dv, kraw, memn, mem, g_norm, w_kv, g_kn):
    m = mem.shape[0]

    def body(dkn_ref, dv_ref, kraw_ref, memn_ref, mem_ref, g_ref, w_ref, gk_ref, dw_ref, dgn_ref, dgk_ref, dkv_sc):
        gk_acc = jnp.zeros((8, MEM_HEAD_DIM), F32)
        for hh in range(MEM_HEADS):
            sl = slice(hh * MEM_HEAD_DIM, (hh + 1) * MEM_HEAD_DIM)
            dk, gg = _rms_bwd(kraw_ref[:, sl], gk_ref[...], dkn_ref[:, sl], MEM_HEAD_DIM)
            dkv_sc[:, sl] = dk.astype(BF16)
            gk_acc = gk_acc + _fold8(gg)
        dgk_ref[...] = gk_acc
        dkv_sc[:, 1024:] = dv_ref[...].astype(BF16)
        dkv = dkv_sc[...]
        dw_ref[...] = _dot(memn_ref[...], dkv, "tn")
        dmemn = _dot(dkv, w_ref[...], "nt")
        xf = mem_ref[...]
        r = lax.rsqrt(jnp.mean(xf * xf, axis=-1, keepdims=True) + EPS)
        dgn_ref[...] = _fold8(dmemn * (xf * r))

    return pl.pallas_call(
        body, name="mem_kv_bwd",
        out_shape=[jax.ShapeDtypeStruct((D_MODEL, D_MODEL), F32), jax.ShapeDtypeStruct((8, D_MODEL), F32),
                   jax.ShapeDtypeStruct((8, MEM_HEAD_DIM), F32)],
        scratch_shapes=[pltpu.VMEM((m, D_MODEL), BF16)],
        compiler_params=_params(),
    )(dkn, dv, kraw, memn, mem, g_norm, w_kv, g_kn)


def _dh_bwd(dps, w_all_t, row0, nrows, name, side=None):
    tm, tk = min(TS_ROW, nrows), 1024
    blk0 = row0 // tm
    widths = [a.shape[1] // tk for a in dps]
    starts = [int(v) for v in np.cumsum([0] + widths[:-1])]
    nk = sum(widths)

    def body(*refs):
        d_refs, w_ref, o_ref, acc_ref = refs[:5], refs[5], refs[6], refs[7]
        kk = pl.program_id(1)

        @pl.when(kk == 0)
        def _():
            acc_ref[...] = jnp.zeros_like(acc_ref)

        for d_ref, st, wd in zip(d_refs, starts, widths):
            @pl.when((kk >= st) & (kk < st + wd))
            def _(d_ref=d_ref):
                acc_ref[...] += _dot(d_ref[...], w_ref[...])

        @pl.when(kk == nk - 1)
        def _():
            o_ref[...] = acc_ref[...]

    def dspec(st, wd):
        return pl.BlockSpec((tm, tk), lambda i, kk: (blk0 + i, jnp.clip(kk - st, 0, wd - 1)))

    outs, extra = _pcall(
        body, side, name, (nrows // tm, nk),
        [dspec(st, wd) for st, wd in zip(starts, widths)] + [pl.BlockSpec((tk, D_MODEL), lambda i, kk: (kk, 0))],
        [pl.BlockSpec((tm, D_MODEL), lambda i, kk: (i, 0))], [jax.ShapeDtypeStruct((nrows, D_MODEL), F32)],
        (*dps, w_all_t), ("parallel", "arbitrary"), scratch_shapes=[pltpu.VMEM((tm, D_MODEL), F32)])
    return outs[0], extra


def _dx_bwd(dh_a, dh_b, x, dy, g, side=None):
    s = x.shape[0]
    ts = min(TS_BWD, dh_a.shape[0])
    na, nb = dh_a.shape[0] // ts, dh_b.shape[0] // ts

    def body(dha_ref, dhb_ref, x_ref, dy_ref, g_ref, o_ref, dg_ref):
        @pl.when(pl.program_id(0) == 0)
        def _():
            dg_ref[...] = jnp.zeros_like(dg_ref)

        dh = jnp.where(pl.program_id(0) < na, dha_ref[...], dhb_ref[...])
        dx, gg = _rms_bwd(x_ref[...], g_ref[...], dh, D_MODEL)
        o_ref[...] = dy_ref[...] + dx
        dg_ref[...] += _fold8(gg)

    row = pl.BlockSpec((ts, D_MODEL), lambda i: (i, 0))
    return _pcall(
        body, side, "dx_bwd", (s // ts,),
        [pl.BlockSpec((ts, D_MODEL), lambda i: (jnp.minimum(i, na - 1), 0)),
         pl.BlockSpec((ts, D_MODEL), lambda i: (jnp.clip(i - na, 0, nb - 1), 0)),
         row, row, pl.BlockSpec((1, D_MODEL), lambda i: (0, 0))],
        [row, pl.BlockSpec((8, D_MODEL), lambda i: (0, 0))],
        [jax.ShapeDtypeStruct((s, D_MODEL), F32), jax.ShapeDtypeStruct((8, D_MODEL), F32)],
        (dh_a, dh_b, x, dy, g), ("arbitrary",))


def _pad128(v, n):
    return jnp.pad(v.reshape(1, n), ((0, 0), (0, 128 - n)))


def _local_step(x, positions, mem, target, comm, gains):
    half = QK_ROPE // 2
    inv_freq = jnp.power(ROPE_THETA, -jnp.arange(half, dtype=F32) / half)
    ang = positions.astype(F32)[:, None] * inv_freq
    cos, sin = jnp.cos(ang), jnp.sin(ang)
    zpad = jnp.zeros((x.shape[0], 64), F32)
    cs = jnp.concatenate([cos, cos, zpad], axis=1)
    sg = jnp.concatenate([-sin, sin, zpad], axis=1)
    g_qr, g_kr = _pad128(gains["mla_qn_rope_g"], 64), _pad128(gains["mla_kn_rope_g"], 64)
    g_qn, g_kn = gains["mla_qn_nope_g"], gains["mla_kn_nope_g"]

    def hosted_matmul(tag, a, b, mode, out_dtype):
        side = comm.side(tag)
        if side is None:
            return _matmul(a, b, mode, out_dtype, tag)
        out, extra = _matmul(a, b, mode, out_dtype, tag, side=side)
        comm.done(tag, extra)
        return out

    (p_all, h), extra = _proj_fwd(x, gains["norm_g"], comm.weight("w_all_t"), comm.side("proj_fwd"))
    comm.done("proj_fwd", extra)
    a_conv, co = _conv_fwd(p_all, comm.weight("conv_w"))
    (cqn, ckvn, krope), extra = _lora_fwd(p_all, gains["mla_q_norm_g"], gains["mla_kv_norm_g"], g_kr, cs, sg,
                                          comm.side("lora_fwd"))
    comm.done("lora_fwd", extra)
    q_raw = _matmul(cqn, comm.weight("w_uq"), "nn", BF16, "q_up")
    kv = _matmul(ckvn, comm.weight("w_ukv"), "nn", BF16, "kv_up")
    qcat = _q_prep(q_raw, g_qn, g_qr, cs, sg)
    kcat, vt = _k_prep(kv, krope, g_kn)
    mla_y, a_mla, lse = _flash_fwd(qcat, kcat, vt, p_all)
    memn, kraw, kn, vmem = _mem_kv_fwd(mem, gains["mem_norm_g"], comm.weight("w_mem_kv"), gains["mem_kn_g"])
    a_mem = _mem_attn_fwd(p_all, kn, vmem, gains["mem_qn_g"])
    o_c, o_m, o_e, merged = _merge_fwd(a_conv, comm.weight("w_conv_out"), a_mla, comm.weight("w_mla_out"), a_mem,
                                       comm.weight("w_mem_out"), p_all)
    dy, dyb, loss_parts = _out_fwd(merged, comm.weight("w_o"), x, target)

    comm.put("w_o", _matmul(merged, dyb, "tn", BF16, "dw_o"))
    do_c, do_m, do_e, dp_g = _merge_bwd(dyb, comm.weight("w_o"), p_all, o_c, o_m, o_e)
    comm.put("w_conv_out", _matmul(a_conv, do_c, "tn", BF16, "dw_conv_out"))
    comm.put("w_mla_out", _matmul(a_mla, do_m, "tn", BF16, "dw_mla_out"))
    comm.put("w_mem_out", _matmul(a_mem, do_e, "tn", BF16, "dw_mem_out"))
    da_conv = hosted_matmul("da_conv", do_c, comm.weight("w_conv_out"), "nt", BF16)
    da_mla = _matmul(do_m, comm.weight("w_mla_out"), "nt", BF16, "da_mla")
    da_mem = _matmul(do_e, comm.weight("w_mem_out"), "nt", BF16, "da_mem")
    dp_c, dconv_w = _conv_bwd(da_conv, p_all, co, comm.weight("conv_w"))
    dmla_y, dp_z, delta = _mla_gate_bwd(da_mla, mla_y, p_all)
    dqcat, dkcat, dv = _flash_bwd(qcat, kcat, kv, dmla_y, lse, delta)
    (dq_raw, dg_q), extra = _q_prep_bwd(dqcat, q_raw, g_qn, g_qr, cs, sg, comm.side("q_prep_bwd"))
    comm.done("q_prep_bwd", extra)
    (dkv, dkr, dg_kn), extra = _k_prep_bwd(dkcat, dv, kv, g_kn, comm.side("k_prep_bwd"))
    comm.done("k_prep_bwd", extra)
    comm.put("w_uq", _matmul(cqn, dq_raw, "tn", BF16, "dw_uq"))
    comm.put("w_ukv", _matmul(ckvn, dkv, "tn", BF16, "dw_ukv"))
    dcqn = _matmul(dq_raw, comm.weight("w_uq"), "nt", F32, "dcqn")
    dckvn = _matmul(dkv, comm.weight("w_ukv"), "nt", F32, "dckvn")
    dp_s, dg_qn, dg_kvn, dg_kr = _lora_bwd(dcqn, dckvn, dkr, p_all, gains["mla_q_norm_g"],
                                            gains["mla_kv_norm_g"], g_kr, cs, sg)
    dp_m, dkn, dvm, dg_mq = _mem_attn_bwd(da_mem, p_all, kn, vmem, gains["mem_qn_g"])
    dw_mem_kv, dg_mn, dg_mk = _mem_kv_bwd(dkn, dvm, kraw, memn, mem, gains["mem_norm_g"],
                                          comm.weight("w_mem_kv"), gains["mem_kn_g"])
    comm.put("w_mem_kv", dw_mem_kv.astype(BF16))
    dps = [dp_g, dp_z, dp_c, dp_m, dp_s]
    comm.put("w_all_t", [hosted_matmul("dw_in_%d" % k, d, h, "tn", BF16) for k, d in enumerate(dps)])
    s = x.shape[0]
    dh_a, extra = _dh_bwd(dps, comm.weight("w_all_t"), 0, s // 4, "dh_a", comm.side("dh_a"))
    comm.done("dh_a", extra)
    dh_b, extra = _dh_bwd(dps, comm.weight("w_all_t"), s // 4, s - s // 4, "dh_b", comm.side("dh_b"))
    comm.done("dh_b", extra)
    (grad_x, dg_n), extra = _dx_bwd(dh_a, dh_b, x, dy, gains["norm_g"], comm.side("dx_bwd"))
    comm.done("dx_bwd", extra)

    gsmall = {
        "norm_g": dg_n.sum(0), "conv_w": dconv_w.sum(1), "mla_q_norm_g": dg_qn.sum(0),
        "mla_kv_norm_g": dg_kvn.sum(0), "mla_qn_nope_g": dg_q[0].sum(0), "mla_qn_rope_g": dg_q[1].sum(0)[:64],
        "mla_kn_nope_g": dg_kn.sum(0), "mla_kn_rope_g": dg_kr.sum(0)[:64], "mem_norm_g": dg_mn.sum(0),
        "mem_qn_g": dg_mq.sum(0), "mem_kn_g": dg_mk.sum(0),
    }
    return loss_parts, grad_x, gsmall


def _me():
    return lax.axis_index("x"), lax.axis_index("y"), lax.axis_index("c")


def _chips(x, y):
    return [(1 - x, y), (x, 1 - y), (1 - x, 1 - y)]


def _allgather_shards(xs, name):
    r, c = xs.shape
    hc = c // 2

    def body(x_ref, out_ref, send_sems, recv_sems):
        x, y, cc = _me()
        me, sibling = (x, y, cc), (x, y, 1 - cc)
        across_x, across_y, diagonal = _chips(x, y)

        def rows(px, py, pc):
            return out_ref.at[2 * px + py, :, pl.ds(pc * hc, hc)]

        def copy(k, block, to, src=None):
            return pltpu.make_async_remote_copy(
                src_ref=rows(*block) if src is None else src, dst_ref=rows(*block),
                send_sem=send_sems.at[k], recv_sem=recv_sems.at[k], device_id=to, device_id_type=MESH)

        my_half = x_ref.at[:, pl.ds(cc * hc, hc)]
        direct = [copy(0, me, (*across_x, cc), src=my_half), copy(1, me, (*across_y, cc), src=my_half)]
        for cp in direct:
            cp.start()
        copy(0, (*across_x, cc), me).wait_recv()
        copy(1, (*across_y, cc), me).wait_recv()
        from_x = cc == 0
        came = (jnp.where(from_x, across_x[0], across_y[0]), jnp.where(from_x, across_x[1], across_y[1]), cc)
        goes = (jnp.where(from_x, across_y[0], across_x[0]), jnp.where(from_x, across_y[1], across_x[1]), cc)
        relay = copy(2, came, goes)
        relay.start()
        passed = [copy(3, (*across_x, cc), sibling), copy(4, (*across_y, cc), sibling)]
        for cp in passed:
            cp.start()
        copy(2, (*diagonal, cc), me).wait_recv()
        passed.append(copy(5, (*diagonal, cc), sibling))
        passed[2].start()
        for j, chip in enumerate((across_x, across_y, diagonal)):
            copy(3 + j, (*chip, 1 - cc), me).wait_recv()
        for cp in direct + [relay] + passed:
            cp.wait_send()

    return pl.pallas_call(
        body, name=name, in_specs=[ANY], out_specs=ANY,
        out_shape=jax.ShapeDtypeStruct((4, r, c), xs.dtype),
        scratch_shapes=[pltpu.SemaphoreType.DMA((6,)), pltpu.SemaphoreType.DMA((6,))],
    )(xs)


def _with_own(g4, own):
    xi, yi, _ = _me()
    pick = lax.broadcasted_iota(jnp.int32, (4,) + (1,) * own.ndim, 0) == 2 * xi + yi
    return jnp.where(pick, own[None], g4)


def _remote(src, dst, send_sems, recv_sems, k, to):
    return pltpu.make_async_remote_copy(src_ref=src, dst_ref=dst, send_sem=send_sems.at[k], recv_sem=recv_sems.at[k],
                                        device_id=to, device_id_type=MESH)


def _side_gather_ici(shards):
    def build(ins, outs, send_sems, recv_sems):
        x, y, cc = _me()
        pairs = []
        for a, (x_ref, o_ref) in enumerate(zip(ins, outs)):
            hr = x_ref.shape[0] // 2
            my_half = x_ref.at[pl.ds(cc * hr, hr), :]
            for j, (px, py) in enumerate(_chips(x, y)):
                mine = o_ref.at[pl.ds((4 * x + 2 * y + cc) * hr, hr), :]
                theirs = o_ref.at[pl.ds((4 * px + 2 * py + cc) * hr, hr), :]
                pairs.append((_remote(my_half, mine, send_sems, recv_sems, 3 * a + j, (px, py, cc)),
                              _remote(my_half, theirs, send_sems, recv_sems, 3 * a + j, (px, py, cc))))
        return pairs

    shapes = [jax.ShapeDtypeStruct((4 * s.shape[0], s.shape[1]), s.dtype) for s in shards]
    return _Side(shards, shapes, 3 * len(shards), build)


def _side_gather_pass(bufs):
    def build(ins, outs, send_sems, recv_sems):
        x, y, cc = _me()
        pairs = []
        for a, o_ref in enumerate(outs):
            hr = o_ref.shape[0] // 8
            for j, (px, py) in enumerate(_chips(x, y)):
                got = o_ref.at[pl.ds((4 * px + 2 * py + cc) * hr, hr), :]
                coming = o_ref.at[pl.ds((4 * px + 2 * py + 1 - cc) * hr, hr), :]
                pairs.append((_remote(got, got, send_sems, recv_sems, 3 * a + j, (x, y, 1 - cc)),
                              _remote(got, coming, send_sems, recv_sems, 3 * a + j, (x, y, 1 - cc))))
        return pairs

    shapes = [jax.ShapeDtypeStruct(b.shape, b.dtype) for b in bufs]
    return _Side(bufs, shapes, 3 * len(bufs), build, aliases={a: a for a in range(len(bufs))})


def _side_swap(srcs, parts):
    def view(s_ref, part, cc):
        if part == "lead":
            return s_ref.at[1 - cc]
        if part == "cols":
            hc = s_ref.shape[1] // 2
            return s_ref.at[:, pl.ds((1 - cc) * hc, hc)]
        return s_ref

    def build(ins, outs, send_sems, recv_sems):
        x, y, cc = _me()
        pairs = []
        for a, (s_ref, o_ref) in enumerate(zip(ins, outs)):
            cp = _remote(view(s_ref, parts[a], cc), o_ref, send_sems, recv_sems, a, (x, y, 1 - cc))
            pairs.append((cp, cp))
        return pairs

    out_shape = {"lead": lambda s: s.shape[1:], "cols": lambda s: (s.shape[0], s.shape[1] // 2), "all": lambda s: s.shape}
    shapes = [jax.ShapeDtypeStruct(out_shape[p](s), s.dtype) for s, p in zip(srcs, parts)]
    return _Side(srcs, shapes, len(srcs), build)


def _side_chips(ts):
    def build(ins, outs, send_sems, recv_sems):
        x, y, cc = _me()
        pairs = []
        for a, (t_ref, r_ref) in enumerate(zip(ins, outs)):
            for j, (px, py) in enumerate(_chips(x, y)):
                cp = _remote(t_ref.at[2 * px + py], r_ref.at[j], send_sems, recv_sems, 3 * a + j, (px, py, cc))
                pairs.append((cp, cp))
        return pairs

    shapes = [jax.ShapeDtypeStruct((3,) + t.shape[1:], t.dtype) for t in ts]
    return _Side(ts, shapes, 3 * len(ts), build)


ELEMENTWISE_BLOCK_BYTES = 2 << 20


def _tile(r, c, itemsize):
    tr, tc = r, c
    while tr * tc * itemsize > ELEMENTWISE_BLOCK_BYTES and tr % 32 == 0:
        tr //= 2
    while tr * tc * itemsize > ELEMENTWISE_BLOCK_BYTES and tc % 256 == 0:
        tc //= 2
    return tr, tc


def _add_own_half(gb, got, sel, name):
    _, _, hr, c = gb.shape
    tr, tc = _tile(hr, c, 4)

    def body(sel_ref, a_ref, b_ref, o_ref):
        o_ref[...] = (a_ref[0].astype(F32) + b_ref[...].astype(F32)).astype(BF16)

    return pl.pallas_call(
        body, name=name,
        grid_spec=pltpu.PrefetchScalarGridSpec(
            num_scalar_prefetch=1, grid=(4, hr // tr, c // tc),
            in_specs=[pl.BlockSpec((1, 1, tr, tc), lambda k, i, j, sel_ref: (sel_ref[0], k, i, j)),
                      pl.BlockSpec((1, tr, tc), lambda k, i, j, sel_ref: (k, i, j))],
            out_specs=pl.BlockSpec((1, tr, tc), lambda k, i, j, sel_ref: (k, i, j))),
        out_shape=jax.ShapeDtypeStruct(got.shape, BF16),
        compiler_params=_params(("parallel", "parallel", "parallel")),
    )(sel, gb, got)


def _add_own_cols(g, got, sel, name):
    r, c = g.shape
    hr, hc = r // 4, c // 2
    tr, tc = _tile(hr, hc, 4)
    nj = hc // tc

    def body(sel_ref, a_ref, b_ref, o_ref):
        o_ref[...] = (a_ref[...].astype(F32) + b_ref[...].astype(F32)).astype(BF16)

    t = pl.pallas_call(
        body, name=name,
        grid_spec=pltpu.PrefetchScalarGridSpec(
            num_scalar_prefetch=1, grid=(r // tr, nj),
            in_specs=[pl.BlockSpec((tr, tc), lambda i, j, sel_ref: (i, sel_ref[0] * nj + j)),
                      pl.BlockSpec((tr, tc), lambda i, j, sel_ref: (i, j))],
            out_specs=pl.BlockSpec((tr, tc), lambda i, j, sel_ref: (i, j))),
        out_shape=jax.ShapeDtypeStruct((r, hc), BF16),
        compiler_params=_params(("parallel", "parallel")),
    )(sel, g, got)
    return t.reshape(4, hr, hc)


def _sum_partials(t, rcv, sel, name):
    _, hr, c = t.shape
    tr, tc = _tile(hr, c, 4)

    def body(sel_ref, t_ref, r_ref, o_ref):
        o_ref[...] = ((t_ref[0].astype(F32) + r_ref[0].astype(F32)) + r_ref[1].astype(F32)) + r_ref[2].astype(F32)

    return pl.pallas_call(
        body, name=name,
        grid_spec=pltpu.PrefetchScalarGridSpec(
            num_scalar_prefetch=1, grid=(hr // tr, c // tc),
            in_specs=[pl.BlockSpec((1, tr, tc), lambda i, j, sel_ref: (sel_ref[0], i, j)),
                      pl.BlockSpec((3, tr, tc), lambda i, j, sel_ref: (0, i, j))],
            out_specs=pl.BlockSpec((tr, tc), lambda i, j, sel_ref: (i, j))),
        out_shape=jax.ShapeDtypeStruct((hr, c), F32),
        compiler_params=_params(("parallel", "parallel")),
    )(sel, t, rcv)


class _GroupReduce:
    def __init__(self):
        self.names, self.gb, self.t, self.fh, self.other = [], [], [], [], []

    def put(self, name, gb):
        self.names.append(name)
        self.gb.append(gb)

    def side(self, stage):
        if stage == 1:
            return _side_swap(self.gb, ["lead" if g.ndim == 4 else "cols" for g in self.gb])
        return _side_chips(self.t) if stage == 2 else _side_swap(self.fh, ["all"] * len(self.fh))

    def done(self, stage, outs):
        x, y, cc = _me()
        if stage == 1:
            sel = jnp.reshape(cc, (1,)).astype(jnp.int32)
            self.t = [(_add_own_half if gb.ndim == 4 else _add_own_cols)(gb, got, sel, "rs_add_" + n)
                      for n, gb, got in zip(self.names, self.gb, outs)]
        elif stage == 2:
            sel = jnp.reshape(2 * x + y, (1,)).astype(jnp.int32)
            self.fh = [_sum_partials(t, rcv, sel, "rs_sum_" + n) for n, t, rcv in zip(self.names, self.t, outs)]
        else:
            self.other = list(outs)

    def result(self):
        return {n: (fh, other) for n, fh, other in zip(self.names, self.fh, self.other)}


def _allreduce_small(v, tag):
    r = v.shape[0]

    def body(v_ref, out_ref, buf_ref, send_sems, recv_sems):
        x, y, cc = _me()
        me = 4 * x + 2 * y + cc
        buf_ref[pl.ds(pl.multiple_of(me * r, 8), r), :] = v_ref[...]
        peers = [(x, y, 1 - cc)] + [(px, py, pc) for (px, py) in _chips(x, y) for pc in (cc, 1 - cc)]
        cps = []
        for k, (px, py, pc) in enumerate(peers):
            mine = buf_ref.at[pl.ds(pl.multiple_of(me * r, 8), r), :]
            cps.append(pltpu.make_async_remote_copy(
                src_ref=v_ref, dst_ref=mine, send_sem=send_sems.at[k], recv_sem=recv_sems.at[k],
                device_id=(px, py, pc), device_id_type=MESH))
        for cp in cps:
            cp.start()
        for cp in cps:
            cp.wait()
        acc = buf_ref[0:r, :]
        for d in range(1, 8):
            acc = acc + buf_ref[d * r:(d + 1) * r, :]
        out_ref[...] = acc

    vm = pl.BlockSpec(memory_space=pltpu.VMEM)
    return pl.pallas_call(
        body, name="allreduce_small_" + tag, in_specs=[vm], out_specs=vm,
        out_shape=jax.ShapeDtypeStruct((r, 128), F32),
        scratch_shapes=[pltpu.VMEM((8 * r, 128), F32), pltpu.SemaphoreType.DMA((7,)), pltpu.SemaphoreType.DMA((7,))],
    )(v)


def _adamw(w, g, m, v, name):
    r, c = w.shape
    tr = r
    c1 = 1.0 / (1.0 - ADAM_B1 ** ADAM_STEP)
    c2 = 1.0 / (1.0 - ADAM_B2 ** ADAM_STEP)

    def body(w_ref, g_ref, m_ref, v_ref, d_ref, mo_ref, vo_ref):
        gg = g_ref[...]
        mn = ADAM_B1 * m_ref[...] + (1.0 - ADAM_B1) * gg
        vn = ADAM_B2 * v_ref[...] + (1.0 - ADAM_B2) * (gg * gg)
        mo_ref[...] = mn
        vo_ref[...] = vn
        d_ref[...] = -ADAM_LR * ((mn * c1) / (jnp.sqrt(vn * c2) + ADAM_EPS) + ADAM_WD * w_ref[...])

    blk = pl.BlockSpec((tr, c), lambda i: (i, 0))
    return pl.pallas_call(
        body, name=name, grid=(r // tr,), in_specs=[blk] * 4, out_specs=[blk] * 3,
        out_shape=[jax.ShapeDtypeStruct((r, c), F32)] * 3,
        compiler_params=_params(("parallel",)),
    )(w, g, m, v)


def _adamw_halves(w, g_own, g_other, m, v, sel, name, axis):
    r, c = w.shape
    tr, tc = _tile(g_own.shape[0], g_own.shape[1], 4)
    nh = (g_own.shape[axis]) // (tr, tc)[axis]
    c1 = 1.0 / (1.0 - ADAM_B1 ** ADAM_STEP)
    c2 = 1.0 / (1.0 - ADAM_B2 ** ADAM_STEP)

    def body(sel_ref, w_ref, go_ref, gx_ref, m_ref, v_ref, d_ref, mo_ref, vo_ref, g_ref):
        mine = (pl.program_id(axis) // nh) == sel_ref[0]
        gg = jnp.where(mine, go_ref[...], gx_ref[...])
        g_ref[...] = gg
        mn = ADAM_B1 * m_ref[...] + (1.0 - ADAM_B1) * gg
        vn = ADAM_B2 * v_ref[...] + (1.0 - ADAM_B2) * (gg * gg)
        mo_ref[...] = mn
        vo_ref[...] = vn
        d_ref[...] = -ADAM_LR * ((mn * c1) / (jnp.sqrt(vn * c2) + ADAM_EPS) + ADAM_WD * w_ref[...])

    blk = pl.BlockSpec((tr, tc), lambda i, j, sel_ref: (i, j))
    if axis == 0:
        half = pl.BlockSpec((tr, tc), lambda i, j, sel_ref: (i % nh, j))
    else:
        half = pl.BlockSpec((tr, tc), lambda i, j, sel_ref: (i, j % nh))
    return pl.pallas_call(
        body, name=name,
        grid_spec=pltpu.PrefetchScalarGridSpec(
            num_scalar_prefetch=1, grid=(r // tr, c // tc),
            in_specs=[blk, half, half, blk, blk], out_specs=[blk] * 4),
        out_shape=[jax.ShapeDtypeStruct((r, c), F32)] * 4,
        compiler_params=_params(("parallel", "parallel")),
    )(sel, w, g_own, g_other, m, v)


WEIGHTS = ['norm_g', 'w_in', 'conv_w', 'w_conv_out', 'mla_q_norm_g', 'w_uq', 'mla_kv_norm_g', 'w_ukv',
           'mla_qn_nope_g', 'mla_qn_rope_g', 'mla_kn_nope_g', 'mla_kn_rope_g', 'w_mla_out', 'mem_norm_g',
           'w_mem_kv', 'mem_qn_g', 'mem_kn_g', 'w_mem_out', 'w_o']
COL_SHARDED = ['w_in', 'w_conv_out', 'w_uq', 'w_ukv', 'w_mem_out']
ROW_SHARDED = ['w_mla_out', 'w_mem_kv', 'w_o']
SMALL = ['norm_g', 'conv_w', 'mla_q_norm_g', 'mla_kv_norm_g', 'mla_qn_nope_g', 'mla_qn_rope_g', 'mla_kn_nope_g',
         'mla_kn_rope_g', 'mem_norm_g', 'mem_qn_g', 'mem_kn_g']
SMALL_SIZES = [2048, 3072, 512, 512, 128, 64, 128, 64, 2048, 256, 256]
PACK_ROWS = 72


def _full_from_shards(name, g4):
    if name in COL_SHARDED:
        return g4.transpose(1, 0, 2).reshape(g4.shape[1], 4 * g4.shape[2])
    return g4.reshape(4 * g4.shape[1], g4.shape[2])


def _w_all_t_from_w_in_t(w_in_t):
    conv, small, mz, memq, gates = (w_in_t[0:4096], w_in_t[4096:5184], w_in_t[5184:7232],
                                    w_in_t[7232:9280], w_in_t[9280:15424])
    return jnp.concatenate([gates, mz, conv, memq, small, jnp.zeros((2048 - 1088, D_MODEL), BF16)], axis=0)


def _pad_w_uq(w_uq):
    return jnp.pad(w_uq.reshape(512, MLA_HEADS, 192), ((0, 0), (0, 0), (0, 64))).reshape(512, 4096)


def _grad_halves_by_owner(name, g):
    if name == "w_all_t":
        dw_g, dw_z, dw_c, dw_m, dw_s = g
        return "w_in", jnp.concatenate([dw_c, dw_s[:1088], dw_z, dw_m, dw_g], axis=0)
    if name == "w_uq":
        g = g.reshape(512, MLA_HEADS, HEAD_PAD)[:, :, :192].reshape(512, 3072)
    r, c = g.shape
    if name in COL_SHARDED:
        return name, g.reshape(2, r // 2, 4, c // 4).transpose(0, 2, 1, 3)
    return name, g.reshape(4, 2, r // 8, c).transpose(1, 0, 2, 3)


LATE_WEIGHTS = ['w_conv_out', 'w_uq', 'w_ukv', 'w_mem_out', 'w_mla_out', 'w_mem_kv', 'w_o']
HOSTS = {"proj_fwd": ("gather", 1), "lora_fwd": ("gather", 2),
         "da_conv": ("g1", 1), "q_prep_bwd": ("g1", 2), "k_prep_bwd": ("g1", 3),
         "dw_in_0": ("g2", 1), "dw_in_2": ("g2", 2), "dw_in_4": ("g2", 3),
         "dh_a": ("g3", 1), "dh_b": ("g3", 2), "dx_bwd": ("g3", 3)}
GROUP_OF = {"w_o": "g1", "w_conv_out": "g1", "w_mla_out": "g1", "w_mem_out": "g1",
            "w_uq": "g2", "w_ukv": "g2", "w_mem_kv": "g2", "w_in": "g3"}


class _Comm:
    def __init__(self, shards, w_in_t_full, conv_full):
        self.shards = shards
        self.w = {"w_all_t": _w_all_t_from_w_in_t(w_in_t_full), "conv_w": conv_full}
        self.bufs = None
        self.groups = {"g1": _GroupReduce(), "g2": _GroupReduce(), "g3": _GroupReduce()}

    def weight(self, name):
        return self.w[name]

    def side(self, tag):
        if tag not in HOSTS:
            return None
        kind, stage = HOSTS[tag]
        if kind == "gather":
            return _side_gather_ici([self.shards[n] for n in LATE_WEIGHTS]) if stage == 1 else \
                _side_gather_pass(self.bufs)
        return self.groups[kind].side(stage)

    def done(self, tag, outs):
        kind, stage = HOSTS[tag]
        if kind != "gather":
            self.groups[kind].done(stage, outs)
        elif stage == 1:
            self.bufs = list(outs)
        else:
            for n, buf in zip(LATE_WEIGHTS, outs):
                own = self.shards[n]
                self.w[n] = _full_from_shards(n, _with_own(buf.reshape((4,) + own.shape), own))
            self.w["w_uq"] = _pad_w_uq(self.w["w_uq"])

    def put(self, name, g):
        name, gb = _grad_halves_by_owner(name, g)
        self.groups[GROUP_OF[name]].put(name, gb)

    def reduced(self):
        out = {}
        for grp in self.groups.values():
            out.update(grp.result())
        return out


def kernel(x, positions, mem, norm_g, w_in, conv_w, w_conv_out, mla_q_norm_g, w_uq, mla_kv_norm_g, w_ukv, mla_qn_nope_g, mla_qn_rope_g, mla_kn_nope_g, mla_kn_rope_g, w_mla_out, mem_norm_g, w_mem_kv, mem_qn_g, mem_kn_g, w_mem_out, w_o, loss_target, m_norm_g, m_w_in, m_conv_w, m_w_conv_out, m_mla_q_norm_g, m_w_uq, m_mla_kv_norm_g, m_w_ukv, m_mla_qn_nope_g, m_mla_qn_rope_g, m_mla_kn_nope_g, m_mla_kn_rope_g, m_w_mla_out, m_mem_norm_g, m_w_mem_kv, m_mem_qn_g, m_mem_kn_g, m_w_mem_out, m_w_o, v_norm_g, v_w_in, v_conv_w, v_w_conv_out, v_mla_q_norm_g, v_w_uq, v_mla_kv_norm_g, v_w_ukv, v_mla_qn_nope_g, v_mla_qn_rope_g, v_mla_kn_nope_g, v_mla_kn_rope_g, v_w_mla_out, v_mem_norm_g, v_w_mem_kv, v_mem_qn_g, v_mem_kn_g, v_w_mem_out, v_w_o):
    args = locals()
    w = {n: args[n][0] for n in WEIGHTS}
    m1 = {n: args["m_" + n][0] for n in WEIGHTS}
    v2 = {n: args["v_" + n][0] for n in WEIGHTS}
    xi, yi, ci = _me()
    chip = 2 * xi + yi

    conv_slot = jnp.zeros((3, 4, 256), F32).at[:, chip, :].set(w["conv_w"] * jnp.where(ci == 0, 1.0, 0.0))
    pre = jnp.zeros((PACK_ROWS * 128,), F32).at[0:3072].set(conv_slot.reshape(3072))
    conv_full = _allreduce_small(pre.reshape(PACK_ROWS, 128), "conv_w").reshape(-1)[0:3072].reshape(3, 1024)

    w_in_t = w["w_in"].T.astype(BF16)
    w_in_t_full = _with_own(_allgather_shards(w_in_t, "ag_w_in"), w_in_t).reshape(4 * w_in_t.shape[0], D_MODEL)
    comm = _Comm({n: w[n].astype(BF16) for n in LATE_WEIGHTS}, w_in_t_full, conv_full)
    gains = {n: w[n].reshape(1, -1) for n in SMALL if n != "conv_w"}

    loss_parts, grad_x, gsmall = _local_step(x[0], positions[0], mem[0], loss_target[0], comm, gains)

    loss_local = 0.5 * jnp.sum(loss_parts) * (1.0 / D_MODEL)
    flat = jnp.concatenate([gsmall[n].reshape(-1) for n in SMALL] + [loss_local.reshape(1)])
    flat = jnp.pad(flat, (0, PACK_ROWS * 128 - flat.shape[0]))
    tot = _allreduce_small(flat.reshape(PACK_ROWS, 128), "grads").reshape(-1)
    grads = {}
    off = 0
    for n, sz in zip(SMALL, SMALL_SIZES):
        grads[n] = tot[off:off + sz]
        off += sz
    loss = tot[off]
    grads["conv_w"] = lax.dynamic_slice(grads["conv_w"].reshape(3, 1024), (0, chip * 256), (3, 256))
    for n in SMALL:
        grads[n] = grads[n].reshape(w[n].shape)

    deltas, new_m, new_v = {}, {}, {}
    for n in SMALL:
        shp = w[n].shape
        two_d = (lambda a: a.reshape(1, -1)) if len(shp) == 1 else (lambda a: a)
        d, mn, vn = _adamw(two_d(w[n]), two_d(grads[n]), two_d(m1[n]), two_d(v2[n]), "adamw_" + n)
        deltas[n], new_m[n], new_v[n] = d.reshape(shp), mn.reshape(shp), vn.reshape(shp)

    sel_c = jnp.reshape(ci, (1,)).astype(jnp.int32)
    reduced = comm.reduced()
    for n in COL_SHARDED + ROW_SHARDED:
        g_own, g_other = reduced[n]
        if n == "w_in":
            outs = _adamw_halves(w[n].T, g_own, g_other, m1[n].T, v2[n].T, sel_c, "adamw_" + n, 1)
            deltas[n], new_m[n], new_v[n], grads[n] = [o.T for o in outs]
        else:
            deltas[n], new_m[n], new_v[n], grads[n] = _adamw_halves(w[n], g_own, g_other, m1[n], v2[n], sel_c,
                                                                     "adamw_" + n, 0)

    lead = lambda a: a[None]
    return (loss, grad_x[None], *[lead(grads[n]) for n in WEIGHTS], *[lead(deltas[n]) for n in WEIGHTS],
            *[lead(new_m[n]) for n in WEIGHTS], *[lead(new_v[n]) for n in WEIGHTS])
```

```python
import functools

import numpy as np
import jax
import jax.numpy as jnp
from jax import lax
from jax.experimental import pallas as pl
from jax.experimental.pallas import tpu as pltpu

F32 = jnp.float32
BF16 = jnp.bfloat16
MESH = pl.DeviceIdType.MESH

D_MODEL = 2048
EPS = 1e-6
CHUNK = 64
MLA_HEADS = 16
QK_NOPE = 128
QK_ROPE = 64
HEAD_PAD = 256
MEM_TOKENS = 256
MEM_HEADS = 4
MEM_HEAD_DIM = 256
ROPE_THETA = 10000.0
MLA_SCALE = (QK_NOPE + QK_ROPE) ** -0.5
MEM_SCALE = MEM_HEAD_DIM ** -0.5
LN2 = 0.6931471805599453
Q_SCALE = MLA_SCALE / LN2
NEG = -1e30

ADAM_LR = 0.001
ADAM_B1 = 0.9
ADAM_B2 = 0.999
ADAM_EPS = 1e-08
ADAM_WD = 0.01
ADAM_STEP = 10

NP = 16384
COL_G, COL_Z, COL_C, COL_M, COL_S = 0, 6144, 8192, 12288, 14336

TM_PROJ = 1024
TN_PROJ = 1024
TS_ROW = 512
TS_BWD = 256
TQ_ATT = 1024
TK_ATT = 1024
TR_ATT = 256
VMEM_LIMIT = 56 * 1024 * 1024

NT_DIMS = (((1,), (1,)), ((), ()))
TN_DIMS = (((0,), (0,)), ((), ()))


def _dot(a, b, mode="nn"):
    if mode == "nn":
        return jnp.dot(a, b, preferred_element_type=F32)
    return lax.dot_general(a, b, NT_DIMS if mode == "nt" else TN_DIMS, preferred_element_type=F32)


def _sigmoid(z):
    return 1.0 / (1.0 + jnp.exp(-z))


def _params(sem=None, vmem=VMEM_LIMIT):
    return pltpu.CompilerParams(dimension_semantics=sem, vmem_limit_bytes=vmem)


def _fold8(v):
    r, c = v.shape
    return v.reshape(r // 8, 8, c).sum(axis=0)


def _swap32(t):
    lane = lax.broadcasted_iota(jnp.int32, t.shape, 1)
    return jnp.where(lane < 32, pltpu.roll(t, 96, 1), pltpu.roll(t, 32, 1))


def _rope(t, cs, sg):
    return t * cs + _swap32(t) * sg


def _rope_bwd(d, cs, sg):
    return d * cs + _swap32(d * sg)


def _rms_fwd(xf, g, n):
    r = lax.rsqrt(jnp.sum(xf * xf, axis=-1, keepdims=True) * (1.0 / n) + EPS)
    return xf * r * g


def _rms_bwd(xf, g, dy, n):
    r = lax.rsqrt(jnp.sum(xf * xf, axis=-1, keepdims=True) * (1.0 / n) + EPS)
    xhat = xf * r
    dyg = dy * g
    dx = r * (dyg - xhat * (jnp.sum(dyg * xhat, axis=-1, keepdims=True) * (1.0 / n)))
    return dx, dy * xhat


ANY = pl.BlockSpec(memory_space=pl.ANY)


class _Side:
    def __init__(self, inputs, out_shapes, n_copies, build, aliases=()):
        self.inputs, self.out_shapes, self.n_copies = list(inputs), list(out_shapes), n_copies
        self.build, self.aliases = build, dict(aliases)


def _pcall(body, side, name, grid, in_specs, out_specs, out_shape, args, sem, scratch_shapes=(), prefetch=()):
    n_pre, n_in, n_out = len(prefetch), len(in_specs), len(out_specs)
    if side is None:
        outs = pl.pallas_call(
            body, name=name,
            grid_spec=pltpu.PrefetchScalarGridSpec(num_scalar_prefetch=n_pre, grid=grid, in_specs=list(in_specs),
                                                   out_specs=list(out_specs), scratch_shapes=list(scratch_shapes)),
            out_shape=list(out_shape), compiler_params=_params(sem),
        )(*prefetch, *args)
        return list(outs), []
    ns_in, ns_out = len(side.inputs), len(side.out_shapes)

    def wrapped(*refs):
        pre, r = refs[:n_pre], refs[n_pre:]
        ins, s_ins = r[:n_in], r[n_in:n_in + ns_in]
        outs = r[n_in + ns_in:n_in + ns_in + n_out]
        s_outs = r[n_in + ns_in + n_out:n_in + ns_in + n_out + ns_out]
        scr, send_sems, recv_sems = r[n_in + ns_in + n_out + ns_out:-2], r[-2], r[-1]
        pairs = side.build(s_ins, s_outs, send_sems, recv_sems)
        first = pl.program_id(0) == 0
        last = pl.program_id(0) == grid[0] - 1
        for d in range(1, len(grid)):
            first = first & (pl.program_id(d) == 0)
            last = last & (pl.program_id(d) == grid[d] - 1)

        @pl.when(first)
        def _():
            for send, _ in pairs:
                send.start()

        body(*pre, *ins, *outs, *scr)

        @pl.when(last)
        def _():
            for send, recv in pairs:
                send.wait_send()
                recv.wait_recv()

    outs = pl.pallas_call(
        wrapped, name=name,
        grid_spec=pltpu.PrefetchScalarGridSpec(
            num_scalar_prefetch=n_pre, grid=grid, in_specs=list(in_specs) + [ANY] * ns_in,
            out_specs=list(out_specs) + [ANY] * ns_out,
            scratch_shapes=list(scratch_shapes) + [pltpu.SemaphoreType.DMA((side.n_copies,)),
                                                   pltpu.SemaphoreType.DMA((side.n_copies,))]),
        out_shape=list(out_shape) + side.out_shapes,
        input_output_aliases={n_pre + n_in + i: n_out + o for i, o in side.aliases.items()},
        compiler_params=_params(("arbitrary",) * len(grid)),
    )(*prefetch, *args, *side.inputs)
    return list(outs[:n_out]), list(outs[n_out:])


def _matmul(a, b, mode, out_dtype, name, tm=1024, tn=1024, tk=1024, side=None):
    if mode == "nn":
        (m, k), (_, n) = a.shape, b.shape
    elif mode == "nt":
        (m, k), (n, _) = a.shape, b.shape
    else:
        (k, m), (_, n) = a.shape, b.shape
    tm, tn, tk = min(tm, m), min(tn, n), min(tk, k)
    nk = k // tk
    if mode == "tn":
        a_spec = pl.BlockSpec((tk, tm), lambda i, j, kk: (kk, i))
    else:
        a_spec = pl.BlockSpec((tm, tk), lambda i, j, kk: (i, kk))
    if mode == "nt":
        b_spec = pl.BlockSpec((tn, tk), lambda i, j, kk: (j, kk))
    else:
        b_spec = pl.BlockSpec((tk, tn), lambda i, j, kk: (kk, j))

    def body(a_ref, b_ref, o_ref, acc_ref):
        kk = pl.program_id(2)

        @pl.when(kk == 0)
        def _():
            acc_ref[...] = jnp.zeros_like(acc_ref)

        acc_ref[...] += _dot(a_ref[...].astype(BF16), b_ref[...].astype(BF16), mode)

        @pl.when(kk == nk - 1)
        def _():
            o_ref[...] = acc_ref[...].astype(out_dtype)

    outs, extra = _pcall(
        body, side, name, (m // tm, n // tn, nk), [a_spec, b_spec],
        [pl.BlockSpec((tm, tn), lambda i, j, kk: (i, j))], [jax.ShapeDtypeStruct((m, n), out_dtype)], (a, b),
        ("parallel", "parallel", "arbitrary"), scratch_shapes=[pltpu.VMEM((tm, tn), F32)])
    return outs[0] if side is None else (outs[0], extra)


def _proj_fwd(x, g, w_all_t, side=None):
    s = x.shape[0]
    tm, tn = min(TM_PROJ, s), TN_PROJ

    def body(x_ref, g_ref, w_ref, p_ref, h_ref):
        @pl.when(pl.program_id(1) == 0)
        def _():
            h_ref[...] = _rms_fwd(x_ref[...], g_ref[...], D_MODEL).astype(BF16)

        p_ref[...] = _dot(h_ref[...], w_ref[...], "nt").astype(BF16)

    return _pcall(
        body, side, "proj_fwd", (s // tm, NP // tn),
        [pl.BlockSpec((tm, D_MODEL), lambda i, j: (i, 0)),
         pl.BlockSpec((1, D_MODEL), lambda i, j: (0, 0)),
         pl.BlockSpec((tn, D_MODEL), lambda i, j: (j, 0))],
        [pl.BlockSpec((tm, tn), lambda i, j: (i, j)), pl.BlockSpec((tm, D_MODEL), lambda i, j: (i, 0))],
        [jax.ShapeDtypeStruct((s, NP), BF16), jax.ShapeDtypeStruct((s, D_MODEL), BF16)], (x, g, w_all_t),
        ("parallel", "arbitrary"))


def _shift_rows(v, carry, j, ts, back):
    if back:
        main = pltpu.roll(v, j, 0)
        edge = pltpu.roll(jnp.concatenate([carry, v[:8]], axis=0), j, 0)[8:]
        return jnp.concatenate([edge, main[8:]], axis=0)
    main = pltpu.roll(v, ts - j, 0)
    edge = pltpu.roll(jnp.concatenate([v[ts - 8:], carry], axis=0), 16 - j, 0)[:8]
    return jnp.concatenate([main[:ts - 8], edge], axis=0)


def _conv_fwd(p_all, conv_w):
    s = p_all.shape[0]
    ts = min(TS_ROW, s)
    c0 = COL_C // 1024

    def body(cg_ref, bg_ref, u_ref, z_ref, w_ref, a_ref, co_ref, carry_ref):
        @pl.when(pl.program_id(0) == 0)
        def _():
            carry_ref[...] = jnp.zeros_like(carry_ref)

        p = cg_ref[...].astype(F32) * u_ref[...].astype(F32)
        carry = carry_ref[...]
        co = (w_ref[2:3, :] * p + w_ref[1:2, :] * _shift_rows(p, carry, 1, ts, True)
              + w_ref[0:1, :] * _shift_rows(p, carry, 2, ts, True))
        carry_ref[...] = p[ts - 8:]
        z = z_ref[...].astype(F32)
        a_ref[...] = (bg_ref[...].astype(F32) * co * (z * _sigmoid(z))).astype(BF16)
        co_ref[...] = co.astype(BF16)

    seg = lambda c: pl.BlockSpec((ts, 1024), lambda i: (i, c0 + c))
    return pl.pallas_call(
        body, name="conv_fwd", grid=(s // ts,),
        in_specs=[seg(0), seg(1), seg(2), seg(3), pl.BlockSpec((3, 1024), lambda i: (0, 0))],
        out_specs=[pl.BlockSpec((ts, 1024), lambda i: (i, 0))] * 2,
        out_shape=[jax.ShapeDtypeStruct((s, 1024), BF16)] * 2,
        scratch_shapes=[pltpu.VMEM((8, 1024), F32)],
        compiler_params=_params(("arbitrary",)),
    )(p_all, p_all, p_all, p_all, conv_w)


def _lora_fwd(p_all, gq, gkv, gkr, cs, sg, side=None):
    s = p_all.shape[0]
    ts = min(TS_ROW, s)

    def body(cq_ref, ckv_ref, kr_ref, gq_ref, gkv_ref, gkr_ref, cs_ref, sg_ref, cqn_ref, ckvn_ref, krope_ref):
        cqn_ref[...] = _rms_fwd(cq_ref[...].astype(F32), gq_ref[...], 512).astype(BF16)
        ckvn_ref[...] = _rms_fwd(ckv_ref[...].astype(F32), gkv_ref[...], 512).astype(BF16)
        kn = _rms_fwd(kr_ref[...].astype(F32), gkr_ref[...], QK_ROPE)
        krope_ref[...] = _rope(kn, cs_ref[...], sg_ref[...]).astype(BF16)

    row = lambda w: pl.BlockSpec((ts, w), lambda i: (i, 0))
    vec = lambda w: pl.BlockSpec((1, w), lambda i: (0, 0))
    return _pcall(
        body, side, "lora_fwd", (s // ts,),
        [pl.BlockSpec((ts, 512), lambda i: (i, COL_S // 512)),
         pl.BlockSpec((ts, 512), lambda i: (i, COL_S // 512 + 1)),
         pl.BlockSpec((ts, 128), lambda i: (i, (COL_S + 1024) // 128)),
         vec(512), vec(512), vec(128), row(128), row(128)],
        [row(512), row(512), row(128)],
        [jax.ShapeDtypeStruct((s, 512), BF16), jax.ShapeDtypeStruct((s, 512), BF16),
         jax.ShapeDtypeStruct((s, 128), BF16)],
        (p_all, p_all, p_all, gq, gkv, gkr, cs, sg), ("parallel",))


def _q_prep(q_raw, gn, gr, cs, sg):
    s = q_raw.shape[0]
    ts = min(TS_ROW, s)

    def body(q_ref, gn_ref, gr_ref, cs_ref, sg_ref, o_ref):
        for h in range(MLA_HEADS):
            q = q_ref[:, h * HEAD_PAD:(h + 1) * HEAD_PAD].astype(F32)
            a = _rms_fwd(q[:, :128], gn_ref[...], QK_NOPE)
            b = _rope(_rms_fwd(q[:, 128:], gr_ref[...], QK_ROPE), cs_ref[...], sg_ref[...])
            o_ref[h] = (jnp.concatenate([a, b], axis=1) * Q_SCALE).astype(BF16)

    return pl.pallas_call(
        body, name="q_prep", grid=(s // ts,),
        in_specs=[pl.BlockSpec((ts, MLA_HEADS * HEAD_PAD), lambda i: (i, 0)),
                  pl.BlockSpec((1, 128), lambda i: (0, 0)), pl.BlockSpec((1, 128), lambda i: (0, 0)),
                  pl.BlockSpec((ts, 128), lambda i: (i, 0)), pl.BlockSpec((ts, 128), lambda i: (i, 0))],
        out_specs=pl.BlockSpec((MLA_HEADS, ts, HEAD_PAD), lambda i: (0, i, 0)),
        out_shape=jax.ShapeDtypeStruct((MLA_HEADS, s, HEAD_PAD), BF16),
        compiler_params=_params(("parallel",)),
    )(q_raw, gn, gr, cs, sg)


def _k_prep(kv, krope, gk):
    s = kv.shape[0]
    ts = min(TS_ROW, s)

    def body(k_ref, kr_ref, gk_ref, o_ref, vt_ref):
        for h in range(MLA_HEADS):
            a = _rms_fwd(k_ref[:, h * HEAD_PAD:h * HEAD_PAD + 128].astype(F32), gk_ref[...], QK_NOPE)
            o_ref[h] = jnp.concatenate([a.astype(BF16), kr_ref[...]], axis=1)
            vt_ref[h] = k_ref[:, h * HEAD_PAD + 128:(h + 1) * HEAD_PAD].astype(F32).T.astype(BF16)

    return pl.pallas_call(
        body, name="k_prep", grid=(s // ts,),
        in_specs=[pl.BlockSpec((ts, MLA_HEADS * HEAD_PAD), lambda i: (i, 0)),
                  pl.BlockSpec((ts, 128), lambda i: (i, 0)),
                  pl.BlockSpec((1, 128), lambda i: (0, 0))],
        out_specs=[pl.BlockSpec((MLA_HEADS, ts, HEAD_PAD), lambda i: (0, i, 0)),
                   pl.BlockSpec((MLA_HEADS, 128, ts), lambda i: (0, 0, i))],
        out_shape=[jax.ShapeDtypeStruct((MLA_HEADS, s, HEAD_PAD), BF16),
                   jax.ShapeDtypeStruct((MLA_HEADS, 128, s), BF16)],
        compiler_params=_params(("parallel",)),
    )(kv, krope, gk)


def _chunk_mask(tq, tk, row0=0):
    r = (lax.broadcasted_iota(jnp.int32, (tq, tk), 0) + row0) // CHUNK
    c = lax.broadcasted_iota(jnp.int32, (tq, tk), 1) // CHUNK
    return c <= r


def _causal_pairs(n, by_key):
    if by_key:
        pairs = [(i, j) for j in range(n) for i in range(j, n)]
    else:
        pairs = [(i, j) for i in range(n) for j in range(i + 1)]
    return (jnp.asarray([p[0] for p in pairs], jnp.int32), jnp.asarray([p[1] for p in pairs], jnp.int32))


def _flash_fwd(qcat, kcat, vt, p_all):
    s = qcat.shape[1]
    tq = tk = min(TQ_ATT, s)
    tc = min(TR_ATT, tq)
    nq = s // tq
    zc = COL_Z // 128
    qi, kj = _causal_pairs(nq, False)

    def body(qi_ref, kj_ref, q_ref, k_ref, vt_ref, z_ref, y_ref, a_ref, lse_ref, m_sc, l_sc, acc_sc):
        t = pl.program_id(1)
        i, j = qi_ref[t], kj_ref[t]

        @pl.when(j == 0)
        def _():
            m_sc[...] = jnp.full_like(m_sc, NEG)
            l_sc[...] = jnp.zeros_like(l_sc)
            acc_sc[...] = jnp.zeros_like(acc_sc)

        def step(masked):
            nc = tq // tc
            state = [[m_sc[:, c * tc:(c + 1) * tc], l_sc[:, c * tc:(c + 1) * tc], acc_sc[:, c * tc:(c + 1) * tc]]
                     for c in range(nc)]
            units = [(u, c) for u in range(tk // tc) for c in range(nc) if (u <= c or not masked)]

            def scores(u, c):
                return _dot(k_ref[0, u * tc:(u + 1) * tc, :], q_ref[0, c * tc:(c + 1) * tc, :], "nt")

            def weighted_values(u, c, alpha, pb):
                state[c][2] = alpha * state[c][2] + _dot(vt_ref[0, :, u * tc:(u + 1) * tc], pb)

            ahead = 4
            pending = [scores(*un) for un in units[:ahead]]
            late = None
            for n, (u, c) in enumerate(units):
                st = pending.pop(0)
                if n + ahead < len(units):
                    pending.append(scores(*units[n + ahead]))
                if masked and u == c:
                    kc = lax.broadcasted_iota(jnp.int32, (tc, tc), 0) // CHUNK
                    qc = lax.broadcasted_iota(jnp.int32, (tc, tc), 1) // CHUNK
                    st = jnp.where(kc <= qc, st, NEG)
                m_run, l_run, _ = state[c]
                m_new = jnp.maximum(m_run, jnp.max(st, axis=0, keepdims=True))
                alpha = jnp.exp2(m_run - m_new)
                p = jnp.exp2(st - m_new)
                state[c][0] = m_new
                state[c][1] = alpha * l_run + jnp.sum(p, axis=0, keepdims=True)
                if late is not None:
                    weighted_values(*late)
                late = (u, c, alpha, p.astype(BF16))
            weighted_values(*late)
            m_sc[...] = jnp.concatenate([s_[0] for s_ in state], axis=1)
            l_sc[...] = jnp.concatenate([s_[1] for s_ in state], axis=1)
            acc_sc[...] = jnp.concatenate([s_[2] for s_ in state], axis=1)

        @pl.when(j < i)
        def _():
            step(False)

        @pl.when(j == i)
        def _():
            step(True)
            y = (acc_sc[...] * (1.0 / l_sc[...])).T
            z = z_ref[...].astype(F32)
            y_ref[...] = y.astype(BF16)
            a_ref[...] = (y * (z * _sigmoid(z))).astype(BF16)
            lse2 = m_sc[...] + jnp.log(l_sc[...]) * (1.0 / LN2)
            lse_ref[0] = jnp.broadcast_to(lse2, (128, tq)).T

    return pl.pallas_call(
        body, name="flash_fwd",
        grid_spec=pltpu.PrefetchScalarGridSpec(
            num_scalar_prefetch=2, grid=(MLA_HEADS, qi.shape[0]),
            in_specs=[pl.BlockSpec((1, tq, HEAD_PAD), lambda h, t, qi_ref, kj_ref: (h, qi_ref[t], 0)),
                      pl.BlockSpec((1, tk, HEAD_PAD), lambda h, t, qi_ref, kj_ref: (h, kj_ref[t], 0)),
                      pl.BlockSpec((1, 128, tk), lambda h, t, qi_ref, kj_ref: (h, 0, kj_ref[t])),
                      pl.BlockSpec((tq, 128), lambda h, t, qi_ref, kj_ref: (qi_ref[t], zc + h))],
            out_specs=[pl.BlockSpec((tq, 128), lambda h, t, qi_ref, kj_ref: (qi_ref[t], h)),
                       pl.BlockSpec((tq, 128), lambda h, t, qi_ref, kj_ref: (qi_ref[t], h)),
                       pl.BlockSpec((1, tq, 128), lambda h, t, qi_ref, kj_ref: (h, qi_ref[t], 0))],
            scratch_shapes=[pltpu.VMEM((1, tq), F32), pltpu.VMEM((1, tq), F32), pltpu.VMEM((128, tq), F32)]),
        out_shape=[jax.ShapeDtypeStruct((s, MLA_HEADS * 128), BF16),
                   jax.ShapeDtypeStruct((s, MLA_HEADS * 128), BF16),
                   jax.ShapeDtypeStruct((MLA_HEADS, s, 128), F32)],
        compiler_params=_params(("parallel", "arbitrary")),
    )(qi, kj, qcat, kcat, vt, p_all)


def _mem_kv_fwd(mem, g_norm, w_kv, g_kn):
    m = mem.shape[0]

    def body(mem_ref, g_ref, w_ref, gk_ref, memn_ref, kraw_ref, kn_ref, v_ref):
        memn = _rms_fwd(mem_ref[...], g_ref[...], D_MODEL).astype(BF16)
        memn_ref[...] = memn
        kvm = _dot(memn, w_ref[...])
        kraw_ref[...] = kvm[:, :1024]
        v_ref[...] = kvm[:, 1024:].astype(BF16)
        for hh in range(MEM_HEADS):
            sl = slice(hh * MEM_HEAD_DIM, (hh + 1) * MEM_HEAD_DIM)
            kn_ref[:, sl] = _rms_fwd(kvm[:, sl], gk_ref[...], MEM_HEAD_DIM).astype(BF16)

    return pl.pallas_call(
        body, name="mem_kv_fwd",
        out_shape=[jax.ShapeDtypeStruct((m, D_MODEL), BF16), jax.ShapeDtypeStruct((m, 1024), F32),
                   jax.ShapeDtypeStruct((m, 1024), BF16), jax.ShapeDtypeStruct((m, 1024), BF16)],
        compiler_params=_params(),
    )(mem, g_norm, w_kv, g_kn)


def _mem_attn_fwd(p_all, kn, v, gq):
    s = p_all.shape[0]
    tq = min(TS_ROW, s)

    def body(q_ref, z_ref, kn_ref, v_ref, gq_ref, a_ref):
        z = z_ref[...].astype(F32)
        gate = z * _sigmoid(z)
        for hh in range(MEM_HEADS):
            sl = slice(hh * MEM_HEAD_DIM, (hh + 1) * MEM_HEAD_DIM)
            qn = _rms_fwd(q_ref[:, sl].astype(F32), gq_ref[...], MEM_HEAD_DIM).astype(BF16)
            sc = _dot(qn, kn_ref[:, sl], "nt") * MEM_SCALE
            e = jnp.exp(sc - jnp.max(sc, axis=-1, keepdims=True))
            p = e * (1.0 / jnp.sum(e, axis=-1, keepdims=True))
            y = _dot(p.astype(BF16), v_ref[:, sl])
            a_ref[:, sl] = (y * gate[:, sl]).astype(BF16)

    full = lambda a: pl.BlockSpec(a.shape, lambda i: (0, 0))
    return pl.pallas_call(
        body, name="mem_attn_fwd", grid=(s // tq,),
        in_specs=[pl.BlockSpec((tq, 1024), lambda i: (i, COL_M // 1024)),
                  pl.BlockSpec((tq, 1024), lambda i: (i, COL_M // 1024 + 1)),
                  full(kn), full(v), full(gq)],
        out_specs=pl.BlockSpec((tq, 1024), lambda i: (i, 0)),
        out_shape=jax.ShapeDtypeStruct((s, 1024), BF16),
        compiler_params=_params(("parallel",)),
    )(p_all, p_all, kn, v, gq)


def _merge_fwd(a_conv, w_c, a_mla, w_m, a_mem, w_e, p_all):
    s = a_conv.shape[0]
    tm, tn = min(TS_ROW, s), 1024
    nj = D_MODEL // tn

    def body(ac_ref, wc_ref, am_ref, wm_ref, ae_ref, we_ref, gc_ref, gm_ref, ge_ref,
             oc_ref, om_ref, oe_ref, mg_ref):
        oc = _dot(ac_ref[...], wc_ref[...])
        om = _dot(am_ref[...], wm_ref[...])
        oe = _dot(ae_ref[...], we_ref[...])
        oc_ref[...] = oc.astype(BF16)
        om_ref[...] = om.astype(BF16)
        oe_ref[...] = oe.astype(BF16)
        mg_ref[...] = (_sigmoid(gc_ref[...].astype(F32)) * oc + _sigmoid(gm_ref[...].astype(F32)) * om
                       + _sigmoid(ge_ref[...].astype(F32)) * oe).astype(BF16)

    act = lambda k: pl.BlockSpec((tm, k), lambda i, j: (i, 0))
    wgt = lambda k: pl.BlockSpec((k, tn), lambda i, j: (0, j))
    gate = lambda b: pl.BlockSpec((tm, tn), lambda i, j: (i, b * nj + j))
    out = pl.BlockSpec((tm, tn), lambda i, j: (i, j))
    return pl.pallas_call(
        body, name="merge_fwd", grid=(s // tm, nj),
        in_specs=[act(1024), wgt(1024), act(2048), wgt(2048), act(1024), wgt(1024), gate(0), gate(1), gate(2)],
        out_specs=[out] * 4,
        out_shape=[jax.ShapeDtypeStruct((s, D_MODEL), BF16)] * 4,
        compiler_params=_params(("parallel", "parallel")),
    )(a_conv, w_c, a_mla, w_m, a_mem, w_e, p_all, p_all, p_all)


def _out_fwd(merged, w_o, x, target):
    s = merged.shape[0]
    tm, tn = min(TS_ROW, s), 1024
    nj = D_MODEL // tn

    def body(mg_ref, w_ref, x_ref, t_ref, dy_ref, dyb_ref, ls_ref):
        e = x_ref[...] + _dot(mg_ref[...], w_ref[...]) - t_ref[...]
        dy = e * (1.0 / D_MODEL)
        dy_ref[...] = dy
        dyb_ref[...] = dy.astype(BF16)
        r = _fold8(e * e)
        acc = r[:, 0:128]
        for cc in range(1, tn // 128):
            acc = acc + r[:, cc * 128:(cc + 1) * 128]
        ls_ref[...] = acc

    tile = pl.BlockSpec((tm, tn), lambda i, j: (i, j))
    return pl.pallas_call(
        body, name="out_fwd", grid=(s // tm, nj),
        in_specs=[pl.BlockSpec((tm, D_MODEL), lambda i, j: (i, 0)),
                  pl.BlockSpec((D_MODEL, tn), lambda i, j: (0, j)), tile, tile],
        out_specs=[tile, tile, pl.BlockSpec((8, 128), lambda i, j: (i, j))],
        out_shape=[jax.ShapeDtypeStruct((s, D_MODEL), F32), jax.ShapeDtypeStruct((s, D_MODEL), BF16),
                   jax.ShapeDtypeStruct((s // tm * 8, nj * 128), F32)],
        compiler_params=_params(("parallel", "parallel")),
    )(merged, w_o, x, target)


def _merge_bwd(dyb, w_o, p_all, o_c, o_m, o_e):
    s = dyb.shape[0]
    tm = min(256, s)

    def body(dy_ref, w_ref, g_ref, oc_ref, om_ref, oe_ref, dc_ref, dm_ref, de_ref, dg_ref):
        dmg = _dot(dy_ref[...], w_ref[...], "nt")
        for b, (o_ref, d_ref) in enumerate(((oc_ref, dc_ref), (om_ref, dm_ref), (oe_ref, de_ref))):
            sl = slice(b * D_MODEL, (b + 1) * D_MODEL)
            sg = _sigmoid(g_ref[:, sl].astype(F32))
            d_ref[...] = (dmg * sg).astype(BF16)
            dg_ref[:, sl] = (dmg * o_ref[...].astype(F32) * sg * (1.0 - sg)).astype(BF16)

    row = pl.BlockSpec((tm, D_MODEL), lambda i: (i, 0))
    wide = pl.BlockSpec((tm, 3 * D_MODEL), lambda i: (i, 0))
    return pl.pallas_call(
        body, name="merge_bwd", grid=(s // tm,),
        in_specs=[row, pl.BlockSpec((D_MODEL, D_MODEL), lambda i: (0, 0)), wide, row, row, row],
        out_specs=[row, row, row, wide],
        out_shape=[jax.ShapeDtypeStruct((s, D_MODEL), BF16)] * 3 + [jax.ShapeDtypeStruct((s, 3 * D_MODEL), BF16)],
        compiler_params=_params(("parallel",)),
    )(dyb, w_o, p_all, o_c, o_m, o_e)


def _conv_bwd(da, p_all, co, conv_w):
    s = da.shape[0]
    ts = min(TS_ROW, s)
    n = s // ts
    c0 = COL_C // 1024

    def body(da_ref, cg_ref, bg_ref, u_ref, z_ref, co_ref, w_ref, dp_ref, dw_ref, carry_ref):
        @pl.when(pl.program_id(0) == 0)
        def _():
            carry_ref[...] = jnp.zeros_like(carry_ref)
            dw_ref[...] = jnp.zeros_like(dw_ref)

        da_ = da_ref[...].astype(F32)
        cg, bg = cg_ref[...].astype(F32), bg_ref[...].astype(F32)
        u, z, cov = u_ref[...].astype(F32), z_ref[...].astype(F32), co_ref[...].astype(F32)
        sz = _sigmoid(z)
        dyc = da_ * (z * sz)
        dz = da_ * (bg * cov) * (sz * (1.0 + z * (1.0 - sz)))
        db = dyc * cov
        dco = dyc * bg
        carry = carry_ref[...]
        d1 = _shift_rows(dco, carry, 1, ts, False)
        d2 = _shift_rows(dco, carry, 2, ts, False)
        carry_ref[...] = dco[:8]
        dpp = w_ref[2:3, :] * dco + w_ref[1:2, :] * d1 + w_ref[0:1, :] * d2
        p = cg * u
        dw_ref[0] += _fold8(p * d2)
        dw_ref[1] += _fold8(p * d1)
        dw_ref[2] += _fold8(p * dco)
        dp_ref[...] = jnp.concatenate([dpp * u, db, dpp * cg, dz], axis=1).astype(BF16)

    rev = lambda c: pl.BlockSpec((ts, 1024), lambda i: (n - 1 - i, c))
    return pl.pallas_call(
        body, name="conv_bwd", grid=(n,),
        in_specs=[rev(0), rev(c0), rev(c0 + 1), rev(c0 + 2), rev(c0 + 3), rev(0),
                  pl.BlockSpec((3, 1024), lambda i: (0, 0))],
        out_specs=[pl.BlockSpec((ts, 4096), lambda i: (n - 1 - i, 0)),
                   pl.BlockSpec((3, 8, 1024), lambda i: (0, 0, 0))],
        out_shape=[jax.ShapeDtypeStruct((s, 4096), BF16), jax.ShapeDtypeStruct((3, 8, 1024), F32)],
        scratch_shapes=[pltpu.VMEM((8, 1024), F32)],
        compiler_params=_params(("arbitrary",)),
    )(da, p_all, p_all, p_all, p_all, co, conv_w)


def _mla_gate_bwd(da, y, p_all):
    s = da.shape[0]
    ts = min(TS_ROW, s)

    def body(da_ref, y_ref, z_ref, dy_ref, dz_ref, dl_ref):
        da_, yv, z = da_ref[...].astype(F32), y_ref[...].astype(F32), z_ref[...].astype(F32)
        sz = _sigmoid(z)
        dyv = da_ * (z * sz)
        dy_ref[...] = dyv.astype(BF16)
        dz_ref[...] = (da_ * yv * (sz * (1.0 + z * (1.0 - sz)))).astype(BF16)
        pr = dyv * yv
        for h in range(MLA_HEADS):
            dl_ref[h] = jnp.broadcast_to(jnp.sum(pr[:, h * 128:(h + 1) * 128], axis=-1, keepdims=True), (ts, 128))

    row = pl.BlockSpec((ts, D_MODEL), lambda i: (i, 0))
    return pl.pallas_call(
        body, name="mla_gate_bwd", grid=(s // ts,),
        in_specs=[row, row, pl.BlockSpec((ts, D_MODEL), lambda i: (i, COL_Z // D_MODEL))],
        out_specs=[row, row, pl.BlockSpec((MLA_HEADS, ts, 128), lambda i: (0, i, 0))],
        out_shape=[jax.ShapeDtypeStruct((s, D_MODEL), BF16)] * 2 + [jax.ShapeDtypeStruct((MLA_HEADS, s, 128), F32)],
        compiler_params=_params(("parallel",)),
    )(da, y, p_all)


def _flash_bwd(qcat, kcat, kv, dy, lse, delta):
    s = qcat.shape[1]
    tq = tk = min(TQ_ATT, s)
    tr = min(TR_ATT, tq)
    nq = s // tq
    qi, kj = _causal_pairs(nq, True)
    npairs = qi.shape[0]

    def body(qi_ref, kj_ref, q_ref, k_ref, v_ref, do_ref, lse_ref, dl_ref, dq_ref, dk_ref, dv_ref,
             dq_acc, dk_acc, dv_acc):
        t = pl.program_id(1)
        i, j = qi_ref[t], kj_ref[t]

        def add_dq(rows, dq_new):
            @pl.when(j == 0)
            def _():
                dq_acc[rows, :] = dq_new

            @pl.when(j > 0)
            def _():
                dq_acc[rows, :] += dq_new

        def full_step():
            q, k = q_ref[0], k_ref[0]
            p = jnp.exp2(_dot(q, k, "nt") - lse_ref[0][:, 0:1])
            do = do_ref[...]
            dpv = _dot(do, v_ref[...], "nt")
            ds = (p * (dpv - dl_ref[0][:, 0:1])).astype(BF16)
            dv_acc[...] += _dot(p.astype(BF16), do, "tn")
            dk_acc[...] += _dot(ds, q, "tn")
            add_dq(pl.ds(pl.multiple_of(i * tq, tq), tq), _dot(ds, k))

        def diagonal_step():
            def keys(r):
                return (r + 1) * tr

            def scores(r):
                rows = slice(r * tr, (r + 1) * tr)
                return (_dot(q_ref[0, rows, :], k_ref[0, :keys(r), :], "nt"),
                        _dot(do_ref[rows, :], v_ref[:keys(r), :], "nt"))

            def gradients(n, r, pb, ds):
                nk = keys(r)
                rows = slice(r * tr, (r + 1) * tr)
                dv_new = _dot(pb, do_ref[rows, :], "tn")
                dk_new = _dot(ds, q_ref[0, rows, :], "tn")
                if n == 0:
                    dv_acc[...] = dv_new
                    dk_acc[...] = dk_new
                else:
                    dv_acc[:nk, :] += dv_new
                    dk_acc[:nk, :] += dk_new
                add_dq(pl.ds(pl.multiple_of(i * tq + r * tr, tr), tr), _dot(ds, k_ref[0, :nk, :]))

            order = list(range(tq // tr - 1, -1, -1))
            pending = [scores(r) for r in order[:2]]
            late = None
            for n, r in enumerate(order):
                sc, dpv = pending.pop(0)
                if n + 2 < len(order):
                    pending.append(scores(order[n + 2]))
                rows = slice(r * tr, (r + 1) * tr)
                sc = jnp.where(_chunk_mask(tr, keys(r), r * tr), sc, NEG)
                p = jnp.exp2(sc - lse_ref[0][rows, 0:1])
                ds = (p * (dpv - dl_ref[0][rows, 0:1])).astype(BF16)
                if late is not None:
                    gradients(*late)
                late = (n, r, p.astype(BF16), ds)
            gradients(*late)

        @pl.when(i == j)
        def _():
            diagonal_step()

        @pl.when(i > j)
        def _():
            full_step()

        @pl.when(i == nq - 1)
        def _():
            dk_ref[0] = dk_acc[...].astype(BF16)
            dv_ref[0] = dv_acc[...].astype(BF16)

        @pl.when(t == npairs - 1)
        def _():
            dq_ref[0] = dq_acc[...].astype(BF16)

    qrow = lambda w: pl.BlockSpec((1, tq, w), lambda h, t, qi_ref, kj_ref: (h, qi_ref[t], 0))
    krow = lambda w: pl.BlockSpec((1, tk, w), lambda h, t, qi_ref, kj_ref: (h, kj_ref[t], 0))
    return pl.pallas_call(
        body, name="flash_bwd",
        grid_spec=pltpu.PrefetchScalarGridSpec(
            num_scalar_prefetch=2, grid=(MLA_HEADS, npairs),
            in_specs=[qrow(HEAD_PAD), krow(HEAD_PAD),
                      pl.BlockSpec((tk, 128), lambda h, t, qi_ref, kj_ref: (kj_ref[t], 2 * h + 1)),
                      pl.BlockSpec((tq, 128), lambda h, t, qi_ref, kj_ref: (qi_ref[t], h)),
                      qrow(128), qrow(128)],
            out_specs=[pl.BlockSpec((1, s, HEAD_PAD), lambda h, t, qi_ref, kj_ref: (h, 0, 0)),
                       krow(HEAD_PAD), krow(128)],
            scratch_shapes=[pltpu.VMEM((s, HEAD_PAD), F32), pltpu.VMEM((tk, HEAD_PAD), F32),
                            pltpu.VMEM((tk, 128), F32)]),
        out_shape=[jax.ShapeDtypeStruct((MLA_HEADS, s, HEAD_PAD), BF16),
                   jax.ShapeDtypeStruct((MLA_HEADS, s, HEAD_PAD), BF16),
                   jax.ShapeDtypeStruct((MLA_HEADS, s, 128), BF16)],
        compiler_params=_params(("parallel", "arbitrary")),
    )(qi, kj, qcat, kcat, kv, dy, lse, delta)


def _q_prep_bwd(dqcat, q_raw, gn, gr, cs, sg, side=None):
    s = q_raw.shape[0]
    ts = min(TS_BWD, s)

    def body(dq_ref, q_ref, gn_ref, gr_ref, cs_ref, sg_ref, o_ref, dg_ref):
        @pl.when(pl.program_id(0) == 0)
        def _():
            dg_ref[...] = jnp.zeros_like(dg_ref)

        ga_acc = jnp.zeros((8, 128), F32)
        gb_acc = jnp.zeros((8, 128), F32)
        for h in range(MLA_HEADS):
            cols = slice(h * HEAD_PAD, (h + 1) * HEAD_PAD)
            q = q_ref[:, cols].astype(F32)
            d = dq_ref[h].astype(F32) * MLA_SCALE
            da, ga = _rms_bwd(q[:, :128], gn_ref[...], d[:, :128], QK_NOPE)
            db, gb = _rms_bwd(q[:, 128:], gr_ref[...], _rope_bwd(d[:, 128:], cs_ref[...], sg_ref[...]), QK_ROPE)
            o_ref[:, cols] = jnp.concatenate([da, db], axis=1).astype(BF16)
            ga_acc = ga_acc + _fold8(ga)
            gb_acc = gb_acc + _fold8(gb)
        dg_ref[0] += ga_acc
        dg_ref[1] += gb_acc

    wide = pl.BlockSpec((ts, MLA_HEADS * HEAD_PAD), lambda i: (i, 0))
    return _pcall(
        body, side, "q_prep_bwd", (s // ts,),
        [pl.BlockSpec((MLA_HEADS, ts, HEAD_PAD), lambda i: (0, i, 0)), wide,
         pl.BlockSpec((1, 128), lambda i: (0, 0)), pl.BlockSpec((1, 128), lambda i: (0, 0)),
         pl.BlockSpec((ts, 128), lambda i: (i, 0)), pl.BlockSpec((ts, 128), lambda i: (i, 0))],
        [wide, pl.BlockSpec((2, 8, 128), lambda i: (0, 0, 0))],
        [jax.ShapeDtypeStruct((s, MLA_HEADS * HEAD_PAD), BF16), jax.ShapeDtypeStruct((2, 8, 128), F32)],
        (dqcat, q_raw, gn, gr, cs, sg), ("arbitrary",))


def _k_prep_bwd(dkcat, dv, kv, gk, side=None):
    s = kv.shape[0]
    ts = min(TS_BWD, s)

    def body(dk_ref, dv_ref, k_ref, gk_ref, o_ref, dkr_ref, dg_ref):
        @pl.when(pl.program_id(0) == 0)
        def _():
            dg_ref[...] = jnp.zeros_like(dg_ref)

        g_acc = jnp.zeros((8, 128), F32)
        dkr = jnp.zeros((ts, 128), F32)
        for h in range(MLA_HEADS):
            d = dk_ref[h].astype(F32) * LN2
            k = k_ref[:, h * HEAD_PAD:h * HEAD_PAD + 128].astype(F32)
            dk_raw, gg = _rms_bwd(k, gk_ref[...], d[:, :128], QK_NOPE)
            o_ref[:, h * HEAD_PAD:(h + 1) * HEAD_PAD] = jnp.concatenate([dk_raw.astype(BF16), dv_ref[h]], axis=1)
            g_acc = g_acc + _fold8(gg)
            dkr = dkr + d[:, 128:]
        dkr_ref[...] = dkr
        dg_ref[...] += g_acc

    wide = pl.BlockSpec((ts, MLA_HEADS * HEAD_PAD), lambda i: (i, 0))
    return _pcall(
        body, side, "k_prep_bwd", (s // ts,),
        [pl.BlockSpec((MLA_HEADS, ts, HEAD_PAD), lambda i: (0, i, 0)),
         pl.BlockSpec((MLA_HEADS, ts, 128), lambda i: (0, i, 0)), wide,
         pl.BlockSpec((1, 128), lambda i: (0, 0))],
        [wide, pl.BlockSpec((ts, 128), lambda i: (i, 0)), pl.BlockSpec((8, 128), lambda i: (0, 0))],
        [jax.ShapeDtypeStruct((s, MLA_HEADS * HEAD_PAD), BF16), jax.ShapeDtypeStruct((s, 128), F32),
         jax.ShapeDtypeStruct((8, 128), F32)],
        (dkcat, dv, kv, gk), ("arbitrary",))


def _lora_bwd(dcqn, dckvn, dkr, p_all, gq, gkv, gkr, cs, sg):
    s = p_all.shape[0]
    ts = min(TS_ROW, s)

    def body(dq_ref, dkv_ref, dkr_ref, cq_ref, ckv_ref, kr_ref, gq_ref, gkv_ref, gkr_ref, cs_ref, sg_ref,
             o_ref, dgq_ref, dgkv_ref, dgkr_ref):
        @pl.when(pl.program_id(0) == 0)
        def _():
            dgq_ref[...] = jnp.zeros_like(dgq_ref)
            dgkv_ref[...] = jnp.zeros_like(dgkv_ref)
            dgkr_ref[...] = jnp.zeros_like(dgkr_ref)

        dq, g1 = _rms_bwd(cq_ref[...].astype(F32), gq_ref[...], dq_ref[...], 512)
        dkv, g2 = _rms_bwd(ckv_ref[...].astype(F32), gkv_ref[...], dkv_ref[...], 512)
        dkn = _rope_bwd(dkr_ref[...], cs_ref[...], sg_ref[...])
        dk, g3 = _rms_bwd(kr_ref[...].astype(F32), gkr_ref[...], dkn, QK_ROPE)
        o_ref[...] = jnp.concatenate([dq, dkv, dk, jnp.zeros((ts, 2048 - 1152), F32)], axis=1).astype(BF16)
        dgq_ref[...] += _fold8(g1)
        dgkv_ref[...] += _fold8(g2)
        dgkr_ref[...] += _fold8(g3)

    row = lambda w: pl.BlockSpec((ts, w), lambda i: (i, 0))
    vec = lambda w: pl.BlockSpec((1, w), lambda i: (0, 0))
    acc = lambda w: pl.BlockSpec((8, w), lambda i: (0, 0))
    return pl.pallas_call(
        body, name="lora_bwd", grid=(s // ts,),
        in_specs=[row(512), row(512), row(128),
                  pl.BlockSpec((ts, 512), lambda i: (i, COL_S // 512)),
                  pl.BlockSpec((ts, 512), lambda i: (i, COL_S // 512 + 1)),
                  pl.BlockSpec((ts, 128), lambda i: (i, (COL_S + 1024) // 128)),
                  vec(512), vec(512), vec(128), row(128), row(128)],
        out_specs=[row(2048), acc(512), acc(512), acc(128)],
        out_shape=[jax.ShapeDtypeStruct((s, 2048), BF16), jax.ShapeDtypeStruct((8, 512), F32),
                   jax.ShapeDtypeStruct((8, 512), F32), jax.ShapeDtypeStruct((8, 128), F32)],
        compiler_params=_params(("arbitrary",)),
    )(dcqn, dckvn, dkr, p_all, p_all, p_all, gq, gkv, gkr, cs, sg)


def _mem_attn_bwd(da, p_all, kn, v, gq):
    s = p_all.shape[0]
    tq = min(TS_ROW, s)

    def body(da_ref, q_ref, z_ref, kn_ref, v_ref, gq_ref, o_ref, dkn_ref, dv_ref, dg_ref):
        @pl.when(pl.program_id(0) == 0)
        def _():
            dkn_ref[...] = jnp.zeros_like(dkn_ref)
            dv_ref[...] = jnp.zeros_like(dv_ref)
            dg_ref[...] = jnp.zeros_like(dg_ref)

        z = z_ref[...].astype(F32)
        da_ = da_ref[...].astype(F32)
        sz = _sigmoid(z)
        gate = z * sz
        dgate = sz * (1.0 + z * (1.0 - sz))
        for hh in range(MEM_HEADS):
            sl = slice(hh * MEM_HEAD_DIM, (hh + 1) * MEM_HEAD_DIM)
            qf = q_ref[:, sl].astype(F32)
            qn = _rms_fwd(qf, gq_ref[...], MEM_HEAD_DIM).astype(BF16)
            knh, vh = kn_ref[:, sl], v_ref[:, sl]
            sc = _dot(qn, knh, "nt") * MEM_SCALE
            e = jnp.exp(sc - jnp.max(sc, axis=-1, keepdims=True))
            p = e * (1.0 / jnp.sum(e, axis=-1, keepdims=True))
            pb = p.astype(BF16)
            y = _dot(pb, vh)
            dyh = da_[:, sl] * gate[:, sl]
            o_ref[:, 1024 + hh * MEM_HEAD_DIM:1024 + (hh + 1) * MEM_HEAD_DIM] = (
                da_[:, sl] * y * dgate[:, sl]).astype(BF16)
            dyb = dyh.astype(BF16)
            dpm = _dot(dyb, vh, "nt")
            ds = (p * (dpm - jnp.sum(dpm * p, axis=-1, keepdims=True)) * MEM_SCALE).astype(BF16)
            dqn = _dot(ds, knh)
            dkn_ref[:, sl] += _dot(ds, qn, "tn")
            dv_ref[:, sl] += _dot(pb, dyb, "tn")
            dq, gg = _rms_bwd(qf, gq_ref[...], dqn, MEM_HEAD_DIM)
            o_ref[:, sl] = dq.astype(BF16)
            dg_ref[...] += _fold8(gg)

    full = lambda a: pl.BlockSpec(a.shape, lambda i: (0, 0))
    return pl.pallas_call(
        body, name="mem_attn_bwd", grid=(s // tq,),
        in_specs=[pl.BlockSpec((tq, 1024), lambda i: (i, 0)),
                  pl.BlockSpec((tq, 1024), lambda i: (i, COL_M // 1024)),
                  pl.BlockSpec((tq, 1024), lambda i: (i, COL_M // 1024 + 1)),
                  full(kn), full(v), full(gq)],
        out_specs=[pl.BlockSpec((tq, 2048), lambda i: (i, 0)),
                   pl.BlockSpec((MEM_TOKENS, 1024), lambda i: (0, 0)),
                   pl.BlockSpec((MEM_TOKENS, 1024), lambda i: (0, 0)),
                   pl.BlockSpec((8, MEM_HEAD_DIM), lambda i: (0, 0))],
        out_shape=[jax.ShapeDtypeStruct((s, 2048), BF16), jax.ShapeDtypeStruct((MEM_TOKENS, 1024), F32),
                   jax.ShapeDtypeStruct((MEM_TOKENS, 1024), F32), jax.ShapeDtypeStruct((8, MEM_HEAD_DIM), F32)],
        compiler_params=_params(("arbitrary",)),
    )(da, p_all, p_all, kn, v, gq)


def _mem_kv_bwd(dkn, dv, kraw, memn, mem, g_norm, w_kv, g_kn):
    m = mem.shape[0]

    def body(dkn_ref, dv_ref, kraw_ref, memn_ref, mem_ref, g_ref, w_ref, gk_ref, dw_ref, dgn_ref, dgk_ref, dkv_sc):
        gk_acc = jnp.zeros((8, MEM_HEAD_DIM), F32)
        for hh in range(MEM_HEADS):
            sl = slice(hh * MEM_HEAD_DIM, (hh + 1) * MEM_HEAD_DIM)
            dk, gg = _rms_bwd(kraw_ref[:, sl], gk_ref[...], dkn_ref[:, sl], MEM_HEAD_DIM)
            dkv_sc[:, sl] = dk.astype(BF16)
            gk_acc = gk_acc + _fold8(gg)
        dgk_ref[...] = gk_acc
        dkv_sc[:, 1024:] = dv_ref[...].astype(BF16)
        dkv = dkv_sc[...]
        dw_ref[...] = _dot(memn_ref[...], dkv, "tn")
        dmemn = _dot(dkv, w_ref[...], "nt")
        xf = mem_ref[...]
        r = lax.rsqrt(jnp.mean(xf * xf, axis=-1, keepdims=True) + EPS)
        dgn_ref[...] = _fold8(dmemn * (xf * r))

    return pl.pallas_call(
        body, name="mem_kv_bwd",
        out_shape=[jax.ShapeDtypeStruct((D_MODEL, D_MODEL), F32), jax.ShapeDtypeStruct((8, D_MODEL), F32),
                   jax.ShapeDtypeStruct((8, MEM_HEAD_DIM), F32)],
        scratch_shapes=[pltpu.VMEM((m, D_MODEL), BF16)],
        compiler_params=_params(),
    )(dkn, dv, kraw, memn, mem, g_norm, w_kv, g_kn)


def _dh_bwd(dps, w_all_t, row0, nrows, name, side=None):
    tm, tk = min(TS_ROW, nrows), 1024
    blk0 = row0 // tm
    widths = [a.shape[1] // tk for a in dps]
    starts = [int(v) for v in np.cumsum([0] + widths[:-1])]
    nk = sum(widths)

    def body(*refs):
        d_refs, w_ref, o_ref, acc_ref = refs[:5], refs[5], refs[6], refs[7]
        kk = pl.program_id(1)

        @pl.when(kk == 0)
        def _():
            acc_ref[...] = jnp.zeros_like(acc_ref)

        for d_ref, st, wd in zip(d_refs, starts, widths):
            @pl.when((kk >= st) & (kk < st + wd))
            def _(d_ref=d_ref):
                acc_ref[...] += _dot(d_ref[...], w_ref[...])

        @pl.when(kk == nk - 1)
        def _():
            o_ref[...] = acc_ref[...]

    def dspec(st, wd):
        return pl.BlockSpec((tm, tk), lambda i, kk: (blk0 + i, jnp.clip(kk - st, 0, wd - 1)))

    outs, extra = _pcall(
        body, side, name, (nrows // tm, nk),
        [dspec(st, wd) for st, wd in zip(starts, widths)] + [pl.BlockSpec((tk, D_MODEL), lambda i, kk: (kk, 0))],
        [pl.BlockSpec((tm, D_MODEL), lambda i, kk: (i, 0))], [jax.ShapeDtypeStruct((nrows, D_MODEL), F32)],
        (*dps, w_all_t), ("parallel", "arbitrary"), scratch_shapes=[pltpu.VMEM((tm, D_MODEL), F32)])
    return outs[0], extra


def _dx_bwd(dh_a, dh_b, x, dy, g, side=None):
    s = x.shape[0]
    ts = min(TS_BWD, dh_a.shape[0])
    na, nb = dh_a.shape[0] // ts, dh_b.shape[0] // ts

    def body(dha_ref, dhb_ref, x_ref, dy_ref, g_ref, o_ref, dg_ref):
        @pl.when(pl.program_id(0) == 0)
        def _():
            dg_ref[...] = jnp.zeros_like(dg_ref)

        dh = jnp.where(pl.program_id(0) < na, dha_ref[...], dhb_ref[...])
        dx, gg = _rms_bwd(x_ref[...], g_ref[...], dh, D_MODEL)
        o_ref[...] = dy_ref[...] + dx
        dg_ref[...] += _fold8(gg)

    row = pl.BlockSpec((ts, D_MODEL), lambda i: (i, 0))
    return _pcall(
        body, side, "dx_bwd", (s // ts,),
        [pl.BlockSpec((ts, D_MODEL), lambda i: (jnp.minimum(i, na - 1), 0)),
         pl.BlockSpec((ts, D_MODEL), lambda i: (jnp.clip(i - na, 0, nb - 1), 0)),
         row, row, pl.BlockSpec((1, D_MODEL), lambda i: (0, 0))],
        [row, pl.BlockSpec((8, D_MODEL), lambda i: (0, 0))],
        [jax.ShapeDtypeStruct((s, D_MODEL), F32), jax.ShapeDtypeStruct((8, D_MODEL), F32)],
        (dh_a, dh_b, x, dy, g), ("arbitrary",))


def _pad128(v, n):
    return jnp.pad(v.reshape(1, n), ((0, 0), (0, 128 - n)))


def _local_step(x, positions, mem, target, comm, gains):
    half = QK_ROPE // 2
    inv_freq = jnp.power(ROPE_THETA, -jnp.arange(half, dtype=F32) / half)
    ang = positions.astype(F32)[:, None] * inv_freq
    cos, sin = jnp.cos(ang), jnp.sin(ang)
    zpad = jnp.zeros((x.shape[0], 64), F32)
    cs = jnp.concatenate([cos, cos, zpad], axis=1)
    sg = jnp.concatenate([-sin, sin, zpad], axis=1)
    g_qr, g_kr = _pad128(gains["mla_qn_rope_g"], 64), _pad128(gains["mla_kn_rope_g"], 64)
    g_qn, g_kn = gains["mla_qn_nope_g"], gains["mla_kn_nope_g"]

    def hosted_matmul(tag, a, b, mode, out_dtype):
        side = comm.side(tag)
        if side is None:
            return _matmul(a, b, mode, out_dtype, tag)
        out, extra = _matmul(a, b, mode, out_dtype, tag, side=side)
        comm.done(tag, extra)
        return out

    (p_all, h), extra = _proj_fwd(x, gains["norm_g"], comm.weight("w_all_t"), comm.side("proj_fwd"))
    comm.done("proj_fwd", extra)
    a_conv, co = _conv_fwd(p_all, comm.weight("conv_w"))
    (cqn, ckvn, krope), extra = _lora_fwd(p_all, gains["mla_q_norm_g"], gains["mla_kv_norm_g"], g_kr, cs, sg,
                                          comm.side("lora_fwd"))
    comm.done("lora_fwd", extra)
    q_raw = _matmul(cqn, comm.weight("w_uq"), "nn", BF16, "q_up")
    kv = _matmul(ckvn, comm.weight("w_ukv"), "nn", BF16, "kv_up")
    qcat = _q_prep(q_raw, g_qn, g_qr, cs, sg)
    kcat, vt = _k_prep(kv, krope, g_kn)
    mla_y, a_mla, lse = _flash_fwd(qcat, kcat, vt, p_all)
    memn, kraw, kn, vmem = _mem_kv_fwd(mem, gains["mem_norm_g"], comm.weight("w_mem_kv"), gains["mem_kn_g"])
    a_mem = _mem_attn_fwd(p_all, kn, vmem, gains["mem_qn_g"])
    o_c, o_m, o_e, merged = _merge_fwd(a_conv, comm.weight("w_conv_out"), a_mla, comm.weight("w_mla_out"), a_mem,
                                       comm.weight("w_mem_out"), p_all)
    dy, dyb, loss_parts = _out_fwd(merged, comm.weight("w_o"), x, target)

    comm.put("w_o", _matmul(merged, dyb, "tn", BF16, "dw_o"))
    do_c, do_m, do_e, dp_g = _merge_bwd(dyb, comm.weight("w_o"), p_all, o_c, o_m, o_e)
    comm.put("w_conv_out", _matmul(a_conv, do_c, "tn", BF16, "dw_conv_out"))
    comm.put("w_mla_out", _matmul(a_mla, do_m, "tn", BF16, "dw_mla_out"))
    comm.put("w_mem_out", _matmul(a_mem, do_e, "tn", BF16, "dw_mem_out"))
    da_conv = hosted_matmul("da_conv", do_c, comm.weight("w_conv_out"), "nt", BF16)
    da_mla = _matmul(do_m, comm.weight("w_mla_out"), "nt", BF16, "da_mla")
    da_mem = _matmul(do_e, comm.weight("w_mem_out"), "nt", BF16, "da_mem")
    dp_c, dconv_w = _conv_bwd(da_conv, p_all, co, comm.weight("conv_w"))
    dmla_y, dp_z, delta = _mla_gate_bwd(da_mla, mla_y, p_all)
    dqcat, dkcat, dv = _flash_bwd(qcat, kcat, kv, dmla_y, lse, delta)
    (dq_raw, dg_q), extra = _q_prep_bwd(dqcat, q_raw, g_qn, g_qr, cs, sg, comm.side("q_prep_bwd"))
    comm.done("q_prep_bwd", extra)
    (dkv, dkr, dg_kn), extra = _k_prep_bwd(dkcat, dv, kv, g_kn, comm.side("k_prep_bwd"))
    comm.done("k_prep_bwd", extra)
    comm.put("w_uq", _matmul(cqn, dq_raw, "tn", BF16, "dw_uq"))
    comm.put("w_ukv", _matmul(ckvn, dkv, "tn", BF16, "dw_ukv"))
    dcqn = _matmul(dq_raw, comm.weight("w_uq"), "nt", F32, "dcqn")
    dckvn = _matmul(dkv, comm.weight("w_ukv"), "nt", F32, "dckvn")
    dp_s, dg_qn, dg_kvn, dg_kr = _lora_bwd(dcqn, dckvn, dkr, p_all, gains["mla_q_norm_g"],
                                            gains["mla_kv_norm_g"], g_kr, cs, sg)
    dp_m, dkn, dvm, dg_mq = _mem_attn_bwd(da_mem, p_all, kn, vmem, gains["mem_qn_g"])
    dw_mem_kv, dg_mn, dg_mk = _mem_kv_bwd(dkn, dvm, kraw, memn, mem, gains["mem_norm_g"],
                                          comm.weight("w_mem_kv"), gains["mem_kn_g"])
    comm.put("w_mem_kv", dw_mem_kv.astype(BF16))
    dps = [dp_g, dp_z, dp_c, dp_m, dp_s]
    comm.put("w_all_t", [hosted_matmul("dw_in_%d" % k, d, h, "tn", BF16) for k, d in enumerate(dps)])
    s = x.shape[0]
    dh_a, extra = _dh_bwd(dps, comm.weight("w_all_t"), 0, s // 4, "dh_a", comm.side("dh_a"))
    comm.done("dh_a", extra)
    dh_b, extra = _dh_bwd(dps, comm.weight("w_all_t"), s // 4, s - s // 4, "dh_b", comm.side("dh_b"))
    comm.done("dh_b", extra)
    (grad_x, dg_n), extra = _dx_bwd(dh_a, dh_b, x, dy, gains["norm_g"], comm.side("dx_bwd"))
    comm.done("dx_bwd", extra)

    gsmall = {
        "norm_g": dg_n.sum(0), "conv_w": dconv_w.sum(1), "mla_q_norm_g": dg_qn.sum(0),
        "mla_kv_norm_g": dg_kvn.sum(0), "mla_qn_nope_g": dg_q[0].sum(0), "mla_qn_rope_g": dg_q[1].sum(0)[:64],
        "mla_kn_nope_g": dg_kn.sum(0), "mla_kn_rope_g": dg_kr.sum(0)[:64], "mem_norm_g": dg_mn.sum(0),
        "mem_qn_g": dg_mq.sum(0), "mem_kn_g": dg_mk.sum(0),
    }
    return loss_parts, grad_x, gsmall


def _me():
    return lax.axis_index("x"), lax.axis_index("y"), lax.axis_index("c")


def _chips(x, y):
    return [(1 - x, y), (x, 1 - y), (1 - x, 1 - y)]


def _allgather_shards(xs, name):
    r, c = xs.shape
    hc = c // 2

    def body(x_ref, out_ref, send_sems, recv_sems):
        x, y, cc = _me()
        me, sibling = (x, y, cc), (x, y, 1 - cc)
        across_x, across_y, diagonal = _chips(x, y)

        def rows(px, py, pc):
            return out_ref.at[2 * px + py, :, pl.ds(pc * hc, hc)]

        def copy(k, block, to, src=None):
            return pltpu.make_async_remote_copy(
                src_ref=rows(*block) if src is None else src, dst_ref=rows(*block),
                send_sem=send_sems.at[k], recv_sem=recv_sems.at[k], device_id=to, device_id_type=MESH)

        my_half = x_ref.at[:, pl.ds(cc * hc, hc)]
        direct = [copy(0, me, (*across_x, cc), src=my_half), copy(1, me, (*across_y, cc), src=my_half)]
        for cp in direct:
            cp.start()
        copy(0, (*across_x, cc), me).wait_recv()
        copy(1, (*across_y, cc), me).wait_recv()
        from_x = cc == 0
        came = (jnp.where(from_x, across_x[0], across_y[0]), jnp.where(from_x, across_x[1], across_y[1]), cc)
        goes = (jnp.where(from_x, across_y[0], across_x[0]), jnp.where(from_x, across_y[1], across_x[1]), cc)
        relay = copy(2, came, goes)
        relay.start()
        passed = [copy(3, (*across_x, cc), sibling), copy(4, (*across_y, cc), sibling)]
        for cp in passed:
            cp.start()
        copy(2, (*diagonal, cc), me).wait_recv()
        passed.append(copy(5, (*diagonal, cc), sibling))
        passed[2].start()
        for j, chip in enumerate((across_x, across_y, diagonal)):
            copy(3 + j, (*chip, 1 - cc), me).wait_recv()
        for cp in direct + [relay] + passed:
            cp.wait_send()

    return pl.pallas_call(
        body, name=name, in_specs=[ANY], out_specs=ANY,
        out_shape=jax.ShapeDtypeStruct((4, r, c), xs.dtype),
        scratch_shapes=[pltpu.SemaphoreType.DMA((6,)), pltpu.SemaphoreType.DMA((6,))],
    )(xs)


def _with_own(g4, own):
    xi, yi, _ = _me()
    pick = lax.broadcasted_iota(jnp.int32, (4,) + (1,) * own.ndim, 0) == 2 * xi + yi
    return jnp.where(pick, own[None], g4)


def _remote(src, dst, send_sems, recv_sems, k, to):
    return pltpu.make_async_remote_copy(src_ref=src, dst_ref=dst, send_sem=send_sems.at[k], recv_sem=recv_sems.at[k],
                                        device_id=to, device_id_type=MESH)


def _side_gather_ici(shards):
    def build(ins, outs, send_sems, recv_sems):
        x, y, cc = _me()
        pairs = []
        for a, (x_ref, o_ref) in enumerate(zip(ins, outs)):
            hr = x_ref.shape[0] // 2
            my_half = x_ref.at[pl.ds(cc * hr, hr), :]
            for j, (px, py) in enumerate(_chips(x, y)):
                mine = o_ref.at[pl.ds((4 * x + 2 * y + cc) * hr, hr), :]
                theirs = o_ref.at[pl.ds((4 * px + 2 * py + cc) * hr, hr), :]
                pairs.append((_remote(my_half, mine, send_sems, recv_sems, 3 * a + j, (px, py, cc)),
                              _remote(my_half, theirs, send_sems, recv_sems, 3 * a + j, (px, py, cc))))
        return pairs

    shapes = [jax.ShapeDtypeStruct((4 * s.shape[0], s.shape[1]), s.dtype) for s in shards]
    return _Side(shards, shapes, 3 * len(shards), build)


def _side_gather_pass(bufs):
    def build(ins, outs, send_sems, recv_sems):
        x, y, cc = _me()
        pairs = []
        for a, o_ref in enumerate(outs):
            hr = o_ref.shape[0] // 8
            for j, (px, py) in enumerate(_chips(x, y)):
                got = o_ref.at[pl.ds((4 * px + 2 * py + cc) * hr, hr), :]
                coming = o_ref.at[pl.ds((4 * px + 2 * py + 1 - cc) * hr, hr), :]
                pairs.append((_remote(got, got, send_sems, recv_sems, 3 * a + j, (x, y, 1 - cc)),
                              _remote(got, coming, send_sems, recv_sems, 3 * a + j, (x, y, 1 - cc))))
        return pairs

    shapes = [jax.ShapeDtypeStruct(b.shape, b.dtype) for b in bufs]
    return _Side(bufs, shapes, 3 * len(bufs), build, aliases={a: a for a in range(len(bufs))})


def _side_swap(srcs, parts):
    def view(s_ref, part, cc):
        if part == "lead":
            return s_ref.at[1 - cc]
        if part == "cols":
            hc = s_ref.shape[1] // 2
            return s_ref.at[:, pl.ds((1 - cc) * hc, hc)]
        return s_ref

    def build(ins, outs, send_sems, recv_sems):
        x, y, cc = _me()
        pairs = []
        for a, (s_ref, o_ref) in enumerate(zip(ins, outs)):
            cp = _remote(view(s_ref, parts[a], cc), o_ref, send_sems, recv_sems, a, (x, y, 1 - cc))
            pairs.append((cp, cp))
        return pairs

    out_shape = {"lead": lambda s: s.shape[1:], "cols": lambda s: (s.shape[0], s.shape[1] // 2), "all": lambda s: s.shape}
    shapes = [jax.ShapeDtypeStruct(out_shape[p](s), s.dtype) for s, p in zip(srcs, parts)]
    return _Side(srcs, shapes, len(srcs), build)


def _side_chips(ts):
    def build(ins, outs, send_sems, recv_sems):
        x, y, cc = _me()
        pairs = []
        for a, (t_ref, r_ref) in enumerate(zip(ins, outs)):
            for j, (px, py) in enumerate(_chips(x, y)):
                cp = _remote(t_ref.at[2 * px + py], r_ref.at[j], send_sems, recv_sems, 3 * a + j, (px, py, cc))
                pairs.append((cp, cp))
        return pairs

    shapes = [jax.ShapeDtypeStruct((3,) + t.shape[1:], t.dtype) for t in ts]
    return _Side(ts, shapes, 3 * len(ts), build)


ELEMENTWISE_BLOCK_BYTES = 2 << 20


def _tile(r, c, itemsize):
    tr, tc = r, c
    while tr * tc * itemsize > ELEMENTWISE_BLOCK_BYTES and tr % 32 == 0:
        tr //= 2
    while tr * tc * itemsize > ELEMENTWISE_BLOCK_BYTES and tc % 256 == 0:
        tc //= 2
    return tr, tc


def _add_own_half(gb, got, sel, name):
    _, _, hr, c = gb.shape
    tr, tc = _tile(hr, c, 4)

    def body(sel_ref, a_ref, b_ref, o_ref):
        o_ref[...] = (a_ref[0].astype(F32) + b_ref[...].astype(F32)).astype(BF16)

    return pl.pallas_call(
        body, name=name,
        grid_spec=pltpu.PrefetchScalarGridSpec(
            num_scalar_prefetch=1, grid=(4, hr // tr, c // tc),
            in_specs=[pl.BlockSpec((1, 1, tr, tc), lambda k, i, j, sel_ref: (sel_ref[0], k, i, j)),
                      pl.BlockSpec((1, tr, tc), lambda k, i, j, sel_ref: (k, i, j))],
            out_specs=pl.BlockSpec((1, tr, tc), lambda k, i, j, sel_ref: (k, i, j))),
        out_shape=jax.ShapeDtypeStruct(got.shape, BF16),
        compiler_params=_params(("parallel", "parallel", "parallel")),
    )(sel, gb, got)


def _add_own_cols(g, got, sel, name):
    r, c = g.shape
    hr, hc = r // 4, c // 2
    tr, tc = _tile(hr, hc, 4)
    nj = hc // tc

    def body(sel_ref, a_ref, b_ref, o_ref):
        o_ref[...] = (a_ref[...].astype(F32) + b_ref[...].astype(F32)).astype(BF16)

    t = pl.pallas_call(
        body, name=name,
        grid_spec=pltpu.PrefetchScalarGridSpec(
            num_scalar_prefetch=1, grid=(r // tr, nj),
            in_specs=[pl.BlockSpec((tr, tc), lambda i, j, sel_ref: (i, sel_ref[0] * nj + j)),
                      pl.BlockSpec((tr, tc), lambda i, j, sel_ref: (i, j))],
            out_specs=pl.BlockSpec((tr, tc), lambda i, j, sel_ref: (i, j))),
        out_shape=jax.ShapeDtypeStruct((r, hc), BF16),
        compiler_params=_params(("parallel", "parallel")),
    )(sel, g, got)
    return t.reshape(4, hr, hc)


def _sum_partials(t, rcv, sel, name):
    _, hr, c = t.shape
    tr, tc = _tile(hr, c, 4)

    def body(sel_ref, t_ref, r_ref, o_ref):
        o_ref[...] = ((t_ref[0].astype(F32) + r_ref[0].astype(F32)) + r_ref[1].astype(F32)) + r_ref[2].astype(F32)

    return pl.pallas_call(
        body, name=name,
        grid_spec=pltpu.PrefetchScalarGridSpec(
            num_scalar_prefetch=1, grid=(hr // tr, c // tc),
            in_specs=[pl.BlockSpec((1, tr, tc), lambda i, j, sel_ref: (sel_ref[0], i, j)),
                      pl.BlockSpec((3, tr, tc), lambda i, j, sel_ref: (0, i, j))],
            out_specs=pl.BlockSpec((tr, tc), lambda i, j, sel_ref: (i, j))),
        out_shape=jax.ShapeDtypeStruct((hr, c), F32),
        compiler_params=_params(("parallel", "parallel")),
    )(sel, t, rcv)


class _GroupReduce:
    def __init__(self):
        self.names, self.gb, self.t, self.fh, self.other = [], [], [], [], []

    def put(self, name, gb):
        self.names.append(name)
        self.gb.append(gb)

    def side(self, stage):
        if stage == 1:
            return _side_swap(self.gb, ["lead" if g.ndim == 4 else "cols" for g in self.gb])
        return _side_chips(self.t) if stage == 2 else _side_swap(self.fh, ["all"] * len(self.fh))

    def done(self, stage, outs):
        x, y, cc = _me()
        if stage == 1:
            sel = jnp.reshape(cc, (1,)).astype(jnp.int32)
            self.t = [(_add_own_half if gb.ndim == 4 else _add_own_cols)(gb, got, sel, "rs_add_" + n)
                      for n, gb, got in zip(self.names, self.gb, outs)]
        elif stage == 2:
            sel = jnp.reshape(2 * x + y, (1,)).astype(jnp.int32)
            self.fh = [_sum_partials(t, rcv, sel, "rs_sum_" + n) for n, t, rcv in zip(self.names, self.t, outs)]
        else:
            self.other = list(outs)

    def result(self):
        return {n: (fh, other) for n, fh, other in zip(self.names, self.fh, self.other)}


def _allreduce_small(v, tag):
    r = v.shape[0]

    def body(v_ref, out_ref, buf_ref, send_sems, recv_sems):
        x, y, cc = _me()
        me = 4 * x + 2 * y + cc
        buf_ref[pl.ds(pl.multiple_of(me * r, 8), r), :] = v_ref[...]
        peers = [(x, y, 1 - cc)] + [(px, py, pc) for (px, py) in _chips(x, y) for pc in (cc, 1 - cc)]
        cps = []
        for k, (px, py, pc) in enumerate(peers):
            mine = buf_ref.at[pl.ds(pl.multiple_of(me * r, 8), r), :]
            cps.append(pltpu.make_async_remote_copy(
                src_ref=v_ref, dst_ref=mine, send_sem=send_sems.at[k], recv_sem=recv_sems.at[k],
                device_id=(px, py, pc), device_id_type=MESH))
        for cp in cps:
            cp.start()
        for cp in cps:
            cp.wait()
        acc = buf_ref[0:r, :]
        for d in range(1, 8):
            acc = acc + buf_ref[d * r:(d + 1) * r, :]
        out_ref[...] = acc

    vm = pl.BlockSpec(memory_space=pltpu.VMEM)
    return pl.pallas_call(
        body, name="allreduce_small_" + tag, in_specs=[vm], out_specs=vm,
        out_shape=jax.ShapeDtypeStruct((r, 128), F32),
        scratch_shapes=[pltpu.VMEM((8 * r, 128), F32), pltpu.SemaphoreType.DMA((7,)), pltpu.SemaphoreType.DMA((7,))],
    )(v)


def _adamw(w, g, m, v, name):
    r, c = w.shape
    tr = r
    c1 = 1.0 / (1.0 - ADAM_B1 ** ADAM_STEP)
    c2 = 1.0 / (1.0 - ADAM_B2 ** ADAM_STEP)

    def body(w_ref, g_ref, m_ref, v_ref, d_ref, mo_ref, vo_ref):
        gg = g_ref[...]
        mn = ADAM_B1 * m_ref[...] + (1.0 - ADAM_B1) * gg
        vn = ADAM_B2 * v_ref[...] + (1.0 - ADAM_B2) * (gg * gg)
        mo_ref[...] = mn
        vo_ref[...] = vn
        d_ref[...] = -ADAM_LR * ((mn * c1) / (jnp.sqrt(vn * c2) + ADAM_EPS) + ADAM_WD * w_ref[...])

    blk = pl.BlockSpec((tr, c), lambda i: (i, 0))
    return pl.pallas_call(
        body, name=name, grid=(r // tr,), in_specs=[blk] * 4, out_specs=[blk] * 3,
        out_shape=[jax.ShapeDtypeStruct((r, c), F32)] * 3,
        compiler_params=_params(("parallel",)),
    )(w, g, m, v)


def _adamw_halves(w, g_own, g_other, m, v, sel, name, axis):
    r, c = w.shape
    tr, tc = _tile(g_own.shape[0], g_own.shape[1], 4)
    nh = (g_own.shape[axis]) // (tr, tc)[axis]
    c1 = 1.0 / (1.0 - ADAM_B1 ** ADAM_STEP)
    c2 = 1.0 / (1.0 - ADAM_B2 ** ADAM_STEP)

    def body(sel_ref, w_ref, go_ref, gx_ref, m_ref, v_ref, d_ref, mo_ref, vo_ref, g_ref):
        mine = (pl.program_id(axis) // nh) == sel_ref[0]
        gg = jnp.where(mine, go_ref[...], gx_ref[...])
        g_ref[...] = gg
        mn = ADAM_B1 * m_ref[...] + (1.0 - ADAM_B1) * gg
        vn = ADAM_B2 * v_ref[...] + (1.0 - ADAM_B2) * (gg * gg)
        mo_ref[...] = mn
        vo_ref[...] = vn
        d_ref[...] = -ADAM_LR * ((mn * c1) / (jnp.sqrt(vn * c2) + ADAM_EPS) + ADAM_WD * w_ref[...])

    blk = pl.BlockSpec((tr, tc), lambda i, j, sel_ref: (i, j))
    if axis == 0:
        half = pl.BlockSpec((tr, tc), lambda i, j, sel_ref: (i % nh, j))
    else:
        half = pl.BlockSpec((tr, tc), lambda i, j, sel_ref: (i, j % nh))
    return pl.pallas_call(
        body, name=name,
        grid_spec=pltpu.PrefetchScalarGridSpec(
            num_scalar_prefetch=1, grid=(r // tr, c // tc),
            in_specs=[blk, half, half, blk, blk], out_specs=[blk] * 4),
        out_shape=[jax.ShapeDtypeStruct((r, c), F32)] * 4,
        compiler_params=_params(("parallel", "parallel")),
    )(sel, w, g_own, g_other, m, v)


WEIGHTS = ['norm_g', 'w_in', 'conv_w', 'w_conv_out', 'mla_q_norm_g', 'w_uq', 'mla_kv_norm_g', 'w_ukv',
           'mla_qn_nope_g', 'mla_qn_rope_g', 'mla_kn_nope_g', 'mla_kn_rope_g', 'w_mla_out', 'mem_norm_g',
           'w_mem_kv', 'mem_qn_g', 'mem_kn_g', 'w_mem_out', 'w_o']
COL_SHARDED = ['w_in', 'w_conv_out', 'w_uq', 'w_ukv', 'w_mem_out']
ROW_SHARDED = ['w_mla_out', 'w_mem_kv', 'w_o']
SMALL = ['norm_g', 'conv_w', 'mla_q_norm_g', 'mla_kv_norm_g', 'mla_qn_nope_g', 'mla_qn_rope_g', 'mla_kn_nope_g',
         'mla_kn_rope_g', 'mem_norm_g', 'mem_qn_g', 'mem_kn_g']
SMALL_SIZES = [2048, 3072, 512, 512, 128, 64, 128, 64, 2048, 256, 256]
PACK_ROWS = 72


def _full_from_shards(name, g4):
    if name in COL_SHARDED:
        return g4.transpose(1, 0, 2).reshape(g4.shape[1], 4 * g4.shape[2])
    return g4.reshape(4 * g4.shape[1], g4.shape[2])


def _w_all_t_from_w_in_t(w_in_t):
    conv, small, mz, memq, gates = (w_in_t[0:4096], w_in_t[4096:5184], w_in_t[5184:7232],
                                    w_in_t[7232:9280], w_in_t[9280:15424])
    return jnp.concatenate([gates, mz, conv, memq, small, jnp.zeros((2048 - 1088, D_MODEL), BF16)], axis=0)


def _pad_w_uq(w_uq):
    return jnp.pad(w_uq.reshape(512, MLA_HEADS, 192), ((0, 0), (0, 0), (0, 64))).reshape(512, 4096)


def _grad_halves_by_owner(name, g):
    if name == "w_all_t":
        dw_g, dw_z, dw_c, dw_m, dw_s = g
        return "w_in", jnp.concatenate([dw_c, dw_s[:1088], dw_z, dw_m, dw_g], axis=0)
    if name == "w_uq":
        g = g.reshape(512, MLA_HEADS, HEAD_PAD)[:, :, :192].reshape(512, 3072)
    r, c = g.shape
    if name in COL_SHARDED:
        return name, g.reshape(2, r // 2, 4, c // 4).transpose(0, 2, 1, 3)
    return name, g.reshape(4, 2, r // 8, c).transpose(1, 0, 2, 3)


LATE_WEIGHTS = ['w_conv_out', 'w_uq', 'w_ukv', 'w_mem_out', 'w_mla_out', 'w_mem_kv', 'w_o']
HOSTS = {"proj_fwd": ("gather", 1), "lora_fwd": ("gather", 2),
         "da_conv": ("g1", 1), "q_prep_bwd": ("g1", 2), "k_prep_bwd": ("g1", 3),
         "dw_in_0": ("g2", 1), "dw_in_2": ("g2", 2), "dw_in_4": ("g2", 3),
         "dh_a": ("g3", 1), "dh_b": ("g3", 2), "dx_bwd": ("g3", 3)}
GROUP_OF = {"w_o": "g1", "w_conv_out": "g1", "w_mla_out": "g1", "w_mem_out": "g1",
            "w_uq": "g2", "w_ukv": "g2", "w_mem_kv": "g2", "w_in": "g3"}


class _Comm:
    def __init__(self, shards, w_in_t_full, conv_full):
        self.shards = shards
        self.w = {"w_all_t": _w_all_t_from_w_in_t(w_in_t_full), "conv_w": conv_full}
        self.bufs = None
        self.groups = {"g1": _GroupReduce(), "g2": _GroupReduce(), "g3": _GroupReduce()}

    def weight(self, name):
        return self.w[name]

    def side(self, tag):
        if tag not in HOSTS:
            return None
        kind, stage = HOSTS[tag]
        if kind == "gather":
            return _side_gather_ici([self.shards[n] for n in LATE_WEIGHTS]) if stage == 1 else \
                _side_gather_pass(self.bufs)
        return self.groups[kind].side(stage)

    def done(self, tag, outs):
        kind, stage = HOSTS[tag]
        if kind != "gather":
            self.groups[kind].done(stage, outs)
        elif stage == 1:
            self.bufs = list(outs)
        else:
            for n, buf in zip(LATE_WEIGHTS, outs):
                own = self.shards[n]
                self.w[n] = _full_from_shards(n, _with_own(buf.reshape((4,) + own.shape), own))
            self.w["w_uq"] = _pad_w_uq(self.w["w_uq"])

    def put(self, name, g):
        name, gb = _grad_halves_by_owner(name, g)
        self.groups[GROUP_OF[name]].put(name, gb)

    def reduced(self):
        out = {}
        for grp in self.groups.values():
            out.update(grp.result())
        return out


def kernel(x, positions, mem, norm_g, w_in, conv_w, w_conv_out, mla_q_norm_g, w_uq, mla_kv_norm_g, w_ukv, mla_qn_nope_g, mla_qn_rope_g, mla_kn_nope_g, mla_kn_rope_g, w_mla_out, mem_norm_g, w_mem_kv, mem_qn_g, mem_kn_g, w_mem_out, w_o, loss_target, m_norm_g, m_w_in, m_conv_w, m_w_conv_out, m_mla_q_norm_g, m_w_uq, m_mla_kv_norm_g, m_w_ukv, m_mla_qn_nope_g, m_mla_qn_rope_g, m_mla_kn_nope_g, m_mla_kn_rope_g, m_w_mla_out, m_mem_norm_g, m_w_mem_kv, m_mem_qn_g, m_mem_kn_g, m_w_mem_out, m_w_o, v_norm_g, v_w_in, v_conv_w, v_w_conv_out, v_mla_q_norm_g, v_w_uq, v_mla_kv_norm_g, v_w_ukv, v_mla_qn_nope_g, v_mla_qn_rope_g, v_mla_kn_nope_g, v_mla_kn_rope_g, v_w_mla_out, v_mem_norm_g, v_w_mem_kv, v_mem_qn_g, v_mem_kn_g, v_w_mem_out, v_w_o):
    args = locals()
    w = {n: args[n][0] for n in WEIGHTS}
    m1 = {n: args["m_" + n][0] for n in WEIGHTS}
    v2 = {n: args["v_" + n][0] for n in WEIGHTS}
    xi, yi, ci = _me()
    chip = 2 * xi + yi

    conv_slot = jnp.zeros((3, 4, 256), F32).at[:, chip, :].set(w["conv_w"] * jnp.where(ci == 0, 1.0, 0.0))
    pre = jnp.zeros((PACK_ROWS * 128,), F32).at[0:3072].set(conv_slot.reshape(3072))
    conv_full = _allreduce_small(pre.reshape(PACK_ROWS, 128), "conv_w").reshape(-1)[0:3072].reshape(3, 1024)

    w_in_t = w["w_in"].T.astype(BF16)
    w_in_t_full = _with_own(_allgather_shards(w_in_t, "ag_w_in"), w_in_t).reshape(4 * w_in_t.shape[0], D_MODEL)
    comm = _Comm({n: w[n].astype(BF16) for n in LATE_WEIGHTS}, w_in_t_full, conv_full)
    gains = {n: w[n].reshape(1, -1) for n in SMALL if n != "conv_w"}

    loss_parts, grad_x, gsmall = _local_step(x[0], positions[0], mem[0], loss_target[0], comm, gains)

    loss_local = 0.5 * jnp.sum(loss_parts) * (1.0 / D_MODEL)
    flat = jnp.concatenate([gsmall[n].reshape(-1) for n in SMALL] + [loss_local.reshape(1)])
    flat = jnp.pad(flat, (0, PACK_ROWS * 128 - flat.shape[0]))
    tot = _allreduce_small(flat.reshape(PACK_ROWS, 128), "grads").reshape(-1)
    grads = {}
    off = 0
    for n, sz in zip(SMALL, SMALL_SIZES):
        grads[n] = tot[off:off + sz]
        off += sz
    loss = tot[off]
    grads["conv_w"] = lax.dynamic_slice(grads["conv_w"].reshape(3, 1024), (0, chip * 256), (3, 256))
    for n in SMALL:
        grads[n] = grads[n].reshape(w[n].shape)

    deltas, new_m, new_v = {}, {}, {}
    for n in SMALL:
        shp = w[n].shape
        two_d = (lambda a: a.reshape(1, -1)) if len(shp) == 1 else (lambda a: a)
        d, mn, vn = _adamw(two_d(w[n]), two_d(grads[n]), two_d(m1[n]), two_d(v2[n]), "adamw_" + n)
        deltas[n], new_m[n], new_v[n] = d.reshape(shp), mn.reshape(shp), vn.reshape(shp)

    sel_c = jnp.reshape(ci, (1,)).astype(jnp.int32)
    reduced = comm.reduced()
    for n in COL_SHARDED + ROW_SHARDED:
        g_own, g_other = reduced[n]
        if n == "w_in":
            outs = _adamw_halves(w[n].T, g_own, g_other, m1[n].T, v2[n].T, sel_c, "adamw_" + n, 1)
            deltas[n], new_m[n], new_v[n], grads[n] = [o.T for o in outs]
        else:
            deltas[n], new_m[n], new_v[n], grads[n] = _adamw_halves(w[n], g_own, g_other, m1[n], v2[n], sel_c,
                                                                     "adamw_" + n, 0)

    lead = lambda a: a[None]
    return (loss, grad_x[None], *[lead(grads[n]) for n in WEIGHTS], *[lead(deltas[n]) for n in WEIGHTS],
            *[lead(new_m[n]) for n in WEIGHTS], *[lead(new_v[n]) for n in WEIGHTS])
```

```python
import functools

import numpy as np
import jax
import jax.numpy as jnp
from jax import lax
from jax.experimental import pallas as pl
from jax.experimental.pallas import tpu as pltpu

F32 = jnp.float32
BF16 = jnp.bfloat16
MESH = pl.DeviceIdType.MESH

D_MODEL = 2048
EPS = 1e-6
CHUNK = 64
MLA_HEADS = 16
QK_NOPE = 128
QK_ROPE = 64
HEAD_PAD = 256
MEM_TOKENS = 256
MEM_HEADS = 4
MEM_HEAD_DIM = 256
ROPE_THETA = 10000.0
MLA_SCALE = (QK_NOPE + QK_ROPE) ** -0.5
MEM_SCALE = MEM_HEAD_DIM ** -0.5
LN2 = 0.6931471805599453
Q_SCALE = MLA_SCALE / LN2
NEG = -1e30

ADAM_LR = 0.001
ADAM_B1 = 0.9
ADAM_B2 = 0.999
ADAM_EPS = 1e-08
ADAM_WD = 0.01
ADAM_STEP = 10

NP = 16384
COL_G, COL_Z, COL_C, COL_M, COL_S = 0, 6144, 8192, 12288, 14336

TM_PROJ = 1024
TN_PROJ = 1024
TS_ROW = 512
TS_BWD = 256
TQ_ATT = 1024
TK_ATT = 1024
TR_ATT = 256
FWD_HEADS_PER_STEP = 2
VMEM_LIMIT = 56 * 1024 * 1024

NT_DIMS = (((1,), (1,)), ((), ()))
TN_DIMS = (((0,), (0,)), ((), ()))


def _dot(a, b, mode="nn"):
    if mode == "nn":
        return jnp.dot(a, b, preferred_element_type=F32)
    return lax.dot_general(a, b, NT_DIMS if mode == "nt" else TN_DIMS, preferred_element_type=F32)


def _sigmoid(z):
    return 1.0 / (1.0 + jnp.exp(-z))


def _params(sem=None, vmem=VMEM_LIMIT):
    return pltpu.CompilerParams(dimension_semantics=sem, vmem_limit_bytes=vmem)


def _fold8(v):
    r, c = v.shape
    return v.reshape(r // 8, 8, c).sum(axis=0)


def _swap32(t):
    lane = lax.broadcasted_iota(jnp.int32, t.shape, 1)
    return jnp.where(lane < 32, pltpu.roll(t, 96, 1), pltpu.roll(t, 32, 1))


def _rope(t, cs, sg):
    return t * cs + _swap32(t) * sg


def _rope_bwd(d, cs, sg):
    return d * cs + _swap32(d * sg)


def _rms_fwd(xf, g, n):
    r = lax.rsqrt(jnp.sum(xf * xf, axis=-1, keepdims=True) * (1.0 / n) + EPS)
    return xf * r * g


def _rms_bwd(xf, g, dy, n):
    r = lax.rsqrt(jnp.sum(xf * xf, axis=-1, keepdims=True) * (1.0 / n) + EPS)
    xhat = xf * r
    dyg = dy * g
    dx = r * (dyg - xhat * (jnp.sum(dyg * xhat, axis=-1, keepdims=True) * (1.0 / n)))
    return dx, dy * xhat


ANY = pl.BlockSpec(memory_space=pl.ANY)


class _Side:
    def __init__(self, inputs, out_shapes, n_copies, build, aliases=()):
        self.inputs, self.out_shapes, self.n_copies = list(inputs), list(out_shapes), n_copies
        self.build, self.aliases = build, dict(aliases)


def _pcall(body, side, name, grid, in_specs, out_specs, out_shape, args, sem, scratch_shapes=(), prefetch=()):
    n_pre, n_in, n_out = len(prefetch), len(in_specs), len(out_specs)
    if side is None:
        outs = pl.pallas_call(
            body, name=name,
            grid_spec=pltpu.PrefetchScalarGridSpec(num_scalar_prefetch=n_pre, grid=grid, in_specs=list(in_specs),
                                                   out_specs=list(out_specs), scratch_shapes=list(scratch_shapes)),
            out_shape=list(out_shape), compiler_params=_params(sem),
        )(*prefetch, *args)
        return list(outs), []
    ns_in, ns_out = len(side.inputs), len(side.out_shapes)

    def wrapped(*refs):
        pre, r = refs[:n_pre], refs[n_pre:]
        ins, s_ins = r[:n_in], r[n_in:n_in + ns_in]
        outs = r[n_in + ns_in:n_in + ns_in + n_out]
        s_outs = r[n_in + ns_in + n_out:n_in + ns_in + n_out + ns_out]
        scr, send_sems, recv_sems = r[n_in + ns_in + n_out + ns_out:-2], r[-2], r[-1]
        pairs = side.build(s_ins, s_outs, send_sems, recv_sems)
        first = pl.program_id(0) == 0
        last = pl.program_id(0) == grid[0] - 1
        for d in range(1, len(grid)):
            first = first & (pl.program_id(d) == 0)
            last = last & (pl.program_id(d) == grid[d] - 1)

        @pl.when(first)
        def _():
            for send, _ in pairs:
                send.start()

        body(*pre, *ins, *outs, *scr)

        @pl.when(last)
        def _():
            for send, recv in pairs:
                send.wait_send()
                recv.wait_recv()

    outs = pl.pallas_call(
        wrapped, name=name,
        grid_spec=pltpu.PrefetchScalarGridSpec(
            num_scalar_prefetch=n_pre, grid=grid, in_specs=list(in_specs) + [ANY] * ns_in,
            out_specs=list(out_specs) + [ANY] * ns_out,
            scratch_shapes=list(scratch_shapes) + [pltpu.SemaphoreType.DMA((side.n_copies,)),
                                                   pltpu.SemaphoreType.DMA((side.n_copies,))]),
        out_shape=list(out_shape) + side.out_shapes,
        input_output_aliases={n_pre + n_in + i: n_out + o for i, o in side.aliases.items()},
        compiler_params=_params(("arbitrary",) * len(grid)),
    )(*prefetch, *args, *side.inputs)
    return list(outs[:n_out]), list(outs[n_out:])


def _matmul(a, b, mode, out_dtype, name, tm=1024, tn=1024, tk=1024, side=None):
    if mode == "nn":
        (m, k), (_, n) = a.shape, b.shape
    elif mode == "nt":
        (m, k), (n, _) = a.shape, b.shape
    else:
        (k, m), (_, n) = a.shape, b.shape
    tm, tn, tk = min(tm, m), min(tn, n), min(tk, k)
    nk = k // tk
    if mode == "tn":
        a_spec = pl.BlockSpec((tk, tm), lambda i, j, kk: (kk, i))
    else:
        a_spec = pl.BlockSpec((tm, tk), lambda i, j, kk: (i, kk))
    if mode == "nt":
        b_spec = pl.BlockSpec((tn, tk), lambda i, j, kk: (j, kk))
    else:
        b_spec = pl.BlockSpec((tk, tn), lambda i, j, kk: (kk, j))

    def body(a_ref, b_ref, o_ref, acc_ref):
        kk = pl.program_id(2)

        @pl.when(kk == 0)
        def _():
            acc_ref[...] = jnp.zeros_like(acc_ref)

        acc_ref[...] += _dot(a_ref[...].astype(BF16), b_ref[...].astype(BF16), mode)

        @pl.when(kk == nk - 1)
        def _():
            o_ref[...] = acc_ref[...].astype(out_dtype)

    outs, extra = _pcall(
        body, side, name, (m // tm, n // tn, nk), [a_spec, b_spec],
        [pl.BlockSpec((tm, tn), lambda i, j, kk: (i, j))], [jax.ShapeDtypeStruct((m, n), out_dtype)], (a, b),
        ("parallel", "parallel", "arbitrary"), scratch_shapes=[pltpu.VMEM((tm, tn), F32)])
    return outs[0] if side is None else (outs[0], extra)


def _proj_fwd(x, g, w_all_t, side=None):
    s = x.shape[0]
    tm, tn = min(TM_PROJ, s), TN_PROJ

    def body(x_ref, g_ref, w_ref, p_ref, h_ref):
        @pl.when(pl.program_id(1) == 0)
        def _():
            h_ref[...] = _rms_fwd(x_ref[...], g_ref[...], D_MODEL).astype(BF16)

        p_ref[...] = _dot(h_ref[...], w_ref[...], "nt").astype(BF16)

    return _pcall(
        body, side, "proj_fwd", (s // tm, NP // tn),
        [pl.BlockSpec((tm, D_MODEL), lambda i, j: (i, 0)),
         pl.BlockSpec((1, D_MODEL), lambda i, j: (0, 0)),
         pl.BlockSpec((tn, D_MODEL), lambda i, j: (j, 0))],
        [pl.BlockSpec((tm, tn), lambda i, j: (i, j)), pl.BlockSpec((tm, D_MODEL), lambda i, j: (i, 0))],
        [jax.ShapeDtypeStruct((s, NP), BF16), jax.ShapeDtypeStruct((s, D_MODEL), BF16)], (x, g, w_all_t),
        ("parallel", "arbitrary"))


def _shift_rows(v, carry, j, ts, back):
    if back:
        main = pltpu.roll(v, j, 0)
        edge = pltpu.roll(jnp.concatenate([carry, v[:8]], axis=0), j, 0)[8:]
        return jnp.concatenate([edge, main[8:]], axis=0)
    main = pltpu.roll(v, ts - j, 0)
    edge = pltpu.roll(jnp.concatenate([v[ts - 8:], carry], axis=0), 16 - j, 0)[:8]
    return jnp.concatenate([main[:ts - 8], edge], axis=0)


def _conv_fwd(p_all, conv_w):
    s = p_all.shape[0]
    ts = min(TS_ROW, s)
    c0 = COL_C // 1024

    def body(cg_ref, bg_ref, u_ref, z_ref, w_ref, a_ref, co_ref, carry_ref):
        @pl.when(pl.program_id(0) == 0)
        def _():
            carry_ref[...] = jnp.zeros_like(carry_ref)

        p = cg_ref[...].astype(F32) * u_ref[...].astype(F32)
        carry = carry_ref[...]
        co = (w_ref[2:3, :] * p + w_ref[1:2, :] * _shift_rows(p, carry, 1, ts, True)
              + w_ref[0:1, :] * _shift_rows(p, carry, 2, ts, True))
        carry_ref[...] = p[ts - 8:]
        z = z_ref[...].astype(F32)
        a_ref[...] = (bg_ref[...].astype(F32) * co * (z * _sigmoid(z))).astype(BF16)
        co_ref[...] = co.astype(BF16)

    seg = lambda c: pl.BlockSpec((ts, 1024), lambda i: (i, c0 + c))
    return pl.pallas_call(
        body, name="conv_fwd", grid=(s // ts,),
        in_specs=[seg(0), seg(1), seg(2), seg(3), pl.BlockSpec((3, 1024), lambda i: (0, 0))],
        out_specs=[pl.BlockSpec((ts, 1024), lambda i: (i, 0))] * 2,
        out_shape=[jax.ShapeDtypeStruct((s, 1024), BF16)] * 2,
        scratch_shapes=[pltpu.VMEM((8, 1024), F32)],
        compiler_params=_params(("arbitrary",)),
    )(p_all, p_all, p_all, p_all, conv_w)


def _lora_fwd(p_all, gq, gkv, gkr, cs, sg, side=None):
    s = p_all.shape[0]
    ts = min(TS_ROW, s)

    def body(cq_ref, ckv_ref, kr_ref, gq_ref, gkv_ref, gkr_ref, cs_ref, sg_ref, cqn_ref, ckvn_ref, krope_ref):
        cqn_ref[...] = _rms_fwd(cq_ref[...].astype(F32), gq_ref[...], 512).astype(BF16)
        ckvn_ref[...] = _rms_fwd(ckv_ref[...].astype(F32), gkv_ref[...], 512).astype(BF16)
        kn = _rms_fwd(kr_ref[...].astype(F32), gkr_ref[...], QK_ROPE)
        krope_ref[...] = _rope(kn, cs_ref[...], sg_ref[...]).astype(BF16)

    row = lambda w: pl.BlockSpec((ts, w), lambda i: (i, 0))
    vec = lambda w: pl.BlockSpec((1, w), lambda i: (0, 0))
    return _pcall(
        body, side, "lora_fwd", (s // ts,),
        [pl.BlockSpec((ts, 512), lambda i: (i, COL_S // 512)),
         pl.BlockSpec((ts, 512), lambda i: (i, COL_S // 512 + 1)),
         pl.BlockSpec((ts, 128), lambda i: (i, (COL_S + 1024) // 128)),
         vec(512), vec(512), vec(128), row(128), row(128)],
        [row(512), row(512), row(128)],
        [jax.ShapeDtypeStruct((s, 512), BF16), jax.ShapeDtypeStruct((s, 512), BF16),
         jax.ShapeDtypeStruct((s, 128), BF16)],
        (p_all, p_all, p_all, gq, gkv, gkr, cs, sg), ("parallel",))


UP_HEADS = 4


def _q_up(cqn, w_uq, gn, gr, cs, sg):
    s = cqn.shape[0]
    tm, tn = min(TS_ROW, s), UP_HEADS * HEAD_PAD

    def body(c_ref, w_ref, gn_ref, gr_ref, cs_ref, sg_ref, q_ref, o_ref):
        qb = _dot(c_ref[...], w_ref[...]).astype(BF16)
        q_ref[...] = qb
        for h in range(UP_HEADS):
            q = qb[:, h * HEAD_PAD:(h + 1) * HEAD_PAD].astype(F32)
            a = _rms_fwd(q[:, :128], gn_ref[...], QK_NOPE)
            b = _rope(_rms_fwd(q[:, 128:], gr_ref[...], QK_ROPE), cs_ref[...], sg_ref[...])
            o_ref[h] = (jnp.concatenate([a, b], axis=1) * Q_SCALE).astype(BF16)

    vec = pl.BlockSpec((1, 128), lambda i, j: (0, 0))
    tab = pl.BlockSpec((tm, 128), lambda i, j: (i, 0))
    return pl.pallas_call(
        body, name="q_up", grid=(s // tm, MLA_HEADS // UP_HEADS),
        in_specs=[pl.BlockSpec((tm, 512), lambda i, j: (i, 0)), pl.BlockSpec((512, tn), lambda i, j: (0, j)),
                  vec, vec, tab, tab],
        out_specs=[pl.BlockSpec((tm, tn), lambda i, j: (i, j)),
                   pl.BlockSpec((UP_HEADS, tm, HEAD_PAD), lambda i, j: (j, i, 0))],
        out_shape=[jax.ShapeDtypeStruct((s, MLA_HEADS * HEAD_PAD), BF16),
                   jax.ShapeDtypeStruct((MLA_HEADS, s, HEAD_PAD), BF16)],
        compiler_params=_params(("parallel", "parallel")),
    )(cqn, w_uq, gn, gr, cs, sg)


def _kv_up(ckvn, w_ukv, krope, gk):
    s = ckvn.shape[0]
    tm, tn = min(TS_ROW, s), UP_HEADS * HEAD_PAD

    def body(c_ref, w_ref, kr_ref, gk_ref, kv_ref, o_ref, vt_ref):
        kvb = _dot(c_ref[...], w_ref[...]).astype(BF16)
        kv_ref[...] = kvb
        for h in range(UP_HEADS):
            a = _rms_fwd(kvb[:, h * HEAD_PAD:h * HEAD_PAD + 128].astype(F32), gk_ref[...], QK_NOPE)
            o_ref[h] = jnp.concatenate([a.astype(BF16), kr_ref[...]], axis=1)
            vt_ref[h] = kvb[:, h * HEAD_PAD + 128:(h + 1) * HEAD_PAD].astype(F32).T.astype(BF16)

    return pl.pallas_call(
        body, name="kv_up", grid=(s // tm, MLA_HEADS // UP_HEADS),
        in_specs=[pl.BlockSpec((tm, 512), lambda i, j: (i, 0)), pl.BlockSpec((512, tn), lambda i, j: (0, j)),
                  pl.BlockSpec((tm, 128), lambda i, j: (i, 0)), pl.BlockSpec((1, 128), lambda i, j: (0, 0))],
        out_specs=[pl.BlockSpec((tm, tn), lambda i, j: (i, j)),
                   pl.BlockSpec((UP_HEADS, tm, HEAD_PAD), lambda i, j: (j, i, 0)),
                   pl.BlockSpec((UP_HEADS, 128, tm), lambda i, j: (j, 0, i))],
        out_shape=[jax.ShapeDtypeStruct((s, MLA_HEADS * HEAD_PAD), BF16),
                   jax.ShapeDtypeStruct((MLA_HEADS, s, HEAD_PAD), BF16),
                   jax.ShapeDtypeStruct((MLA_HEADS, 128, s), BF16)],
        compiler_params=_params(("parallel", "parallel")),
    )(ckvn, w_ukv, krope, gk)


def _chunk_mask(tq, tk, row0=0):
    r = (lax.broadcasted_iota(jnp.int32, (tq, tk), 0) + row0) // CHUNK
    c = lax.broadcasted_iota(jnp.int32, (tq, tk), 1) // CHUNK
    return c <= r


def _causal_pairs(n, by_key):
    if by_key:
        pairs = [(i, j) for j in range(n) for i in range(j, n)]
    else:
        pairs = [(i, j) for i in range(n) for j in range(i + 1)]
    return (jnp.asarray([p[0] for p in pairs], jnp.int32), jnp.asarray([p[1] for p in pairs], jnp.int32))


def _flash_fwd(qcat, kcat, vt, p_all):
    s = qcat.shape[1]
    tq = tk = min(TQ_ATT, s)
    tc = min(TR_ATT, tq)
    nq = s // tq
    hg = FWD_HEADS_PER_STEP
    zc = COL_Z // (128 * hg)
    qi, kj = _causal_pairs(nq, False)

    def body(qi_ref, kj_ref, q_ref, k_ref, vt_ref, z_ref, y_ref, a_ref, lse_ref, m_sc, l_sc, acc_sc):
        t = pl.program_id(1)
        i, j = qi_ref[t], kj_ref[t]

        @pl.when(j == 0)
        def _():
            m_sc[...] = jnp.full_like(m_sc, NEG)
            l_sc[...] = jnp.zeros_like(l_sc)
            acc_sc[...] = jnp.zeros_like(acc_sc)

        def step(masked):
            nc = tq // tc
            state = {(g, c): [m_sc[g, :, c * tc:(c + 1) * tc], l_sc[g, :, c * tc:(c + 1) * tc],
                              acc_sc[g, :, c * tc:(c + 1) * tc]] for g in range(hg) for c in range(nc)}
            units = [(g, u, c) for u in range(tk // tc) for c in range(nc) for g in range(hg)
                     if (u <= c or not masked)]

            def scores(g, u, c):
                return _dot(k_ref[g, u * tc:(u + 1) * tc, :], q_ref[g, c * tc:(c + 1) * tc, :], "nt")

            def weighted_values(g, u, c, alpha, pb):
                state[g, c][2] = alpha * state[g, c][2] + _dot(vt_ref[g, :, u * tc:(u + 1) * tc], pb)

            ahead = 4
            pending = [scores(*un) for un in units[:ahead]]
            late = None
            for n, (g, u, c) in enumerate(units):
                st = pending.pop(0)
                if n + ahead < len(units):
                    pending.append(scores(*units[n + ahead]))
                if masked and u == c:
                    kc = lax.broadcasted_iota(jnp.int32, (tc, tc), 0) // CHUNK
                    qc = lax.broadcasted_iota(jnp.int32, (tc, tc), 1) // CHUNK
                    st = jnp.where(kc <= qc, st, NEG)
                m_run, l_run, _ = state[g, c]
                m_new = jnp.maximum(m_run, jnp.max(st, axis=0, keepdims=True))
                alpha = jnp.exp2(m_run - m_new)
                p = jnp.exp2(st - m_new)
                state[g, c][0] = m_new
                state[g, c][1] = alpha * l_run + jnp.sum(p, axis=0, keepdims=True)
                if late is not None:
                    weighted_values(*late)
                late = (g, u, c, alpha, p.astype(BF16))
            weighted_values(*late)
            for g in range(hg):
                m_sc[g] = jnp.concatenate([state[g, c][0] for c in range(nc)], axis=1)
                l_sc[g] = jnp.concatenate([state[g, c][1] for c in range(nc)], axis=1)
                acc_sc[g] = jnp.concatenate([state[g, c][2] for c in range(nc)], axis=1)

        @pl.when(j < i)
        def _():
            step(False)

        @pl.when(j == i)
        def _():
            step(True)
            for g in range(hg):
                cols = slice(g * 128, (g + 1) * 128)
                y = (acc_sc[g] * (1.0 / l_sc[g])).T
                z = z_ref[:, cols].astype(F32)
                y_ref[:, cols] = y.astype(BF16)
                a_ref[:, cols] = (y * (z * _sigmoid(z))).astype(BF16)
                lse2 = m_sc[g] + jnp.log(l_sc[g]) * (1.0 / LN2)
                lse_ref[g] = jnp.broadcast_to(lse2, (128, tq)).T

    wide = pl.BlockSpec((tq, 128 * hg), lambda h, t, qi_ref, kj_ref: (qi_ref[t], h))
    return pl.pallas_call(
        body, name="flash_fwd",
        grid_spec=pltpu.PrefetchScalarGridSpec(
            num_scalar_prefetch=2, grid=(MLA_HEADS // hg, qi.shape[0]),
            in_specs=[pl.BlockSpec((hg, tq, HEAD_PAD), lambda h, t, qi_ref, kj_ref: (h, qi_ref[t], 0)),
                      pl.BlockSpec((hg, tk, HEAD_PAD), lambda h, t, qi_ref, kj_ref: (h, kj_ref[t], 0)),
                      pl.BlockSpec((hg, 128, tk), lambda h, t, qi_ref, kj_ref: (h, 0, kj_ref[t])),
                      pl.BlockSpec((tq, 128 * hg), lambda h, t, qi_ref, kj_ref: (qi_ref[t], zc + h))],
            out_specs=[wide, wide,
                       pl.BlockSpec((hg, tq, 128), lambda h, t, qi_ref, kj_ref: (h, qi_ref[t], 0))],
            scratch_shapes=[pltpu.VMEM((hg, 1, tq), F32), pltpu.VMEM((hg, 1, tq), F32),
                            pltpu.VMEM((hg, 128, tq), F32)]),
        out_shape=[jax.ShapeDtypeStruct((s, MLA_HEADS * 128), BF16),
                   jax.ShapeDtypeStruct((s, MLA_HEADS * 128), BF16),
                   jax.ShapeDtypeStruct((MLA_HEADS, s, 128), F32)],
        compiler_params=_params(("parallel", "arbitrary")),
    )(qi, kj, qcat, kcat, vt, p_all)


def _mem_kv_fwd(mem, g_norm, w_kv, g_kn):
    m = mem.shape[0]

    def body(mem_ref, g_ref, w_ref, gk_ref, memn_ref, kraw_ref, kn_ref, v_ref):
        memn = _rms_fwd(mem_ref[...], g_ref[...], D_MODEL).astype(BF16)
        memn_ref[...] = memn
        kvm = _dot(memn, w_ref[...])
        kraw_ref[...] = kvm[:, :1024]
        v_ref[...] = kvm[:, 1024:].astype(BF16)
        for hh in range(MEM_HEADS):
            sl = slice(hh * MEM_HEAD_DIM, (hh + 1) * MEM_HEAD_DIM)
            kn_ref[:, sl] = _rms_fwd(kvm[:, sl], gk_ref[...], MEM_HEAD_DIM).astype(BF16)

    return pl.pallas_call(
        body, name="mem_kv_fwd",
        out_shape=[jax.ShapeDtypeStruct((m, D_MODEL), BF16), jax.ShapeDtypeStruct((m, 1024), F32),
                   jax.ShapeDtypeStruct((m, 1024), BF16), jax.ShapeDtypeStruct((m, 1024), BF16)],
        compiler_params=_params(),
    )(mem, g_norm, w_kv, g_kn)


def _mem_attn_fwd(p_all, kn, v, gq):
    s = p_all.shape[0]
    tq = min(TS_ROW, s)

    def body(q_ref, z_ref, kn_ref, v_ref, gq_ref, a_ref):
        z = z_ref[...].astype(F32)
        gate = z * _sigmoid(z)
        for hh in range(MEM_HEADS):
            sl = slice(hh * MEM_HEAD_DIM, (hh + 1) * MEM_HEAD_DIM)
            qn = _rms_fwd(q_ref[:, sl].astype(F32), gq_ref[...], MEM_HEAD_DIM).astype(BF16)
            sc = _dot(qn, kn_ref[:, sl], "nt") * MEM_SCALE
            e = jnp.exp(sc - jnp.max(sc, axis=-1, keepdims=True))
            p = e * (1.0 / jnp.sum(e, axis=-1, keepdims=True))
            y = _dot(p.astype(BF16), v_ref[:, sl])
            a_ref[:, sl] = (y * gate[:, sl]).astype(BF16)

    full = lambda a: pl.BlockSpec(a.shape, lambda i: (0, 0))
    return pl.pallas_call(
        body, name="mem_attn_fwd", grid=(s // tq,),
        in_specs=[pl.BlockSpec((tq, 1024), lambda i: (i, COL_M // 1024)),
                  pl.BlockSpec((tq, 1024), lambda i: (i, COL_M // 1024 + 1)),
                  full(kn), full(v), full(gq)],
        out_specs=pl.BlockSpec((tq, 1024), lambda i: (i, 0)),
        out_shape=jax.ShapeDtypeStruct((s, 1024), BF16),
        compiler_params=_params(("parallel",)),
    )(p_all, p_all, kn, v, gq)


def _merge_fwd(a_conv, w_c, a_mla, w_m, a_mem, w_e, p_all):
    s = a_conv.shape[0]
    tm, tn = min(TS_ROW, s), 1024
    nj = D_MODEL // tn

    def body(ac_ref, wc_ref, am_ref, wm_ref, ae_ref, we_ref, gc_ref, gm_ref, ge_ref,
             oc_ref, om_ref, oe_ref, mg_ref):
        oc = _dot(ac_ref[...], wc_ref[...])
        om = _dot(am_ref[...], wm_ref[...])
        oe = _dot(ae_ref[...], we_ref[...])
        oc_ref[...] = oc.astype(BF16)
        om_ref[...] = om.astype(BF16)
        oe_ref[...] = oe.astype(BF16)
        mg_ref[...] = (_sigmoid(gc_ref[...].astype(F32)) * oc + _sigmoid(gm_ref[...].astype(F32)) * om
                       + _sigmoid(ge_ref[...].astype(F32)) * oe).astype(BF16)

    act = lambda k: pl.BlockSpec((tm, k), lambda i, j: (i, 0))
    wgt = lambda k: pl.BlockSpec((k, tn), lambda i, j: (0, j))
    gate = lambda b: pl.BlockSpec((tm, tn), lambda i, j: (i, b * nj + j))
    out = pl.BlockSpec((tm, tn), lambda i, j: (i, j))
    return pl.pallas_call(
        body, name="merge_fwd", grid=(s // tm, nj),
        in_specs=[act(1024), wgt(1024), act(2048), wgt(2048), act(1024), wgt(1024), gate(0), gate(1), gate(2)],
        out_specs=[out] * 4,
        out_shape=[jax.ShapeDtypeStruct((s, D_MODEL), BF16)] * 4,
        compiler_params=_params(("parallel", "parallel")),
    )(a_conv, w_c, a_mla, w_m, a_mem, w_e, p_all, p_all, p_all)


def _out_fwd(merged, w_o, x, target):
    s = merged.shape[0]
    tm, tn = min(TS_ROW, s), 1024
    nj = D_MODEL // tn

    def body(mg_ref, w_ref, x_ref, t_ref, dy_ref, dyb_ref, ls_ref):
        e = x_ref[...] + _dot(mg_ref[...], w_ref[...]) - t_ref[...]
        dy = e * (1.0 / D_MODEL)
        dy_ref[...] = dy
        dyb_ref[...] = dy.astype(BF16)
        r = _fold8(e * e)
        acc = r[:, 0:128]
        for cc in range(1, tn // 128):
            acc = acc + r[:, cc * 128:(cc + 1) * 128]
        ls_ref[...] = acc

    tile = pl.BlockSpec((tm, tn), lambda i, j: (i, j))
    return pl.pallas_call(
        body, name="out_fwd", grid=(s // tm, nj),
        in_specs=[pl.BlockSpec((tm, D_MODEL), lambda i, j: (i, 0)),
                  pl.BlockSpec((D_MODEL, tn), lambda i, j: (0, j)), tile, tile],
        out_specs=[tile, tile, pl.BlockSpec((8, 128), lambda i, j: (i, j))],
        out_shape=[jax.ShapeDtypeStruct((s, D_MODEL), F32), jax.ShapeDtypeStruct((s, D_MODEL), BF16),
                   jax.ShapeDtypeStruct((s // tm * 8, nj * 128), F32)],
        compiler_params=_params(("parallel", "parallel")),
    )(merged, w_o, x, target)


def _merge_bwd(dyb, w_o, p_all, o_c, o_m, o_e):
    s = dyb.shape[0]
    tm = min(256, s)

    def body(dy_ref, w_ref, g_ref, oc_ref, om_ref, oe_ref, dc_ref, dm_ref, de_ref, dg_ref):
        dmg = _dot(dy_ref[...], w_ref[...], "nt")
        for b, (o_ref, d_ref) in enumerate(((oc_ref, dc_ref), (om_ref, dm_ref), (oe_ref, de_ref))):
            sl = slice(b * D_MODEL, (b + 1) * D_MODEL)
            sg = _sigmoid(g_ref[:, sl].astype(F32))
            d_ref[...] = (dmg * sg).astype(BF16)
            dg_ref[:, sl] = (dmg * o_ref[...].astype(F32) * sg * (1.0 - sg)).astype(BF16)

    row = pl.BlockSpec((tm, D_MODEL), lambda i: (i, 0))
    wide = pl.BlockSpec((tm, 3 * D_MODEL), lambda i: (i, 0))
    return pl.pallas_call(
        body, name="merge_bwd", grid=(s // tm,),
        in_specs=[row, pl.BlockSpec((D_MODEL, D_MODEL), lambda i: (0, 0)), wide, row, row, row],
        out_specs=[row, row, row, wide],
        out_shape=[jax.ShapeDtypeStruct((s, D_MODEL), BF16)] * 3 + [jax.ShapeDtypeStruct((s, 3 * D_MODEL), BF16)],
        compiler_params=_params(("parallel",)),
    )(dyb, w_o, p_all, o_c, o_m, o_e)


def _conv_bwd(da, p_all, co, conv_w):
    s = da.shape[0]
    ts = min(TS_ROW, s)
    n = s // ts
    c0 = COL_C // 1024

    def body(da_ref, cg_ref, bg_ref, u_ref, z_ref, co_ref, w_ref, dp_ref, dw_ref, carry_ref):
        @pl.when(pl.program_id(0) == 0)
        def _():
            carry_ref[...] = jnp.zeros_like(carry_ref)
            dw_ref[...] = jnp.zeros_like(dw_ref)

        da_ = da_ref[...].astype(F32)
        cg, bg = cg_ref[...].astype(F32), bg_ref[...].astype(F32)
        u, z, cov = u_ref[...].astype(F32), z_ref[...].astype(F32), co_ref[...].astype(F32)
        sz = _sigmoid(z)
        dyc = da_ * (z * sz)
        dz = da_ * (bg * cov) * (sz * (1.0 + z * (1.0 - sz)))
        db = dyc * cov
        dco = dyc * bg
        carry = carry_ref[...]
        d1 = _shift_rows(dco, carry, 1, ts, False)
        d2 = _shift_rows(dco, carry, 2, ts, False)
        carry_ref[...] = dco[:8]
        dpp = w_ref[2:3, :] * dco + w_ref[1:2, :] * d1 + w_ref[0:1, :] * d2
        p = cg * u
        dw_ref[0] += _fold8(p * d2)
        dw_ref[1] += _fold8(p * d1)
        dw_ref[2] += _fold8(p * dco)
        dp_ref[...] = jnp.concatenate([dpp * u, db, dpp * cg, dz], axis=1).astype(BF16)

    rev = lambda c: pl.BlockSpec((ts, 1024), lambda i: (n - 1 - i, c))
    return pl.pallas_call(
        body, name="conv_bwd", grid=(n,),
        in_specs=[rev(0), rev(c0), rev(c0 + 1), rev(c0 + 2), rev(c0 + 3), rev(0),
                  pl.BlockSpec((3, 1024), lambda i: (0, 0))],
        out_specs=[pl.BlockSpec((ts, 4096), lambda i: (n - 1 - i, 0)),
                   pl.BlockSpec((3, 8, 1024), lambda i: (0, 0, 0))],
        out_shape=[jax.ShapeDtypeStruct((s, 4096), BF16), jax.ShapeDtypeStruct((3, 8, 1024), F32)],
        scratch_shapes=[pltpu.VMEM((8, 1024), F32)],
        compiler_params=_params(("arbitrary",)),
    )(da, p_all, p_all, p_all, p_all, co, conv_w)


def _mla_gate_bwd(do_m, w_mla_out, y, p_all):
    s = do_m.shape[0]
    tm, tn = min(TS_ROW, s), 1024
    heads = tn // 128

    def body(do_ref, w_ref, y_ref, z_ref, dy_ref, dz_ref, dl_ref):
        da_ = _dot(do_ref[...], w_ref[...], "nt").astype(BF16).astype(F32)
        yv, z = y_ref[...].astype(F32), z_ref[...].astype(F32)
        sz = _sigmoid(z)
        dyv = da_ * (z * sz)
        dy_ref[...] = dyv.astype(BF16)
        dz_ref[...] = (da_ * yv * (sz * (1.0 + z * (1.0 - sz)))).astype(BF16)
        pr = dyv * yv
        for h in range(heads):
            dl_ref[h] = jnp.broadcast_to(jnp.sum(pr[:, h * 128:(h + 1) * 128], axis=-1, keepdims=True), (tm, 128))

    tile = pl.BlockSpec((tm, tn), lambda i, j: (i, j))
    return pl.pallas_call(
        body, name="mla_gate_bwd", grid=(s // tm, D_MODEL // tn),
        in_specs=[pl.BlockSpec((tm, D_MODEL), lambda i, j: (i, 0)), pl.BlockSpec((tn, D_MODEL), lambda i, j: (j, 0)),
                  tile, pl.BlockSpec((tm, tn), lambda i, j: (i, COL_Z // tn + j))],
        out_specs=[tile, tile, pl.BlockSpec((heads, tm, 128), lambda i, j: (j, i, 0))],
        out_shape=[jax.ShapeDtypeStruct((s, D_MODEL), BF16)] * 2 + [jax.ShapeDtypeStruct((MLA_HEADS, s, 128), F32)],
        compiler_params=_params(("parallel", "parallel")),
    )(do_m, w_mla_out, y, p_all)


def _flash_bwd(qcat, kcat, kv, dy, lse, delta):
    s = qcat.shape[1]
    tq = tk = min(TQ_ATT, s)
    tr = min(TR_ATT, tq)
    nq = s // tq
    qi, kj = _causal_pairs(nq, True)
    npairs = qi.shape[0]

    def body(qi_ref, kj_ref, q_ref, k_ref, v_ref, do_ref, lse_ref, dl_ref, dq_ref, dk_ref, dv_ref,
             dq_acc, dk_acc, dv_acc):
        t = pl.program_id(1)
        i, j = qi_ref[t], kj_ref[t]

        def add_dq(rows, dq_new):
            @pl.when(j == 0)
            def _():
                dq_acc[rows, :] = dq_new

            @pl.when(j > 0)
            def _():
                dq_acc[rows, :] += dq_new

        def full_step():
            q, k = q_ref[0], k_ref[0]
            p = jnp.exp2(_dot(q, k, "nt") - lse_ref[0][:, 0:1])
            do = do_ref[...]
            dpv = _dot(do, v_ref[...], "nt")
            ds = (p * (dpv - dl_ref[0][:, 0:1])).astype(BF16)
            dv_acc[...] += _dot(p.astype(BF16), do, "tn")
            dk_acc[...] += _dot(ds, q, "tn")
            add_dq(pl.ds(pl.multiple_of(i * tq, tq), tq), _dot(ds, k))

        def diagonal_step():
            def keys(r):
                return (r + 1) * tr

            def scores(r):
                rows = slice(r * tr, (r + 1) * tr)
                return (_dot(q_ref[0, rows, :], k_ref[0, :keys(r), :], "nt"),
                        _dot(do_ref[rows, :], v_ref[:keys(r), :], "nt"))

            def gradients(n, r, pb, ds):
                nk = keys(r)
                rows = slice(r * tr, (r + 1) * tr)
                dv_new = _dot(pb, do_ref[rows, :], "tn")
                dk_new = _dot(ds, q_ref[0, rows, :], "tn")
                if n == 0:
                    dv_acc[...] = dv_new
                    dk_acc[...] = dk_new
                else:
                    dv_acc[:nk, :] += dv_new
                    dk_acc[:nk, :] += dk_new
                add_dq(pl.ds(pl.multiple_of(i * tq + r * tr, tr), tr), _dot(ds, k_ref[0, :nk, :]))

            order = list(range(tq // tr - 1, -1, -1))
            pending = [scores(r) for r in order[:2]]
            late = None
            for n, r in enumerate(order):
                sc, dpv = pending.pop(0)
                if n + 2 < len(order):
                    pending.append(scores(order[n + 2]))
                rows = slice(r * tr, (r + 1) * tr)
                sc = jnp.where(_chunk_mask(tr, keys(r), r * tr), sc, NEG)
                p = jnp.exp2(sc - lse_ref[0][rows, 0:1])
                ds = (p * (dpv - dl_ref[0][rows, 0:1])).astype(BF16)
                if late is not None:
                    gradients(*late)
                late = (n, r, p.astype(BF16), ds)
            gradients(*late)

        @pl.when(i == j)
        def _():
            diagonal_step()

        @pl.when(i > j)
        def _():
            full_step()

        @pl.when(i == nq - 1)
        def _():
            dk_ref[0] = dk_acc[...].astype(BF16)
            dv_ref[0] = dv_acc[...].astype(BF16)

        @pl.when(t == npairs - 1)
        def _():
            dq_ref[0] = dq_acc[...].astype(BF16)

    qrow = lambda w: pl.BlockSpec((1, tq, w), lambda h, t, qi_ref, kj_ref: (h, qi_ref[t], 0))
    krow = lambda w: pl.BlockSpec((1, tk, w), lambda h, t, qi_ref, kj_ref: (h, kj_ref[t], 0))
    return pl.pallas_call(
        body, name="flash_bwd",
        grid_spec=pltpu.PrefetchScalarGridSpec(
            num_scalar_prefetch=2, grid=(MLA_HEADS, npairs),
            in_specs=[qrow(HEAD_PAD), krow(HEAD_PAD),
                      pl.BlockSpec((tk, 128), lambda h, t, qi_ref, kj_ref: (kj_ref[t], 2 * h + 1)),
                      pl.BlockSpec((tq, 128), lambda h, t, qi_ref, kj_ref: (qi_ref[t], h)),
                      qrow(128), qrow(128)],
            out_specs=[pl.BlockSpec((1, s, HEAD_PAD), lambda h, t, qi_ref, kj_ref: (h, 0, 0)),
                       krow(HEAD_PAD), krow(128)],
            scratch_shapes=[pltpu.VMEM((s, HEAD_PAD), F32), pltpu.VMEM((tk, HEAD_PAD), F32),
                            pltpu.VMEM((tk, 128), F32)]),
        out_shape=[jax.ShapeDtypeStruct((MLA_HEADS, s, HEAD_PAD), BF16),
                   jax.ShapeDtypeStruct((MLA_HEADS, s, HEAD_PAD), BF16),
                   jax.ShapeDtypeStruct((MLA_HEADS, s, 128), BF16)],
        compiler_params=_params(("parallel", "arbitrary")),
    )(qi, kj, qcat, kcat, kv, dy, lse, delta)


def _q_prep_bwd(dqcat, q_raw, gn, gr, cs, sg, side=None):
    s = q_raw.shape[0]
    ts = min(TS_BWD, s)

    def body(dq_ref, q_ref, gn_ref, gr_ref, cs_ref, sg_ref, o_ref, dg_ref):
        @pl.when(pl.program_id(0) == 0)
        def _():
            dg_ref[...] = jnp.zeros_like(dg_ref)

        ga_acc = jnp.zeros((8, 128), F32)
        gb_acc = jnp.zeros((8, 128), F32)
        for h in range(MLA_HEADS):
            cols = slice(h * HEAD_PAD, (h + 1) * HEAD_PAD)
            q = q_ref[:, cols].astype(F32)
            d = dq_ref[h].astype(F32) * MLA_SCALE
            da, ga = _rms_bwd(q[:, :128], gn_ref[...], d[:, :128], QK_NOPE)
            db, gb = _rms_bwd(q[:, 128:], gr_ref[...], _rope_bwd(d[:, 128:], cs_ref[...], sg_ref[...]), QK_ROPE)
            o_ref[:, cols] = jnp.concatenate([da, db], axis=1).astype(BF16)
            ga_acc = ga_acc + _fold8(ga)
            gb_acc = gb_acc + _fold8(gb)
        dg_ref[0] += ga_acc
        dg_ref[1] += gb_acc

    wide = pl.BlockSpec((ts, MLA_HEADS * HEAD_PAD), lambda i: (i, 0))
    return _pcall(
        body, side, "q_prep_bwd", (s // ts,),
        [pl.BlockSpec((MLA_HEADS, ts, HEAD_PAD), lambda i: (0, i, 0)), wide,
         pl.BlockSpec((1, 128), lambda i: (0, 0)), pl.BlockSpec((1, 128), lambda i: (0, 0)),
         pl.BlockSpec((ts, 128), lambda i: (i, 0)), pl.BlockSpec((ts, 128), lambda i: (i, 0))],
        [wide, pl.BlockSpec((2, 8, 128), lambda i: (0, 0, 0))],
        [jax.ShapeDtypeStruct((s, MLA_HEADS * HEAD_PAD), BF16), jax.ShapeDtypeStruct((2, 8, 128), F32)],
        (dqcat, q_raw, gn, gr, cs, sg), ("arbitrary",))


def _k_prep_bwd(dkcat, dv, kv, gk, side=None):
    s = kv.shape[0]
    ts = min(TS_BWD, s)

    def body(dk_ref, dv_ref, k_ref, gk_ref, o_ref, dkr_ref, dg_ref):
        @pl.when(pl.program_id(0) == 0)
        def _():
            dg_ref[...] = jnp.zeros_like(dg_ref)

        g_acc = jnp.zeros((8, 128), F32)
        dkr = jnp.zeros((ts, 128), F32)
        for h in range(MLA_HEADS):
            d = dk_ref[h].astype(F32) * LN2
            k = k_ref[:, h * HEAD_PAD:h * HEAD_PAD + 128].astype(F32)
            dk_raw, gg = _rms_bwd(k, gk_ref[...], d[:, :128], QK_NOPE)
            o_ref[:, h * HEAD_PAD:(h + 1) * HEAD_PAD] = jnp.concatenate([dk_raw.astype(BF16), dv_ref[h]], axis=1)
            g_acc = g_acc + _fold8(gg)
            dkr = dkr + d[:, 128:]
        dkr_ref[...] = dkr
        dg_ref[...] += g_acc

    wide = pl.BlockSpec((ts, MLA_HEADS * HEAD_PAD), lambda i: (i, 0))
    return _pcall(
        body, side, "k_prep_bwd", (s // ts,),
        [pl.BlockSpec((MLA_HEADS, ts, HEAD_PAD), lambda i: (0, i, 0)),
         pl.BlockSpec((MLA_HEADS, ts, 128), lambda i: (0, i, 0)), wide,
         pl.BlockSpec((1, 128), lambda i: (0, 0))],
        [wide, pl.BlockSpec((ts, 128), lambda i: (i, 0)), pl.BlockSpec((8, 128), lambda i: (0, 0))],
        [jax.ShapeDtypeStruct((s, MLA_HEADS * HEAD_PAD), BF16), jax.ShapeDtypeStruct((s, 128), F32),
         jax.ShapeDtypeStruct((8, 128), F32)],
        (dkcat, dv, kv, gk), ("arbitrary",))


def _lora_bwd(dcqn, dckvn, dkr, p_all, gq, gkv, gkr, cs, sg):
    s = p_all.shape[0]
    ts = min(TS_ROW, s)

    def body(dq_ref, dkv_ref, dkr_ref, cq_ref, ckv_ref, kr_ref, gq_ref, gkv_ref, gkr_ref, cs_ref, sg_ref,
             o_ref, dgq_ref, dgkv_ref, dgkr_ref):
        @pl.when(pl.program_id(0) == 0)
        def _():
            dgq_ref[...] = jnp.zeros_like(dgq_ref)
            dgkv_ref[...] = jnp.zeros_like(dgkv_ref)
            dgkr_ref[...] = jnp.zeros_like(dgkr_ref)

        dq, g1 = _rms_bwd(cq_ref[...].astype(F32), gq_ref[...], dq_ref[...], 512)
        dkv, g2 = _rms_bwd(ckv_ref[...].astype(F32), gkv_ref[...], dkv_ref[...], 512)
        dkn = _rope_bwd(dkr_ref[...], cs_ref[...], sg_ref[...])
        dk, g3 = _rms_bwd(kr_ref[...].astype(F32), gkr_ref[...], dkn, QK_ROPE)
        o_ref[...] = jnp.concatenate([dq, dkv, dk, jnp.zeros((ts, 2048 - 1152), F32)], axis=1).astype(BF16)
        dgq_ref[...] += _fold8(g1)
        dgkv_ref[...] += _fold8(g2)
        dgkr_ref[...] += _fold8(g3)

    row = lambda w: pl.BlockSpec((ts, w), lambda i: (i, 0))
    vec = lambda w: pl.BlockSpec((1, w), lambda i: (0, 0))
    acc = lambda w: pl.BlockSpec((8, w), lambda i: (0, 0))
    return pl.pallas_call(
        body, name="lora_bwd", grid=(s // ts,),
        in_specs=[row(512), row(512), row(128),
                  pl.BlockSpec((ts, 512), lambda i: (i, COL_S // 512)),
                  pl.BlockSpec((ts, 512), lambda i: (i, COL_S // 512 + 1)),
                  pl.BlockSpec((ts, 128), lambda i: (i, (COL_S + 1024) // 128)),
                  vec(512), vec(512), vec(128), row(128), row(128)],
        out_specs=[row(2048), acc(512), acc(512), acc(128)],
        out_shape=[jax.ShapeDtypeStruct((s, 2048), BF16), jax.ShapeDtypeStruct((8, 512), F32),
                   jax.ShapeDtypeStruct((8, 512), F32), jax.ShapeDtypeStruct((8, 128), F32)],
        compiler_params=_params(("arbitrary",)),
    )(dcqn, dckvn, dkr, p_all, p_all, p_all, gq, gkv, gkr, cs, sg)


def _mem_attn_bwd(da, p_all, kn, v, gq):
    s = p_all.shape[0]
    tq = min(TS_ROW, s)

    def body(da_ref, q_ref, z_ref, kn_ref, v_ref, gq_ref, o_ref, dkn_ref, dv_ref, dg_ref):
        @pl.when(pl.program_id(0) == 0)
        def _():
            dkn_ref[...] = jnp.zeros_like(dkn_ref)
            dv_ref[...] = jnp.zeros_like(dv_ref)
            dg_ref[...] = jnp.zeros_like(dg_ref)

        z = z_ref[...].astype(F32)
        da_ = da_ref[...].astype(F32)
        sz = _sigmoid(z)
        gate = z * sz
        dgate = sz * (1.0 + z * (1.0 - sz))
        for hh in range(MEM_HEADS):
            sl = slice(hh * MEM_HEAD_DIM, (hh + 1) * MEM_HEAD_DIM)
            qf = q_ref[:, sl].astype(F32)
            qn = _rms_fwd(qf, gq_ref[...], MEM_HEAD_DIM).astype(BF16)
            knh, vh = kn_ref[:, sl], v_ref[:, sl]
            sc = _dot(qn, knh, "nt") * MEM_SCALE
            e = jnp.exp(sc - jnp.max(sc, axis=-1, keepdims=True))
            p = e * (1.0 / jnp.sum(e, axis=-1, keepdims=True))
            pb = p.astype(BF16)
            y = _dot(pb, vh)
            dyh = da_[:, sl] * gate[:, sl]
            o_ref[:, 1024 + hh * MEM_HEAD_DIM:1024 + (hh + 1) * MEM_HEAD_DIM] = (
                da_[:, sl] * y * dgate[:, sl]).astype(BF16)
            dyb = dyh.astype(BF16)
            dpm = _dot(dyb, vh, "nt")
            ds = (p * (dpm - jnp.sum(dpm * p, axis=-1, keepdims=True)) * MEM_SCALE).astype(BF16)
            dqn = _dot(ds, knh)
            dkn_ref[:, sl] += _dot(ds, qn, "tn")
            dv_ref[:, sl] += _dot(pb, dyb, "tn")
            dq, gg = _rms_bwd(qf, gq_ref[...], dqn, MEM_HEAD_DIM)
            o_ref[:, sl] = dq.astype(BF16)
            dg_ref[...] += _fold8(gg)

    full = lambda a: pl.BlockSpec(a.shape, lambda i: (0, 0))
    return pl.pallas_call(
        body, name="mem_attn_bwd", grid=(s // tq,),
        in_specs=[pl.BlockSpec((tq, 1024), lambda i: (i, 0)),
                  pl.BlockSpec((tq, 1024), lambda i: (i, COL_M // 1024)),
                  pl.BlockSpec((tq, 1024), lambda i: (i, COL_M // 1024 + 1)),
                  full(kn), full(v), full(gq)],
        out_specs=[pl.BlockSpec((tq, 2048), lambda i: (i, 0)),
                   pl.BlockSpec((MEM_TOKENS, 1024), lambda i: (0, 0)),
                   pl.BlockSpec((MEM_TOKENS, 1024), lambda i: (0, 0)),
                   pl.BlockSpec((8, MEM_HEAD_DIM), lambda i: (0, 0))],
        out_shape=[jax.ShapeDtypeStruct((s, 2048), BF16), jax.ShapeDtypeStruct((MEM_TOKENS, 1024), F32),
                   jax.ShapeDtypeStruct((MEM_TOKENS, 1024), F32), jax.ShapeDtypeStruct((8, MEM_HEAD_DIM), F32)],
        compiler_params=_params(("arbitrary",)),
    )(da, p_all, p_all, kn, v, gq)


def _mem_kv_bwd(dkn, dv, kraw, memn, mem, g_norm, w_kv, g_kn):
    m = mem.shape[0]

    def body(dkn_ref, dv_ref, kraw_ref, memn_ref, mem_ref, g_ref, w_ref, gk_ref, dw_ref, dgn_ref, dgk_ref, dkv_sc):
        gk_acc = jnp.zeros((8, MEM_HEAD_DIM), F32)
        for hh in range(MEM_HEADS):
            sl = slice(hh * MEM_HEAD_DIM, (hh + 1) * MEM_HEAD_DIM)
            dk, gg = _rms_bwd(kraw_ref[:, sl], gk_ref[...], dkn_ref[:, sl], MEM_HEAD_DIM)
            dkv_sc[:, sl] = dk.astype(BF16)
            gk_acc = gk_acc + _fold8(gg)
        dgk_ref[...] = gk_acc
        dkv_sc[:, 1024:] = dv_ref[...].astype(BF16)
        dkv = dkv_sc[...]
        dw_ref[...] = _dot(memn_ref[...], dkv, "tn")
        dmemn = _dot(dkv, w_ref[...], "nt")
        xf = mem_ref[...]
        r = lax.rsqrt(jnp.mean(xf * xf, axis=-1, keepdims=True) + EPS)
        dgn_ref[...] = _fold8(dmemn * (xf * r))

    return pl.pallas_call(
        body, name="mem_kv_bwd",
        out_shape=[jax.ShapeDtypeStruct((D_MODEL, D_MODEL), F32), jax.ShapeDtypeStruct((8, D_MODEL), F32),
                   jax.ShapeDtypeStruct((8, MEM_HEAD_DIM), F32)],
        scratch_shapes=[pltpu.VMEM((m, D_MODEL), BF16)],
        compiler_params=_params(),
    )(dkn, dv, kraw, memn, mem, g_norm, w_kv, g_kn)


def _dh_bwd(dps, w_all_t, row0, nrows, name, side=None):
    tm, tk = min(TS_ROW, nrows), 1024
    blk0 = row0 // tm
    widths = [a.shape[1] // tk for a in dps]
    starts = [int(v) for v in np.cumsum([0] + widths[:-1])]
    nk = sum(widths)

    def body(*refs):
        d_refs, w_ref, o_ref, acc_ref = refs[:5], refs[5], refs[6], refs[7]
        kk = pl.program_id(1)

        @pl.when(kk == 0)
        def _():
            acc_ref[...] = jnp.zeros_like(acc_ref)

        for d_ref, st, wd in zip(d_refs, starts, widths):
            @pl.when((kk >= st) & (kk < st + wd))
            def _(d_ref=d_ref):
                acc_ref[...] += _dot(d_ref[...], w_ref[...])

        @pl.when(kk == nk - 1)
        def _():
            o_ref[...] = acc_ref[...]

    def dspec(st, wd):
        return pl.BlockSpec((tm, tk), lambda i, kk: (blk0 + i, jnp.clip(kk - st, 0, wd - 1)))

    outs, extra = _pcall(
        body, side, name, (nrows // tm, nk),
        [dspec(st, wd) for st, wd in zip(starts, widths)] + [pl.BlockSpec((tk, D_MODEL), lambda i, kk: (kk, 0))],
        [pl.BlockSpec((tm, D_MODEL), lambda i, kk: (i, 0))], [jax.ShapeDtypeStruct((nrows, D_MODEL), F32)],
        (*dps, w_all_t), ("parallel", "arbitrary"), scratch_shapes=[pltpu.VMEM((tm, D_MODEL), F32)])
    return outs[0], extra


def _dx_bwd(dh_a, dh_b, x, dy, g, side=None):
    s = x.shape[0]
    ts = min(TS_BWD, dh_a.shape[0])
    na, nb = dh_a.shape[0] // ts, dh_b.shape[0] // ts

    def body(dha_ref, dhb_ref, x_ref, dy_ref, g_ref, o_ref, dg_ref):
        @pl.when(pl.program_id(0) == 0)
        def _():
            dg_ref[...] = jnp.zeros_like(dg_ref)

        dh = jnp.where(pl.program_id(0) < na, dha_ref[...], dhb_ref[...])
        dx, gg = _rms_bwd(x_ref[...], g_ref[...], dh, D_MODEL)
        o_ref[...] = dy_ref[...] + dx
        dg_ref[...] += _fold8(gg)

    row = pl.BlockSpec((ts, D_MODEL), lambda i: (i, 0))
    return _pcall(
        body, side, "dx_bwd", (s // ts,),
        [pl.BlockSpec((ts, D_MODEL), lambda i: (jnp.minimum(i, na - 1), 0)),
         pl.BlockSpec((ts, D_MODEL), lambda i: (jnp.clip(i - na, 0, nb - 1), 0)),
         row, row, pl.BlockSpec((1, D_MODEL), lambda i: (0, 0))],
        [row, pl.BlockSpec((8, D_MODEL), lambda i: (0, 0))],
        [jax.ShapeDtypeStruct((s, D_MODEL), F32), jax.ShapeDtypeStruct((8, D_MODEL), F32)],
        (dh_a, dh_b, x, dy, g), ("arbitrary",))


def _pad128(v, n):
    return jnp.pad(v.reshape(1, n), ((0, 0), (0, 128 - n)))


def _local_step(x, positions, mem, target, comm, gains):
    half = QK_ROPE // 2
    inv_freq = jnp.power(ROPE_THETA, -jnp.arange(half, dtype=F32) / half)
    ang = positions.astype(F32)[:, None] * inv_freq
    cos, sin = jnp.cos(ang), jnp.sin(ang)
    zpad = jnp.zeros((x.shape[0], 64), F32)
    cs = jnp.concatenate([cos, cos, zpad], axis=1)
    sg = jnp.concatenate([-sin, sin, zpad], axis=1)
    g_qr, g_kr = _pad128(gains["mla_qn_rope_g"], 64), _pad128(gains["mla_kn_rope_g"], 64)
    g_qn, g_kn = gains["mla_qn_nope_g"], gains["mla_kn_nope_g"]

    def hosted_matmul(tag, a, b, mode, out_dtype):
        side = comm.side(tag)
        if side is None:
            return _matmul(a, b, mode, out_dtype, tag)
        out, extra = _matmul(a, b, mode, out_dtype, tag, side=side)
        comm.done(tag, extra)
        return out

    (p_all, h), extra = _proj_fwd(x, gains["norm_g"], comm.weight("w_all_t"), comm.side("proj_fwd"))
    comm.done("proj_fwd", extra)
    a_conv, co = _conv_fwd(p_all, comm.weight("conv_w"))
    (cqn, ckvn, krope), extra = _lora_fwd(p_all, gains["mla_q_norm_g"], gains["mla_kv_norm_g"], g_kr, cs, sg,
                                          comm.side("lora_fwd"))
    comm.done("lora_fwd", extra)
    q_raw, qcat = _q_up(cqn, comm.weight("w_uq"), g_qn, g_qr, cs, sg)
    kv, kcat, vt = _kv_up(ckvn, comm.weight("w_ukv"), krope, g_kn)
    mla_y, a_mla, lse = _flash_fwd(qcat, kcat, vt, p_all)
    memn, kraw, kn, vmem = _mem_kv_fwd(mem, gains["mem_norm_g"], comm.weight("w_mem_kv"), gains["mem_kn_g"])
    a_mem = _mem_attn_fwd(p_all, kn, vmem, gains["mem_qn_g"])
    o_c, o_m, o_e, merged = _merge_fwd(a_conv, comm.weight("w_conv_out"), a_mla, comm.weight("w_mla_out"), a_mem,
                                       comm.weight("w_mem_out"), p_all)
    dy, dyb, loss_parts = _out_fwd(merged, comm.weight("w_o"), x, target)

    comm.put("w_o", _matmul(merged, dyb, "tn", BF16, "dw_o"))
    do_c, do_m, do_e, dp_g = _merge_bwd(dyb, comm.weight("w_o"), p_all, o_c, o_m, o_e)
    comm.put("w_conv_out", _matmul(a_conv, do_c, "tn", BF16, "dw_conv_out"))
    comm.put("w_mla_out", _matmul(a_mla, do_m, "tn", BF16, "dw_mla_out"))
    comm.put("w_mem_out", _matmul(a_mem, do_e, "tn", BF16, "dw_mem_out"))
    da_conv = hosted_matmul("da_conv", do_c, comm.weight("w_conv_out"), "nt", BF16)
    da_mem = _matmul(do_e, comm.weight("w_mem_out"), "nt", BF16, "da_mem")
    dp_c, dconv_w = _conv_bwd(da_conv, p_all, co, comm.weight("conv_w"))
    dmla_y, dp_z, delta = _mla_gate_bwd(do_m, comm.weight("w_mla_out"), mla_y, p_all)
    dqcat, dkcat, dv = _flash_bwd(qcat, kcat, kv, dmla_y, lse, delta)
    (dq_raw, dg_q), extra = _q_prep_bwd(dqcat, q_raw, g_qn, g_qr, cs, sg, comm.side("q_prep_bwd"))
    comm.done("q_prep_bwd", extra)
    (dkv, dkr, dg_kn), extra = _k_prep_bwd(dkcat, dv, kv, g_kn, comm.side("k_prep_bwd"))
    comm.done("k_prep_bwd", extra)
    comm.put("w_uq", _matmul(cqn, dq_raw, "tn", BF16, "dw_uq"))
    comm.put("w_ukv", _matmul(ckvn, dkv, "tn", BF16, "dw_ukv"))
    dcqn = _matmul(dq_raw, comm.weight("w_uq"), "nt", F32, "dcqn")
    dckvn = _matmul(dkv, comm.weight("w_ukv"), "nt", F32, "dckvn")
    dp_s, dg_qn, dg_kvn, dg_kr = _lora_bwd(dcqn, dckvn, dkr, p_all, gains["mla_q_norm_g"],
                                            gains["mla_kv_norm_g"], g_kr, cs, sg)
    dp_m, dkn, dvm, dg_mq = _mem_attn_bwd(da_mem, p_all, kn, vmem, gains["mem_qn_g"])
    dw_mem_kv, dg_mn, dg_mk = _mem_kv_bwd(dkn, dvm, kraw, memn, mem, gains["mem_norm_g"],
                                          comm.weight("w_mem_kv"), gains["mem_kn_g"])
    comm.put("w_mem_kv", dw_mem_kv.astype(BF16))
    dps = [dp_g, dp_z, dp_c, dp_m, dp_s]
    comm.put("w_all_t", [hosted_matmul("dw_in_%d" % k, d, h, "tn", BF16) for k, d in enumerate(dps)])
    s = x.shape[0]
    dh_a, extra = _dh_bwd(dps, comm.weight("w_all_t"), 0, s // 4, "dh_a", comm.side("dh_a"))
    comm.done("dh_a", extra)
    dh_b, extra = _dh_bwd(dps, comm.weight("w_all_t"), s // 4, s - s // 4, "dh_b", comm.side("dh_b"))
    comm.done("dh_b", extra)
    (grad_x, dg_n), extra = _dx_bwd(dh_a, dh_b, x, dy, gains["norm_g"], comm.side("dx_bwd"))
    comm.done("dx_bwd", extra)

    gsmall = {
        "norm_g": dg_n.sum(0), "conv_w": dconv_w.sum(1), "mla_q_norm_g": dg_qn.sum(0),
        "mla_kv_norm_g": dg_kvn.sum(0), "mla_qn_nope_g": dg_q[0].sum(0), "mla_qn_rope_g": dg_q[1].sum(0)[:64],
        "mla_kn_nope_g": dg_kn.sum(0), "mla_kn_rope_g": dg_kr.sum(0)[:64], "mem_norm_g": dg_mn.sum(0),
        "mem_qn_g": dg_mq.sum(0), "mem_kn_g": dg_mk.sum(0),
    }
    return loss_parts, grad_x, gsmall


def _me():
    return lax.axis_index("x"), lax.axis_index("y"), lax.axis_index("c")


def _chips(x, y):
    return [(1 - x, y), (x, 1 - y), (1 - x, 1 - y)]


def _allgather_shards(xs, name):
    r, c = xs.shape
    hc = c // 2

    def body(x_ref, out_ref, send_sems, recv_sems):
        x, y, cc = _me()
        me, sibling = (x, y, cc), (x, y, 1 - cc)
        across_x, across_y, diagonal = _chips(x, y)

        def rows(px, py, pc):
            return out_ref.at[2 * px + py, :, pl.ds(pc * hc, hc)]

        def copy(k, block, to, src=None):
            return pltpu.make_async_remote_copy(
                src_ref=rows(*block) if src is None else src, dst_ref=rows(*block),
                send_sem=send_sems.at[k], recv_sem=recv_sems.at[k], device_id=to, device_id_type=MESH)

        my_half = x_ref.at[:, pl.ds(cc * hc, hc)]
        direct = [copy(0, me, (*across_x, cc), src=my_half), copy(1, me, (*across_y, cc), src=my_half)]
        for cp in direct:
            cp.start()
        copy(0, (*across_x, cc), me).wait_recv()
        copy(1, (*across_y, cc), me).wait_recv()
        from_x = cc == 0
        came = (jnp.where(from_x, across_x[0], across_y[0]), jnp.where(from_x, across_x[1], across_y[1]), cc)
        goes = (jnp.where(from_x, across_y[0], across_x[0]), jnp.where(from_x, across_y[1], across_x[1]), cc)
        relay = copy(2, came, goes)
        relay.start()
        passed = [copy(3, (*across_x, cc), sibling), copy(4, (*across_y, cc), sibling)]
        for cp in passed:
            cp.start()
        copy(2, (*diagonal, cc), me).wait_recv()
        passed.append(copy(5, (*diagonal, cc), sibling))
        passed[2].start()
        for j, chip in enumerate((across_x, across_y, diagonal)):
            copy(3 + j, (*chip, 1 - cc), me).wait_recv()
        for cp in direct + [relay] + passed:
            cp.wait_send()

    return pl.pallas_call(
        body, name=name, in_specs=[ANY], out_specs=ANY,
        out_shape=jax.ShapeDtypeStruct((4, r, c), xs.dtype),
        scratch_shapes=[pltpu.SemaphoreType.DMA((6,)), pltpu.SemaphoreType.DMA((6,))],
    )(xs)


def _with_own(g4, own):
    xi, yi, _ = _me()
    pick = lax.broadcasted_iota(jnp.int32, (4,) + (1,) * own.ndim, 0) == 2 * xi + yi
    return jnp.where(pick, own[None], g4)


def _remote(src, dst, send_sems, recv_sems, k, to):
    return pltpu.make_async_remote_copy(src_ref=src, dst_ref=dst, send_sem=send_sems.at[k], recv_sem=recv_sems.at[k],
                                        device_id=to, device_id_type=MESH)


def _side_gather_ici(shards):
    def build(ins, outs, send_sems, recv_sems):
        x, y, cc = _me()
        pairs = []
        for a, (x_ref, o_ref) in enumerate(zip(ins, outs)):
            hr = x_ref.shape[0] // 2
            my_half = x_ref.at[pl.ds(cc * hr, hr), :]
            for j, (px, py) in enumerate(_chips(x, y)):
                mine = o_ref.at[pl.ds((4 * x + 2 * y + cc) * hr, hr), :]
                theirs = o_ref.at[pl.ds((4 * px + 2 * py + cc) * hr, hr), :]
                pairs.append((_remote(my_half, mine, send_sems, recv_sems, 3 * a + j, (px, py, cc)),
                              _remote(my_half, theirs, send_sems, recv_sems, 3 * a + j, (px, py, cc))))
        return pairs

    shapes = [jax.ShapeDtypeStruct((4 * s.shape[0], s.shape[1]), s.dtype) for s in shards]
    return _Side(shards, shapes, 3 * len(shards), build)


def _side_gather_pass(bufs):
    def build(ins, outs, send_sems, recv_sems):
        x, y, cc = _me()
        pairs = []
        for a, o_ref in enumerate(outs):
            hr = o_ref.shape[0] // 8
            for j, (px, py) in enumerate(_chips(x, y)):
                got = o_ref.at[pl.ds((4 * px + 2 * py + cc) * hr, hr), :]
                coming = o_ref.at[pl.ds((4 * px + 2 * py + 1 - cc) * hr, hr), :]
                pairs.append((_remote(got, got, send_sems, recv_sems, 3 * a + j, (x, y, 1 - cc)),
                              _remote(got, coming, send_sems, recv_sems, 3 * a + j, (x, y, 1 - cc))))
        return pairs

    shapes = [jax.ShapeDtypeStruct(b.shape, b.dtype) for b in bufs]
    return _Side(bufs, shapes, 3 * len(bufs), build, aliases={a: a for a in range(len(bufs))})


def _side_swap(srcs, parts):
    def view(s_ref, part, cc):
        if part == "lead":
            return s_ref.at[1 - cc]
        if part == "cols":
            hc = s_ref.shape[1] // 2
            return s_ref.at[:, pl.ds((1 - cc) * hc, hc)]
        return s_ref

    def build(ins, outs, send_sems, recv_sems):
        x, y, cc = _me()
        pairs = []
        for a, (s_ref, o_ref) in enumerate(zip(ins, outs)):
            cp = _remote(view(s_ref, parts[a], cc), o_ref, send_sems, recv_sems, a, (x, y, 1 - cc))
            pairs.append((cp, cp))
        return pairs

    out_shape = {"lead": lambda s: s.shape[1:], "cols": lambda s: (s.shape[0], s.shape[1] // 2), "all": lambda s: s.shape}
    shapes = [jax.ShapeDtypeStruct(out_shape[p](s), s.dtype) for s, p in zip(srcs, parts)]
    return _Side(srcs, shapes, len(srcs), build)


def _side_chips(ts):
    def build(ins, outs, send_sems, recv_sems):
        x, y, cc = _me()
        pairs = []
        for a, (t_ref, r_ref) in enumerate(zip(ins, outs)):
            for j, (px, py) in enumerate(_chips(x, y)):
                cp = _remote(t_ref.at[2 * px + py], r_ref.at[j], send_sems, recv_sems, 3 * a + j, (px, py, cc))
                pairs.append((cp, cp))
        return pairs

    shapes = [jax.ShapeDtypeStruct((3,) + t.shape[1:], t.dtype) for t in ts]
    return _Side(ts, shapes, 3 * len(ts), build)


ELEMENTWISE_BLOCK_BYTES = 2 << 20


def _tile(r, c, itemsize):
    tr, tc = r, c
    while tr * tc * itemsize > ELEMENTWISE_BLOCK_BYTES and tr % 32 == 0:
        tr //= 2
    while tr * tc * itemsize > ELEMENTWISE_BLOCK_BYTES and tc % 256 == 0:
        tc //= 2
    return tr, tc


def _add_own_half(gb, got, sel, name):
    _, _, hr, c = gb.shape
    tr, tc = _tile(hr, c, 4)

    def body(sel_ref, a_ref, b_ref, o_ref):
        o_ref[...] = (a_ref[0].astype(F32) + b_ref[...].astype(F32)).astype(BF16)

    return pl.pallas_call(
        body, name=name,
        grid_spec=pltpu.PrefetchScalarGridSpec(
            num_scalar_prefetch=1, grid=(4, hr // tr, c // tc),
            in_specs=[pl.BlockSpec((1, 1, tr, tc), lambda k, i, j, sel_ref: (sel_ref[0], k, i, j)),
                      pl.BlockSpec((1, tr, tc), lambda k, i, j, sel_ref: (k, i, j))],
            out_specs=pl.BlockSpec((1, tr, tc), lambda k, i, j, sel_ref: (k, i, j))),
        out_shape=jax.ShapeDtypeStruct(got.shape, BF16),
        compiler_params=_params(("parallel", "parallel", "parallel")),
    )(sel, gb, got)


def _add_own_cols(g, got, sel, name):
    r, c = g.shape
    hr, hc = r // 4, c // 2
    tr, tc = _tile(hr, hc, 4)
    nj = hc // tc

    def body(sel_ref, a_ref, b_ref, o_ref):
        o_ref[...] = (a_ref[...].astype(F32) + b_ref[...].astype(F32)).astype(BF16)

    t = pl.pallas_call(
        body, name=name,
        grid_spec=pltpu.PrefetchScalarGridSpec(
            num_scalar_prefetch=1, grid=(r // tr, nj),
            in_specs=[pl.BlockSpec((tr, tc), lambda i, j, sel_ref: (i, sel_ref[0] * nj + j)),
                      pl.BlockSpec((tr, tc), lambda i, j, sel_ref: (i, j))],
            out_specs=pl.BlockSpec((tr, tc), lambda i, j, sel_ref: (i, j))),
        out_shape=jax.ShapeDtypeStruct((r, hc), BF16),
        compiler_params=_params(("parallel", "parallel")),
    )(sel, g, got)
    return t.reshape(4, hr, hc)


def _sum_partials(t, rcv, sel, name):
    _, hr, c = t.shape
    tr, tc = _tile(hr, c, 4)

    def body(sel_ref, t_ref, r_ref, o_ref):
        o_ref[...] = ((t_ref[0].astype(F32) + r_ref[0].astype(F32)) + r_ref[1].astype(F32)) + r_ref[2].astype(F32)

    return pl.pallas_call(
        body, name=name,
        grid_spec=pltpu.PrefetchScalarGridSpec(
            num_scalar_prefetch=1, grid=(hr // tr, c // tc),
            in_specs=[pl.BlockSpec((1, tr, tc), lambda i, j, sel_ref: (sel_ref[0], i, j)),
                      pl.BlockSpec((3, tr, tc), lambda i, j, sel_ref: (0, i, j))],
            out_specs=pl.BlockSpec((tr, tc), lambda i, j, sel_ref: (i, j))),
        out_shape=jax.ShapeDtypeStruct((hr, c), F32),
        compiler_params=_params(("parallel", "parallel")),
    )(sel, t, rcv)


class _GroupReduce:
    def __init__(self):
        self.names, self.gb, self.t, self.fh, self.other = [], [], [], [], []

    def put(self, name, gb):
        self.names.append(name)
        self.gb.append(gb)

    def side(self, stage):
        if stage == 1:
            return _side_swap(self.gb, ["lead" if g.ndim == 4 else "cols" for g in self.gb])
        return _side_chips(self.t) if stage == 2 else _side_swap(self.fh, ["all"] * len(self.fh))

    def done(self, stage, outs):
        x, y, cc = _me()
        if stage == 1:
            sel = jnp.reshape(cc, (1,)).astype(jnp.int32)
            self.t = [(_add_own_half if gb.ndim == 4 else _add_own_cols)(gb, got, sel, "rs_add_" + n)
                      for n, gb, got in zip(self.names, self.gb, outs)]
        elif stage == 2:
            sel = jnp.reshape(2 * x + y, (1,)).astype(jnp.int32)
            self.fh = [_sum_partials(t, rcv, sel, "rs_sum_" + n) for n, t, rcv in zip(self.names, self.t, outs)]
        else:
            self.other = list(outs)

    def result(self):
        return {n: (fh, other) for n, fh, other in zip(self.names, self.fh, self.other)}


def _allreduce_small(v, tag):
    r = v.shape[0]

    def body(v_ref, out_ref, buf_ref, send_sems, recv_sems):
        x, y, cc = _me()
        me = 4 * x + 2 * y + cc
        buf_ref[pl.ds(pl.multiple_of(me * r, 8), r), :] = v_ref[...]
        peers = [(x, y, 1 - cc)] + [(px, py, pc) for (px, py) in _chips(x, y) for pc in (cc, 1 - cc)]
        cps = []
        for k, (px, py, pc) in enumerate(peers):
            mine = buf_ref.at[pl.ds(pl.multiple_of(me * r, 8), r), :]
            cps.append(pltpu.make_async_remote_copy(
                src_ref=v_ref, dst_ref=mine, send_sem=send_sems.at[k], recv_sem=recv_sems.at[k],
                device_id=(px, py, pc), device_id_type=MESH))
        for cp in cps:
            cp.start()
        for cp in cps:
            cp.wait()
        acc = buf_ref[0:r, :]
        for d in range(1, 8):
            acc = acc + buf_ref[d * r:(d + 1) * r, :]
        out_ref[...] = acc

    vm = pl.BlockSpec(memory_space=pltpu.VMEM)
    return pl.pallas_call(
        body, name="allreduce_small_" + tag, in_specs=[vm], out_specs=vm,
        out_shape=jax.ShapeDtypeStruct((r, 128), F32),
        scratch_shapes=[pltpu.VMEM((8 * r, 128), F32), pltpu.SemaphoreType.DMA((7,)), pltpu.SemaphoreType.DMA((7,))],
    )(v)


def _adamw(w, g, m, v, name):
    r, c = w.shape
    tr = r
    c1 = 1.0 / (1.0 - ADAM_B1 ** ADAM_STEP)
    c2 = 1.0 / (1.0 - ADAM_B2 ** ADAM_STEP)

    def body(w_ref, g_ref, m_ref, v_ref, d_ref, mo_ref, vo_ref):
        gg = g_ref[...]
        mn = ADAM_B1 * m_ref[...] + (1.0 - ADAM_B1) * gg
        vn = ADAM_B2 * v_ref[...] + (1.0 - ADAM_B2) * (gg * gg)
        mo_ref[...] = mn
        vo_ref[...] = vn
        d_ref[...] = -ADAM_LR * ((mn * c1) / (jnp.sqrt(vn * c2) + ADAM_EPS) + ADAM_WD * w_ref[...])

    blk = pl.BlockSpec((tr, c), lambda i: (i, 0))
    return pl.pallas_call(
        body, name=name, grid=(r // tr,), in_specs=[blk] * 4, out_specs=[blk] * 3,
        out_shape=[jax.ShapeDtypeStruct((r, c), F32)] * 3,
        compiler_params=_params(("parallel",)),
    )(w, g, m, v)


def _adamw_halves(w, g_own, g_other, m, v, sel, name, axis):
    r, c = w.shape
    tr, tc = _tile(g_own.shape[0], g_own.shape[1], 4)
    nh = (g_own.shape[axis]) // (tr, tc)[axis]
    c1 = 1.0 / (1.0 - ADAM_B1 ** ADAM_STEP)
    c2 = 1.0 / (1.0 - ADAM_B2 ** ADAM_STEP)

    def body(sel_ref, w_ref, go_ref, gx_ref, m_ref, v_ref, d_ref, mo_ref, vo_ref, g_ref):
        mine = (pl.program_id(axis) // nh) == sel_ref[0]
        gg = jnp.where(mine, go_ref[...], gx_ref[...])
        g_ref[...] = gg
        mn = ADAM_B1 * m_ref[...] + (1.0 - ADAM_B1) * gg
        vn = ADAM_B2 * v_ref[...] + (1.0 - ADAM_B2) * (gg * gg)
        mo_ref[...] = mn
        vo_ref[...] = vn
        d_ref[...] = -ADAM_LR * ((mn * c1) / (jnp.sqrt(vn * c2) + ADAM_EPS) + ADAM_WD * w_ref[...])

    blk = pl.BlockSpec((tr, tc), lambda i, j, sel_ref: (i, j))
    if axis == 0:
        half = pl.BlockSpec((tr, tc), lambda i, j, sel_ref: (i % nh, j))
    else:
        half = pl.BlockSpec((tr, tc), lambda i, j, sel_ref: (i, j % nh))
    return pl.pallas_call(
        body, name=name,
        grid_spec=pltpu.PrefetchScalarGridSpec(
            num_scalar_prefetch=1, grid=(r // tr, c // tc),
            in_specs=[blk, half, half, blk, blk], out_specs=[blk] * 4),
        out_shape=[jax.ShapeDtypeStruct((r, c), F32)] * 4,
        compiler_params=_params(("parallel", "parallel")),
    )(sel, w, g_own, g_other, m, v)


WEIGHTS = ['norm_g', 'w_in', 'conv_w', 'w_conv_out', 'mla_q_norm_g', 'w_uq', 'mla_kv_norm_g', 'w_ukv',
           'mla_qn_nope_g', 'mla_qn_rope_g', 'mla_kn_nope_g', 'mla_kn_rope_g', 'w_mla_out', 'mem_norm_g',
           'w_mem_kv', 'mem_qn_g', 'mem_kn_g', 'w_mem_out', 'w_o']
COL_SHARDED = ['w_in', 'w_conv_out', 'w_uq', 'w_ukv', 'w_mem_out']
ROW_SHARDED = ['w_mla_out', 'w_mem_kv', 'w_o']
SMALL = ['norm_g', 'conv_w', 'mla_q_norm_g', 'mla_kv_norm_g', 'mla_qn_nope_g', 'mla_qn_rope_g', 'mla_kn_nope_g',
         'mla_kn_rope_g', 'mem_norm_g', 'mem_qn_g', 'mem_kn_g']
SMALL_SIZES = [2048, 3072, 512, 512, 128, 64, 128, 64, 2048, 256, 256]
PACK_ROWS = 72


def _full_from_shards(name, g4):
    if name in COL_SHARDED:
        return g4.transpose(1, 0, 2).reshape(g4.shape[1], 4 * g4.shape[2])
    return g4.reshape(4 * g4.shape[1], g4.shape[2])


def _w_all_t_from_w_in_t(w_in_t):
    conv, small, mz, memq, gates = (w_in_t[0:4096], w_in_t[4096:5184], w_in_t[5184:7232],
                                    w_in_t[7232:9280], w_in_t[9280:15424])
    return jnp.concatenate([gates, mz, conv, memq, small, jnp.zeros((2048 - 1088, D_MODEL), BF16)], axis=0)


def _pad_w_uq(w_uq):
    return jnp.pad(w_uq.reshape(512, MLA_HEADS, 192), ((0, 0), (0, 0), (0, 64))).reshape(512, 4096)


def _grad_halves_by_owner(name, g):
    if name == "w_all_t":
        dw_g, dw_z, dw_c, dw_m, dw_s = g
        return "w_in", jnp.concatenate([dw_c, dw_s[:1088], dw_z, dw_m, dw_g], axis=0)
    if name == "w_uq":
        g = g.reshape(512, MLA_HEADS, HEAD_PAD)[:, :, :192].reshape(512, 3072)
    r, c = g.shape
    if name in COL_SHARDED:
        return name, g.reshape(2, r // 2, 4, c // 4).transpose(0, 2, 1, 3)
    return name, g.reshape(4, 2, r // 8, c).transpose(1, 0, 2, 3)


LATE_WEIGHTS = ['w_conv_out', 'w_uq', 'w_ukv', 'w_mem_out', 'w_mla_out', 'w_mem_kv', 'w_o']
HOSTS = {"proj_fwd": ("gather", 1), "lora_fwd": ("gather", 2),
         "da_conv": ("g1", 1), "q_prep_bwd": ("g1", 2), "k_prep_bwd": ("g1", 3),
         "dw_in_0": ("g2", 1), "dw_in_2": ("g2", 2), "dw_in_4": ("g2", 3),
         "dh_a": ("g3", 1), "dh_b": ("g3", 2), "dx_bwd": ("g3", 3)}
GROUP_OF = {"w_o": "g1", "w_conv_out": "g1", "w_mla_out": "g1", "w_mem_out": "g1",
            "w_uq": "g2", "w_ukv": "g2", "w_mem_kv": "g2", "w_in": "g3"}


class _Comm:
    def __init__(self, shards, w_in_t_full, conv_full):
        self.shards = shards
        self.w = {"w_all_t": _w_all_t_from_w_in_t(w_in_t_full), "conv_w": conv_full}
        self.bufs = None
        self.groups = {"g1": _GroupReduce(), "g2": _GroupReduce(), "g3": _GroupReduce()}

    def weight(self, name):
        return self.w[name]

    def side(self, tag):
        if tag not in HOSTS:
            return None
        kind, stage = HOSTS[tag]
        if kind == "gather":
            return _side_gather_ici([self.shards[n] for n in LATE_WEIGHTS]) if stage == 1 else \
                _side_gather_pass(self.bufs)
        return self.groups[kind].side(stage)

    def done(self, tag, outs):
        kind, stage = HOSTS[tag]
        if kind != "gather":
            self.groups[kind].done(stage, outs)
        elif stage == 1:
            self.bufs = list(outs)
        else:
            for n, buf in zip(LATE_WEIGHTS, outs):
                own = self.shards[n]
                self.w[n] = _full_from_shards(n, _with_own(buf.reshape((4,) + own.shape), own))
            self.w["w_uq"] = _pad_w_uq(self.w["w_uq"])

    def put(self, name, g):
        name, gb = _grad_halves_by_owner(name, g)
        self.groups[GROUP_OF[name]].put(name, gb)

    def reduced(self):
        out = {}
        for grp in self.groups.values():
            out.update(grp.result())
        return out


def kernel(x, positions, mem, norm_g, w_in, conv_w, w_conv_out, mla_q_norm_g, w_uq, mla_kv_norm_g, w_ukv, mla_qn_nope_g, mla_qn_rope_g, mla_kn_nope_g, mla_kn_rope_g, w_mla_out, mem_norm_g, w_mem_kv, mem_qn_g, mem_kn_g, w_mem_out, w_o, loss_target, m_norm_g, m_w_in, m_conv_w, m_w_conv_out, m_mla_q_norm_g, m_w_uq, m_mla_kv_norm_g, m_w_ukv, m_mla_qn_nope_g, m_mla_qn_rope_g, m_mla_kn_nope_g, m_mla_kn_rope_g, m_w_mla_out, m_mem_norm_g, m_w_mem_kv, m_mem_qn_g, m_mem_kn_g, m_w_mem_out, m_w_o, v_norm_g, v_w_in, v_conv_w, v_w_conv_out, v_mla_q_norm_g, v_w_uq, v_mla_kv_norm_g, v_w_ukv, v_mla_qn_nope_g, v_mla_qn_rope_g, v_mla_kn_nope_g, v_mla_kn_rope_g, v_w_mla_out, v_mem_norm_g, v_w_mem_kv, v_mem_qn_g, v_mem_kn_g, v_w_mem_out, v_w_o):
    args = locals()
    w = {n: args[n][0] for n in WEIGHTS}
    m1 = {n: args["m_" + n][0] for n in WEIGHTS}
    v2 = {n: args["v_" + n][0] for n in WEIGHTS}
    xi, yi, ci = _me()
    chip = 2 * xi + yi

    conv_slot = jnp.zeros((3, 4, 256), F32).at[:, chip, :].set(w["conv_w"] * jnp.where(ci == 0, 1.0, 0.0))
    pre = jnp.zeros((PACK_ROWS * 128,), F32).at[0:3072].set(conv_slot.reshape(3072))
    conv_full = _allreduce_small(pre.reshape(PACK_ROWS, 128), "conv_w").reshape(-1)[0:3072].reshape(3, 1024)

    w_in_t = w["w_in"].T.astype(BF16)
    w_in_t_full = _with_own(_allgather_shards(w_in_t, "ag_w_in"), w_in_t).reshape(4 * w_in_t.shape[0], D_MODEL)
    comm = _Comm({n: w[n].astype(BF16) for n in LATE_WEIGHTS}, w_in_t_full, conv_full)
    gains = {n: w[n].reshape(1, -1) for n in SMALL if n != "conv_w"}

    loss_parts, grad_x, gsmall = _local_step(x[0], positions[0], mem[0], loss_target[0], comm, gains)

    loss_local = 0.5 * jnp.sum(loss_parts) * (1.0 / D_MODEL)
    flat = jnp.concatenate([gsmall[n].reshape(-1) for n in SMALL] + [loss_local.reshape(1)])
    flat = jnp.pad(flat, (0, PACK_ROWS * 128 - flat.shape[0]))
    tot = _allreduce_small(flat.reshape(PACK_ROWS, 128), "grads").reshape(-1)
    grads = {}
    off = 0
    for n, sz in zip(SMALL, SMALL_SIZES):
        grads[n] = tot[off:off + sz]
        off += sz
    loss = tot[off]
    grads["conv_w"] = lax.dynamic_slice(grads["conv_w"].reshape(3, 1024), (0, chip * 256), (3, 256))
    for n in SMALL:
        grads[n] = grads[n].reshape(w[n].shape)

    deltas, new_m, new_v = {}, {}, {}
    for n in SMALL:
        shp = w[n].shape
        two_d = (lambda a: a.reshape(1, -1)) if len(shp) == 1 else (lambda a: a)
        d, mn, vn = _adamw(two_d(w[n]), two_d(grads[n]), two_d(m1[n]), two_d(v2[n]), "adamw_" + n)
        deltas[n], new_m[n], new_v[n] = d.reshape(shp), mn.reshape(shp), vn.reshape(shp)

    sel_c = jnp.reshape(ci, (1,)).astype(jnp.int32)
    reduced = comm.reduced()
    for n in COL_SHARDED + ROW_SHARDED:
        g_own, g_other = reduced[n]
        if n == "w_in":
            outs = _adamw_halves(w[n].T, g_own, g_other, m1[n].T, v2[n].T, sel_c, "adamw_" + n, 1)
            deltas[n], new_m[n], new_v[n], grads[n] = [o.T for o in outs]
        else:
            deltas[n], new_m[n], new_v[n], grads[n] = _adamw_halves(w[n], g_own, g_other, m1[n], v2[n], sel_c,
                                                                     "adamw_" + n, 0)

    lead = lambda a: a[None]
    return (loss, grad_x[None], *[lead(grads[n]) for n in WEIGHTS], *[lead(deltas[n]) for n in WEIGHTS],
            *[lead(new_m[n]) for n in WEIGHTS], *[lead(new_v[n]) for n in WEIGHTS])
```

```python
import functools

import numpy as np
import jax
import jax.numpy as jnp
from jax import lax
from jax.experimental import pallas as pl
from jax.experimental.pallas import tpu as pltpu

F32 = jnp.float32
BF16 = jnp.bfloat16
MESH = pl.DeviceIdType.MESH

D_MODEL = 2048
EPS = 1e-6
CHUNK = 64
MLA_HEADS = 16
QK_NOPE = 128
QK_ROPE = 64
HEAD_PAD = 256
MEM_TOKENS = 256
MEM_HEADS = 4
MEM_HEAD_DIM = 256
ROPE_THETA = 10000.0
MLA_SCALE = (QK_NOPE + QK_ROPE) ** -0.5
MEM_SCALE = MEM_HEAD_DIM ** -0.5
LN2 = 0.6931471805599453
Q_SCALE = MLA_SCALE / LN2
NEG = -1e30

ADAM_LR = 0.001
ADAM_B1 = 0.9
ADAM_B2 = 0.999
ADAM_EPS = 1e-08
ADAM_WD = 0.01
ADAM_STEP = 10

NP = 16384
COL_G, COL_Z, COL_C, COL_M, COL_S = 0, 6144, 8192, 12288, 14336

TM_PROJ = 1024
TN_PROJ = 2048
TS_ROW = 512
TS_BWD = 256
TQ_ATT = 1024
TK_ATT = 1024
TR_ATT = 256
FWD_HEADS_PER_STEP = 4
VMEM_LIMIT = 56 * 1024 * 1024

NT_DIMS = (((1,), (1,)), ((), ()))
TN_DIMS = (((0,), (0,)), ((), ()))


def _dot(a, b, mode="nn"):
    if mode == "nn":
        return jnp.dot(a, b, preferred_element_type=F32)
    return lax.dot_general(a, b, NT_DIMS if mode == "nt" else TN_DIMS, preferred_element_type=F32)


def _sigmoid(z):
    return 1.0 / (1.0 + jnp.exp(-z))


def _params(sem=None, vmem=VMEM_LIMIT):
    return pltpu.CompilerParams(dimension_semantics=sem, vmem_limit_bytes=vmem)


def _fold8(v):
    r, c = v.shape
    return v.reshape(r // 8, 8, c).sum(axis=0)


def _swap32(t):
    lane = lax.broadcasted_iota(jnp.int32, t.shape, 1)
    return jnp.where(lane < 32, pltpu.roll(t, 96, 1), pltpu.roll(t, 32, 1))


def _rope(t, cs, sg):
    return t * cs + _swap32(t) * sg


def _rope_bwd(d, cs, sg):
    return d * cs + _swap32(d * sg)


def _rms_fwd(xf, g, n):
    r = lax.rsqrt(jnp.sum(xf * xf, axis=-1, keepdims=True) * (1.0 / n) + EPS)
    return xf * r * g


def _rms_bwd(xf, g, dy, n):
    r = lax.rsqrt(jnp.sum(xf * xf, axis=-1, keepdims=True) * (1.0 / n) + EPS)
    xhat = xf * r
    dyg = dy * g
    dx = r * (dyg - xhat * (jnp.sum(dyg * xhat, axis=-1, keepdims=True) * (1.0 / n)))
    return dx, dy * xhat


ANY = pl.BlockSpec(memory_space=pl.ANY)


class _Side:
    def __init__(self, inputs, out_shapes, n_copies, build, aliases=()):
        self.inputs, self.out_shapes, self.n_copies = list(inputs), list(out_shapes), n_copies
        self.build, self.aliases = build, dict(aliases)


def _pcall(body, side, name, grid, in_specs, out_specs, out_shape, args, sem, scratch_shapes=(), prefetch=()):
    n_pre, n_in, n_out = len(prefetch), len(in_specs), len(out_specs)
    if side is None:
        outs = pl.pallas_call(
            body, name=name,
            grid_spec=pltpu.PrefetchScalarGridSpec(num_scalar_prefetch=n_pre, grid=grid, in_specs=list(in_specs),
                                                   out_specs=list(out_specs), scratch_shapes=list(scratch_shapes)),
            out_shape=list(out_shape), compiler_params=_params(sem),
        )(*prefetch, *args)
        return list(outs), []
    ns_in, ns_out = len(side.inputs), len(side.out_shapes)

    def wrapped(*refs):
        pre, r = refs[:n_pre], refs[n_pre:]
        ins, s_ins = r[:n_in], r[n_in:n_in + ns_in]
        outs = r[n_in + ns_in:n_in + ns_in + n_out]
        s_outs = r[n_in + ns_in + n_out:n_in + ns_in + n_out + ns_out]
        scr, send_sems, recv_sems = r[n_in + ns_in + n_out + ns_out:-2], r[-2], r[-1]
        pairs = side.build(s_ins, s_outs, send_sems, recv_sems)
        first = pl.program_id(0) == 0
        last = pl.program_id(0) == grid[0] - 1
        for d in range(1, len(grid)):
            first = first & (pl.program_id(d) == 0)
            last = last & (pl.program_id(d) == grid[d] - 1)

        @pl.when(first)
        def _():
            for send, _ in pairs:
                send.start()

        body(*pre, *ins, *outs, *scr)

        @pl.when(last)
        def _():
            for send, recv in pairs:
                send.wait_send()
                recv.wait_recv()

    outs = pl.pallas_call(
        wrapped, name=name,
        grid_spec=pltpu.PrefetchScalarGridSpec(
            num_scalar_prefetch=n_pre, grid=grid, in_specs=list(in_specs) + [ANY] * ns_in,
            out_specs=list(out_specs) + [ANY] * ns_out,
            scratch_shapes=list(scratch_shapes) + [pltpu.SemaphoreType.DMA((side.n_copies,)),
                                                   pltpu.SemaphoreType.DMA((side.n_copies,))]),
        out_shape=list(out_shape) + side.out_shapes,
        input_output_aliases={n_pre + n_in + i: n_out + o for i, o in side.aliases.items()},
        compiler_params=_params(("arbitrary",) * len(grid)),
    )(*prefetch, *args, *side.inputs)
    return list(outs[:n_out]), list(outs[n_out:])


def _matmul(a, b, mode, out_dtype, name, tm=1024, tn=1024, tk=None, side=None):
    if tk is None:
        tk = 4096 if mode == "tn" else 2048
    if mode == "nn":
        (m, k), (_, n) = a.shape, b.shape
    elif mode == "nt":
        (m, k), (n, _) = a.shape, b.shape
    else:
        (k, m), (_, n) = a.shape, b.shape
    tm, tn, tk = min(tm, m), min(tn, n), min(tk, k)
    nk = k // tk
    if mode == "tn":
        a_spec = pl.BlockSpec((tk, tm), lambda i, j, kk: (kk, i))
    else:
        a_spec = pl.BlockSpec((tm, tk), lambda i, j, kk: (i, kk))
    if mode == "nt":
        b_spec = pl.BlockSpec((tn, tk), lambda i, j, kk: (j, kk))
    else:
        b_spec = pl.BlockSpec((tk, tn), lambda i, j, kk: (kk, j))

    def body(a_ref, b_ref, o_ref, acc_ref):
        kk = pl.program_id(2)

        @pl.when(kk == 0)
        def _():
            acc_ref[...] = jnp.zeros_like(acc_ref)

        acc_ref[...] += _dot(a_ref[...].astype(BF16), b_ref[...].astype(BF16), mode)

        @pl.when(kk == nk - 1)
        def _():
            o_ref[...] = acc_ref[...].astype(out_dtype)

    def single_body(a_ref, b_ref, o_ref):
        o_ref[...] = _dot(a_ref[...].astype(BF16), b_ref[...].astype(BF16), mode).astype(out_dtype)

    outs, extra = _pcall(
        single_body if nk == 1 else body, side, name, (m // tm, n // tn, nk), [a_spec, b_spec],
        [pl.BlockSpec((tm, tn), lambda i, j, kk: (i, j))], [jax.ShapeDtypeStruct((m, n), out_dtype)], (a, b),
        ("parallel", "parallel", "arbitrary"), scratch_shapes=[] if nk == 1 else [pltpu.VMEM((tm, tn), F32)])
    return outs[0] if side is None else (outs[0], extra)


def _norm_fwd(x, g):
    s = x.shape[0]
    ts = min(TS_ROW, s)

    def body(x_ref, g_ref, h_ref):
        h_ref[...] = _rms_fwd(x_ref[...], g_ref[...], D_MODEL).astype(BF16)

    row = pl.BlockSpec((ts, D_MODEL), lambda i: (i, 0))
    return pl.pallas_call(
        body, name="norm_fwd", grid=(s // ts,),
        in_specs=[row, pl.BlockSpec((1, D_MODEL), lambda i: (0, 0))], out_specs=row,
        out_shape=jax.ShapeDtypeStruct((s, D_MODEL), BF16), compiler_params=_params(("parallel",)),
    )(x, g)


def _shift_rows(v, carry, j, ts, back):
    if back:
        main = pltpu.roll(v, j, 0)
        edge = pltpu.roll(jnp.concatenate([carry, v[:8]], axis=0), j, 0)[8:]
        return jnp.concatenate([edge, main[8:]], axis=0)
    main = pltpu.roll(v, ts - j, 0)
    edge = pltpu.roll(jnp.concatenate([v[ts - 8:], carry], axis=0), 16 - j, 0)[:8]
    return jnp.concatenate([main[:ts - 8], edge], axis=0)


def _conv_fwd(p_all, conv_w):
    s = p_all.shape[0]
    ts = min(TS_ROW, s)
    c0 = COL_C // 1024

    def body(cg_ref, bg_ref, u_ref, z_ref, w_ref, a_ref, co_ref, carry_ref):
        @pl.when(pl.program_id(0) == 0)
        def _():
            carry_ref[...] = jnp.zeros_like(carry_ref)

        p = cg_ref[...].astype(F32) * u_ref[...].astype(F32)
        carry = carry_ref[...]
        co = (w_ref[2:3, :] * p + w_ref[1:2, :] * _shift_rows(p, carry, 1, ts, True)
              + w_ref[0:1, :] * _shift_rows(p, carry, 2, ts, True))
        carry_ref[...] = p[ts - 8:]
        z = z_ref[...].astype(F32)
        a_ref[...] = (bg_ref[...].astype(F32) * co * (z * _sigmoid(z))).astype(BF16)
        co_ref[...] = co.astype(BF16)

    seg = lambda c: pl.BlockSpec((ts, 1024), lambda i: (i, c0 + c))
    return pl.pallas_call(
        body, name="conv_fwd", grid=(s // ts,),
        in_specs=[seg(0), seg(1), seg(2), seg(3), pl.BlockSpec((3, 1024), lambda i: (0, 0))],
        out_specs=[pl.BlockSpec((ts, 1024), lambda i: (i, 0))] * 2,
        out_shape=[jax.ShapeDtypeStruct((s, 1024), BF16)] * 2,
        scratch_shapes=[pltpu.VMEM((8, 1024), F32)],
        compiler_params=_params(("arbitrary",)),
    )(p_all, p_all, p_all, p_all, conv_w)


def _lora_fwd(p_all, gq, gkv, gkr, cs, sg, side=None):
    s = p_all.shape[0]
    ts = min(TS_ROW, s)

    def body(cq_ref, ckv_ref, kr_ref, gq_ref, gkv_ref, gkr_ref, cs_ref, sg_ref, cqn_ref, ckvn_ref, krope_ref):
        cqn_ref[...] = _rms_fwd(cq_ref[...].astype(F32), gq_ref[...], 512).astype(BF16)
        ckvn_ref[...] = _rms_fwd(ckv_ref[...].astype(F32), gkv_ref[...], 512).astype(BF16)
        kn = _rms_fwd(kr_ref[...].astype(F32), gkr_ref[...], QK_ROPE)
        krope_ref[...] = _rope(kn, cs_ref[...], sg_ref[...]).astype(BF16)

    row = lambda w: pl.BlockSpec((ts, w), lambda i: (i, 0))
    vec = lambda w: pl.BlockSpec((1, w), lambda i: (0, 0))
    return _pcall(
        body, side, "lora_fwd", (s // ts,),
        [pl.BlockSpec((ts, 512), lambda i: (i, COL_S // 512)),
         pl.BlockSpec((ts, 512), lambda i: (i, COL_S // 512 + 1)),
         pl.BlockSpec((ts, 128), lambda i: (i, (COL_S + 1024) // 128)),
         vec(512), vec(512), vec(128), row(128), row(128)],
        [row(512), row(512), row(128)],
        [jax.ShapeDtypeStruct((s, 512), BF16), jax.ShapeDtypeStruct((s, 512), BF16),
         jax.ShapeDtypeStruct((s, 128), BF16)],
        (p_all, p_all, p_all, gq, gkv, gkr, cs, sg), ("parallel",))


UP_HEADS = 4


def _q_up(cqn, w_uq, gn, gr, cs, sg):
    s = cqn.shape[0]
    tm, tn = min(TS_ROW, s), UP_HEADS * HEAD_PAD

    def body(c_ref, w_ref, gn_ref, gr_ref, cs_ref, sg_ref, q_ref, o_ref):
        qb = _dot(c_ref[...], w_ref[...]).astype(BF16)
        q_ref[...] = qb
        for h in range(UP_HEADS):
            q = qb[:, h * HEAD_PAD:(h + 1) * HEAD_PAD].astype(F32)
            a = _rms_fwd(q[:, :128], gn_ref[...], QK_NOPE)
            b = _rope(_rms_fwd(q[:, 128:], gr_ref[...], QK_ROPE), cs_ref[...], sg_ref[...])
            o_ref[h] = (jnp.concatenate([a, b], axis=1) * Q_SCALE).astype(BF16)

    vec = pl.BlockSpec((1, 128), lambda i, j: (0, 0))
    tab = pl.BlockSpec((tm, 128), lambda i, j: (i, 0))
    return pl.pallas_call(
        body, name="q_up", grid=(s // tm, MLA_HEADS // UP_HEADS),
        in_specs=[pl.BlockSpec((tm, 512), lambda i, j: (i, 0)), pl.BlockSpec((512, tn), lambda i, j: (0, j)),
                  vec, vec, tab, tab],
        out_specs=[pl.BlockSpec((tm, tn), lambda i, j: (i, j)),
                   pl.BlockSpec((UP_HEADS, tm, HEAD_PAD), lambda i, j: (j, i, 0))],
        out_shape=[jax.ShapeDtypeStruct((s, MLA_HEADS * HEAD_PAD), BF16),
                   jax.ShapeDtypeStruct((MLA_HEADS, s, HEAD_PAD), BF16)],
        compiler_params=_params(("parallel", "parallel")),
    )(cqn, w_uq, gn, gr, cs, sg)


def _kv_up(ckvn, w_ukv, krope, gk):
    s = ckvn.shape[0]
    tm, tn = min(TS_ROW, s), UP_HEADS * HEAD_PAD

    def body(c_ref, w_ref, kr_ref, gk_ref, kv_ref, o_ref, vt_ref):
        kvb = _dot(c_ref[...], w_ref[...]).astype(BF16)
        kv_ref[...] = kvb
        for h in range(UP_HEADS):
            a = _rms_fwd(kvb[:, h * HEAD_PAD:h * HEAD_PAD + 128].astype(F32), gk_ref[...], QK_NOPE)
            o_ref[h] = jnp.concatenate([a.astype(BF16), kr_ref[...]], axis=1)
            vt_ref[h] = kvb[:, h * HEAD_PAD + 128:(h + 1) * HEAD_PAD].astype(F32).T.astype(BF16)

    return pl.pallas_call(
        body, name="kv_up", grid=(s // tm, MLA_HEADS // UP_HEADS),
        in_specs=[pl.BlockSpec((tm, 512), lambda i, j: (i, 0)), pl.BlockSpec((512, tn), lambda i, j: (0, j)),
                  pl.BlockSpec((tm, 128), lambda i, j: (i, 0)), pl.BlockSpec((1, 128), lambda i, j: (0, 0))],
        out_specs=[pl.BlockSpec((tm, tn), lambda i, j: (i, j)),
                   pl.BlockSpec((UP_HEADS, tm, HEAD_PAD), lambda i, j: (j, i, 0)),
                   pl.BlockSpec((UP_HEADS, 128, tm), lambda i, j: (j, 0, i))],
        out_shape=[jax.ShapeDtypeStruct((s, MLA_HEADS * HEAD_PAD), BF16),
                   jax.ShapeDtypeStruct((MLA_HEADS, s, HEAD_PAD), BF16),
                   jax.ShapeDtypeStruct((MLA_HEADS, 128, s), BF16)],
        compiler_params=_params(("parallel", "parallel")),
    )(ckvn, w_ukv, krope, gk)


def _chunk_mask(tq, tk, row0=0):
    r = (lax.broadcasted_iota(jnp.int32, (tq, tk), 0) + row0) // CHUNK
    c = lax.broadcasted_iota(jnp.int32, (tq, tk), 1) // CHUNK
    return c <= r


def _causal_pairs(n, by_key):
    if by_key:
        pairs = [(i, j) for j in range(n) for i in range(j, n)]
    else:
        pairs = [(i, j) for i in range(n) for j in range(i + 1)]
    return (jnp.asarray([p[0] for p in pairs], jnp.int32), jnp.asarray([p[1] for p in pairs], jnp.int32))


def _flash_fwd(qcat, kcat, vt, p_all):
    s = qcat.shape[1]
    tq = tk = min(TQ_ATT, s)
    tc = min(TR_ATT, tq)
    nq = s // tq
    hg = FWD_HEADS_PER_STEP
    zc = COL_Z // (128 * hg)
    qi, kj = _causal_pairs(nq, False)

    def body(qi_ref, kj_ref, q_ref, k_ref, vt_ref, z_ref, y_ref, a_ref, lse_ref, m_sc, l_sc, acc_sc):
        t = pl.program_id(1)
        i, j = qi_ref[t], kj_ref[t]

        @pl.when(j == 0)
        def _():
            m_sc[...] = jnp.full_like(m_sc, NEG)
            l_sc[...] = jnp.zeros_like(l_sc)
            acc_sc[...] = jnp.zeros_like(acc_sc)

        def step(masked):
            nc = tq // tc
            state = {(g, c): [m_sc[g, :, c * tc:(c + 1) * tc], l_sc[g, :, c * tc:(c + 1) * tc],
                              acc_sc[g, :, c * tc:(c + 1) * tc]] for g in range(hg) for c in range(nc)}
            units = [(g, u, c) for u in range(tk // tc) for c in range(nc) for g in range(hg)
                     if (u <= c or not masked)]

            def scores(g, u, c):
                return _dot(k_ref[g, u * tc:(u + 1) * tc, :], q_ref[g, c * tc:(c + 1) * tc, :], "nt")

            def weighted_values(g, u, c, alpha, pb):
                state[g, c][2] = alpha * state[g, c][2] + _dot(vt_ref[g, :, u * tc:(u + 1) * tc], pb)

            ahead = 4
            pending = [scores(*un) for un in units[:ahead]]
            late = None
            for n, (g, u, c) in enumerate(units):
                st = pending.pop(0)
                if n + ahead < len(units):
                    pending.append(scores(*units[n + ahead]))
                if masked and u == c:
                    kc = lax.broadcasted_iota(jnp.int32, (tc, tc), 0) // CHUNK
                    qc = lax.broadcasted_iota(jnp.int32, (tc, tc), 1) // CHUNK
                    st = jnp.where(kc <= qc, st, NEG)
                m_run, l_run, _ = state[g, c]
                m_new = jnp.maximum(m_run, jnp.max(st, axis=0, keepdims=True))
                alpha = jnp.exp2(m_run - m_new)
                p = jnp.exp2(st - m_new)
                state[g, c][0] = m_new
                state[g, c][1] = alpha * l_run + jnp.sum(p, axis=0, keepdims=True)
                if late is not None:
                    weighted_values(*late)
                late = (g, u, c, alpha, p.astype(BF16))
            weighted_values(*late)
            for g in range(hg):
                m_sc[g] = jnp.concatenate([state[g, c][0] for c in range(nc)], axis=1)
                l_sc[g] = jnp.concatenate([state[g, c][1] for c in range(nc)], axis=1)
                acc_sc[g] = jnp.concatenate([state[g, c][2] for c in range(nc)], axis=1)

        @pl.when(j < i)
        def _():
            step(False)

        @pl.when(j == i)
        def _():
            step(True)
            for g in range(hg):
                cols = slice(g * 128, (g + 1) * 128)
                y = (acc_sc[g] * (1.0 / l_sc[g])).T
                z = z_ref[:, cols].astype(F32)
                y_ref[:, cols] = y.astype(BF16)
                a_ref[:, cols] = (y * (z * _sigmoid(z))).astype(BF16)
                lse2 = m_sc[g] + jnp.log(l_sc[g]) * (1.0 / LN2)
                lse_ref[g] = jnp.broadcast_to(lse2, (128, tq)).T

    wide = pl.BlockSpec((tq, 128 * hg), lambda h, t, qi_ref, kj_ref: (qi_ref[t], h))
    return pl.pallas_call(
        body, name="flash_fwd",
        grid_spec=pltpu.PrefetchScalarGridSpec(
            num_scalar_prefetch=2, grid=(MLA_HEADS // hg, qi.shape[0]),
            in_specs=[pl.BlockSpec((hg, tq, HEAD_PAD), lambda h, t, qi_ref, kj_ref: (h, qi_ref[t], 0)),
                      pl.BlockSpec((hg, tk, HEAD_PAD), lambda h, t, qi_ref, kj_ref: (h, kj_ref[t], 0)),
                      pl.BlockSpec((hg, 128, tk), lambda h, t, qi_ref, kj_ref: (h, 0, kj_ref[t])),
                      pl.BlockSpec((tq, 128 * hg), lambda h, t, qi_ref, kj_ref: (qi_ref[t], zc + h))],
            out_specs=[wide, wide,
                       pl.BlockSpec((hg, tq, 128), lambda h, t, qi_ref, kj_ref: (h, qi_ref[t], 0))],
            scratch_shapes=[pltpu.VMEM((hg, 1, tq), F32), pltpu.VMEM((hg, 1, tq), F32),
                            pltpu.VMEM((hg, 128, tq), F32)]),
        out_shape=[jax.ShapeDtypeStruct((s, MLA_HEADS * 128), BF16),
                   jax.ShapeDtypeStruct((s, MLA_HEADS * 128), BF16),
                   jax.ShapeDtypeStruct((MLA_HEADS, s, 128), F32)],
        compiler_params=_params(("parallel", "arbitrary")),
    )(qi, kj, qcat, kcat, vt, p_all)


def _mem_kv_fwd(mem, g_norm, w_kv, g_kn):
    m = mem.shape[0]

    def body(mem_ref, g_ref, w_ref, gk_ref, memn_ref, kraw_ref, kn_ref, v_ref):
        memn = _rms_fwd(mem_ref[...], g_ref[...], D_MODEL).astype(BF16)
        memn_ref[...] = memn
        kvm = _dot(memn, w_ref[...])
        kraw_ref[...] = kvm[:, :1024]
        v_ref[...] = kvm[:, 1024:].astype(BF16)
        for hh in range(MEM_HEADS):
            sl = slice(hh * MEM_HEAD_DIM, (hh + 1) * MEM_HEAD_DIM)
            kn_ref[:, sl] = _rms_fwd(kvm[:, sl], gk_ref[...], MEM_HEAD_DIM).astype(BF16)

    return pl.pallas_call(
        body, name="mem_kv_fwd",
        out_shape=[jax.ShapeDtypeStruct((m, D_MODEL), BF16), jax.ShapeDtypeStruct((m, 1024), F32),
                   jax.ShapeDtypeStruct((m, 1024), BF16), jax.ShapeDtypeStruct((m, 1024), BF16)],
        compiler_params=_params(),
    )(mem, g_norm, w_kv, g_kn)


def _mem_attn_fwd(p_all, kn, v, gq):
    s = p_all.shape[0]
    tq = min(TS_ROW, s)

    def body(q_ref, z_ref, kn_ref, v_ref, gq_ref, a_ref):
        z = z_ref[...].astype(F32)
        gate = z * _sigmoid(z)
        for hh in range(MEM_HEADS):
            sl = slice(hh * MEM_HEAD_DIM, (hh + 1) * MEM_HEAD_DIM)
            qn = _rms_fwd(q_ref[:, sl].astype(F32), gq_ref[...], MEM_HEAD_DIM).astype(BF16)
            sc = _dot(qn, kn_ref[:, sl], "nt") * MEM_SCALE
            e = jnp.exp(sc - jnp.max(sc, axis=-1, keepdims=True))
            p = e * (1.0 / jnp.sum(e, axis=-1, keepdims=True))
            y = _dot(p.astype(BF16), v_ref[:, sl])
            a_ref[:, sl] = (y * gate[:, sl]).astype(BF16)

    full = lambda a: pl.BlockSpec(a.shape, lambda i: (0, 0))
    return pl.pallas_call(
        body, name="mem_attn_fwd", grid=(s // tq,),
        in_specs=[pl.BlockSpec((tq, 1024), lambda i: (i, COL_M // 1024)),
                  pl.BlockSpec((tq, 1024), lambda i: (i, COL_M // 1024 + 1)),
                  full(kn), full(v), full(gq)],
        out_specs=pl.BlockSpec((tq, 1024), lambda i: (i, 0)),
        out_shape=jax.ShapeDtypeStruct((s, 1024), BF16),
        compiler_params=_params(("parallel",)),
    )(p_all, p_all, kn, v, gq)


def _merge_fwd(a_conv, w_c, a_mla, w_m, a_mem, w_e, p_all):
    s = a_conv.shape[0]
    tm, tn = min(TS_ROW, s), 1024
    nj = D_MODEL // tn

    def body(ac_ref, wc_ref, am_ref, wm_ref, ae_ref, we_ref, gc_ref, gm_ref, ge_ref,
             oc_ref, om_ref, oe_ref, mg_ref):
        oc = _dot(ac_ref[...], wc_ref[...])
        om = _dot(am_ref[...], wm_ref[...])
        oe = _dot(ae_ref[...], we_ref[...])
        oc_ref[...] = oc.astype(BF16)
        om_ref[...] = om.astype(BF16)
        oe_ref[...] = oe.astype(BF16)
        mg_ref[...] = (_sigmoid(gc_ref[...].astype(F32)) * oc + _sigmoid(gm_ref[...].astype(F32)) * om
                       + _sigmoid(ge_ref[...].astype(F32)) * oe).astype(BF16)

    act = lambda k: pl.BlockSpec((tm, k), lambda i, j: (i, 0))
    wgt = lambda k: pl.BlockSpec((k, tn), lambda i, j: (0, j))
    gate = lambda b: pl.BlockSpec((tm, tn), lambda i, j: (i, b * nj + j))
    out = pl.BlockSpec((tm, tn), lambda i, j: (i, j))
    return pl.pallas_call(
        body, name="merge_fwd", grid=(s // tm, nj),
        in_specs=[act(1024), wgt(1024), act(2048), wgt(2048), act(1024), wgt(1024), gate(0), gate(1), gate(2)],
        out_specs=[out] * 4,
        out_shape=[jax.ShapeDtypeStruct((s, D_MODEL), BF16)] * 4,
        compiler_params=_params(("parallel", "parallel")),
    )(a_conv, w_c, a_mla, w_m, a_mem, w_e, p_all, p_all, p_all)


def _out_fwd(merged, w_o, x, target):
    s = merged.shape[0]
    tm, tn = min(TS_ROW, s), 1024
    nj = D_MODEL // tn

    def body(mg_ref, w_ref, x_ref, t_ref, dy_ref, dyb_ref, ls_ref):
        e = x_ref[...] + _dot(mg_ref[...], w_ref[...]) - t_ref[...]
        dy = e * (1.0 / D_MODEL)
        dy_ref[...] = dy
        dyb_ref[...] = dy.astype(BF16)
        r = _fold8(e * e)
        acc = r[:, 0:128]
        for cc in range(1, tn // 128):
            acc = acc + r[:, cc * 128:(cc + 1) * 128]
        ls_ref[...] = acc

    tile = pl.BlockSpec((tm, tn), lambda i, j: (i, j))
    return pl.pallas_call(
        body, name="out_fwd", grid=(s // tm, nj),
        in_specs=[pl.BlockSpec((tm, D_MODEL), lambda i, j: (i, 0)),
                  pl.BlockSpec((D_MODEL, tn), lambda i, j: (0, j)), tile, tile],
        out_specs=[tile, tile, pl.BlockSpec((8, 128), lambda i, j: (i, j))],
        out_shape=[jax.ShapeDtypeStruct((s, D_MODEL), F32), jax.ShapeDtypeStruct((s, D_MODEL), BF16),
                   jax.ShapeDtypeStruct((s // tm * 8, nj * 128), F32)],
        compiler_params=_params(("parallel", "parallel")),
    )(merged, w_o, x, target)


def _merge_bwd(dyb, w_o, p_all, o_c, o_m, o_e):
    s = dyb.shape[0]
    tm = min(256, s)

    def body(dy_ref, w_ref, g_ref, oc_ref, om_ref, oe_ref, dc_ref, dm_ref, de_ref, dg_ref):
        dmg = _dot(dy_ref[...], w_ref[...], "nt")
        for b, (o_ref, d_ref) in enumerate(((oc_ref, dc_ref), (om_ref, dm_ref), (oe_ref, de_ref))):
            sl = slice(b * D_MODEL, (b + 1) * D_MODEL)
            sg = _sigmoid(g_ref[:, sl].astype(F32))
            d_ref[...] = (dmg * sg).astype(BF16)
            dg_ref[:, sl] = (dmg * o_ref[...].astype(F32) * sg * (1.0 - sg)).astype(BF16)

    row = pl.BlockSpec((tm, D_MODEL), lambda i: (i, 0))
    wide = pl.BlockSpec((tm, 3 * D_MODEL), lambda i: (i, 0))
    return pl.pallas_call(
        body, name="merge_bwd", grid=(s // tm,),
        in_specs=[row, pl.BlockSpec((D_MODEL, D_MODEL), lambda i: (0, 0)), wide, row, row, row],
        out_specs=[row, row, row, wide],
        out_shape=[jax.ShapeDtypeStruct((s, D_MODEL), BF16)] * 3 + [jax.ShapeDtypeStruct((s, 3 * D_MODEL), BF16)],
        compiler_params=_params(("parallel",)),
    )(dyb, w_o, p_all, o_c, o_m, o_e)


def _conv_bwd(da, p_all, co, conv_w):
    s = da.shape[0]
    ts = min(TS_ROW, s)
    n = s // ts
    c0 = COL_C // 1024

    def body(da_ref, cg_ref, bg_ref, u_ref, z_ref, co_ref, w_ref, dp_ref, dw_ref, carry_ref):
        @pl.when(pl.program_id(0) == 0)
        def _():
            carry_ref[...] = jnp.zeros_like(carry_ref)
            dw_ref[...] = jnp.zeros_like(dw_ref)

        da_ = da_ref[...].astype(F32)
        cg, bg = cg_ref[...].astype(F32), bg_ref[...].astype(F32)
        u, z, cov = u_ref[...].astype(F32), z_ref[...].astype(F32), co_ref[...].astype(F32)
        sz = _sigmoid(z)
        dyc = da_ * (z * sz)
        dz = da_ * (bg * cov) * (sz * (1.0 + z * (1.0 - sz)))
        db = dyc * cov
        dco = dyc * bg
        carry = carry_ref[...]
        d1 = _shift_rows(dco, carry, 1, ts, False)
        d2 = _shift_rows(dco, carry, 2, ts, False)
        carry_ref[...] = dco[:8]
        dpp = w_ref[2:3, :] * dco + w_ref[1:2, :] * d1 + w_ref[0:1, :] * d2
        p = cg * u
        dw_ref[0] += _fold8(p * d2)
        dw_ref[1] += _fold8(p * d1)
        dw_ref[2] += _fold8(p * dco)
        dp_ref[...] = jnp.concatenate([dpp * u, db, dpp * cg, dz], axis=1).astype(BF16)

    rev = lambda c: pl.BlockSpec((ts, 1024), lambda i: (n - 1 - i, c))
    return pl.pallas_call(
        body, name="conv_bwd", grid=(n,),
        in_specs=[rev(0), rev(c0), rev(c0 + 1), rev(c0 + 2), rev(c0 + 3), rev(0),
                  pl.BlockSpec((3, 1024), lambda i: (0, 0))],
        out_specs=[pl.BlockSpec((ts, 4096), lambda i: (n - 1 - i, 0)),
                   pl.BlockSpec((3, 8, 1024), lambda i: (0, 0, 0))],
        out_shape=[jax.ShapeDtypeStruct((s, 4096), BF16), jax.ShapeDtypeStruct((3, 8, 1024), F32)],
        scratch_shapes=[pltpu.VMEM((8, 1024), F32)],
        compiler_params=_params(("arbitrary",)),
    )(da, p_all, p_all, p_all, p_all, co, conv_w)


def _mla_gate_bwd(do_m, w_mla_out, y, p_all):
    s = do_m.shape[0]
    tm, tn = min(TS_ROW, s), 1024
    heads = tn // 128

    def body(do_ref, w_ref, y_ref, z_ref, dy_ref, dz_ref, dl_ref):
        da_ = _dot(do_ref[...], w_ref[...], "nt").astype(BF16).astype(F32)
        yv, z = y_ref[...].astype(F32), z_ref[...].astype(F32)
        sz = _sigmoid(z)
        dyv = da_ * (z * sz)
        dy_ref[...] = dyv.astype(BF16)
        dz_ref[...] = (da_ * yv * (sz * (1.0 + z * (1.0 - sz)))).astype(BF16)
        pr = dyv * yv
        for h in range(heads):
            dl_ref[h] = jnp.broadcast_to(jnp.sum(pr[:, h * 128:(h + 1) * 128], axis=-1, keepdims=True), (tm, 128))

    tile = pl.BlockSpec((tm, tn), lambda i, j: (i, j))
    return pl.pallas_call(
        body, name="mla_gate_bwd", grid=(s // tm, D_MODEL // tn),
        in_specs=[pl.BlockSpec((tm, D_MODEL), lambda i, j: (i, 0)), pl.BlockSpec((tn, D_MODEL), lambda i, j: (j, 0)),
                  tile, pl.BlockSpec((tm, tn), lambda i, j: (i, COL_Z // tn + j))],
        out_specs=[tile, tile, pl.BlockSpec((heads, tm, 128), lambda i, j: (j, i, 0))],
        out_shape=[jax.ShapeDtypeStruct((s, D_MODEL), BF16)] * 2 + [jax.ShapeDtypeStruct((MLA_HEADS, s, 128), F32)],
        compiler_params=_params(("parallel", "parallel")),
    )(do_m, w_mla_out, y, p_all)


def _flash_bwd(qcat, kcat, kv, dy, lse, delta):
    s = qcat.shape[1]
    tq = tk = min(TQ_ATT, s)
    tr = min(TR_ATT, tq)
    nq = s // tq
    qi, kj = _causal_pairs(nq, True)
    npairs = qi.shape[0]

    def body(qi_ref, kj_ref, q_ref, k_ref, v_ref, do_ref, lse_ref, dl_ref, dq_ref, dk_ref, dv_ref,
             dq_acc, dk_acc, dv_acc):
        t = pl.program_id(1)
        i, j = qi_ref[t], kj_ref[t]

        def add_dq(rows, dq_new):
            @pl.when(j == 0)
            def _():
                dq_acc[rows, :] = dq_new

            @pl.when(j > 0)
            def _():
                dq_acc[rows, :] += dq_new

        def full_step():
            q, k = q_ref[0], k_ref[0]
            p = jnp.exp2(_dot(q, k, "nt") - lse_ref[0][:, 0:1])
            do = do_ref[...]
            dpv = _dot(do, v_ref[...], "nt")
            ds = (p * (dpv - dl_ref[0][:, 0:1])).astype(BF16)
            dv_acc[...] += _dot(p.astype(BF16), do, "tn")
            dk_acc[...] += _dot(ds, q, "tn")
            add_dq(pl.ds(pl.multiple_of(i * tq, tq), tq), _dot(ds, k))

        def diagonal_step():
            def keys(r):
                return (r + 1) * tr

            def scores(r):
                rows = slice(r * tr, (r + 1) * tr)
                return (_dot(q_ref[0, rows, :], k_ref[0, :keys(r), :], "nt"),
                        _dot(do_ref[rows, :], v_ref[:keys(r), :], "nt"))

            def gradients(n, r, pb, ds):
                nk = keys(r)
                rows = slice(r * tr, (r + 1) * tr)
                dv_new = _dot(pb, do_ref[rows, :], "tn")
                dk_new = _dot(ds, q_ref[0, rows, :], "tn")
                if n == 0:
                    dv_acc[...] = dv_new
                    dk_acc[...] = dk_new
                else:
                    dv_acc[:nk, :] += dv_new
                    dk_acc[:nk, :] += dk_new
                add_dq(pl.ds(pl.multiple_of(i * tq + r * tr, tr), tr), _dot(ds, k_ref[0, :nk, :]))

            order = list(range(tq // tr - 1, -1, -1))
            pending = [scores(r) for r in order[:2]]
            late = None
            for n, r in enumerate(order):
                sc, dpv = pending.pop(0)
                if n + 2 < len(order):
                    pending.append(scores(order[n + 2]))
                rows = slice(r * tr, (r + 1) * tr)
                sc = jnp.where(_chunk_mask(tr, keys(r), r * tr), sc, NEG)
                p = jnp.exp2(sc - lse_ref[0][rows, 0:1])
                ds = (p * (dpv - dl_ref[0][rows, 0:1])).astype(BF16)
                if late is not None:
                    gradients(*late)
                late = (n, r, p.astype(BF16), ds)
            gradients(*late)

        @pl.when(i == j)
        def _():
            diagonal_step()

        @pl.when(i > j)
        def _():
            full_step()

        @pl.when(i == nq - 1)
        def _():
            dk_ref[0] = dk_acc[...].astype(BF16)
            dv_ref[0] = dv_acc[...].astype(BF16)

        @pl.when(t == npairs - 1)
        def _():
            dq_ref[0] = dq_acc[...].astype(BF16)

    qrow = lambda w: pl.BlockSpec((1, tq, w), lambda h, t, qi_ref, kj_ref: (h, qi_ref[t], 0))
    krow = lambda w: pl.BlockSpec((1, tk, w), lambda h, t, qi_ref, kj_ref: (h, kj_ref[t], 0))
    return pl.pallas_call(
        body, name="flash_bwd",
        grid_spec=pltpu.PrefetchScalarGridSpec(
            num_scalar_prefetch=2, grid=(MLA_HEADS, npairs),
            in_specs=[qrow(HEAD_PAD), krow(HEAD_PAD),
                      pl.BlockSpec((tk, 128), lambda h, t, qi_ref, kj_ref: (kj_ref[t], 2 * h + 1)),
                      pl.BlockSpec((tq, 128), lambda h, t, qi_ref, kj_ref: (qi_ref[t], h)),
                      qrow(128), qrow(128)],
            out_specs=[pl.BlockSpec((1, s, HEAD_PAD), lambda h, t, qi_ref, kj_ref: (h, 0, 0)),
                       krow(HEAD_PAD), krow(128)],
            scratch_shapes=[pltpu.VMEM((s, HEAD_PAD), F32), pltpu.VMEM((tk, HEAD_PAD), F32),
                            pltpu.VMEM((tk, 128), F32)]),
        out_shape=[jax.ShapeDtypeStruct((MLA_HEADS, s, HEAD_PAD), BF16),
                   jax.ShapeDtypeStruct((MLA_HEADS, s, HEAD_PAD), BF16),
                   jax.ShapeDtypeStruct((MLA_HEADS, s, 128), BF16)],
        compiler_params=_params(("parallel", "arbitrary")),
    )(qi, kj, qcat, kcat, kv, dy, lse, delta)


def _q_prep_bwd(dqcat, q_raw, gn, gr, cs, sg, side=None):
    s = q_raw.shape[0]
    ts = min(TS_BWD, s)

    def body(dq_ref, q_ref, gn_ref, gr_ref, cs_ref, sg_ref, o_ref, dg_ref):
        @pl.when(pl.program_id(0) == 0)
        def _():
            dg_ref[...] = jnp.zeros_like(dg_ref)

        ga_acc = jnp.zeros((8, 128), F32)
        gb_acc = jnp.zeros((8, 128), F32)
        for h in range(MLA_HEADS):
            cols = slice(h * HEAD_PAD, (h + 1) * HEAD_PAD)
            q = q_ref[:, cols].astype(F32)
            d = dq_ref[h].astype(F32) * MLA_SCALE
            da, ga = _rms_bwd(q[:, :128], gn_ref[...], d[:, :128], QK_NOPE)
            db, gb = _rms_bwd(q[:, 128:], gr_ref[...], _rope_bwd(d[:, 128:], cs_ref[...], sg_ref[...]), QK_ROPE)
            o_ref[:, cols] = jnp.concatenate([da, db], axis=1).astype(BF16)
            ga_acc = ga_acc + _fold8(ga)
            gb_acc = gb_acc + _fold8(gb)
        dg_ref[0] += ga_acc
        dg_ref[1] += gb_acc

    wide = pl.BlockSpec((ts, MLA_HEADS * HEAD_PAD), lambda i: (i, 0))
    return _pcall(
        body, side, "q_prep_bwd", (s // ts,),
        [pl.BlockSpec((MLA_HEADS, ts, HEAD_PAD), lambda i: (0, i, 0)), wide,
         pl.BlockSpec((1, 128), lambda i: (0, 0)), pl.BlockSpec((1, 128), lambda i: (0, 0)),
         pl.BlockSpec((ts, 128), lambda i: (i, 0)), pl.BlockSpec((ts, 128), lambda i: (i, 0))],
        [wide, pl.BlockSpec((2, 8, 128), lambda i: (0, 0, 0))],
        [jax.ShapeDtypeStruct((s, MLA_HEADS * HEAD_PAD), BF16), jax.ShapeDtypeStruct((2, 8, 128), F32)],
        (dqcat, q_raw, gn, gr, cs, sg), ("arbitrary",))


def _k_prep_bwd(dkcat, dv, kv, gk, side=None):
    s = kv.shape[0]
    ts = min(TS_BWD, s)

    def body(dk_ref, dv_ref, k_ref, gk_ref, o_ref, dkr_ref, dg_ref):
        @pl.when(pl.program_id(0) == 0)
        def _():
            dg_ref[...] = jnp.zeros_like(dg_ref)

        g_acc = jnp.zeros((8, 128), F32)
        dkr = jnp.zeros((ts, 128), F32)
        for h in range(MLA_HEADS):
            d = dk_ref[h].astype(F32) * LN2
            k = k_ref[:, h * HEAD_PAD:h * HEAD_PAD + 128].astype(F32)
            dk_raw, gg = _rms_bwd(k, gk_ref[...], d[:, :128], QK_NOPE)
            o_ref[:, h * HEAD_PAD:(h + 1) * HEAD_PAD] = jnp.concatenate([dk_raw.astype(BF16), dv_ref[h]], axis=1)
            g_acc = g_acc + _fold8(gg)
            dkr = dkr + d[:, 128:]
        dkr_ref[...] = dkr
        dg_ref[...] += g_acc

    wide = pl.BlockSpec((ts, MLA_HEADS * HEAD_PAD), lambda i: (i, 0))
    return _pcall(
        body, side, "k_prep_bwd", (s // ts,),
        [pl.BlockSpec((MLA_HEADS, ts, HEAD_PAD), lambda i: (0, i, 0)),
         pl.BlockSpec((MLA_HEADS, ts, 128), lambda i: (0, i, 0)), wide,
         pl.BlockSpec((1, 128), lambda i: (0, 0))],
        [wide, pl.BlockSpec((ts, 128), lambda i: (i, 0)), pl.BlockSpec((8, 128), lambda i: (0, 0))],
        [jax.ShapeDtypeStruct((s, MLA_HEADS * HEAD_PAD), BF16), jax.ShapeDtypeStruct((s, 128), F32),
         jax.ShapeDtypeStruct((8, 128), F32)],
        (dkcat, dv, kv, gk), ("arbitrary",))


def _lora_bwd(dcqn, dckvn, dkr, p_all, gq, gkv, gkr, cs, sg):
    s = p_all.shape[0]
    ts = min(TS_ROW, s)

    def body(dq_ref, dkv_ref, dkr_ref, cq_ref, ckv_ref, kr_ref, gq_ref, gkv_ref, gkr_ref, cs_ref, sg_ref,
             o_ref, dgq_ref, dgkv_ref, dgkr_ref):
        @pl.when(pl.program_id(0) == 0)
        def _():
            dgq_ref[...] = jnp.zeros_like(dgq_ref)
            dgkv_ref[...] = jnp.zeros_like(dgkv_ref)
            dgkr_ref[...] = jnp.zeros_like(dgkr_ref)

        dq, g1 = _rms_bwd(cq_ref[...].astype(F32), gq_ref[...], dq_ref[...], 512)
        dkv, g2 = _rms_bwd(ckv_ref[...].astype(F32), gkv_ref[...], dkv_ref[...], 512)
        dkn = _rope_bwd(dkr_ref[...], cs_ref[...], sg_ref[...])
        dk, g3 = _rms_bwd(kr_ref[...].astype(F32), gkr_ref[...], dkn, QK_ROPE)
        o_ref[...] = jnp.concatenate([dq, dkv, dk, jnp.zeros((ts, 2048 - 1152), F32)], axis=1).astype(BF16)
        dgq_ref[...] += _fold8(g1)
        dgkv_ref[...] += _fold8(g2)
        dgkr_ref[...] += _fold8(g3)

    row = lambda w: pl.BlockSpec((ts, w), lambda i: (i, 0))
    vec = lambda w: pl.BlockSpec((1, w), lambda i: (0, 0))
    acc = lambda w: pl.BlockSpec((8, w), lambda i: (0, 0))
    return pl.pallas_call(
        body, name="lora_bwd", grid=(s // ts,),
        in_specs=[row(512), row(512), row(128),
                  pl.BlockSpec((ts, 512), lambda i: (i, COL_S // 512)),
                  pl.BlockSpec((ts, 512), lambda i: (i, COL_S // 512 + 1)),
                  pl.BlockSpec((ts, 128), lambda i: (i, (COL_S + 1024) // 128)),
                  vec(512), vec(512), vec(128), row(128), row(128)],
        out_specs=[row(2048), acc(512), acc(512), acc(128)],
        out_shape=[jax.ShapeDtypeStruct((s, 2048), BF16), jax.ShapeDtypeStruct((8, 512), F32),
                   jax.ShapeDtypeStruct((8, 512), F32), jax.ShapeDtypeStruct((8, 128), F32)],
        compiler_params=_params(("arbitrary",)),
    )(dcqn, dckvn, dkr, p_all, p_all, p_all, gq, gkv, gkr, cs, sg)


def _mem_attn_bwd(da, p_all, kn, v, gq):
    s = p_all.shape[0]
    tq = min(TS_ROW, s)

    def body(da_ref, q_ref, z_ref, kn_ref, v_ref, gq_ref, o_ref, dkn_ref, dv_ref, dg_ref):
        @pl.when(pl.program_id(0) == 0)
        def _():
            dkn_ref[...] = jnp.zeros_like(dkn_ref)
            dv_ref[...] = jnp.zeros_like(dv_ref)
            dg_ref[...] = jnp.zeros_like(dg_ref)

        z = z_ref[...].astype(F32)
        da_ = da_ref[...].astype(F32)
        sz = _sigmoid(z)
        gate = z * sz
        dgate = sz * (1.0 + z * (1.0 - sz))
        for hh in range(MEM_HEADS):
            sl = slice(hh * MEM_HEAD_DIM, (hh + 1) * MEM_HEAD_DIM)
            qf = q_ref[:, sl].astype(F32)
            qn = _rms_fwd(qf, gq_ref[...], MEM_HEAD_DIM).astype(BF16)
            knh, vh = kn_ref[:, sl], v_ref[:, sl]
            sc = _dot(qn, knh, "nt") * MEM_SCALE
            e = jnp.exp(sc - jnp.max(sc, axis=-1, keepdims=True))
            p = e * (1.0 / jnp.sum(e, axis=-1, keepdims=True))
            pb = p.astype(BF16)
            y = _dot(pb, vh)
            dyh = da_[:, sl] * gate[:, sl]
            o_ref[:, 1024 + hh * MEM_HEAD_DIM:1024 + (hh + 1) * MEM_HEAD_DIM] = (
                da_[:, sl] * y * dgate[:, sl]).astype(BF16)
            dyb = dyh.astype(BF16)
            dpm = _dot(dyb, vh, "nt")
            ds = (p * (dpm - jnp.sum(dpm * p, axis=-1, keepdims=True)) * MEM_SCALE).astype(BF16)
            dqn = _dot(ds, knh)
            dkn_ref[:, sl] += _dot(ds, qn, "tn")
            dv_ref[:, sl] += _dot(pb, dyb, "tn")
            dq, gg = _rms_bwd(qf, gq_ref[...], dqn, MEM_HEAD_DIM)
            o_ref[:, sl] = dq.astype(BF16)
            dg_ref[...] += _fold8(gg)

    full = lambda a: pl.BlockSpec(a.shape, lambda i: (0, 0))
    return pl.pallas_call(
        body, name="mem_attn_bwd", grid=(s // tq,),
        in_specs=[pl.BlockSpec((tq, 1024), lambda i: (i, 0)),
                  pl.BlockSpec((tq, 1024), lambda i: (i, COL_M // 1024)),
                  pl.BlockSpec((tq, 1024), lambda i: (i, COL_M // 1024 + 1)),
                  full(kn), full(v), full(gq)],
        out_specs=[pl.BlockSpec((tq, 2048), lambda i: (i, 0)),
                   pl.BlockSpec((MEM_TOKENS, 1024), lambda i: (0, 0)),
                   pl.BlockSpec((MEM_TOKENS, 1024), lambda i: (0, 0)),
                   pl.BlockSpec((8, MEM_HEAD_DIM), lambda i: (0, 0))],
        out_shape=[jax.ShapeDtypeStruct((s, 2048), BF16), jax.ShapeDtypeStruct((MEM_TOKENS, 1024), F32),
                   jax.ShapeDtypeStruct((MEM_TOKENS, 1024), F32), jax.ShapeDtypeStruct((8, MEM_HEAD_DIM), F32)],
        compiler_params=_params(("arbitrary",)),
    )(da, p_all, p_all, kn, v, gq)


def _mem_kv_bwd(dkn, dv, kraw, memn, mem, g_norm, w_kv, g_kn):
    m = mem.shape[0]

    def body(dkn_ref, dv_ref, kraw_ref, memn_ref, mem_ref, g_ref, w_ref, gk_ref, dw_ref, dgn_ref, dgk_ref, dkv_sc):
        gk_acc = jnp.zeros((8, MEM_HEAD_DIM), F32)
        for hh in range(MEM_HEADS):
            sl = slice(hh * MEM_HEAD_DIM, (hh + 1) * MEM_HEAD_DIM)
            dk, gg = _rms_bwd(kraw_ref[:, sl], gk_ref[...], dkn_ref[:, sl], MEM_HEAD_DIM)
            dkv_sc[:, sl] = dk.astype(BF16)
            gk_acc = gk_acc + _fold8(gg)
        dgk_ref[...] = gk_acc
        dkv_sc[:, 1024:] = dv_ref[...].astype(BF16)
        dkv = dkv_sc[...]
        dw_ref[...] = _dot(memn_ref[...], dkv, "tn")
        dmemn = _dot(dkv, w_ref[...], "nt")
        xf = mem_ref[...]
        r = lax.rsqrt(jnp.mean(xf * xf, axis=-1, keepdims=True) + EPS)
        dgn_ref[...] = _fold8(dmemn * (xf * r))

    return pl.pallas_call(
        body, name="mem_kv_bwd",
        out_shape=[jax.ShapeDtypeStruct((D_MODEL, D_MODEL), F32), jax.ShapeDtypeStruct((8, D_MODEL), F32),
                   jax.ShapeDtypeStruct((8, MEM_HEAD_DIM), F32)],
        scratch_shapes=[pltpu.VMEM((m, D_MODEL), BF16)],
        compiler_params=_params(),
    )(dkn, dv, kraw, memn, mem, g_norm, w_kv, g_kn)


def _dh_bwd(dps, w_all_t, row0, nrows, name, side=None):
    tm, tk = min(TS_ROW, nrows), 2048
    blk0 = row0 // tm
    widths = [a.shape[1] // tk for a in dps]
    starts = [int(v) for v in np.cumsum([0] + widths[:-1])]
    nk = sum(widths)

    def body(*refs):
        d_refs, w_ref, o_ref, acc_ref = refs[:5], refs[5], refs[6], refs[7]
        kk = pl.program_id(1)

        @pl.when(kk == 0)
        def _():
            acc_ref[...] = jnp.zeros_like(acc_ref)

        for d_ref, st, wd in zip(d_refs, starts, widths):
            @pl.when((kk >= st) & (kk < st + wd))
            def _(d_ref=d_ref):
                acc_ref[...] += _dot(d_ref[...], w_ref[...])

        @pl.when(kk == nk - 1)
        def _():
            o_ref[...] = acc_ref[...]

    def dspec(st, wd):
        return pl.BlockSpec((tm, tk), lambda i, kk: (blk0 + i, jnp.clip(kk - st, 0, wd - 1)))

    outs, extra = _pcall(
        body, side, name, (nrows // tm, nk),
        [dspec(st, wd) for st, wd in zip(starts, widths)] + [pl.BlockSpec((tk, D_MODEL), lambda i, kk: (kk, 0))],
        [pl.BlockSpec((tm, D_MODEL), lambda i, kk: (i, 0))], [jax.ShapeDtypeStruct((nrows, D_MODEL), F32)],
        (*dps, w_all_t), ("parallel", "arbitrary"), scratch_shapes=[pltpu.VMEM((tm, D_MODEL), F32)])
    return outs[0], extra


def _dx_bwd(dh_a, dh_b, x, dy, g, side=None):
    s = x.shape[0]
    ts = min(TS_BWD, dh_a.shape[0])
    na, nb = dh_a.shape[0] // ts, dh_b.shape[0] // ts

    def body(dha_ref, dhb_ref, x_ref, dy_ref, g_ref, o_ref, dg_ref):
        @pl.when(pl.program_id(0) == 0)
        def _():
            dg_ref[...] = jnp.zeros_like(dg_ref)

        dh = jnp.where(pl.program_id(0) < na, dha_ref[...], dhb_ref[...])
        dx, gg = _rms_bwd(x_ref[...], g_ref[...], dh, D_MODEL)
        o_ref[...] = dy_ref[...] + dx
        dg_ref[...] += _fold8(gg)

    row = pl.BlockSpec((ts, D_MODEL), lambda i: (i, 0))
    return _pcall(
        body, side, "dx_bwd", (s // ts,),
        [pl.BlockSpec((ts, D_MODEL), lambda i: (jnp.minimum(i, na - 1), 0)),
         pl.BlockSpec((ts, D_MODEL), lambda i: (jnp.clip(i - na, 0, nb - 1), 0)),
         row, row, pl.BlockSpec((1, D_MODEL), lambda i: (0, 0))],
        [row, pl.BlockSpec((8, D_MODEL), lambda i: (0, 0))],
        [jax.ShapeDtypeStruct((s, D_MODEL), F32), jax.ShapeDtypeStruct((8, D_MODEL), F32)],
        (dh_a, dh_b, x, dy, g), ("arbitrary",))


def _pad128(v, n):
    return jnp.pad(v.reshape(1, n), ((0, 0), (0, 128 - n)))


def _local_step(x, positions, mem, target, comm, gains):
    half = QK_ROPE // 2
    inv_freq = jnp.power(ROPE_THETA, -jnp.arange(half, dtype=F32) / half)
    ang = positions.astype(F32)[:, None] * inv_freq
    cos, sin = jnp.cos(ang), jnp.sin(ang)
    zpad = jnp.zeros((x.shape[0], 64), F32)
    cs = jnp.concatenate([cos, cos, zpad], axis=1)
    sg = jnp.concatenate([-sin, sin, zpad], axis=1)
    g_qr, g_kr = _pad128(gains["mla_qn_rope_g"], 64), _pad128(gains["mla_kn_rope_g"], 64)
    g_qn, g_kn = gains["mla_qn_nope_g"], gains["mla_kn_nope_g"]

    def hosted_matmul(tag, a, b, mode, out_dtype, **tiles):
        side = comm.side(tag)
        if side is None:
            return _matmul(a, b, mode, out_dtype, tag, **tiles)
        out, extra = _matmul(a, b, mode, out_dtype, tag, side=side, **tiles)
        comm.done(tag, extra)
        return out

    h = _norm_fwd(x, gains["norm_g"])
    p_all = hosted_matmul("proj_fwd", h, comm.weight("w_all_t"), "nt", BF16, tm=TM_PROJ, tn=TN_PROJ)
    a_conv, co = _conv_fwd(p_all, comm.weight("conv_w"))
    (cqn, ckvn, krope), extra = _lora_fwd(p_all, gains["mla_q_norm_g"], gains["mla_kv_norm_g"], g_kr, cs, sg,
                                          comm.side("lora_fwd"))
    comm.done("lora_fwd", extra)
    q_raw, qcat = _q_up(cqn, comm.weight("w_uq"), g_qn, g_qr, cs, sg)
    kv, kcat, vt = _kv_up(ckvn, comm.weight("w_ukv"), krope, g_kn)
    mla_y, a_mla, lse = _flash_fwd(qcat, kcat, vt, p_all)
    memn, kraw, kn, vmem = _mem_kv_fwd(mem, gains["mem_norm_g"], comm.weight("w_mem_kv"), gains["mem_kn_g"])
    a_mem = _mem_attn_fwd(p_all, kn, vmem, gains["mem_qn_g"])
    o_c, o_m, o_e, merged = _merge_fwd(a_conv, comm.weight("w_conv_out"), a_mla, comm.weight("w_mla_out"), a_mem,
                                       comm.weight("w_mem_out"), p_all)
    dy, dyb, loss_parts = _out_fwd(merged, comm.weight("w_o"), x, target)

    comm.put("w_o", _matmul(merged, dyb, "tn", BF16, "dw_o"))
    do_c, do_m, do_e, dp_g = _merge_bwd(dyb, comm.weight("w_o"), p_all, o_c, o_m, o_e)
    comm.put("w_conv_out", _matmul(a_conv, do_c, "tn", BF16, "dw_conv_out"))
    comm.put("w_mla_out", _matmul(a_mla, do_m, "tn", BF16, "dw_mla_out"))
    comm.put("w_mem_out", _matmul(a_mem, do_e, "tn", BF16, "dw_mem_out"))
    da_conv = hosted_matmul("da_conv", do_c, comm.weight("w_conv_out"), "nt", BF16)
    da_mem = _matmul(do_e, comm.weight("w_mem_out"), "nt", BF16, "da_mem")
    dp_c, dconv_w = _conv_bwd(da_conv, p_all, co, comm.weight("conv_w"))
    dmla_y, dp_z, delta = _mla_gate_bwd(do_m, comm.weight("w_mla_out"), mla_y, p_all)
    dqcat, dkcat, dv = _flash_bwd(qcat, kcat, kv, dmla_y, lse, delta)
    (dq_raw, dg_q), extra = _q_prep_bwd(dqcat, q_raw, g_qn, g_qr, cs, sg, comm.side("q_prep_bwd"))
    comm.done("q_prep_bwd", extra)
    (dkv, dkr, dg_kn), extra = _k_prep_bwd(dkcat, dv, kv, g_kn, comm.side("k_prep_bwd"))
    comm.done("k_prep_bwd", extra)
    comm.put("w_uq", _matmul(cqn, dq_raw, "tn", BF16, "dw_uq"))
    comm.put("w_ukv", _matmul(ckvn, dkv, "tn", BF16, "dw_ukv"))
    dcqn = _matmul(dq_raw, comm.weight("w_uq"), "nt", F32, "dcqn")
    dckvn = _matmul(dkv, comm.weight("w_ukv"), "nt", F32, "dckvn")
    dp_s, dg_qn, dg_kvn, dg_kr = _lora_bwd(dcqn, dckvn, dkr, p_all, gains["mla_q_norm_g"],
                                            gains["mla_kv_norm_g"], g_kr, cs, sg)
    dp_m, dkn, dvm, dg_mq = _mem_attn_bwd(da_mem, p_all, kn, vmem, gains["mem_qn_g"])
    dw_mem_kv, dg_mn, dg_mk = _mem_kv_bwd(dkn, dvm, kraw, memn, mem, gains["mem_norm_g"],
                                          comm.weight("w_mem_kv"), gains["mem_kn_g"])
    comm.put("w_mem_kv", dw_mem_kv.astype(BF16))
    dps = [dp_g, dp_z, dp_c, dp_m, dp_s]
    comm.put("w_all_t", [hosted_matmul("dw_in_%d" % k, d, h, "tn", BF16) for k, d in enumerate(dps)])
    s = x.shape[0]
    dh_a, extra = _dh_bwd(dps, comm.weight("w_all_t"), 0, s // 4, "dh_a", comm.side("dh_a"))
    comm.done("dh_a", extra)
    dh_b, extra = _dh_bwd(dps, comm.weight("w_all_t"), s // 4, s - s // 4, "dh_b", comm.side("dh_b"))
    comm.done("dh_b", extra)
    (grad_x, dg_n), extra = _dx_bwd(dh_a, dh_b, x, dy, gains["norm_g"], comm.side("dx_bwd"))
    comm.done("dx_bwd", extra)

    gsmall = {
        "norm_g": dg_n.sum(0), "conv_w": dconv_w.sum(1), "mla_q_norm_g": dg_qn.sum(0),
        "mla_kv_norm_g": dg_kvn.sum(0), "mla_qn_nope_g": dg_q[0].sum(0), "mla_qn_rope_g": dg_q[1].sum(0)[:64],
        "mla_kn_nope_g": dg_kn.sum(0), "mla_kn_rope_g": dg_kr.sum(0)[:64], "mem_norm_g": dg_mn.sum(0),
        "mem_qn_g": dg_mq.sum(0), "mem_kn_g": dg_mk.sum(0),
    }
    return loss_parts, grad_x, gsmall


def _me():
    return lax.axis_index("x"), lax.axis_index("y"), lax.axis_index("c")


def _chips(x, y):
    return [(1 - x, y), (x, 1 - y), (1 - x, 1 - y)]


def _allgather_shards(xs, name):
    r, c = xs.shape
    hc = c // 2

    def body(x_ref, out_ref, send_sems, recv_sems):
        x, y, cc = _me()
        me, sibling = (x, y, cc), (x, y, 1 - cc)
        across_x, across_y, diagonal = _chips(x, y)

        def rows(px, py, pc):
            return out_ref.at[2 * px + py, :, pl.ds(pc * hc, hc)]

        def copy(k, block, to, src=None):
            return pltpu.make_async_remote_copy(
                src_ref=rows(*block) if src is None else src, dst_ref=rows(*block),
                send_sem=send_sems.at[k], recv_sem=recv_sems.at[k], device_id=to, device_id_type=MESH)

        my_half = x_ref.at[:, pl.ds(cc * hc, hc)]
        direct = [copy(0, me, (*across_x, cc), src=my_half), copy(1, me, (*across_y, cc), src=my_half)]
        for cp in direct:
            cp.start()
        copy(0, (*across_x, cc), me).wait_recv()
        copy(1, (*across_y, cc), me).wait_recv()
        from_x = cc == 0
        came = (jnp.where(from_x, across_x[0], across_y[0]), jnp.where(from_x, across_x[1], across_y[1]), cc)
        goes = (jnp.where(from_x, across_y[0], across_x[0]), jnp.where(from_x, across_y[1], across_x[1]), cc)
        relay = copy(2, came, goes)
        relay.start()
        passed = [copy(3, (*across_x, cc), sibling), copy(4, (*across_y, cc), sibling)]
        for cp in passed:
            cp.start()
        copy(2, (*diagonal, cc), me).wait_recv()
        passed.append(copy(5, (*diagonal, cc), sibling))
        passed[2].start()
        for j, chip in enumerate((across_x, across_y, diagonal)):
            copy(3 + j, (*chip, 1 - cc), me).wait_recv()
        for cp in direct + [relay] + passed:
            cp.wait_send()

    return pl.pallas_call(
        body, name=name, in_specs=[ANY], out_specs=ANY,
        out_shape=jax.ShapeDtypeStruct((4, r, c), xs.dtype),
        scratch_shapes=[pltpu.SemaphoreType.DMA((6,)), pltpu.SemaphoreType.DMA((6,))],
    )(xs)


def _with_own(g4, own):
    xi, yi, _ = _me()
    pick = lax.broadcasted_iota(jnp.int32, (4,) + (1,) * own.ndim, 0) == 2 * xi + yi
    return jnp.where(pick, own[None], g4)


def _remote(src, dst, send_sems, recv_sems, k, to):
    return pltpu.make_async_remote_copy(src_ref=src, dst_ref=dst, send_sem=send_sems.at[k], recv_sem=recv_sems.at[k],
                                        device_id=to, device_id_type=MESH)


def _side_gather_ici(shards):
    def build(ins, outs, send_sems, recv_sems):
        x, y, cc = _me()
        pairs = []
        for a, (x_ref, o_ref) in enumerate(zip(ins, outs)):
            hr = x_ref.shape[0] // 2
            my_half = x_ref.at[pl.ds(cc * hr, hr), :]
            for j, (px, py) in enumerate(_chips(x, y)):
                mine = o_ref.at[pl.ds((4 * x + 2 * y + cc) * hr, hr), :]
                theirs = o_ref.at[pl.ds((4 * px + 2 * py + cc) * hr, hr), :]
                pairs.append((_remote(my_half, mine, send_sems, recv_sems, 3 * a + j, (px, py, cc)),
                              _remote(my_half, theirs, send_sems, recv_sems, 3 * a + j, (px, py, cc))))
        return pairs

    shapes = [jax.ShapeDtypeStruct((4 * s.shape[0], s.shape[1]), s.dtype) for s in shards]
    return _Side(shards, shapes, 3 * len(shards), build)


def _side_gather_pass(bufs):
    def build(ins, outs, send_sems, recv_sems):
        x, y, cc = _me()
        pairs = []
        for a, o_ref in enumerate(outs):
            hr = o_ref.shape[0] // 8
            for j, (px, py) in enumerate(_chips(x, y)):
                got = o_ref.at[pl.ds((4 * px + 2 * py + cc) * hr, hr), :]
                coming = o_ref.at[pl.ds((4 * px + 2 * py + 1 - cc) * hr, hr), :]
                pairs.append((_remote(got, got, send_sems, recv_sems, 3 * a + j, (x, y, 1 - cc)),
                              _remote(got, coming, send_sems, recv_sems, 3 * a + j, (x, y, 1 - cc))))
        return pairs

    shapes = [jax.ShapeDtypeStruct(b.shape, b.dtype) for b in bufs]
    return _Side(bufs, shapes, 3 * len(bufs), build, aliases={a: a for a in range(len(bufs))})


def _side_swap(srcs, parts):
    def view(s_ref, part, cc):
        if part == "lead":
            return s_ref.at[1 - cc]
        if part == "cols":
            hc = s_ref.shape[1] // 2
            return s_ref.at[:, pl.ds((1 - cc) * hc, hc)]
        return s_ref

    def build(ins, outs, send_sems, recv_sems):
        x, y, cc = _me()
        pairs = []
        for a, (s_ref, o_ref) in enumerate(zip(ins, outs)):
            cp = _remote(view(s_ref, parts[a], cc), o_ref, send_sems, recv_sems, a, (x, y, 1 - cc))
            pairs.append((cp, cp))
        return pairs

    out_shape = {"lead": lambda s: s.shape[1:], "cols": lambda s: (s.shape[0], s.shape[1] // 2), "all": lambda s: s.shape}
    shapes = [jax.ShapeDtypeStruct(out_shape[p](s), s.dtype) for s, p in zip(srcs, parts)]
    return _Side(srcs, shapes, len(srcs), build)


def _side_chips(ts):
    def build(ins, outs, send_sems, recv_sems):
        x, y, cc = _me()
        pairs = []
        for a, (t_ref, r_ref) in enumerate(zip(ins, outs)):
            for j, (px, py) in enumerate(_chips(x, y)):
                cp = _remote(t_ref.at[2 * px + py], r_ref.at[j], send_sems, recv_sems, 3 * a + j, (px, py, cc))
                pairs.append((cp, cp))
        return pairs

    shapes = [jax.ShapeDtypeStruct((3,) + t.shape[1:], t.dtype) for t in ts]
    return _Side(ts, shapes, 3 * len(ts), build)


ELEMENTWISE_BLOCK_BYTES = 2 << 20


def _tile(r, c, itemsize):
    tr, tc = r, c
    while tr * tc * itemsize > ELEMENTWISE_BLOCK_BYTES and tr % 32 == 0:
        tr //= 2
    while tr * tc * itemsize > ELEMENTWISE_BLOCK_BYTES and tc % 256 == 0:
        tc //= 2
    return tr, tc


def _add_own_half(gb, got, sel, name):
    _, _, hr, c = gb.shape
    tr, tc = _tile(hr, c, 4)

    def body(sel_ref, a_ref, b_ref, o_ref):
        o_ref[...] = (a_ref[0].astype(F32) + b_ref[...].astype(F32)).astype(BF16)

    return pl.pallas_call(
        body, name=name,
        grid_spec=pltpu.PrefetchScalarGridSpec(
            num_scalar_prefetch=1, grid=(4, hr // tr, c // tc),
            in_specs=[pl.BlockSpec((1, 1, tr, tc), lambda k, i, j, sel_ref: (sel_ref[0], k, i, j)),
                      pl.BlockSpec((1, tr, tc), lambda k, i, j, sel_ref: (k, i, j))],
            out_specs=pl.BlockSpec((1, tr, tc), lambda k, i, j, sel_ref: (k, i, j))),
        out_shape=jax.ShapeDtypeStruct(got.shape, BF16),
        compiler_params=_params(("parallel", "parallel", "parallel")),
    )(sel, gb, got)


def _add_own_cols(g, got, sel, name):
    r, c = g.shape
    hr, hc = r // 4, c // 2
    tr, tc = _tile(hr, hc, 4)
    nj = hc // tc

    def body(sel_ref, a_ref, b_ref, o_ref):
        o_ref[...] = (a_ref[...].astype(F32) + b_ref[...].astype(F32)).astype(BF16)

    t = pl.pallas_call(
        body, name=name,
        grid_spec=pltpu.PrefetchScalarGridSpec(
            num_scalar_prefetch=1, grid=(r // tr, nj),
            in_specs=[pl.BlockSpec((tr, tc), lambda i, j, sel_ref: (i, sel_ref[0] * nj + j)),
                      pl.BlockSpec((tr, tc), lambda i, j, sel_ref: (i, j))],
            out_specs=pl.BlockSpec((tr, tc), lambda i, j, sel_ref: (i, j))),
        out_shape=jax.ShapeDtypeStruct((r, hc), BF16),
        compiler_params=_params(("parallel", "parallel")),
    )(sel, g, got)
    return t.reshape(4, hr, hc)


def _sum_partials(t, rcv, sel, name):
    _, hr, c = t.shape
    tr, tc = _tile(hr, c, 4)

    def body(sel_ref, t_ref, r_ref, o_ref):
        o_ref[...] = ((t_ref[0].astype(F32) + r_ref[0].astype(F32)) + r_ref[1].astype(F32)) + r_ref[2].astype(F32)

    return pl.pallas_call(
        body, name=name,
        grid_spec=pltpu.PrefetchScalarGridSpec(
            num_scalar_prefetch=1, grid=(hr // tr, c // tc),
            in_specs=[pl.BlockSpec((1, tr, tc), lambda i, j, sel_ref: (sel_ref[0], i, j)),
                      pl.BlockSpec((3, tr, tc), lambda i, j, sel_ref: (0, i, j))],
            out_specs=pl.BlockSpec((tr, tc), lambda i, j, sel_ref: (i, j))),
        out_shape=jax.ShapeDtypeStruct((hr, c), F32),
        compiler_params=_params(("parallel", "parallel")),
    )(sel, t, rcv)


class _GroupReduce:
    def __init__(self):
        self.names, self.gb, self.t, self.fh, self.other = [], [], [], [], []

    def put(self, name, gb):
        self.names.append(name)
        self.gb.append(gb)

    def side(self, stage):
        if stage == 1:
            return _side_swap(self.gb, ["lead" if g.ndim == 4 else "cols" for g in self.gb])
        return _side_chips(self.t) if stage == 2 else _side_swap(self.fh, ["all"] * len(self.fh))

    def done(self, stage, outs):
        x, y, cc = _me()
        if stage == 1:
            sel = jnp.reshape(cc, (1,)).astype(jnp.int32)
            self.t = [(_add_own_half if gb.ndim == 4 else _add_own_cols)(gb, got, sel, "rs_add_" + n)
                      for n, gb, got in zip(self.names, self.gb, outs)]
        elif stage == 2:
            sel = jnp.reshape(2 * x + y, (1,)).astype(jnp.int32)
            self.fh = [_sum_partials(t, rcv, sel, "rs_sum_" + n) for n, t, rcv in zip(self.names, self.t, outs)]
        else:
            self.other = list(outs)

    def result(self):
        return {n: (fh, other) for n, fh, other in zip(self.names, self.fh, self.other)}


def _allreduce_small(v, tag):
    r = v.shape[0]

    def body(v_ref, out_ref, buf_ref, send_sems, recv_sems):
        x, y, cc = _me()
        me = 4 * x + 2 * y + cc
        buf_ref[pl.ds(pl.multiple_of(me * r, 8), r), :] = v_ref[...]
        peers = [(x, y, 1 - cc)] + [(px, py, pc) for (px, py) in _chips(x, y) for pc in (cc, 1 - cc)]
        cps = []
        for k, (px, py, pc) in enumerate(peers):
            mine = buf_ref.at[pl.ds(pl.multiple_of(me * r, 8), r), :]
            cps.append(pltpu.make_async_remote_copy(
                src_ref=v_ref, dst_ref=mine, send_sem=send_sems.at[k], recv_sem=recv_sems.at[k],
                device_id=(px, py, pc), device_id_type=MESH))
        for cp in cps:
            cp.start()
        for cp in cps:
            cp.wait()
        acc = buf_ref[0:r, :]
        for d in range(1, 8):
            acc = acc + buf_ref[d * r:(d + 1) * r, :]
        out_ref[...] = acc

    vm = pl.BlockSpec(memory_space=pltpu.VMEM)
    return pl.pallas_call(
        body, name="allreduce_small_" + tag, in_specs=[vm], out_specs=vm,
        out_shape=jax.ShapeDtypeStruct((r, 128), F32),
        scratch_shapes=[pltpu.VMEM((8 * r, 128), F32), pltpu.SemaphoreType.DMA((7,)), pltpu.SemaphoreType.DMA((7,))],
    )(v)


def _adamw(w, g, m, v, name):
    r, c = w.shape
    tr = r
    c1 = 1.0 / (1.0 - ADAM_B1 ** ADAM_STEP)
    c2 = 1.0 / (1.0 - ADAM_B2 ** ADAM_STEP)

    def body(w_ref, g_ref, m_ref, v_ref, d_ref, mo_ref, vo_ref):
        gg = g_ref[...]
        mn = ADAM_B1 * m_ref[...] + (1.0 - ADAM_B1) * gg
        vn = ADAM_B2 * v_ref[...] + (1.0 - ADAM_B2) * (gg * gg)
        mo_ref[...] = mn
        vo_ref[...] = vn
        d_ref[...] = -ADAM_LR * ((mn * c1) / (jnp.sqrt(vn * c2) + ADAM_EPS) + ADAM_WD * w_ref[...])

    blk = pl.BlockSpec((tr, c), lambda i: (i, 0))
    return pl.pallas_call(
        body, name=name, grid=(r // tr,), in_specs=[blk] * 4, out_specs=[blk] * 3,
        out_shape=[jax.ShapeDtypeStruct((r, c), F32)] * 3,
        compiler_params=_params(("parallel",)),
    )(w, g, m, v)


def _adamw_halves(w, g_own, g_other, m, v, sel, name, axis):
    r, c = w.shape
    tr, tc = _tile(g_own.shape[0], g_own.shape[1], 4)
    nh = (g_own.shape[axis]) // (tr, tc)[axis]
    c1 = 1.0 / (1.0 - ADAM_B1 ** ADAM_STEP)
    c2 = 1.0 / (1.0 - ADAM_B2 ** ADAM_STEP)

    def body(sel_ref, w_ref, go_ref, gx_ref, m_ref, v_ref, d_ref, mo_ref, vo_ref, g_ref):
        mine = (pl.program_id(axis) // nh) == sel_ref[0]
        gg = jnp.where(mine, go_ref[...], gx_ref[...])
        g_ref[...] = gg
        mn = ADAM_B1 * m_ref[...] + (1.0 - ADAM_B1) * gg
        vn = ADAM_B2 * v_ref[...] + (1.0 - ADAM_B2) * (gg * gg)
        mo_ref[...] = mn
        vo_ref[...] = vn
        d_ref[...] = -ADAM_LR * ((mn * c1) / (jnp.sqrt(vn * c2) + ADAM_EPS) + ADAM_WD * w_ref[...])

    blk = pl.BlockSpec((tr, tc), lambda i, j, sel_ref: (i, j))
    if axis == 0:
        half = pl.BlockSpec((tr, tc), lambda i, j, sel_ref: (i % nh, j))
    else:
        half = pl.BlockSpec((tr, tc), lambda i, j, sel_ref: (i, j % nh))
    return pl.pallas_call(
        body, name=name,
        grid_spec=pltpu.PrefetchScalarGridSpec(
            num_scalar_prefetch=1, grid=(r // tr, c // tc),
            in_specs=[blk, half, half, blk, blk], out_specs=[blk] * 4),
        out_shape=[jax.ShapeDtypeStruct((r, c), F32)] * 4,
        compiler_params=_params(("parallel", "parallel")),
    )(sel, w, g_own, g_other, m, v)


WEIGHTS = ['norm_g', 'w_in', 'conv_w', 'w_conv_out', 'mla_q_norm_g', 'w_uq', 'mla_kv_norm_g', 'w_ukv',
           'mla_qn_nope_g', 'mla_qn_rope_g', 'mla_kn_nope_g', 'mla_kn_rope_g', 'w_mla_out', 'mem_norm_g',
           'w_mem_kv', 'mem_qn_g', 'mem_kn_g', 'w_mem_out', 'w_o']
COL_SHARDED = ['w_in', 'w_conv_out', 'w_uq', 'w_ukv', 'w_mem_out']
ROW_SHARDED = ['w_mla_out', 'w_mem_kv', 'w_o']
SMALL = ['norm_g', 'conv_w', 'mla_q_norm_g', 'mla_kv_norm_g', 'mla_qn_nope_g', 'mla_qn_rope_g', 'mla_kn_nope_g',
         'mla_kn_rope_g', 'mem_norm_g', 'mem_qn_g', 'mem_kn_g']
SMALL_SIZES = [2048, 3072, 512, 512, 128, 64, 128, 64, 2048, 256, 256]
PACK_ROWS = 72


def _full_from_shards(name, g4):
    if name in COL_SHARDED:
        return g4.transpose(1, 0, 2).reshape(g4.shape[1], 4 * g4.shape[2])
    return g4.reshape(4 * g4.shape[1], g4.shape[2])


def _w_all_t_from_w_in_t(w_in_t):
    conv, small, mz, memq, gates = (w_in_t[0:4096], w_in_t[4096:5184], w_in_t[5184:7232],
                                    w_in_t[7232:9280], w_in_t[9280:15424])
    return jnp.concatenate([gates, mz, conv, memq, small, jnp.zeros((2048 - 1088, D_MODEL), BF16)], axis=0)


def _pad_w_uq(w_uq):
    return jnp.pad(w_uq.reshape(512, MLA_HEADS, 192), ((0, 0), (0, 0), (0, 64))).reshape(512, 4096)


def _grad_halves_by_owner(name, g):
    if name == "w_all_t":
        dw_g, dw_z, dw_c, dw_m, dw_s = g
        return "w_in", jnp.concatenate([dw_c, dw_s[:1088], dw_z, dw_m, dw_g], axis=0)
    if name == "w_uq":
        g = g.reshape(512, MLA_HEADS, HEAD_PAD)[:, :, :192].reshape(512, 3072)
    r, c = g.shape
    if name in COL_SHARDED:
        return name, g.reshape(2, r // 2, 4, c // 4).transpose(0, 2, 1, 3)
    return name, g.reshape(4, 2, r // 8, c).transpose(1, 0, 2, 3)


LATE_WEIGHTS = ['w_conv_out', 'w_uq', 'w_ukv', 'w_mem_out', 'w_mla_out', 'w_mem_kv', 'w_o']
HOSTS = {"proj_fwd": ("gather", 1), "lora_fwd": ("gather", 2),
         "da_conv": ("g1", 1), "q_prep_bwd": ("g1", 2), "k_prep_bwd": ("g1", 3),
         "dw_in_0": ("g2", 1), "dw_in_2": ("g2", 2), "dw_in_4": ("g2", 3),
         "dh_a": ("g3", 1), "dh_b": ("g3", 2), "dx_bwd": ("g3", 3)}
GROUP_OF = {"w_o": "g1", "w_conv_out": "g1", "w_mla_out": "g1", "w_mem_out": "g1",
            "w_uq": "g2", "w_ukv": "g2", "w_mem_kv": "g2", "w_in": "g3"}


class _Comm:
    def __init__(self, shards, w_in_t_full, conv_full):
        self.shards = shards
        self.w = {"w_all_t": _w_all_t_from_w_in_t(w_in_t_full), "conv_w": conv_full}
        self.bufs = None
        self.groups = {"g1": _GroupReduce(), "g2": _GroupReduce(), "g3": _GroupReduce()}

    def weight(self, name):
        return self.w[name]

    def side(self, tag):
        if tag not in HOSTS:
            return None
        kind, stage = HOSTS[tag]
        if kind == "gather":
            return _side_gather_ici([self.shards[n] for n in LATE_WEIGHTS]) if stage == 1 else \
                _side_gather_pass(self.bufs)
        return self.groups[kind].side(stage)

    def done(self, tag, outs):
        kind, stage = HOSTS[tag]
        if kind != "gather":
            self.groups[kind].done(stage, outs)
        elif stage == 1:
            self.bufs = list(outs)
        else:
            for n, buf in zip(LATE_WEIGHTS, outs):
                own = self.shards[n]
                self.w[n] = _full_from_shards(n, _with_own(buf.reshape((4,) + own.shape), own))
            self.w["w_uq"] = _pad_w_uq(self.w["w_uq"])

    def put(self, name, g):
        name, gb = _grad_halves_by_owner(name, g)
        self.groups[GROUP_OF[name]].put(name, gb)

    def reduced(self):
        out = {}
        for grp in self.groups.values():
            out.update(grp.result())
        return out


def kernel(x, positions, mem, norm_g, w_in, conv_w, w_conv_out, mla_q_norm_g, w_uq, mla_kv_norm_g, w_ukv, mla_qn_nope_g, mla_qn_rope_g, mla_kn_nope_g, mla_kn_rope_g, w_mla_out, mem_norm_g, w_mem_kv, mem_qn_g, mem_kn_g, w_mem_out, w_o, loss_target, m_norm_g, m_w_in, m_conv_w, m_w_conv_out, m_mla_q_norm_g, m_w_uq, m_mla_kv_norm_g, m_w_ukv, m_mla_qn_nope_g, m_mla_qn_rope_g, m_mla_kn_nope_g, m_mla_kn_rope_g, m_w_mla_out, m_mem_norm_g, m_w_mem_kv, m_mem_qn_g, m_mem_kn_g, m_w_mem_out, m_w_o, v_norm_g, v_w_in, v_conv_w, v_w_conv_out, v_mla_q_norm_g, v_w_uq, v_mla_kv_norm_g, v_w_ukv, v_mla_qn_nope_g, v_mla_qn_rope_g, v_mla_kn_nope_g, v_mla_kn_rope_g, v_w_mla_out, v_mem_norm_g, v_w_mem_kv, v_mem_qn_g, v_mem_kn_g, v_w_mem_out, v_w_o):
    args = locals()
    w = {n: args[n][0] for n in WEIGHTS}
    m1 = {n: args["m_" + n][0] for n in WEIGHTS}
    v2 = {n: args["v_" + n][0] for n in WEIGHTS}
    xi, yi, ci = _me()
    chip = 2 * xi + yi

    conv_slot = jnp.zeros((3, 4, 256), F32).at[:, chip, :].set(w["conv_w"] * jnp.where(ci == 0, 1.0, 0.0))
    pre = jnp.zeros((PACK_ROWS * 128,), F32).at[0:3072].set(conv_slot.reshape(3072))
    conv_full = _allreduce_small(pre.reshape(PACK_ROWS, 128), "conv_w").reshape(-1)[0:3072].reshape(3, 1024)

    w_in_t = w["w_in"].T.astype(BF16)
    w_in_t_full = _with_own(_allgather_shards(w_in_t, "ag_w_in"), w_in_t).reshape(4 * w_in_t.shape[0], D_MODEL)
    comm = _Comm({n: w[n].astype(BF16) for n in LATE_WEIGHTS}, w_in_t_full, conv_full)
    gains = {n: w[n].reshape(1, -1) for n in SMALL if n != "conv_w"}

    loss_parts, grad_x, gsmall = _local_step(x[0], positions[0], mem[0], loss_target[0], comm, gains)

    loss_local = 0.5 * jnp.sum(loss_parts) * (1.0 / D_MODEL)
    flat = jnp.concatenate([gsmall[n].reshape(-1) for n in SMALL] + [loss_local.reshape(1)])
    flat = jnp.pad(flat, (0, PACK_ROWS * 128 - flat.shape[0]))
    tot = _allreduce_small(flat.reshape(PACK_ROWS, 128), "grads").reshape(-1)
    grads = {}
    off = 0
    for n, sz in zip(SMALL, SMALL_SIZES):
        grads[n] = tot[off:off + sz]
        off += sz
    loss = tot[off]
    grads["conv_w"] = lax.dynamic_slice(grads["conv_w"].reshape(3, 1024), (0, chip * 256), (3, 256))
    for n in SMALL:
        grads[n] = grads[n].reshape(w[n].shape)

    deltas, new_m, new_v = {}, {}, {}
    for n in SMALL:
        shp = w[n].shape
        two_d = (lambda a: a.reshape(1, -1)) if len(shp) == 1 else (lambda a: a)
        d, mn, vn = _adamw(two_d(w[n]), two_d(grads[n]), two_d(m1[n]), two_d(v2[n]), "adamw_" + n)
        deltas[n], new_m[n], new_v[n] = d.reshape(shp), mn.reshape(shp), vn.reshape(shp)

    sel_c = jnp.reshape(ci, (1,)).astype(jnp.int32)
    reduced = comm.reduced()
    for n in COL_SHARDED + ROW_SHARDED:
        g_own, g_other = reduced[n]
        if n == "w_in":
            outs = _adamw_halves(w[n].T, g_own, g_other, m1[n].T, v2[n].T, sel_c, "adamw_" + n, 1)
            deltas[n], new_m[n], new_v[n], grads[n] = [o.T for o in outs]
        else:
            deltas[n], new_m[n], new_v[n], grads[n] = _adamw_halves(w[n], g_own, g_other, m1[n], v2[n], sel_c,
                                                                     "adamw_" + n, 0)

    lead = lambda a: a[None]
    return (loss, grad_x[None], *[lead(grads[n]) for n in WEIGHTS], *[lead(deltas[n]) for n in WEIGHTS],
            *[lead(new_m[n]) for n in WEIGHTS], *[lead(new_v[n]) for n in WEIGHTS])
```

```python
import functools

import numpy as np
import jax
import jax.numpy as jnp
from jax import lax
from jax.experimental import pallas as pl
from jax.experimental.pallas import tpu as pltpu

F32 = jnp.float32
BF16 = jnp.bfloat16
MESH = pl.DeviceIdType.MESH

D_MODEL = 2048
EPS = 1e-6
CHUNK = 64
MLA_HEADS = 16
QK_NOPE = 128
QK_ROPE = 64
HEAD_PAD = 256
MEM_TOKENS = 256
MEM_HEADS = 4
MEM_HEAD_DIM = 256
ROPE_THETA = 10000.0
MLA_SCALE = (QK_NOPE + QK_ROPE) ** -0.5
MEM_SCALE = MEM_HEAD_DIM ** -0.5
LN2 = 0.6931471805599453
Q_SCALE = MLA_SCALE / LN2
NEG = -1e30

ADAM_LR = 0.001
ADAM_B1 = 0.9
ADAM_B2 = 0.999
ADAM_EPS = 1e-08
ADAM_WD = 0.01
ADAM_STEP = 10

NP = 16384
COL_G, COL_Z, COL_C, COL_M, COL_S = 0, 6144, 8192, 12288, 14336

TM_PROJ = 1024
TN_PROJ = 2048
TS_ROW = 512
TS_BWD = 256
TQ_ATT = 1024
TK_ATT = 1024
TR_ATT = 256
FWD_HEADS_PER_STEP = 8
VMEM_LIMIT = 56 * 1024 * 1024

NT_DIMS = (((1,), (1,)), ((), ()))
TN_DIMS = (((0,), (0,)), ((), ()))


def _dot(a, b, mode="nn"):
    if mode == "nn":
        return jnp.dot(a, b, preferred_element_type=F32)
    return lax.dot_general(a, b, NT_DIMS if mode == "nt" else TN_DIMS, preferred_element_type=F32)


def _sigmoid(z):
    return 1.0 / (1.0 + jnp.exp(-z))


def _params(sem=None, vmem=VMEM_LIMIT):
    return pltpu.CompilerParams(dimension_semantics=sem, vmem_limit_bytes=vmem)


def _fold8(v):
    r, c = v.shape
    return v.reshape(r // 8, 8, c).sum(axis=0)


def _swap32(t):
    lane = lax.broadcasted_iota(jnp.int32, t.shape, 1)
    return jnp.where(lane < 32, pltpu.roll(t, 96, 1), pltpu.roll(t, 32, 1))


def _rope(t, cs, sg):
    return t * cs + _swap32(t) * sg


def _rope_bwd(d, cs, sg):
    return d * cs + _swap32(d * sg)


def _rms_fwd(xf, g, n):
    r = lax.rsqrt(jnp.sum(xf * xf, axis=-1, keepdims=True) * (1.0 / n) + EPS)
    return xf * r * g


def _rms_bwd(xf, g, dy, n):
    r = lax.rsqrt(jnp.sum(xf * xf, axis=-1, keepdims=True) * (1.0 / n) + EPS)
    xhat = xf * r
    dyg = dy * g
    dx = r * (dyg - xhat * (jnp.sum(dyg * xhat, axis=-1, keepdims=True) * (1.0 / n)))
    return dx, dy * xhat


ANY = pl.BlockSpec(memory_space=pl.ANY)


class _Side:
    def __init__(self, inputs, out_shapes, n_copies, build, aliases=()):
        self.inputs, self.out_shapes, self.n_copies = list(inputs), list(out_shapes), n_copies
        self.build, self.aliases = build, dict(aliases)


def _pcall(body, side, name, grid, in_specs, out_specs, out_shape, args, sem, scratch_shapes=(), prefetch=()):
    n_pre, n_in, n_out = len(prefetch), len(in_specs), len(out_specs)
    if side is None:
        outs = pl.pallas_call(
            body, name=name,
            grid_spec=pltpu.PrefetchScalarGridSpec(num_scalar_prefetch=n_pre, grid=grid, in_specs=list(in_specs),
                                                   out_specs=list(out_specs), scratch_shapes=list(scratch_shapes)),
            out_shape=list(out_shape), compiler_params=_params(sem),
        )(*prefetch, *args)
        return list(outs), []
    ns_in, ns_out = len(side.inputs), len(side.out_shapes)

    def wrapped(*refs):
        pre, r = refs[:n_pre], refs[n_pre:]
        ins, s_ins = r[:n_in], r[n_in:n_in + ns_in]
        outs = r[n_in + ns_in:n_in + ns_in + n_out]
        s_outs = r[n_in + ns_in + n_out:n_in + ns_in + n_out + ns_out]
        scr, send_sems, recv_sems = r[n_in + ns_in + n_out + ns_out:-2], r[-2], r[-1]
        pairs = side.build(s_ins, s_outs, send_sems, recv_sems)
        first = pl.program_id(0) == 0
        last = pl.program_id(0) == grid[0] - 1
        for d in range(1, len(grid)):
            first = first & (pl.program_id(d) == 0)
            last = last & (pl.program_id(d) == grid[d] - 1)

        @pl.when(first)
        def _():
            for send, _ in pairs:
                send.start()

        body(*pre, *ins, *outs, *scr)

        @pl.when(last)
        def _():
            for send, recv in pairs:
                send.wait_send()
                recv.wait_recv()

    outs = pl.pallas_call(
        wrapped, name=name,
        grid_spec=pltpu.PrefetchScalarGridSpec(
            num_scalar_prefetch=n_pre, grid=grid, in_specs=list(in_specs) + [ANY] * ns_in,
            out_specs=list(out_specs) + [ANY] * ns_out,
            scratch_shapes=list(scratch_shapes) + [pltpu.SemaphoreType.DMA((side.n_copies,)),
                                                   pltpu.SemaphoreType.DMA((side.n_copies,))]),
        out_shape=list(out_shape) + side.out_shapes,
        input_output_aliases={n_pre + n_in + i: n_out + o for i, o in side.aliases.items()},
        compiler_params=_params(("arbitrary",) * len(grid)),
    )(*prefetch, *args, *side.inputs)
    return list(outs[:n_out]), list(outs[n_out:])


def _matmul(a, b, mode, out_dtype, name, tm=1024, tn=1024, tk=None, side=None):
    if tk is None:
        tk = 4096 if mode == "tn" else 2048
    if mode == "nn":
        (m, k), (_, n) = a.shape, b.shape
    elif mode == "nt":
        (m, k), (n, _) = a.shape, b.shape
    else:
        (k, m), (_, n) = a.shape, b.shape
    tm, tn, tk = min(tm, m), min(tn, n), min(tk, k)
    nk = k // tk
    if mode == "tn":
        a_spec = pl.BlockSpec((tk, tm), lambda i, j, kk: (kk, i))
    else:
        a_spec = pl.BlockSpec((tm, tk), lambda i, j, kk: (i, kk))
    if mode == "nt":
        b_spec = pl.BlockSpec((tn, tk), lambda i, j, kk: (j, kk))
    else:
        b_spec = pl.BlockSpec((tk, tn), lambda i, j, kk: (kk, j))

    def body(a_ref, b_ref, o_ref, acc_ref):
        kk = pl.program_id(2)

        @pl.when(kk == 0)
        def _():
            acc_ref[...] = jnp.zeros_like(acc_ref)

        acc_ref[...] += _dot(a_ref[...].astype(BF16), b_ref[...].astype(BF16), mode)

        @pl.when(kk == nk - 1)
        def _():
            o_ref[...] = acc_ref[...].astype(out_dtype)

    def single_body(a_ref, b_ref, o_ref):
        o_ref[...] = _dot(a_ref[...].astype(BF16), b_ref[...].astype(BF16), mode).astype(out_dtype)

    outs, extra = _pcall(
        single_body if nk == 1 else body, side, name, (m // tm, n // tn, nk), [a_spec, b_spec],
        [pl.BlockSpec((tm, tn), lambda i, j, kk: (i, j))], [jax.ShapeDtypeStruct((m, n), out_dtype)], (a, b),
        ("parallel", "parallel", "arbitrary"), scratch_shapes=[] if nk == 1 else [pltpu.VMEM((tm, tn), F32)])
    return outs[0] if side is None else (outs[0], extra)


def _norm_fwd(x, g):
    s = x.shape[0]
    ts = min(TS_ROW, s)

    def body(x_ref, g_ref, h_ref):
        h_ref[...] = _rms_fwd(x_ref[...], g_ref[...], D_MODEL).astype(BF16)

    row = pl.BlockSpec((ts, D_MODEL), lambda i: (i, 0))
    return pl.pallas_call(
        body, name="norm_fwd", grid=(s // ts,),
        in_specs=[row, pl.BlockSpec((1, D_MODEL), lambda i: (0, 0))], out_specs=row,
        out_shape=jax.ShapeDtypeStruct((s, D_MODEL), BF16), compiler_params=_params(("parallel",)),
    )(x, g)


def _shift_rows(v, carry, j, ts, back):
    if back:
        main = pltpu.roll(v, j, 0)
        edge = pltpu.roll(jnp.concatenate([carry, v[:8]], axis=0), j, 0)[8:]
        return jnp.concatenate([edge, main[8:]], axis=0)
    main = pltpu.roll(v, ts - j, 0)
    edge = pltpu.roll(jnp.concatenate([v[ts - 8:], carry], axis=0), 16 - j, 0)[:8]
    return jnp.concatenate([main[:ts - 8], edge], axis=0)


def _conv_fwd(p_all, conv_w):
    s = p_all.shape[0]
    ts = min(TS_ROW, s)
    c0 = COL_C // 1024

    def body(cg_ref, bg_ref, u_ref, z_ref, w_ref, a_ref, co_ref, carry_ref):
        @pl.when(pl.program_id(0) == 0)
        def _():
            carry_ref[...] = jnp.zeros_like(carry_ref)

        p = cg_ref[...].astype(F32) * u_ref[...].astype(F32)
        carry = carry_ref[...]
        co = (w_ref[2:3, :] * p + w_ref[1:2, :] * _shift_rows(p, carry, 1, ts, True)
              + w_ref[0:1, :] * _shift_rows(p, carry, 2, ts, True))
        carry_ref[...] = p[ts - 8:]
        z = z_ref[...].astype(F32)
        a_ref[...] = (bg_ref[...].astype(F32) * co * (z * _sigmoid(z))).astype(BF16)
        co_ref[...] = co.astype(BF16)

    seg = lambda c: pl.BlockSpec((ts, 1024), lambda i: (i, c0 + c))
    return pl.pallas_call(
        body, name="conv_fwd", grid=(s // ts,),
        in_specs=[seg(0), seg(1), seg(2), seg(3), pl.BlockSpec((3, 1024), lambda i: (0, 0))],
        out_specs=[pl.BlockSpec((ts, 1024), lambda i: (i, 0))] * 2,
        out_shape=[jax.ShapeDtypeStruct((s, 1024), BF16)] * 2,
        scratch_shapes=[pltpu.VMEM((8, 1024), F32)],
        compiler_params=_params(("arbitrary",)),
    )(p_all, p_all, p_all, p_all, conv_w)


def _lora_fwd(p_all, gq, gkv, gkr, cs, sg, side=None):
    s = p_all.shape[0]
    ts = min(TS_ROW, s)

    def body(cq_ref, ckv_ref, kr_ref, gq_ref, gkv_ref, gkr_ref, cs_ref, sg_ref, cqn_ref, ckvn_ref, krope_ref):
        cqn_ref[...] = _rms_fwd(cq_ref[...].astype(F32), gq_ref[...], 512).astype(BF16)
        ckvn_ref[...] = _rms_fwd(ckv_ref[...].astype(F32), gkv_ref[...], 512).astype(BF16)
        kn = _rms_fwd(kr_ref[...].astype(F32), gkr_ref[...], QK_ROPE)
        krope_ref[...] = _rope(kn, cs_ref[...], sg_ref[...]).astype(BF16)

    row = lambda w: pl.BlockSpec((ts, w), lambda i: (i, 0))
    vec = lambda w: pl.BlockSpec((1, w), lambda i: (0, 0))
    return _pcall(
        body, side, "lora_fwd", (s // ts,),
        [pl.BlockSpec((ts, 512), lambda i: (i, COL_S // 512)),
         pl.BlockSpec((ts, 512), lambda i: (i, COL_S // 512 + 1)),
         pl.BlockSpec((ts, 128), lambda i: (i, (COL_S + 1024) // 128)),
         vec(512), vec(512), vec(128), row(128), row(128)],
        [row(512), row(512), row(128)],
        [jax.ShapeDtypeStruct((s, 512), BF16), jax.ShapeDtypeStruct((s, 512), BF16),
         jax.ShapeDtypeStruct((s, 128), BF16)],
        (p_all, p_all, p_all, gq, gkv, gkr, cs, sg), ("parallel",))


UP_HEADS = 4


def _q_up(cqn, w_uq, gn, gr, cs, sg):
    s = cqn.shape[0]
    tm, tn = min(TS_ROW, s), UP_HEADS * HEAD_PAD

    def body(c_ref, w_ref, gn_ref, gr_ref, cs_ref, sg_ref, q_ref, o_ref):
        qb = _dot(c_ref[...], w_ref[...]).astype(BF16)
        q_ref[...] = qb
        for h in range(UP_HEADS):
            q = qb[:, h * HEAD_PAD:(h + 1) * HEAD_PAD].astype(F32)
            a = _rms_fwd(q[:, :128], gn_ref[...], QK_NOPE)
            b = _rope(_rms_fwd(q[:, 128:], gr_ref[...], QK_ROPE), cs_ref[...], sg_ref[...])
            o_ref[h] = (jnp.concatenate([a, b], axis=1) * Q_SCALE).astype(BF16)

    vec = pl.BlockSpec((1, 128), lambda i, j: (0, 0))
    tab = pl.BlockSpec((tm, 128), lambda i, j: (i, 0))
    return pl.pallas_call(
        body, name="q_up", grid=(s // tm, MLA_HEADS // UP_HEADS),
        in_specs=[pl.BlockSpec((tm, 512), lambda i, j: (i, 0)), pl.BlockSpec((512, tn), lambda i, j: (0, j)),
                  vec, vec, tab, tab],
        out_specs=[pl.BlockSpec((tm, tn), lambda i, j: (i, j)),
                   pl.BlockSpec((UP_HEADS, tm, HEAD_PAD), lambda i, j: (j, i, 0))],
        out_shape=[jax.ShapeDtypeStruct((s, MLA_HEADS * HEAD_PAD), BF16),
                   jax.ShapeDtypeStruct((MLA_HEADS, s, HEAD_PAD), BF16)],
        compiler_params=_params(("parallel", "parallel")),
    )(cqn, w_uq, gn, gr, cs, sg)


def _kv_up(ckvn, w_ukv, krope, gk):
    s = ckvn.shape[0]
    tm, tn = min(TS_ROW, s), UP_HEADS * HEAD_PAD

    def body(c_ref, w_ref, kr_ref, gk_ref, kv_ref, o_ref, vt_ref):
        kvb = _dot(c_ref[...], w_ref[...]).astype(BF16)
        kv_ref[...] = kvb
        for h in range(UP_HEADS):
            a = _rms_fwd(kvb[:, h * HEAD_PAD:h * HEAD_PAD + 128].astype(F32), gk_ref[...], QK_NOPE)
            o_ref[h] = jnp.concatenate([a.astype(BF16), kr_ref[...]], axis=1)
            vt_ref[h] = kvb[:, h * HEAD_PAD + 128:(h + 1) * HEAD_PAD].astype(F32).T.astype(BF16)

    return pl.pallas_call(
        body, name="kv_up", grid=(s // tm, MLA_HEADS // UP_HEADS),
        in_specs=[pl.BlockSpec((tm, 512), lambda i, j: (i, 0)), pl.BlockSpec((512, tn), lambda i, j: (0, j)),
                  pl.BlockSpec((tm, 128), lambda i, j: (i, 0)), pl.BlockSpec((1, 128), lambda i, j: (0, 0))],
        out_specs=[pl.BlockSpec((tm, tn), lambda i, j: (i, j)),
                   pl.BlockSpec((UP_HEADS, tm, HEAD_PAD), lambda i, j: (j, i, 0)),
                   pl.BlockSpec((UP_HEADS, 128, tm), lambda i, j: (j, 0, i))],
        out_shape=[jax.ShapeDtypeStruct((s, MLA_HEADS * HEAD_PAD), BF16),
                   jax.ShapeDtypeStruct((MLA_HEADS, s, HEAD_PAD), BF16),
                   jax.ShapeDtypeStruct((MLA_HEADS, 128, s), BF16)],
        compiler_params=_params(("parallel", "parallel")),
    )(ckvn, w_ukv, krope, gk)


def _chunk_mask(tq, tk, row0=0):
    r = (lax.broadcasted_iota(jnp.int32, (tq, tk), 0) + row0) // CHUNK
    c = lax.broadcasted_iota(jnp.int32, (tq, tk), 1) // CHUNK
    return c <= r


def _causal_pairs(n, by_key):
    if by_key:
        pairs = [(i, j) for j in range(n) for i in range(j, n)]
    else:
        pairs = [(i, j) for i in range(n) for j in range(i + 1)]
    return (jnp.asarray([p[0] for p in pairs], jnp.int32), jnp.asarray([p[1] for p in pairs], jnp.int32))


def _flash_fwd(qcat, kcat, vt, p_all):
    s = qcat.shape[1]
    tq = tk = min(TQ_ATT, s)
    tc = min(TR_ATT, tq)
    nq = s // tq
    hg = FWD_HEADS_PER_STEP
    zc = COL_Z // (128 * hg)
    qi, kj = _causal_pairs(nq, False)

    def body(qi_ref, kj_ref, q_ref, k_ref, vt_ref, z_ref, y_ref, a_ref, lse_ref, m_sc, l_sc, acc_sc):
        t = pl.program_id(1)
        i, j = qi_ref[t], kj_ref[t]

        @pl.when(j == 0)
        def _():
            m_sc[...] = jnp.full_like(m_sc, NEG)
            l_sc[...] = jnp.zeros_like(l_sc)
            acc_sc[...] = jnp.zeros_like(acc_sc)

        def step(masked):
            nc = tq // tc
            state = {(g, c): [m_sc[g, :, c * tc:(c + 1) * tc], l_sc[g, :, c * tc:(c + 1) * tc],
                              acc_sc[g, :, c * tc:(c + 1) * tc]] for g in range(hg) for c in range(nc)}
            units = [(g, u, c) for u in range(tk // tc) for c in range(nc) for g in range(hg)
                     if (u <= c or not masked)]

            def scores(g, u, c):
                return _dot(k_ref[g, u * tc:(u + 1) * tc, :], q_ref[g, c * tc:(c + 1) * tc, :], "nt")

            def weighted_values(g, u, c, alpha, pb):
                state[g, c][2] = alpha * state[g, c][2] + _dot(vt_ref[g, :, u * tc:(u + 1) * tc], pb)

            ahead = 4
            pending = [scores(*un) for un in units[:ahead]]
            late = None
            for n, (g, u, c) in enumerate(units):
                st = pending.pop(0)
                if n + ahead < len(units):
                    pending.append(scores(*units[n + ahead]))
                if masked and u == c:
                    kc = lax.broadcasted_iota(jnp.int32, (tc, tc), 0) // CHUNK
                    qc = lax.broadcasted_iota(jnp.int32, (tc, tc), 1) // CHUNK
                    st = jnp.where(kc <= qc, st, NEG)
                m_run, l_run, _ = state[g, c]
                m_new = jnp.maximum(m_run, jnp.max(st, axis=0, keepdims=True))
                alpha = jnp.exp2(m_run - m_new)
                p = jnp.exp2(st - m_new)
                state[g, c][0] = m_new
                state[g, c][1] = alpha * l_run + jnp.sum(p, axis=0, keepdims=True)
                if late is not None:
                    weighted_values(*late)
                late = (g, u, c, alpha, p.astype(BF16))
            weighted_values(*late)
            for g in range(hg):
                m_sc[g] = jnp.concatenate([state[g, c][0] for c in range(nc)], axis=1)
                l_sc[g] = jnp.concatenate([state[g, c][1] for c in range(nc)], axis=1)
                acc_sc[g] = jnp.concatenate([state[g, c][2] for c in range(nc)], axis=1)

        @pl.when(j < i)
        def _():
            step(False)

        @pl.when(j == i)
        def _():
            step(True)
            for g in range(hg):
                cols = slice(g * 128, (g + 1) * 128)
                y = (acc_sc[g] * (1.0 / l_sc[g])).T
                z = z_ref[:, cols].astype(F32)
                y_ref[:, cols] = y.astype(BF16)
                a_ref[:, cols] = (y * (z * _sigmoid(z))).astype(BF16)
                lse2 = m_sc[g] + jnp.log(l_sc[g]) * (1.0 / LN2)
                lse_ref[g] = jnp.broadcast_to(lse2, (128, tq)).T

    wide = pl.BlockSpec((tq, 128 * hg), lambda h, t, qi_ref, kj_ref: (qi_ref[t], h))
    return pl.pallas_call(
        body, name="flash_fwd",
        grid_spec=pltpu.PrefetchScalarGridSpec(
            num_scalar_prefetch=2, grid=(MLA_HEADS // hg, qi.shape[0]),
            in_specs=[pl.BlockSpec((hg, tq, HEAD_PAD), lambda h, t, qi_ref, kj_ref: (h, qi_ref[t], 0)),
                      pl.BlockSpec((hg, tk, HEAD_PAD), lambda h, t, qi_ref, kj_ref: (h, kj_ref[t], 0)),
                      pl.BlockSpec((hg, 128, tk), lambda h, t, qi_ref, kj_ref: (h, 0, kj_ref[t])),
                      pl.BlockSpec((tq, 128 * hg), lambda h, t, qi_ref, kj_ref: (qi_ref[t], zc + h))],
            out_specs=[wide, wide,
                       pl.BlockSpec((hg, tq, 128), lambda h, t, qi_ref, kj_ref: (h, qi_ref[t], 0))],
            scratch_shapes=[pltpu.VMEM((hg, 1, tq), F32), pltpu.VMEM((hg, 1, tq), F32),
                            pltpu.VMEM((hg, 128, tq), F32)]),
        out_shape=[jax.ShapeDtypeStruct((s, MLA_HEADS * 128), BF16),
                   jax.ShapeDtypeStruct((s, MLA_HEADS * 128), BF16),
                   jax.ShapeDtypeStruct((MLA_HEADS, s, 128), F32)],
        compiler_params=_params(("parallel", "arbitrary")),
    )(qi, kj, qcat, kcat, vt, p_all)


def _mem_kv_fwd(mem, g_norm, w_kv, g_kn):
    m = mem.shape[0]

    def body(mem_ref, g_ref, w_ref, gk_ref, memn_ref, kraw_ref, kn_ref, v_ref):
        memn = _rms_fwd(mem_ref[...], g_ref[...], D_MODEL).astype(BF16)
        memn_ref[...] = memn
        kvm = _dot(memn, w_ref[...])
        kraw_ref[...] = kvm[:, :1024]
        v_ref[...] = kvm[:, 1024:].astype(BF16)
        for hh in range(MEM_HEADS):
            sl = slice(hh * MEM_HEAD_DIM, (hh + 1) * MEM_HEAD_DIM)
            kn_ref[:, sl] = _rms_fwd(kvm[:, sl], gk_ref[...], MEM_HEAD_DIM).astype(BF16)

    return pl.pallas_call(
        body, name="mem_kv_fwd",
        out_shape=[jax.ShapeDtypeStruct((m, D_MODEL), BF16), jax.ShapeDtypeStruct((m, 1024), F32),
                   jax.ShapeDtypeStruct((m, 1024), BF16), jax.ShapeDtypeStruct((m, 1024), BF16)],
        compiler_params=_params(),
    )(mem, g_norm, w_kv, g_kn)


def _mem_attn_fwd(p_all, kn, v, gq):
    s = p_all.shape[0]
    tq = min(TS_ROW, s)

    def body(q_ref, z_ref, kn_ref, v_ref, gq_ref, a_ref):
        z = z_ref[...].astype(F32)
        gate = z * _sigmoid(z)
        for hh in range(MEM_HEADS):
            sl = slice(hh * MEM_HEAD_DIM, (hh + 1) * MEM_HEAD_DIM)
            qn = _rms_fwd(q_ref[:, sl].astype(F32), gq_ref[...], MEM_HEAD_DIM).astype(BF16)
            sc = _dot(qn, kn_ref[:, sl], "nt") * MEM_SCALE
            e = jnp.exp(sc - jnp.max(sc, axis=-1, keepdims=True))
            p = e * (1.0 / jnp.sum(e, axis=-1, keepdims=True))
            y = _dot(p.astype(BF16), v_ref[:, sl])
            a_ref[:, sl] = (y * gate[:, sl]).astype(BF16)

    full = lambda a: pl.BlockSpec(a.shape, lambda i: (0, 0))
    return pl.pallas_call(
        body, name="mem_attn_fwd", grid=(s // tq,),
        in_specs=[pl.BlockSpec((tq, 1024), lambda i: (i, COL_M // 1024)),
                  pl.BlockSpec((tq, 1024), lambda i: (i, COL_M // 1024 + 1)),
                  full(kn), full(v), full(gq)],
        out_specs=pl.BlockSpec((tq, 1024), lambda i: (i, 0)),
        out_shape=jax.ShapeDtypeStruct((s, 1024), BF16),
        compiler_params=_params(("parallel",)),
    )(p_all, p_all, kn, v, gq)


def _merge_fwd(a_conv, w_c, a_mla, w_m, a_mem, w_e, p_all):
    s = a_conv.shape[0]
    tm, tn = min(TS_ROW, s), 1024
    nj = D_MODEL // tn

    def body(ac_ref, wc_ref, am_ref, wm_ref, ae_ref, we_ref, gc_ref, gm_ref, ge_ref,
             oc_ref, om_ref, oe_ref, mg_ref):
        oc = _dot(ac_ref[...], wc_ref[...])
        om = _dot(am_ref[...], wm_ref[...])
        oe = _dot(ae_ref[...], we_ref[...])
        oc_ref[...] = oc.astype(BF16)
        om_ref[...] = om.astype(BF16)
        oe_ref[...] = oe.astype(BF16)
        mg_ref[...] = (_sigmoid(gc_ref[...].astype(F32)) * oc + _sigmoid(gm_ref[...].astype(F32)) * om
                       + _sigmoid(ge_ref[...].astype(F32)) * oe).astype(BF16)

    act = lambda k: pl.BlockSpec((tm, k), lambda i, j: (i, 0))
    wgt = lambda k: pl.BlockSpec((k, tn), lambda i, j: (0, j))
    gate = lambda b: pl.BlockSpec((tm, tn), lambda i, j: (i, b * nj + j))
    out = pl.BlockSpec((tm, tn), lambda i, j: (i, j))
    return pl.pallas_call(
        body, name="merge_fwd", grid=(s // tm, nj),
        in_specs=[act(1024), wgt(1024), act(2048), wgt(2048), act(1024), wgt(1024), gate(0), gate(1), gate(2)],
        out_specs=[out] * 4,
        out_shape=[jax.ShapeDtypeStruct((s, D_MODEL), BF16)] * 4,
        compiler_params=_params(("parallel", "parallel")),
    )(a_conv, w_c, a_mla, w_m, a_mem, w_e, p_all, p_all, p_all)


def _out_fwd(merged, w_o, x, target):
    s = merged.shape[0]
    tm, tn = min(TS_ROW, s), 1024
    nj = D_MODEL // tn

    def body(mg_ref, w_ref, x_ref, t_ref, dy_ref, dyb_ref, ls_ref):
        e = x_ref[...] + _dot(mg_ref[...], w_ref[...]) - t_ref[...]
        dy = e * (1.0 / D_MODEL)
        dy_ref[...] = dy
        dyb_ref[...] = dy.astype(BF16)
        r = _fold8(e * e)
        acc = r[:, 0:128]
        for cc in range(1, tn // 128):
            acc = acc + r[:, cc * 128:(cc + 1) * 128]
        ls_ref[...] = acc

    tile = pl.BlockSpec((tm, tn), lambda i, j: (i, j))
    return pl.pallas_call(
        body, name="out_fwd", grid=(s // tm, nj),
        in_specs=[pl.BlockSpec((tm, D_MODEL), lambda i, j: (i, 0)),
                  pl.BlockSpec((D_MODEL, tn), lambda i, j: (0, j)), tile, tile],
        out_specs=[tile, tile, pl.BlockSpec((8, 128), lambda i, j: (i, j))],
        out_shape=[jax.ShapeDtypeStruct((s, D_MODEL), F32), jax.ShapeDtypeStruct((s, D_MODEL), BF16),
                   jax.ShapeDtypeStruct((s // tm * 8, nj * 128), F32)],
        compiler_params=_params(("parallel", "parallel")),
    )(merged, w_o, x, target)


def _merge_bwd(dyb, w_o, p_all, o_c, o_m, o_e):
    s = dyb.shape[0]
    tm = min(256, s)

    def body(dy_ref, w_ref, g_ref, oc_ref, om_ref, oe_ref, dc_ref, dm_ref, de_ref, dg_ref):
        dmg = _dot(dy_ref[...], w_ref[...], "nt")
        for b, (o_ref, d_ref) in enumerate(((oc_ref, dc_ref), (om_ref, dm_ref), (oe_ref, de_ref))):
            sl = slice(b * D_MODEL, (b + 1) * D_MODEL)
            sg = _sigmoid(g_ref[:, sl].astype(F32))
            d_ref[...] = (dmg * sg).astype(BF16)
            dg_ref[:, sl] = (dmg * o_ref[...].astype(F32) * sg * (1.0 - sg)).astype(BF16)

    row = pl.BlockSpec((tm, D_MODEL), lambda i: (i, 0))
    wide = pl.BlockSpec((tm, 3 * D_MODEL), lambda i: (i, 0))
    return pl.pallas_call(
        body, name="merge_bwd", grid=(s // tm,),
        in_specs=[row, pl.BlockSpec((D_MODEL, D_MODEL), lambda i: (0, 0)), wide, row, row, row],
        out_specs=[row, row, row, wide],
        out_shape=[jax.ShapeDtypeStruct((s, D_MODEL), BF16)] * 3 + [jax.ShapeDtypeStruct((s, 3 * D_MODEL), BF16)],
        compiler_params=_params(("parallel",)),
    )(dyb, w_o, p_all, o_c, o_m, o_e)


def _conv_bwd(da, p_all, co, conv_w):
    s = da.shape[0]
    ts = min(TS_ROW, s)
    n = s // ts
    c0 = COL_C // 1024

    def body(da_ref, cg_ref, bg_ref, u_ref, z_ref, co_ref, w_ref, dp_ref, dw_ref, carry_ref):
        @pl.when(pl.program_id(0) == 0)
        def _():
            carry_ref[...] = jnp.zeros_like(carry_ref)
            dw_ref[...] = jnp.zeros_like(dw_ref)

        da_ = da_ref[...].astype(F32)
        cg, bg = cg_ref[...].astype(F32), bg_ref[...].astype(F32)
        u, z, cov = u_ref[...].astype(F32), z_ref[...].astype(F32), co_ref[...].astype(F32)
        sz = _sigmoid(z)
        dyc = da_ * (z * sz)
        dz = da_ * (bg * cov) * (sz * (1.0 + z * (1.0 - sz)))
        db = dyc * cov
        dco = dyc * bg
        carry = carry_ref[...]
        d1 = _shift_rows(dco, carry, 1, ts, False)
        d2 = _shift_rows(dco, carry, 2, ts, False)
        carry_ref[...] = dco[:8]
        dpp = w_ref[2:3, :] * dco + w_ref[1:2, :] * d1 + w_ref[0:1, :] * d2
        p = cg * u
        dw_ref[0] += _fold8(p * d2)
        dw_ref[1] += _fold8(p * d1)
        dw_ref[2] += _fold8(p * dco)
        dp_ref[...] = jnp.concatenate([dpp * u, db, dpp * cg, dz], axis=1).astype(BF16)

    rev = lambda c: pl.BlockSpec((ts, 1024), lambda i: (n - 1 - i, c))
    return pl.pallas_call(
        body, name="conv_bwd", grid=(n,),
        in_specs=[rev(0), rev(c0), rev(c0 + 1), rev(c0 + 2), rev(c0 + 3), rev(0),
                  pl.BlockSpec((3, 1024), lambda i: (0, 0))],
        out_specs=[pl.BlockSpec((ts, 4096), lambda i: (n - 1 - i, 0)),
                   pl.BlockSpec((3, 8, 1024), lambda i: (0, 0, 0))],
        out_shape=[jax.ShapeDtypeStruct((s, 4096), BF16), jax.ShapeDtypeStruct((3, 8, 1024), F32)],
        scratch_shapes=[pltpu.VMEM((8, 1024), F32)],
        compiler_params=_params(("arbitrary",)),
    )(da, p_all, p_all, p_all, p_all, co, conv_w)


def _mla_gate_bwd(do_m, w_mla_out, y, p_all):
    s = do_m.shape[0]
    tm, tn = min(TS_ROW, s), 1024
    heads = tn // 128

    def body(do_ref, w_ref, y_ref, z_ref, dy_ref, dz_ref, dl_ref):
        da_ = _dot(do_ref[...], w_ref[...], "nt").astype(BF16).astype(F32)
        yv, z = y_ref[...].astype(F32), z_ref[...].astype(F32)
        sz = _sigmoid(z)
        dyv = da_ * (z * sz)
        dy_ref[...] = dyv.astype(BF16)
        dz_ref[...] = (da_ * yv * (sz * (1.0 + z * (1.0 - sz)))).astype(BF16)
        pr = dyv * yv
        for h in range(heads):
            dl_ref[h] = jnp.broadcast_to(jnp.sum(pr[:, h * 128:(h + 1) * 128], axis=-1, keepdims=True), (tm, 128))

    tile = pl.BlockSpec((tm, tn), lambda i, j: (i, j))
    return pl.pallas_call(
        body, name="mla_gate_bwd", grid=(s // tm, D_MODEL // tn),
        in_specs=[pl.BlockSpec((tm, D_MODEL), lambda i, j: (i, 0)), pl.BlockSpec((tn, D_MODEL), lambda i, j: (j, 0)),
                  tile, pl.BlockSpec((tm, tn), lambda i, j: (i, COL_Z // tn + j))],
        out_specs=[tile, tile, pl.BlockSpec((heads, tm, 128), lambda i, j: (j, i, 0))],
        out_shape=[jax.ShapeDtypeStruct((s, D_MODEL), BF16)] * 2 + [jax.ShapeDtypeStruct((MLA_HEADS, s, 128), F32)],
        compiler_params=_params(("parallel", "parallel")),
    )(do_m, w_mla_out, y, p_all)


def _flash_bwd(qcat, kcat, kv, dy, lse, delta):
    s = qcat.shape[1]
    tq = tk = min(TQ_ATT, s)
    tr = min(TR_ATT, tq)
    nq = s // tq
    qi, kj = _causal_pairs(nq, True)
    npairs = qi.shape[0]

    def body(qi_ref, kj_ref, q_ref, k_ref, v_ref, do_ref, lse_ref, dl_ref, dq_ref, dk_ref, dv_ref,
             dq_acc, dk_acc, dv_acc):
        t = pl.program_id(1)
        i, j = qi_ref[t], kj_ref[t]

        def add_dq(rows, dq_new):
            @pl.when(j == 0)
            def _():
                dq_acc[rows, :] = dq_new

            @pl.when(j > 0)
            def _():
                dq_acc[rows, :] += dq_new

        def full_step():
            q, k = q_ref[0], k_ref[0]
            p = jnp.exp2(_dot(q, k, "nt") - lse_ref[0][:, 0:1])
            do = do_ref[...]
            dpv = _dot(do, v_ref[...], "nt")
            ds = (p * (dpv - dl_ref[0][:, 0:1])).astype(BF16)
            dv_acc[...] += _dot(p.astype(BF16), do, "tn")
            dk_acc[...] += _dot(ds, q, "tn")
            add_dq(pl.ds(pl.multiple_of(i * tq, tq), tq), _dot(ds, k))

        def diagonal_step():
            def keys(r):
                return (r + 1) * tr

            def scores(r):
                rows = slice(r * tr, (r + 1) * tr)
                return (_dot(q_ref[0, rows, :], k_ref[0, :keys(r), :], "nt"),
                        _dot(do_ref[rows, :], v_ref[:keys(r), :], "nt"))

            def gradients(n, r, pb, ds):
                nk = keys(r)
                rows = slice(r * tr, (r + 1) * tr)
                dv_new = _dot(pb, do_ref[rows, :], "tn")
                dk_new = _dot(ds, q_ref[0, rows, :], "tn")
                if n == 0:
                    dv_acc[...] = dv_new
                    dk_acc[...] = dk_new
                else:
                    dv_acc[:nk, :] += dv_new
                    dk_acc[:nk, :] += dk_new
                add_dq(pl.ds(pl.multiple_of(i * tq + r * tr, tr), tr), _dot(ds, k_ref[0, :nk, :]))

            order = list(range(tq // tr - 1, -1, -1))
            pending = [scores(r) for r in order[:2]]
            late = None
            for n, r in enumerate(order):
                sc, dpv = pending.pop(0)
                if n + 2 < len(order):
                    pending.append(scores(order[n + 2]))
                rows = slice(r * tr, (r + 1) * tr)
                sc = jnp.where(_chunk_mask(tr, keys(r), r * tr), sc, NEG)
                p = jnp.exp2(sc - lse_ref[0][rows, 0:1])
                ds = (p * (dpv - dl_ref[0][rows, 0:1])).astype(BF16)
                if late is not None:
                    gradients(*late)
                late = (n, r, p.astype(BF16), ds)
            gradients(*late)

        @pl.when(i == j)
        def _():
            diagonal_step()

        @pl.when(i > j)
        def _():
            full_step()

        @pl.when(i == nq - 1)
        def _():
            dk_ref[0] = dk_acc[...].astype(BF16)
            dv_ref[0] = dv_acc[...].astype(BF16)

        @pl.when(t == npairs - 1)
        def _():
            dq_ref[0] = dq_acc[...].astype(BF16)

    qrow = lambda w: pl.BlockSpec((1, tq, w), lambda h, t, qi_ref, kj_ref: (h, qi_ref[t], 0))
    krow = lambda w: pl.BlockSpec((1, tk, w), lambda h, t, qi_ref, kj_ref: (h, kj_ref[t], 0))
    return pl.pallas_call(
        body, name="flash_bwd",
        grid_spec=pltpu.PrefetchScalarGridSpec(
            num_scalar_prefetch=2, grid=(MLA_HEADS, npairs),
            in_specs=[qrow(HEAD_PAD), krow(HEAD_PAD),
                      pl.BlockSpec((tk, 128), lambda h, t, qi_ref, kj_ref: (kj_ref[t], 2 * h + 1)),
                      pl.BlockSpec((tq, 128), lambda h, t, qi_ref, kj_ref: (qi_ref[t], h)),
                      qrow(128), qrow(128)],
            out_specs=[pl.BlockSpec((1, s, HEAD_PAD), lambda h, t, qi_ref, kj_ref: (h, 0, 0)),
                       krow(HEAD_PAD), krow(128)],
            scratch_shapes=[pltpu.VMEM((s, HEAD_PAD), F32), pltpu.VMEM((tk, HEAD_PAD), F32),
                            pltpu.VMEM((tk, 128), F32)]),
        out_shape=[jax.ShapeDtypeStruct((MLA_HEADS, s, HEAD_PAD), BF16),
                   jax.ShapeDtypeStruct((MLA_HEADS, s, HEAD_PAD), BF16),
                   jax.ShapeDtypeStruct((MLA_HEADS, s, 128), BF16)],
        compiler_params=_params(("parallel", "arbitrary")),
    )(qi, kj, qcat, kcat, kv, dy, lse, delta)


def _q_prep_bwd(dqcat, q_raw, gn, gr, cs, sg, side=None):
    s = q_raw.shape[0]
    ts = min(TS_BWD, s)

    def body(dq_ref, q_ref, gn_ref, gr_ref, cs_ref, sg_ref, o_ref, dg_ref):
        @pl.when(pl.program_id(0) == 0)
        def _():
            dg_ref[...] = jnp.zeros_like(dg_ref)

        ga_acc = jnp.zeros((8, 128), F32)
        gb_acc = jnp.zeros((8, 128), F32)
        for h in range(MLA_HEADS):
            cols = slice(h * HEAD_PAD, (h + 1) * HEAD_PAD)
            q = q_ref[:, cols].astype(F32)
            d = dq_ref[h].astype(F32) * MLA_SCALE
            da, ga = _rms_bwd(q[:, :128], gn_ref[...], d[:, :128], QK_NOPE)
            db, gb = _rms_bwd(q[:, 128:], gr_ref[...], _rope_bwd(d[:, 128:], cs_ref[...], sg_ref[...]), QK_ROPE)
            o_ref[:, cols] = jnp.concatenate([da, db], axis=1).astype(BF16)
            ga_acc = ga_acc + _fold8(ga)
            gb_acc = gb_acc + _fold8(gb)
        dg_ref[0] += ga_acc
        dg_ref[1] += gb_acc

    wide = pl.BlockSpec((ts, MLA_HEADS * HEAD_PAD), lambda i: (i, 0))
    return _pcall(
        body, side, "q_prep_bwd", (s // ts,),
        [pl.BlockSpec((MLA_HEADS, ts, HEAD_PAD), lambda i: (0, i, 0)), wide,
         pl.BlockSpec((1, 128), lambda i: (0, 0)), pl.BlockSpec((1, 128), lambda i: (0, 0)),
         pl.BlockSpec((ts, 128), lambda i: (i, 0)), pl.BlockSpec((ts, 128), lambda i: (i, 0))],
        [wide, pl.BlockSpec((2, 8, 128), lambda i: (0, 0, 0))],
        [jax.ShapeDtypeStruct((s, MLA_HEADS * HEAD_PAD), BF16), jax.ShapeDtypeStruct((2, 8, 128), F32)],
        (dqcat, q_raw, gn, gr, cs, sg), ("arbitrary",))


def _k_prep_bwd(dkcat, dv, kv, gk, side=None):
    s = kv.shape[0]
    ts = min(TS_BWD, s)

    def body(dk_ref, dv_ref, k_ref, gk_ref, o_ref, dkr_ref, dg_ref):
        @pl.when(pl.program_id(0) == 0)
        def _():
            dg_ref[...] = jnp.zeros_like(dg_ref)

        g_acc = jnp.zeros((8, 128), F32)
        dkr = jnp.zeros((ts, 128), F32)
        for h in range(MLA_HEADS):
            d = dk_ref[h].astype(F32) * LN2
            k = k_ref[:, h * HEAD_PAD:h * HEAD_PAD + 128].astype(F32)
            dk_raw, gg = _rms_bwd(k, gk_ref[...], d[:, :128], QK_NOPE)
            o_ref[:, h * HEAD_PAD:(h + 1) * HEAD_PAD] = jnp.concatenate([dk_raw.astype(BF16), dv_ref[h]], axis=1)
            g_acc = g_acc + _fold8(gg)
            dkr = dkr + d[:, 128:]
        dkr_ref[...] = dkr
        dg_ref[...] += g_acc

    wide = pl.BlockSpec((ts, MLA_HEADS * HEAD_PAD), lambda i: (i, 0))
    return _pcall(
        body, side, "k_prep_bwd", (s // ts,),
        [pl.BlockSpec((MLA_HEADS, ts, HEAD_PAD), lambda i: (0, i, 0)),
         pl.BlockSpec((MLA_HEADS, ts, 128), lambda i: (0, i, 0)), wide,
         pl.BlockSpec((1, 128), lambda i: (0, 0))],
        [wide, pl.BlockSpec((ts, 128), lambda i: (i, 0)), pl.BlockSpec((8, 128), lambda i: (0, 0))],
        [jax.ShapeDtypeStruct((s, MLA_HEADS * HEAD_PAD), BF16), jax.ShapeDtypeStruct((s, 128), F32),
         jax.ShapeDtypeStruct((8, 128), F32)],
        (dkcat, dv, kv, gk), ("arbitrary",))


def _lora_bwd(dcqn, dckvn, dkr, p_all, gq, gkv, gkr, cs, sg):
    s = p_all.shape[0]
    ts = min(TS_ROW, s)

    def body(dq_ref, dkv_ref, dkr_ref, cq_ref, ckv_ref, kr_ref, gq_ref, gkv_ref, gkr_ref, cs_ref, sg_ref,
             o_ref, dgq_ref, dgkv_ref, dgkr_ref):
        @pl.when(pl.program_id(0) == 0)
        def _():
            dgq_ref[...] = jnp.zeros_like(dgq_ref)
            dgkv_ref[...] = jnp.zeros_like(dgkv_ref)
            dgkr_ref[...] = jnp.zeros_like(dgkr_ref)

        dq, g1 = _rms_bwd(cq_ref[...].astype(F32), gq_ref[...], dq_ref[...], 512)
        dkv, g2 = _rms_bwd(ckv_ref[...].astype(F32), gkv_ref[...], dkv_ref[...], 512)
        dkn = _rope_bwd(dkr_ref[...], cs_ref[...], sg_ref[...])
        dk, g3 = _rms_bwd(kr_ref[...].astype(F32), gkr_ref[...], dkn, QK_ROPE)
        o_ref[...] = jnp.concatenate([dq, dkv, dk, jnp.zeros((ts, 2048 - 1152), F32)], axis=1).astype(BF16)
        dgq_ref[...] += _fold8(g1)
        dgkv_ref[...] += _fold8(g2)
        dgkr_ref[...] += _fold8(g3)

    row = lambda w: pl.BlockSpec((ts, w), lambda i: (i, 0))
    vec = lambda w: pl.BlockSpec((1, w), lambda i: (0, 0))
    acc = lambda w: pl.BlockSpec((8, w), lambda i: (0, 0))
    return pl.pallas_call(
        body, name="lora_bwd", grid=(s // ts,),
        in_specs=[row(512), row(512), row(128),
                  pl.BlockSpec((ts, 512), lambda i: (i, COL_S // 512)),
                  pl.BlockSpec((ts, 512), lambda i: (i, COL_S // 512 + 1)),
                  pl.BlockSpec((ts, 128), lambda i: (i, (COL_S + 1024) // 128)),
                  vec(512), vec(512), vec(128), row(128), row(128)],
        out_specs=[row(2048), acc(512), acc(512), acc(128)],
        out_shape=[jax.ShapeDtypeStruct((s, 2048), BF16), jax.ShapeDtypeStruct((8, 512), F32),
                   jax.ShapeDtypeStruct((8, 512), F32), jax.ShapeDtypeStruct((8, 128), F32)],
        compiler_params=_params(("arbitrary",)),
    )(dcqn, dckvn, dkr, p_all, p_all, p_all, gq, gkv, gkr, cs, sg)


def _mem_attn_bwd(da, p_all, kn, v, gq):
    s = p_all.shape[0]
    tq = min(TS_ROW, s)

    def body(da_ref, q_ref, z_ref, kn_ref, v_ref, gq_ref, o_ref, dkn_ref, dv_ref, dg_ref):
        @pl.when(pl.program_id(0) == 0)
        def _():
            dkn_ref[...] = jnp.zeros_like(dkn_ref)
            dv_ref[...] = jnp.zeros_like(dv_ref)
            dg_ref[...] = jnp.zeros_like(dg_ref)

        z = z_ref[...].astype(F32)
        da_ = da_ref[...].astype(F32)
        sz = _sigmoid(z)
        gate = z * sz
        dgate = sz * (1.0 + z * (1.0 - sz))
        for hh in range(MEM_HEADS):
            sl = slice(hh * MEM_HEAD_DIM, (hh + 1) * MEM_HEAD_DIM)
            qf = q_ref[:, sl].astype(F32)
            qn = _rms_fwd(qf, gq_ref[...], MEM_HEAD_DIM).astype(BF16)
            knh, vh = kn_ref[:, sl], v_ref[:, sl]
            sc = _dot(qn, knh, "nt") * MEM_SCALE
            e = jnp.exp(sc - jnp.max(sc, axis=-1, keepdims=True))
            p = e * (1.0 / jnp.sum(e, axis=-1, keepdims=True))
            pb = p.astype(BF16)
            y = _dot(pb, vh)
            dyh = da_[:, sl] * gate[:, sl]
            o_ref[:, 1024 + hh * MEM_HEAD_DIM:1024 + (hh + 1) * MEM_HEAD_DIM] = (
                da_[:, sl] * y * dgate[:, sl]).astype(BF16)
            dyb = dyh.astype(BF16)
            dpm = _dot(dyb, vh, "nt")
            ds = (p * (dpm - jnp.sum(dpm * p, axis=-1, keepdims=True)) * MEM_SCALE).astype(BF16)
            dqn = _dot(ds, knh)
            dkn_ref[:, sl] += _dot(ds, qn, "tn")
            dv_ref[:, sl] += _dot(pb, dyb, "tn")
            dq, gg = _rms_bwd(qf, gq_ref[...], dqn, MEM_HEAD_DIM)
            o_ref[:, sl] = dq.astype(BF16)
            dg_ref[...] += _fold8(gg)

    full = lambda a: pl.BlockSpec(a.shape, lambda i: (0, 0))
    return pl.pallas_call(
        body, name="mem_attn_bwd", grid=(s // tq,),
        in_specs=[pl.BlockSpec((tq, 1024), lambda i: (i, 0)),
                  pl.BlockSpec((tq, 1024), lambda i: (i, COL_M // 1024)),
                  pl.BlockSpec((tq, 1024), lambda i: (i, COL_M // 1024 + 1)),
                  full(kn), full(v), full(gq)],
        out_specs=[pl.BlockSpec((tq, 2048), lambda i: (i, 0)),
                   pl.BlockSpec((MEM_TOKENS, 1024), lambda i: (0, 0)),
                   pl.BlockSpec((MEM_TOKENS, 1024), lambda i: (0, 0)),
                   pl.BlockSpec((8, MEM_HEAD_DIM), lambda i: (0, 0))],
        out_shape=[jax.ShapeDtypeStruct((s, 2048), BF16), jax.ShapeDtypeStruct((MEM_TOKENS, 1024), F32),
                   jax.ShapeDtypeStruct((MEM_TOKENS, 1024), F32), jax.ShapeDtypeStruct((8, MEM_HEAD_DIM), F32)],
        compiler_params=_params(("arbitrary",)),
    )(da, p_all, p_all, kn, v, gq)


def _mem_kv_bwd(dkn, dv, kraw, memn, mem, g_norm, w_kv, g_kn):
    m = mem.shape[0]

    def body(dkn_ref, dv_ref, kraw_ref, memn_ref, mem_ref, g_ref, w_ref, gk_ref, dw_ref, dgn_ref, dgk_ref, dkv_sc):
        gk_acc = jnp.zeros((8, MEM_HEAD_DIM), F32)
        for hh in range(MEM_HEADS):
            sl = slice(hh * MEM_HEAD_DIM, (hh + 1) * MEM_HEAD_DIM)
            dk, gg = _rms_bwd(kraw_ref[:, sl], gk_ref[...], dkn_ref[:, sl], MEM_HEAD_DIM)
            dkv_sc[:, sl] = dk.astype(BF16)
            gk_acc = gk_acc + _fold8(gg)
        dgk_ref[...] = gk_acc
        dkv_sc[:, 1024:] = dv_ref[...].astype(BF16)
        dkv = dkv_sc[...]
        dw_ref[...] = _dot(memn_ref[...], dkv, "tn")
        dmemn = _dot(dkv, w_ref[...], "nt")
        xf = mem_ref[...]
        r = lax.rsqrt(jnp.mean(xf * xf, axis=-1, keepdims=True) + EPS)
        dgn_ref[...] = _fold8(dmemn * (xf * r))

    return pl.pallas_call(
        body, name="mem_kv_bwd",
        out_shape=[jax.ShapeDtypeStruct((D_MODEL, D_MODEL), F32), jax.ShapeDtypeStruct((8, D_MODEL), F32),
                   jax.ShapeDtypeStruct((8, MEM_HEAD_DIM), F32)],
        scratch_shapes=[pltpu.VMEM((m, D_MODEL), BF16)],
        compiler_params=_params(),
    )(dkn, dv, kraw, memn, mem, g_norm, w_kv, g_kn)


def _dh_bwd(dps, w_all_t, row0, nrows, name, side=None):
    tm, tk = min(TS_ROW, nrows), 2048
    blk0 = row0 // tm
    widths = [a.shape[1] // tk for a in dps]
    starts = [int(v) for v in np.cumsum([0] + widths[:-1])]
    nk = sum(widths)

    def body(*refs):
        d_refs, w_ref, o_ref, acc_ref = refs[:5], refs[5], refs[6], refs[7]
        kk = pl.program_id(1)

        @pl.when(kk == 0)
        def _():
            acc_ref[...] = jnp.zeros_like(acc_ref)

        for d_ref, st, wd in zip(d_refs, starts, widths):
            @pl.when((kk >= st) & (kk < st + wd))
            def _(d_ref=d_ref):
                acc_ref[...] += _dot(d_ref[...], w_ref[...])

        @pl.when(kk == nk - 1)
        def _():
            o_ref[...] = acc_ref[...]

    def dspec(st, wd):
        return pl.BlockSpec((tm, tk), lambda i, kk: (blk0 + i, jnp.clip(kk - st, 0, wd - 1)))

    outs, extra = _pcall(
        body, side, name, (nrows // tm, nk),
        [dspec(st, wd) for st, wd in zip(starts, widths)] + [pl.BlockSpec((tk, D_MODEL), lambda i, kk: (kk, 0))],
        [pl.BlockSpec((tm, D_MODEL), lambda i, kk: (i, 0))], [jax.ShapeDtypeStruct((nrows, D_MODEL), F32)],
        (*dps, w_all_t), ("parallel", "arbitrary"), scratch_shapes=[pltpu.VMEM((tm, D_MODEL), F32)])
    return outs[0], extra


def _dx_bwd(dh_a, dh_b, x, dy, g):
    s = x.shape[0]
    ts = min(TS_BWD, dh_a.shape[0])
    na, nb = dh_a.shape[0] // ts, dh_b.shape[0] // ts

    def body(dha_ref, dhb_ref, x_ref, dy_ref, g_ref, o_ref, dg_ref):
        @pl.when(pl.program_id(0) == 0)
        def _():
            dg_ref[...] = jnp.zeros_like(dg_ref)

        dh = jnp.where(pl.program_id(0) < na, dha_ref[...], dhb_ref[...])
        dx, gg = _rms_bwd(x_ref[...], g_ref[...], dh, D_MODEL)
        o_ref[...] = dy_ref[...] + dx
        dg_ref[...] += _fold8(gg)

    row = pl.BlockSpec((ts, D_MODEL), lambda i: (i, 0))
    outs, _ = _pcall(
        body, None, "dx_bwd", (s // ts,),
        [pl.BlockSpec((ts, D_MODEL), lambda i: (jnp.minimum(i, na - 1), 0)),
         pl.BlockSpec((ts, D_MODEL), lambda i: (jnp.clip(i - na, 0, nb - 1), 0)),
         row, row, pl.BlockSpec((1, D_MODEL), lambda i: (0, 0))],
        [row, pl.BlockSpec((8, D_MODEL), lambda i: (0, 0))],
        [jax.ShapeDtypeStruct((s, D_MODEL), F32), jax.ShapeDtypeStruct((8, D_MODEL), F32)],
        (dh_a, dh_b, x, dy, g), ("arbitrary",))
    return outs


def _pad128(v, n):
    return jnp.pad(v.reshape(1, n), ((0, 0), (0, 128 - n)))


def _local_step(x, positions, mem, target, comm, gains):
    half = QK_ROPE // 2
    inv_freq = jnp.power(ROPE_THETA, -jnp.arange(half, dtype=F32) / half)
    ang = positions.astype(F32)[:, None] * inv_freq
    cos, sin = jnp.cos(ang), jnp.sin(ang)
    zpad = jnp.zeros((x.shape[0], 64), F32)
    cs = jnp.concatenate([cos, cos, zpad], axis=1)
    sg = jnp.concatenate([-sin, sin, zpad], axis=1)
    g_qr, g_kr = _pad128(gains["mla_qn_rope_g"], 64), _pad128(gains["mla_kn_rope_g"], 64)
    g_qn, g_kn = gains["mla_qn_nope_g"], gains["mla_kn_nope_g"]

    def hosted_matmul(tag, a, b, mode, out_dtype, **tiles):
        side = comm.side(tag)
        if side is None:
            return _matmul(a, b, mode, out_dtype, tag, **tiles)
        out, extra = _matmul(a, b, mode, out_dtype, tag, side=side, **tiles)
        comm.done(tag, extra)
        return out

    h = _norm_fwd(x, gains["norm_g"])
    p_all = hosted_matmul("proj_fwd", h, comm.weight("w_all_t"), "nt", BF16, tm=TM_PROJ, tn=TN_PROJ)
    a_conv, co = _conv_fwd(p_all, comm.weight("conv_w"))
    (cqn, ckvn, krope), extra = _lora_fwd(p_all, gains["mla_q_norm_g"], gains["mla_kv_norm_g"], g_kr, cs, sg,
                                          comm.side("lora_fwd"))
    comm.done("lora_fwd", extra)
    q_raw, qcat = _q_up(cqn, comm.weight("w_uq"), g_qn, g_qr, cs, sg)
    kv, kcat, vt = _kv_up(ckvn, comm.weight("w_ukv"), krope, g_kn)
    mla_y, a_mla, lse = _flash_fwd(qcat, kcat, vt, p_all)
    memn, kraw, kn, vmem = _mem_kv_fwd(mem, gains["mem_norm_g"], comm.weight("w_mem_kv"), gains["mem_kn_g"])
    a_mem = _mem_attn_fwd(p_all, kn, vmem, gains["mem_qn_g"])
    o_c, o_m, o_e, merged = _merge_fwd(a_conv, comm.weight("w_conv_out"), a_mla, comm.weight("w_mla_out"), a_mem,
                                       comm.weight("w_mem_out"), p_all)
    dy, dyb, loss_parts = _out_fwd(merged, comm.weight("w_o"), x, target)

    comm.put("w_o", _matmul(merged, dyb, "tn", BF16, "dw_o"))
    do_c, do_m, do_e, dp_g = _merge_bwd(dyb, comm.weight("w_o"), p_all, o_c, o_m, o_e)
    comm.put("w_conv_out", _matmul(a_conv, do_c, "tn", BF16, "dw_conv_out"))
    comm.put("w_mla_out", _matmul(a_mla, do_m, "tn", BF16, "dw_mla_out"))
    comm.put("w_mem_out", _matmul(a_mem, do_e, "tn", BF16, "dw_mem_out"))
    da_conv = hosted_matmul("da_conv", do_c, comm.weight("w_conv_out"), "nt", BF16)
    da_mem = _matmul(do_e, comm.weight("w_mem_out"), "nt", BF16, "da_mem")
    dp_c, dconv_w = _conv_bwd(da_conv, p_all, co, comm.weight("conv_w"))
    dmla_y, dp_z, delta = _mla_gate_bwd(do_m, comm.weight("w_mla_out"), mla_y, p_all)
    dqcat, dkcat, dv = _flash_bwd(qcat, kcat, kv, dmla_y, lse, delta)
    (dq_raw, dg_q), extra = _q_prep_bwd(dqcat, q_raw, g_qn, g_qr, cs, sg, comm.side("q_prep_bwd"))
    comm.done("q_prep_bwd", extra)
    (dkv, dkr, dg_kn), extra = _k_prep_bwd(dkcat, dv, kv, g_kn, comm.side("k_prep_bwd"))
    comm.done("k_prep_bwd", extra)
    comm.put("w_uq", _matmul(cqn, dq_raw, "tn", BF16, "dw_uq"))
    comm.put("w_ukv", _matmul(ckvn, dkv, "tn", BF16, "dw_ukv"))
    dcqn = _matmul(dq_raw, comm.weight("w_uq"), "nt", F32, "dcqn")
    dckvn = _matmul(dkv, comm.weight("w_ukv"), "nt", F32, "dckvn")
    dp_s, dg_qn, dg_kvn, dg_kr = _lora_bwd(dcqn, dckvn, dkr, p_all, gains["mla_q_norm_g"],
                                            gains["mla_kv_norm_g"], g_kr, cs, sg)
    dp_m, dkn, dvm, dg_mq = _mem_attn_bwd(da_mem, p_all, kn, vmem, gains["mem_qn_g"])
    dw_mem_kv, dg_mn, dg_mk = _mem_kv_bwd(dkn, dvm, kraw, memn, mem, gains["mem_norm_g"],
                                          comm.weight("w_mem_kv"), gains["mem_kn_g"])
    comm.put("w_mem_kv", dw_mem_kv.astype(BF16))
    dps = [dp_g, dp_z, dp_c, dp_m, dp_s]
    comm.put("w_all_t", [hosted_matmul("dw_in_%d" % k, d, h, "tn", BF16) for k, d in enumerate(dps)])
    s = x.shape[0]
    dh_a, extra = _dh_bwd(dps, comm.weight("w_all_t"), 0, s // 4, "dh_a", comm.side("dh_a"))
    comm.done("dh_a", extra)
    dh_b, extra = _dh_bwd(dps, comm.weight("w_all_t"), s // 4, s - s // 4, "dh_b", comm.side("dh_b"))
    comm.done("dh_b", extra)
    grad_x, dg_n = _dx_bwd(dh_a, dh_b, x, dy, gains["norm_g"])

    gsmall = {
        "norm_g": dg_n.sum(0), "conv_w": dconv_w.sum(1), "mla_q_norm_g": dg_qn.sum(0),
        "mla_kv_norm_g": dg_kvn.sum(0), "mla_qn_nope_g": dg_q[0].sum(0), "mla_qn_rope_g": dg_q[1].sum(0)[:64],
        "mla_kn_nope_g": dg_kn.sum(0), "mla_kn_rope_g": dg_kr.sum(0)[:64], "mem_norm_g": dg_mn.sum(0),
        "mem_qn_g": dg_mq.sum(0), "mem_kn_g": dg_mk.sum(0),
    }
    return loss_parts, grad_x, gsmall


def _me():
    return lax.axis_index("x"), lax.axis_index("y"), lax.axis_index("c")


def _chips(x, y):
    return [(1 - x, y), (x, 1 - y), (1 - x, 1 - y)]


def _allgather_shards(xs, name):
    r, c = xs.shape
    hc = c // 2

    def body(x_ref, out_ref, send_sems, recv_sems, own_sem):
        x, y, cc = _me()
        me, sibling = (x, y, cc), (x, y, 1 - cc)
        across_x, across_y, diagonal = _chips(x, y)
        own = pltpu.make_async_copy(x_ref, out_ref.at[2 * x + y], own_sem)
        own.start()

        def rows(px, py, pc):
            return out_ref.at[2 * px + py, :, pl.ds(pc * hc, hc)]

        def copy(k, block, to, src=None):
            return pltpu.make_async_remote_copy(
                src_ref=rows(*block) if src is None else src, dst_ref=rows(*block),
                send_sem=send_sems.at[k], recv_sem=recv_sems.at[k], device_id=to, device_id_type=MESH)

        my_half = x_ref.at[:, pl.ds(cc * hc, hc)]
        direct = [copy(0, me, (*across_x, cc), src=my_half), copy(1, me, (*across_y, cc), src=my_half)]
        for cp in direct:
            cp.start()
        copy(0, (*across_x, cc), me).wait_recv()
        copy(1, (*across_y, cc), me).wait_recv()
        from_x = cc == 0
        came = (jnp.where(from_x, across_x[0], across_y[0]), jnp.where(from_x, across_x[1], across_y[1]), cc)
        goes = (jnp.where(from_x, across_y[0], across_x[0]), jnp.where(from_x, across_y[1], across_x[1]), cc)
        relay = copy(2, came, goes)
        relay.start()
        passed = [copy(3, (*across_x, cc), sibling), copy(4, (*across_y, cc), sibling)]
        for cp in passed:
            cp.start()
        copy(2, (*diagonal, cc), me).wait_recv()
        passed.append(copy(5, (*diagonal, cc), sibling))
        passed[2].start()
        for j, chip in enumerate((across_x, across_y, diagonal)):
            copy(3 + j, (*chip, 1 - cc), me).wait_recv()
        for cp in direct + [relay] + passed:
            cp.wait_send()
        own.wait()

    return pl.pallas_call(
        body, name=name, in_specs=[ANY], out_specs=ANY,
        out_shape=jax.ShapeDtypeStruct((4, r, c), xs.dtype),
        scratch_shapes=[pltpu.SemaphoreType.DMA((6,)), pltpu.SemaphoreType.DMA((6,)), pltpu.SemaphoreType.DMA(())],
    )(xs)


def _with_own(g4, own):
    xi, yi, _ = _me()
    pick = lax.broadcasted_iota(jnp.int32, (4,) + (1,) * own.ndim, 0) == 2 * xi + yi
    return jnp.where(pick, own[None], g4)


def _remote(src, dst, send_sems, recv_sems, k, to):
    return pltpu.make_async_remote_copy(src_ref=src, dst_ref=dst, send_sem=send_sems.at[k], recv_sem=recv_sems.at[k],
                                        device_id=to, device_id_type=MESH)


def _side_gather_ici(shards):
    def build(ins, outs, send_sems, recv_sems):
        x, y, cc = _me()
        pairs = []
        for a, (x_ref, o_ref) in enumerate(zip(ins, outs)):
            hr = x_ref.shape[0] // 2
            my_half = x_ref.at[pl.ds(cc * hr, hr), :]
            for j, (px, py) in enumerate(_chips(x, y)):
                mine = o_ref.at[pl.ds((4 * x + 2 * y + cc) * hr, hr), :]
                theirs = o_ref.at[pl.ds((4 * px + 2 * py + cc) * hr, hr), :]
                pairs.append((_remote(my_half, mine, send_sems, recv_sems, 3 * a + j, (px, py, cc)),
                              _remote(my_half, theirs, send_sems, recv_sems, 3 * a + j, (px, py, cc))))
        return pairs

    shapes = [jax.ShapeDtypeStruct((4 * s.shape[0], s.shape[1]), s.dtype) for s in shards]
    return _Side(shards, shapes, 3 * len(shards), build)


def _side_gather_pass(bufs):
    def build(ins, outs, send_sems, recv_sems):
        x, y, cc = _me()
        pairs = []
        for a, o_ref in enumerate(outs):
            hr = o_ref.shape[0] // 8
            for j, (px, py) in enumerate(_chips(x, y)):
                got = o_ref.at[pl.ds((4 * px + 2 * py + cc) * hr, hr), :]
                coming = o_ref.at[pl.ds((4 * px + 2 * py + 1 - cc) * hr, hr), :]
                pairs.append((_remote(got, got, send_sems, recv_sems, 3 * a + j, (x, y, 1 - cc)),
                              _remote(got, coming, send_sems, recv_sems, 3 * a + j, (x, y, 1 - cc))))
        return pairs

    shapes = [jax.ShapeDtypeStruct(b.shape, b.dtype) for b in bufs]
    return _Side(bufs, shapes, 3 * len(bufs), build, aliases={a: a for a in range(len(bufs))})


def _side_swap(srcs, parts):
    def view(s_ref, part, cc):
        if part == "lead":
            return s_ref.at[1 - cc]
        if part == "cols":
            hc = s_ref.shape[1] // 2
            return s_ref.at[:, pl.ds((1 - cc) * hc, hc)]
        return s_ref

    def build(ins, outs, send_sems, recv_sems):
        x, y, cc = _me()
        pairs = []
        for a, (s_ref, o_ref) in enumerate(zip(ins, outs)):
            cp = _remote(view(s_ref, parts[a], cc), o_ref, send_sems, recv_sems, a, (x, y, 1 - cc))
            pairs.append((cp, cp))
        return pairs

    out_shape = {"lead": lambda s: s.shape[1:], "cols": lambda s: (s.shape[0], s.shape[1] // 2), "all": lambda s: s.shape}
    shapes = [jax.ShapeDtypeStruct(out_shape[p](s), s.dtype) for s, p in zip(srcs, parts)]
    return _Side(srcs, shapes, len(srcs), build)


def _side_chips(ts):
    def build(ins, outs, send_sems, recv_sems):
        x, y, cc = _me()
        pairs = []
        for a, (t_ref, r_ref) in enumerate(zip(ins, outs)):
            for j, (px, py) in enumerate(_chips(x, y)):
                cp = _remote(t_ref.at[2 * px + py], r_ref.at[j], send_sems, recv_sems, 3 * a + j, (px, py, cc))
                pairs.append((cp, cp))
        return pairs

    shapes = [jax.ShapeDtypeStruct((3,) + t.shape[1:], t.dtype) for t in ts]
    return _Side(ts, shapes, 3 * len(ts), build)


def _exchange(side, name):
    ns_in, ns_out = len(side.inputs), len(side.out_shapes)

    def body(*refs):
        pairs = side.build(refs[:ns_in], refs[ns_in:ns_in + ns_out], refs[-2], refs[-1])
        for send, _ in pairs:
            send.start()
        for send, recv in pairs:
            send.wait_send()
            recv.wait_recv()

    outs = pl.pallas_call(
        body, name=name, in_specs=[ANY] * ns_in, out_specs=[ANY] * ns_out, out_shape=side.out_shapes,
        scratch_shapes=[pltpu.SemaphoreType.DMA((side.n_copies,)), pltpu.SemaphoreType.DMA((side.n_copies,))],
    )(*side.inputs)
    return list(outs)


ELEMENTWISE_BLOCK_BYTES = 2 << 20


def _tile(r, c, itemsize):
    tr, tc = r, c
    while tr * tc * itemsize > ELEMENTWISE_BLOCK_BYTES and tr % 32 == 0:
        tr //= 2
    while tr * tc * itemsize > ELEMENTWISE_BLOCK_BYTES and tc % 256 == 0:
        tc //= 2
    return tr, tc


def _add_own_half(gb, got, sel, name):
    _, _, hr, c = gb.shape
    tr, tc = _tile(hr, c, 4)

    def body(sel_ref, a_ref, b_ref, o_ref):
        o_ref[...] = (a_ref[0].astype(F32) + b_ref[...].astype(F32)).astype(BF16)

    return pl.pallas_call(
        body, name=name,
        grid_spec=pltpu.PrefetchScalarGridSpec(
            num_scalar_prefetch=1, grid=(4, hr // tr, c // tc),
            in_specs=[pl.BlockSpec((1, 1, tr, tc), lambda k, i, j, sel_ref: (sel_ref[0], k, i, j)),
                      pl.BlockSpec((1, tr, tc), lambda k, i, j, sel_ref: (k, i, j))],
            out_specs=pl.BlockSpec((1, tr, tc), lambda k, i, j, sel_ref: (k, i, j))),
        out_shape=jax.ShapeDtypeStruct(got.shape, BF16),
        compiler_params=_params(("parallel", "parallel", "parallel")),
    )(sel, gb, got)


def _add_own_cols(g, got, sel, name):
    r, c = g.shape
    hr, hc = r // 4, c // 2
    tr, tc = _tile(hr, hc, 4)
    nj = hc // tc

    def body(sel_ref, a_ref, b_ref, o_ref):
        o_ref[...] = (a_ref[...].astype(F32) + b_ref[...].astype(F32)).astype(BF16)

    t = pl.pallas_call(
        body, name=name,
        grid_spec=pltpu.PrefetchScalarGridSpec(
            num_scalar_prefetch=1, grid=(r // tr, nj),
            in_specs=[pl.BlockSpec((tr, tc), lambda i, j, sel_ref: (i, sel_ref[0] * nj + j)),
                      pl.BlockSpec((tr, tc), lambda i, j, sel_ref: (i, j))],
            out_specs=pl.BlockSpec((tr, tc), lambda i, j, sel_ref: (i, j))),
        out_shape=jax.ShapeDtypeStruct((r, hc), BF16),
        compiler_params=_params(("parallel", "parallel")),
    )(sel, g, got)
    return t.reshape(4, hr, hc)


def _sum_partials(t, rcv, sel, name):
    _, hr, c = t.shape
    tr, tc = _tile(hr, c, 4)

    def body(sel_ref, t_ref, r_ref, o_ref):
        o_ref[...] = ((t_ref[0].astype(F32) + r_ref[0].astype(F32)) + r_ref[1].astype(F32)) + r_ref[2].astype(F32)

    return pl.pallas_call(
        body, name=name,
        grid_spec=pltpu.PrefetchScalarGridSpec(
            num_scalar_prefetch=1, grid=(hr // tr, c // tc),
            in_specs=[pl.BlockSpec((1, tr, tc), lambda i, j, sel_ref: (sel_ref[0], i, j)),
                      pl.BlockSpec((3, tr, tc), lambda i, j, sel_ref: (0, i, j))],
            out_specs=pl.BlockSpec((tr, tc), lambda i, j, sel_ref: (i, j))),
        out_shape=jax.ShapeDtypeStruct((hr, c), F32),
        compiler_params=_params(("parallel", "parallel")),
    )(sel, t, rcv)


class _GroupReduce:
    def __init__(self):
        self.names, self.gb, self.t, self.fh, self.other = [], [], [], [], []

    def put(self, name, gb):
        self.names.append(name)
        self.gb.append(gb)

    def side(self, stage):
        if stage == 1:
            return _side_swap(self.gb, ["lead" if g.ndim == 4 else "cols" for g in self.gb])
        return _side_chips(self.t) if stage == 2 else _side_swap(self.fh, ["all"] * len(self.fh))

    def done(self, stage, outs):
        x, y, cc = _me()
        if stage == 1:
            sel = jnp.reshape(cc, (1,)).astype(jnp.int32)
            self.t = [(_add_own_half if gb.ndim == 4 else _add_own_cols)(gb, got, sel, "rs_add_" + n)
                      for n, gb, got in zip(self.names, self.gb, outs)]
        elif stage == 2:
            sel = jnp.reshape(2 * x + y, (1,)).astype(jnp.int32)
            self.fh = [_sum_partials(t, rcv, sel, "rs_sum_" + n) for n, t, rcv in zip(self.names, self.t, outs)]
        else:
            self.other = list(outs)

    def result(self):
        return {n: (fh, other) for n, fh, other in zip(self.names, self.fh, self.other)}


def _allreduce_small(v, tag):
    r = v.shape[0]

    def body(v_ref, out_ref, buf_ref, send_sems, recv_sems):
        x, y, cc = _me()
        me = 4 * x + 2 * y + cc
        buf_ref[pl.ds(pl.multiple_of(me * r, 8), r), :] = v_ref[...]
        peers = [(x, y, 1 - cc)] + [(px, py, pc) for (px, py) in _chips(x, y) for pc in (cc, 1 - cc)]
        cps = []
        for k, (px, py, pc) in enumerate(peers):
            mine = buf_ref.at[pl.ds(pl.multiple_of(me * r, 8), r), :]
            cps.append(pltpu.make_async_remote_copy(
                src_ref=v_ref, dst_ref=mine, send_sem=send_sems.at[k], recv_sem=recv_sems.at[k],
                device_id=(px, py, pc), device_id_type=MESH))
        for cp in cps:
            cp.start()
        for cp in cps:
            cp.wait()
        acc = buf_ref[0:r, :]
        for d in range(1, 8):
            acc = acc + buf_ref[d * r:(d + 1) * r, :]
        out_ref[...] = acc

    vm = pl.BlockSpec(memory_space=pltpu.VMEM)
    return pl.pallas_call(
        body, name="allreduce_small_" + tag, in_specs=[vm], out_specs=vm,
        out_shape=jax.ShapeDtypeStruct((r, 128), F32),
        scratch_shapes=[pltpu.VMEM((8 * r, 128), F32), pltpu.SemaphoreType.DMA((7,)), pltpu.SemaphoreType.DMA((7,))],
    )(v)


def _adamw(w, g, m, v, name):
    r, c = w.shape
    tr = r
    c1 = 1.0 / (1.0 - ADAM_B1 ** ADAM_STEP)
    c2 = 1.0 / (1.0 - ADAM_B2 ** ADAM_STEP)

    def body(w_ref, g_ref, m_ref, v_ref, d_ref, mo_ref, vo_ref):
        gg = g_ref[...]
        mn = ADAM_B1 * m_ref[...] + (1.0 - ADAM_B1) * gg
        vn = ADAM_B2 * v_ref[...] + (1.0 - ADAM_B2) * (gg * gg)
        mo_ref[...] = mn
        vo_ref[...] = vn
        d_ref[...] = -ADAM_LR * ((mn * c1) / (jnp.sqrt(vn * c2) + ADAM_EPS) + ADAM_WD * w_ref[...])

    blk = pl.BlockSpec((tr, c), lambda i: (i, 0))
    return pl.pallas_call(
        body, name=name, grid=(r // tr,), in_specs=[blk] * 4, out_specs=[blk] * 3,
        out_shape=[jax.ShapeDtypeStruct((r, c), F32)] * 3,
        compiler_params=_params(("parallel",)),
    )(w, g, m, v)


def _adamw_halves(w, g_own, g_other, m, v, sel, name, axis):
    r, c = w.shape
    tr, tc = _tile(g_own.shape[0], g_own.shape[1], 4)
    nh = (g_own.shape[axis]) // (tr, tc)[axis]
    c1 = 1.0 / (1.0 - ADAM_B1 ** ADAM_STEP)
    c2 = 1.0 / (1.0 - ADAM_B2 ** ADAM_STEP)

    def body(sel_ref, w_ref, go_ref, gx_ref, m_ref, v_ref, d_ref, mo_ref, vo_ref, g_ref):
        mine = (pl.program_id(axis) // nh) == sel_ref[0]
        gg = jnp.where(mine, go_ref[...], gx_ref[...])
        g_ref[...] = gg
        mn = ADAM_B1 * m_ref[...] + (1.0 - ADAM_B1) * gg
        vn = ADAM_B2 * v_ref[...] + (1.0 - ADAM_B2) * (gg * gg)
        mo_ref[...] = mn
        vo_ref[...] = vn
        d_ref[...] = -ADAM_LR * ((mn * c1) / (jnp.sqrt(vn * c2) + ADAM_EPS) + ADAM_WD * w_ref[...])

    blk = pl.BlockSpec((tr, tc), lambda i, j, sel_ref: (i, j))
    if axis == 0:
        half = pl.BlockSpec((tr, tc), lambda i, j, sel_ref: (i % nh, j))
    else:
        half = pl.BlockSpec((tr, tc), lambda i, j, sel_ref: (i, j % nh))
    return pl.pallas_call(
        body, name=name,
        grid_spec=pltpu.PrefetchScalarGridSpec(
            num_scalar_prefetch=1, grid=(r // tr, c // tc),
            in_specs=[blk, half, half, blk, blk], out_specs=[blk] * 4),
        out_shape=[jax.ShapeDtypeStruct((r, c), F32)] * 4,
        compiler_params=_params(("parallel", "parallel")),
    )(sel, w, g_own, g_other, m, v)


WEIGHTS = ['norm_g', 'w_in', 'conv_w', 'w_conv_out', 'mla_q_norm_g', 'w_uq', 'mla_kv_norm_g', 'w_ukv',
           'mla_qn_nope_g', 'mla_qn_rope_g', 'mla_kn_nope_g', 'mla_kn_rope_g', 'w_mla_out', 'mem_norm_g',
           'w_mem_kv', 'mem_qn_g', 'mem_kn_g', 'w_mem_out', 'w_o']
COL_SHARDED = ['w_in', 'w_conv_out', 'w_uq', 'w_ukv', 'w_mem_out']
ROW_SHARDED = ['w_mla_out', 'w_mem_kv', 'w_o']
SMALL = ['norm_g', 'conv_w', 'mla_q_norm_g', 'mla_kv_norm_g', 'mla_qn_nope_g', 'mla_qn_rope_g', 'mla_kn_nope_g',
         'mla_kn_rope_g', 'mem_norm_g', 'mem_qn_g', 'mem_kn_g']
SMALL_SIZES = [2048, 3072, 512, 512, 128, 64, 128, 64, 2048, 256, 256]
PACK_ROWS = 72


def _full_from_shards(name, g4):
    if name in COL_SHARDED:
        return g4.transpose(1, 0, 2).reshape(g4.shape[1], 4 * g4.shape[2])
    return g4.reshape(4 * g4.shape[1], g4.shape[2])


def _w_all_t_from_w_in_t(w_in_t):
    conv, small, mz, memq, gates = (w_in_t[0:4096], w_in_t[4096:5184], w_in_t[5184:7232],
                                    w_in_t[7232:9280], w_in_t[9280:15424])
    return jnp.concatenate([gates, mz, conv, memq, small, jnp.zeros((2048 - 1088, D_MODEL), BF16)], axis=0)


def _pad_w_uq(w_uq):
    return jnp.pad(w_uq.reshape(512, MLA_HEADS, 192), ((0, 0), (0, 0), (0, 64))).reshape(512, 4096)


def _grad_halves_by_owner(name, g):
    if name == "w_all_t":
        dw_g, dw_z, dw_c, dw_m, dw_s = g
        return "w_in", jnp.concatenate([dw_c, dw_s[:1088], dw_z, dw_m, dw_g], axis=0)
    if name == "w_uq":
        g = g.reshape(512, MLA_HEADS, HEAD_PAD)[:, :, :192].reshape(512, 3072)
    r, c = g.shape
    if name in COL_SHARDED:
        return name, g.reshape(2, r // 2, 4, c // 4).transpose(0, 2, 1, 3)
    return name, g.reshape(4, 2, r // 8, c).transpose(1, 0, 2, 3)


LATE_WEIGHTS = ['w_conv_out', 'w_uq', 'w_ukv', 'w_mem_out', 'w_mla_out', 'w_mem_kv', 'w_o']
HOSTS = {"proj_fwd": ("gather", 1), "lora_fwd": ("gather", 2),
         "da_conv": ("g1", 1), "q_prep_bwd": ("g1", 2), "k_prep_bwd": ("g1", 3),
         "dw_in_0": ("g2", 1), "dw_in_2": ("g2", 2), "dw_in_4": ("g2", 3),
         "dh_a": ("g3", 1), "dh_b": ("g3", 2)}
GROUP_OF = {"w_o": "g1", "w_conv_out": "g1", "w_mla_out": "g1", "w_mem_out": "g1",
            "w_uq": "g2", "w_ukv": "g2", "w_mem_kv": "g2", "w_in": "g3"}


class _Comm:
    def __init__(self, shards, w_in_t_full, conv_full):
        self.shards = shards
        self.w = {"w_all_t": _w_all_t_from_w_in_t(w_in_t_full), "conv_w": conv_full}
        self.bufs = None
        self.groups = {"g1": _GroupReduce(), "g2": _GroupReduce(), "g3": _GroupReduce()}

    def weight(self, name):
        return self.w[name]

    def side(self, tag):
        if tag not in HOSTS:
            return None
        kind, stage = HOSTS[tag]
        if kind == "gather":
            return _side_gather_ici([self.shards[n] for n in LATE_WEIGHTS]) if stage == 1 else \
                _side_gather_pass(self.bufs)
        return self.groups[kind].side(stage)

    def done(self, tag, outs):
        kind, stage = HOSTS[tag]
        if kind != "gather":
            self.groups[kind].done(stage, outs)
        elif stage == 1:
            self.bufs = list(outs)
        else:
            for n, buf in zip(LATE_WEIGHTS, outs):
                own = self.shards[n]
                self.w[n] = _full_from_shards(n, _with_own(buf.reshape((4,) + own.shape), own))
            self.w["w_uq"] = _pad_w_uq(self.w["w_uq"])

    def put(self, name, g):
        name, gb = _grad_halves_by_owner(name, g)
        self.groups[GROUP_OF[name]].put(name, gb)

    def reduced(self):
        last = self.groups["g3"]
        last.done(3, _exchange(last.side(3), "rs_swap_w_in"))
        out = {}
        for grp in self.groups.values():
            out.update(grp.result())
        return out


def kernel(x, positions, mem, norm_g, w_in, conv_w, w_conv_out, mla_q_norm_g, w_uq, mla_kv_norm_g, w_ukv, mla_qn_nope_g, mla_qn_rope_g, mla_kn_nope_g, mla_kn_rope_g, w_mla_out, mem_norm_g, w_mem_kv, mem_qn_g, mem_kn_g, w_mem_out, w_o, loss_target, m_norm_g, m_w_in, m_conv_w, m_w_conv_out, m_mla_q_norm_g, m_w_uq, m_mla_kv_norm_g, m_w_ukv, m_mla_qn_nope_g, m_mla_qn_rope_g, m_mla_kn_nope_g, m_mla_kn_rope_g, m_w_mla_out, m_mem_norm_g, m_w_mem_kv, m_mem_qn_g, m_mem_kn_g, m_w_mem_out, m_w_o, v_norm_g, v_w_in, v_conv_w, v_w_conv_out, v_mla_q_norm_g, v_w_uq, v_mla_kv_norm_g, v_w_ukv, v_mla_qn_nope_g, v_mla_qn_rope_g, v_mla_kn_nope_g, v_mla_kn_rope_g, v_w_mla_out, v_mem_norm_g, v_w_mem_kv, v_mem_qn_g, v_mem_kn_g, v_w_mem_out, v_w_o):
    args = locals()
    w = {n: args[n][0] for n in WEIGHTS}
    m1 = {n: args["m_" + n][0] for n in WEIGHTS}
    v2 = {n: args["v_" + n][0] for n in WEIGHTS}
    xi, yi, ci = _me()
    chip = 2 * xi + yi

    conv_slot = jnp.zeros((3, 4, 256), F32).at[:, chip, :].set(w["conv_w"] * jnp.where(ci == 0, 1.0, 0.0))
    pre = jnp.zeros((PACK_ROWS * 128,), F32).at[0:3072].set(conv_slot.reshape(3072))
    conv_full = _allreduce_small(pre.reshape(PACK_ROWS, 128), "conv_w").reshape(-1)[0:3072].reshape(3, 1024)

    w_in_t = w["w_in"].T.astype(BF16)
    w_in_t_full = _allgather_shards(w_in_t, "ag_w_in").reshape(4 * w_in_t.shape[0], D_MODEL)
    comm = _Comm({n: w[n].astype(BF16) for n in LATE_WEIGHTS}, w_in_t_full, conv_full)
    gains = {n: w[n].reshape(1, -1) for n in SMALL if n != "conv_w"}

    loss_parts, grad_x, gsmall = _local_step(x[0], positions[0], mem[0], loss_target[0], comm, gains)

    loss_local = 0.5 * jnp.sum(loss_parts) * (1.0 / D_MODEL)
    flat = jnp.concatenate([gsmall[n].reshape(-1) for n in SMALL] + [loss_local.reshape(1)])
    flat = jnp.pad(flat, (0, PACK_ROWS * 128 - flat.shape[0]))
    tot = _allreduce_small(flat.reshape(PACK_ROWS, 128), "grads").reshape(-1)
    grads = {}
    off = 0
    for n, sz in zip(SMALL, SMALL_SIZES):
        grads[n] = tot[off:off + sz]
        off += sz
    loss = tot[off]
    grads["conv_w"] = lax.dynamic_slice(grads["conv_w"].reshape(3, 1024), (0, chip * 256), (3, 256))
    for n in SMALL:
        grads[n] = grads[n].reshape(w[n].shape)

    deltas, new_m, new_v = {}, {}, {}
    for n in SMALL:
        shp = w[n].shape
        two_d = (lambda a: a.reshape(1, -1)) if len(shp) == 1 else (lambda a: a)
        d, mn, vn = _adamw(two_d(w[n]), two_d(grads[n]), two_d(m1[n]), two_d(v2[n]), "adamw_" + n)
        deltas[n], new_m[n], new_v[n] = d.reshape(shp), mn.reshape(shp), vn.reshape(shp)

    sel_c = jnp.reshape(ci, (1,)).astype(jnp.int32)
    reduced = comm.reduced()
    for n in COL_SHARDED + ROW_SHARDED:
        g_own, g_other = reduced[n]
        if n == "w_in":
            outs = _adamw_halves(w[n].T, g_own, g_other, m1[n].T, v2[n].T, sel_c, "adamw_" + n, 1)
            deltas[n], new_m[n], new_v[n], grads[n] = [o.T for o in outs]
        else:
            deltas[n], new_m[n], new_v[n], grads[n] = _adamw_halves(w[n], g_own, g_other, m1[n], v2[n], sel_c,
                                                                     "adamw_" + n, 0)

    lead = lambda a: a[None]
    return (loss, grad_x[None], *[lead(grads[n]) for n in WEIGHTS], *[lead(deltas[n]) for n in WEIGHTS],
            *[lead(new_m[n]) for n in WEIGHTS], *[lead(new_v[n]) for n in WEIGHTS])
```

```python
import functools

import numpy as np
import jax
import jax.numpy as jnp
from jax import lax
from jax.experimental import pallas as pl
from jax.experimental.pallas import tpu as pltpu

F32 = jnp.float32
BF16 = jnp.bfloat16
MESH = pl.DeviceIdType.MESH

D_MODEL = 2048
EPS = 1e-6
CHUNK = 64
MLA_HEADS = 16
QK_NOPE = 128
QK_ROPE = 64
HEAD_PAD = 256
MEM_TOKENS = 256
MEM_HEADS = 4
MEM_HEAD_DIM = 256
ROPE_THETA = 10000.0
MLA_SCALE = (QK_NOPE + QK_ROPE) ** -0.5
MEM_SCALE = MEM_HEAD_DIM ** -0.5
LN2 = 0.6931471805599453
Q_SCALE = MLA_SCALE / LN2
NEG = -1e30

ADAM_LR = 0.001
ADAM_B1 = 0.9
ADAM_B2 = 0.999
ADAM_EPS = 1e-08
ADAM_WD = 0.01
ADAM_STEP = 10

NP = 16384
COL_G, COL_Z, COL_C, COL_M, COL_S = 0, 6144, 8192, 12288, 14336

TM_PROJ = 1024
TN_PROJ = 2048
TS_ROW = 512
TS_BWD = 256
TQ_ATT = 1024
TK_ATT = 1024
TR_ATT = 256
FWD_HEADS_PER_STEP = 8
VMEM_LIMIT = 56 * 1024 * 1024

NT_DIMS = (((1,), (1,)), ((), ()))
TN_DIMS = (((0,), (0,)), ((), ()))


def _dot(a, b, mode="nn"):
    if mode == "nn":
        return jnp.dot(a, b, preferred_element_type=F32)
    return lax.dot_general(a, b, NT_DIMS if mode == "nt" else TN_DIMS, preferred_element_type=F32)


def _sigmoid(z):
    return 1.0 / (1.0 + jnp.exp(-z))


def _params(sem=None, vmem=VMEM_LIMIT):
    return pltpu.CompilerParams(dimension_semantics=sem, vmem_limit_bytes=vmem)


def _fold8(v):
    r, c = v.shape
    return v.reshape(r // 8, 8, c).sum(axis=0)


def _swap32(t):
    lane = lax.broadcasted_iota(jnp.int32, t.shape, 1)
    return jnp.where(lane < 32, pltpu.roll(t, 96, 1), pltpu.roll(t, 32, 1))


def _rope(t, cs, sg):
    return t * cs + _swap32(t) * sg


def _rope_bwd(d, cs, sg):
    return d * cs + _swap32(d * sg)


def _rms_fwd(xf, g, n):
    r = lax.rsqrt(jnp.sum(xf * xf, axis=-1, keepdims=True) * (1.0 / n) + EPS)
    return xf * r * g


def _rms_bwd(xf, g, dy, n):
    r = lax.rsqrt(jnp.sum(xf * xf, axis=-1, keepdims=True) * (1.0 / n) + EPS)
    xhat = xf * r
    dyg = dy * g
    dx = r * (dyg - xhat * (jnp.sum(dyg * xhat, axis=-1, keepdims=True) * (1.0 / n)))
    return dx, dy * xhat


ANY = pl.BlockSpec(memory_space=pl.ANY)


class _Side:
    def __init__(self, inputs, out_shapes, n_copies, build, aliases=()):
        self.inputs, self.out_shapes, self.n_copies = list(inputs), list(out_shapes), n_copies
        self.build, self.aliases = build, dict(aliases)


def _pcall(body, side, name, grid, in_specs, out_specs, out_shape, args, sem, scratch_shapes=(), prefetch=()):
    n_pre, n_in, n_out = len(prefetch), len(in_specs), len(out_specs)
    if side is None:
        outs = pl.pallas_call(
            body, name=name,
            grid_spec=pltpu.PrefetchScalarGridSpec(num_scalar_prefetch=n_pre, grid=grid, in_specs=list(in_specs),
                                                   out_specs=list(out_specs), scratch_shapes=list(scratch_shapes)),
            out_shape=list(out_shape), compiler_params=_params(sem),
        )(*prefetch, *args)
        return list(outs), []
    ns_in, ns_out = len(side.inputs), len(side.out_shapes)

    def wrapped(*refs):
        pre, r = refs[:n_pre], refs[n_pre:]
        ins, s_ins = r[:n_in], r[n_in:n_in + ns_in]
        outs = r[n_in + ns_in:n_in + ns_in + n_out]
        s_outs = r[n_in + ns_in + n_out:n_in + ns_in + n_out + ns_out]
        scr, send_sems, recv_sems = r[n_in + ns_in + n_out + ns_out:-2], r[-2], r[-1]
        pairs = side.build(s_ins, s_outs, send_sems, recv_sems)
        first = pl.program_id(0) == 0
        last = pl.program_id(0) == grid[0] - 1
        for d in range(1, len(grid)):
            first = first & (pl.program_id(d) == 0)
            last = last & (pl.program_id(d) == grid[d] - 1)

        @pl.when(first)
        def _():
            for send, _ in pairs:
                send.start()

        body(*pre, *ins, *outs, *scr)

        @pl.when(last)
        def _():
            for send, recv in pairs:
                send.wait_send()
                recv.wait_recv()

    outs = pl.pallas_call(
        wrapped, name=name,
        grid_spec=pltpu.PrefetchScalarGridSpec(
            num_scalar_prefetch=n_pre, grid=grid, in_specs=list(in_specs) + [ANY] * ns_in,
            out_specs=list(out_specs) + [ANY] * ns_out,
            scratch_shapes=list(scratch_shapes) + [pltpu.SemaphoreType.DMA((side.n_copies,)),
                                                   pltpu.SemaphoreType.DMA((side.n_copies,))]),
        out_shape=list(out_shape) + side.out_shapes,
        input_output_aliases={n_pre + n_in + i: n_out + o for i, o in side.aliases.items()},
        compiler_params=_params(("arbitrary",) * len(grid)),
    )(*prefetch, *args, *side.inputs)
    return list(outs[:n_out]), list(outs[n_out:])


def _matmul(a, b, mode, out_dtype, name, tm=1024, tn=1024, tk=None, side=None):
    if tk is None:
        tk = 4096 if mode == "tn" else 2048
    if mode == "nn":
        (m, k), (_, n) = a.shape, b.shape
    elif mode == "nt":
        (m, k), (n, _) = a.shape, b.shape
    else:
        (k, m), (_, n) = a.shape, b.shape
    tm, tn, tk = min(tm, m), min(tn, n), min(tk, k)
    nk = k // tk
    if mode == "tn":
        a_spec = pl.BlockSpec((tk, tm), lambda i, j, kk: (kk, i))
    else:
        a_spec = pl.BlockSpec((tm, tk), lambda i, j, kk: (i, kk))
    if mode == "nt":
        b_spec = pl.BlockSpec((tn, tk), lambda i, j, kk: (j, kk))
    else:
        b_spec = pl.BlockSpec((tk, tn), lambda i, j, kk: (kk, j))

    def body(a_ref, b_ref, o_ref, acc_ref):
        kk = pl.program_id(2)

        @pl.when(kk == 0)
        def _():
            acc_ref[...] = jnp.zeros_like(acc_ref)

        acc_ref[...] += _dot(a_ref[...].astype(BF16), b_ref[...].astype(BF16), mode)

        @pl.when(kk == nk - 1)
        def _():
            o_ref[...] = acc_ref[...].astype(out_dtype)

    def single_body(a_ref, b_ref, o_ref):
        o_ref[...] = _dot(a_ref[...].astype(BF16), b_ref[...].astype(BF16), mode).astype(out_dtype)

    outs, extra = _pcall(
        single_body if nk == 1 else body, side, name, (m // tm, n // tn, nk), [a_spec, b_spec],
        [pl.BlockSpec((tm, tn), lambda i, j, kk: (i, j))], [jax.ShapeDtypeStruct((m, n), out_dtype)], (a, b),
        ("parallel", "parallel", "arbitrary"), scratch_shapes=[] if nk == 1 else [pltpu.VMEM((tm, tn), F32)])
    return outs[0] if side is None else (outs[0], extra)


def _norm_fwd(x, g):
    s = x.shape[0]
    ts = min(TS_ROW, s)

    def body(x_ref, g_ref, h_ref):
        h_ref[...] = _rms_fwd(x_ref[...], g_ref[...], D_MODEL).astype(BF16)

    row = pl.BlockSpec((ts, D_MODEL), lambda i: (i, 0))
    return pl.pallas_call(
        body, name="norm_fwd", grid=(s // ts,),
        in_specs=[row, pl.BlockSpec((1, D_MODEL), lambda i: (0, 0))], out_specs=row,
        out_shape=jax.ShapeDtypeStruct((s, D_MODEL), BF16), compiler_params=_params(("parallel",)),
    )(x, g)


def _shift_rows(v, carry, j, ts, back):
    if back:
        main = pltpu.roll(v, j, 0)
        edge = pltpu.roll(jnp.concatenate([carry, v[:8]], axis=0), j, 0)[8:]
        return jnp.concatenate([edge, main[8:]], axis=0)
    main = pltpu.roll(v, ts - j, 0)
    edge = pltpu.roll(jnp.concatenate([v[ts - 8:], carry], axis=0), 16 - j, 0)[:8]
    return jnp.concatenate([main[:ts - 8], edge], axis=0)


def _conv_fwd(p_all, conv_w):
    s = p_all.shape[0]
    ts = min(TS_ROW, s)
    c0 = COL_C // 1024

    def body(cg_ref, bg_ref, u_ref, z_ref, w_ref, a_ref, co_ref, carry_ref):
        @pl.when(pl.program_id(0) == 0)
        def _():
            carry_ref[...] = jnp.zeros_like(carry_ref)

        p = cg_ref[...].astype(F32) * u_ref[...].astype(F32)
        carry = carry_ref[...]
        co = (w_ref[2:3, :] * p + w_ref[1:2, :] * _shift_rows(p, carry, 1, ts, True)
              + w_ref[0:1, :] * _shift_rows(p, carry, 2, ts, True))
        carry_ref[...] = p[ts - 8:]
        z = z_ref[...].astype(F32)
        a_ref[...] = (bg_ref[...].astype(F32) * co * (z * _sigmoid(z))).astype(BF16)
        co_ref[...] = co.astype(BF16)

    seg = lambda c: pl.BlockSpec((ts, 1024), lambda i: (i, c0 + c))
    return pl.pallas_call(
        body, name="conv_fwd", grid=(s // ts,),
        in_specs=[seg(0), seg(1), seg(2), seg(3), pl.BlockSpec((3, 1024), lambda i: (0, 0))],
        out_specs=[pl.BlockSpec((ts, 1024), lambda i: (i, 0))] * 2,
        out_shape=[jax.ShapeDtypeStruct((s, 1024), BF16)] * 2,
        scratch_shapes=[pltpu.VMEM((8, 1024), F32)],
        compiler_params=_params(("arbitrary",)),
    )(p_all, p_all, p_all, p_all, conv_w)


def _lora_fwd(p_all, gq, gkv, gkr, cs, sg, side=None):
    s = p_all.shape[0]
    ts = min(TS_ROW, s)

    def body(cq_ref, ckv_ref, kr_ref, gq_ref, gkv_ref, gkr_ref, cs_ref, sg_ref, cqn_ref, ckvn_ref, krope_ref):
        cqn_ref[...] = _rms_fwd(cq_ref[...].astype(F32), gq_ref[...], 512).astype(BF16)
        ckvn_ref[...] = _rms_fwd(ckv_ref[...].astype(F32), gkv_ref[...], 512).astype(BF16)
        kn = _rms_fwd(kr_ref[...].astype(F32), gkr_ref[...], QK_ROPE)
        krope_ref[...] = _rope(kn, cs_ref[...], sg_ref[...]).astype(BF16)

    row = lambda w: pl.BlockSpec((ts, w), lambda i: (i, 0))
    vec = lambda w: pl.BlockSpec((1, w), lambda i: (0, 0))
    return _pcall(
        body, side, "lora_fwd", (s // ts,),
        [pl.BlockSpec((ts, 512), lambda i: (i, COL_S // 512)),
         pl.BlockSpec((ts, 512), lambda i: (i, COL_S // 512 + 1)),
         pl.BlockSpec((ts, 128), lambda i: (i, (COL_S + 1024) // 128)),
         vec(512), vec(512), vec(128), row(128), row(128)],
        [row(512), row(512), row(128)],
        [jax.ShapeDtypeStruct((s, 512), BF16), jax.ShapeDtypeStruct((s, 512), BF16),
         jax.ShapeDtypeStruct((s, 128), BF16)],
        (p_all, p_all, p_all, gq, gkv, gkr, cs, sg), ("parallel",))


UP_HEADS = 4


def _q_up(cqn, w_uq, gn, gr, cs, sg):
    s = cqn.shape[0]
    tm, tn = min(TS_ROW, s), UP_HEADS * HEAD_PAD

    def body(c_ref, w_ref, gn_ref, gr_ref, cs_ref, sg_ref, q_ref, o_ref):
        qb = _dot(c_ref[...], w_ref[...]).astype(BF16)
        q_ref[...] = qb
        for h in range(UP_HEADS):
            q = qb[:, h * HEAD_PAD:(h + 1) * HEAD_PAD].astype(F32)
            a = _rms_fwd(q[:, :128], gn_ref[...], QK_NOPE)
            b = _rope(_rms_fwd(q[:, 128:], gr_ref[...], QK_ROPE), cs_ref[...], sg_ref[...])
            o_ref[h] = (jnp.concatenate([a, b], axis=1) * Q_SCALE).astype(BF16)

    vec = pl.BlockSpec((1, 128), lambda i, j: (0, 0))
    tab = pl.BlockSpec((tm, 128), lambda i, j: (i, 0))
    return pl.pallas_call(
        body, name="q_up", grid=(s // tm, MLA_HEADS // UP_HEADS),
        in_specs=[pl.BlockSpec((tm, 512), lambda i, j: (i, 0)), pl.BlockSpec((512, tn), lambda i, j: (0, j)),
                  vec, vec, tab, tab],
        out_specs=[pl.BlockSpec((tm, tn), lambda i, j: (i, j)),
                   pl.BlockSpec((UP_HEADS, tm, HEAD_PAD), lambda i, j: (j, i, 0))],
        out_shape=[jax.ShapeDtypeStruct((s, MLA_HEADS * HEAD_PAD), BF16),
                   jax.ShapeDtypeStruct((MLA_HEADS, s, HEAD_PAD), BF16)],
        compiler_params=_params(("parallel", "parallel")),
    )(cqn, w_uq, gn, gr, cs, sg)


def _kv_up(ckvn, w_ukv, krope, gk):
    s = ckvn.shape[0]
    tm, tn = min(TS_ROW, s), UP_HEADS * HEAD_PAD

    def body(c_ref, w_ref, kr_ref, gk_ref, kv_ref, o_ref, vt_ref):
        kvb = _dot(c_ref[...], w_ref[...]).astype(BF16)
        kv_ref[...] = kvb
        for h in range(UP_HEADS):
            a = _rms_fwd(kvb[:, h * HEAD_PAD:h * HEAD_PAD + 128].astype(F32), gk_ref[...], QK_NOPE)
            o_ref[h] = jnp.concatenate([a.astype(BF16), kr_ref[...]], axis=1)
            vt_ref[h] = kvb[:, h * HEAD_PAD + 128:(h + 1) * HEAD_PAD].astype(F32).T.astype(BF16)

    return pl.pallas_call(
        body, name="kv_up", grid=(s // tm, MLA_HEADS // UP_HEADS),
        in_specs=[pl.BlockSpec((tm, 512), lambda i, j: (i, 0)), pl.BlockSpec((512, tn), lambda i, j: (0, j)),
                  pl.BlockSpec((tm, 128), lambda i, j: (i, 0)), pl.BlockSpec((1, 128), lambda i, j: (0, 0))],
        out_specs=[pl.BlockSpec((tm, tn), lambda i, j: (i, j)),
                   pl.BlockSpec((UP_HEADS, tm, HEAD_PAD), lambda i, j: (j, i, 0)),
                   pl.BlockSpec((UP_HEADS, 128, tm), lambda i, j: (j, 0, i))],
        out_shape=[jax.ShapeDtypeStruct((s, MLA_HEADS * HEAD_PAD), BF16),
                   jax.ShapeDtypeStruct((MLA_HEADS, s, HEAD_PAD), BF16),
                   jax.ShapeDtypeStruct((MLA_HEADS, 128, s), BF16)],
        compiler_params=_params(("parallel", "parallel")),
    )(ckvn, w_ukv, krope, gk)


def _chunk_mask(tq, tk, row0=0):
    r = (lax.broadcasted_iota(jnp.int32, (tq, tk), 0) + row0) // CHUNK
    c = lax.broadcasted_iota(jnp.int32, (tq, tk), 1) // CHUNK
    return c <= r


def _causal_pairs(n, by_key):
    if by_key:
        pairs = [(i, j) for j in range(n) for i in range(j, n)]
    else:
        pairs = [(i, j) for i in range(n) for j in range(i + 1)]
    return (jnp.asarray([p[0] for p in pairs], jnp.int32), jnp.asarray([p[1] for p in pairs], jnp.int32))


def _flash_fwd(qcat, kcat, vt, p_all):
    s = qcat.shape[1]
    tq = tk = min(TQ_ATT, s)
    tc = min(TR_ATT, tq)
    nq = s // tq
    hg = FWD_HEADS_PER_STEP
    zc = COL_Z // (128 * hg)
    qi, kj = _causal_pairs(nq, False)

    def body(qi_ref, kj_ref, q_ref, k_ref, vt_ref, z_ref, y_ref, a_ref, lse_ref, m_sc, l_sc, acc_sc):
        t = pl.program_id(1)
        i, j = qi_ref[t], kj_ref[t]

        @pl.when(j == 0)
        def _():
            m_sc[...] = jnp.full_like(m_sc, NEG)
            l_sc[...] = jnp.zeros_like(l_sc)
            acc_sc[...] = jnp.zeros_like(acc_sc)

        def step(masked):
            nc = tq // tc
            state = {(g, c): [m_sc[g, :, c * tc:(c + 1) * tc], l_sc[g, :, c * tc:(c + 1) * tc],
                              acc_sc[g, :, c * tc:(c + 1) * tc]] for g in range(hg) for c in range(nc)}
            units = [(g, u, c) for u in range(tk // tc) for c in range(nc) for g in range(hg)
                     if (u <= c or not masked)]

            def scores(g, u, c):
                return _dot(k_ref[g, u * tc:(u + 1) * tc, :], q_ref[g, c * tc:(c + 1) * tc, :], "nt")

            def weighted_values(g, u, c, alpha, pb):
                state[g, c][2] = alpha * state[g, c][2] + _dot(vt_ref[g, :, u * tc:(u + 1) * tc], pb)

            ahead = 4
            pending = [scores(*un) for un in units[:ahead]]
            late = None
            for n, (g, u, c) in enumerate(units):
                st = pending.pop(0)
                if n + ahead < len(units):
                    pending.append(scores(*units[n + ahead]))
                if masked and u == c:
                    kc = lax.broadcasted_iota(jnp.int32, (tc, tc), 0) // CHUNK
                    qc = lax.broadcasted_iota(jnp.int32, (tc, tc), 1) // CHUNK
                    st = jnp.where(kc <= qc, st, NEG)
                m_run, l_run, _ = state[g, c]
                m_new = jnp.maximum(m_run, jnp.max(st, axis=0, keepdims=True))
                alpha = jnp.exp2(m_run - m_new)
                p = jnp.exp2(st - m_new)
                state[g, c][0] = m_new
                state[g, c][1] = alpha * l_run + jnp.sum(p, axis=0, keepdims=True)
                if late is not None:
                    weighted_values(*late)
                late = (g, u, c, alpha, p.astype(BF16))
            weighted_values(*late)
            for g in range(hg):
                m_sc[g] = jnp.concatenate([state[g, c][0] for c in range(nc)], axis=1)
                l_sc[g] = jnp.concatenate([state[g, c][1] for c in range(nc)], axis=1)
                acc_sc[g] = jnp.concatenate([state[g, c][2] for c in range(nc)], axis=1)

        @pl.when(j < i)
        def _():
            step(False)

        @pl.when(j == i)
        def _():
            step(True)
            for g in range(hg):
                cols = slice(g * 128, (g + 1) * 128)
                y = (acc_sc[g] * (1.0 / l_sc[g])).T
                z = z_ref[:, cols].astype(F32)
                y_ref[:, cols] = y.astype(BF16)
                a_ref[:, cols] = (y * (z * _sigmoid(z))).astype(BF16)
                lse2 = m_sc[g] + jnp.log(l_sc[g]) * (1.0 / LN2)
                lse_ref[g] = jnp.broadcast_to(lse2, (128, tq)).T

    wide = pl.BlockSpec((tq, 128 * hg), lambda h, t, qi_ref, kj_ref: (qi_ref[t], h))
    return pl.pallas_call(
        body, name="flash_fwd",
        grid_spec=pltpu.PrefetchScalarGridSpec(
            num_scalar_prefetch=2, grid=(MLA_HEADS // hg, qi.shape[0]),
            in_specs=[pl.BlockSpec((hg, tq, HEAD_PAD), lambda h, t, qi_ref, kj_ref: (h, qi_ref[t], 0)),
                      pl.BlockSpec((hg, tk, HEAD_PAD), lambda h, t, qi_ref, kj_ref: (h, kj_ref[t], 0)),
                      pl.BlockSpec((hg, 128, tk), lambda h, t, qi_ref, kj_ref: (h, 0, kj_ref[t])),
                      pl.BlockSpec((tq, 128 * hg), lambda h, t, qi_ref, kj_ref: (qi_ref[t], zc + h))],
            out_specs=[wide, wide,
                       pl.BlockSpec((hg, tq, 128), lambda h, t, qi_ref, kj_ref: (h, qi_ref[t], 0))],
            scratch_shapes=[pltpu.VMEM((hg, 1, tq), F32), pltpu.VMEM((hg, 1, tq), F32),
                            pltpu.VMEM((hg, 128, tq), F32)]),
        out_shape=[jax.ShapeDtypeStruct((s, MLA_HEADS * 128), BF16),
                   jax.ShapeDtypeStruct((s, MLA_HEADS * 128), BF16),
                   jax.ShapeDtypeStruct((MLA_HEADS, s, 128), F32)],
        compiler_params=_params(("parallel", "arbitrary")),
    )(qi, kj, qcat, kcat, vt, p_all)


def _mem_kv_fwd(mem, g_norm, w_kv, g_kn):
    m = mem.shape[0]

    def body(mem_ref, g_ref, w_ref, gk_ref, memn_ref, kraw_ref, kn_ref, v_ref):
        memn = _rms_fwd(mem_ref[...], g_ref[...], D_MODEL).astype(BF16)
        memn_ref[...] = memn
        kvm = _dot(memn, w_ref[...])
        kraw_ref[...] = kvm[:, :1024]
        v_ref[...] = kvm[:, 1024:].astype(BF16)
        for hh in range(MEM_HEADS):
            sl = slice(hh * MEM_HEAD_DIM, (hh + 1) * MEM_HEAD_DIM)
            kn_ref[:, sl] = _rms_fwd(kvm[:, sl], gk_ref[...], MEM_HEAD_DIM).astype(BF16)

    return pl.pallas_call(
        body, name="mem_kv_fwd",
        out_shape=[jax.ShapeDtypeStruct((m, D_MODEL), BF16), jax.ShapeDtypeStruct((m, 1024), F32),
                   jax.ShapeDtypeStruct((m, 1024), BF16), jax.ShapeDtypeStruct((m, 1024), BF16)],
        compiler_params=_params(),
    )(mem, g_norm, w_kv, g_kn)


def _mem_attn_fwd(p_all, kn, v, gq):
    s = p_all.shape[0]
    tq = min(TS_ROW, s)

    def body(q_ref, z_ref, kn_ref, v_ref, gq_ref, a_ref):
        z = z_ref[...].astype(F32)
        gate = z * _sigmoid(z)
        for hh in range(MEM_HEADS):
            sl = slice(hh * MEM_HEAD_DIM, (hh + 1) * MEM_HEAD_DIM)
            qn = _rms_fwd(q_ref[:, sl].astype(F32), gq_ref[...], MEM_HEAD_DIM).astype(BF16)
            sc = _dot(qn, kn_ref[:, sl], "nt") * MEM_SCALE
            e = jnp.exp(sc - jnp.max(sc, axis=-1, keepdims=True))
            p = e * (1.0 / jnp.sum(e, axis=-1, keepdims=True))
            y = _dot(p.astype(BF16), v_ref[:, sl])
            a_ref[:, sl] = (y * gate[:, sl]).astype(BF16)

    full = lambda a: pl.BlockSpec(a.shape, lambda i: (0, 0))
    return pl.pallas_call(
        body, name="mem_attn_fwd", grid=(s // tq,),
        in_specs=[pl.BlockSpec((tq, 1024), lambda i: (i, COL_M // 1024)),
                  pl.BlockSpec((tq, 1024), lambda i: (i, COL_M // 1024 + 1)),
                  full(kn), full(v), full(gq)],
        out_specs=pl.BlockSpec((tq, 1024), lambda i: (i, 0)),
        out_shape=jax.ShapeDtypeStruct((s, 1024), BF16),
        compiler_params=_params(("parallel",)),
    )(p_all, p_all, kn, v, gq)


def _merge_fwd(a_conv, w_c, a_mla, w_m, a_mem, w_e, p_all):
    s = a_conv.shape[0]
    tm, tn = min(TS_ROW, s), 1024
    nj = D_MODEL // tn

    def body(ac_ref, wc_ref, am_ref, wm_ref, ae_ref, we_ref, gc_ref, gm_ref, ge_ref,
             oc_ref, om_ref, oe_ref, mg_ref):
        oc = _dot(ac_ref[...], wc_ref[...])
        om = _dot(am_ref[...], wm_ref[...])
        oe = _dot(ae_ref[...], we_ref[...])
        oc_ref[...] = oc.astype(BF16)
        om_ref[...] = om.astype(BF16)
        oe_ref[...] = oe.astype(BF16)
        mg_ref[...] = (_sigmoid(gc_ref[...].astype(F32)) * oc + _sigmoid(gm_ref[...].astype(F32)) * om
                       + _sigmoid(ge_ref[...].astype(F32)) * oe).astype(BF16)

    act = lambda k: pl.BlockSpec((tm, k), lambda i, j: (i, 0))
    wgt = lambda k: pl.BlockSpec((k, tn), lambda i, j: (0, j))
    gate = lambda b: pl.BlockSpec((tm, tn), lambda i, j: (i, b * nj + j))
    out = pl.BlockSpec((tm, tn), lambda i, j: (i, j))
    return pl.pallas_call(
        body, name="merge_fwd", grid=(s // tm, nj),
        in_specs=[act(1024), wgt(1024), act(2048), wgt(2048), act(1024), wgt(1024), gate(0), gate(1), gate(2)],
        out_specs=[out] * 4,
        out_shape=[jax.ShapeDtypeStruct((s, D_MODEL), BF16)] * 4,
        compiler_params=_params(("parallel", "parallel")),
    )(a_conv, w_c, a_mla, w_m, a_mem, w_e, p_all, p_all, p_all)


def _out_fwd(merged, w_o, x, target):
    s = merged.shape[0]
    tm, tn = min(TS_ROW, s), 1024
    nj = D_MODEL // tn

    def body(mg_ref, w_ref, x_ref, t_ref, dy_ref, dyb_ref, ls_ref):
        e = x_ref[...] + _dot(mg_ref[...], w_ref[...]) - t_ref[...]
        dy = e * (1.0 / D_MODEL)
        dy_ref[...] = dy
        dyb_ref[...] = dy.astype(BF16)
        r = _fold8(e * e)
        acc = r[:, 0:128]
        for cc in range(1, tn // 128):
            acc = acc + r[:, cc * 128:(cc + 1) * 128]
        ls_ref[...] = acc

    tile = pl.BlockSpec((tm, tn), lambda i, j: (i, j))
    return pl.pallas_call(
        body, name="out_fwd", grid=(s // tm, nj),
        in_specs=[pl.BlockSpec((tm, D_MODEL), lambda i, j: (i, 0)),
                  pl.BlockSpec((D_MODEL, tn), lambda i, j: (0, j)), tile, tile],
        out_specs=[tile, tile, pl.BlockSpec((8, 128), lambda i, j: (i, j))],
        out_shape=[jax.ShapeDtypeStruct((s, D_MODEL), F32), jax.ShapeDtypeStruct((s, D_MODEL), BF16),
                   jax.ShapeDtypeStruct((s // tm * 8, nj * 128), F32)],
        compiler_params=_params(("parallel", "parallel")),
    )(merged, w_o, x, target)


def _merge_bwd(dyb, w_o, p_all, o_c, o_m, o_e):
    s = dyb.shape[0]
    tm = min(256, s)

    def body(dy_ref, w_ref, g_ref, oc_ref, om_ref, oe_ref, dc_ref, dm_ref, de_ref, dg_ref):
        dmg = _dot(dy_ref[...], w_ref[...], "nt")
        for b, (o_ref, d_ref) in enumerate(((oc_ref, dc_ref), (om_ref, dm_ref), (oe_ref, de_ref))):
            sl = slice(b * D_MODEL, (b + 1) * D_MODEL)
            sg = _sigmoid(g_ref[:, sl].astype(F32))
            d_ref[...] = (dmg * sg).astype(BF16)
            dg_ref[:, sl] = (dmg * o_ref[...].astype(F32) * sg * (1.0 - sg)).astype(BF16)

    row = pl.BlockSpec((tm, D_MODEL), lambda i: (i, 0))
    wide = pl.BlockSpec((tm, 3 * D_MODEL), lambda i: (i, 0))
    return pl.pallas_call(
        body, name="merge_bwd", grid=(s // tm,),
        in_specs=[row, pl.BlockSpec((D_MODEL, D_MODEL), lambda i: (0, 0)), wide, row, row, row],
        out_specs=[row, row, row, wide],
        out_shape=[jax.ShapeDtypeStruct((s, D_MODEL), BF16)] * 3 + [jax.ShapeDtypeStruct((s, 3 * D_MODEL), BF16)],
        compiler_params=_params(("parallel",)),
    )(dyb, w_o, p_all, o_c, o_m, o_e)


def _conv_bwd(da, p_all, co, conv_w):
    s = da.shape[0]
    ts = min(TS_ROW, s)
    n = s // ts
    c0 = COL_C // 1024

    def body(da_ref, cg_ref, bg_ref, u_ref, z_ref, co_ref, w_ref, dp_ref, dw_ref, carry_ref):
        @pl.when(pl.program_id(0) == 0)
        def _():
            carry_ref[...] = jnp.zeros_like(carry_ref)
            dw_ref[...] = jnp.zeros_like(dw_ref)

        da_ = da_ref[...].astype(F32)
        cg, bg = cg_ref[...].astype(F32), bg_ref[...].astype(F32)
        u, z, cov = u_ref[...].astype(F32), z_ref[...].astype(F32), co_ref[...].astype(F32)
        sz = _sigmoid(z)
        dyc = da_ * (z * sz)
        dz = da_ * (bg * cov) * (sz * (1.0 + z * (1.0 - sz)))
        db = dyc * cov
        dco = dyc * bg
        carry = carry_ref[...]
        d1 = _shift_rows(dco, carry, 1, ts, False)
        d2 = _shift_rows(dco, carry, 2, ts, False)
        carry_ref[...] = dco[:8]
        dpp = w_ref[2:3, :] * dco + w_ref[1:2, :] * d1 + w_ref[0:1, :] * d2
        p = cg * u
        dw_ref[0] += _fold8(p * d2)
        dw_ref[1] += _fold8(p * d1)
        dw_ref[2] += _fold8(p * dco)
        dp_ref[...] = jnp.concatenate([dpp * u, db, dpp * cg, dz], axis=1).astype(BF16)

    rev = lambda c: pl.BlockSpec((ts, 1024), lambda i: (n - 1 - i, c))
    return pl.pallas_call(
        body, name="conv_bwd", grid=(n,),
        in_specs=[rev(0), rev(c0), rev(c0 + 1), rev(c0 + 2), rev(c0 + 3), rev(0),
                  pl.BlockSpec((3, 1024), lambda i: (0, 0))],
        out_specs=[pl.BlockSpec((ts, 4096), lambda i: (n - 1 - i, 0)),
                   pl.BlockSpec((3, 8, 1024), lambda i: (0, 0, 0))],
        out_shape=[jax.ShapeDtypeStruct((s, 4096), BF16), jax.ShapeDtypeStruct((3, 8, 1024), F32)],
        scratch_shapes=[pltpu.VMEM((8, 1024), F32)],
        compiler_params=_params(("arbitrary",)),
    )(da, p_all, p_all, p_all, p_all, co, conv_w)


def _mla_gate_bwd(do_m, w_mla_out, y, p_all):
    s = do_m.shape[0]
    tm, tn = min(TS_ROW, s), 1024
    heads = tn // 128

    def body(do_ref, w_ref, y_ref, z_ref, dy_ref, dz_ref, dl_ref):
        da_ = _dot(do_ref[...], w_ref[...], "nt").astype(BF16).astype(F32)
        yv, z = y_ref[...].astype(F32), z_ref[...].astype(F32)
        sz = _sigmoid(z)
        dyv = da_ * (z * sz)
        dy_ref[...] = dyv.astype(BF16)
        dz_ref[...] = (da_ * yv * (sz * (1.0 + z * (1.0 - sz)))).astype(BF16)
        pr = dyv * yv
        for h in range(heads):
            dl_ref[h] = jnp.broadcast_to(jnp.sum(pr[:, h * 128:(h + 1) * 128], axis=-1, keepdims=True), (tm, 128))

    tile = pl.BlockSpec((tm, tn), lambda i, j: (i, j))
    return pl.pallas_call(
        body, name="mla_gate_bwd", grid=(s // tm, D_MODEL // tn),
        in_specs=[pl.BlockSpec((tm, D_MODEL), lambda i, j: (i, 0)), pl.BlockSpec((tn, D_MODEL), lambda i, j: (j, 0)),
                  tile, pl.BlockSpec((tm, tn), lambda i, j: (i, COL_Z // tn + j))],
        out_specs=[tile, tile, pl.BlockSpec((heads, tm, 128), lambda i, j: (j, i, 0))],
        out_shape=[jax.ShapeDtypeStruct((s, D_MODEL), BF16)] * 2 + [jax.ShapeDtypeStruct((MLA_HEADS, s, 128), F32)],
        compiler_params=_params(("parallel", "parallel")),
    )(do_m, w_mla_out, y, p_all)


def _flash_bwd(qcat, kcat, kv, dy, lse, delta):
    s = qcat.shape[1]
    tq = tk = min(TQ_ATT, s)
    tr = min(TR_ATT, tq)
    nq = s // tq
    qi, kj = _causal_pairs(nq, True)
    npairs = qi.shape[0]

    def body(qi_ref, kj_ref, q_ref, k_ref, v_ref, do_ref, lse_ref, dl_ref, dq_ref, dk_ref, dv_ref,
             dq_acc, dk_acc, dv_acc):
        t = pl.program_id(1)
        i, j = qi_ref[t], kj_ref[t]

        def add_dq(rows, dq_new):
            @pl.when(j == 0)
            def _():
                dq_acc[rows, :] = dq_new

            @pl.when(j > 0)
            def _():
                dq_acc[rows, :] += dq_new

        def full_step():
            q, k = q_ref[0], k_ref[0]
            p = jnp.exp2(_dot(q, k, "nt") - lse_ref[0][:, 0:1])
            do = do_ref[...]
            dpv = _dot(do, v_ref[...], "nt")
            ds = (p * (dpv - dl_ref[0][:, 0:1])).astype(BF16)
            dv_acc[...] += _dot(p.astype(BF16), do, "tn")
            dk_acc[...] += _dot(ds, q, "tn")
            add_dq(pl.ds(pl.multiple_of(i * tq, tq), tq), _dot(ds, k))

        def diagonal_step():
            def keys(r):
                return (r + 1) * tr

            def scores(r):
                rows = slice(r * tr, (r + 1) * tr)
                return (_dot(q_ref[0, rows, :], k_ref[0, :keys(r), :], "nt"),
                        _dot(do_ref[rows, :], v_ref[:keys(r), :], "nt"))

            def gradients(n, r, pb, ds):
                nk = keys(r)
                rows = slice(r * tr, (r + 1) * tr)
                dv_new = _dot(pb, do_ref[rows, :], "tn")
                dk_new = _dot(ds, q_ref[0, rows, :], "tn")
                if n == 0:
                    dv_acc[...] = dv_new
                    dk_acc[...] = dk_new
                else:
                    dv_acc[:nk, :] += dv_new
                    dk_acc[:nk, :] += dk_new
                add_dq(pl.ds(pl.multiple_of(i * tq + r * tr, tr), tr), _dot(ds, k_ref[0, :nk, :]))

            order = list(range(tq // tr - 1, -1, -1))
            pending = [scores(r) for r in order[:2]]
            late = None
            for n, r in enumerate(order):
                sc, dpv = pending.pop(0)
                if n + 2 < len(order):
                    pending.append(scores(order[n + 2]))
                rows = slice(r * tr, (r + 1) * tr)
                sc = jnp.where(_chunk_mask(tr, keys(r), r * tr), sc, NEG)
                p = jnp.exp2(sc - lse_ref[0][rows, 0:1])
                ds = (p * (dpv - dl_ref[0][rows, 0:1])).astype(BF16)
                if late is not None:
                    gradients(*late)
                late = (n, r, p.astype(BF16), ds)
            gradients(*late)

        @pl.when(i == j)
        def _():
            diagonal_step()

        @pl.when(i > j)
        def _():
            full_step()

        @pl.when(i == nq - 1)
        def _():
            dk_ref[0] = dk_acc[...].astype(BF16)
            dv_ref[0] = dv_acc[...].astype(BF16)

        @pl.when(t == npairs - 1)
        def _():
            dq_ref[0] = dq_acc[...].astype(BF16)

    qrow = lambda w: pl.BlockSpec((1, tq, w), lambda h, t, qi_ref, kj_ref: (h, qi_ref[t], 0))
    krow = lambda w: pl.BlockSpec((1, tk, w), lambda h, t, qi_ref, kj_ref: (h, kj_ref[t], 0))
    return pl.pallas_call(
        body, name="flash_bwd",
        grid_spec=pltpu.PrefetchScalarGridSpec(
            num_scalar_prefetch=2, grid=(MLA_HEADS, npairs),
            in_specs=[qrow(HEAD_PAD), krow(HEAD_PAD),
                      pl.BlockSpec((tk, 128), lambda h, t, qi_ref, kj_ref: (kj_ref[t], 2 * h + 1)),
                      pl.BlockSpec((tq, 128), lambda h, t, qi_ref, kj_ref: (qi_ref[t], h)),
                      qrow(128), qrow(128)],
            out_specs=[pl.BlockSpec((1, s, HEAD_PAD), lambda h, t, qi_ref, kj_ref: (h, 0, 0)),
                       krow(HEAD_PAD), krow(128)],
            scratch_shapes=[pltpu.VMEM((s, HEAD_PAD), F32), pltpu.VMEM((tk, HEAD_PAD), F32),
                            pltpu.VMEM((tk, 128), F32)]),
        out_shape=[jax.ShapeDtypeStruct((MLA_HEADS, s, HEAD_PAD), BF16),
                   jax.ShapeDtypeStruct((MLA_HEADS, s, HEAD_PAD), BF16),
                   jax.ShapeDtypeStruct((MLA_HEADS, s, 128), BF16)],
        compiler_params=_params(("parallel", "arbitrary")),
    )(qi, kj, qcat, kcat, kv, dy, lse, delta)


def _q_prep_bwd(dqcat, q_raw, gn, gr, cs, sg, side=None):
    s = q_raw.shape[0]
    ts = min(TS_BWD, s)

    def body(dq_ref, q_ref, gn_ref, gr_ref, cs_ref, sg_ref, o_ref, dg_ref):
        @pl.when(pl.program_id(0) == 0)
        def _():
            dg_ref[...] = jnp.zeros_like(dg_ref)

        ga_acc = jnp.zeros((8, 128), F32)
        gb_acc = jnp.zeros((8, 128), F32)
        for h in range(MLA_HEADS):
            cols = slice(h * HEAD_PAD, (h + 1) * HEAD_PAD)
            q = q_ref[:, cols].astype(F32)
            d = dq_ref[h].astype(F32) * MLA_SCALE
            da, ga = _rms_bwd(q[:, :128], gn_ref[...], d[:, :128], QK_NOPE)
            db, gb = _rms_bwd(q[:, 128:], gr_ref[...], _rope_bwd(d[:, 128:], cs_ref[...], sg_ref[...]), QK_ROPE)
            o_ref[:, cols] = jnp.concatenate([da, db], axis=1).astype(BF16)
            ga_acc = ga_acc + _fold8(ga)
            gb_acc = gb_acc + _fold8(gb)
        dg_ref[0] += ga_acc
        dg_ref[1] += gb_acc

    wide = pl.BlockSpec((ts, MLA_HEADS * HEAD_PAD), lambda i: (i, 0))
    return _pcall(
        body, side, "q_prep_bwd", (s // ts,),
        [pl.BlockSpec((MLA_HEADS, ts, HEAD_PAD), lambda i: (0, i, 0)), wide,
         pl.BlockSpec((1, 128), lambda i: (0, 0)), pl.BlockSpec((1, 128), lambda i: (0, 0)),
         pl.BlockSpec((ts, 128), lambda i: (i, 0)), pl.BlockSpec((ts, 128), lambda i: (i, 0))],
        [wide, pl.BlockSpec((2, 8, 128), lambda i: (0, 0, 0))],
        [jax.ShapeDtypeStruct((s, MLA_HEADS * HEAD_PAD), BF16), jax.ShapeDtypeStruct((2, 8, 128), F32)],
        (dqcat, q_raw, gn, gr, cs, sg), ("arbitrary",))


def _k_prep_bwd(dkcat, dv, kv, gk, side=None):
    s = kv.shape[0]
    ts = min(TS_BWD, s)

    def body(dk_ref, dv_ref, k_ref, gk_ref, o_ref, dkr_ref, dg_ref):
        @pl.when(pl.program_id(0) == 0)
        def _():
            dg_ref[...] = jnp.zeros_like(dg_ref)

        g_acc = jnp.zeros((8, 128), F32)
        dkr = jnp.zeros((ts, 128), F32)
        for h in range(MLA_HEADS):
            d = dk_ref[h].astype(F32) * LN2
            k = k_ref[:, h * HEAD_PAD:h * HEAD_PAD + 128].astype(F32)
            dk_raw, gg = _rms_bwd(k, gk_ref[...], d[:, :128], QK_NOPE)
            o_ref[:, h * HEAD_PAD:(h + 1) * HEAD_PAD] = jnp.concatenate([dk_raw.astype(BF16), dv_ref[h]], axis=1)
            g_acc = g_acc + _fold8(gg)
            dkr = dkr + d[:, 128:]
        dkr_ref[...] = dkr
        dg_ref[...] += g_acc

    wide = pl.BlockSpec((ts, MLA_HEADS * HEAD_PAD), lambda i: (i, 0))
    return _pcall(
        body, side, "k_prep_bwd", (s // ts,),
        [pl.BlockSpec((MLA_HEADS, ts, HEAD_PAD), lambda i: (0, i, 0)),
         pl.BlockSpec((MLA_HEADS, ts, 128), lambda i: (0, i, 0)), wide,
         pl.BlockSpec((1, 128), lambda i: (0, 0))],
        [wide, pl.BlockSpec((ts, 128), lambda i: (i, 0)), pl.BlockSpec((8, 128), lambda i: (0, 0))],
        [jax.ShapeDtypeStruct((s, MLA_HEADS * HEAD_PAD), BF16), jax.ShapeDtypeStruct((s, 128), F32),
         jax.ShapeDtypeStruct((8, 128), F32)],
        (dkcat, dv, kv, gk), ("arbitrary",))


def _lora_bwd(dcqn, dckvn, dkr, p_all, gq, gkv, gkr, cs, sg):
    s = p_all.shape[0]
    ts = min(TS_ROW, s)

    def body(dq_ref, dkv_ref, dkr_ref, cq_ref, ckv_ref, kr_ref, gq_ref, gkv_ref, gkr_ref, cs_ref, sg_ref,
             o_ref, dgq_ref, dgkv_ref, dgkr_ref):
        @pl.when(pl.program_id(0) == 0)
        def _():
            dgq_ref[...] = jnp.zeros_like(dgq_ref)
            dgkv_ref[...] = jnp.zeros_like(dgkv_ref)
            dgkr_ref[...] = jnp.zeros_like(dgkr_ref)

        dq, g1 = _rms_bwd(cq_ref[...].astype(F32), gq_ref[...], dq_ref[...], 512)
        dkv, g2 = _rms_bwd(ckv_ref[...].astype(F32), gkv_ref[...], dkv_ref[...], 512)
        dkn = _rope_bwd(dkr_ref[...], cs_ref[...], sg_ref[...])
        dk, g3 = _rms_bwd(kr_ref[...].astype(F32), gkr_ref[...], dkn, QK_ROPE)
        o_ref[...] = jnp.concatenate([dq, dkv, dk, jnp.zeros((ts, 2048 - 1152), F32)], axis=1).astype(BF16)
        dgq_ref[...] += _fold8(g1)
        dgkv_ref[...] += _fold8(g2)
        dgkr_ref[...] += _fold8(g3)

    row = lambda w: pl.BlockSpec((ts, w), lambda i: (i, 0))
    vec = lambda w: pl.BlockSpec((1, w), lambda i: (0, 0))
    acc = lambda w: pl.BlockSpec((8, w), lambda i: (0, 0))
    return pl.pallas_call(
        body, name="lora_bwd", grid=(s // ts,),
        in_specs=[row(512), row(512), row(128),
                  pl.BlockSpec((ts, 512), lambda i: (i, COL_S // 512)),
                  pl.BlockSpec((ts, 512), lambda i: (i, COL_S // 512 + 1)),
                  pl.BlockSpec((ts, 128), lambda i: (i, (COL_S + 1024) // 128)),
                  vec(512), vec(512), vec(128), row(128), row(128)],
        out_specs=[row(2048), acc(512), acc(512), acc(128)],
        out_shape=[jax.ShapeDtypeStruct((s, 2048), BF16), jax.ShapeDtypeStruct((8, 512), F32),
                   jax.ShapeDtypeStruct((8, 512), F32), jax.ShapeDtypeStruct((8, 128), F32)],
        compiler_params=_params(("arbitrary",)),
    )(dcqn, dckvn, dkr, p_all, p_all, p_all, gq, gkv, gkr, cs, sg)


def _mem_attn_bwd(da, p_all, kn, v, gq):
    s = p_all.shape[0]
    tq = min(TS_ROW, s)

    def body(da_ref, q_ref, z_ref, kn_ref, v_ref, gq_ref, o_ref, dkn_ref, dv_ref, dg_ref):
        @pl.when(pl.program_id(0) == 0)
        def _():
            dkn_ref[...] = jnp.zeros_like(dkn_ref)
            dv_ref[...] = jnp.zeros_like(dv_ref)
            dg_ref[...] = jnp.zeros_like(dg_ref)

        z = z_ref[...].astype(F32)
        da_ = da_ref[...].astype(F32)
        sz = _sigmoid(z)
        gate = z * sz
        dgate = sz * (1.0 + z * (1.0 - sz))
        for hh in range(MEM_HEADS):
            sl = slice(hh * MEM_HEAD_DIM, (hh + 1) * MEM_HEAD_DIM)
            qf = q_ref[:, sl].astype(F32)
            qn = _rms_fwd(qf, gq_ref[...], MEM_HEAD_DIM).astype(BF16)
            knh, vh = kn_ref[:, sl], v_ref[:, sl]
            sc = _dot(qn, knh, "nt") * MEM_SCALE
            e = jnp.exp(sc - jnp.max(sc, axis=-1, keepdims=True))
            p = e * (1.0 / jnp.sum(e, axis=-1, keepdims=True))
            pb = p.astype(BF16)
            y = _dot(pb, vh)
            dyh = da_[:, sl] * gate[:, sl]
            o_ref[:, 1024 + hh * MEM_HEAD_DIM:1024 + (hh + 1) * MEM_HEAD_DIM] = (
                da_[:, sl] * y * dgate[:, sl]).astype(BF16)
            dyb = dyh.astype(BF16)
            dpm = _dot(dyb, vh, "nt")
            ds = (p * (dpm - jnp.sum(dpm * p, axis=-1, keepdims=True)) * MEM_SCALE).astype(BF16)
            dqn = _dot(ds, knh)
            dkn_ref[:, sl] += _dot(ds, qn, "tn")
            dv_ref[:, sl] += _dot(pb, dyb, "tn")
            dq, gg = _rms_bwd(qf, gq_ref[...], dqn, MEM_HEAD_DIM)
            o_ref[:, sl] = dq.astype(BF16)
            dg_ref[...] += _fold8(gg)

    full = lambda a: pl.BlockSpec(a.shape, lambda i: (0, 0))
    return pl.pallas_call(
        body, name="mem_attn_bwd", grid=(s // tq,),
        in_specs=[pl.BlockSpec((tq, 1024), lambda i: (i, 0)),
                  pl.BlockSpec((tq, 1024), lambda i: (i, COL_M // 1024)),
                  pl.BlockSpec((tq, 1024), lambda i: (i, COL_M // 1024 + 1)),
                  full(kn), full(v), full(gq)],
        out_specs=[pl.BlockSpec((tq, 2048), lambda i: (i, 0)),
                   pl.BlockSpec((MEM_TOKENS, 1024), lambda i: (0, 0)),
                   pl.BlockSpec((MEM_TOKENS, 1024), lambda i: (0, 0)),
                   pl.BlockSpec((8, MEM_HEAD_DIM), lambda i: (0, 0))],
        out_shape=[jax.ShapeDtypeStruct((s, 2048), BF16), jax.ShapeDtypeStruct((MEM_TOKENS, 1024), F32),
                   jax.ShapeDtypeStruct((MEM_TOKENS, 1024), F32), jax.ShapeDtypeStruct((8, MEM_HEAD_DIM), F32)],
        compiler_params=_params(("arbitrary",)),
    )(da, p_all, p_all, kn, v, gq)


def _mem_kv_bwd(dkn, dv, kraw, memn, mem, g_norm, w_kv, g_kn):
    m = mem.shape[0]

    def body(dkn_ref, dv_ref, kraw_ref, memn_ref, mem_ref, g_ref, w_ref, gk_ref, dw_ref, dgn_ref, dgk_ref, dkv_sc):
        gk_acc = jnp.zeros((8, MEM_HEAD_DIM), F32)
        for hh in range(MEM_HEADS):
            sl = slice(hh * MEM_HEAD_DIM, (hh + 1) * MEM_HEAD_DIM)
            dk, gg = _rms_bwd(kraw_ref[:, sl], gk_ref[...], dkn_ref[:, sl], MEM_HEAD_DIM)
            dkv_sc[:, sl] = dk.astype(BF16)
            gk_acc = gk_acc + _fold8(gg)
        dgk_ref[...] = gk_acc
        dkv_sc[:, 1024:] = dv_ref[...].astype(BF16)
        dkv = dkv_sc[...]
        dw_ref[...] = _dot(memn_ref[...], dkv, "tn")
        dmemn = _dot(dkv, w_ref[...], "nt")
        xf = mem_ref[...]
        r = lax.rsqrt(jnp.mean(xf * xf, axis=-1, keepdims=True) + EPS)
        dgn_ref[...] = _fold8(dmemn * (xf * r))

    return pl.pallas_call(
        body, name="mem_kv_bwd",
        out_shape=[jax.ShapeDtypeStruct((D_MODEL, D_MODEL), F32), jax.ShapeDtypeStruct((8, D_MODEL), F32),
                   jax.ShapeDtypeStruct((8, MEM_HEAD_DIM), F32)],
        scratch_shapes=[pltpu.VMEM((m, D_MODEL), BF16)],
        compiler_params=_params(),
    )(dkn, dv, kraw, memn, mem, g_norm, w_kv, g_kn)


def _dh_bwd(dps, w_all_t, row0, nrows, name, side=None):
    tm, tk = min(TS_ROW, nrows), 2048
    blk0 = row0 // tm
    widths = [a.shape[1] // tk for a in dps]
    starts = [int(v) for v in np.cumsum([0] + widths[:-1])]
    nk = sum(widths)

    def body(*refs):
        d_refs, w_ref, o_ref, acc_ref = refs[:5], refs[5], refs[6], refs[7]
        kk = pl.program_id(1)

        @pl.when(kk == 0)
        def _():
            acc_ref[...] = jnp.zeros_like(acc_ref)

        for d_ref, st, wd in zip(d_refs, starts, widths):
            @pl.when((kk >= st) & (kk < st + wd))
            def _(d_ref=d_ref):
                acc_ref[...] += _dot(d_ref[...], w_ref[...])

        @pl.when(kk == nk - 1)
        def _():
            o_ref[...] = acc_ref[...]

    def dspec(st, wd):
        return pl.BlockSpec((tm, tk), lambda i, kk: (blk0 + i, jnp.clip(kk - st, 0, wd - 1)))

    outs, extra = _pcall(
        body, side, name, (nrows // tm, nk),
        [dspec(st, wd) for st, wd in zip(starts, widths)] + [pl.BlockSpec((tk, D_MODEL), lambda i, kk: (kk, 0))],
        [pl.BlockSpec((tm, D_MODEL), lambda i, kk: (i, 0))], [jax.ShapeDtypeStruct((nrows, D_MODEL), F32)],
        (*dps, w_all_t), ("parallel", "arbitrary"), scratch_shapes=[pltpu.VMEM((tm, D_MODEL), F32)])
    return outs[0], extra


def _dx_bwd(dh_a, dh_b, x, dy, g):
    s = x.shape[0]
    ts = min(TS_BWD, dh_a.shape[0])
    na, nb = dh_a.shape[0] // ts, dh_b.shape[0] // ts

    def body(dha_ref, dhb_ref, x_ref, dy_ref, g_ref, o_ref, dg_ref):
        @pl.when(pl.program_id(0) == 0)
        def _():
            dg_ref[...] = jnp.zeros_like(dg_ref)

        dh = jnp.where(pl.program_id(0) < na, dha_ref[...], dhb_ref[...])
        dx, gg = _rms_bwd(x_ref[...], g_ref[...], dh, D_MODEL)
        o_ref[...] = dy_ref[...] + dx
        dg_ref[...] += _fold8(gg)

    row = pl.BlockSpec((ts, D_MODEL), lambda i: (i, 0))
    outs, _ = _pcall(
        body, None, "dx_bwd", (s // ts,),
        [pl.BlockSpec((ts, D_MODEL), lambda i: (jnp.minimum(i, na - 1), 0)),
         pl.BlockSpec((ts, D_MODEL), lambda i: (jnp.clip(i - na, 0, nb - 1), 0)),
         row, row, pl.BlockSpec((1, D_MODEL), lambda i: (0, 0))],
        [row, pl.BlockSpec((8, D_MODEL), lambda i: (0, 0))],
        [jax.ShapeDtypeStruct((s, D_MODEL), F32), jax.ShapeDtypeStruct((8, D_MODEL), F32)],
        (dh_a, dh_b, x, dy, g), ("arbitrary",))
    return outs


def _pad128(v, n):
    return jnp.pad(v.reshape(1, n), ((0, 0), (0, 128 - n)))


def _local_step(x, positions, mem, target, comm, gains):
    half = QK_ROPE // 2
    inv_freq = jnp.power(ROPE_THETA, -jnp.arange(half, dtype=F32) / half)
    ang = positions.astype(F32)[:, None] * inv_freq
    cos, sin = jnp.cos(ang), jnp.sin(ang)
    zpad = jnp.zeros((x.shape[0], 64), F32)
    cs = jnp.concatenate([cos, cos, zpad], axis=1)
    sg = jnp.concatenate([-sin, sin, zpad], axis=1)
    g_qr, g_kr = _pad128(gains["mla_qn_rope_g"], 64), _pad128(gains["mla_kn_rope_g"], 64)
    g_qn, g_kn = gains["mla_qn_nope_g"], gains["mla_kn_nope_g"]

    def hosted_matmul(tag, a, b, mode, out_dtype, **tiles):
        side = comm.side(tag)
        if side is None:
            return _matmul(a, b, mode, out_dtype, tag, **tiles)
        out, extra = _matmul(a, b, mode, out_dtype, tag, side=side, **tiles)
        comm.done(tag, extra)
        return out

    h = _norm_fwd(x, gains["norm_g"])
    p_all = hosted_matmul("proj_fwd", h, comm.weight("w_all_t"), "nt", BF16, tm=TM_PROJ, tn=TN_PROJ)
    a_conv, co = _conv_fwd(p_all, comm.weight("conv_w"))
    (cqn, ckvn, krope), extra = _lora_fwd(p_all, gains["mla_q_norm_g"], gains["mla_kv_norm_g"], g_kr, cs, sg,
                                          comm.side("lora_fwd"))
    comm.done("lora_fwd", extra)
    q_raw, qcat = _q_up(cqn, comm.weight("w_uq"), g_qn, g_qr, cs, sg)
    kv, kcat, vt = _kv_up(ckvn, comm.weight("w_ukv"), krope, g_kn)
    mla_y, a_mla, lse = _flash_fwd(qcat, kcat, vt, p_all)
    memn, kraw, kn, vmem = _mem_kv_fwd(mem, gains["mem_norm_g"], comm.weight("w_mem_kv"), gains["mem_kn_g"])
    a_mem = _mem_attn_fwd(p_all, kn, vmem, gains["mem_qn_g"])
    o_c, o_m, o_e, merged = _merge_fwd(a_conv, comm.weight("w_conv_out"), a_mla, comm.weight("w_mla_out"), a_mem,
                                       comm.weight("w_mem_out"), p_all)
    dy, dyb, loss_parts = _out_fwd(merged, comm.weight("w_o"), x, target)

    comm.put("w_o", _matmul(merged, dyb, "tn", BF16, "dw_o"))
    do_c, do_m, do_e, dp_g = _merge_bwd(dyb, comm.weight("w_o"), p_all, o_c, o_m, o_e)
    comm.put("w_conv_out", _matmul(a_conv, do_c, "tn", BF16, "dw_conv_out"))
    comm.put("w_mla_out", _matmul(a_mla, do_m, "tn", BF16, "dw_mla_out"))
    comm.put("w_mem_out", _matmul(a_mem, do_e, "tn", BF16, "dw_mem_out"))
    da_conv = hosted_matmul("da_conv", do_c, comm.weight("w_conv_out"), "nt", BF16)
    da_mem = _matmul(do_e, comm.weight("w_mem_out"), "nt", BF16, "da_mem")
    dp_c, dconv_w = _conv_bwd(da_conv, p_all, co, comm.weight("conv_w"))
    dmla_y, dp_z, delta = _mla_gate_bwd(do_m, comm.weight("w_mla_out"), mla_y, p_all)
    dqcat, dkcat, dv = _flash_bwd(qcat, kcat, kv, dmla_y, lse, delta)
    (dq_raw, dg_q), extra = _q_prep_bwd(dqcat, q_raw, g_qn, g_qr, cs, sg, comm.side("q_prep_bwd"))
    comm.done("q_prep_bwd", extra)
    (dkv, dkr, dg_kn), extra = _k_prep_bwd(dkcat, dv, kv, g_kn, comm.side("k_prep_bwd"))
    comm.done("k_prep_bwd", extra)
    comm.put("w_uq", _matmul(cqn, dq_raw, "tn", BF16, "dw_uq"))
    comm.put("w_ukv", _matmul(ckvn, dkv, "tn", BF16, "dw_ukv"))
    dcqn = _matmul(dq_raw, comm.weight("w_uq"), "nt", F32, "dcqn")
    dckvn = _matmul(dkv, comm.weight("w_ukv"), "nt", F32, "dckvn")
    dp_s, dg_qn, dg_kvn, dg_kr = _lora_bwd(dcqn, dckvn, dkr, p_all, gains["mla_q_norm_g"],
                                            gains["mla_kv_norm_g"], g_kr, cs, sg)
    dp_m, dkn, dvm, dg_mq = _mem_attn_bwd(da_mem, p_all, kn, vmem, gains["mem_qn_g"])
    dw_mem_kv, dg_mn, dg_mk = _mem_kv_bwd(dkn, dvm, kraw, memn, mem, gains["mem_norm_g"],
                                          comm.weight("w_mem_kv"), gains["mem_kn_g"])
    comm.put("w_mem_kv", dw_mem_kv.astype(BF16))
    dps = [dp_g, dp_z, dp_c, dp_m, dp_s]
    comm.put("w_all_t", [hosted_matmul("dw_in_%d" % k, d, h, "tn", BF16) for k, d in enumerate(dps)])
    s = x.shape[0]
    dh_a, extra = _dh_bwd(dps, comm.weight("w_all_t"), 0, s // 4, "dh_a", comm.side("dh_a"))
    comm.done("dh_a", extra)
    dh_b, extra = _dh_bwd(dps, comm.weight("w_all_t"), s // 4, s - s // 4, "dh_b", comm.side("dh_b"))
    comm.done("dh_b", extra)
    grad_x, dg_n = _dx_bwd(dh_a, dh_b, x, dy, gains["norm_g"])

    gsmall = {
        "norm_g": dg_n.sum(0), "conv_w": dconv_w.sum(1), "mla_q_norm_g": dg_qn.sum(0),
        "mla_kv_norm_g": dg_kvn.sum(0), "mla_qn_nope_g": dg_q[0].sum(0), "mla_qn_rope_g": dg_q[1].sum(0)[:64],
        "mla_kn_nope_g": dg_kn.sum(0), "mla_kn_rope_g": dg_kr.sum(0)[:64], "mem_norm_g": dg_mn.sum(0),
        "mem_qn_g": dg_mq.sum(0), "mem_kn_g": dg_mk.sum(0),
    }
    return loss_parts, grad_x, gsmall


def _me():
    return lax.axis_index("x"), lax.axis_index("y"), lax.axis_index("c")


def _chips(x, y):
    return [(1 - x, y), (x, 1 - y), (1 - x, 1 - y)]


def _allgather_shards(xs, name):
    r, c = xs.shape
    hc = c // 2

    def body(x_ref, out_ref, send_sems, recv_sems):
        x, y, cc = _me()
        me, sibling = (x, y, cc), (x, y, 1 - cc)
        across_x, across_y, diagonal = _chips(x, y)

        def rows(px, py, pc):
            return out_ref.at[2 * px + py, :, pl.ds(pc * hc, hc)]

        def copy(k, block, to, src=None):
            return pltpu.make_async_remote_copy(
                src_ref=rows(*block) if src is None else src, dst_ref=rows(*block),
                send_sem=send_sems.at[k], recv_sem=recv_sems.at[k], device_id=to, device_id_type=MESH)

        my_half = x_ref.at[:, pl.ds(cc * hc, hc)]
        direct = [copy(0, me, (*across_x, cc), src=my_half), copy(1, me, (*across_y, cc), src=my_half)]
        for cp in direct:
            cp.start()
        copy(0, (*across_x, cc), me).wait_recv()
        copy(1, (*across_y, cc), me).wait_recv()
        from_x = cc == 0
        came = (jnp.where(from_x, across_x[0], across_y[0]), jnp.where(from_x, across_x[1], across_y[1]), cc)
        goes = (jnp.where(from_x, across_y[0], across_x[0]), jnp.where(from_x, across_y[1], across_x[1]), cc)
        relay = copy(2, came, goes)
        relay.start()
        passed = [copy(3, (*across_x, cc), sibling), copy(4, (*across_y, cc), sibling)]
        for cp in passed:
            cp.start()
        copy(2, (*diagonal, cc), me).wait_recv()
        passed.append(copy(5, (*diagonal, cc), sibling))
        passed[2].start()
        for j, chip in enumerate((across_x, across_y, diagonal)):
            copy(3 + j, (*chip, 1 - cc), me).wait_recv()
        for cp in direct + [relay] + passed:
            cp.wait_send()

    return pl.pallas_call(
        body, name=name, in_specs=[ANY], out_specs=ANY,
        out_shape=jax.ShapeDtypeStruct((4, r, c), xs.dtype),
        scratch_shapes=[pltpu.SemaphoreType.DMA((6,)), pltpu.SemaphoreType.DMA((6,))],
    )(xs)


def _put_own(g4, own, name):
    r, c = own.shape
    tc = c // 2
    xi, yi, _ = _me()
    sel = jnp.reshape(2 * xi + yi, (1,)).astype(jnp.int32)

    def body(sel_ref, own_ref, g_ref, o_ref):
        o_ref[0] = own_ref[...]

    return pl.pallas_call(
        body, name=name,
        grid_spec=pltpu.PrefetchScalarGridSpec(
            num_scalar_prefetch=1, grid=(c // tc,),
            in_specs=[pl.BlockSpec((r, tc), lambda j, sel_ref: (0, j)), ANY],
            out_specs=pl.BlockSpec((1, r, tc), lambda j, sel_ref: (sel_ref[0], 0, j))),
        out_shape=jax.ShapeDtypeStruct(g4.shape, g4.dtype), input_output_aliases={2: 0},
        compiler_params=_params(("arbitrary",)),
    )(sel, own, g4)


def _with_own(g4, own):
    xi, yi, _ = _me()
    pick = lax.broadcasted_iota(jnp.int32, (4,) + (1,) * own.ndim, 0) == 2 * xi + yi
    return jnp.where(pick, own[None], g4)


def _remote(src, dst, send_sems, recv_sems, k, to):
    return pltpu.make_async_remote_copy(src_ref=src, dst_ref=dst, send_sem=send_sems.at[k], recv_sem=recv_sems.at[k],
                                        device_id=to, device_id_type=MESH)


def _side_gather_ici(shards):
    def build(ins, outs, send_sems, recv_sems):
        x, y, cc = _me()
        pairs = []
        for a, (x_ref, o_ref) in enumerate(zip(ins, outs)):
            hr = x_ref.shape[0] // 2
            my_half = x_ref.at[pl.ds(cc * hr, hr), :]
            for j, (px, py) in enumerate(_chips(x, y)):
                mine = o_ref.at[pl.ds((4 * x + 2 * y + cc) * hr, hr), :]
                theirs = o_ref.at[pl.ds((4 * px + 2 * py + cc) * hr, hr), :]
                pairs.append((_remote(my_half, mine, send_sems, recv_sems, 3 * a + j, (px, py, cc)),
                              _remote(my_half, theirs, send_sems, recv_sems, 3 * a + j, (px, py, cc))))
        return pairs

    shapes = [jax.ShapeDtypeStruct((4 * s.shape[0], s.shape[1]), s.dtype) for s in shards]
    return _Side(shards, shapes, 3 * len(shards), build)


def _side_gather_pass(bufs):
    def build(ins, outs, send_sems, recv_sems):
        x, y, cc = _me()
        pairs = []
        for a, o_ref in enumerate(outs):
            hr = o_ref.shape[0] // 8
            for j, (px, py) in enumerate(_chips(x, y)):
                got = o_ref.at[pl.ds((4 * px + 2 * py + cc) * hr, hr), :]
                coming = o_ref.at[pl.ds((4 * px + 2 * py + 1 - cc) * hr, hr), :]
                pairs.append((_remote(got, got, send_sems, recv_sems, 3 * a + j, (x, y, 1 - cc)),
                              _remote(got, coming, send_sems, recv_sems, 3 * a + j, (x, y, 1 - cc))))
        return pairs

    shapes = [jax.ShapeDtypeStruct(b.shape, b.dtype) for b in bufs]
    return _Side(bufs, shapes, 3 * len(bufs), build, aliases={a: a for a in range(len(bufs))})


def _side_swap(srcs, parts):
    def view(s_ref, part, cc):
        if part == "lead":
            return s_ref.at[1 - cc]
        if part == "cols":
            hc = s_ref.shape[1] // 2
            return s_ref.at[:, pl.ds((1 - cc) * hc, hc)]
        return s_ref

    def build(ins, outs, send_sems, recv_sems):
        x, y, cc = _me()
        pairs = []
        for a, (s_ref, o_ref) in enumerate(zip(ins, outs)):
            cp = _remote(view(s_ref, parts[a], cc), o_ref, send_sems, recv_sems, a, (x, y, 1 - cc))
            pairs.append((cp, cp))
        return pairs

    out_shape = {"lead": lambda s: s.shape[1:], "cols": lambda s: (s.shape[0], s.shape[1] // 2), "all": lambda s: s.shape}
    shapes = [jax.ShapeDtypeStruct(out_shape[p](s), s.dtype) for s, p in zip(srcs, parts)]
    return _Side(srcs, shapes, len(srcs), build)


def _side_chips(ts):
    def build(ins, outs, send_sems, recv_sems):
        x, y, cc = _me()
        pairs = []
        for a, (t_ref, r_ref) in enumerate(zip(ins, outs)):
            for j, (px, py) in enumerate(_chips(x, y)):
                cp = _remote(t_ref.at[2 * px + py], r_ref.at[j], send_sems, recv_sems, 3 * a + j, (px, py, cc))
                pairs.append((cp, cp))
        return pairs

    shapes = [jax.ShapeDtypeStruct((3,) + t.shape[1:], t.dtype) for t in ts]
    return _Side(ts, shapes, 3 * len(ts), build)


def _exchange(side, name):
    ns_in, ns_out = len(side.inputs), len(side.out_shapes)

    def body(*refs):
        pairs = side.build(refs[:ns_in], refs[ns_in:ns_in + ns_out], refs[-2], refs[-1])
        for send, _ in pairs:
            send.start()
        for send, recv in pairs:
            send.wait_send()
            recv.wait_recv()

    outs = pl.pallas_call(
        body, name=name, in_specs=[ANY] * ns_in, out_specs=[ANY] * ns_out, out_shape=side.out_shapes,
        scratch_shapes=[pltpu.SemaphoreType.DMA((side.n_copies,)), pltpu.SemaphoreType.DMA((side.n_copies,))],
    )(*side.inputs)
    return list(outs)


ELEMENTWISE_BLOCK_BYTES = 2 << 20


def _tile(r, c, itemsize):
    tr, tc = r, c
    while tr * tc * itemsize > ELEMENTWISE_BLOCK_BYTES and tr % 32 == 0:
        tr //= 2
    while tr * tc * itemsize > ELEMENTWISE_BLOCK_BYTES and tc % 256 == 0:
        tc //= 2
    return tr, tc


def _add_own_half(gb, got, sel, name):
    _, _, hr, c = gb.shape
    tr, tc = _tile(hr, c, 4)

    def body(sel_ref, a_ref, b_ref, o_ref):
        o_ref[...] = (a_ref[0].astype(F32) + b_ref[...].astype(F32)).astype(BF16)

    return pl.pallas_call(
        body, name=name,
        grid_spec=pltpu.PrefetchScalarGridSpec(
            num_scalar_prefetch=1, grid=(4, hr // tr, c // tc),
            in_specs=[pl.BlockSpec((1, 1, tr, tc), lambda k, i, j, sel_ref: (sel_ref[0], k, i, j)),
                      pl.BlockSpec((1, tr, tc), lambda k, i, j, sel_ref: (k, i, j))],
            out_specs=pl.BlockSpec((1, tr, tc), lambda k, i, j, sel_ref: (k, i, j))),
        out_shape=jax.ShapeDtypeStruct(got.shape, BF16),
        compiler_params=_params(("parallel", "parallel", "parallel")),
    )(sel, gb, got)


def _add_own_cols(g, got, sel, name):
    r, c = g.shape
    hr, hc = r // 4, c // 2
    tr, tc = _tile(hr, hc, 4)
    nj = hc // tc

    def body(sel_ref, a_ref, b_ref, o_ref):
        o_ref[...] = (a_ref[...].astype(F32) + b_ref[...].astype(F32)).astype(BF16)

    t = pl.pallas_call(
        body, name=name,
        grid_spec=pltpu.PrefetchScalarGridSpec(
            num_scalar_prefetch=1, grid=(r // tr, nj),
            in_specs=[pl.BlockSpec((tr, tc), lambda i, j, sel_ref: (i, sel_ref[0] * nj + j)),
                      pl.BlockSpec((tr, tc), lambda i, j, sel_ref: (i, j))],
            out_specs=pl.BlockSpec((tr, tc), lambda i, j, sel_ref: (i, j))),
        out_shape=jax.ShapeDtypeStruct((r, hc), BF16),
        compiler_params=_params(("parallel", "parallel")),
    )(sel, g, got)
    return t.reshape(4, hr, hc)


def _sum_partials(t, rcv, sel, name):
    _, hr, c = t.shape
    tr, tc = _tile(hr, c, 4)

    def body(sel_ref, t_ref, r_ref, o_ref):
        o_ref[...] = ((t_ref[0].astype(F32) + r_ref[0].astype(F32)) + r_ref[1].astype(F32)) + r_ref[2].astype(F32)

    return pl.pallas_call(
        body, name=name,
        grid_spec=pltpu.PrefetchScalarGridSpec(
            num_scalar_prefetch=1, grid=(hr // tr, c // tc),
            in_specs=[pl.BlockSpec((1, tr, tc), lambda i, j, sel_ref: (sel_ref[0], i, j)),
                      pl.BlockSpec((3, tr, tc), lambda i, j, sel_ref: (0, i, j))],
            out_specs=pl.BlockSpec((tr, tc), lambda i, j, sel_ref: (i, j))),
        out_shape=jax.ShapeDtypeStruct((hr, c), F32),
        compiler_params=_params(("parallel", "parallel")),
    )(sel, t, rcv)


class _GroupReduce:
    def __init__(self):
        self.names, self.gb, self.t, self.fh, self.other = [], [], [], [], []

    def put(self, name, gb):
        self.names.append(name)
        self.gb.append(gb)

    def side(self, stage):
        if stage == 1:
            return _side_swap(self.gb, ["lead" if g.ndim == 4 else "cols" for g in self.gb])
        return _side_chips(self.t) if stage == 2 else _side_swap(self.fh, ["all"] * len(self.fh))

    def done(self, stage, outs):
        x, y, cc = _me()
        if stage == 1:
            sel = jnp.reshape(cc, (1,)).astype(jnp.int32)
            self.t = [(_add_own_half if gb.ndim == 4 else _add_own_cols)(gb, got, sel, "rs_add_" + n)
                      for n, gb, got in zip(self.names, self.gb, outs)]
        elif stage == 2:
            sel = jnp.reshape(2 * x + y, (1,)).astype(jnp.int32)
            self.fh = [_sum_partials(t, rcv, sel, "rs_sum_" + n) for n, t, rcv in zip(self.names, self.t, outs)]
        else:
            self.other = list(outs)

    def result(self):
        return {n: (fh, other) for n, fh, other in zip(self.names, self.fh, self.other)}


def _allreduce_small(v, tag):
    r = v.shape[0]

    def body(v_ref, out_ref, buf_ref, send_sems, recv_sems):
        x, y, cc = _me()
        me = 4 * x + 2 * y + cc
        buf_ref[pl.ds(pl.multiple_of(me * r, 8), r), :] = v_ref[...]
        peers = [(x, y, 1 - cc)] + [(px, py, pc) for (px, py) in _chips(x, y) for pc in (cc, 1 - cc)]
        cps = []
        for k, (px, py, pc) in enumerate(peers):
            mine = buf_ref.at[pl.ds(pl.multiple_of(me * r, 8), r), :]
            cps.append(pltpu.make_async_remote_copy(
                src_ref=v_ref, dst_ref=mine, send_sem=send_sems.at[k], recv_sem=recv_sems.at[k],
                device_id=(px, py, pc), device_id_type=MESH))
        for cp in cps:
            cp.start()
        for cp in cps:
            cp.wait()
        acc = buf_ref[0:r, :]
        for d in range(1, 8):
            acc = acc + buf_ref[d * r:(d + 1) * r, :]
        out_ref[...] = acc

    vm = pl.BlockSpec(memory_space=pltpu.VMEM)
    return pl.pallas_call(
        body, name="allreduce_small_" + tag, in_specs=[vm], out_specs=vm,
        out_shape=jax.ShapeDtypeStruct((r, 128), F32),
        scratch_shapes=[pltpu.VMEM((8 * r, 128), F32), pltpu.SemaphoreType.DMA((7,)), pltpu.SemaphoreType.DMA((7,))],
    )(v)


def _adamw(w, g, m, v, name):
    r, c = w.shape
    tr = r
    c1 = 1.0 / (1.0 - ADAM_B1 ** ADAM_STEP)
    c2 = 1.0 / (1.0 - ADAM_B2 ** ADAM_STEP)

    def body(w_ref, g_ref, m_ref, v_ref, d_ref, mo_ref, vo_ref):
        gg = g_ref[...]
        mn = ADAM_B1 * m_ref[...] + (1.0 - ADAM_B1) * gg
        vn = ADAM_B2 * v_ref[...] + (1.0 - ADAM_B2) * (gg * gg)
        mo_ref[...] = mn
        vo_ref[...] = vn
        d_ref[...] = -ADAM_LR * ((mn * c1) / (jnp.sqrt(vn * c2) + ADAM_EPS) + ADAM_WD * w_ref[...])

    blk = pl.BlockSpec((tr, c), lambda i: (i, 0))
    return pl.pallas_call(
        body, name=name, grid=(r // tr,), in_specs=[blk] * 4, out_specs=[blk] * 3,
        out_shape=[jax.ShapeDtypeStruct((r, c), F32)] * 3,
        compiler_params=_params(("parallel",)),
    )(w, g, m, v)


def _adamw_halves(w, g_own, g_other, m, v, sel, name, axis):
    r, c = w.shape
    tr, tc = _tile(g_own.shape[0], g_own.shape[1], 4)
    nh = (g_own.shape[axis]) // (tr, tc)[axis]
    c1 = 1.0 / (1.0 - ADAM_B1 ** ADAM_STEP)
    c2 = 1.0 / (1.0 - ADAM_B2 ** ADAM_STEP)

    def body(sel_ref, w_ref, go_ref, gx_ref, m_ref, v_ref, d_ref, mo_ref, vo_ref, g_ref):
        mine = (pl.program_id(axis) // nh) == sel_ref[0]
        gg = jnp.where(mine, go_ref[...], gx_ref[...])
        g_ref[...] = gg
        mn = ADAM_B1 * m_ref[...] + (1.0 - ADAM_B1) * gg
        vn = ADAM_B2 * v_ref[...] + (1.0 - ADAM_B2) * (gg * gg)
        mo_ref[...] = mn
        vo_ref[...] = vn
        d_ref[...] = -ADAM_LR * ((mn * c1) / (jnp.sqrt(vn * c2) + ADAM_EPS) + ADAM_WD * w_ref[...])

    blk = pl.BlockSpec((tr, tc), lambda i, j, sel_ref: (i, j))
    if axis == 0:
        half = pl.BlockSpec((tr, tc), lambda i, j, sel_ref: (i % nh, j))
    else:
        half = pl.BlockSpec((tr, tc), lambda i, j, sel_ref: (i, j % nh))
    return pl.pallas_call(
        body, name=name,
        grid_spec=pltpu.PrefetchScalarGridSpec(
            num_scalar_prefetch=1, grid=(r // tr, c // tc),
            in_specs=[blk, half, half, blk, blk], out_specs=[blk] * 4),
        out_shape=[jax.ShapeDtypeStruct((r, c), F32)] * 4,
        compiler_params=_params(("parallel", "parallel")),
    )(sel, w, g_own, g_other, m, v)


WEIGHTS = ['norm_g', 'w_in', 'conv_w', 'w_conv_out', 'mla_q_norm_g', 'w_uq', 'mla_kv_norm_g', 'w_ukv',
           'mla_qn_nope_g', 'mla_qn_rope_g', 'mla_kn_nope_g', 'mla_kn_rope_g', 'w_mla_out', 'mem_norm_g',
           'w_mem_kv', 'mem_qn_g', 'mem_kn_g', 'w_mem_out', 'w_o']
COL_SHARDED = ['w_in', 'w_conv_out', 'w_uq', 'w_ukv', 'w_mem_out']
ROW_SHARDED = ['w_mla_out', 'w_mem_kv', 'w_o']
SMALL = ['norm_g', 'conv_w', 'mla_q_norm_g', 'mla_kv_norm_g', 'mla_qn_nope_g', 'mla_qn_rope_g', 'mla_kn_nope_g',
         'mla_kn_rope_g', 'mem_norm_g', 'mem_qn_g', 'mem_kn_g']
SMALL_SIZES = [2048, 3072, 512, 512, 128, 64, 128, 64, 2048, 256, 256]
PACK_ROWS = 72


def _full_from_shards(name, g4):
    if name in COL_SHARDED:
        return g4.transpose(1, 0, 2).reshape(g4.shape[1], 4 * g4.shape[2])
    return g4.reshape(4 * g4.shape[1], g4.shape[2])


def _w_all_t_from_w_in_t(w_in_t):
    conv, small, mz, memq, gates = (w_in_t[0:4096], w_in_t[4096:5184], w_in_t[5184:7232],
                                    w_in_t[7232:9280], w_in_t[9280:15424])
    return jnp.concatenate([gates, mz, conv, memq, small, jnp.zeros((2048 - 1088, D_MODEL), BF16)], axis=0)


def _pad_w_uq(w_uq):
    return jnp.pad(w_uq.reshape(512, MLA_HEADS, 192), ((0, 0), (0, 0), (0, 64))).reshape(512, 4096)


def _grad_halves_by_owner(name, g):
    if name == "w_all_t":
        dw_g, dw_z, dw_c, dw_m, dw_s = g
        return "w_in", jnp.concatenate([dw_c, dw_s[:1088], dw_z, dw_m, dw_g], axis=0)
    if name == "w_uq":
        g = g.reshape(512, MLA_HEADS, HEAD_PAD)[:, :, :192].reshape(512, 3072)
    r, c = g.shape
    if name in COL_SHARDED:
        return name, g.reshape(2, r // 2, 4, c // 4).transpose(0, 2, 1, 3)
    return name, g.reshape(4, 2, r // 8, c).transpose(1, 0, 2, 3)


LATE_WEIGHTS = ['w_conv_out', 'w_uq', 'w_ukv', 'w_mem_out', 'w_mla_out', 'w_mem_kv', 'w_o']
HOSTS = {"proj_fwd": ("gather", 1), "lora_fwd": ("gather", 2),
         "da_conv": ("g1", 1), "q_prep_bwd": ("g1", 2), "k_prep_bwd": ("g1", 3),
         "dw_in_0": ("g2", 1), "dw_in_2": ("g2", 2), "dw_in_4": ("g2", 3),
         "dh_a": ("g3", 1), "dh_b": ("g3", 2)}
GROUP_OF = {"w_o": "g1", "w_conv_out": "g1", "w_mla_out": "g1", "w_mem_out": "g1",
            "w_uq": "g2", "w_ukv": "g2", "w_mem_kv": "g2", "w_in": "g3"}


class _Comm:
    def __init__(self, shards, w_in_t_full, conv_full):
        self.shards = shards
        self.w = {"w_all_t": _w_all_t_from_w_in_t(w_in_t_full), "conv_w": conv_full}
        self.bufs = None
        self.groups = {"g1": _GroupReduce(), "g2": _GroupReduce(), "g3": _GroupReduce()}

    def weight(self, name):
        return self.w[name]

    def side(self, tag):
        if tag not in HOSTS:
            return None
        kind, stage = HOSTS[tag]
        if kind == "gather":
            return _side_gather_ici([self.shards[n] for n in LATE_WEIGHTS]) if stage == 1 else \
                _side_gather_pass(self.bufs)
        return self.groups[kind].side(stage)

    def done(self, tag, outs):
        kind, stage = HOSTS[tag]
        if kind != "gather":
            self.groups[kind].done(stage, outs)
        elif stage == 1:
            self.bufs = list(outs)
        else:
            for n, buf in zip(LATE_WEIGHTS, outs):
                own = self.shards[n]
                self.w[n] = _full_from_shards(n, _with_own(buf.reshape((4,) + own.shape), own))
            self.w["w_uq"] = _pad_w_uq(self.w["w_uq"])

    def put(self, name, g):
        name, gb = _grad_halves_by_owner(name, g)
        self.groups[GROUP_OF[name]].put(name, gb)

    def reduced(self):
        last = self.groups["g3"]
        last.done(3, _exchange(last.side(3), "rs_swap_w_in"))
        out = {}
        for grp in self.groups.values():
            out.update(grp.result())
        return out


def kernel(x, positions, mem, norm_g, w_in, conv_w, w_conv_out, mla_q_norm_g, w_uq, mla_kv_norm_g, w_ukv, mla_qn_nope_g, mla_qn_rope_g, mla_kn_nope_g, mla_kn_rope_g, w_mla_out, mem_norm_g, w_mem_kv, mem_qn_g, mem_kn_g, w_mem_out, w_o, loss_target, m_norm_g, m_w_in, m_conv_w, m_w_conv_out, m_mla_q_norm_g, m_w_uq, m_mla_kv_norm_g, m_w_ukv, m_mla_qn_nope_g, m_mla_qn_rope_g, m_mla_kn_nope_g, m_mla_kn_rope_g, m_w_mla_out, m_mem_norm_g, m_w_mem_kv, m_mem_qn_g, m_mem_kn_g, m_w_mem_out, m_w_o, v_norm_g, v_w_in, v_conv_w, v_w_conv_out, v_mla_q_norm_g, v_w_uq, v_mla_kv_norm_g, v_w_ukv, v_mla_qn_nope_g, v_mla_qn_rope_g, v_mla_kn_nope_g, v_mla_kn_rope_g, v_w_mla_out, v_mem_norm_g, v_w_mem_kv, v_mem_qn_g, v_mem_kn_g, v_w_mem_out, v_w_o):
    args = locals()
    w = {n: args[n][0] for n in WEIGHTS}
    m1 = {n: args["m_" + n][0] for n in WEIGHTS}
    v2 = {n: args["v_" + n][0] for n in WEIGHTS}
    xi, yi, ci = _me()
    chip = 2 * xi + yi

    conv_slot = jnp.zeros((3, 4, 256), F32).at[:, chip, :].set(w["conv_w"] * jnp.where(ci == 0, 1.0, 0.0))
    pre = jnp.zeros((PACK_ROWS * 128,), F32).at[0:3072].set(conv_slot.reshape(3072))
    conv_full = _allreduce_small(pre.reshape(PACK_ROWS, 128), "conv_w").reshape(-1)[0:3072].reshape(3, 1024)

    w_in_t = w["w_in"].T.astype(BF16)
    w_in_t_full = _put_own(_allgather_shards(w_in_t, "ag_w_in"), w_in_t, "own_w_in").reshape(4 * w_in_t.shape[0],
                                                                                               D_MODEL)
    comm = _Comm({n: w[n].astype(BF16) for n in LATE_WEIGHTS}, w_in_t_full, conv_full)
    gains = {n: w[n].reshape(1, -1) for n in SMALL if n != "conv_w"}

    loss_parts, grad_x, gsmall = _local_step(x[0], positions[0], mem[0], loss_target[0], comm, gains)

    loss_local = 0.5 * jnp.sum(loss_parts) * (1.0 / D_MODEL)
    flat = jnp.concatenate([gsmall[n].reshape(-1) for n in SMALL] + [loss_local.reshape(1)])
    flat = jnp.pad(flat, (0, PACK_ROWS * 128 - flat.shape[0]))
    tot = _allreduce_small(flat.reshape(PACK_ROWS, 128), "grads").reshape(-1)
    grads = {}
    off = 0
    for n, sz in zip(SMALL, SMALL_SIZES):
        grads[n] = tot[off:off + sz]
        off += sz
    loss = tot[off]
    grads["conv_w"] = lax.dynamic_slice(grads["conv_w"].reshape(3, 1024), (0, chip * 256), (3, 256))
    for n in SMALL:
        grads[n] = grads[n].reshape(w[n].shape)

    deltas, new_m, new_v = {}, {}, {}
    for n in SMALL:
        shp = w[n].shape
        two_d = (lambda a: a.reshape(1, -1)) if len(shp) == 1 else (lambda a: a)
        d, mn, vn = _adamw(two_d(w[n]), two_d(grads[n]), two_d(m1[n]), two_d(v2[n]), "adamw_" + n)
        deltas[n], new_m[n], new_v[n] = d.reshape(shp), mn.reshape(shp), vn.reshape(shp)

    sel_c = jnp.reshape(ci, (1,)).astype(jnp.int32)
    reduced = comm.reduced()
    for n in COL_SHARDED + ROW_SHARDED:
        g_own, g_other = reduced[n]
        if n == "w_in":
            outs = _adamw_halves(w[n].T, g_own, g_other, m1[n].T, v2[n].T, sel_c, "adamw_" + n, 1)
            deltas[n], new_m[n], new_v[n], grads[n] = [o.T for o in outs]
        else:
            deltas[n], new_m[n], new_v[n], grads[n] = _adamw_halves(w[n], g_own, g_other, m1[n], v2[n], sel_c,
                                                                     "adamw_" + n, 0)

    lead = lambda a: a[None]
    return (loss, grad_x[None], *[lead(grads[n]) for n in WEIGHTS], *[lead(deltas[n]) for n in WEIGHTS],
            *[lead(new_m[n]) for n in WEIGHTS], *[lead(new_v[n]) for n in WEIGHTS])
```

```python
import functools

import numpy as np
import jax
import jax.numpy as jnp
from jax import lax
from jax.experimental import pallas as pl
from jax.experimental.pallas import tpu as pltpu

F32 = jnp.float32
BF16 = jnp.bfloat16
MESH = pl.DeviceIdType.MESH

D_MODEL = 2048
EPS = 1e-6
CHUNK = 64
MLA_HEADS = 16
QK_NOPE = 128
QK_ROPE = 64
HEAD_PAD = 256
MEM_TOKENS = 256
MEM_HEADS = 4
MEM_HEAD_DIM = 256
ROPE_THETA = 10000.0
MLA_SCALE = (QK_NOPE + QK_ROPE) ** -0.5
MEM_SCALE = MEM_HEAD_DIM ** -0.5
LN2 = 0.6931471805599453
Q_SCALE = MLA_SCALE / LN2
NEG = -1e30

ADAM_LR = 0.001
ADAM_B1 = 0.9
ADAM_B2 = 0.999
ADAM_EPS = 1e-08
ADAM_WD = 0.01
ADAM_STEP = 10

NP = 16384
COL_G, COL_Z, COL_C, COL_M, COL_S = 0, 6144, 8192, 12288, 14336

TM_PROJ = 1024
TN_PROJ = 2048
TS_ROW = 512
TS_BWD = 512
TS_DX = 256
TQ_ATT = 1024
TK_ATT = 1024
TR_ATT = 256
FWD_HEADS_PER_STEP = 8
VMEM_LIMIT = 56 * 1024 * 1024

NT_DIMS = (((1,), (1,)), ((), ()))
TN_DIMS = (((0,), (0,)), ((), ()))


def _dot(a, b, mode="nn"):
    if mode == "nn":
        return jnp.dot(a, b, preferred_element_type=F32)
    return lax.dot_general(a, b, NT_DIMS if mode == "nt" else TN_DIMS, preferred_element_type=F32)


def _sigmoid(z):
    return 1.0 / (1.0 + jnp.exp(-z))


def _params(sem=None, vmem=VMEM_LIMIT):
    return pltpu.CompilerParams(dimension_semantics=sem, vmem_limit_bytes=vmem)


def _fold8(v):
    r, c = v.shape
    return v.reshape(r // 8, 8, c).sum(axis=0)


def _swap32(t):
    lane = lax.broadcasted_iota(jnp.int32, t.shape, 1)
    return jnp.where(lane < 32, pltpu.roll(t, 96, 1), pltpu.roll(t, 32, 1))


def _rope(t, cs, sg):
    return t * cs + _swap32(t) * sg


def _rope_bwd(d, cs, sg):
    return d * cs + _swap32(d * sg)


def _rms_fwd(xf, g, n):
    r = lax.rsqrt(jnp.sum(xf * xf, axis=-1, keepdims=True) * (1.0 / n) + EPS)
    return xf * r * g


def _rms_bwd(xf, g, dy, n):
    r = lax.rsqrt(jnp.sum(xf * xf, axis=-1, keepdims=True) * (1.0 / n) + EPS)
    xhat = xf * r
    dyg = dy * g
    dx = r * (dyg - xhat * (jnp.sum(dyg * xhat, axis=-1, keepdims=True) * (1.0 / n)))
    return dx, dy * xhat


ANY = pl.BlockSpec(memory_space=pl.ANY)


class _Side:
    def __init__(self, inputs, out_shapes, n_copies, build, aliases=()):
        self.inputs, self.out_shapes, self.n_copies = list(inputs), list(out_shapes), n_copies
        self.build, self.aliases = build, dict(aliases)


def _pcall(body, side, name, grid, in_specs, out_specs, out_shape, args, sem, scratch_shapes=(), prefetch=()):
    n_pre, n_in, n_out = len(prefetch), len(in_specs), len(out_specs)
    if side is None:
        outs = pl.pallas_call(
            body, name=name,
            grid_spec=pltpu.PrefetchScalarGridSpec(num_scalar_prefetch=n_pre, grid=grid, in_specs=list(in_specs),
                                                   out_specs=list(out_specs), scratch_shapes=list(scratch_shapes)),
            out_shape=list(out_shape), compiler_params=_params(sem),
        )(*prefetch, *args)
        return list(outs), []
    ns_in, ns_out = len(side.inputs), len(side.out_shapes)

    def wrapped(*refs):
        pre, r = refs[:n_pre], refs[n_pre:]
        ins, s_ins = r[:n_in], r[n_in:n_in + ns_in]
        outs = r[n_in + ns_in:n_in + ns_in + n_out]
        s_outs = r[n_in + ns_in + n_out:n_in + ns_in + n_out + ns_out]
        scr, send_sems, recv_sems = r[n_in + ns_in + n_out + ns_out:-2], r[-2], r[-1]
        pairs = side.build(s_ins, s_outs, send_sems, recv_sems)
        first = pl.program_id(0) == 0
        last = pl.program_id(0) == grid[0] - 1
        for d in range(1, len(grid)):
            first = first & (pl.program_id(d) == 0)
            last = last & (pl.program_id(d) == grid[d] - 1)

        @pl.when(first)
        def _():
            for send, _ in pairs:
                send.start()

        body(*pre, *ins, *outs, *scr)

        @pl.when(last)
        def _():
            for send, recv in pairs:
                send.wait_send()
                recv.wait_recv()

    outs = pl.pallas_call(
        wrapped, name=name,
        grid_spec=pltpu.PrefetchScalarGridSpec(
            num_scalar_prefetch=n_pre, grid=grid, in_specs=list(in_specs) + [ANY] * ns_in,
            out_specs=list(out_specs) + [ANY] * ns_out,
            scratch_shapes=list(scratch_shapes) + [pltpu.SemaphoreType.DMA((side.n_copies,)),
                                                   pltpu.SemaphoreType.DMA((side.n_copies,))]),
        out_shape=list(out_shape) + side.out_shapes,
        input_output_aliases={n_pre + n_in + i: n_out + o for i, o in side.aliases.items()},
        compiler_params=_params(("arbitrary",) * len(grid)),
    )(*prefetch, *args, *side.inputs)
    return list(outs[:n_out]), list(outs[n_out:])


def _matmul(a, b, mode, out_dtype, name, tm=1024, tn=1024, tk=None, side=None):
    if tk is None:
        tk = 4096 if mode == "tn" else 2048
    if mode == "nn":
        (m, k), (_, n) = a.shape, b.shape
    elif mode == "nt":
        (m, k), (n, _) = a.shape, b.shape
    else:
        (k, m), (_, n) = a.shape, b.shape
    tm, tn, tk = min(tm, m), min(tn, n), min(tk, k)
    nk = k // tk
    if mode == "tn":
        a_spec = pl.BlockSpec((tk, tm), lambda i, j, kk: (kk, i))
    else:
        a_spec = pl.BlockSpec((tm, tk), lambda i, j, kk: (i, kk))
    if mode == "nt":
        b_spec = pl.BlockSpec((tn, tk), lambda i, j, kk: (j, kk))
    else:
        b_spec = pl.BlockSpec((tk, tn), lambda i, j, kk: (kk, j))

    def body(a_ref, b_ref, o_ref, acc_ref):
        kk = pl.program_id(2)

        @pl.when(kk == 0)
        def _():
            acc_ref[...] = jnp.zeros_like(acc_ref)

        acc_ref[...] += _dot(a_ref[...].astype(BF16), b_ref[...].astype(BF16), mode)

        @pl.when(kk == nk - 1)
        def _():
            o_ref[...] = acc_ref[...].astype(out_dtype)

    def single_body(a_ref, b_ref, o_ref):
        o_ref[...] = _dot(a_ref[...].astype(BF16), b_ref[...].astype(BF16), mode).astype(out_dtype)

    outs, extra = _pcall(
        single_body if nk == 1 else body, side, name, (m // tm, n // tn, nk), [a_spec, b_spec],
        [pl.BlockSpec((tm, tn), lambda i, j, kk: (i, j))], [jax.ShapeDtypeStruct((m, n), out_dtype)], (a, b),
        ("parallel", "parallel", "arbitrary"), scratch_shapes=[] if nk == 1 else [pltpu.VMEM((tm, tn), F32)])
    return outs[0] if side is None else (outs[0], extra)


def _norm_fwd(x, g):
    s = x.shape[0]
    ts = min(TS_ROW, s)

    def body(x_ref, g_ref, h_ref):
        h_ref[...] = _rms_fwd(x_ref[...], g_ref[...], D_MODEL).astype(BF16)

    row = pl.BlockSpec((ts, D_MODEL), lambda i: (i, 0))
    return pl.pallas_call(
        body, name="norm_fwd", grid=(s // ts,),
        in_specs=[row, pl.BlockSpec((1, D_MODEL), lambda i: (0, 0))], out_specs=row,
        out_shape=jax.ShapeDtypeStruct((s, D_MODEL), BF16), compiler_params=_params(("parallel",)),
    )(x, g)


def _shift_rows(v, carry, j, ts, back):
    if back:
        main = pltpu.roll(v, j, 0)
        edge = pltpu.roll(jnp.concatenate([carry, v[:8]], axis=0), j, 0)[8:]
        return jnp.concatenate([edge, main[8:]], axis=0)
    main = pltpu.roll(v, ts - j, 0)
    edge = pltpu.roll(jnp.concatenate([v[ts - 8:], carry], axis=0), 16 - j, 0)[:8]
    return jnp.concatenate([main[:ts - 8], edge], axis=0)


def _conv_fwd(p_all, conv_w):
    s = p_all.shape[0]
    ts = min(TS_ROW, s)
    c0 = COL_C // 1024

    def body(cg_ref, bg_ref, u_ref, z_ref, w_ref, a_ref, co_ref, carry_ref):
        @pl.when(pl.program_id(0) == 0)
        def _():
            carry_ref[...] = jnp.zeros_like(carry_ref)

        p = cg_ref[...].astype(F32) * u_ref[...].astype(F32)
        carry = carry_ref[...]
        co = (w_ref[2:3, :] * p + w_ref[1:2, :] * _shift_rows(p, carry, 1, ts, True)
              + w_ref[0:1, :] * _shift_rows(p, carry, 2, ts, True))
        carry_ref[...] = p[ts - 8:]
        z = z_ref[...].astype(F32)
        a_ref[...] = (bg_ref[...].astype(F32) * co * (z * _sigmoid(z))).astype(BF16)
        co_ref[...] = co.astype(BF16)

    seg = lambda c: pl.BlockSpec((ts, 1024), lambda i: (i, c0 + c))
    return pl.pallas_call(
        body, name="conv_fwd", grid=(s // ts,),
        in_specs=[seg(0), seg(1), seg(2), seg(3), pl.BlockSpec((3, 1024), lambda i: (0, 0))],
        out_specs=[pl.BlockSpec((ts, 1024), lambda i: (i, 0))] * 2,
        out_shape=[jax.ShapeDtypeStruct((s, 1024), BF16)] * 2,
        scratch_shapes=[pltpu.VMEM((8, 1024), F32)],
        compiler_params=_params(("arbitrary",)),
    )(p_all, p_all, p_all, p_all, conv_w)


def _lora_fwd(p_all, gq, gkv, gkr, cs, sg, side=None):
    s = p_all.shape[0]
    ts = min(TS_ROW, s)

    def body(cq_ref, ckv_ref, kr_ref, gq_ref, gkv_ref, gkr_ref, cs_ref, sg_ref, cqn_ref, ckvn_ref, krope_ref):
        cqn_ref[...] = _rms_fwd(cq_ref[...].astype(F32), gq_ref[...], 512).astype(BF16)
        ckvn_ref[...] = _rms_fwd(ckv_ref[...].astype(F32), gkv_ref[...], 512).astype(BF16)
        kn = _rms_fwd(kr_ref[...].astype(F32), gkr_ref[...], QK_ROPE)
        krope_ref[...] = _rope(kn, cs_ref[...], sg_ref[...]).astype(BF16)

    row = lambda w: pl.BlockSpec((ts, w), lambda i: (i, 0))
    vec = lambda w: pl.BlockSpec((1, w), lambda i: (0, 0))
    return _pcall(
        body, side, "lora_fwd", (s // ts,),
        [pl.BlockSpec((ts, 512), lambda i: (i, COL_S // 512)),
         pl.BlockSpec((ts, 512), lambda i: (i, COL_S // 512 + 1)),
         pl.BlockSpec((ts, 128), lambda i: (i, (COL_S + 1024) // 128)),
         vec(512), vec(512), vec(128), row(128), row(128)],
        [row(512), row(512), row(128)],
        [jax.ShapeDtypeStruct((s, 512), BF16), jax.ShapeDtypeStruct((s, 512), BF16),
         jax.ShapeDtypeStruct((s, 128), BF16)],
        (p_all, p_all, p_all, gq, gkv, gkr, cs, sg), ("parallel",))


UP_HEADS = 8


def _q_up(cqn, w_uq, gn, gr, cs, sg):
    s = cqn.shape[0]
    tm, tn = min(TS_ROW, s), UP_HEADS * HEAD_PAD

    def body(c_ref, w_ref, gn_ref, gr_ref, cs_ref, sg_ref, q_ref, o_ref):
        qb = _dot(c_ref[...], w_ref[...]).astype(BF16)
        q_ref[...] = qb
        for h in range(UP_HEADS):
            q = qb[:, h * HEAD_PAD:(h + 1) * HEAD_PAD].astype(F32)
            a = _rms_fwd(q[:, :128], gn_ref[...], QK_NOPE)
            b = _rope(_rms_fwd(q[:, 128:], gr_ref[...], QK_ROPE), cs_ref[...], sg_ref[...])
            o_ref[h] = (jnp.concatenate([a, b], axis=1) * Q_SCALE).astype(BF16)

    vec = pl.BlockSpec((1, 128), lambda i, j: (0, 0))
    tab = pl.BlockSpec((tm, 128), lambda i, j: (i, 0))
    return pl.pallas_call(
        body, name="q_up", grid=(s // tm, MLA_HEADS // UP_HEADS),
        in_specs=[pl.BlockSpec((tm, 512), lambda i, j: (i, 0)), pl.BlockSpec((512, tn), lambda i, j: (0, j)),
                  vec, vec, tab, tab],
        out_specs=[pl.BlockSpec((tm, tn), lambda i, j: (i, j)),
                   pl.BlockSpec((UP_HEADS, tm, HEAD_PAD), lambda i, j: (j, i, 0))],
        out_shape=[jax.ShapeDtypeStruct((s, MLA_HEADS * HEAD_PAD), BF16),
                   jax.ShapeDtypeStruct((MLA_HEADS, s, HEAD_PAD), BF16)],
        compiler_params=_params(("parallel", "parallel")),
    )(cqn, w_uq, gn, gr, cs, sg)


def _kv_up(ckvn, w_ukv, krope, gk):
    s = ckvn.shape[0]
    tm, tn = min(TS_ROW, s), UP_HEADS * HEAD_PAD

    def body(c_ref, w_ref, kr_ref, gk_ref, kv_ref, o_ref, vt_ref):
        kvb = _dot(c_ref[...], w_ref[...]).astype(BF16)
        kv_ref[...] = kvb
        for h in range(UP_HEADS):
            a = _rms_fwd(kvb[:, h * HEAD_PAD:h * HEAD_PAD + 128].astype(F32), gk_ref[...], QK_NOPE)
            o_ref[h] = jnp.concatenate([a.astype(BF16), kr_ref[...]], axis=1)
            vt_ref[h] = kvb[:, h * HEAD_PAD + 128:(h + 1) * HEAD_PAD].astype(F32).T.astype(BF16)

    return pl.pallas_call(
        body, name="kv_up", grid=(s // tm, MLA_HEADS // UP_HEADS),
        in_specs=[pl.BlockSpec((tm, 512), lambda i, j: (i, 0)), pl.BlockSpec((512, tn), lambda i, j: (0, j)),
                  pl.BlockSpec((tm, 128), lambda i, j: (i, 0)), pl.BlockSpec((1, 128), lambda i, j: (0, 0))],
        out_specs=[pl.BlockSpec((tm, tn), lambda i, j: (i, j)),
                   pl.BlockSpec((UP_HEADS, tm, HEAD_PAD), lambda i, j: (j, i, 0)),
                   pl.BlockSpec((UP_HEADS, 128, tm), lambda i, j: (j, 0, i))],
        out_shape=[jax.ShapeDtypeStruct((s, MLA_HEADS * HEAD_PAD), BF16),
                   jax.ShapeDtypeStruct((MLA_HEADS, s, HEAD_PAD), BF16),
                   jax.ShapeDtypeStruct((MLA_HEADS, 128, s), BF16)],
        compiler_params=_params(("parallel", "parallel")),
    )(ckvn, w_ukv, krope, gk)


def _chunk_mask(tq, tk, row0=0):
    r = (lax.broadcasted_iota(jnp.int32, (tq, tk), 0) + row0) // CHUNK
    c = lax.broadcasted_iota(jnp.int32, (tq, tk), 1) // CHUNK
    return c <= r


def _causal_pairs(n, by_key):
    if by_key:
        pairs = [(i, j) for j in range(n) for i in range(j, n)]
    else:
        pairs = [(i, j) for i in range(n) for j in range(i + 1)]
    return (jnp.asarray([p[0] for p in pairs], jnp.int32), jnp.asarray([p[1] for p in pairs], jnp.int32))


def _flash_fwd(qcat, kcat, vt, p_all):
    s = qcat.shape[1]
    tq = tk = min(TQ_ATT, s)
    tc = min(TR_ATT, tq)
    nq = s // tq
    hg = FWD_HEADS_PER_STEP
    zc = COL_Z // (128 * hg)
    qi, kj = _causal_pairs(nq, False)

    def body(qi_ref, kj_ref, q_ref, k_ref, vt_ref, z_ref, y_ref, a_ref, lse_ref, m_sc, l_sc, acc_sc):
        t = pl.program_id(1)
        i, j = qi_ref[t], kj_ref[t]

        @pl.when(j == 0)
        def _():
            m_sc[...] = jnp.full_like(m_sc, NEG)
            l_sc[...] = jnp.zeros_like(l_sc)
            acc_sc[...] = jnp.zeros_like(acc_sc)

        def step(masked):
            nc = tq // tc
            state = {(g, c): [m_sc[g, :, c * tc:(c + 1) * tc], l_sc[g, :, c * tc:(c + 1) * tc],
                              acc_sc[g, :, c * tc:(c + 1) * tc]] for g in range(hg) for c in range(nc)}
            units = [(g, u, c) for u in range(tk // tc) for c in range(nc) for g in range(hg)
                     if (u <= c or not masked)]

            def scores(g, u, c):
                return _dot(k_ref[g, u * tc:(u + 1) * tc, :], q_ref[g, c * tc:(c + 1) * tc, :], "nt")

            def weighted_values(g, u, c, alpha, pb):
                state[g, c][2] = alpha * state[g, c][2] + _dot(vt_ref[g, :, u * tc:(u + 1) * tc], pb)

            ahead = 4
            pending = [scores(*un) for un in units[:ahead]]
            late = None
            for n, (g, u, c) in enumerate(units):
                st = pending.pop(0)
                if n + ahead < len(units):
                    pending.append(scores(*units[n + ahead]))
                if masked and u == c:
                    kc = lax.broadcasted_iota(jnp.int32, (tc, tc), 0) // CHUNK
                    qc = lax.broadcasted_iota(jnp.int32, (tc, tc), 1) // CHUNK
                    st = jnp.where(kc <= qc, st, NEG)
                m_run, l_run, _ = state[g, c]
                m_new = jnp.maximum(m_run, jnp.max(st, axis=0, keepdims=True))
                alpha = jnp.exp2(m_run - m_new)
                p = jnp.exp2(st - m_new)
                state[g, c][0] = m_new
                state[g, c][1] = alpha * l_run + jnp.sum(p, axis=0, keepdims=True)
                if late is not None:
                    weighted_values(*late)
                late = (g, u, c, alpha, p.astype(BF16))
            weighted_values(*late)
            for g in range(hg):
                m_sc[g] = jnp.concatenate([state[g, c][0] for c in range(nc)], axis=1)
                l_sc[g] = jnp.concatenate([state[g, c][1] for c in range(nc)], axis=1)
                acc_sc[g] = jnp.concatenate([state[g, c][2] for c in range(nc)], axis=1)

        @pl.when(j < i)
        def _():
            step(False)

        @pl.when(j == i)
        def _():
            step(True)
            for g in range(hg):
                cols = slice(g * 128, (g + 1) * 128)
                y = (acc_sc[g] * (1.0 / l_sc[g])).T
                z = z_ref[:, cols].astype(F32)
                y_ref[:, cols] = y.astype(BF16)
                a_ref[:, cols] = (y * (z * _sigmoid(z))).astype(BF16)
                lse2 = m_sc[g] + jnp.log(l_sc[g]) * (1.0 / LN2)
                lse_ref[g] = jnp.broadcast_to(lse2, (128, tq)).T

    wide = pl.BlockSpec((tq, 128 * hg), lambda h, t, qi_ref, kj_ref: (qi_ref[t], h))
    return pl.pallas_call(
        body, name="flash_fwd",
        grid_spec=pltpu.PrefetchScalarGridSpec(
            num_scalar_prefetch=2, grid=(MLA_HEADS // hg, qi.shape[0]),
            in_specs=[pl.BlockSpec((hg, tq, HEAD_PAD), lambda h, t, qi_ref, kj_ref: (h, qi_ref[t], 0)),
                      pl.BlockSpec((hg, tk, HEAD_PAD), lambda h, t, qi_ref, kj_ref: (h, kj_ref[t], 0)),
                      pl.BlockSpec((hg, 128, tk), lambda h, t, qi_ref, kj_ref: (h, 0, kj_ref[t])),
                      pl.BlockSpec((tq, 128 * hg), lambda h, t, qi_ref, kj_ref: (qi_ref[t], zc + h))],
            out_specs=[wide, wide,
                       pl.BlockSpec((hg, tq, 128), lambda h, t, qi_ref, kj_ref: (h, qi_ref[t], 0))],
            scratch_shapes=[pltpu.VMEM((hg, 1, tq), F32), pltpu.VMEM((hg, 1, tq), F32),
                            pltpu.VMEM((hg, 128, tq), F32)]),
        out_shape=[jax.ShapeDtypeStruct((s, MLA_HEADS * 128), BF16),
                   jax.ShapeDtypeStruct((s, MLA_HEADS * 128), BF16),
                   jax.ShapeDtypeStruct((MLA_HEADS, s, 128), F32)],
        compiler_params=_params(("parallel", "arbitrary")),
    )(qi, kj, qcat, kcat, vt, p_all)


def _mem_kv_fwd(mem, g_norm, w_kv, g_kn):
    m = mem.shape[0]

    def body(mem_ref, g_ref, w_ref, gk_ref, memn_ref, kraw_ref, kn_ref, v_ref):
        memn = _rms_fwd(mem_ref[...], g_ref[...], D_MODEL).astype(BF16)
        memn_ref[...] = memn
        kvm = _dot(memn, w_ref[...])
        kraw_ref[...] = kvm[:, :1024]
        v_ref[...] = kvm[:, 1024:].astype(BF16)
        for hh in range(MEM_HEADS):
            sl = slice(hh * MEM_HEAD_DIM, (hh + 1) * MEM_HEAD_DIM)
            kn_ref[:, sl] = _rms_fwd(kvm[:, sl], gk_ref[...], MEM_HEAD_DIM).astype(BF16)

    return pl.pallas_call(
        body, name="mem_kv_fwd",
        out_shape=[jax.ShapeDtypeStruct((m, D_MODEL), BF16), jax.ShapeDtypeStruct((m, 1024), F32),
                   jax.ShapeDtypeStruct((m, 1024), BF16), jax.ShapeDtypeStruct((m, 1024), BF16)],
        compiler_params=_params(),
    )(mem, g_norm, w_kv, g_kn)


def _mem_attn_fwd(p_all, kn, v, gq):
    s = p_all.shape[0]
    tq = min(TS_ROW, s)

    def body(q_ref, z_ref, kn_ref, v_ref, gq_ref, a_ref):
        z = z_ref[...].astype(F32)
        gate = z * _sigmoid(z)
        for hh in range(MEM_HEADS):
            sl = slice(hh * MEM_HEAD_DIM, (hh + 1) * MEM_HEAD_DIM)
            qn = _rms_fwd(q_ref[:, sl].astype(F32), gq_ref[...], MEM_HEAD_DIM).astype(BF16)
            sc = _dot(qn, kn_ref[:, sl], "nt") * MEM_SCALE
            e = jnp.exp(sc - jnp.max(sc, axis=-1, keepdims=True))
            p = e * (1.0 / jnp.sum(e, axis=-1, keepdims=True))
            y = _dot(p.astype(BF16), v_ref[:, sl])
            a_ref[:, sl] = (y * gate[:, sl]).astype(BF16)

    full = lambda a: pl.BlockSpec(a.shape, lambda i: (0, 0))
    return pl.pallas_call(
        body, name="mem_attn_fwd", grid=(s // tq,),
        in_specs=[pl.BlockSpec((tq, 1024), lambda i: (i, COL_M // 1024)),
                  pl.BlockSpec((tq, 1024), lambda i: (i, COL_M // 1024 + 1)),
                  full(kn), full(v), full(gq)],
        out_specs=pl.BlockSpec((tq, 1024), lambda i: (i, 0)),
        out_shape=jax.ShapeDtypeStruct((s, 1024), BF16),
        compiler_params=_params(("parallel",)),
    )(p_all, p_all, kn, v, gq)


def _merge_fwd(a_conv, w_c, a_mla, w_m, a_mem, w_e, p_all):
    s = a_conv.shape[0]
    tm, tn = min(TS_ROW, s), 1024
    nj = D_MODEL // tn

    def body(ac_ref, wc_ref, am_ref, wm_ref, ae_ref, we_ref, gc_ref, gm_ref, ge_ref,
             oc_ref, om_ref, oe_ref, mg_ref):
        oc = _dot(ac_ref[...], wc_ref[...])
        om = _dot(am_ref[...], wm_ref[...])
        oe = _dot(ae_ref[...], we_ref[...])
        oc_ref[...] = oc.astype(BF16)
        om_ref[...] = om.astype(BF16)
        oe_ref[...] = oe.astype(BF16)
        mg_ref[...] = (_sigmoid(gc_ref[...].astype(F32)) * oc + _sigmoid(gm_ref[...].astype(F32)) * om
                       + _sigmoid(ge_ref[...].astype(F32)) * oe).astype(BF16)

    act = lambda k: pl.BlockSpec((tm, k), lambda i, j: (i, 0))
    wgt = lambda k: pl.BlockSpec((k, tn), lambda i, j: (0, j))
    gate = lambda b: pl.BlockSpec((tm, tn), lambda i, j: (i, b * nj + j))
    out = pl.BlockSpec((tm, tn), lambda i, j: (i, j))
    return pl.pallas_call(
        body, name="merge_fwd", grid=(s // tm, nj),
        in_specs=[act(1024), wgt(1024), act(2048), wgt(2048), act(1024), wgt(1024), gate(0), gate(1), gate(2)],
        out_specs=[out] * 4,
        out_shape=[jax.ShapeDtypeStruct((s, D_MODEL), BF16)] * 4,
        compiler_params=_params(("parallel", "parallel")),
    )(a_conv, w_c, a_mla, w_m, a_mem, w_e, p_all, p_all, p_all)


def _out_fwd(merged, w_o, x, target):
    s = merged.shape[0]
    tm, tn = min(TS_ROW, s), 1024
    nj = D_MODEL // tn

    def body(mg_ref, w_ref, x_ref, t_ref, dy_ref, dyb_ref, ls_ref):
        e = x_ref[...] + _dot(mg_ref[...], w_ref[...]) - t_ref[...]
        dy = e * (1.0 / D_MODEL)
        dy_ref[...] = dy
        dyb_ref[...] = dy.astype(BF16)
        r = _fold8(e * e)
        acc = r[:, 0:128]
        for cc in range(1, tn // 128):
            acc = acc + r[:, cc * 128:(cc + 1) * 128]
        ls_ref[...] = acc

    tile = pl.BlockSpec((tm, tn), lambda i, j: (i, j))
    return pl.pallas_call(
        body, name="out_fwd", grid=(s // tm, nj),
        in_specs=[pl.BlockSpec((tm, D_MODEL), lambda i, j: (i, 0)),
                  pl.BlockSpec((D_MODEL, tn), lambda i, j: (0, j)), tile, tile],
        out_specs=[tile, tile, pl.BlockSpec((8, 128), lambda i, j: (i, j))],
        out_shape=[jax.ShapeDtypeStruct((s, D_MODEL), F32), jax.ShapeDtypeStruct((s, D_MODEL), BF16),
                   jax.ShapeDtypeStruct((s // tm * 8, nj * 128), F32)],
        compiler_params=_params(("parallel", "parallel")),
    )(merged, w_o, x, target)


def _merge_bwd(dyb, w_o, p_all, o_c, o_m, o_e):
    s = dyb.shape[0]
    tm = min(256, s)

    def body(dy_ref, w_ref, g_ref, oc_ref, om_ref, oe_ref, dc_ref, dm_ref, de_ref, dg_ref):
        dmg = _dot(dy_ref[...], w_ref[...], "nt")
        for b, (o_ref, d_ref) in enumerate(((oc_ref, dc_ref), (om_ref, dm_ref), (oe_ref, de_ref))):
            sl = slice(b * D_MODEL, (b + 1) * D_MODEL)
            sg = _sigmoid(g_ref[:, sl].astype(F32))
            d_ref[...] = (dmg * sg).astype(BF16)
            dg_ref[:, sl] = (dmg * o_ref[...].astype(F32) * sg * (1.0 - sg)).astype(BF16)

    row = pl.BlockSpec((tm, D_MODEL), lambda i: (i, 0))
    wide = pl.BlockSpec((tm, 3 * D_MODEL), lambda i: (i, 0))
    return pl.pallas_call(
        body, name="merge_bwd", grid=(s // tm,),
        in_specs=[row, pl.BlockSpec((D_MODEL, D_MODEL), lambda i: (0, 0)), wide, row, row, row],
        out_specs=[row, row, row, wide],
        out_shape=[jax.ShapeDtypeStruct((s, D_MODEL), BF16)] * 3 + [jax.ShapeDtypeStruct((s, 3 * D_MODEL), BF16)],
        compiler_params=_params(("parallel",)),
    )(dyb, w_o, p_all, o_c, o_m, o_e)


def _conv_bwd(da, p_all, co, conv_w):
    s = da.shape[0]
    ts = min(TS_ROW, s)
    n = s // ts
    c0 = COL_C // 1024

    def body(da_ref, cg_ref, bg_ref, u_ref, z_ref, co_ref, w_ref, dp_ref, dw_ref, carry_ref):
        @pl.when(pl.program_id(0) == 0)
        def _():
            carry_ref[...] = jnp.zeros_like(carry_ref)
            dw_ref[...] = jnp.zeros_like(dw_ref)

        da_ = da_ref[...].astype(F32)
        cg, bg = cg_ref[...].astype(F32), bg_ref[...].astype(F32)
        u, z, cov = u_ref[...].astype(F32), z_ref[...].astype(F32), co_ref[...].astype(F32)
        sz = _sigmoid(z)
        dyc = da_ * (z * sz)
        dz = da_ * (bg * cov) * (sz * (1.0 + z * (1.0 - sz)))
        db = dyc * cov
        dco = dyc * bg
        carry = carry_ref[...]
        d1 = _shift_rows(dco, carry, 1, ts, False)
        d2 = _shift_rows(dco, carry, 2, ts, False)
        carry_ref[...] = dco[:8]
        dpp = w_ref[2:3, :] * dco + w_ref[1:2, :] * d1 + w_ref[0:1, :] * d2
        p = cg * u
        dw_ref[0] += _fold8(p * d2)
        dw_ref[1] += _fold8(p * d1)
        dw_ref[2] += _fold8(p * dco)
        dp_ref[...] = jnp.concatenate([dpp * u, db, dpp * cg, dz], axis=1).astype(BF16)

    rev = lambda c: pl.BlockSpec((ts, 1024), lambda i: (n - 1 - i, c))
    return pl.pallas_call(
        body, name="conv_bwd", grid=(n,),
        in_specs=[rev(0), rev(c0), rev(c0 + 1), rev(c0 + 2), rev(c0 + 3), rev(0),
                  pl.BlockSpec((3, 1024), lambda i: (0, 0))],
        out_specs=[pl.BlockSpec((ts, 4096), lambda i: (n - 1 - i, 0)),
                   pl.BlockSpec((3, 8, 1024), lambda i: (0, 0, 0))],
        out_shape=[jax.ShapeDtypeStruct((s, 4096), BF16), jax.ShapeDtypeStruct((3, 8, 1024), F32)],
        scratch_shapes=[pltpu.VMEM((8, 1024), F32)],
        compiler_params=_params(("arbitrary",)),
    )(da, p_all, p_all, p_all, p_all, co, conv_w)


def _mla_gate_bwd(do_m, w_mla_out, y, p_all):
    s = do_m.shape[0]
    tm, tn = min(TS_ROW, s), 1024
    heads = tn // 128

    def body(do_ref, w_ref, y_ref, z_ref, dy_ref, dz_ref, dl_ref):
        da_ = _dot(do_ref[...], w_ref[...], "nt").astype(BF16).astype(F32)
        yv, z = y_ref[...].astype(F32), z_ref[...].astype(F32)
        sz = _sigmoid(z)
        dyv = da_ * (z * sz)
        dy_ref[...] = dyv.astype(BF16)
        dz_ref[...] = (da_ * yv * (sz * (1.0 + z * (1.0 - sz)))).astype(BF16)
        pr = dyv * yv
        for h in range(heads):
            dl_ref[h] = jnp.broadcast_to(jnp.sum(pr[:, h * 128:(h + 1) * 128], axis=-1, keepdims=True), (tm, 128))

    tile = pl.BlockSpec((tm, tn), lambda i, j: (i, j))
    return pl.pallas_call(
        body, name="mla_gate_bwd", grid=(s // tm, D_MODEL // tn),
        in_specs=[pl.BlockSpec((tm, D_MODEL), lambda i, j: (i, 0)), pl.BlockSpec((tn, D_MODEL), lambda i, j: (j, 0)),
                  tile, pl.BlockSpec((tm, tn), lambda i, j: (i, COL_Z // tn + j))],
        out_specs=[tile, tile, pl.BlockSpec((heads, tm, 128), lambda i, j: (j, i, 0))],
        out_shape=[jax.ShapeDtypeStruct((s, D_MODEL), BF16)] * 2 + [jax.ShapeDtypeStruct((MLA_HEADS, s, 128), F32)],
        compiler_params=_params(("parallel", "parallel")),
    )(do_m, w_mla_out, y, p_all)


def _flash_bwd(qcat, kcat, kv, dy, lse, delta):
    s = qcat.shape[1]
    tq = tk = min(TQ_ATT, s)
    tr = min(TR_ATT, tq)
    nq = s // tq
    qi, kj = _causal_pairs(nq, True)
    npairs = qi.shape[0]

    def body(qi_ref, kj_ref, q_ref, k_ref, v_ref, do_ref, lse_ref, dl_ref, dq_ref, dk_ref, dv_ref,
             dq_acc, dk_acc, dv_acc):
        t = pl.program_id(1)
        i, j = qi_ref[t], kj_ref[t]

        def add_dq(rows, dq_new):
            @pl.when(j == 0)
            def _():
                dq_acc[rows, :] = dq_new

            @pl.when(j > 0)
            def _():
                dq_acc[rows, :] += dq_new

        def full_step():
            q, k = q_ref[0], k_ref[0]
            p = jnp.exp2(_dot(q, k, "nt") - lse_ref[0][:, 0:1])
            do = do_ref[...]
            dpv = _dot(do, v_ref[...], "nt")
            ds = (p * (dpv - dl_ref[0][:, 0:1])).astype(BF16)
            dv_acc[...] += _dot(p.astype(BF16), do, "tn")
            dk_acc[...] += _dot(ds, q, "tn")
            add_dq(pl.ds(pl.multiple_of(i * tq, tq), tq), _dot(ds, k))

        def diagonal_step():
            def keys(r):
                return (r + 1) * tr

            def scores(r):
                rows = slice(r * tr, (r + 1) * tr)
                return (_dot(q_ref[0, rows, :], k_ref[0, :keys(r), :], "nt"),
                        _dot(do_ref[rows, :], v_ref[:keys(r), :], "nt"))

            def gradients(n, r, pb, ds):
                nk = keys(r)
                rows = slice(r * tr, (r + 1) * tr)
                dv_new = _dot(pb, do_ref[rows, :], "tn")
                dk_new = _dot(ds, q_ref[0, rows, :], "tn")
                if n == 0:
                    dv_acc[...] = dv_new
                    dk_acc[...] = dk_new
                else:
                    dv_acc[:nk, :] += dv_new
                    dk_acc[:nk, :] += dk_new
                add_dq(pl.ds(pl.multiple_of(i * tq + r * tr, tr), tr), _dot(ds, k_ref[0, :nk, :]))

            order = list(range(tq // tr - 1, -1, -1))
            pending = [scores(r) for r in order[:2]]
            late = None
            for n, r in enumerate(order):
                sc, dpv = pending.pop(0)
                if n + 2 < len(order):
                    pending.append(scores(order[n + 2]))
                rows = slice(r * tr, (r + 1) * tr)
                sc = jnp.where(_chunk_mask(tr, keys(r), r * tr), sc, NEG)
                p = jnp.exp2(sc - lse_ref[0][rows, 0:1])
                ds = (p * (dpv - dl_ref[0][rows, 0:1])).astype(BF16)
                if late is not None:
                    gradients(*late)
                late = (n, r, p.astype(BF16), ds)
            gradients(*late)

        @pl.when(i == j)
        def _():
            diagonal_step()

        @pl.when(i > j)
        def _():
            full_step()

        @pl.when(i == nq - 1)
        def _():
            dk_ref[0] = dk_acc[...].astype(BF16)
            dv_ref[0] = dv_acc[...].astype(BF16)

        @pl.when(t == npairs - 1)
        def _():
            dq_ref[0] = dq_acc[...].astype(BF16)

    qrow = lambda w: pl.BlockSpec((1, tq, w), lambda h, t, qi_ref, kj_ref: (h, qi_ref[t], 0))
    krow = lambda w: pl.BlockSpec((1, tk, w), lambda h, t, qi_ref, kj_ref: (h, kj_ref[t], 0))
    return pl.pallas_call(
        body, name="flash_bwd",
        grid_spec=pltpu.PrefetchScalarGridSpec(
            num_scalar_prefetch=2, grid=(MLA_HEADS, npairs),
            in_specs=[qrow(HEAD_PAD), krow(HEAD_PAD),
                      pl.BlockSpec((tk, 128), lambda h, t, qi_ref, kj_ref: (kj_ref[t], 2 * h + 1)),
                      pl.BlockSpec((tq, 128), lambda h, t, qi_ref, kj_ref: (qi_ref[t], h)),
                      qrow(128), qrow(128)],
            out_specs=[pl.BlockSpec((1, s, HEAD_PAD), lambda h, t, qi_ref, kj_ref: (h, 0, 0)),
                       krow(HEAD_PAD), krow(128)],
            scratch_shapes=[pltpu.VMEM((s, HEAD_PAD), F32), pltpu.VMEM((tk, HEAD_PAD), F32),
                            pltpu.VMEM((tk, 128), F32)]),
        out_shape=[jax.ShapeDtypeStruct((MLA_HEADS, s, HEAD_PAD), BF16),
                   jax.ShapeDtypeStruct((MLA_HEADS, s, HEAD_PAD), BF16),
                   jax.ShapeDtypeStruct((MLA_HEADS, s, 128), BF16)],
        compiler_params=_params(("parallel", "arbitrary")),
    )(qi, kj, qcat, kcat, kv, dy, lse, delta)


def _q_prep_bwd(dqcat, q_raw, gn, gr, cs, sg, side=None):
    s = q_raw.shape[0]
    ts = min(TS_BWD, s)

    def body(dq_ref, q_ref, gn_ref, gr_ref, cs_ref, sg_ref, o_ref, dg_ref):
        @pl.when(pl.program_id(0) == 0)
        def _():
            dg_ref[...] = jnp.zeros_like(dg_ref)

        ga_acc = jnp.zeros((8, 128), F32)
        gb_acc = jnp.zeros((8, 128), F32)
        for h in range(MLA_HEADS):
            cols = slice(h * HEAD_PAD, (h + 1) * HEAD_PAD)
            q = q_ref[:, cols].astype(F32)
            d = dq_ref[h].astype(F32) * MLA_SCALE
            da, ga = _rms_bwd(q[:, :128], gn_ref[...], d[:, :128], QK_NOPE)
            db, gb = _rms_bwd(q[:, 128:], gr_ref[...], _rope_bwd(d[:, 128:], cs_ref[...], sg_ref[...]), QK_ROPE)
            o_ref[:, cols] = jnp.concatenate([da, db], axis=1).astype(BF16)
            ga_acc = ga_acc + _fold8(ga)
            gb_acc = gb_acc + _fold8(gb)
        dg_ref[0] += ga_acc
        dg_ref[1] += gb_acc

    wide = pl.BlockSpec((ts, MLA_HEADS * HEAD_PAD), lambda i: (i, 0))
    return _pcall(
        body, side, "q_prep_bwd", (s // ts,),
        [pl.BlockSpec((MLA_HEADS, ts, HEAD_PAD), lambda i: (0, i, 0)), wide,
         pl.BlockSpec((1, 128), lambda i: (0, 0)), pl.BlockSpec((1, 128), lambda i: (0, 0)),
         pl.BlockSpec((ts, 128), lambda i: (i, 0)), pl.BlockSpec((ts, 128), lambda i: (i, 0))],
        [wide, pl.BlockSpec((2, 8, 128), lambda i: (0, 0, 0))],
        [jax.ShapeDtypeStruct((s, MLA_HEADS * HEAD_PAD), BF16), jax.ShapeDtypeStruct((2, 8, 128), F32)],
        (dqcat, q_raw, gn, gr, cs, sg), ("arbitrary",))


def _k_prep_bwd(dkcat, dv, kv, gk, side=None):
    s = kv.shape[0]
    ts = min(TS_BWD, s)

    def body(dk_ref, dv_ref, k_ref, gk_ref, o_ref, dkr_ref, dg_ref):
        @pl.when(pl.program_id(0) == 0)
        def _():
            dg_ref[...] = jnp.zeros_like(dg_ref)

        g_acc = jnp.zeros((8, 128), F32)
        dkr = jnp.zeros((ts, 128), F32)
        for h in range(MLA_HEADS):
            d = dk_ref[h].astype(F32) * LN2
            k = k_ref[:, h * HEAD_PAD:h * HEAD_PAD + 128].astype(F32)
            dk_raw, gg = _rms_bwd(k, gk_ref[...], d[:, :128], QK_NOPE)
            o_ref[:, h * HEAD_PAD:(h + 1) * HEAD_PAD] = jnp.concatenate([dk_raw.astype(BF16), dv_ref[h]], axis=1)
            g_acc = g_acc + _fold8(gg)
            dkr = dkr + d[:, 128:]
        dkr_ref[...] = dkr
        dg_ref[...] += g_acc

    wide = pl.BlockSpec((ts, MLA_HEADS * HEAD_PAD), lambda i: (i, 0))
    return _pcall(
        body, side, "k_prep_bwd", (s // ts,),
        [pl.BlockSpec((MLA_HEADS, ts, HEAD_PAD), lambda i: (0, i, 0)),
         pl.BlockSpec((MLA_HEADS, ts, 128), lambda i: (0, i, 0)), wide,
         pl.BlockSpec((1, 128), lambda i: (0, 0))],
        [wide, pl.BlockSpec((ts, 128), lambda i: (i, 0)), pl.BlockSpec((8, 128), lambda i: (0, 0))],
        [jax.ShapeDtypeStruct((s, MLA_HEADS * HEAD_PAD), BF16), jax.ShapeDtypeStruct((s, 128), F32),
         jax.ShapeDtypeStruct((8, 128), F32)],
        (dkcat, dv, kv, gk), ("arbitrary",))


def _lora_bwd(dcqn, dckvn, dkr, p_all, gq, gkv, gkr, cs, sg):
    s = p_all.shape[0]
    ts = min(TS_ROW, s)

    def body(dq_ref, dkv_ref, dkr_ref, cq_ref, ckv_ref, kr_ref, gq_ref, gkv_ref, gkr_ref, cs_ref, sg_ref,
             o_ref, dgq_ref, dgkv_ref, dgkr_ref):
        @pl.when(pl.program_id(0) == 0)
        def _():
            dgq_ref[...] = jnp.zeros_like(dgq_ref)
            dgkv_ref[...] = jnp.zeros_like(dgkv_ref)
            dgkr_ref[...] = jnp.zeros_like(dgkr_ref)

        dq, g1 = _rms_bwd(cq_ref[...].astype(F32), gq_ref[...], dq_ref[...], 512)
        dkv, g2 = _rms_bwd(ckv_ref[...].astype(F32), gkv_ref[...], dkv_ref[...], 512)
        dkn = _rope_bwd(dkr_ref[...], cs_ref[...], sg_ref[...])
        dk, g3 = _rms_bwd(kr_ref[...].astype(F32), gkr_ref[...], dkn, QK_ROPE)
        o_ref[...] = jnp.concatenate([dq, dkv, dk, jnp.zeros((ts, 2048 - 1152), F32)], axis=1).astype(BF16)
        dgq_ref[...] += _fold8(g1)
        dgkv_ref[...] += _fold8(g2)
        dgkr_ref[...] += _fold8(g3)

    row = lambda w: pl.BlockSpec((ts, w), lambda i: (i, 0))
    vec = lambda w: pl.BlockSpec((1, w), lambda i: (0, 0))
    acc = lambda w: pl.BlockSpec((8, w), lambda i: (0, 0))
    return pl.pallas_call(
        body, name="lora_bwd", grid=(s // ts,),
        in_specs=[row(512), row(512), row(128),
                  pl.BlockSpec((ts, 512), lambda i: (i, COL_S // 512)),
                  pl.BlockSpec((ts, 512), lambda i: (i, COL_S // 512 + 1)),
                  pl.BlockSpec((ts, 128), lambda i: (i, (COL_S + 1024) // 128)),
                  vec(512), vec(512), vec(128), row(128), row(128)],
        out_specs=[row(2048), acc(512), acc(512), acc(128)],
        out_shape=[jax.ShapeDtypeStruct((s, 2048), BF16), jax.ShapeDtypeStruct((8, 512), F32),
                   jax.ShapeDtypeStruct((8, 512), F32), jax.ShapeDtypeStruct((8, 128), F32)],
        compiler_params=_params(("arbitrary",)),
    )(dcqn, dckvn, dkr, p_all, p_all, p_all, gq, gkv, gkr, cs, sg)


def _mem_attn_bwd(da, p_all, kn, v, gq):
    s = p_all.shape[0]
    tq = min(TS_ROW, s)

    def body(da_ref, q_ref, z_ref, kn_ref, v_ref, gq_ref, o_ref, dkn_ref, dv_ref, dg_ref):
        @pl.when(pl.program_id(0) == 0)
        def _():
            dkn_ref[...] = jnp.zeros_like(dkn_ref)
            dv_ref[...] = jnp.zeros_like(dv_ref)
            dg_ref[...] = jnp.zeros_like(dg_ref)

        z = z_ref[...].astype(F32)
        da_ = da_ref[...].astype(F32)
        sz = _sigmoid(z)
        gate = z * sz
        dgate = sz * (1.0 + z * (1.0 - sz))
        for hh in range(MEM_HEADS):
            sl = slice(hh * MEM_HEAD_DIM, (hh + 1) * MEM_HEAD_DIM)
            qf = q_ref[:, sl].astype(F32)
            qn = _rms_fwd(qf, gq_ref[...], MEM_HEAD_DIM).astype(BF16)
            knh, vh = kn_ref[:, sl], v_ref[:, sl]
            sc = _dot(qn, knh, "nt") * MEM_SCALE
            e = jnp.exp(sc - jnp.max(sc, axis=-1, keepdims=True))
            p = e * (1.0 / jnp.sum(e, axis=-1, keepdims=True))
            pb = p.astype(BF16)
            y = _dot(pb, vh)
            dyh = da_[:, sl] * gate[:, sl]
            o_ref[:, 1024 + hh * MEM_HEAD_DIM:1024 + (hh + 1) * MEM_HEAD_DIM] = (
                da_[:, sl] * y * dgate[:, sl]).astype(BF16)
            dyb = dyh.astype(BF16)
            dpm = _dot(dyb, vh, "nt")
            ds = (p * (dpm - jnp.sum(dpm * p, axis=-1, keepdims=True)) * MEM_SCALE).astype(BF16)
            dqn = _dot(ds, knh)
            dkn_ref[:, sl] += _dot(ds, qn, "tn")
            dv_ref[:, sl] += _dot(pb, dyb, "tn")
            dq, gg = _rms_bwd(qf, gq_ref[...], dqn, MEM_HEAD_DIM)
            o_ref[:, sl] = dq.astype(BF16)
            dg_ref[...] += _fold8(gg)

    full = lambda a: pl.BlockSpec(a.shape, lambda i: (0, 0))
    return pl.pallas_call(
        body, name="mem_attn_bwd", grid=(s // tq,),
        in_specs=[pl.BlockSpec((tq, 1024), lambda i: (i, 0)),
                  pl.BlockSpec((tq, 1024), lambda i: (i, COL_M // 1024)),
                  pl.BlockSpec((tq, 1024), lambda i: (i, COL_M // 1024 + 1)),
                  full(kn), full(v), full(gq)],
        out_specs=[pl.BlockSpec((tq, 2048), lambda i: (i, 0)),
                   pl.BlockSpec((MEM_TOKENS, 1024), lambda i: (0, 0)),
                   pl.BlockSpec((MEM_TOKENS, 1024), lambda i: (0, 0)),
                   pl.BlockSpec((8, MEM_HEAD_DIM), lambda i: (0, 0))],
        out_shape=[jax.ShapeDtypeStruct((s, 2048), BF16), jax.ShapeDtypeStruct((MEM_TOKENS, 1024), F32),
                   jax.ShapeDtypeStruct((MEM_TOKENS, 1024), F32), jax.ShapeDtypeStruct((8, MEM_HEAD_DIM), F32)],
        compiler_params=_params(("arbitrary",)),
    )(da, p_all, p_all, kn, v, gq)


def _mem_kv_bwd(dkn, dv, kraw, memn, mem, g_norm, w_kv, g_kn):
    m = mem.shape[0]

    def body(dkn_ref, dv_ref, kraw_ref, memn_ref, mem_ref, g_ref, w_ref, gk_ref, dw_ref, dgn_ref, dgk_ref, dkv_sc):
        gk_acc = jnp.zeros((8, MEM_HEAD_DIM), F32)
        for hh in range(MEM_HEADS):
            sl = slice(hh * MEM_HEAD_DIM, (hh + 1) * MEM_HEAD_DIM)
            dk, gg = _rms_bwd(kraw_ref[:, sl], gk_ref[...], dkn_ref[:, sl], MEM_HEAD_DIM)
            dkv_sc[:, sl] = dk.astype(BF16)
            gk_acc = gk_acc + _fold8(gg)
        dgk_ref[...] = gk_acc
        dkv_sc[:, 1024:] = dv_ref[...].astype(BF16)
        dkv = dkv_sc[...]
        dw_ref[...] = _dot(memn_ref[...], dkv, "tn")
        dmemn = _dot(dkv, w_ref[...], "nt")
        xf = mem_ref[...]
        r = lax.rsqrt(jnp.mean(xf * xf, axis=-1, keepdims=True) + EPS)
        dgn_ref[...] = _fold8(dmemn * (xf * r))

    return pl.pallas_call(
        body, name="mem_kv_bwd",
        out_shape=[jax.ShapeDtypeStruct((D_MODEL, D_MODEL), F32), jax.ShapeDtypeStruct((8, D_MODEL), F32),
                   jax.ShapeDtypeStruct((8, MEM_HEAD_DIM), F32)],
        scratch_shapes=[pltpu.VMEM((m, D_MODEL), BF16)],
        compiler_params=_params(),
    )(dkn, dv, kraw, memn, mem, g_norm, w_kv, g_kn)


def _dh_bwd(dps, w_all_t, row0, nrows, name, side=None):
    tm, tk = min(TS_ROW, nrows), 2048
    blk0 = row0 // tm
    widths = [a.shape[1] // tk for a in dps]
    starts = [int(v) for v in np.cumsum([0] + widths[:-1])]
    nk = sum(widths)

    def body(*refs):
        d_refs, w_ref, o_ref, acc_ref = refs[:5], refs[5], refs[6], refs[7]
        kk = pl.program_id(1)

        @pl.when(kk == 0)
        def _():
            acc_ref[...] = jnp.zeros_like(acc_ref)

        for d_ref, st, wd in zip(d_refs, starts, widths):
            @pl.when((kk >= st) & (kk < st + wd))
            def _(d_ref=d_ref):
                acc_ref[...] += _dot(d_ref[...], w_ref[...])

        @pl.when(kk == nk - 1)
        def _():
            o_ref[...] = acc_ref[...]

    def dspec(st, wd):
        return pl.BlockSpec((tm, tk), lambda i, kk: (blk0 + i, jnp.clip(kk - st, 0, wd - 1)))

    outs, extra = _pcall(
        body, side, name, (nrows // tm, nk),
        [dspec(st, wd) for st, wd in zip(starts, widths)] + [pl.BlockSpec((tk, D_MODEL), lambda i, kk: (kk, 0))],
        [pl.BlockSpec((tm, D_MODEL), lambda i, kk: (i, 0))], [jax.ShapeDtypeStruct((nrows, D_MODEL), F32)],
        (*dps, w_all_t), ("parallel", "arbitrary"), scratch_shapes=[pltpu.VMEM((tm, D_MODEL), F32)])
    return outs[0], extra


def _dx_bwd(dh_a, dh_b, x, dy, g):
    s = x.shape[0]
    ts = min(TS_DX, dh_a.shape[0])
    na, nb = dh_a.shape[0] // ts, dh_b.shape[0] // ts

    def body(dha_ref, dhb_ref, x_ref, dy_ref, g_ref, o_ref, dg_ref):
        @pl.when(pl.program_id(0) == 0)
        def _():
            dg_ref[...] = jnp.zeros_like(dg_ref)

        dh = jnp.where(pl.program_id(0) < na, dha_ref[...], dhb_ref[...])
        dx, gg = _rms_bwd(x_ref[...], g_ref[...], dh, D_MODEL)
        o_ref[...] = dy_ref[...] + dx
        dg_ref[...] += _fold8(gg)

    row = pl.BlockSpec((ts, D_MODEL), lambda i: (i, 0))
    outs, _ = _pcall(
        body, None, "dx_bwd", (s // ts,),
        [pl.BlockSpec((ts, D_MODEL), lambda i: (jnp.minimum(i, na - 1), 0)),
         pl.BlockSpec((ts, D_MODEL), lambda i: (jnp.clip(i - na, 0, nb - 1), 0)),
         row, row, pl.BlockSpec((1, D_MODEL), lambda i: (0, 0))],
        [row, pl.BlockSpec((8, D_MODEL), lambda i: (0, 0))],
        [jax.ShapeDtypeStruct((s, D_MODEL), F32), jax.ShapeDtypeStruct((8, D_MODEL), F32)],
        (dh_a, dh_b, x, dy, g), ("arbitrary",))
    return outs


def _pad128(v, n):
    return jnp.pad(v.reshape(1, n), ((0, 0), (0, 128 - n)))


def _local_step(x, positions, mem, target, comm, gains):
    half = QK_ROPE // 2
    inv_freq = jnp.power(ROPE_THETA, -jnp.arange(half, dtype=F32) / half)
    ang = positions.astype(F32)[:, None] * inv_freq
    cos, sin = jnp.cos(ang), jnp.sin(ang)
    zpad = jnp.zeros((x.shape[0], 64), F32)
    cs = jnp.concatenate([cos, cos, zpad], axis=1)
    sg = jnp.concatenate([-sin, sin, zpad], axis=1)
    g_qr, g_kr = _pad128(gains["mla_qn_rope_g"], 64), _pad128(gains["mla_kn_rope_g"], 64)
    g_qn, g_kn = gains["mla_qn_nope_g"], gains["mla_kn_nope_g"]

    def hosted_matmul(tag, a, b, mode, out_dtype, **tiles):
        side = comm.side(tag)
        if side is None:
            return _matmul(a, b, mode, out_dtype, tag, **tiles)
        out, extra = _matmul(a, b, mode, out_dtype, tag, side=side, **tiles)
        comm.done(tag, extra)
        return out

    h = _norm_fwd(x, gains["norm_g"])
    p_all = hosted_matmul("proj_fwd", h, comm.weight("w_all_t"), "nt", BF16, tm=TM_PROJ, tn=TN_PROJ)
    a_conv, co = _conv_fwd(p_all, comm.weight("conv_w"))
    (cqn, ckvn, krope), extra = _lora_fwd(p_all, gains["mla_q_norm_g"], gains["mla_kv_norm_g"], g_kr, cs, sg,
                                          comm.side("lora_fwd"))
    comm.done("lora_fwd", extra)
    q_raw, qcat = _q_up(cqn, comm.weight("w_uq"), g_qn, g_qr, cs, sg)
    kv, kcat, vt = _kv_up(ckvn, comm.weight("w_ukv"), krope, g_kn)
    mla_y, a_mla, lse = _flash_fwd(qcat, kcat, vt, p_all)
    memn, kraw, kn, vmem = _mem_kv_fwd(mem, gains["mem_norm_g"], comm.weight("w_mem_kv"), gains["mem_kn_g"])
    a_mem = _mem_attn_fwd(p_all, kn, vmem, gains["mem_qn_g"])
    o_c, o_m, o_e, merged = _merge_fwd(a_conv, comm.weight("w_conv_out"), a_mla, comm.weight("w_mla_out"), a_mem,
                                       comm.weight("w_mem_out"), p_all)
    dy, dyb, loss_parts = _out_fwd(merged, comm.weight("w_o"), x, target)

    comm.put("w_o", _matmul(merged, dyb, "tn", BF16, "dw_o"))
    do_c, do_m, do_e, dp_g = _merge_bwd(dyb, comm.weight("w_o"), p_all, o_c, o_m, o_e)
    comm.put("w_conv_out", _matmul(a_conv, do_c, "tn", BF16, "dw_conv_out"))
    comm.put("w_mla_out", _matmul(a_mla, do_m, "tn", BF16, "dw_mla_out"))
    comm.put("w_mem_out", _matmul(a_mem, do_e, "tn", BF16, "dw_mem_out"))
    da_conv = hosted_matmul("da_conv", do_c, comm.weight("w_conv_out"), "nt", BF16)
    da_mem = _matmul(do_e, comm.weight("w_mem_out"), "nt", BF16, "da_mem")
    dp_c, dconv_w = _conv_bwd(da_conv, p_all, co, comm.weight("conv_w"))
    dmla_y, dp_z, delta = _mla_gate_bwd(do_m, comm.weight("w_mla_out"), mla_y, p_all)
    dqcat, dkcat, dv = _flash_bwd(qcat, kcat, kv, dmla_y, lse, delta)
    (dq_raw, dg_q), extra = _q_prep_bwd(dqcat, q_raw, g_qn, g_qr, cs, sg, comm.side("q_prep_bwd"))
    comm.done("q_prep_bwd", extra)
    (dkv, dkr, dg_kn), extra = _k_prep_bwd(dkcat, dv, kv, g_kn, comm.side("k_prep_bwd"))
    comm.done("k_prep_bwd", extra)
    comm.put("w_uq", _matmul(cqn, dq_raw, "tn", BF16, "dw_uq"))
    comm.put("w_ukv", _matmul(ckvn, dkv, "tn", BF16, "dw_ukv"))
    dcqn = _matmul(dq_raw, comm.weight("w_uq"), "nt", F32, "dcqn")
    dckvn = _matmul(dkv, comm.weight("w_ukv"), "nt", F32, "dckvn")
    dp_s, dg_qn, dg_kvn, dg_kr = _lora_bwd(dcqn, dckvn, dkr, p_all, gains["mla_q_norm_g"],
                                            gains["mla_kv_norm_g"], g_kr, cs, sg)
    dp_m, dkn, dvm, dg_mq = _mem_attn_bwd(da_mem, p_all, kn, vmem, gains["mem_qn_g"])
    dw_mem_kv, dg_mn, dg_mk = _mem_kv_bwd(dkn, dvm, kraw, memn, mem, gains["mem_norm_g"],
                                          comm.weight("w_mem_kv"), gains["mem_kn_g"])
    comm.put("w_mem_kv", dw_mem_kv.astype(BF16))
    dps = [dp_g, dp_z, dp_c, dp_m, dp_s]
    comm.put("w_all_t", [hosted_matmul("dw_in_%d" % k, d, h, "tn", BF16) for k, d in enumerate(dps)])
    s = x.shape[0]
    dh_a, extra = _dh_bwd(dps, comm.weight("w_all_t"), 0, s // 4, "dh_a", comm.side("dh_a"))
    comm.done("dh_a", extra)
    dh_b, extra = _dh_bwd(dps, comm.weight("w_all_t"), s // 4, s - s // 4, "dh_b", comm.side("dh_b"))
    comm.done("dh_b", extra)
    grad_x, dg_n = _dx_bwd(dh_a, dh_b, x, dy, gains["norm_g"])

    gsmall = {
        "norm_g": dg_n.sum(0), "conv_w": dconv_w.sum(1), "mla_q_norm_g": dg_qn.sum(0),
        "mla_kv_norm_g": dg_kvn.sum(0), "mla_qn_nope_g": dg_q[0].sum(0), "mla_qn_rope_g": dg_q[1].sum(0)[:64],
        "mla_kn_nope_g": dg_kn.sum(0), "mla_kn_rope_g": dg_kr.sum(0)[:64], "mem_norm_g": dg_mn.sum(0),
        "mem_qn_g": dg_mq.sum(0), "mem_kn_g": dg_mk.sum(0),
    }
    return loss_parts, grad_x, gsmall


def _me():
    return lax.axis_index("x"), lax.axis_index("y"), lax.axis_index("c")


def _chips(x, y):
    return [(1 - x, y), (x, 1 - y), (1 - x, 1 - y)]


def _allgather_shards(xs, name):
    r, c = xs.shape
    hc = c // 2

    def body(x_ref, out_ref, send_sems, recv_sems):
        x, y, cc = _me()
        me, sibling = (x, y, cc), (x, y, 1 - cc)
        across_x, across_y, diagonal = _chips(x, y)

        def rows(px, py, pc):
            return out_ref.at[2 * px + py, :, pl.ds(pc * hc, hc)]

        def copy(k, block, to, src=None):
            return pltpu.make_async_remote_copy(
                src_ref=rows(*block) if src is None else src, dst_ref=rows(*block),
                send_sem=send_sems.at[k], recv_sem=recv_sems.at[k], device_id=to, device_id_type=MESH)

        my_half = x_ref.at[:, pl.ds(cc * hc, hc)]
        direct = [copy(0, me, (*across_x, cc), src=my_half), copy(1, me, (*across_y, cc), src=my_half)]
        for cp in direct:
            cp.start()
        copy(0, (*across_x, cc), me).wait_recv()
        copy(1, (*across_y, cc), me).wait_recv()
        from_x = cc == 0
        came = (jnp.where(from_x, across_x[0], across_y[0]), jnp.where(from_x, across_x[1], across_y[1]), cc)
        goes = (jnp.where(from_x, across_y[0], across_x[0]), jnp.where(from_x, across_y[1], across_x[1]), cc)
        relay = copy(2, came, goes)
        relay.start()
        passed = [copy(3, (*across_x, cc), sibling), copy(4, (*across_y, cc), sibling)]
        for cp in passed:
            cp.start()
        copy(2, (*diagonal, cc), me).wait_recv()
        passed.append(copy(5, (*diagonal, cc), sibling))
        passed[2].start()
        for j, chip in enumerate((across_x, across_y, diagonal)):
            copy(3 + j, (*chip, 1 - cc), me).wait_recv()
        for cp in direct + [relay] + passed:
            cp.wait_send()

    return pl.pallas_call(
        body, name=name, in_specs=[ANY], out_specs=ANY,
        out_shape=jax.ShapeDtypeStruct((4, r, c), xs.dtype),
        scratch_shapes=[pltpu.SemaphoreType.DMA((6,)), pltpu.SemaphoreType.DMA((6,))],
    )(xs)


def _put_own(g4, own, name):
    r, c = own.shape
    tc = c // 2
    xi, yi, _ = _me()
    sel = jnp.reshape(2 * xi + yi, (1,)).astype(jnp.int32)

    def body(sel_ref, own_ref, g_ref, o_ref):
        o_ref[0] = own_ref[...]

    return pl.pallas_call(
        body, name=name,
        grid_spec=pltpu.PrefetchScalarGridSpec(
            num_scalar_prefetch=1, grid=(c // tc,),
            in_specs=[pl.BlockSpec((r, tc), lambda j, sel_ref: (0, j)), ANY],
            out_specs=pl.BlockSpec((1, r, tc), lambda j, sel_ref: (sel_ref[0], 0, j))),
        out_shape=jax.ShapeDtypeStruct(g4.shape, g4.dtype), input_output_aliases={2: 0},
        compiler_params=_params(("arbitrary",)),
    )(sel, own, g4)


def _with_own(g4, own):
    xi, yi, _ = _me()
    pick = lax.broadcasted_iota(jnp.int32, (4,) + (1,) * own.ndim, 0) == 2 * xi + yi
    return jnp.where(pick, own[None], g4)


def _remote(src, dst, send_sems, recv_sems, k, to):
    return pltpu.make_async_remote_copy(src_ref=src, dst_ref=dst, send_sem=send_sems.at[k], recv_sem=recv_sems.at[k],
                                        device_id=to, device_id_type=MESH)


def _side_gather_ici(shards):
    def build(ins, outs, send_sems, recv_sems):
        x, y, cc = _me()
        pairs = []
        for a, (x_ref, o_ref) in enumerate(zip(ins, outs)):
            hr = x_ref.shape[0] // 2
            my_half = x_ref.at[pl.ds(cc * hr, hr), :]
            for j, (px, py) in enumerate(_chips(x, y)):
                mine = o_ref.at[pl.ds((4 * x + 2 * y + cc) * hr, hr), :]
                theirs = o_ref.at[pl.ds((4 * px + 2 * py + cc) * hr, hr), :]
                pairs.append((_remote(my_half, mine, send_sems, recv_sems, 3 * a + j, (px, py, cc)),
                              _remote(my_half, theirs, send_sems, recv_sems, 3 * a + j, (px, py, cc))))
        return pairs

    shapes = [jax.ShapeDtypeStruct((4 * s.shape[0], s.shape[1]), s.dtype) for s in shards]
    return _Side(shards, shapes, 3 * len(shards), build)


def _side_gather_pass(bufs):
    def build(ins, outs, send_sems, recv_sems):
        x, y, cc = _me()
        pairs = []
        for a, o_ref in enumerate(outs):
            hr = o_ref.shape[0] // 8
            for j, (px, py) in enumerate(_chips(x, y)):
                got = o_ref.at[pl.ds((4 * px + 2 * py + cc) * hr, hr), :]
                coming = o_ref.at[pl.ds((4 * px + 2 * py + 1 - cc) * hr, hr), :]
                pairs.append((_remote(got, got, send_sems, recv_sems, 3 * a + j, (x, y, 1 - cc)),
                              _remote(got, coming, send_sems, recv_sems, 3 * a + j, (x, y, 1 - cc))))
        return pairs

    shapes = [jax.ShapeDtypeStruct(b.shape, b.dtype) for b in bufs]
    return _Side(bufs, shapes, 3 * len(bufs), build, aliases={a: a for a in range(len(bufs))})


def _side_swap(srcs, parts):
    def view(s_ref, part, cc):
        if part == "lead":
            return s_ref.at[1 - cc]
        if part == "cols":
            hc = s_ref.shape[1] // 2
            return s_ref.at[:, pl.ds((1 - cc) * hc, hc)]
        return s_ref

    def build(ins, outs, send_sems, recv_sems):
        x, y, cc = _me()
        pairs = []
        for a, (s_ref, o_ref) in enumerate(zip(ins, outs)):
            cp = _remote(view(s_ref, parts[a], cc), o_ref, send_sems, recv_sems, a, (x, y, 1 - cc))
            pairs.append((cp, cp))
        return pairs

    out_shape = {"lead": lambda s: s.shape[1:], "cols": lambda s: (s.shape[0], s.shape[1] // 2), "all": lambda s: s.shape}
    shapes = [jax.ShapeDtypeStruct(out_shape[p](s), s.dtype) for s, p in zip(srcs, parts)]
    return _Side(srcs, shapes, len(srcs), build)


def _side_chips(ts):
    def build(ins, outs, send_sems, recv_sems):
        x, y, cc = _me()
        pairs = []
        for a, (t_ref, r_ref) in enumerate(zip(ins, outs)):
            for j, (px, py) in enumerate(_chips(x, y)):
                cp = _remote(t_ref.at[2 * px + py], r_ref.at[j], send_sems, recv_sems, 3 * a + j, (px, py, cc))
                pairs.append((cp, cp))
        return pairs

    shapes = [jax.ShapeDtypeStruct((3,) + t.shape[1:], t.dtype) for t in ts]
    return _Side(ts, shapes, 3 * len(ts), build)


def _exchange(side, name):
    ns_in, ns_out = len(side.inputs), len(side.out_shapes)

    def body(*refs):
        pairs = side.build(refs[:ns_in], refs[ns_in:ns_in + ns_out], refs[-2], refs[-1])
        for send, _ in pairs:
            send.start()
        for send, recv in pairs:
            send.wait_send()
            recv.wait_recv()

    outs = pl.pallas_call(
        body, name=name, in_specs=[ANY] * ns_in, out_specs=[ANY] * ns_out, out_shape=side.out_shapes,
        scratch_shapes=[pltpu.SemaphoreType.DMA((side.n_copies,)), pltpu.SemaphoreType.DMA((side.n_copies,))],
    )(*side.inputs)
    return list(outs)


ELEMENTWISE_BLOCK_BYTES = 2 << 20


def _tile(r, c, itemsize):
    tr, tc = r, c
    while tr * tc * itemsize > ELEMENTWISE_BLOCK_BYTES and tr % 32 == 0:
        tr //= 2
    while tr * tc * itemsize > ELEMENTWISE_BLOCK_BYTES and tc % 256 == 0:
        tc //= 2
    return tr, tc


def _add_own_half(gb, got, sel, name):
    _, _, hr, c = gb.shape
    tr, tc = _tile(hr, c, 4)

    def body(sel_ref, a_ref, b_ref, o_ref):
        o_ref[...] = (a_ref[0].astype(F32) + b_ref[...].astype(F32)).astype(BF16)

    return pl.pallas_call(
        body, name=name,
        grid_spec=pltpu.PrefetchScalarGridSpec(
            num_scalar_prefetch=1, grid=(4, hr // tr, c // tc),
            in_specs=[pl.BlockSpec((1, 1, tr, tc), lambda k, i, j, sel_ref: (sel_ref[0], k, i, j)),
                      pl.BlockSpec((1, tr, tc), lambda k, i, j, sel_ref: (k, i, j))],
            out_specs=pl.BlockSpec((1, tr, tc), lambda k, i, j, sel_ref: (k, i, j))),
        out_shape=jax.ShapeDtypeStruct(got.shape, BF16),
        compiler_params=_params(("parallel", "parallel", "parallel")),
    )(sel, gb, got)


def _add_own_cols(g, got, sel, name):
    r, c = g.shape
    hr, hc = r // 4, c // 2
    tr, tc = _tile(hr, hc, 4)
    nj = hc // tc

    def body(sel_ref, a_ref, b_ref, o_ref):
        o_ref[...] = (a_ref[...].astype(F32) + b_ref[...].astype(F32)).astype(BF16)

    t = pl.pallas_call(
        body, name=name,
        grid_spec=pltpu.PrefetchScalarGridSpec(
            num_scalar_prefetch=1, grid=(r // tr, nj),
            in_specs=[pl.BlockSpec((tr, tc), lambda i, j, sel_ref: (i, sel_ref[0] * nj + j)),
                      pl.BlockSpec((tr, tc), lambda i, j, sel_ref: (i, j))],
            out_specs=pl.BlockSpec((tr, tc), lambda i, j, sel_ref: (i, j))),
        out_shape=jax.ShapeDtypeStruct((r, hc), BF16),
        compiler_params=_params(("parallel", "parallel")),
    )(sel, g, got)
    return t.reshape(4, hr, hc)


def _sum_partials(t, rcv, sel, name):
    _, hr, c = t.shape
    tr, tc = _tile(hr, c, 4)

    def body(sel_ref, t_ref, r_ref, o_ref):
        o_ref[...] = ((t_ref[0].astype(F32) + r_ref[0].astype(F32)) + r_ref[1].astype(F32)) + r_ref[2].astype(F32)

    return pl.pallas_call(
        body, name=name,
        grid_spec=pltpu.PrefetchScalarGridSpec(
            num_scalar_prefetch=1, grid=(hr // tr, c // tc),
            in_specs=[pl.BlockSpec((1, tr, tc), lambda i, j, sel_ref: (sel_ref[0], i, j)),
                      pl.BlockSpec((3, tr, tc), lambda i, j, sel_ref: (0, i, j))],
            out_specs=pl.BlockSpec((tr, tc), lambda i, j, sel_ref: (i, j))),
        out_shape=jax.ShapeDtypeStruct((hr, c), F32),
        compiler_params=_params(("parallel", "parallel")),
    )(sel, t, rcv)


class _GroupReduce:
    def __init__(self):
        self.names, self.gb, self.t, self.fh, self.other = [], [], [], [], []

    def put(self, name, gb):
        self.names.append(name)
        self.gb.append(gb)

    def side(self, stage):
        if stage == 1:
            return _side_swap(self.gb, ["lead" if g.ndim == 4 else "cols" for g in self.gb])
        return _side_chips(self.t) if stage == 2 else _side_swap(self.fh, ["all"] * len(self.fh))

    def done(self, stage, outs):
        x, y, cc = _me()
        if stage == 1:
            sel = jnp.reshape(cc, (1,)).astype(jnp.int32)
            self.t = [(_add_own_half if gb.ndim == 4 else _add_own_cols)(gb, got, sel, "rs_add_" + n)
                      for n, gb, got in zip(self.names, self.gb, outs)]
        elif stage == 2:
            sel = jnp.reshape(2 * x + y, (1,)).astype(jnp.int32)
            self.fh = [_sum_partials(t, rcv, sel, "rs_sum_" + n) for n, t, rcv in zip(self.names, self.t, outs)]
        else:
            self.other = list(outs)

    def result(self):
        return {n: (fh, other) for n, fh, other in zip(self.names, self.fh, self.other)}


def _allreduce_small(v, tag):
    r = v.shape[0]

    def body(v_ref, out_ref, buf_ref, send_sems, recv_sems):
        x, y, cc = _me()
        me = 4 * x + 2 * y + cc
        buf_ref[pl.ds(pl.multiple_of(me * r, 8), r), :] = v_ref[...]
        peers = [(x, y, 1 - cc)] + [(px, py, pc) for (px, py) in _chips(x, y) for pc in (cc, 1 - cc)]
        cps = []
        for k, (px, py, pc) in enumerate(peers):
            mine = buf_ref.at[pl.ds(pl.multiple_of(me * r, 8), r), :]
            cps.append(pltpu.make_async_remote_copy(
                src_ref=v_ref, dst_ref=mine, send_sem=send_sems.at[k], recv_sem=recv_sems.at[k],
                device_id=(px, py, pc), device_id_type=MESH))
        for cp in cps:
            cp.start()
        for cp in cps:
            cp.wait()
        acc = buf_ref[0:r, :]
        for d in range(1, 8):
            acc = acc + buf_ref[d * r:(d + 1) * r, :]
        out_ref[...] = acc

    vm = pl.BlockSpec(memory_space=pltpu.VMEM)
    return pl.pallas_call(
        body, name="allreduce_small_" + tag, in_specs=[vm], out_specs=vm,
        out_shape=jax.ShapeDtypeStruct((r, 128), F32),
        scratch_shapes=[pltpu.VMEM((8 * r, 128), F32), pltpu.SemaphoreType.DMA((7,)), pltpu.SemaphoreType.DMA((7,))],
    )(v)


def _adamw(w, g, m, v, name):
    r, c = w.shape
    tr = r
    c1 = 1.0 / (1.0 - ADAM_B1 ** ADAM_STEP)
    c2 = 1.0 / (1.0 - ADAM_B2 ** ADAM_STEP)

    def body(w_ref, g_ref, m_ref, v_ref, d_ref, mo_ref, vo_ref):
        gg = g_ref[...]
        mn = ADAM_B1 * m_ref[...] + (1.0 - ADAM_B1) * gg
        vn = ADAM_B2 * v_ref[...] + (1.0 - ADAM_B2) * (gg * gg)
        mo_ref[...] = mn
        vo_ref[...] = vn
        d_ref[...] = -ADAM_LR * ((mn * c1) / (jnp.sqrt(vn * c2) + ADAM_EPS) + ADAM_WD * w_ref[...])

    blk = pl.BlockSpec((tr, c), lambda i: (i, 0))
    return pl.pallas_call(
        body, name=name, grid=(r // tr,), in_specs=[blk] * 4, out_specs=[blk] * 3,
        out_shape=[jax.ShapeDtypeStruct((r, c), F32)] * 3,
        compiler_params=_params(("parallel",)),
    )(w, g, m, v)


def _adamw_halves(w, g_own, g_other, m, v, sel, name, axis):
    r, c = w.shape
    tr, tc = _tile(g_own.shape[0], g_own.shape[1], 4)
    nh = (g_own.shape[axis]) // (tr, tc)[axis]
    c1 = 1.0 / (1.0 - ADAM_B1 ** ADAM_STEP)
    c2 = 1.0 / (1.0 - ADAM_B2 ** ADAM_STEP)

    def body(sel_ref, w_ref, go_ref, gx_ref, m_ref, v_ref, d_ref, mo_ref, vo_ref, g_ref):
        mine = (pl.program_id(axis) // nh) == sel_ref[0]
        gg = jnp.where(mine, go_ref[...], gx_ref[...])
        g_ref[...] = gg
        mn = ADAM_B1 * m_ref[...] + (1.0 - ADAM_B1) * gg
        vn = ADAM_B2 * v_ref[...] + (1.0 - ADAM_B2) * (gg * gg)
        mo_ref[...] = mn
        vo_ref[...] = vn
        d_ref[...] = -ADAM_LR * ((mn * c1) / (jnp.sqrt(vn * c2) + ADAM_EPS) + ADAM_WD * w_ref[...])

    blk = pl.BlockSpec((tr, tc), lambda i, j, sel_ref: (i, j))
    if axis == 0:
        half = pl.BlockSpec((tr, tc), lambda i, j, sel_ref: (i % nh, j))
    else:
        half = pl.BlockSpec((tr, tc), lambda i, j, sel_ref: (i, j % nh))
    return pl.pallas_call(
        body, name=name,
        grid_spec=pltpu.PrefetchScalarGridSpec(
            num_scalar_prefetch=1, grid=(r // tr, c // tc),
            in_specs=[blk, half, half, blk, blk], out_specs=[blk] * 4),
        out_shape=[jax.ShapeDtypeStruct((r, c), F32)] * 4,
        compiler_params=_params(("parallel", "parallel")),
    )(sel, w, g_own, g_other, m, v)


WEIGHTS = ['norm_g', 'w_in', 'conv_w', 'w_conv_out', 'mla_q_norm_g', 'w_uq', 'mla_kv_norm_g', 'w_ukv',
           'mla_qn_nope_g', 'mla_qn_rope_g', 'mla_kn_nope_g', 'mla_kn_rope_g', 'w_mla_out', 'mem_norm_g',
           'w_mem_kv', 'mem_qn_g', 'mem_kn_g', 'w_mem_out', 'w_o']
COL_SHARDED = ['w_in', 'w_conv_out', 'w_uq', 'w_ukv', 'w_mem_out']
ROW_SHARDED = ['w_mla_out', 'w_mem_kv', 'w_o']
SMALL = ['norm_g', 'conv_w', 'mla_q_norm_g', 'mla_kv_norm_g', 'mla_qn_nope_g', 'mla_qn_rope_g', 'mla_kn_nope_g',
         'mla_kn_rope_g', 'mem_norm_g', 'mem_qn_g', 'mem_kn_g']
SMALL_SIZES = [2048, 3072, 512, 512, 128, 64, 128, 64, 2048, 256, 256]
PACK_ROWS = 72


def _full_from_shards(name, g4):
    if name in COL_SHARDED:
        return g4.transpose(1, 0, 2).reshape(g4.shape[1], 4 * g4.shape[2])
    return g4.reshape(4 * g4.shape[1], g4.shape[2])


def _w_all_t_from_w_in_t(w_in_t):
    conv, small, mz, memq, gates = (w_in_t[0:4096], w_in_t[4096:5184], w_in_t[5184:7232],
                                    w_in_t[7232:9280], w_in_t[9280:15424])
    return jnp.concatenate([gates, mz, conv, memq, small, jnp.zeros((2048 - 1088, D_MODEL), BF16)], axis=0)


def _pad_w_uq(w_uq):
    return jnp.pad(w_uq.reshape(512, MLA_HEADS, 192), ((0, 0), (0, 0), (0, 64))).reshape(512, 4096)


def _grad_halves_by_owner(name, g):
    if name == "w_all_t":
        dw_g, dw_z, dw_c, dw_m, dw_s = g
        return "w_in", jnp.concatenate([dw_c, dw_s[:1088], dw_z, dw_m, dw_g], axis=0)
    if name == "w_uq":
        g = g.reshape(512, MLA_HEADS, HEAD_PAD)[:, :, :192].reshape(512, 3072)
    r, c = g.shape
    if name in COL_SHARDED:
        return name, g.reshape(2, r // 2, 4, c // 4).transpose(0, 2, 1, 3)
    return name, g.reshape(4, 2, r // 8, c).transpose(1, 0, 2, 3)


LATE_WEIGHTS = ['w_conv_out', 'w_uq', 'w_ukv', 'w_mem_out', 'w_mla_out', 'w_mem_kv', 'w_o']
HOSTS = {"proj_fwd": ("gather", 1), "lora_fwd": ("gather", 2),
         "da_conv": ("g1", 1), "q_prep_bwd": ("g1", 2), "k_prep_bwd": ("g1", 3),
         "dw_in_0": ("g2", 1), "dw_in_2": ("g2", 2), "dw_in_4": ("g2", 3),
         "dh_a": ("g3", 1), "dh_b": ("g3", 2)}
GROUP_OF = {"w_o": "g1", "w_conv_out": "g1", "w_mla_out": "g1", "w_mem_out": "g1",
            "w_uq": "g2", "w_ukv": "g2", "w_mem_kv": "g2", "w_in": "g3"}


class _Comm:
    def __init__(self, shards, w_in_t_full, conv_full):
        self.shards = shards
        self.w = {"w_all_t": _w_all_t_from_w_in_t(w_in_t_full), "conv_w": conv_full}
        self.bufs = None
        self.groups = {"g1": _GroupReduce(), "g2": _GroupReduce(), "g3": _GroupReduce()}

    def weight(self, name):
        return self.w[name]

    def side(self, tag):
        if tag not in HOSTS:
            return None
        kind, stage = HOSTS[tag]
        if kind == "gather":
            return _side_gather_ici([self.shards[n] for n in LATE_WEIGHTS]) if stage == 1 else \
                _side_gather_pass(self.bufs)
        return self.groups[kind].side(stage)

    def done(self, tag, outs):
        kind, stage = HOSTS[tag]
        if kind != "gather":
            self.groups[kind].done(stage, outs)
        elif stage == 1:
            self.bufs = list(outs)
        else:
            for n, buf in zip(LATE_WEIGHTS, outs):
                own = self.shards[n]
                self.w[n] = _full_from_shards(n, _with_own(buf.reshape((4,) + own.shape), own))
            self.w["w_uq"] = _pad_w_uq(self.w["w_uq"])

    def put(self, name, g):
        name, gb = _grad_halves_by_owner(name, g)
        self.groups[GROUP_OF[name]].put(name, gb)

    def reduced(self):
        last = self.groups["g3"]
        last.done(3, _exchange(last.side(3), "rs_swap_w_in"))
        out = {}
        for grp in self.groups.values():
            out.update(grp.result())
        return out


def kernel(x, positions, mem, norm_g, w_in, conv_w, w_conv_out, mla_q_norm_g, w_uq, mla_kv_norm_g, w_ukv, mla_qn_nope_g, mla_qn_rope_g, mla_kn_nope_g, mla_kn_rope_g, w_mla_out, mem_norm_g, w_mem_kv, mem_qn_g, mem_kn_g, w_mem_out, w_o, loss_target, m_norm_g, m_w_in, m_conv_w, m_w_conv_out, m_mla_q_norm_g, m_w_uq, m_mla_kv_norm_g, m_w_ukv, m_mla_qn_nope_g, m_mla_qn_rope_g, m_mla_kn_nope_g, m_mla_kn_rope_g, m_w_mla_out, m_mem_norm_g, m_w_mem_kv, m_mem_qn_g, m_mem_kn_g, m_w_mem_out, m_w_o, v_norm_g, v_w_in, v_conv_w, v_w_conv_out, v_mla_q_norm_g, v_w_uq, v_mla_kv_norm_g, v_w_ukv, v_mla_qn_nope_g, v_mla_qn_rope_g, v_mla_kn_nope_g, v_mla_kn_rope_g, v_w_mla_out, v_mem_norm_g, v_w_mem_kv, v_mem_qn_g, v_mem_kn_g, v_w_mem_out, v_w_o):
    args = locals()
    w = {n: args[n][0] for n in WEIGHTS}
    m1 = {n: args["m_" + n][0] for n in WEIGHTS}
    v2 = {n: args["v_" + n][0] for n in WEIGHTS}
    xi, yi, ci = _me()
    chip = 2 * xi + yi

    conv_slot = jnp.zeros((3, 4, 256), F32).at[:, chip, :].set(w["conv_w"] * jnp.where(ci == 0, 1.0, 0.0))
    pre = jnp.zeros((PACK_ROWS * 128,), F32).at[0:3072].set(conv_slot.reshape(3072))
    conv_full = _allreduce_small(pre.reshape(PACK_ROWS, 128), "conv_w").reshape(-1)[0:3072].reshape(3, 1024)

    w_in_t = w["w_in"].T.astype(BF16)
    w_in_t_full = _put_own(_allgather_shards(w_in_t, "ag_w_in"), w_in_t, "own_w_in").reshape(4 * w_in_t.shape[0],
                                                                                               D_MODEL)
    comm = _Comm({n: w[n].astype(BF16) for n in LATE_WEIGHTS}, w_in_t_full, conv_full)
    gains = {n: w[n].reshape(1, -1) for n in SMALL if n != "conv_w"}

    loss_parts, grad_x, gsmall = _local_step(x[0], positions[0], mem[0], loss_target[0], comm, gains)

    loss_local = 0.5 * jnp.sum(loss_parts) * (1.0 / D_MODEL)
    flat = jnp.concatenate([gsmall[n].reshape(-1) for n in SMALL] + [loss_local.reshape(1)])
    flat = jnp.pad(flat, (0, PACK_ROWS * 128 - flat.shape[0]))
    tot = _allreduce_small(flat.reshape(PACK_ROWS, 128), "grads").reshape(-1)
    grads = {}
    off = 0
    for n, sz in zip(SMALL, SMALL_SIZES):
        grads[n] = tot[off:off + sz]
        off += sz
    loss = tot[off]
    grads["conv_w"] = lax.dynamic_slice(grads["conv_w"].reshape(3, 1024), (0, chip * 256), (3, 256))
    for n in SMALL:
        grads[n] = grads[n].reshape(w[n].shape)

    deltas, new_m, new_v = {}, {}, {}
    for n in SMALL:
        shp = w[n].shape
        two_d = (lambda a: a.reshape(1, -1)) if len(shp) == 1 else (lambda a: a)
        d, mn, vn = _adamw(two_d(w[n]), two_d(grads[n]), two_d(m1[n]), two_d(v2[n]), "adamw_" + n)
        deltas[n], new_m[n], new_v[n] = d.reshape(shp), mn.reshape(shp), vn.reshape(shp)

    sel_c = jnp.reshape(ci, (1,)).astype(jnp.int32)
    reduced = comm.reduced()
    for n in COL_SHARDED + ROW_SHARDED:
        g_own, g_other = reduced[n]
        if n == "w_in":
            outs = _adamw_halves(w[n].T, g_own, g_other, m1[n].T, v2[n].T, sel_c, "adamw_" + n, 1)
            deltas[n], new_m[n], new_v[n], grads[n] = [o.T for o in outs]
        else:
            deltas[n], new_m[n], new_v[n], grads[n] = _adamw_halves(w[n], g_own, g_other, m1[n], v2[n], sel_c,
                                                                     "adamw_" + n, 0)

    lead = lambda a: a[None]
    return (loss, grad_x[None], *[lead(grads[n]) for n in WEIGHTS], *[lead(deltas[n]) for n in WEIGHTS],
            *[lead(new_m[n]) for n in WEIGHTS], *[lead(new_v[n]) for n in WEIGHTS])
```

```python
import functools

import numpy as np
import jax
import jax.numpy as jnp
from jax import lax
from jax.experimental import pallas as pl
from jax.experimental.pallas import tpu as pltpu

F32 = jnp.float32
BF16 = jnp.bfloat16
MESH = pl.DeviceIdType.MESH

D_MODEL = 2048
EPS = 1e-6
CHUNK = 64
MLA_HEADS = 16
QK_NOPE = 128
QK_ROPE = 64
HEAD_PAD = 256
MEM_TOKENS = 256
MEM_HEADS = 4
MEM_HEAD_DIM = 256
ROPE_THETA = 10000.0
MLA_SCALE = (QK_NOPE + QK_ROPE) ** -0.5
MEM_SCALE = MEM_HEAD_DIM ** -0.5
LN2 = 0.6931471805599453
Q_SCALE = MLA_SCALE / LN2
NEG = -1e30

ADAM_LR = 0.001
ADAM_B1 = 0.9
ADAM_B2 = 0.999
ADAM_EPS = 1e-08
ADAM_WD = 0.01
ADAM_STEP = 10

NP = 16384
COL_G, COL_Z, COL_C, COL_M, COL_S = 0, 6144, 8192, 12288, 14336

TM_PROJ = 1024
TN_PROJ = 2048
TS_ROW = 512
TS_BWD = 512
TS_DX = 256
TQ_ATT = 1024
TK_ATT = 1024
TR_ATT = 256
FWD_HEADS_PER_STEP = 8
VMEM_LIMIT = 56 * 1024 * 1024

NT_DIMS = (((1,), (1,)), ((), ()))
TN_DIMS = (((0,), (0,)), ((), ()))


def _dot(a, b, mode="nn"):
    if mode == "nn":
        return jnp.dot(a, b, preferred_element_type=F32)
    return lax.dot_general(a, b, NT_DIMS if mode == "nt" else TN_DIMS, preferred_element_type=F32)


def _sigmoid(z):
    return 1.0 / (1.0 + jnp.exp(-z))


def _params(sem=None, vmem=VMEM_LIMIT):
    return pltpu.CompilerParams(dimension_semantics=sem, vmem_limit_bytes=vmem)


def _fold8(v):
    r, c = v.shape
    return v.reshape(r // 8, 8, c).sum(axis=0)


def _swap32(t):
    lane = lax.broadcasted_iota(jnp.int32, t.shape, 1)
    return jnp.where(lane < 32, pltpu.roll(t, 96, 1), pltpu.roll(t, 32, 1))


def _rope(t, cs, sg):
    return t * cs + _swap32(t) * sg


def _rope_bwd(d, cs, sg):
    return d * cs + _swap32(d * sg)


def _rms_fwd(xf, g, n):
    r = lax.rsqrt(jnp.sum(xf * xf, axis=-1, keepdims=True) * (1.0 / n) + EPS)
    return xf * r * g


def _rms_bwd(xf, g, dy, n):
    r = lax.rsqrt(jnp.sum(xf * xf, axis=-1, keepdims=True) * (1.0 / n) + EPS)
    xhat = xf * r
    dyg = dy * g
    dx = r * (dyg - xhat * (jnp.sum(dyg * xhat, axis=-1, keepdims=True) * (1.0 / n)))
    return dx, dy * xhat


ANY = pl.BlockSpec(memory_space=pl.ANY)


class _Side:
    def __init__(self, inputs, out_shapes, n_copies, build, aliases=()):
        self.inputs, self.out_shapes, self.n_copies = list(inputs), list(out_shapes), n_copies
        self.build, self.aliases = build, dict(aliases)


def _pcall(body, side, name, grid, in_specs, out_specs, out_shape, args, sem, scratch_shapes=(), prefetch=()):
    n_pre, n_in, n_out = len(prefetch), len(in_specs), len(out_specs)
    if side is None:
        outs = pl.pallas_call(
            body, name=name,
            grid_spec=pltpu.PrefetchScalarGridSpec(num_scalar_prefetch=n_pre, grid=grid, in_specs=list(in_specs),
                                                   out_specs=list(out_specs), scratch_shapes=list(scratch_shapes)),
            out_shape=list(out_shape), compiler_params=_params(sem),
        )(*prefetch, *args)
        return list(outs), []
    ns_in, ns_out = len(side.inputs), len(side.out_shapes)

    def wrapped(*refs):
        pre, r = refs[:n_pre], refs[n_pre:]
        ins, s_ins = r[:n_in], r[n_in:n_in + ns_in]
        outs = r[n_in + ns_in:n_in + ns_in + n_out]
        s_outs = r[n_in + ns_in + n_out:n_in + ns_in + n_out + ns_out]
        scr, send_sems, recv_sems = r[n_in + ns_in + n_out + ns_out:-2], r[-2], r[-1]
        pairs = side.build(s_ins, s_outs, send_sems, recv_sems)
        first = pl.program_id(0) == 0
        last = pl.program_id(0) == grid[0] - 1
        for d in range(1, len(grid)):
            first = first & (pl.program_id(d) == 0)
            last = last & (pl.program_id(d) == grid[d] - 1)

        @pl.when(first)
        def _():
            for send, _ in pairs:
                send.start()

        body(*pre, *ins, *outs, *scr)

        @pl.when(last)
        def _():
            for send, recv in pairs:
                send.wait_send()
                recv.wait_recv()

    outs = pl.pallas_call(
        wrapped, name=name,
        grid_spec=pltpu.PrefetchScalarGridSpec(
            num_scalar_prefetch=n_pre, grid=grid, in_specs=list(in_specs) + [ANY] * ns_in,
            out_specs=list(out_specs) + [ANY] * ns_out,
            scratch_shapes=list(scratch_shapes) + [pltpu.SemaphoreType.DMA((side.n_copies,)),
                                                   pltpu.SemaphoreType.DMA((side.n_copies,))]),
        out_shape=list(out_shape) + side.out_shapes,
        input_output_aliases={n_pre + n_in + i: n_out + o for i, o in side.aliases.items()},
        compiler_params=_params(("arbitrary",) * len(grid)),
    )(*prefetch, *args, *side.inputs)
    return list(outs[:n_out]), list(outs[n_out:])


def _matmul(a, b, mode, out_dtype, name, tm=1024, tn=1024, tk=None, side=None):
    if tk is None:
        tk = 4096 if mode == "tn" else 2048
    if mode == "nn":
        (m, k), (_, n) = a.shape, b.shape
    elif mode == "nt":
        (m, k), (n, _) = a.shape, b.shape
    else:
        (k, m), (_, n) = a.shape, b.shape
    tm, tn, tk = min(tm, m), min(tn, n), min(tk, k)
    nk = k // tk
    if mode == "tn":
        a_spec = pl.BlockSpec((tk, tm), lambda i, j, kk: (kk, i))
    else:
        a_spec = pl.BlockSpec((tm, tk), lambda i, j, kk: (i, kk))
    if mode == "nt":
        b_spec = pl.BlockSpec((tn, tk), lambda i, j, kk: (j, kk))
    else:
        b_spec = pl.BlockSpec((tk, tn), lambda i, j, kk: (kk, j))

    def body(a_ref, b_ref, o_ref, acc_ref):
        kk = pl.program_id(2)

        @pl.when(kk == 0)
        def _():
            acc_ref[...] = jnp.zeros_like(acc_ref)

        acc_ref[...] += _dot(a_ref[...].astype(BF16), b_ref[...].astype(BF16), mode)

        @pl.when(kk == nk - 1)
        def _():
            o_ref[...] = acc_ref[...].astype(out_dtype)

    def single_body(a_ref, b_ref, o_ref):
        o_ref[...] = _dot(a_ref[...].astype(BF16), b_ref[...].astype(BF16), mode).astype(out_dtype)

    outs, extra = _pcall(
        single_body if nk == 1 else body, side, name, (m // tm, n // tn, nk), [a_spec, b_spec],
        [pl.BlockSpec((tm, tn), lambda i, j, kk: (i, j))], [jax.ShapeDtypeStruct((m, n), out_dtype)], (a, b),
        ("parallel", "parallel", "arbitrary"), scratch_shapes=[] if nk == 1 else [pltpu.VMEM((tm, tn), F32)])
    return outs[0] if side is None else (outs[0], extra)


def _norm_fwd(x, g):
    s = x.shape[0]
    ts = min(TS_ROW, s)

    def body(x_ref, g_ref, h_ref):
        h_ref[...] = _rms_fwd(x_ref[...], g_ref[...], D_MODEL).astype(BF16)

    row = pl.BlockSpec((ts, D_MODEL), lambda i: (i, 0))
    return pl.pallas_call(
        body, name="norm_fwd", grid=(s // ts,),
        in_specs=[row, pl.BlockSpec((1, D_MODEL), lambda i: (0, 0))], out_specs=row,
        out_shape=jax.ShapeDtypeStruct((s, D_MODEL), BF16), compiler_params=_params(("parallel",)),
    )(x, g)


def _shift_rows(v, carry, j, ts, back):
    if back:
        main = pltpu.roll(v, j, 0)
        edge = pltpu.roll(jnp.concatenate([carry, v[:8]], axis=0), j, 0)[8:]
        return jnp.concatenate([edge, main[8:]], axis=0)
    main = pltpu.roll(v, ts - j, 0)
    edge = pltpu.roll(jnp.concatenate([v[ts - 8:], carry], axis=0), 16 - j, 0)[:8]
    return jnp.concatenate([main[:ts - 8], edge], axis=0)


def _conv_fwd(p_all, conv_w):
    s = p_all.shape[0]
    ts = min(TS_ROW, s)
    c0 = COL_C // 1024

    def body(cg_ref, bg_ref, u_ref, z_ref, w_ref, a_ref, co_ref, carry_ref):
        @pl.when(pl.program_id(0) == 0)
        def _():
            carry_ref[...] = jnp.zeros_like(carry_ref)

        p = cg_ref[...].astype(F32) * u_ref[...].astype(F32)
        carry = carry_ref[...]
        co = (w_ref[2:3, :] * p + w_ref[1:2, :] * _shift_rows(p, carry, 1, ts, True)
              + w_ref[0:1, :] * _shift_rows(p, carry, 2, ts, True))
        carry_ref[...] = p[ts - 8:]
        z = z_ref[...].astype(F32)
        a_ref[...] = (bg_ref[...].astype(F32) * co * (z * _sigmoid(z))).astype(BF16)
        co_ref[...] = co.astype(BF16)

    seg = lambda c: pl.BlockSpec((ts, 1024), lambda i: (i, c0 + c))
    return pl.pallas_call(
        body, name="conv_fwd", grid=(s // ts,),
        in_specs=[seg(0), seg(1), seg(2), seg(3), pl.BlockSpec((3, 1024), lambda i: (0, 0))],
        out_specs=[pl.BlockSpec((ts, 1024), lambda i: (i, 0))] * 2,
        out_shape=[jax.ShapeDtypeStruct((s, 1024), BF16)] * 2,
        scratch_shapes=[pltpu.VMEM((8, 1024), F32)],
        compiler_params=_params(("arbitrary",)),
    )(p_all, p_all, p_all, p_all, conv_w)


def _lora_fwd(p_all, gq, gkv, gkr, cs, sg, side=None):
    s = p_all.shape[0]
    ts = min(TS_ROW, s)

    def body(cq_ref, ckv_ref, kr_ref, gq_ref, gkv_ref, gkr_ref, cs_ref, sg_ref, cqn_ref, ckvn_ref, krope_ref):
        cqn_ref[...] = _rms_fwd(cq_ref[...].astype(F32), gq_ref[...], 512).astype(BF16)
        ckvn_ref[...] = _rms_fwd(ckv_ref[...].astype(F32), gkv_ref[...], 512).astype(BF16)
        kn = _rms_fwd(kr_ref[...].astype(F32), gkr_ref[...], QK_ROPE)
        krope_ref[...] = _rope(kn, cs_ref[...], sg_ref[...]).astype(BF16)

    row = lambda w: pl.BlockSpec((ts, w), lambda i: (i, 0))
    vec = lambda w: pl.BlockSpec((1, w), lambda i: (0, 0))
    return _pcall(
        body, side, "lora_fwd", (s // ts,),
        [pl.BlockSpec((ts, 512), lambda i: (i, COL_S // 512)),
         pl.BlockSpec((ts, 512), lambda i: (i, COL_S // 512 + 1)),
         pl.BlockSpec((ts, 128), lambda i: (i, (COL_S + 1024) // 128)),
         vec(512), vec(512), vec(128), row(128), row(128)],
        [row(512), row(512), row(128)],
        [jax.ShapeDtypeStruct((s, 512), BF16), jax.ShapeDtypeStruct((s, 512), BF16),
         jax.ShapeDtypeStruct((s, 128), BF16)],
        (p_all, p_all, p_all, gq, gkv, gkr, cs, sg), ("parallel",))


UP_HEADS = 8


def _q_up(cqn, w_uq, gn, gr, cs, sg):
    s = cqn.shape[0]
    tm, tn = min(TS_ROW, s), UP_HEADS * HEAD_PAD

    def body(c_ref, w_ref, gn_ref, gr_ref, cs_ref, sg_ref, q_ref, o_ref):
        qb = _dot(c_ref[...], w_ref[...]).astype(BF16)
        q_ref[...] = qb
        for h in range(UP_HEADS):
            q = qb[:, h * HEAD_PAD:(h + 1) * HEAD_PAD].astype(F32)
            a = _rms_fwd(q[:, :128], gn_ref[...], QK_NOPE)
            b = _rope(_rms_fwd(q[:, 128:], gr_ref[...], QK_ROPE), cs_ref[...], sg_ref[...])
            o_ref[h] = (jnp.concatenate([a, b], axis=1) * Q_SCALE).astype(BF16)

    vec = pl.BlockSpec((1, 128), lambda i, j: (0, 0))
    tab = pl.BlockSpec((tm, 128), lambda i, j: (i, 0))
    return pl.pallas_call(
        body, name="q_up", grid=(s // tm, MLA_HEADS // UP_HEADS),
        in_specs=[pl.BlockSpec((tm, 512), lambda i, j: (i, 0)), pl.BlockSpec((512, tn), lambda i, j: (0, j)),
                  vec, vec, tab, tab],
        out_specs=[pl.BlockSpec((tm, tn), lambda i, j: (i, j)),
                   pl.BlockSpec((UP_HEADS, tm, HEAD_PAD), lambda i, j: (j, i, 0))],
        out_shape=[jax.ShapeDtypeStruct((s, MLA_HEADS * HEAD_PAD), BF16),
                   jax.ShapeDtypeStruct((MLA_HEADS, s, HEAD_PAD), BF16)],
        compiler_params=_params(("parallel", "parallel")),
    )(cqn, w_uq, gn, gr, cs, sg)


def _kv_up(ckvn, w_ukv, krope, gk):
    s = ckvn.shape[0]
    tm, tn = min(TS_ROW, s), UP_HEADS * HEAD_PAD

    def body(c_ref, w_ref, kr_ref, gk_ref, kv_ref, o_ref, vt_ref):
        kvb = _dot(c_ref[...], w_ref[...]).astype(BF16)
        kv_ref[...] = kvb
        for h in range(UP_HEADS):
            a = _rms_fwd(kvb[:, h * HEAD_PAD:h * HEAD_PAD + 128].astype(F32), gk_ref[...], QK_NOPE)
            o_ref[h] = jnp.concatenate([a.astype(BF16), kr_ref[...]], axis=1)
            vt_ref[h] = kvb[:, h * HEAD_PAD + 128:(h + 1) * HEAD_PAD].astype(F32).T.astype(BF16)

    return pl.pallas_call(
        body, name="kv_up", grid=(s // tm, MLA_HEADS // UP_HEADS),
        in_specs=[pl.BlockSpec((tm, 512), lambda i, j: (i, 0)), pl.BlockSpec((512, tn), lambda i, j: (0, j)),
                  pl.BlockSpec((tm, 128), lambda i, j: (i, 0)), pl.BlockSpec((1, 128), lambda i, j: (0, 0))],
        out_specs=[pl.BlockSpec((tm, tn), lambda i, j: (i, j)),
                   pl.BlockSpec((UP_HEADS, tm, HEAD_PAD), lambda i, j: (j, i, 0)),
                   pl.BlockSpec((UP_HEADS, 128, tm), lambda i, j: (j, 0, i))],
        out_shape=[jax.ShapeDtypeStruct((s, MLA_HEADS * HEAD_PAD), BF16),
                   jax.ShapeDtypeStruct((MLA_HEADS, s, HEAD_PAD), BF16),
                   jax.ShapeDtypeStruct((MLA_HEADS, 128, s), BF16)],
        compiler_params=_params(("parallel", "parallel")),
    )(ckvn, w_ukv, krope, gk)


def _chunk_mask(tq, tk, row0=0):
    r = (lax.broadcasted_iota(jnp.int32, (tq, tk), 0) + row0) // CHUNK
    c = lax.broadcasted_iota(jnp.int32, (tq, tk), 1) // CHUNK
    return c <= r


def _causal_pairs(n, by_key):
    if by_key:
        pairs = [(i, j) for j in range(n) for i in range(j, n)]
    else:
        pairs = [(i, j) for i in range(n) for j in range(i + 1)]
    return (jnp.asarray([p[0] for p in pairs], jnp.int32), jnp.asarray([p[1] for p in pairs], jnp.int32))


def _flash_fwd(qcat, kcat, vt, p_all):
    s = qcat.shape[1]
    tq = tk = min(TQ_ATT, s)
    tc = min(TR_ATT, tq)
    nq = s // tq
    hg = FWD_HEADS_PER_STEP
    zc = COL_Z // (128 * hg)
    qi, kj = _causal_pairs(nq, False)

    def body(qi_ref, kj_ref, q_ref, k_ref, vt_ref, z_ref, y_ref, a_ref, lse_ref, m_sc, l_sc, acc_sc):
        t = pl.program_id(1)
        i, j = qi_ref[t], kj_ref[t]

        @pl.when(j == 0)
        def _():
            m_sc[...] = jnp.full_like(m_sc, NEG)
            l_sc[...] = jnp.zeros_like(l_sc)
            acc_sc[...] = jnp.zeros_like(acc_sc)

        def step(masked):
            nc = tq // tc
            state = {(g, c): [m_sc[g, :, c * tc:(c + 1) * tc], l_sc[g, :, c * tc:(c + 1) * tc],
                              acc_sc[g, :, c * tc:(c + 1) * tc]] for g in range(hg) for c in range(nc)}
            units = [(g, u, c) for u in range(tk // tc) for c in range(nc) for g in range(hg)
                     if (u <= c or not masked)]

            def scores(g, u, c):
                return _dot(k_ref[g, u * tc:(u + 1) * tc, :], q_ref[g, c * tc:(c + 1) * tc, :], "nt")

            def weighted_values(g, u, c, alpha, pb):
                state[g, c][2] = alpha * state[g, c][2] + _dot(vt_ref[g, :, u * tc:(u + 1) * tc], pb)

            ahead = 4
            pending = [scores(*un) for un in units[:ahead]]
            late = None
            for n, (g, u, c) in enumerate(units):
                st = pending.pop(0)
                if n + ahead < len(units):
                    pending.append(scores(*units[n + ahead]))
                if masked and u == c:
                    kc = lax.broadcasted_iota(jnp.int32, (tc, tc), 0) // CHUNK
                    qc = lax.broadcasted_iota(jnp.int32, (tc, tc), 1) // CHUNK
                    st = jnp.where(kc <= qc, st, NEG)
                m_run, l_run, _ = state[g, c]
                m_new = jnp.maximum(m_run, jnp.max(st, axis=0, keepdims=True))
                alpha = jnp.exp2(m_run - m_new)
                p = jnp.exp2(st - m_new)
                state[g, c][0] = m_new
                state[g, c][1] = alpha * l_run + jnp.sum(p, axis=0, keepdims=True)
                if late is not None:
                    weighted_values(*late)
                late = (g, u, c, alpha, p.astype(BF16))
            weighted_values(*late)
            for g in range(hg):
                m_sc[g] = jnp.concatenate([state[g, c][0] for c in range(nc)], axis=1)
                l_sc[g] = jnp.concatenate([state[g, c][1] for c in range(nc)], axis=1)
                acc_sc[g] = jnp.concatenate([state[g, c][2] for c in range(nc)], axis=1)

        @pl.when(j < i)
        def _():
            step(False)

        @pl.when(j == i)
        def _():
            step(True)
            for g in range(hg):
                cols = slice(g * 128, (g + 1) * 128)
                y = (acc_sc[g] * (1.0 / l_sc[g])).T
                z = z_ref[:, cols].astype(F32)
                y_ref[:, cols] = y.astype(BF16)
                a_ref[:, cols] = (y * (z * _sigmoid(z))).astype(BF16)
                lse2 = m_sc[g] + jnp.log(l_sc[g]) * (1.0 / LN2)
                lse_ref[g] = jnp.broadcast_to(lse2, (128, tq)).T

    wide = pl.BlockSpec((tq, 128 * hg), lambda h, t, qi_ref, kj_ref: (qi_ref[t], h))
    return pl.pallas_call(
        body, name="flash_fwd",
        grid_spec=pltpu.PrefetchScalarGridSpec(
            num_scalar_prefetch=2, grid=(MLA_HEADS // hg, qi.shape[0]),
            in_specs=[pl.BlockSpec((hg, tq, HEAD_PAD), lambda h, t, qi_ref, kj_ref: (h, qi_ref[t], 0)),
                      pl.BlockSpec((hg, tk, HEAD_PAD), lambda h, t, qi_ref, kj_ref: (h, kj_ref[t], 0)),
                      pl.BlockSpec((hg, 128, tk), lambda h, t, qi_ref, kj_ref: (h, 0, kj_ref[t])),
                      pl.BlockSpec((tq, 128 * hg), lambda h, t, qi_ref, kj_ref: (qi_ref[t], zc + h))],
            out_specs=[wide, wide,
                       pl.BlockSpec((hg, tq, 128), lambda h, t, qi_ref, kj_ref: (h, qi_ref[t], 0))],
            scratch_shapes=[pltpu.VMEM((hg, 1, tq), F32), pltpu.VMEM((hg, 1, tq), F32),
                            pltpu.VMEM((hg, 128, tq), F32)]),
        out_shape=[jax.ShapeDtypeStruct((s, MLA_HEADS * 128), BF16),
                   jax.ShapeDtypeStruct((s, MLA_HEADS * 128), BF16),
                   jax.ShapeDtypeStruct((MLA_HEADS, s, 128), F32)],
        compiler_params=_params(("parallel", "arbitrary")),
    )(qi, kj, qcat, kcat, vt, p_all)


def _mem_kv_fwd(mem, g_norm, w_kv, g_kn):
    m = mem.shape[0]

    def body(mem_ref, g_ref, w_ref, gk_ref, memn_ref, kraw_ref, kn_ref, v_ref):
        memn = _rms_fwd(mem_ref[...], g_ref[...], D_MODEL).astype(BF16)
        memn_ref[...] = memn
        kvm = _dot(memn, w_ref[...])
        kraw_ref[...] = kvm[:, :1024]
        v_ref[...] = kvm[:, 1024:].astype(BF16)
        for hh in range(MEM_HEADS):
            sl = slice(hh * MEM_HEAD_DIM, (hh + 1) * MEM_HEAD_DIM)
            kn_ref[:, sl] = _rms_fwd(kvm[:, sl], gk_ref[...], MEM_HEAD_DIM).astype(BF16)

    return pl.pallas_call(
        body, name="mem_kv_fwd",
        out_shape=[jax.ShapeDtypeStruct((m, D_MODEL), BF16), jax.ShapeDtypeStruct((m, 1024), F32),
                   jax.ShapeDtypeStruct((m, 1024), BF16), jax.ShapeDtypeStruct((m, 1024), BF16)],
        compiler_params=_params(),
    )(mem, g_norm, w_kv, g_kn)


def _mem_attn_fwd(p_all, kn, v, gq):
    s = p_all.shape[0]
    tq = min(TS_ROW, s)

    def body(q_ref, z_ref, kn_ref, v_ref, gq_ref, a_ref):
        z = z_ref[...].astype(F32)
        gate = z * _sigmoid(z)
        for hh in range(MEM_HEADS):
            sl = slice(hh * MEM_HEAD_DIM, (hh + 1) * MEM_HEAD_DIM)
            qn = _rms_fwd(q_ref[:, sl].astype(F32), gq_ref[...], MEM_HEAD_DIM).astype(BF16)
            sc = _dot(qn, kn_ref[:, sl], "nt") * MEM_SCALE
            e = jnp.exp(sc - jnp.max(sc, axis=-1, keepdims=True))
            p = e * (1.0 / jnp.sum(e, axis=-1, keepdims=True))
            y = _dot(p.astype(BF16), v_ref[:, sl])
            a_ref[:, sl] = (y * gate[:, sl]).astype(BF16)

    full = lambda a: pl.BlockSpec(a.shape, lambda i: (0, 0))
    return pl.pallas_call(
        body, name="mem_attn_fwd", grid=(s // tq,),
        in_specs=[pl.BlockSpec((tq, 1024), lambda i: (i, COL_M // 1024)),
                  pl.BlockSpec((tq, 1024), lambda i: (i, COL_M // 1024 + 1)),
                  full(kn), full(v), full(gq)],
        out_specs=pl.BlockSpec((tq, 1024), lambda i: (i, 0)),
        out_shape=jax.ShapeDtypeStruct((s, 1024), BF16),
        compiler_params=_params(("parallel",)),
    )(p_all, p_all, kn, v, gq)


def _merge_fwd(a_conv, w_c, a_mla, w_m, a_mem, w_e, p_all):
    s = a_conv.shape[0]
    tm, tn = min(TS_ROW, s), 1024
    nj = D_MODEL // tn

    def body(ac_ref, wc_ref, am_ref, wm_ref, ae_ref, we_ref, gc_ref, gm_ref, ge_ref,
             oc_ref, om_ref, oe_ref, mg_ref):
        oc = _dot(ac_ref[...], wc_ref[...])
        om = _dot(am_ref[...], wm_ref[...])
        oe = _dot(ae_ref[...], we_ref[...])
        oc_ref[...] = oc.astype(BF16)
        om_ref[...] = om.astype(BF16)
        oe_ref[...] = oe.astype(BF16)
        mg_ref[...] = (_sigmoid(gc_ref[...].astype(F32)) * oc + _sigmoid(gm_ref[...].astype(F32)) * om
                       + _sigmoid(ge_ref[...].astype(F32)) * oe).astype(BF16)

    act = lambda k: pl.BlockSpec((tm, k), lambda i, j: (i, 0))
    wgt = lambda k: pl.BlockSpec((k, tn), lambda i, j: (0, j))
    gate = lambda b: pl.BlockSpec((tm, tn), lambda i, j: (i, b * nj + j))
    out = pl.BlockSpec((tm, tn), lambda i, j: (i, j))
    return pl.pallas_call(
        body, name="merge_fwd", grid=(s // tm, nj),
        in_specs=[act(1024), wgt(1024), act(2048), wgt(2048), act(1024), wgt(1024), gate(0), gate(1), gate(2)],
        out_specs=[out] * 4,
        out_shape=[jax.ShapeDtypeStruct((s, D_MODEL), BF16)] * 4,
        compiler_params=_params(("parallel", "parallel")),
    )(a_conv, w_c, a_mla, w_m, a_mem, w_e, p_all, p_all, p_all)


def _out_fwd(merged, w_o, x, target):
    s = merged.shape[0]
    tm, tn = min(2 * TS_ROW, s), 1024
    nj = D_MODEL // tn

    def body(mg_ref, w_ref, x_ref, t_ref, dy_ref, dyb_ref, ls_ref):
        e = x_ref[...] + _dot(mg_ref[...], w_ref[...]) - t_ref[...]
        dy = e * (1.0 / D_MODEL)
        dy_ref[...] = dy
        dyb_ref[...] = dy.astype(BF16)
        r = _fold8(e * e)
        acc = r[:, 0:128]
        for cc in range(1, tn // 128):
            acc = acc + r[:, cc * 128:(cc + 1) * 128]
        ls_ref[...] = acc

    tile = pl.BlockSpec((tm, tn), lambda i, j: (i, j))
    return pl.pallas_call(
        body, name="out_fwd", grid=(s // tm, nj),
        in_specs=[pl.BlockSpec((tm, D_MODEL), lambda i, j: (i, 0), pipeline_mode=pl.Buffered(1)),
                  pl.BlockSpec((D_MODEL, tn), lambda i, j: (0, j)), tile, tile],
        out_specs=[tile, tile, pl.BlockSpec((8, 128), lambda i, j: (i, j))],
        out_shape=[jax.ShapeDtypeStruct((s, D_MODEL), F32), jax.ShapeDtypeStruct((s, D_MODEL), BF16),
                   jax.ShapeDtypeStruct((s // tm * 8, nj * 128), F32)],
        compiler_params=_params(("parallel", "parallel")),
    )(merged, w_o, x, target)


def _merge_bwd(dyb, w_o, p_all, o_c, o_m, o_e):
    s = dyb.shape[0]
    tm = min(256, s)

    def body(dy_ref, w_ref, g_ref, oc_ref, om_ref, oe_ref, dc_ref, dm_ref, de_ref, dg_ref):
        dmg = _dot(dy_ref[...], w_ref[...], "nt")
        for b, (o_ref, d_ref) in enumerate(((oc_ref, dc_ref), (om_ref, dm_ref), (oe_ref, de_ref))):
            sl = slice(b * D_MODEL, (b + 1) * D_MODEL)
            sg = _sigmoid(g_ref[:, sl].astype(F32))
            d_ref[...] = (dmg * sg).astype(BF16)
            dg_ref[:, sl] = (dmg * o_ref[...].astype(F32) * sg * (1.0 - sg)).astype(BF16)

    row = pl.BlockSpec((tm, D_MODEL), lambda i: (i, 0))
    wide = pl.BlockSpec((tm, 3 * D_MODEL), lambda i: (i, 0))
    return pl.pallas_call(
        body, name="merge_bwd", grid=(s // tm,),
        in_specs=[row, pl.BlockSpec((D_MODEL, D_MODEL), lambda i: (0, 0)), wide, row, row, row],
        out_specs=[row, row, row, wide],
        out_shape=[jax.ShapeDtypeStruct((s, D_MODEL), BF16)] * 3 + [jax.ShapeDtypeStruct((s, 3 * D_MODEL), BF16)],
        compiler_params=_params(("parallel",)),
    )(dyb, w_o, p_all, o_c, o_m, o_e)


def _conv_bwd(da, p_all, co, conv_w):
    s = da.shape[0]
    ts = min(TS_ROW, s)
    n = s // ts
    c0 = COL_C // 1024

    def body(da_ref, cg_ref, bg_ref, u_ref, z_ref, co_ref, w_ref, dp_ref, dw_ref, carry_ref):
        @pl.when(pl.program_id(0) == 0)
        def _():
            carry_ref[...] = jnp.zeros_like(carry_ref)
            dw_ref[...] = jnp.zeros_like(dw_ref)

        da_ = da_ref[...].astype(F32)
        cg, bg = cg_ref[...].astype(F32), bg_ref[...].astype(F32)
        u, z, cov = u_ref[...].astype(F32), z_ref[...].astype(F32), co_ref[...].astype(F32)
        sz = _sigmoid(z)
        dyc = da_ * (z * sz)
        dz = da_ * (bg * cov) * (sz * (1.0 + z * (1.0 - sz)))
        db = dyc * cov
        dco = dyc * bg
        carry = carry_ref[...]
        d1 = _shift_rows(dco, carry, 1, ts, False)
        d2 = _shift_rows(dco, carry, 2, ts, False)
        carry_ref[...] = dco[:8]
        dpp = w_ref[2:3, :] * dco + w_ref[1:2, :] * d1 + w_ref[0:1, :] * d2
        p = cg * u
        dw_ref[0] += _fold8(p * d2)
        dw_ref[1] += _fold8(p * d1)
        dw_ref[2] += _fold8(p * dco)
        dp_ref[...] = jnp.concatenate([dpp * u, db, dpp * cg, dz], axis=1).astype(BF16)

    rev = lambda c: pl.BlockSpec((ts, 1024), lambda i: (n - 1 - i, c))
    return pl.pallas_call(
        body, name="conv_bwd", grid=(n,),
        in_specs=[rev(0), rev(c0), rev(c0 + 1), rev(c0 + 2), rev(c0 + 3), rev(0),
                  pl.BlockSpec((3, 1024), lambda i: (0, 0))],
        out_specs=[pl.BlockSpec((ts, 4096), lambda i: (n - 1 - i, 0)),
                   pl.BlockSpec((3, 8, 1024), lambda i: (0, 0, 0))],
        out_shape=[jax.ShapeDtypeStruct((s, 4096), BF16), jax.ShapeDtypeStruct((3, 8, 1024), F32)],
        scratch_shapes=[pltpu.VMEM((8, 1024), F32)],
        compiler_params=_params(("arbitrary",)),
    )(da, p_all, p_all, p_all, p_all, co, conv_w)


def _mla_gate_bwd(do_m, w_mla_out, y, p_all):
    s = do_m.shape[0]
    tm, tn = min(TS_ROW, s), 1024
    heads = tn // 128

    def body(do_ref, w_ref, y_ref, z_ref, dy_ref, dz_ref, dl_ref):
        da_ = _dot(do_ref[...], w_ref[...], "nt").astype(BF16).astype(F32)
        yv, z = y_ref[...].astype(F32), z_ref[...].astype(F32)
        sz = _sigmoid(z)
        dyv = da_ * (z * sz)
        dy_ref[...] = dyv.astype(BF16)
        dz_ref[...] = (da_ * yv * (sz * (1.0 + z * (1.0 - sz)))).astype(BF16)
        pr = dyv * yv
        for h in range(heads):
            dl_ref[h] = jnp.broadcast_to(jnp.sum(pr[:, h * 128:(h + 1) * 128], axis=-1, keepdims=True), (tm, 128))

    tile = pl.BlockSpec((tm, tn), lambda i, j: (i, j))
    return pl.pallas_call(
        body, name="mla_gate_bwd", grid=(s // tm, D_MODEL // tn),
        in_specs=[pl.BlockSpec((tm, D_MODEL), lambda i, j: (i, 0)), pl.BlockSpec((tn, D_MODEL), lambda i, j: (j, 0)),
                  tile, pl.BlockSpec((tm, tn), lambda i, j: (i, COL_Z // tn + j))],
        out_specs=[tile, tile, pl.BlockSpec((heads, tm, 128), lambda i, j: (j, i, 0))],
        out_shape=[jax.ShapeDtypeStruct((s, D_MODEL), BF16)] * 2 + [jax.ShapeDtypeStruct((MLA_HEADS, s, 128), F32)],
        compiler_params=_params(("parallel", "parallel")),
    )(do_m, w_mla_out, y, p_all)


def _flash_bwd(qcat, kcat, kv, dy, lse, delta):
    s = qcat.shape[1]
    tq = tk = min(TQ_ATT, s)
    tr = min(TR_ATT, tq)
    nq = s // tq
    qi, kj = _causal_pairs(nq, True)
    npairs = qi.shape[0]

    def body(qi_ref, kj_ref, q_ref, k_ref, v_ref, do_ref, lse_ref, dl_ref, dq_ref, dk_ref, dv_ref,
             dq_acc, dk_acc, dv_acc):
        t = pl.program_id(1)
        i, j = qi_ref[t], kj_ref[t]

        def add_dq(rows, dq_new):
            @pl.when(j == 0)
            def _():
                dq_acc[rows, :] = dq_new

            @pl.when(j > 0)
            def _():
                dq_acc[rows, :] += dq_new

        def full_step():
            q, k = q_ref[0], k_ref[0]
            p = jnp.exp2(_dot(q, k, "nt") - lse_ref[0][:, 0:1])
            do = do_ref[...]
            dpv = _dot(do, v_ref[...], "nt")
            ds = (p * (dpv - dl_ref[0][:, 0:1])).astype(BF16)
            dv_acc[...] += _dot(p.astype(BF16), do, "tn")
            dk_acc[...] += _dot(ds, q, "tn")
            add_dq(pl.ds(pl.multiple_of(i * tq, tq), tq), _dot(ds, k))

        def diagonal_step():
            def keys(r):
                return (r + 1) * tr

            def scores(r):
                rows = slice(r * tr, (r + 1) * tr)
                return (_dot(q_ref[0, rows, :], k_ref[0, :keys(r), :], "nt"),
                        _dot(do_ref[rows, :], v_ref[:keys(r), :], "nt"))

            def gradients(n, r, pb, ds):
                nk = keys(r)
                rows = slice(r * tr, (r + 1) * tr)
                dv_new = _dot(pb, do_ref[rows, :], "tn")
                dk_new = _dot(ds, q_ref[0, rows, :], "tn")
                if n == 0:
                    dv_acc[...] = dv_new
                    dk_acc[...] = dk_new
                else:
                    dv_acc[:nk, :] += dv_new
                    dk_acc[:nk, :] += dk_new
                add_dq(pl.ds(pl.multiple_of(i * tq + r * tr, tr), tr), _dot(ds, k_ref[0, :nk, :]))

            order = list(range(tq // tr - 1, -1, -1))
            pending = [scores(r) for r in order[:2]]
            late = None
            for n, r in enumerate(order):
                sc, dpv = pending.pop(0)
                if n + 2 < len(order):
                    pending.append(scores(order[n + 2]))
                rows = slice(r * tr, (r + 1) * tr)
                sc = jnp.where(_chunk_mask(tr, keys(r), r * tr), sc, NEG)
                p = jnp.exp2(sc - lse_ref[0][rows, 0:1])
                ds = (p * (dpv - dl_ref[0][rows, 0:1])).astype(BF16)
                if late is not None:
                    gradients(*late)
                late = (n, r, p.astype(BF16), ds)
            gradients(*late)

        @pl.when(i == j)
        def _():
            diagonal_step()

        @pl.when(i > j)
        def _():
            full_step()

        @pl.when(i == nq - 1)
        def _():
            dk_ref[0] = dk_acc[...].astype(BF16)
            dv_ref[0] = dv_acc[...].astype(BF16)

        @pl.when(t == npairs - 1)
        def _():
            dq_ref[0] = dq_acc[...].astype(BF16)

    qrow = lambda w: pl.BlockSpec((1, tq, w), lambda h, t, qi_ref, kj_ref: (h, qi_ref[t], 0))
    krow = lambda w: pl.BlockSpec((1, tk, w), lambda h, t, qi_ref, kj_ref: (h, kj_ref[t], 0))
    return pl.pallas_call(
        body, name="flash_bwd",
        grid_spec=pltpu.PrefetchScalarGridSpec(
            num_scalar_prefetch=2, grid=(MLA_HEADS, npairs),
            in_specs=[qrow(HEAD_PAD), krow(HEAD_PAD),
                      pl.BlockSpec((tk, 128), lambda h, t, qi_ref, kj_ref: (kj_ref[t], 2 * h + 1)),
                      pl.BlockSpec((tq, 128), lambda h, t, qi_ref, kj_ref: (qi_ref[t], h)),
                      qrow(128), qrow(128)],
            out_specs=[pl.BlockSpec((1, s, HEAD_PAD), lambda h, t, qi_ref, kj_ref: (h, 0, 0)),
                       krow(HEAD_PAD), krow(128)],
            scratch_shapes=[pltpu.VMEM((s, HEAD_PAD), F32), pltpu.VMEM((tk, HEAD_PAD), F32),
                            pltpu.VMEM((tk, 128), F32)]),
        out_shape=[jax.ShapeDtypeStruct((MLA_HEADS, s, HEAD_PAD), BF16),
                   jax.ShapeDtypeStruct((MLA_HEADS, s, HEAD_PAD), BF16),
                   jax.ShapeDtypeStruct((MLA_HEADS, s, 128), BF16)],
        compiler_params=_params(("parallel", "arbitrary")),
    )(qi, kj, qcat, kcat, kv, dy, lse, delta)


def _q_prep_bwd(dqcat, q_raw, gn, gr, cs, sg, side=None):
    s = q_raw.shape[0]
    ts = min(TS_BWD, s)

    def body(dq_ref, q_ref, gn_ref, gr_ref, cs_ref, sg_ref, o_ref, dg_ref):
        @pl.when(pl.program_id(0) == 0)
        def _():
            dg_ref[...] = jnp.zeros_like(dg_ref)

        ga_acc = jnp.zeros((8, 128), F32)
        gb_acc = jnp.zeros((8, 128), F32)
        for h in range(MLA_HEADS):
            cols = slice(h * HEAD_PAD, (h + 1) * HEAD_PAD)
            q = q_ref[:, cols].astype(F32)
            d = dq_ref[h].astype(F32) * MLA_SCALE
            da, ga = _rms_bwd(q[:, :128], gn_ref[...], d[:, :128], QK_NOPE)
            db, gb = _rms_bwd(q[:, 128:], gr_ref[...], _rope_bwd(d[:, 128:], cs_ref[...], sg_ref[...]), QK_ROPE)
            o_ref[:, cols] = jnp.concatenate([da, db], axis=1).astype(BF16)
            ga_acc = ga_acc + _fold8(ga)
            gb_acc = gb_acc + _fold8(gb)
        dg_ref[0] += ga_acc
        dg_ref[1] += gb_acc

    wide = pl.BlockSpec((ts, MLA_HEADS * HEAD_PAD), lambda i: (i, 0))
    return _pcall(
        body, side, "q_prep_bwd", (s // ts,),
        [pl.BlockSpec((MLA_HEADS, ts, HEAD_PAD), lambda i: (0, i, 0)), wide,
         pl.BlockSpec((1, 128), lambda i: (0, 0)), pl.BlockSpec((1, 128), lambda i: (0, 0)),
         pl.BlockSpec((ts, 128), lambda i: (i, 0)), pl.BlockSpec((ts, 128), lambda i: (i, 0))],
        [wide, pl.BlockSpec((2, 8, 128), lambda i: (0, 0, 0))],
        [jax.ShapeDtypeStruct((s, MLA_HEADS * HEAD_PAD), BF16), jax.ShapeDtypeStruct((2, 8, 128), F32)],
        (dqcat, q_raw, gn, gr, cs, sg), ("arbitrary",))


def _k_prep_bwd(dkcat, dv, kv, gk, side=None):
    s = kv.shape[0]
    ts = min(TS_BWD, s)

    def body(dk_ref, dv_ref, k_ref, gk_ref, o_ref, dkr_ref, dg_ref):
        @pl.when(pl.program_id(0) == 0)
        def _():
            dg_ref[...] = jnp.zeros_like(dg_ref)

        g_acc = jnp.zeros((8, 128), F32)
        dkr = jnp.zeros((ts, 128), F32)
        for h in range(MLA_HEADS):
            d = dk_ref[h].astype(F32) * LN2
            k = k_ref[:, h * HEAD_PAD:h * HEAD_PAD + 128].astype(F32)
            dk_raw, gg = _rms_bwd(k, gk_ref[...], d[:, :128], QK_NOPE)
            o_ref[:, h * HEAD_PAD:(h + 1) * HEAD_PAD] = jnp.concatenate([dk_raw.astype(BF16), dv_ref[h]], axis=1)
            g_acc = g_acc + _fold8(gg)
            dkr = dkr + d[:, 128:]
        dkr_ref[...] = dkr
        dg_ref[...] += g_acc

    wide = pl.BlockSpec((ts, MLA_HEADS * HEAD_PAD), lambda i: (i, 0))
    return _pcall(
        body, side, "k_prep_bwd", (s // ts,),
        [pl.BlockSpec((MLA_HEADS, ts, HEAD_PAD), lambda i: (0, i, 0)),
         pl.BlockSpec((MLA_HEADS, ts, 128), lambda i: (0, i, 0)), wide,
         pl.BlockSpec((1, 128), lambda i: (0, 0))],
        [wide, pl.BlockSpec((ts, 128), lambda i: (i, 0)), pl.BlockSpec((8, 128), lambda i: (0, 0))],
        [jax.ShapeDtypeStruct((s, MLA_HEADS * HEAD_PAD), BF16), jax.ShapeDtypeStruct((s, 128), F32),
         jax.ShapeDtypeStruct((8, 128), F32)],
        (dkcat, dv, kv, gk), ("arbitrary",))


def _lora_bwd(dcqn, dckvn, dkr, p_all, gq, gkv, gkr, cs, sg):
    s = p_all.shape[0]
    ts = min(TS_ROW, s)

    def body(dq_ref, dkv_ref, dkr_ref, cq_ref, ckv_ref, kr_ref, gq_ref, gkv_ref, gkr_ref, cs_ref, sg_ref,
             o_ref, dgq_ref, dgkv_ref, dgkr_ref):
        @pl.when(pl.program_id(0) == 0)
        def _():
            dgq_ref[...] = jnp.zeros_like(dgq_ref)
            dgkv_ref[...] = jnp.zeros_like(dgkv_ref)
            dgkr_ref[...] = jnp.zeros_like(dgkr_ref)

        dq, g1 = _rms_bwd(cq_ref[...].astype(F32), gq_ref[...], dq_ref[...], 512)
        dkv, g2 = _rms_bwd(ckv_ref[...].astype(F32), gkv_ref[...], dkv_ref[...], 512)
        dkn = _rope_bwd(dkr_ref[...], cs_ref[...], sg_ref[...])
        dk, g3 = _rms_bwd(kr_ref[...].astype(F32), gkr_ref[...], dkn, QK_ROPE)
        o_ref[...] = jnp.concatenate([dq, dkv, dk, jnp.zeros((ts, 2048 - 1152), F32)], axis=1).astype(BF16)
        dgq_ref[...] += _fold8(g1)
        dgkv_ref[...] += _fold8(g2)
        dgkr_ref[...] += _fold8(g3)

    row = lambda w: pl.BlockSpec((ts, w), lambda i: (i, 0))
    vec = lambda w: pl.BlockSpec((1, w), lambda i: (0, 0))
    acc = lambda w: pl.BlockSpec((8, w), lambda i: (0, 0))
    return pl.pallas_call(
        body, name="lora_bwd", grid=(s // ts,),
        in_specs=[row(512), row(512), row(128),
                  pl.BlockSpec((ts, 512), lambda i: (i, COL_S // 512)),
                  pl.BlockSpec((ts, 512), lambda i: (i, COL_S // 512 + 1)),
                  pl.BlockSpec((ts, 128), lambda i: (i, (COL_S + 1024) // 128)),
                  vec(512), vec(512), vec(128), row(128), row(128)],
        out_specs=[row(2048), acc(512), acc(512), acc(128)],
        out_shape=[jax.ShapeDtypeStruct((s, 2048), BF16), jax.ShapeDtypeStruct((8, 512), F32),
                   jax.ShapeDtypeStruct((8, 512), F32), jax.ShapeDtypeStruct((8, 128), F32)],
        compiler_params=_params(("arbitrary",)),
    )(dcqn, dckvn, dkr, p_all, p_all, p_all, gq, gkv, gkr, cs, sg)


def _mem_attn_bwd(da, p_all, kn, v, gq):
    s = p_all.shape[0]
    tq = min(TS_ROW, s)

    def body(da_ref, q_ref, z_ref, kn_ref, v_ref, gq_ref, o_ref, dkn_ref, dv_ref, dg_ref):
        @pl.when(pl.program_id(0) == 0)
        def _():
            dkn_ref[...] = jnp.zeros_like(dkn_ref)
            dv_ref[...] = jnp.zeros_like(dv_ref)
            dg_ref[...] = jnp.zeros_like(dg_ref)

        z = z_ref[...].astype(F32)
        da_ = da_ref[...].astype(F32)
        sz = _sigmoid(z)
        gate = z * sz
        dgate = sz * (1.0 + z * (1.0 - sz))
        for hh in range(MEM_HEADS):
            sl = slice(hh * MEM_HEAD_DIM, (hh + 1) * MEM_HEAD_DIM)
            qf = q_ref[:, sl].astype(F32)
            qn = _rms_fwd(qf, gq_ref[...], MEM_HEAD_DIM).astype(BF16)
            knh, vh = kn_ref[:, sl], v_ref[:, sl]
            sc = _dot(qn, knh, "nt") * MEM_SCALE
            e = jnp.exp(sc - jnp.max(sc, axis=-1, keepdims=True))
            p = e * (1.0 / jnp.sum(e, axis=-1, keepdims=True))
            pb = p.astype(BF16)
            y = _dot(pb, vh)
            dyh = da_[:, sl] * gate[:, sl]
            o_ref[:, 1024 + hh * MEM_HEAD_DIM:1024 + (hh + 1) * MEM_HEAD_DIM] = (
                da_[:, sl] * y * dgate[:, sl]).astype(BF16)
            dyb = dyh.astype(BF16)
            dpm = _dot(dyb, vh, "nt")
            ds = (p * (dpm - jnp.sum(dpm * p, axis=-1, keepdims=True)) * MEM_SCALE).astype(BF16)
            dqn = _dot(ds, knh)
            dkn_ref[:, sl] += _dot(ds, qn, "tn")
            dv_ref[:, sl] += _dot(pb, dyb, "tn")
            dq, gg = _rms_bwd(qf, gq_ref[...], dqn, MEM_HEAD_DIM)
            o_ref[:, sl] = dq.astype(BF16)
            dg_ref[...] += _fold8(gg)

    full = lambda a: pl.BlockSpec(a.shape, lambda i: (0, 0))
    return pl.pallas_call(
        body, name="mem_attn_bwd", grid=(s // tq,),
        in_specs=[pl.BlockSpec((tq, 1024), lambda i: (i, 0)),
                  pl.BlockSpec((tq, 1024), lambda i: (i, COL_M // 1024)),
                  pl.BlockSpec((tq, 1024), lambda i: (i, COL_M // 1024 + 1)),
                  full(kn), full(v), full(gq)],
        out_specs=[pl.BlockSpec((tq, 2048), lambda i: (i, 0)),
                   pl.BlockSpec((MEM_TOKENS, 1024), lambda i: (0, 0)),
                   pl.BlockSpec((MEM_TOKENS, 1024), lambda i: (0, 0)),
                   pl.BlockSpec((8, MEM_HEAD_DIM), lambda i: (0, 0))],
        out_shape=[jax.ShapeDtypeStruct((s, 2048), BF16), jax.ShapeDtypeStruct((MEM_TOKENS, 1024), F32),
                   jax.ShapeDtypeStruct((MEM_TOKENS, 1024), F32), jax.ShapeDtypeStruct((8, MEM_HEAD_DIM), F32)],
        compiler_params=_params(("arbitrary",)),
    )(da, p_all, p_all, kn, v, gq)


def _mem_kv_bwd(dkn, dv, kraw, memn, mem, g_norm, w_kv, g_kn):
    m = mem.shape[0]

    def body(dkn_ref, dv_ref, kraw_ref, memn_ref, mem_ref, g_ref, w_ref, gk_ref, dw_ref, dgn_ref, dgk_ref, dkv_sc):
        gk_acc = jnp.zeros((8, MEM_HEAD_DIM), F32)
        for hh in range(MEM_HEADS):
            sl = slice(hh * MEM_HEAD_DIM, (hh + 1) * MEM_HEAD_DIM)
            dk, gg = _rms_bwd(kraw_ref[:, sl], gk_ref[...], dkn_ref[:, sl], MEM_HEAD_DIM)
            dkv_sc[:, sl] = dk.astype(BF16)
            gk_acc = gk_acc + _fold8(gg)
        dgk_ref[...] = gk_acc
        dkv_sc[:, 1024:] = dv_ref[...].astype(BF16)
        dkv = dkv_sc[...]
        dw_ref[...] = _dot(memn_ref[...], dkv, "tn")
        dmemn = _dot(dkv, w_ref[...], "nt")
        xf = mem_ref[...]
        r = lax.rsqrt(jnp.mean(xf * xf, axis=-1, keepdims=True) + EPS)
        dgn_ref[...] = _fold8(dmemn * (xf * r))

    return pl.pallas_call(
        body, name="mem_kv_bwd",
        out_shape=[jax.ShapeDtypeStruct((D_MODEL, D_MODEL), F32), jax.ShapeDtypeStruct((8, D_MODEL), F32),
                   jax.ShapeDtypeStruct((8, MEM_HEAD_DIM), F32)],
        scratch_shapes=[pltpu.VMEM((m, D_MODEL), BF16)],
        compiler_params=_params(),
    )(dkn, dv, kraw, memn, mem, g_norm, w_kv, g_kn)


def _dh_bwd(dps, w_all_t, row0, nrows, name, side=None):
    tm, tk = min(TS_ROW, nrows), 2048
    blk0 = row0 // tm
    widths = [a.shape[1] // tk for a in dps]
    starts = [int(v) for v in np.cumsum([0] + widths[:-1])]
    nk = sum(widths)

    def body(*refs):
        d_refs, w_ref, o_ref, acc_ref = refs[:5], refs[5], refs[6], refs[7]
        kk = pl.program_id(1)

        @pl.when(kk == 0)
        def _():
            acc_ref[...] = jnp.zeros_like(acc_ref)

        for d_ref, st, wd in zip(d_refs, starts, widths):
            @pl.when((kk >= st) & (kk < st + wd))
            def _(d_ref=d_ref):
                acc_ref[...] += _dot(d_ref[...], w_ref[...])

        @pl.when(kk == nk - 1)
        def _():
            o_ref[...] = acc_ref[...]

    def dspec(st, wd):
        return pl.BlockSpec((tm, tk), lambda i, kk: (blk0 + i, jnp.clip(kk - st, 0, wd - 1)))

    outs, extra = _pcall(
        body, side, name, (nrows // tm, nk),
        [dspec(st, wd) for st, wd in zip(starts, widths)] + [pl.BlockSpec((tk, D_MODEL), lambda i, kk: (kk, 0))],
        [pl.BlockSpec((tm, D_MODEL), lambda i, kk: (i, 0))], [jax.ShapeDtypeStruct((nrows, D_MODEL), F32)],
        (*dps, w_all_t), ("parallel", "arbitrary"), scratch_shapes=[pltpu.VMEM((tm, D_MODEL), F32)])
    return outs[0], extra


def _dx_bwd(dh_a, dh_b, x, dy, g):
    s = x.shape[0]
    ts = min(TS_DX, dh_a.shape[0])
    na, nb = dh_a.shape[0] // ts, dh_b.shape[0] // ts

    def body(dha_ref, dhb_ref, x_ref, dy_ref, g_ref, o_ref, dg_ref):
        @pl.when(pl.program_id(0) == 0)
        def _():
            dg_ref[...] = jnp.zeros_like(dg_ref)

        dh = jnp.where(pl.program_id(0) < na, dha_ref[...], dhb_ref[...])
        dx, gg = _rms_bwd(x_ref[...], g_ref[...], dh, D_MODEL)
        o_ref[...] = dy_ref[...] + dx
        dg_ref[...] += _fold8(gg)

    row = pl.BlockSpec((ts, D_MODEL), lambda i: (i, 0))
    outs, _ = _pcall(
        body, None, "dx_bwd", (s // ts,),
        [pl.BlockSpec((ts, D_MODEL), lambda i: (jnp.minimum(i, na - 1), 0)),
         pl.BlockSpec((ts, D_MODEL), lambda i: (jnp.clip(i - na, 0, nb - 1), 0)),
         row, row, pl.BlockSpec((1, D_MODEL), lambda i: (0, 0))],
        [row, pl.BlockSpec((8, D_MODEL), lambda i: (0, 0))],
        [jax.ShapeDtypeStruct((s, D_MODEL), F32), jax.ShapeDtypeStruct((8, D_MODEL), F32)],
        (dh_a, dh_b, x, dy, g), ("arbitrary",))
    return outs


def _pad128(v, n):
    return jnp.pad(v.reshape(1, n), ((0, 0), (0, 128 - n)))


def _local_step(x, positions, mem, target, comm, gains):
    half = QK_ROPE // 2
    inv_freq = jnp.power(ROPE_THETA, -jnp.arange(half, dtype=F32) / half)
    ang = positions.astype(F32)[:, None] * inv_freq
    cos, sin = jnp.cos(ang), jnp.sin(ang)
    zpad = jnp.zeros((x.shape[0], 64), F32)
    cs = jnp.concatenate([cos, cos, zpad], axis=1)
    sg = jnp.concatenate([-sin, sin, zpad], axis=1)
    g_qr, g_kr = _pad128(gains["mla_qn_rope_g"], 64), _pad128(gains["mla_kn_rope_g"], 64)
    g_qn, g_kn = gains["mla_qn_nope_g"], gains["mla_kn_nope_g"]

    def hosted_matmul(tag, a, b, mode, out_dtype, **tiles):
        side = comm.side(tag)
        if side is None:
            return _matmul(a, b, mode, out_dtype, tag, **tiles)
        out, extra = _matmul(a, b, mode, out_dtype, tag, side=side, **tiles)
        comm.done(tag, extra)
        return out

    h = _norm_fwd(x, gains["norm_g"])
    p_all = hosted_matmul("proj_fwd", h, comm.weight("w_all_t"), "nt", BF16, tm=TM_PROJ, tn=TN_PROJ)
    a_conv, co = _conv_fwd(p_all, comm.weight("conv_w"))
    (cqn, ckvn, krope), extra = _lora_fwd(p_all, gains["mla_q_norm_g"], gains["mla_kv_norm_g"], g_kr, cs, sg,
                                          comm.side("lora_fwd"))
    comm.done("lora_fwd", extra)
    q_raw, qcat = _q_up(cqn, comm.weight("w_uq"), g_qn, g_qr, cs, sg)
    kv, kcat, vt = _kv_up(ckvn, comm.weight("w_ukv"), krope, g_kn)
    mla_y, a_mla, lse = _flash_fwd(qcat, kcat, vt, p_all)
    memn, kraw, kn, vmem = _mem_kv_fwd(mem, gains["mem_norm_g"], comm.weight("w_mem_kv"), gains["mem_kn_g"])
    a_mem = _mem_attn_fwd(p_all, kn, vmem, gains["mem_qn_g"])
    o_c, o_m, o_e, merged = _merge_fwd(a_conv, comm.weight("w_conv_out"), a_mla, comm.weight("w_mla_out"), a_mem,
                                       comm.weight("w_mem_out"), p_all)
    dy, dyb, loss_parts = _out_fwd(merged, comm.weight("w_o"), x, target)

    comm.put("w_o", _matmul(merged, dyb, "tn", BF16, "dw_o"))
    do_c, do_m, do_e, dp_g = _merge_bwd(dyb, comm.weight("w_o"), p_all, o_c, o_m, o_e)
    comm.put("w_conv_out", _matmul(a_conv, do_c, "tn", BF16, "dw_conv_out"))
    comm.put("w_mla_out", _matmul(a_mla, do_m, "tn", BF16, "dw_mla_out"))
    comm.put("w_mem_out", _matmul(a_mem, do_e, "tn", BF16, "dw_mem_out"))
    da_conv = hosted_matmul("da_conv", do_c, comm.weight("w_conv_out"), "nt", BF16)
    da_mem = _matmul(do_e, comm.weight("w_mem_out"), "nt", BF16, "da_mem")
    dp_c, dconv_w = _conv_bwd(da_conv, p_all, co, comm.weight("conv_w"))
    dmla_y, dp_z, delta = _mla_gate_bwd(do_m, comm.weight("w_mla_out"), mla_y, p_all)
    dqcat, dkcat, dv = _flash_bwd(qcat, kcat, kv, dmla_y, lse, delta)
    (dq_raw, dg_q), extra = _q_prep_bwd(dqcat, q_raw, g_qn, g_qr, cs, sg, comm.side("q_prep_bwd"))
    comm.done("q_prep_bwd", extra)
    (dkv, dkr, dg_kn), extra = _k_prep_bwd(dkcat, dv, kv, g_kn, comm.side("k_prep_bwd"))
    comm.done("k_prep_bwd", extra)
    comm.put("w_uq", _matmul(cqn, dq_raw, "tn", BF16, "dw_uq"))
    comm.put("w_ukv", _matmul(ckvn, dkv, "tn", BF16, "dw_ukv"))
    dcqn = _matmul(dq_raw, comm.weight("w_uq"), "nt", F32, "dcqn")
    dckvn = _matmul(dkv, comm.weight("w_ukv"), "nt", F32, "dckvn")
    dp_s, dg_qn, dg_kvn, dg_kr = _lora_bwd(dcqn, dckvn, dkr, p_all, gains["mla_q_norm_g"],
                                            gains["mla_kv_norm_g"], g_kr, cs, sg)
    dp_m, dkn, dvm, dg_mq = _mem_attn_bwd(da_mem, p_all, kn, vmem, gains["mem_qn_g"])
    dw_mem_kv, dg_mn, dg_mk = _mem_kv_bwd(dkn, dvm, kraw, memn, mem, gains["mem_norm_g"],
                                          comm.weight("w_mem_kv"), gains["mem_kn_g"])
    comm.put("w_mem_kv", dw_mem_kv.astype(BF16))
    dps = [dp_g, dp_z, dp_c, dp_m, dp_s]
    comm.put("w_all_t", [hosted_matmul("dw_in_%d" % k, d, h, "tn", BF16) for k, d in enumerate(dps)])
    s = x.shape[0]
    dh_a, extra = _dh_bwd(dps, comm.weight("w_all_t"), 0, s // 4, "dh_a", comm.side("dh_a"))
    comm.done("dh_a", extra)
    dh_b, extra = _dh_bwd(dps, comm.weight("w_all_t"), s // 4, s - s // 4, "dh_b", comm.side("dh_b"))
    comm.done("dh_b", extra)
    grad_x, dg_n = _dx_bwd(dh_a, dh_b, x, dy, gains["norm_g"])

    gsmall = {
        "norm_g": dg_n.sum(0), "conv_w": dconv_w.sum(1), "mla_q_norm_g": dg_qn.sum(0),
        "mla_kv_norm_g": dg_kvn.sum(0), "mla_qn_nope_g": dg_q[0].sum(0), "mla_qn_rope_g": dg_q[1].sum(0)[:64],
        "mla_kn_nope_g": dg_kn.sum(0), "mla_kn_rope_g": dg_kr.sum(0)[:64], "mem_norm_g": dg_mn.sum(0),
        "mem_qn_g": dg_mq.sum(0), "mem_kn_g": dg_mk.sum(0),
    }
    return loss_parts, grad_x, gsmall


def _me():
    return lax.axis_index("x"), lax.axis_index("y"), lax.axis_index("c")


def _chips(x, y):
    return [(1 - x, y), (x, 1 - y), (1 - x, 1 - y)]


def _allgather_shards(xs, name):
    r, c = xs.shape
    hc = c // 2

    def body(x_ref, out_ref, send_sems, recv_sems):
        x, y, cc = _me()
        me, sibling = (x, y, cc), (x, y, 1 - cc)
        across_x, across_y, diagonal = _chips(x, y)

        def rows(px, py, pc):
            return out_ref.at[2 * px + py, :, pl.ds(pc * hc, hc)]

        def copy(k, block, to, src=None):
            return pltpu.make_async_remote_copy(
                src_ref=rows(*block) if src is None else src, dst_ref=rows(*block),
                send_sem=send_sems.at[k], recv_sem=recv_sems.at[k], device_id=to, device_id_type=MESH)

        my_half = x_ref.at[:, pl.ds(cc * hc, hc)]
        direct = [copy(0, me, (*across_x, cc), src=my_half), copy(1, me, (*across_y, cc), src=my_half)]
        for cp in direct:
            cp.start()
        copy(0, (*across_x, cc), me).wait_recv()
        copy(1, (*across_y, cc), me).wait_recv()
        from_x = cc == 0
        came = (jnp.where(from_x, across_x[0], across_y[0]), jnp.where(from_x, across_x[1], across_y[1]), cc)
        goes = (jnp.where(from_x, across_y[0], across_x[0]), jnp.where(from_x, across_y[1], across_x[1]), cc)
        relay = copy(2, came, goes)
        relay.start()
        passed = [copy(3, (*across_x, cc), sibling), copy(4, (*across_y, cc), sibling)]
        for cp in passed:
            cp.start()
        copy(2, (*diagonal, cc), me).wait_recv()
        passed.append(copy(5, (*diagonal, cc), sibling))
        passed[2].start()
        for j, chip in enumerate((across_x, across_y, diagonal)):
            copy(3 + j, (*chip, 1 - cc), me).wait_recv()
        for cp in direct + [relay] + passed:
            cp.wait_send()

    return pl.pallas_call(
        body, name=name, in_specs=[ANY], out_specs=ANY,
        out_shape=jax.ShapeDtypeStruct((4, r, c), xs.dtype),
        scratch_shapes=[pltpu.SemaphoreType.DMA((6,)), pltpu.SemaphoreType.DMA((6,))],
    )(xs)


def _put_own(g4, own, name):
    r, c = own.shape
    tc = c // 2
    xi, yi, _ = _me()
    sel = jnp.reshape(2 * xi + yi, (1,)).astype(jnp.int32)

    def body(sel_ref, own_ref, g_ref, o_ref):
        o_ref[0] = own_ref[...]

    return pl.pallas_call(
        body, name=name,
        grid_spec=pltpu.PrefetchScalarGridSpec(
            num_scalar_prefetch=1, grid=(c // tc,),
            in_specs=[pl.BlockSpec((r, tc), lambda j, sel_ref: (0, j)), ANY],
            out_specs=pl.BlockSpec((1, r, tc), lambda j, sel_ref: (sel_ref[0], 0, j))),
        out_shape=jax.ShapeDtypeStruct(g4.shape, g4.dtype), input_output_aliases={2: 0},
        compiler_params=_params(("arbitrary",)),
    )(sel, own, g4)


def _with_own(g4, own):
    xi, yi, _ = _me()
    pick = lax.broadcasted_iota(jnp.int32, (4,) + (1,) * own.ndim, 0) == 2 * xi + yi
    return jnp.where(pick, own[None], g4)


def _remote(src, dst, send_sems, recv_sems, k, to):
    return pltpu.make_async_remote_copy(src_ref=src, dst_ref=dst, send_sem=send_sems.at[k], recv_sem=recv_sems.at[k],
                                        device_id=to, device_id_type=MESH)


def _side_gather_ici(shards):
    def build(ins, outs, send_sems, recv_sems):
        x, y, cc = _me()
        pairs = []
        for a, (x_ref, o_ref) in enumerate(zip(ins, outs)):
            hr = x_ref.shape[0] // 2
            my_half = x_ref.at[pl.ds(cc * hr, hr), :]
            for j, (px, py) in enumerate(_chips(x, y)):
                mine = o_ref.at[pl.ds((4 * x + 2 * y + cc) * hr, hr), :]
                theirs = o_ref.at[pl.ds((4 * px + 2 * py + cc) * hr, hr), :]
                pairs.append((_remote(my_half, mine, send_sems, recv_sems, 3 * a + j, (px, py, cc)),
                              _remote(my_half, theirs, send_sems, recv_sems, 3 * a + j, (px, py, cc))))
        return pairs

    shapes = [jax.ShapeDtypeStruct((4 * s.shape[0], s.shape[1]), s.dtype) for s in shards]
    return _Side(shards, shapes, 3 * len(shards), build)


def _side_gather_pass(bufs):
    def build(ins, outs, send_sems, recv_sems):
        x, y, cc = _me()
        pairs = []
        for a, o_ref in enumerate(outs):
            hr = o_ref.shape[0] // 8
            for j, (px, py) in enumerate(_chips(x, y)):
                got = o_ref.at[pl.ds((4 * px + 2 * py + cc) * hr, hr), :]
                coming = o_ref.at[pl.ds((4 * px + 2 * py + 1 - cc) * hr, hr), :]
                pairs.append((_remote(got, got, send_sems, recv_sems, 3 * a + j, (x, y, 1 - cc)),
                              _remote(got, coming, send_sems, recv_sems, 3 * a + j, (x, y, 1 - cc))))
        return pairs

    shapes = [jax.ShapeDtypeStruct(b.shape, b.dtype) for b in bufs]
    return _Side(bufs, shapes, 3 * len(bufs), build, aliases={a: a for a in range(len(bufs))})


def _side_swap(srcs, parts):
    def view(s_ref, part, cc):
        if part == "lead":
            return s_ref.at[1 - cc]
        if part == "cols":
            hc = s_ref.shape[1] // 2
            return s_ref.at[:, pl.ds((1 - cc) * hc, hc)]
        return s_ref

    def build(ins, outs, send_sems, recv_sems):
        x, y, cc = _me()
        pairs = []
        for a, (s_ref, o_ref) in enumerate(zip(ins, outs)):
            cp = _remote(view(s_ref, parts[a], cc), o_ref, send_sems, recv_sems, a, (x, y, 1 - cc))
            pairs.append((cp, cp))
        return pairs

    out_shape = {"lead": lambda s: s.shape[1:], "cols": lambda s: (s.shape[0], s.shape[1] // 2), "all": lambda s: s.shape}
    shapes = [jax.ShapeDtypeStruct(out_shape[p](s), s.dtype) for s, p in zip(srcs, parts)]
    return _Side(srcs, shapes, len(srcs), build)


def _side_chips(ts):
    def build(ins, outs, send_sems, recv_sems):
        x, y, cc = _me()
        pairs = []
        for a, (t_ref, r_ref) in enumerate(zip(ins, outs)):
            for j, (px, py) in enumerate(_chips(x, y)):
                cp = _remote(t_ref.at[2 * px + py], r_ref.at[j], send_sems, recv_sems, 3 * a + j, (px, py, cc))
                pairs.append((cp, cp))
        return pairs

    shapes = [jax.ShapeDtypeStruct((3,) + t.shape[1:], t.dtype) for t in ts]
    return _Side(ts, shapes, 3 * len(ts), build)


def _exchange(side, name):
    ns_in, ns_out = len(side.inputs), len(side.out_shapes)

    def body(*refs):
        pairs = side.build(refs[:ns_in], refs[ns_in:ns_in + ns_out], refs[-2], refs[-1])
        for send, _ in pairs:
            send.start()
        for send, recv in pairs:
            send.wait_send()
            recv.wait_recv()

    outs = pl.pallas_call(
        body, name=name, in_specs=[ANY] * ns_in, out_specs=[ANY] * ns_out, out_shape=side.out_shapes,
        scratch_shapes=[pltpu.SemaphoreType.DMA((side.n_copies,)), pltpu.SemaphoreType.DMA((side.n_copies,))],
    )(*side.inputs)
    return list(outs)


ELEMENTWISE_BLOCK_BYTES = 2 << 20


def _tile(r, c, itemsize):
    tr, tc = r, c
    while tr * tc * itemsize > ELEMENTWISE_BLOCK_BYTES and tr % 32 == 0:
        tr //= 2
    while tr * tc * itemsize > ELEMENTWISE_BLOCK_BYTES and tc % 256 == 0:
        tc //= 2
    return tr, tc


def _add_own_half(gb, got, sel, name):
    _, _, hr, c = gb.shape
    tr, tc = _tile(hr, c, 4)

    def body(sel_ref, a_ref, b_ref, o_ref):
        o_ref[...] = (a_ref[0].astype(F32) + b_ref[...].astype(F32)).astype(BF16)

    return pl.pallas_call(
        body, name=name,
        grid_spec=pltpu.PrefetchScalarGridSpec(
            num_scalar_prefetch=1, grid=(4, hr // tr, c // tc),
            in_specs=[pl.BlockSpec((1, 1, tr, tc), lambda k, i, j, sel_ref: (sel_ref[0], k, i, j)),
                      pl.BlockSpec((1, tr, tc), lambda k, i, j, sel_ref: (k, i, j))],
            out_specs=pl.BlockSpec((1, tr, tc), lambda k, i, j, sel_ref: (k, i, j))),
        out_shape=jax.ShapeDtypeStruct(got.shape, BF16),
        compiler_params=_params(("parallel", "parallel", "parallel")),
    )(sel, gb, got)


def _add_own_cols(g, got, sel, name):
    r, c = g.shape
    hr, hc = r // 4, c // 2
    tr, tc = _tile(hr, hc, 4)
    nj = hc // tc

    def body(sel_ref, a_ref, b_ref, o_ref):
        o_ref[...] = (a_ref[...].astype(F32) + b_ref[...].astype(F32)).astype(BF16)

    t = pl.pallas_call(
        body, name=name,
        grid_spec=pltpu.PrefetchScalarGridSpec(
            num_scalar_prefetch=1, grid=(r // tr, nj),
            in_specs=[pl.BlockSpec((tr, tc), lambda i, j, sel_ref: (i, sel_ref[0] * nj + j)),
                      pl.BlockSpec((tr, tc), lambda i, j, sel_ref: (i, j))],
            out_specs=pl.BlockSpec((tr, tc), lambda i, j, sel_ref: (i, j))),
        out_shape=jax.ShapeDtypeStruct((r, hc), BF16),
        compiler_params=_params(("parallel", "parallel")),
    )(sel, g, got)
    return t.reshape(4, hr, hc)


def _sum_partials(t, rcv, sel, name):
    _, hr, c = t.shape
    tr, tc = _tile(hr, c, 4)

    def body(sel_ref, t_ref, r_ref, o_ref):
        o_ref[...] = ((t_ref[0].astype(F32) + r_ref[0].astype(F32)) + r_ref[1].astype(F32)) + r_ref[2].astype(F32)

    return pl.pallas_call(
        body, name=name,
        grid_spec=pltpu.PrefetchScalarGridSpec(
            num_scalar_prefetch=1, grid=(hr // tr, c // tc),
            in_specs=[pl.BlockSpec((1, tr, tc), lambda i, j, sel_ref: (sel_ref[0], i, j)),
                      pl.BlockSpec((3, tr, tc), lambda i, j, sel_ref: (0, i, j))],
            out_specs=pl.BlockSpec((tr, tc), lambda i, j, sel_ref: (i, j))),
        out_shape=jax.ShapeDtypeStruct((hr, c), F32),
        compiler_params=_params(("parallel", "parallel")),
    )(sel, t, rcv)


class _GroupReduce:
    def __init__(self):
        self.names, self.gb, self.t, self.fh, self.other = [], [], [], [], []

    def put(self, name, gb):
        self.names.append(name)
        self.gb.append(gb)

    def side(self, stage):
        if stage == 1:
            return _side_swap(self.gb, ["lead" if g.ndim == 4 else "cols" for g in self.gb])
        return _side_chips(self.t) if stage == 2 else _side_swap(self.fh, ["all"] * len(self.fh))

    def done(self, stage, outs):
        x, y, cc = _me()
        if stage == 1:
            sel = jnp.reshape(cc, (1,)).astype(jnp.int32)
            self.t = [(_add_own_half if gb.ndim == 4 else _add_own_cols)(gb, got, sel, "rs_add_" + n)
                      for n, gb, got in zip(self.names, self.gb, outs)]
        elif stage == 2:
            sel = jnp.reshape(2 * x + y, (1,)).astype(jnp.int32)
            self.fh = [_sum_partials(t, rcv, sel, "rs_sum_" + n) for n, t, rcv in zip(self.names, self.t, outs)]
        else:
            self.other = list(outs)

    def result(self):
        return {n: (fh, other) for n, fh, other in zip(self.names, self.fh, self.other)}


def _allreduce_small(v, tag):
    r = v.shape[0]

    def body(v_ref, out_ref, buf_ref, send_sems, recv_sems):
        x, y, cc = _me()
        me = 4 * x + 2 * y + cc
        buf_ref[pl.ds(pl.multiple_of(me * r, 8), r), :] = v_ref[...]
        peers = [(x, y, 1 - cc)] + [(px, py, pc) for (px, py) in _chips(x, y) for pc in (cc, 1 - cc)]
        cps = []
        for k, (px, py, pc) in enumerate(peers):
            mine = buf_ref.at[pl.ds(pl.multiple_of(me * r, 8), r), :]
            cps.append(pltpu.make_async_remote_copy(
                src_ref=v_ref, dst_ref=mine, send_sem=send_sems.at[k], recv_sem=recv_sems.at[k],
                device_id=(px, py, pc), device_id_type=MESH))
        for cp in cps:
            cp.start()
        for cp in cps:
            cp.wait()
        acc = buf_ref[0:r, :]
        for d in range(1, 8):
            acc = acc + buf_ref[d * r:(d + 1) * r, :]
        out_ref[...] = acc

    vm = pl.BlockSpec(memory_space=pltpu.VMEM)
    return pl.pallas_call(
        body, name="allreduce_small_" + tag, in_specs=[vm], out_specs=vm,
        out_shape=jax.ShapeDtypeStruct((r, 128), F32),
        scratch_shapes=[pltpu.VMEM((8 * r, 128), F32), pltpu.SemaphoreType.DMA((7,)), pltpu.SemaphoreType.DMA((7,))],
    )(v)


def _adamw(w, g, m, v, name):
    r, c = w.shape
    tr = r
    c1 = 1.0 / (1.0 - ADAM_B1 ** ADAM_STEP)
    c2 = 1.0 / (1.0 - ADAM_B2 ** ADAM_STEP)

    def body(w_ref, g_ref, m_ref, v_ref, d_ref, mo_ref, vo_ref):
        gg = g_ref[...]
        mn = ADAM_B1 * m_ref[...] + (1.0 - ADAM_B1) * gg
        vn = ADAM_B2 * v_ref[...] + (1.0 - ADAM_B2) * (gg * gg)
        mo_ref[...] = mn
        vo_ref[...] = vn
        d_ref[...] = -ADAM_LR * ((mn * c1) / (jnp.sqrt(vn * c2) + ADAM_EPS) + ADAM_WD * w_ref[...])

    blk = pl.BlockSpec((tr, c), lambda i: (i, 0))
    return pl.pallas_call(
        body, name=name, grid=(r // tr,), in_specs=[blk] * 4, out_specs=[blk] * 3,
        out_shape=[jax.ShapeDtypeStruct((r, c), F32)] * 3,
        compiler_params=_params(("parallel",)),
    )(w, g, m, v)


def _adamw_halves(w, g_own, g_other, m, v, sel, name, axis):
    r, c = w.shape
    tr, tc = _tile(g_own.shape[0], g_own.shape[1], 4)
    nh = (g_own.shape[axis]) // (tr, tc)[axis]
    c1 = 1.0 / (1.0 - ADAM_B1 ** ADAM_STEP)
    c2 = 1.0 / (1.0 - ADAM_B2 ** ADAM_STEP)

    def body(sel_ref, w_ref, go_ref, gx_ref, m_ref, v_ref, d_ref, mo_ref, vo_ref, g_ref):
        mine = (pl.program_id(axis) // nh) == sel_ref[0]
        gg = jnp.where(mine, go_ref[...], gx_ref[...])
        g_ref[...] = gg
        mn = ADAM_B1 * m_ref[...] + (1.0 - ADAM_B1) * gg
        vn = ADAM_B2 * v_ref[...] + (1.0 - ADAM_B2) * (gg * gg)
        mo_ref[...] = mn
        vo_ref[...] = vn
        d_ref[...] = -ADAM_LR * ((mn * c1) / (jnp.sqrt(vn * c2) + ADAM_EPS) + ADAM_WD * w_ref[...])

    blk = pl.BlockSpec((tr, tc), lambda i, j, sel_ref: (i, j))
    if axis == 0:
        half = pl.BlockSpec((tr, tc), lambda i, j, sel_ref: (i % nh, j))
    else:
        half = pl.BlockSpec((tr, tc), lambda i, j, sel_ref: (i, j % nh))
    return pl.pallas_call(
        body, name=name,
        grid_spec=pltpu.PrefetchScalarGridSpec(
            num_scalar_prefetch=1, grid=(r // tr, c // tc),
            in_specs=[blk, half, half, blk, blk], out_specs=[blk] * 4),
        out_shape=[jax.ShapeDtypeStruct((r, c), F32)] * 4,
        compiler_params=_params(("parallel", "parallel")),
    )(sel, w, g_own, g_other, m, v)


WEIGHTS = ['norm_g', 'w_in', 'conv_w', 'w_conv_out', 'mla_q_norm_g', 'w_uq', 'mla_kv_norm_g', 'w_ukv',
           'mla_qn_nope_g', 'mla_qn_rope_g', 'mla_kn_nope_g', 'mla_kn_rope_g', 'w_mla_out', 'mem_norm_g',
           'w_mem_kv', 'mem_qn_g', 'mem_kn_g', 'w_mem_out', 'w_o']
COL_SHARDED = ['w_in', 'w_conv_out', 'w_uq', 'w_ukv', 'w_mem_out']
ROW_SHARDED = ['w_mla_out', 'w_mem_kv', 'w_o']
SMALL = ['norm_g', 'conv_w', 'mla_q_norm_g', 'mla_kv_norm_g', 'mla_qn_nope_g', 'mla_qn_rope_g', 'mla_kn_nope_g',
         'mla_kn_rope_g', 'mem_norm_g', 'mem_qn_g', 'mem_kn_g']
SMALL_SIZES = [2048, 3072, 512, 512, 128, 64, 128, 64, 2048, 256, 256]
PACK_ROWS = 72


def _full_from_shards(name, g4):
    if name in COL_SHARDED:
        return g4.transpose(1, 0, 2).reshape(g4.shape[1], 4 * g4.shape[2])
    return g4.reshape(4 * g4.shape[1], g4.shape[2])


def _w_all_t_from_w_in_t(w_in_t):
    conv, small, mz, memq, gates = (w_in_t[0:4096], w_in_t[4096:5184], w_in_t[5184:7232],
                                    w_in_t[7232:9280], w_in_t[9280:15424])
    return jnp.concatenate([gates, mz, conv, memq, small, jnp.zeros((2048 - 1088, D_MODEL), BF16)], axis=0)


def _pad_w_uq(w_uq):
    return jnp.pad(w_uq.reshape(512, MLA_HEADS, 192), ((0, 0), (0, 0), (0, 64))).reshape(512, 4096)


def _grad_halves_by_owner(name, g):
    if name == "w_all_t":
        dw_g, dw_z, dw_c, dw_m, dw_s = g
        return "w_in", jnp.concatenate([dw_c, dw_s[:1088], dw_z, dw_m, dw_g], axis=0)
    if name == "w_uq":
        g = g.reshape(512, MLA_HEADS, HEAD_PAD)[:, :, :192].reshape(512, 3072)
    r, c = g.shape
    if name in COL_SHARDED:
        return name, g.reshape(2, r // 2, 4, c // 4).transpose(0, 2, 1, 3)
    return name, g.reshape(4, 2, r // 8, c).transpose(1, 0, 2, 3)


LATE_WEIGHTS = ['w_conv_out', 'w_uq', 'w_ukv', 'w_mem_out', 'w_mla_out', 'w_mem_kv', 'w_o']
HOSTS = {"proj_fwd": ("gather", 1), "lora_fwd": ("gather", 2),
         "da_conv": ("g1", 1), "q_prep_bwd": ("g1", 2), "k_prep_bwd": ("g1", 3),
         "dw_in_0": ("g2", 1), "dw_in_2": ("g2", 2), "dw_in_4": ("g2", 3),
         "dh_a": ("g3", 1), "dh_b": ("g3", 2)}
GROUP_OF = {"w_o": "g1", "w_conv_out": "g1", "w_mla_out": "g1", "w_mem_out": "g1",
            "w_uq": "g2", "w_ukv": "g2", "w_mem_kv": "g2", "w_in": "g3"}


class _Comm:
    def __init__(self, shards, w_in_t_full, conv_full):
        self.shards = shards
        self.w = {"w_all_t": _w_all_t_from_w_in_t(w_in_t_full), "conv_w": conv_full}
        self.bufs = None
        self.groups = {"g1": _GroupReduce(), "g2": _GroupReduce(), "g3": _GroupReduce()}

    def weight(self, name):
        return self.w[name]

    def side(self, tag):
        if tag not in HOSTS:
            return None
        kind, stage = HOSTS[tag]
        if kind == "gather":
            return _side_gather_ici([self.shards[n] for n in LATE_WEIGHTS]) if stage == 1 else \
                _side_gather_pass(self.bufs)
        return self.groups[kind].side(stage)

    def done(self, tag, outs):
        kind, stage = HOSTS[tag]
        if kind != "gather":
            self.groups[kind].done(stage, outs)
        elif stage == 1:
            self.bufs = list(outs)
        else:
            for n, buf in zip(LATE_WEIGHTS, outs):
                own = self.shards[n]
                self.w[n] = _full_from_shards(n, _with_own(buf.reshape((4,) + own.shape), own))
            self.w["w_uq"] = _pad_w_uq(self.w["w_uq"])

    def put(self, name, g):
        name, gb = _grad_halves_by_owner(name, g)
        self.groups[GROUP_OF[name]].put(name, gb)

    def reduced(self):
        last = self.groups["g3"]
        last.done(3, _exchange(last.side(3), "rs_swap_w_in"))
        out = {}
        for grp in self.groups.values():
            out.update(grp.result())
        return out


def kernel(x, positions, mem, norm_g, w_in, conv_w, w_conv_out, mla_q_norm_g, w_uq, mla_kv_norm_g, w_ukv, mla_qn_nope_g, mla_qn_rope_g, mla_kn_nope_g, mla_kn_rope_g, w_mla_out, mem_norm_g, w_mem_kv, mem_qn_g, mem_kn_g, w_mem_out, w_o, loss_target, m_norm_g, m_w_in, m_conv_w, m_w_conv_out, m_mla_q_norm_g, m_w_uq, m_mla_kv_norm_g, m_w_ukv, m_mla_qn_nope_g, m_mla_qn_rope_g, m_mla_kn_nope_g, m_mla_kn_rope_g, m_w_mla_out, m_mem_norm_g, m_w_mem_kv, m_mem_qn_g, m_mem_kn_g, m_w_mem_out, m_w_o, v_norm_g, v_w_in, v_conv_w, v_w_conv_out, v_mla_q_norm_g, v_w_uq, v_mla_kv_norm_g, v_w_ukv, v_mla_qn_nope_g, v_mla_qn_rope_g, v_mla_kn_nope_g, v_mla_kn_rope_g, v_w_mla_out, v_mem_norm_g, v_w_mem_kv, v_mem_qn_g, v_mem_kn_g, v_w_mem_out, v_w_o):
    args = locals()
    w = {n: args[n][0] for n in WEIGHTS}
    m1 = {n: args["m_" + n][0] for n in WEIGHTS}
    v2 = {n: args["v_" + n][0] for n in WEIGHTS}
    xi, yi, ci = _me()
    chip = 2 * xi + yi

    conv_slot = jnp.zeros((3, 4, 256), F32).at[:, chip, :].set(w["conv_w"] * jnp.where(ci == 0, 1.0, 0.0))
    pre = jnp.zeros((PACK_ROWS * 128,), F32).at[0:3072].set(conv_slot.reshape(3072))
    conv_full = _allreduce_small(pre.reshape(PACK_ROWS, 128), "conv_w").reshape(-1)[0:3072].reshape(3, 1024)

    w_in_t = w["w_in"].T.astype(BF16)
    w_in_t_full = _put_own(_allgather_shards(w_in_t, "ag_w_in"), w_in_t, "own_w_in").reshape(4 * w_in_t.shape[0],
                                                                                               D_MODEL)
    comm = _Comm({n: w[n].astype(BF16) for n in LATE_WEIGHTS}, w_in_t_full, conv_full)
    gains = {n: w[n].reshape(1, -1) for n in SMALL if n != "conv_w"}

    loss_parts, grad_x, gsmall = _local_step(x[0], positions[0], mem[0], loss_target[0], comm, gains)

    loss_local = 0.5 * jnp.sum(loss_parts) * (1.0 / D_MODEL)
    flat = jnp.concatenate([gsmall[n].reshape(-1) for n in SMALL] + [loss_local.reshape(1)])
    flat = jnp.pad(flat, (0, PACK_ROWS * 128 - flat.shape[0]))
    tot = _allreduce_small(flat.reshape(PACK_ROWS, 128), "grads").reshape(-1)
    grads = {}
    off = 0
    for n, sz in zip(SMALL, SMALL_SIZES):
        grads[n] = tot[off:off + sz]
        off += sz
    loss = tot[off]
    grads["conv_w"] = lax.dynamic_slice(grads["conv_w"].reshape(3, 1024), (0, chip * 256), (3, 256))
    for n in SMALL:
        grads[n] = grads[n].reshape(w[n].shape)

    deltas, new_m, new_v = {}, {}, {}
    for n in SMALL:
        shp = w[n].shape
        two_d = (lambda a: a.reshape(1, -1)) if len(shp) == 1 else (lambda a: a)
        d, mn, vn = _adamw(two_d(w[n]), two_d(grads[n]), two_d(m1[n]), two_d(v2[n]), "adamw_" + n)
        deltas[n], new_m[n], new_v[n] = d.reshape(shp), mn.reshape(shp), vn.reshape(shp)

    sel_c = jnp.reshape(ci, (1,)).astype(jnp.int32)
    reduced = comm.reduced()
    for n in COL_SHARDED + ROW_SHARDED:
        g_own, g_other = reduced[n]
        if n == "w_in":
            outs = _adamw_halves(w[n].T, g_own, g_other, m1[n].T, v2[n].T, sel_c, "adamw_" + n, 1)
            deltas[n], new_m[n], new_v[n], grads[n] = [o.T for o in outs]
        else:
            deltas[n], new_m[n], new_v[n], grads[n] = _adamw_halves(w[n], g_own, g_other, m1[n], v2[n], sel_c,
                                                                     "adamw_" + n, 0)

    lead = lambda a: a[None]
    return (loss, grad_x[None], *[lead(grads[n]) for n in WEIGHTS], *[lead(deltas[n]) for n in WEIGHTS],
            *[lead(new_m[n]) for n in WEIGHTS], *[lead(new_v[n]) for n in WEIGHTS])
```
